```python
import math
import jax
import jax.numpy as jnp
from jax import lax
import numpy as np

D_MODEL = 1024
BATCH = 32
SEQ = 256
DEPTH = 1
DEC_BATCH = 4
DEC_SEQ = 4096
PAST_LEN = 256

GRID_W = 64
POS_BASE = 10000.0
EPS = 1e-6
N_MOD = 6
N_DIR = 2
D_MIX = D_MODEL
D_ML = D_MIX // 2
N_ML_HEADS = 4
ML_HEAD = D_ML // N_ML_HEADS
ML_CHUNK = 64
N_GATE = N_DIR * 2 * N_ML_HEADS
D_RG = D_MIX - D_ML
N_RG_BLOCKS = 8
RG_BLOCK = D_RG // N_RG_BLOCKS
RG_C = 8.0
CONV_W = 4
CONV_PAD_LEFT = 2
D_IN = 4 * D_ML + N_GATE + 2 * D_RG
N_EXPERTS = 64
TOP_K = 8
N_GROUPS = 8
TOPK_GROUPS = 4
D_EXPERT = 256
D_SHARED = 256
ROUTED_SCALE = 2.5
MOE_BLOCK = 128

kernel_name = 'hybrid_mlstm_rglru_moe_diffusion_step'


def rmsnorm(x, g):
    xf = x.astype(jnp.float32)
    y = xf * lax.rsqrt(jnp.mean(xf * xf, axis=-1, keepdims=True) + EPS)
    return (y * g.astype(jnp.float32)).astype(x.dtype)


def grid_pos_embed(n_tokens, dim):
    rows_n = n_tokens // GRID_W
    row = jnp.repeat(jnp.arange(rows_n), GRID_W).astype(jnp.float32)
    col = jnp.tile(jnp.arange(GRID_W), rows_n).astype(jnp.float32)
    quarter = dim // 4
    omega = 1.0 / (POS_BASE ** (jnp.arange(quarter, dtype=jnp.float32) / quarter))
    ra = row[:, None] * omega
    ca = col[:, None] * omega
    return jnp.concatenate([jnp.sin(ra), jnp.cos(ra), jnp.sin(ca), jnp.cos(ca)], axis=-1)


def dwconv_centred(x, w, b):
    T = x.shape[1]
    xp = jnp.pad(x, ((0, 0), (CONV_PAD_LEFT, CONV_W - 1 - CONV_PAD_LEFT), (0, 0)))
    y = b
    for j in range(CONV_W):
        y = y + xp[:, j:j + T] * w[j]
    return y


def mlstm_chunked(q, k, v, ig, lf, C0, n0, m0):
    B, H, T, dh = q.shape
    nc = T // ML_CHUNK

    def to_chunks(a):
        return jnp.moveaxis(a.reshape((B, H, nc, ML_CHUNK) + a.shape[3:]), 2, 0)

    causal = jnp.tril(jnp.ones((ML_CHUNK, ML_CHUNK), dtype=bool))

    def step(carry, inp):
        C, n, m = carry
        qb, kb, vb, ib, fb = inp
        b = jnp.cumsum(fb, axis=-1)
        log_d = jnp.where(causal, b[..., :, None] - b[..., None, :] + ib[..., None, :], -jnp.inf)
        inter = b + m[..., None]
        m_t = jnp.maximum(inter, jnp.max(log_d, axis=-1))
        s = jnp.einsum('bhtd,bhsd->bhts', qb, kb) * jnp.exp(log_d - m_t[..., None])
        w_inter = jnp.exp(inter - m_t)
        num = w_inter[..., None] * jnp.einsum('bhtd,bhde->bhte', qb, C) + jnp.einsum('bhts,bhse->bhte', s, vb)
        den = w_inter * jnp.einsum('bhtd,bhd->bht', qb, n) + jnp.sum(s, axis=-1)
        h = num / jnp.maximum(jnp.abs(den), jnp.exp(-m_t))[..., None]
        b_last = b[..., -1]
        log_w = b_last[..., None] - b + ib
        m_new = jnp.maximum(b_last + m, jnp.max(log_w, axis=-1))
        w_s = jnp.exp(log_w - m_new[..., None])
        decay = jnp.exp(b_last + m - m_new)
        C_new = decay[..., None, None] * C + jnp.einsum('bhs,bhsd,bhse->bhde', w_s, kb, vb)
        n_new = decay[..., None] * n + jnp.einsum('bhs,bhsd->bhd', w_s, kb)
        return (C_new, n_new, m_new), h

    (C, n, m), hc = lax.scan(step, (C0, n0, m0), tuple(to_chunks(a) for a in (q, k, v, ig, lf)))
    h = jnp.moveaxis(hc, 0, 2).reshape(B, H, T, dh)
    return h, C, n, m


def mlstm_bidir(q, k, v, ig, lf, C0, n0, m0):
    h_f, Cf, nf, mf = mlstm_chunked(q, k, v, ig[:, 0], lf[:, 0], C0[:, 0], n0[:, 0], m0[:, 0])
    fl = lambda a: jnp.flip(a, axis=2)
    h_b, Cb, nb, mb = mlstm_chunked(fl(q), fl(k), fl(v), fl(ig[:, 1]), fl(lf[:, 1]), C0[:, 1], n0[:, 1], m0[:, 1])
    h = h_f + fl(h_b)
    return h, jnp.stack([Cf, Cb], axis=1), jnp.stack([nf, nb], axis=1), jnp.stack([mf, mb], axis=1)


def _lin_rec(e1, e2):
    a1, b1 = e1
    a2, b2 = e2
    return a1 * a2, a2 * b1 + b2


def rglru_dir(x, wa, ba, wx, bx, lam, h0):
    B, T, _ = x.shape
    xb = x.reshape(B, T, N_RG_BLOCKS, RG_BLOCK)
    r = jax.nn.sigmoid(jnp.einsum('btni,nij->btnj', xb, wa.astype(jnp.float32)).reshape(B, T, D_RG) + ba.astype(jnp.float32))
    i = jax.nn.sigmoid(jnp.einsum('btni,nij->btnj', xb, wx.astype(jnp.float32)).reshape(B, T, D_RG) + bx.astype(jnp.float32))
    log_a = -RG_C * r * jax.nn.softplus(-lam.astype(jnp.float32))
    a = jnp.exp(log_a)
    u = jnp.sqrt(-jnp.expm1(2.0 * log_a)) * (i * x)
    u = u.at[:, 0].add(a[:, 0] * h0)
    _, h = lax.associative_scan(_lin_rec, (a, u), axis=1)
    return h, h[:, -1]


def moe(x2d, router_w, router_bias, exp_w1, exp_w3, exp_w2, shared_w1, shared_w3, shared_w2):
    N, D = x2d.shape
    s = jax.nn.sigmoid(x2d.astype(jnp.float32) @ router_w.astype(jnp.float32))
    sb = s + router_bias.astype(jnp.float32)
    grp = sb.reshape(N, N_GROUPS, N_EXPERTS // N_GROUPS)
    gscore = jnp.sum(lax.top_k(grp, 2)[0], axis=-1)
    _, gidx = lax.top_k(gscore, TOPK_GROUPS)
    gmask = jnp.sum(jax.nn.one_hot(gidx, N_GROUPS, dtype=jnp.float32), axis=1)
    emask = jnp.repeat(gmask, N_EXPERTS // N_GROUPS, axis=1) > 0
    _, eidx = lax.top_k(jnp.where(emask, sb, -jnp.inf), TOP_K)
    wts = jnp.take_along_axis(s, eidx, axis=1)
    wts = wts / jnp.sum(wts, axis=-1, keepdims=True) * ROUTED_SCALE
    S = N * TOP_K
    flat_e = eidx.reshape(S)
    flat_tok = jnp.repeat(jnp.arange(N, dtype=jnp.int32), TOP_K)
    flat_w = wts.reshape(S)
    order = jnp.argsort(flat_e)
    e_sorted = flat_e[order]
    counts = jnp.bincount(flat_e, length=N_EXPERTS)
    padded = (counts + MOE_BLOCK - 1) // MOE_BLOCK * MOE_BLOCK
    start = jnp.cumsum(counts) - counts
    pstart = jnp.cumsum(padded) - padded
    dest = pstart[e_sorted] + jnp.arange(S) - start[e_sorted]
    NB = -(-S // MOE_BLOCK) + N_EXPERTS
    P = NB * MOE_BLOCK
    slot_tok = jnp.zeros((P,), jnp.int32).at[dest].set(flat_tok[order])
    slot_w = jnp.zeros((P,), jnp.float32).at[dest].set(flat_w[order])
    block_end = jnp.cumsum(padded) // MOE_BLOCK
    block_expert = jnp.minimum(jnp.searchsorted(block_end, jnp.arange(NB), side='right'), N_EXPERTS - 1)

    def expert_block(args):
        tok, e = args
        xb = x2d[tok]
        hb = jax.nn.silu(xb @ exp_w1[e]) * (xb @ exp_w3[e])
        return hb @ exp_w2[e]

    yb = lax.map(expert_block, (slot_tok.reshape(NB, MOE_BLOCK), block_expert))
    routed = jax.ops.segment_sum(yb.reshape(P, D) * slot_w[:, None].astype(x2d.dtype), slot_tok, num_segments=N)
    shared = (jax.nn.silu(x2d @ shared_w1) * (x2d @ shared_w3)) @ shared_w2
    return routed + shared


def trunk_layer(x, mod, C0, n0, m0, h0, lp):
    B, T, _ = x.shape
    mod = mod.astype(x.dtype)
    sh1, sc1, g1, sh2, sc2, g2 = [mod[:, j][:, None, :] for j in range(N_MOD)]
    hn = rmsnorm(x, lp['norm1']) * (1 + sc1) + sh1
    z = (hn @ lp['w_in']).astype(jnp.float32)
    cuts = [D_ML, 2 * D_ML, 3 * D_ML, 4 * D_ML, 4 * D_ML + N_GATE, 4 * D_ML + N_GATE + D_RG]
    q, k, v, o, gt, xr, gr = jnp.split(z, cuts, axis=-1)

    def heads(a):
        return a.reshape(B, T, N_ML_HEADS, ML_HEAD).transpose(0, 2, 1, 3)
    q, k, v = heads(q), heads(k) * (ML_HEAD ** -0.5), heads(v)
    gt = gt.reshape(B, T, N_DIR, 2, N_ML_HEADS) + lp['mlstm_gate_bias'].astype(jnp.float32)
    gt = gt.transpose(0, 2, 3, 4, 1)
    ig = gt[:, :, 0]
    lf = jax.nn.log_sigmoid(gt[:, :, 1])
    h_ml, C, n, m = mlstm_bidir(q, k, v, ig, lf, C0.astype(jnp.float32), n0.astype(jnp.float32), m0.astype(jnp.float32))
    h_ml = h_ml * lax.rsqrt(jnp.mean(h_ml * h_ml, axis=-1, keepdims=True) + EPS)
    y_ml = h_ml.transpose(0, 2, 1, 3).reshape(B, T, D_ML) * lp['mlstm_norm'].astype(jnp.float32) * jax.nn.sigmoid(o)

    xr = dwconv_centred(xr, lp['rg_conv_w'].astype(jnp.float32), lp['rg_conv_b'].astype(jnp.float32))
    h0f = h0.astype(jnp.float32)
    hr_f, hfin_f = rglru_dir(xr, lp['rg_wa'][0], lp['rg_ba'][0], lp['rg_wx'][0], lp['rg_bx'][0], lp['rg_lambda'][0], h0f[:, 0])
    hr_b, hfin_b = rglru_dir(jnp.flip(xr, 1), lp['rg_wa'][1], lp['rg_ba'][1], lp['rg_wx'][1], lp['rg_bx'][1], lp['rg_lambda'][1], h0f[:, 1])
    y_rg = rmsnorm(hr_f + jnp.flip(hr_b, 1), lp['rg_norm']) * jax.nn.gelu(gr)

    mix = jnp.concatenate([y_ml, y_rg], axis=-1).astype(x.dtype) @ lp['w_out']
    x = x + g1 * mix

    hn2 = rmsnorm(x, lp['norm2']) * (1 + sc2) + sh2
    y = moe(hn2.reshape(B * T, D_MODEL), lp['router_w'], lp['router_bias'], lp['exp_w1'], lp['exp_w3'],
            lp['exp_w2'], lp['shared_w1'], lp['shared_w3'], lp['shared_w2']).reshape(B, T, D_MODEL)
    x = x + g2 * y
    h_fin = jnp.stack([hfin_f, hfin_b], axis=1)
    return x, C, n, m, h_fin


def setup_inputs(seed: int = 0) -> dict:
    key = jax.random.key(seed)
    ks = jax.random.split(key, 34)

    def nrm(k, shape, scale):
        return jax.random.normal(k, shape, jnp.float32) * scale

    gate_base = jnp.stack([jnp.zeros((N_ML_HEADS,), jnp.float32),
                           jnp.linspace(3.0, 6.0, N_ML_HEADS, dtype=jnp.float32)])
    u = jax.random.uniform(ks[20], (DEPTH, N_DIR, D_RG), jnp.float32, minval=0.9, maxval=0.999)
    return {
        'x_prompt': nrm(ks[0], (BATCH, SEQ, D_MODEL), 1.0),
        'x_sample': nrm(ks[1], (DEC_BATCH, DEC_SEQ, D_MODEL), 1.0),
        'c': nrm(ks[2], (DEC_BATCH, D_MODEL), 1.0),
        'state_mlstm_C': nrm(ks[3], (DEC_BATCH, DEPTH, N_DIR, N_ML_HEADS, ML_HEAD, ML_HEAD), 0.05),
        'state_mlstm_n': nrm(ks[4], (DEC_BATCH, DEPTH, N_DIR, N_ML_HEADS, ML_HEAD), 0.05),
        'state_mlstm_m': nrm(ks[5], (DEC_BATCH, DEPTH, N_DIR, N_ML_HEADS), 1.0),
        'state_rglru_h': nrm(ks[6], (DEC_BATCH, DEPTH, N_DIR, D_RG), 0.5),
        'c_ctx': nrm(ks[7], (D_MODEL,), 1.0),
        'w_ada': nrm(ks[8], (DEPTH, D_MODEL, N_MOD * D_MODEL), 0.5 * D_MODEL ** -0.5),
        'b_ada': nrm(ks[9], (DEPTH, N_MOD * D_MODEL), 0.02),
        'norm1': 1.0 + nrm(ks[10], (DEPTH, D_MODEL), 0.02),
        'w_in': nrm(ks[11], (DEPTH, D_MODEL, D_IN), D_MODEL ** -0.5),
        'mlstm_gate_bias': nrm(ks[12], (DEPTH, N_DIR, 2, N_ML_HEADS), 0.1) + gate_base[None, None],
        'mlstm_norm': 1.0 + nrm(ks[13], (DEPTH, D_ML), 0.02),
        'rg_conv_w': nrm(ks[14], (DEPTH, CONV_W, D_RG), CONV_W ** -0.5),
        'rg_conv_b': nrm(ks[15], (DEPTH, D_RG), 0.02),
        'rg_wa': nrm(ks[16], (DEPTH, N_DIR, N_RG_BLOCKS, RG_BLOCK, RG_BLOCK), RG_BLOCK ** -0.5),
        'rg_ba': nrm(ks[17], (DEPTH, N_DIR, D_RG), 0.02),
        'rg_wx': nrm(ks[18], (DEPTH, N_DIR, N_RG_BLOCKS, RG_BLOCK, RG_BLOCK), RG_BLOCK ** -0.5),
        'rg_bx': nrm(ks[19], (DEPTH, N_DIR, D_RG), 0.02),
        'rg_lambda': jnp.log(u) - jnp.log1p(-u),
        'rg_norm': 1.0 + nrm(ks[21], (DEPTH, D_RG), 0.02),
        'w_out': nrm(ks[22], (DEPTH, D_MIX, D_MODEL), D_MIX ** -0.5),
        'norm2': 1.0 + nrm(ks[23], (DEPTH, D_MODEL), 0.02),
        'router_w': nrm(ks[24], (DEPTH, D_MODEL, N_EXPERTS), D_MODEL ** -0.5),
        'router_bias': nrm(ks[25], (DEPTH, N_EXPERTS), 0.01),
        'exp_w1': nrm(ks[26], (DEPTH, N_EXPERTS, D_MODEL, D_EXPERT), D_MODEL ** -0.5),
        'exp_w3': nrm(ks[27], (DEPTH, N_EXPERTS, D_MODEL, D_EXPERT), D_MODEL ** -0.5),
        'exp_w2': nrm(ks[28], (DEPTH, N_EXPERTS, D_EXPERT, D_MODEL), D_EXPERT ** -0.5),
        'shared_w1': nrm(ks[29], (DEPTH, D_MODEL, D_SHARED), D_MODEL ** -0.5),
        'shared_w3': nrm(ks[30], (DEPTH, D_MODEL, D_SHARED), D_MODEL ** -0.5),
        'shared_w2': nrm(ks[31], (DEPTH, D_SHARED, D_MODEL), D_SHARED ** -0.5),
        'norm_final': 1.0 + nrm(ks[32], (D_MODEL,), 0.02),
    }


def reference(x_prompt, x_sample, c, state_mlstm_C, state_mlstm_n, state_mlstm_m, state_rglru_h, c_ctx,
              w_ada, b_ada, norm1, w_in, mlstm_gate_bias, mlstm_norm, rg_conv_w, rg_conv_b, rg_wa, rg_ba,
              rg_wx, rg_bx, rg_lambda, rg_norm, w_out, norm2, router_w, router_bias, exp_w1, exp_w3, exp_w2,
              shared_w1, shared_w3, shared_w2, norm_final):
    bp = x_prompt.shape[0]
    bs = x_sample.shape[0]
    xp = x_prompt
    xs = x_sample + grid_pos_embed(x_sample.shape[1], D_MODEL).astype(x_sample.dtype)[None]
    zC = jnp.zeros((bp, N_DIR, N_ML_HEADS, ML_HEAD, ML_HEAD), jnp.float32)
    zn = jnp.zeros((bp, N_DIR, N_ML_HEADS, ML_HEAD), jnp.float32)
    zm = jnp.zeros((bp, N_DIR, N_ML_HEADS), jnp.float32)
    zh = jnp.zeros((bp, N_DIR, D_RG), jnp.float32)
    new_C, new_n, new_m, new_h = [], [], [], []
    for l in range(DEPTH):
        lp = {
            'norm1': norm1[l], 'w_in': w_in[l], 'mlstm_gate_bias': mlstm_gate_bias[l], 'mlstm_norm': mlstm_norm[l],
            'rg_conv_w': rg_conv_w[l], 'rg_conv_b': rg_conv_b[l], 'rg_wa': rg_wa[l], 'rg_ba': rg_ba[l],
            'rg_wx': rg_wx[l], 'rg_bx': rg_bx[l], 'rg_lambda': rg_lambda[l], 'rg_norm': rg_norm[l],
            'w_out': w_out[l], 'norm2': norm2[l], 'router_w': router_w[l], 'router_bias': router_bias[l],
            'exp_w1': exp_w1[l], 'exp_w3': exp_w3[l], 'exp_w2': exp_w2[l],
            'shared_w1': shared_w1[l], 'shared_w3': shared_w3[l], 'shared_w2': shared_w2[l],
        }
        mod_ctx = (jax.nn.silu(c_ctx) @ w_ada[l] + b_ada[l]).reshape(1, N_MOD, D_MODEL)
        mod_lat = (jax.nn.silu(c) @ w_ada[l] + b_ada[l]).reshape(bs, N_MOD, D_MODEL)
        xp, Cc, nc, mc, hc = trunk_layer(xp, mod_ctx, zC, zn, zm, zh, lp)
        new_C.append(Cc)
        new_n.append(nc)
        new_m.append(mc)
        new_h.append(hc)
        xs, _, _, _, _ = trunk_layer(xs, mod_lat, state_mlstm_C[:, l], state_mlstm_n[:, l],
                                     state_mlstm_m[:, l], state_rglru_h[:, l], lp)
    y_prompt = rmsnorm(xp, norm_final)
    y_sample = rmsnorm(xs, norm_final)
    new_mlstm_C = jnp.stack(new_C, axis=1)
    new_mlstm_n = jnp.stack(new_n, axis=1)
    new_mlstm_m = jnp.stack(new_m, axis=1)
    new_rglru_h = jnp.stack(new_h, axis=1)
    return (y_prompt, y_sample, new_mlstm_C, new_mlstm_n, new_mlstm_m, new_rglru_h)
```

```python
import functools

import jax
import jax.numpy as jnp
from jax import lax
from jax.experimental import pallas as pl
from jax.experimental.pallas import tpu as pltpu

F32 = jnp.float32
BF16 = jnp.bfloat16
HIGHEST = lax.Precision.HIGHEST

D_MODEL = 1024
N_MOD = 6
D_ML = 512
N_HEADS = 4
HEAD = 128
D_RG = 512
N_RG_BLOCKS = 8
RG_BLOCK = 64
RG_C = 8.0
N_GATE = 16
N_EXPERTS = 64
N_GROUPS = 8
GROUP = 8
TOPK_GROUPS = 4
TOP_K = 8
D_EXPERT = 256
ROUTED_SCALE = 2.5
EPS = 1e-6
GRID_W = 64
POS_BASE = 10000.0

SUBLANES = 8
VMEM_LIMIT = 48 * 1024 * 1024


def _params(*sem):
    return pltpu.CompilerParams(dimension_semantics=sem, vmem_limit_bytes=VMEM_LIMIT)


def _silu(x):
    return x * jax.nn.sigmoid(x)


def _softplus(x):
    return jnp.maximum(x, 0.0) + jnp.log1p(jnp.exp(-jnp.abs(x)))


def _rms(x, g):
    return x * lax.rsqrt(jnp.mean(x * x, axis=-1, keepdims=True) + EPS) * g


def _ada_kernel(c_ref, w_ref, b_ref, o_ref):
    s = _silu(c_ref[...])
    o_ref[...] = jnp.dot(s, w_ref[...], precision=HIGHEST, preferred_element_type=F32) + b_ref[...]


def _ada(cvecs, w_ada, b_ada):
    n_out = w_ada.shape[1]
    tn = 1536
    return pl.pallas_call(
        _ada_kernel,
        grid=(n_out // tn,),
        in_specs=[
            pl.BlockSpec((SUBLANES, D_MODEL), lambda j: (0, 0)),
            pl.BlockSpec((D_MODEL, tn), lambda j: (0, j)),
            pl.BlockSpec((1, tn), lambda j: (0, j)),
        ],
        out_specs=pl.BlockSpec((SUBLANES, tn), lambda j: (0, j)),
        out_shape=jax.ShapeDtypeStruct((SUBLANES, n_out), F32),
        name="ada",
        compiler_params=_params("arbitrary"),
    )(cvecs, w_ada, b_ada.reshape(1, n_out))


def _in_proj_kernel(x_ref, pos_ref, mod_ref, n1_ref, wq_ref, wr_ref, wg_ref, wgt_ref, gb_ref, gbt_ref,
                    q_ref, k_ref, v_ref, o_ref, xr_ref, gr_ref, g_ref, gt_ref):
    x = x_ref[...] + pos_ref[...]
    hn = _rms(x, n1_ref[...]) * (1.0 + mod_ref[0, 1:2, :]) + mod_ref[0, 0:1, :]
    hb = hn.astype(BF16)
    z = jnp.dot(hb, wq_ref[...], preferred_element_type=F32)
    q_ref[...] = z[:, 0:D_ML]
    k_ref[...] = z[:, D_ML:2 * D_ML] * (HEAD ** -0.5)
    v_ref[...] = z[:, 2 * D_ML:3 * D_ML]
    o_ref[...] = z[:, 3 * D_ML:4 * D_ML]
    zr = jnp.dot(hb, wr_ref[...], preferred_element_type=F32)
    xr_ref[...] = zr[:, 0:D_RG]
    gr_ref[...] = zr[:, D_RG:2 * D_RG]
    g = jnp.dot(hn, wg_ref[...], precision=HIGHEST, preferred_element_type=F32) + gb_ref[...]
    col = lax.broadcasted_iota(jnp.int32, g.shape, 1)
    g_ref[...] = jnp.where((col & 4) != 0, -_softplus(-g), g)
    gt = lax.dot_general(wgt_ref[...], hn, (((1,), (1,)), ((), ())), precision=HIGHEST,
                         preferred_element_type=F32) + gbt_ref[...]
    rowi = lax.broadcasted_iota(jnp.int32, gt.shape, 0)
    gt_ref[...] = jnp.where((rowi & 4) != 0, -_softplus(-gt), gt)


def _mod_index(bm, tm, t):
    if bm > 1:
        return lambda i: ((i * tm) // t, 0, 0)
    return lambda i: (0, 0, 0)


def _in_proj(x2d, pos, mod, t, tm, norm1, wq, wr, wg, wgt, gbias):
    n = x2d.shape[0]
    npos = pos.shape[0] // tm
    tok = lambda i: (i, 0)
    const = lambda i: (0, 0)
    f = lambda w: jax.ShapeDtypeStruct((n, w), F32)
    return pl.pallas_call(
        _in_proj_kernel,
        grid=(n // tm,),
        in_specs=[
            pl.BlockSpec((tm, D_MODEL), tok),
            pl.BlockSpec((tm, D_MODEL), lambda i: (i % npos, 0)),
            pl.BlockSpec((1, N_MOD, D_MODEL), _mod_index(mod.shape[0], tm, t)),
            pl.BlockSpec((1, D_MODEL), const),
            pl.BlockSpec((D_MODEL, 4 * D_ML), const),
            pl.BlockSpec((D_MODEL, 2 * D_RG), const),
            pl.BlockSpec((D_MODEL, N_GATE), const),
            pl.BlockSpec((N_GATE, D_MODEL), const),
            pl.BlockSpec((1, N_GATE), const),
            pl.BlockSpec((N_GATE, 1), const),
        ],
        out_specs=[pl.BlockSpec((tm, D_ML), tok)] * 4 + [pl.BlockSpec((tm, D_RG), tok)] * 2
        + [pl.BlockSpec((tm, N_GATE), tok), pl.BlockSpec((N_GATE, tm), lambda i: (0, i))],
        out_shape=[f(D_ML)] * 4 + [f(D_RG)] * 2 + [f(N_GATE), jax.ShapeDtypeStruct((N_GATE, n), F32)],
        name="in_proj",
        compiler_params=_params("arbitrary"),
    )(x2d, pos, mod, norm1, wq, wr, wg, wgt, gbias.reshape(1, N_GATE), gbias.reshape(N_GATE, 1))


def _mlstm_kernel(chunk, nc, emit_state, *refs):
    (qf, kf, vf, gf, gtf, qb, kb, vb, gb, gtb, c0_ref, n0_ref, m0_ref) = refs[:13]
    if emit_state:
        hf_ref, hb_ref, c_out, n_out, m_out, c_sc, n_sc, m_sc = refs[13:]
    else:
        hf_ref, hb_ref, c_sc, n_sc, m_sc = refs[13:]
    i = pl.program_id(1)

    @pl.when(i == 0)
    def _():
        c_sc[...] = c0_ref[0]
        n_sc[...] = n0_ref[0]
        m_sc[...] = m0_ref[0]

    row = lax.broadcasted_iota(jnp.int32, (chunk, chunk), 0)
    col = lax.broadcasted_iota(jnp.int32, (chunk, chunk), 1)
    for d, (q_ref, k_ref, v_ref, g_ref, gt_ref, h_ref) in enumerate(
            ((qf, kf, vf, gf, gtf, hf_ref), (qb, kb, vb, gb, gtb, hb_ref))):
        tri = (col <= row) if d == 0 else (col >= row)
        trif = tri.astype(F32)
        g = g_ref[...]
        gt = gt_ref[...]
        bcol = jnp.dot(trif, g, precision=HIGHEST, preferred_element_type=F32)
        brow = lax.dot_general(gt, trif, (((1,), (1,)), ((), ())), precision=HIGHEST,
                               preferred_element_type=F32)
        blast = bcol[chunk - 1:chunk, :] if d == 0 else bcol[0:1, :]
        for h in range(N_HEADS):
            ci = d * 8 + h
            cf = d * 8 + 4 + h
            j = d * N_HEADS + h
            sl = slice(h * HEAD, (h + 1) * HEAD)
            b_col = bcol[:, cf:cf + 1]
            b_row = brow[cf:cf + 1, :]
            ig_col = g[:, ci:ci + 1]
            ig_row = gt[ci:ci + 1, :]
            m_prev = m_sc[j:j + 1, 0:1]
            log_d = jnp.where(tri, b_col - b_row + ig_row, -jnp.inf)
            inter = b_col + m_prev
            m_t = jnp.maximum(inter, jnp.max(log_d, axis=1, keepdims=True))
            q = q_ref[:, sl]
            qb16 = q.astype(BF16)
            k = k_ref[:, sl]
            v16 = v_ref[:, sl].astype(BF16)
            s = lax.dot_general(qb16, k.astype(BF16), (((1,), (1,)), ((), ())),
                                preferred_element_type=F32) * jnp.exp(log_d - m_t)
            w_inter = jnp.exp(inter - m_t)
            c_prev = c_sc[j]
            n_prev = n_sc[j:j + 1, :]
            num = (w_inter * jnp.dot(qb16, c_prev.astype(BF16), preferred_element_type=F32)
                   + jnp.dot(s.astype(BF16), v16, preferred_element_type=F32))
            den = (w_inter * jnp.sum(q * n_prev, axis=1, keepdims=True)
                   + jnp.sum(s, axis=1, keepdims=True))
            h_ref[:, sl] = num / jnp.maximum(jnp.abs(den), jnp.exp(-m_t))
            b_last = blast[:, cf:cf + 1]
            log_w_col = b_last - b_col + ig_col
            log_w_row = b_last - b_row + ig_row
            m_new = jnp.maximum(b_last + m_prev, jnp.max(log_w_row, axis=1, keepdims=True))
            decay = jnp.exp(b_last + m_prev - m_new)
            kw = k * jnp.exp(log_w_col - m_new)
            c_sc[j] = decay * c_prev + lax.dot_general(
                kw.astype(BF16), v16, (((0,), (0,)), ((), ())), preferred_element_type=F32)
            n_sc[j:j + 1, :] = decay * n_prev + jnp.sum(kw, axis=0, keepdims=True)
            m_sc[j:j + 1, :] = jnp.broadcast_to(m_new, (1, HEAD))

    if emit_state:
        @pl.when(i == nc - 1)
        def _():
            c_out[0] = c_sc[...]
            n_out[0] = n_sc[...]
            m_out[0] = m_sc[...]


def _state_index(bm):
    if bm > 1:
        return lambda b, i: (b,) + (0,) * 3, lambda b, i: (b, 0, 0)
    return lambda b, i: (0,) * 4, lambda b, i: (0, 0, 0)


def _mlstm(q, k, v, g, gt, c0, n0, m0, b, t, chunk, emit_state):
    nc = t // chunk
    n = b * t
    nd = 2 * N_HEADS
    fwd = lambda bi, i: (bi * nc + i, 0)
    bwd = lambda bi, i: (bi * nc + nc - 1 - i, 0)
    fwd_t = lambda bi, i: (0, bi * nc + i)
    bwd_t = lambda bi, i: (0, bi * nc + nc - 1 - i)
    c_idx, n_idx = _state_index(c0.shape[0])
    tok = lambda m: [pl.BlockSpec((chunk, D_ML), m)] * 3 + [pl.BlockSpec((chunk, N_GATE), m)]
    in_specs = (tok(fwd) + [pl.BlockSpec((N_GATE, chunk), fwd_t)]
                + tok(bwd) + [pl.BlockSpec((N_GATE, chunk), bwd_t)]
                + [pl.BlockSpec((1, nd, HEAD, HEAD), c_idx),
                   pl.BlockSpec((1, nd, HEAD), n_idx), pl.BlockSpec((1, nd, HEAD), n_idx)])
    out_specs = [pl.BlockSpec((chunk, D_ML), fwd), pl.BlockSpec((chunk, D_ML), bwd)]
    out_shape = [jax.ShapeDtypeStruct((n, D_ML), F32)] * 2
    if emit_state:
        out_specs += [pl.BlockSpec((1, nd, HEAD, HEAD), lambda bi, i: (bi, 0, 0, 0)),
                      pl.BlockSpec((1, nd, HEAD), lambda bi, i: (bi, 0, 0)),
                      pl.BlockSpec((1, nd, HEAD), lambda bi, i: (bi, 0, 0))]
        out_shape += [jax.ShapeDtypeStruct((b, nd, HEAD, HEAD), F32),
                      jax.ShapeDtypeStruct((b, nd, HEAD), F32),
                      jax.ShapeDtypeStruct((b, nd, HEAD), F32)]
    return pl.pallas_call(
        functools.partial(_mlstm_kernel, chunk, nc, emit_state),
        grid=(b, nc),
        in_specs=in_specs,
        out_specs=out_specs,
        out_shape=out_shape,
        scratch_shapes=[pltpu.VMEM((nd, HEAD, HEAD), F32), pltpu.VMEM((nd, HEAD), F32),
                        pltpu.VMEM((nd, HEAD), F32)],
        name="mlstm",
        compiler_params=_params("arbitrary", "arbitrary"),
    )(q, k, v, g, gt, q, k, v, g, gt, c0, n0, m0)


def _neg_expm1(x):
    u = jnp.exp(x)
    near = jnp.where(u == 1.0, x, (u - 1.0) * x / jnp.log(u))
    return -jnp.where(x < -0.5, u - 1.0, near)


def _rglru_kernel(tb, nb, emit_state, *refs):
    (xf, xf_prev, xf_next, xb, xb_prev, xb_next, h0_ref, w_ref, bias_ref, lam_ref, cw_ref, cb_ref) = refs[:12]
    if emit_state:
        hf_ref, hb_ref, hfin_ref, carry = refs[12:]
    else:
        hf_ref, hb_ref, carry = refs[12:]
    i = pl.program_id(1)

    @pl.when(i == 0)
    def _():
        carry[...] = h0_ref[0]

    row = lax.broadcasted_iota(jnp.int32, (tb, D_RG), 0)
    cw = cw_ref[...]
    softplus_neg_lam = _softplus(-lam_ref[...])

    def conv(main_ref, prev_ref, next_ref, first, last):
        main = main_ref[...]
        prev = jnp.where(first, 0.0, prev_ref[...])
        nxt = jnp.where(last, 0.0, next_ref[...])
        xm2 = jnp.where(row == 0, prev[6:7, :], jnp.where(row == 1, prev[7:8, :], pltpu.roll(main, 2, 0)))
        xm1 = jnp.where(row == 0, prev[7:8, :], pltpu.roll(main, 1, 0))
        xp1 = jnp.where(row == tb - 1, nxt[0:1, :], pltpu.roll(main, tb - 1, 0))
        return cb_ref[...] + xm2 * cw[0:1, :] + xm1 * cw[1:2, :] + main * cw[2:3, :] + xp1 * cw[3:4, :]

    def recurrence_terms(xc, d):
        z = jnp.dot(xc.astype(BF16), w_ref[:, d * 2 * D_RG:(d + 1) * 2 * D_RG],
                    preferred_element_type=F32) + bias_ref[:, d * 2 * D_RG:(d + 1) * 2 * D_RG]
        r = jax.nn.sigmoid(z[:, 0:D_RG])
        ig = jax.nn.sigmoid(z[:, D_RG:2 * D_RG])
        log_a = -RG_C * r * softplus_neg_lam[d:d + 1, :]
        a = jnp.exp(log_a)
        u = jnp.sqrt(_neg_expm1(2.0 * log_a)) * (ig * xc)
        return a, u

    a, u = recurrence_terms(conv(xf, xf_prev, xf_next, i == 0, i == nb - 1), 0)
    sh = 1
    while sh < tb:
        keep = row >= sh
        a_sh = jnp.where(keep, pltpu.roll(a, sh, 0), 1.0)
        u_sh = jnp.where(keep, pltpu.roll(u, sh, 0), 0.0)
        u = a * u_sh + u
        a = a * a_sh
        sh *= 2
    hfw = a * carry[0:1, :] + u
    hf_ref[...] = hfw
    carry[0:1, :] = hfw[tb - 1:tb, :]

    a, u = recurrence_terms(conv(xb, xb_prev, xb_next, i == nb - 1, i == 0), 1)
    sh = 1
    while sh < tb:
        keep = row < tb - sh
        a_sh = jnp.where(keep, pltpu.roll(a, tb - sh, 0), 1.0)
        u_sh = jnp.where(keep, pltpu.roll(u, tb - sh, 0), 0.0)
        u = a * u_sh + u
        a = a * a_sh
        sh *= 2
    hbw = a * carry[1:2, :] + u
    hb_ref[...] = hbw
    carry[1:2, :] = hbw[0:1, :]

    if emit_state:
        @pl.when(i == nb - 1)
        def _():
            hfin_ref[0] = carry[...]


def _rglru(xr, h0, wbd, bias, lam, cw, cb, b, t, tb, emit_state):
    nb = t // tb
    n = b * t
    r8 = tb // SUBLANES
    last8 = n // SUBLANES - 1
    fwd = lambda bi, i: (bi * nb + i, 0)
    bwd = lambda bi, i: (bi * nb + nb - 1 - i, 0)
    fwd_prev = lambda bi, i: (jnp.maximum((bi * nb + i) * r8 - 1, 0), 0)
    fwd_next = lambda bi, i: (jnp.minimum((bi * nb + i + 1) * r8, last8), 0)
    bwd_prev = lambda bi, i: (jnp.maximum((bi * nb + nb - 1 - i) * r8 - 1, 0), 0)
    bwd_next = lambda bi, i: (jnp.minimum((bi * nb + nb - i) * r8, last8), 0)
    const = lambda bi, i: (0, 0)
    h_idx = (lambda bi, i: (bi, 0, 0)) if h0.shape[0] > 1 else (lambda bi, i: (0, 0, 0))
    halo = lambda m: pl.BlockSpec((SUBLANES, D_RG), m)
    in_specs = [pl.BlockSpec((tb, D_RG), fwd), halo(fwd_prev), halo(fwd_next),
                pl.BlockSpec((tb, D_RG), bwd), halo(bwd_prev), halo(bwd_next),
                pl.BlockSpec((1, 2, D_RG), h_idx),
                pl.BlockSpec((D_RG, 4 * D_RG), const), pl.BlockSpec((1, 4 * D_RG), const),
                pl.BlockSpec((2, D_RG), const), pl.BlockSpec((4, D_RG), const), pl.BlockSpec((1, D_RG), const)]
    out_specs = [pl.BlockSpec((tb, D_RG), fwd), pl.BlockSpec((tb, D_RG), bwd)]
    out_shape = [jax.ShapeDtypeStruct((n, D_RG), F32)] * 2
    if emit_state:
        out_specs.append(pl.BlockSpec((1, 2, D_RG), lambda bi, i: (bi, 0, 0)))
        out_shape.append(jax.ShapeDtypeStruct((b, 2, D_RG), F32))
    return pl.pallas_call(
        functools.partial(_rglru_kernel, tb, nb, emit_state),
        grid=(b, nb),
        in_specs=in_specs,
        out_specs=out_specs,
        out_shape=out_shape,
        scratch_shapes=[pltpu.VMEM((2, D_RG), F32)],
        name="rglru",
        compiler_params=_params("arbitrary", "arbitrary"),
    )(xr, xr, xr, xr, xr, xr, h0, wbd, bias, lam, cw, cb)


def _route(s, sb):
    tm = s.shape[1]
    neg = -jnp.inf
    sub = lax.broadcasted_iota(jnp.int32, (GROUP, tm), 0)
    blocks = [sb[gi * GROUP:(gi + 1) * GROUP, :] for gi in range(N_GROUPS)]
    gscore = []
    for blk in blocks:
        m1 = jnp.max(blk, axis=0, keepdims=True)
        first = jnp.min(jnp.where(blk == m1, sub, GROUP), axis=0, keepdims=True)
        m2 = jnp.max(jnp.where(sub == first, neg, blk), axis=0, keepdims=True)
        gscore.append(m1 + m2)
    masked = []
    for gi in range(N_GROUPS):
        rank = jnp.zeros((1, tm), F32)
        for gj in range(N_GROUPS):
            if gj == gi:
                continue
            ahead = (gscore[gj] >= gscore[gi]) if gj < gi else (gscore[gj] > gscore[gi])
            rank = rank + jnp.where(ahead, 1.0, 0.0)
        masked.append(jnp.where(rank < TOPK_GROUPS, blocks[gi], neg))
    v = jnp.concatenate(masked, axis=0)
    eid = lax.broadcasted_iota(jnp.int32, (N_EXPERTS, tm), 0)
    sel = jnp.zeros((N_EXPERTS, tm), F32)
    for _ in range(TOP_K):
        mx = jnp.max(v, axis=0, keepdims=True)
        idx = jnp.min(jnp.where(v == mx, eid, N_EXPERTS), axis=0, keepdims=True)
        pick = eid == idx
        sel = jnp.where(pick, 1.0, sel)
        v = jnp.where(pick, neg, v)
    ws = s * sel
    return ws / jnp.sum(ws, axis=0, keepdims=True) * ROUTED_SCALE


def _mix_out_kernel(hmf_ref, hmb_ref, o_ref, hrf_ref, hrb_ref, gr_ref, x_ref, pos_ref, mod_ref,
                    mln_ref, rgn_ref, wo_ml_ref, wo_rg_ref, n2_ref, rwt_ref, rb_ref,
                    x1_ref, hn2_ref, w_ref):
    hm = hmf_ref[...] + hmb_ref[...]
    heads = []
    for h in range(N_HEADS):
        seg = hm[:, h * HEAD:(h + 1) * HEAD]
        heads.append(seg * lax.rsqrt(jnp.mean(seg * seg, axis=-1, keepdims=True) + EPS))
    y_ml = jnp.concatenate(heads, axis=1) * mln_ref[...] * jax.nn.sigmoid(o_ref[...])
    y_rg = _rms(hrf_ref[...] + hrb_ref[...], rgn_ref[...]) * jax.nn.gelu(gr_ref[...])
    mix = (jnp.dot(y_ml.astype(BF16), wo_ml_ref[...], preferred_element_type=F32)
           + jnp.dot(y_rg.astype(BF16), wo_rg_ref[...], preferred_element_type=F32))
    x1 = x_ref[...] + pos_ref[...] + mod_ref[0, 2:3, :] * mix
    x1_ref[...] = x1
    hn2 = _rms(x1, n2_ref[...]) * (1.0 + mod_ref[0, 4:5, :]) + mod_ref[0, 3:4, :]
    hn2_ref[...] = hn2.astype(BF16)
    logits_t = lax.dot_general(rwt_ref[...], hn2, (((1,), (1,)), ((), ())), precision=HIGHEST,
                               preferred_element_type=F32)
    s = jax.nn.sigmoid(logits_t)
    w_ref[...] = _route(s, s + rb_ref[...]).T


def _mix_out(hmf, hmb, o, hrf, hrb, gr, x2d, pos, mod, t, tm, mln, rgn, wo_ml, wo_rg, norm2, rwt, rbias):
    n = x2d.shape[0]
    npos = pos.shape[0] // tm
    tok = lambda i: (i, 0)
    const = lambda i: (0, 0)
    return pl.pallas_call(
        _mix_out_kernel,
        grid=(n // tm,),
        in_specs=[pl.BlockSpec((tm, D_ML), tok)] * 6 + [
            pl.BlockSpec((tm, D_MODEL), tok),
            pl.BlockSpec((tm, D_MODEL), lambda i: (i % npos, 0)),
            pl.BlockSpec((1, N_MOD, D_MODEL), _mod_index(mod.shape[0], tm, t)),
            pl.BlockSpec((1, D_ML), const), pl.BlockSpec((1, D_RG), const),
            pl.BlockSpec((D_ML, D_MODEL), const), pl.BlockSpec((D_RG, D_MODEL), const),
            pl.BlockSpec((1, D_MODEL), const),
            pl.BlockSpec((N_EXPERTS, D_MODEL), const), pl.BlockSpec((N_EXPERTS, 1), const),
        ],
        out_specs=[pl.BlockSpec((tm, D_MODEL), tok), pl.BlockSpec((tm, D_MODEL), tok),
                   pl.BlockSpec((tm, N_EXPERTS), tok)],
        out_shape=[jax.ShapeDtypeStruct((n, D_MODEL), F32), jax.ShapeDtypeStruct((n, D_MODEL), BF16),
                   jax.ShapeDtypeStruct((n, N_EXPERTS), F32)],
        name="mix_out",
        compiler_params=_params("arbitrary"),
    )(hmf, hmb, o, hrf, hrb, gr, x2d, pos, mod, mln, rgn, wo_ml, wo_rg, norm2, rwt, rbias)


def _swiglu(x, w13):
    h = jnp.dot(x, w13, preferred_element_type=F32)
    return _silu(h[:, 0:D_EXPERT]) * h[:, D_EXPERT:2 * D_EXPERT]


def _moe_kernel(hn2_ref, w_ref, w13_ref, w2_ref, sw13_ref, sw2_ref, x1_ref, mod_ref, nf_ref, y_ref, acc):
    e = pl.program_id(1)
    x = hn2_ref[...]

    @pl.when(e == 0)
    def _():
        acc[...] = jnp.dot(_swiglu(x, sw13_ref[...]).astype(BF16), sw2_ref[...], preferred_element_type=F32)

    w = w_ref[...]
    lane = lax.broadcasted_iota(jnp.int32, w.shape, 1)
    wcol = jnp.sum(jnp.where(lane == e, w, 0.0), axis=1, keepdims=True)
    hh = _swiglu(x, w13_ref[0]) * wcol
    acc[...] += jnp.dot(hh.astype(BF16), w2_ref[0], preferred_element_type=F32)

    @pl.when(e == N_EXPERTS - 1)
    def _():
        x2 = x1_ref[...] + mod_ref[0, 5:6, :] * acc[...]
        y_ref[...] = _rms(x2, nf_ref[...])


def _moe(hn2, wroute, w13, w2, sw13, sw2, x1, mod, t, tm, norm_final):
    n = hn2.shape[0]
    tok = lambda i, e: (i, 0)
    const = lambda i, e: (0, 0)
    mod_idx = _mod_index(mod.shape[0], tm, t)
    return pl.pallas_call(
        _moe_kernel,
        grid=(n // tm, N_EXPERTS),
        in_specs=[
            pl.BlockSpec((tm, D_MODEL), tok),
            pl.BlockSpec((tm, N_EXPERTS), tok),
            pl.BlockSpec((1, D_MODEL, 2 * D_EXPERT), lambda i, e: (e, 0, 0)),
            pl.BlockSpec((1, D_EXPERT, D_MODEL), lambda i, e: (e, 0, 0)),
            pl.BlockSpec((D_MODEL, 2 * D_EXPERT), const),
            pl.BlockSpec((D_EXPERT, D_MODEL), const),
            pl.BlockSpec((tm, D_MODEL), tok),
            pl.BlockSpec((1, N_MOD, D_MODEL), lambda i, e: mod_idx(i)),
            pl.BlockSpec((1, D_MODEL), const),
        ],
        out_specs=pl.BlockSpec((tm, D_MODEL), tok),
        out_shape=jax.ShapeDtypeStruct((n, D_MODEL), F32),
        scratch_shapes=[pltpu.VMEM((tm, D_MODEL), F32)],
        name="moe",
        compiler_params=_params("arbitrary", "arbitrary"),
    )(hn2, wroute, w13, w2, sw13, sw2, x1, mod, norm_final)


def _grid_pos_embed(n_tokens, dim):
    rows_n = n_tokens // GRID_W
    row = jnp.repeat(jnp.arange(rows_n), GRID_W).astype(F32)
    col = jnp.tile(jnp.arange(GRID_W), rows_n).astype(F32)
    quarter = dim // 4
    omega = 1.0 / (POS_BASE ** (jnp.arange(quarter, dtype=F32) / quarter))
    ra = row[:, None] * omega
    ca = col[:, None] * omega
    return jnp.concatenate([jnp.sin(ra), jnp.cos(ra), jnp.sin(ca), jnp.cos(ca)], axis=-1)


def _block_diag(w):
    eye = jnp.eye(N_RG_BLOCKS, dtype=w.dtype)
    return jnp.einsum('nij,nm->nimj', w, eye).reshape(D_RG, D_RG)


def _layer_weights(l, norm1, w_in, mlstm_gate_bias, mlstm_norm, rg_conv_w, rg_conv_b, rg_wa, rg_ba, rg_wx,
                   rg_bx, rg_lambda, rg_norm, w_out, norm2, router_w, router_bias, exp_w1, exp_w3, exp_w2,
                   shared_w1, shared_w3, shared_w2):
    wi = w_in[l]
    c0, c1, c2 = 4 * D_ML, 4 * D_ML + N_GATE, 4 * D_ML + N_GATE + D_RG
    wg = wi[:, c0:c1]
    return dict(
        norm1=norm1[l].reshape(1, D_MODEL),
        wq=wi[:, :c0].astype(BF16),
        wr=wi[:, c1:].astype(BF16),
        wg=wg, wgt=wg.T,
        gbias=mlstm_gate_bias[l].reshape(N_GATE),
        mln=mlstm_norm[l].reshape(1, D_ML),
        cw=rg_conv_w[l], cb=rg_conv_b[l].reshape(1, D_RG),
        wbd=jnp.concatenate([_block_diag(rg_wa[l, 0]), _block_diag(rg_wx[l, 0]),
                             _block_diag(rg_wa[l, 1]), _block_diag(rg_wx[l, 1])], axis=1).astype(BF16),
        rbias=jnp.concatenate([rg_ba[l, 0], rg_bx[l, 0], rg_ba[l, 1], rg_bx[l, 1]]).reshape(1, 4 * D_RG),
        lam=rg_lambda[l], rgn=rg_norm[l].reshape(1, D_RG),
        wo_ml=w_out[l, :D_ML].astype(BF16), wo_rg=w_out[l, D_ML:].astype(BF16),
        norm2=norm2[l].reshape(1, D_MODEL),
        rwt=router_w[l].T, rb=router_bias[l].reshape(N_EXPERTS, 1),
        w13=jnp.concatenate([exp_w1[l], exp_w3[l]], axis=-1).astype(BF16),
        w2=exp_w2[l].astype(BF16),
        sw13=jnp.concatenate([shared_w1[l], shared_w3[l]], axis=-1).astype(BF16),
        sw2=shared_w2[l].astype(BF16),
    )


def _trunk_layer(x2d, pos, mod, c0, n0, m0, h0, lw, b, t, emit_state, norm_final):
    tm = min(512, t) if mod.shape[0] > 1 else 512
    q, k, v, o, xr, gr, g, gt = _in_proj(x2d, pos, mod, t, tm, lw['norm1'], lw['wq'], lw['wr'], lw['wg'],
                                         lw['wgt'], lw['gbias'])
    ml = _mlstm(q, k, v, g, gt, c0, n0, m0, b, t, min(256, t), emit_state)
    rg = _rglru(xr, h0, lw['wbd'], lw['rbias'], lw['lam'], lw['cw'], lw['cb'], b, t, min(512, t), emit_state)
    x1, hn2, wroute = _mix_out(ml[0], ml[1], o, rg[0], rg[1], gr, x2d, pos, mod, t, tm, lw['mln'], lw['rgn'],
                               lw['wo_ml'], lw['wo_rg'], lw['norm2'], lw['rwt'], lw['rb'])
    tm_moe = min(1024, t) if mod.shape[0] > 1 else 1024
    y = _moe(hn2, wroute, lw['w13'], lw['w2'], lw['sw13'], lw['sw2'], x1, mod, t, tm_moe, norm_final)
    return y, ml[2:], rg[2:]


def kernel(x_prompt, x_sample, c, state_mlstm_C, state_mlstm_n, state_mlstm_m, state_rglru_h, c_ctx, w_ada, b_ada, norm1, w_in, mlstm_gate_bias, mlstm_norm, rg_conv_w, rg_conv_b, rg_wa, rg_ba, rg_wx, rg_bx, rg_lambda, rg_norm, w_out, norm2, router_w, router_bias, exp_w1, exp_w3, exp_w2, shared_w1, shared_w3, shared_w2, norm_final):
    bp, tp, _ = x_prompt.shape
    bs, ts, _ = x_sample.shape
    depth = w_ada.shape[0]
    assert depth == 1, "the final norm is fused into the single layer's MoE kernel"
    nd = 2 * N_HEADS
    l = 0
    lw = _layer_weights(l, norm1, w_in, mlstm_gate_bias, mlstm_norm, rg_conv_w, rg_conv_b, rg_wa, rg_ba, rg_wx,
                        rg_bx, rg_lambda, rg_norm, w_out, norm2, router_w, router_bias, exp_w1, exp_w3, exp_w2,
                        shared_w1, shared_w3, shared_w2)
    nf = norm_final.reshape(1, D_MODEL)
    cvecs = jnp.concatenate([c_ctx[None], c, jnp.zeros((SUBLANES - 1 - bs, D_MODEL), F32)], axis=0)
    mod = _ada(cvecs, w_ada[l], b_ada[l]).reshape(SUBLANES, N_MOD, D_MODEL)

    yp, (cc, nc_, mc), (hc,) = _trunk_layer(
        x_prompt.reshape(bp * tp, D_MODEL), jnp.zeros((512, D_MODEL), F32), mod[0:1],
        jnp.zeros((1, nd, HEAD, HEAD), F32), jnp.zeros((1, nd, HEAD), F32), jnp.zeros((1, nd, HEAD), F32),
        jnp.zeros((1, 2, D_RG), F32), lw, bp, tp, True, nf)
    ys, _, _ = _trunk_layer(
        x_sample.reshape(bs * ts, D_MODEL), _grid_pos_embed(ts, D_MODEL), mod[1:1 + bs],
        state_mlstm_C[:, l].reshape(bs, nd, HEAD, HEAD), state_mlstm_n[:, l].reshape(bs, nd, HEAD),
        jnp.broadcast_to(state_mlstm_m[:, l].reshape(bs, nd, 1), (bs, nd, HEAD)),
        state_rglru_h[:, l], lw, bs, ts, False, nf)

    y_prompt = yp.reshape(bp, tp, D_MODEL)
    y_sample = ys.reshape(bs, ts, D_MODEL)
    new_c = cc.reshape(bp, 1, 2, N_HEADS, HEAD, HEAD)
    new_n = nc_.reshape(bp, 1, 2, N_HEADS, HEAD)
    new_m = mc[:, :, 0].reshape(bp, 1, 2, N_HEADS)
    new_h = hc.reshape(bp, 1, 2, D_RG)
    return (y_prompt, y_sample, new_c, new_n, new_m, new_h)
```

```python
import functools

import jax
import jax.numpy as jnp
from jax import lax
from jax.experimental import pallas as pl
from jax.experimental.pallas import tpu as pltpu

F32 = jnp.float32
BF16 = jnp.bfloat16
HIGHEST = lax.Precision.HIGHEST

D_MODEL = 1024
N_MOD = 6
D_ML = 512
N_HEADS = 4
HEAD = 128
D_RG = 512
N_RG_BLOCKS = 8
RG_BLOCK = 64
RG_C = 8.0
N_GATE = 16
N_EXPERTS = 64
N_GROUPS = 8
GROUP = 8
TOPK_GROUPS = 4
TOP_K = 8
D_EXPERT = 256
ROUTED_SCALE = 2.5
EPS = 1e-6
GRID_W = 64
POS_BASE = 10000.0

MOE_EXPERTS_PER_STEP = 4
MOE_ROW_SPLIT = 2
RG_SEG_PAD = 8

SUBLANES = 8
LANES = 128
VMEM_LIMIT = 48 * 1024 * 1024


def _params(*sem):
    return pltpu.CompilerParams(dimension_semantics=sem, vmem_limit_bytes=VMEM_LIMIT)


def _silu(x):
    return x * jax.nn.sigmoid(x)


def _softplus(x):
    return jnp.maximum(x, 0.0) + jnp.log1p(jnp.exp(-jnp.abs(x)))


def _rms(x, g):
    return x * lax.rsqrt(jnp.mean(x * x, axis=-1, keepdims=True) + EPS) * g


def _ada_kernel(c_ref, w_ref, b_ref, o_ref):
    s = _silu(c_ref[...])
    o_ref[...] = jnp.dot(s, w_ref[...], precision=HIGHEST, preferred_element_type=F32) + b_ref[...]


def _ada(cvecs, w_ada, b_ada):
    n_out = w_ada.shape[1]
    tn = 1536
    return pl.pallas_call(
        _ada_kernel,
        grid=(n_out // tn,),
        in_specs=[
            pl.BlockSpec((SUBLANES, D_MODEL), lambda j: (0, 0)),
            pl.BlockSpec((D_MODEL, tn), lambda j: (0, j)),
            pl.BlockSpec((1, tn), lambda j: (0, j)),
        ],
        out_specs=pl.BlockSpec((SUBLANES, tn), lambda j: (0, j)),
        out_shape=jax.ShapeDtypeStruct((SUBLANES, n_out), F32),
        name="ada",
        compiler_params=_params("arbitrary"),
    )(cvecs, w_ada, b_ada.reshape(1, n_out))


def _in_proj_kernel(x_ref, pos_ref, mod_ref, n1_ref, wq_ref, wr_ref, wgh_ref, gb_ref,
                    q_ref, k_ref, v_ref, o_ref, xr_ref, gr_ref, g_ref, gt_ref):
    x = x_ref[...] + pos_ref[...]
    hn = _rms(x, n1_ref[...]) * (1.0 + mod_ref[0, 1:2, :]) + mod_ref[0, 0:1, :]
    hb = hn.astype(BF16)
    z = jnp.dot(hb, wq_ref[...], preferred_element_type=F32)
    q_ref[...] = z[:, 0:D_ML]
    k_ref[...] = z[:, D_ML:2 * D_ML] * (HEAD ** -0.5)
    v_ref[...] = z[:, 2 * D_ML:3 * D_ML]
    o_ref[...] = z[:, 3 * D_ML:4 * D_ML]
    zr = jnp.dot(hb, wr_ref[...], preferred_element_type=F32)
    xr_ref[...] = zr[:, 0:D_RG]
    gr_ref[...] = zr[:, D_RG:2 * D_RG]
    zg = zr[:, 2 * D_RG:2 * D_RG + LANES]
    h_lo = (hn - hb.astype(F32)).astype(BF16)
    g = (zg + pltpu.roll(zg, LANES - N_GATE, 1)
         + jnp.dot(h_lo, wgh_ref[...], preferred_element_type=F32) + gb_ref[...])
    col = lax.broadcasted_iota(jnp.int32, g.shape, 1)
    g = jnp.where((col & 4) != 0, -_softplus(-g), g)
    g_ref[...] = g[:, 0:N_GATE]
    gt_ref[...] = g.T[0:N_GATE, :]


def _mod_index(bm, tm, t):
    if bm > 1:
        return lambda i: ((i * tm) // t, 0, 0)
    return lambda i: (0, 0, 0)


def _in_proj(x2d, pos, mod, t, tm, norm1, wq, wr, wgh, gbias):
    n = x2d.shape[0]
    npos = pos.shape[0] // tm
    tok = lambda i: (i, 0)
    const = lambda i: (0, 0)
    f = lambda w: jax.ShapeDtypeStruct((n, w), F32)
    return pl.pallas_call(
        _in_proj_kernel,
        grid=(n // tm,),
        in_specs=[
            pl.BlockSpec((tm, D_MODEL), tok),
            pl.BlockSpec((tm, D_MODEL), lambda i: (i % npos, 0)),
            pl.BlockSpec((1, N_MOD, D_MODEL), _mod_index(mod.shape[0], tm, t)),
            pl.BlockSpec((1, D_MODEL), const),
            pl.BlockSpec((D_MODEL, 4 * D_ML), const),
            pl.BlockSpec((D_MODEL, 2 * D_RG + LANES), const),
            pl.BlockSpec((D_MODEL, LANES), const),
            pl.BlockSpec((1, LANES), const),
        ],
        out_specs=[pl.BlockSpec((tm, D_ML), tok)] * 4 + [pl.BlockSpec((tm, D_RG), tok)] * 2
        + [pl.BlockSpec((tm, N_GATE), tok), pl.BlockSpec((N_GATE, tm), lambda i: (0, i))],
        out_shape=[f(D_ML)] * 4 + [f(D_RG)] * 2 + [f(N_GATE), jax.ShapeDtypeStruct((N_GATE, n), F32)],
        name="in_proj",
        compiler_params=_params("arbitrary"),
    )(x2d, pos, mod, norm1, wq, wr, wgh, gbias)


def _mlstm_kernel(chunk, nc, emit_state, *refs):
    (qf, kf, vf, gf, gtf, qb, kb, vb, gb, gtb, c0_ref, n0_ref, m0_ref) = refs[:13]
    if emit_state:
        hf_ref, hb_ref, c_out, n_out, m_out, c_sc, n_sc, m_sc = refs[13:]
    else:
        hf_ref, hb_ref, c_sc, n_sc, m_sc = refs[13:]
    i = pl.program_id(1)

    @pl.when(i == 0)
    def _():
        c_sc[...] = c0_ref[0]
        n_sc[...] = n0_ref[0]
        m_sc[...] = m0_ref[0]

    row = lax.broadcasted_iota(jnp.int32, (chunk, chunk), 0)
    col = lax.broadcasted_iota(jnp.int32, (chunk, chunk), 1)
    hd = []
    for d, (q_ref, k_ref, v_ref, g_ref, gt_ref, h_ref) in enumerate(
            ((qf, kf, vf, gf, gtf, hf_ref), (qb, kb, vb, gb, gtb, hb_ref))):
        tri = (col <= row) if d == 0 else (col >= row)
        trif = tri.astype(F32)
        g = g_ref[...]
        gt = gt_ref[...]
        bcol = jnp.dot(trif, g, precision=HIGHEST, preferred_element_type=F32)
        brow = lax.dot_general(gt, trif, (((1,), (1,)), ((), ())), precision=HIGHEST,
                               preferred_element_type=F32)
        blast = bcol[chunk - 1:chunk, :] if d == 0 else bcol[0:1, :]
        for h in range(N_HEADS):
            ci = d * 8 + h
            cf = d * 8 + 4 + h
            j = d * N_HEADS + h
            hd.append(dict(
                j=j, sl=slice(h * HEAD, (h + 1) * HEAD), tri=tri, q_ref=q_ref, k_ref=k_ref, v_ref=v_ref,
                h_ref=h_ref, b_col=bcol[:, cf:cf + 1], b_row=brow[cf:cf + 1, :], ig_col=g[:, ci:ci + 1],
                ig_row=gt[ci:ci + 1, :], b_last=blast[:, cf:cf + 1], m_prev=m_sc[j:j + 1, 0:1],
                c_prev=c_sc[j], n_prev=n_sc[j:j + 1, :]))
    for x in hd:
        log_d = jnp.where(x['tri'], x['b_col'] - x['b_row'] + x['ig_row'], -jnp.inf)
        inter = x['b_col'] + x['m_prev']
        m_t = jnp.maximum(inter, jnp.max(log_d, axis=1, keepdims=True))
        x['dm'] = jnp.exp(log_d - m_t)
        x['w_inter'] = jnp.exp(inter - m_t)
        x['floor'] = jnp.exp(-m_t)
    for x in hd:
        x['q'] = x['q_ref'][:, x['sl']]
        x['k'] = x['k_ref'][:, x['sl']]
        x['q16'] = x['q'].astype(BF16)
        x['v16'] = x['v_ref'][:, x['sl']].astype(BF16)
        x['s'] = lax.dot_general(x['q16'], x['k'].astype(BF16), (((1,), (1,)), ((), ())),
                                 preferred_element_type=F32) * x['dm']
    for x in hd:
        num = (x['w_inter'] * jnp.dot(x['q16'], x['c_prev'].astype(BF16), preferred_element_type=F32)
               + jnp.dot(x['s'].astype(BF16), x['v16'], preferred_element_type=F32))
        den = (x['w_inter'] * jnp.sum(x['q'] * x['n_prev'], axis=1, keepdims=True)
               + jnp.sum(x['s'], axis=1, keepdims=True))
        x['h_ref'][:, x['sl']] = num / jnp.maximum(jnp.abs(den), x['floor'])
    for x in hd:
        j = x['j']
        log_w_col = x['b_last'] - x['b_col'] + x['ig_col']
        log_w_row = x['b_last'] - x['b_row'] + x['ig_row']
        m_new = jnp.maximum(x['b_last'] + x['m_prev'], jnp.max(log_w_row, axis=1, keepdims=True))
        decay = jnp.exp(x['b_last'] + x['m_prev'] - m_new)
        kw = x['k'] * jnp.exp(log_w_col - m_new)
        c_sc[j] = decay * x['c_prev'] + lax.dot_general(
            kw.astype(BF16), x['v16'], (((0,), (0,)), ((), ())), preferred_element_type=F32)
        n_sc[j:j + 1, :] = decay * x['n_prev'] + jnp.sum(kw, axis=0, keepdims=True)
        m_sc[j:j + 1, :] = jnp.broadcast_to(m_new, (1, HEAD))

    if emit_state:
        @pl.when(i == nc - 1)
        def _():
            c_out[0] = c_sc[...]
            n_out[0] = n_sc[...]
            m_out[0] = m_sc[...]


def _state_index(bm):
    if bm > 1:
        return lambda b, i: (b,) + (0,) * 3, lambda b, i: (b, 0, 0)
    return lambda b, i: (0,) * 4, lambda b, i: (0, 0, 0)


def _mlstm(q, k, v, g, gt, c0, n0, m0, b, t, chunk, emit_state):
    nc = t // chunk
    n = b * t
    nd = 2 * N_HEADS
    fwd = lambda bi, i: (bi * nc + i, 0)
    bwd = lambda bi, i: (bi * nc + nc - 1 - i, 0)
    fwd_t = lambda bi, i: (0, bi * nc + i)
    bwd_t = lambda bi, i: (0, bi * nc + nc - 1 - i)
    c_idx, n_idx = _state_index(c0.shape[0])
    tok = lambda m: [pl.BlockSpec((chunk, D_ML), m)] * 3 + [pl.BlockSpec((chunk, N_GATE), m)]
    in_specs = (tok(fwd) + [pl.BlockSpec((N_GATE, chunk), fwd_t)]
                + tok(bwd) + [pl.BlockSpec((N_GATE, chunk), bwd_t)]
                + [pl.BlockSpec((1, nd, HEAD, HEAD), c_idx),
                   pl.BlockSpec((1, nd, HEAD), n_idx), pl.BlockSpec((1, nd, HEAD), n_idx)])
    out_specs = [pl.BlockSpec((chunk, D_ML), fwd), pl.BlockSpec((chunk, D_ML), bwd)]
    out_shape = [jax.ShapeDtypeStruct((n, D_ML), F32)] * 2
    if emit_state:
        out_specs += [pl.BlockSpec((1, nd, HEAD, HEAD), lambda bi, i: (bi, 0, 0, 0)),
                      pl.BlockSpec((1, nd, HEAD), lambda bi, i: (bi, 0, 0)),
                      pl.BlockSpec((1, nd, HEAD), lambda bi, i: (bi, 0, 0))]
        out_shape += [jax.ShapeDtypeStruct((b, nd, HEAD, HEAD), F32),
                      jax.ShapeDtypeStruct((b, nd, HEAD), F32),
                      jax.ShapeDtypeStruct((b, nd, HEAD), F32)]
    return pl.pallas_call(
        functools.partial(_mlstm_kernel, chunk, nc, emit_state),
        grid=(b, nc),
        in_specs=in_specs,
        out_specs=out_specs,
        out_shape=out_shape,
        scratch_shapes=[pltpu.VMEM((nd, HEAD, HEAD), F32), pltpu.VMEM((nd, HEAD), F32),
                        pltpu.VMEM((nd, HEAD), F32)],
        name="mlstm",
        compiler_params=_params("arbitrary", "arbitrary"),
    )(q, k, v, g, gt, q, k, v, g, gt, c0, n0, m0)


def _neg_expm1(x):
    u = jnp.exp(x)
    near = jnp.where(u == 1.0, x, (u - 1.0) * x / jnp.log(u))
    return -jnp.where(x < -0.5, u - 1.0, near)


def _rglru_kernel(tb, nb, emit_state, *refs):
    (xf, xf_prev, xf_next, xb, xb_prev, xb_next, h0_ref, w_ref, bias_ref, lam_ref, cw_ref, cb_ref) = refs[:12]
    n_out = 3 if emit_state else 2
    hf_ref, hb_ref = refs[12:14]
    hfin_ref = refs[14] if emit_state else None
    carry, af_sc, uf_sc, ab_sc, ub_sc, hf_sc, pf_sc, hb_sc, pb_sc = refs[12 + n_out:]
    i = pl.program_id(1)

    @pl.when(i == 0)
    def _():
        carry[...] = h0_ref[0]

    row = lax.broadcasted_iota(jnp.int32, (tb, D_RG), 0)
    cw = cw_ref[...]
    softplus_neg_lam = _softplus(-lam_ref[...])

    def conv(main_ref, prev_ref, next_ref, first, last):
        main = main_ref[...]
        prev = jnp.where(first, 0.0, prev_ref[...])
        nxt = jnp.where(last, 0.0, next_ref[...])
        xm2 = jnp.where(row == 0, prev[6:7, :], jnp.where(row == 1, prev[7:8, :], pltpu.roll(main, 2, 0)))
        xm1 = jnp.where(row == 0, prev[7:8, :], pltpu.roll(main, 1, 0))
        xp1 = jnp.where(row == tb - 1, nxt[0:1, :], pltpu.roll(main, tb - 1, 0))
        return cb_ref[...] + xm2 * cw[0:1, :] + xm1 * cw[1:2, :] + main * cw[2:3, :] + xp1 * cw[3:4, :]

    def recurrence_terms(xc, d):
        z = jnp.dot(xc.astype(BF16), w_ref[:, d * 2 * D_RG:(d + 1) * 2 * D_RG],
                    preferred_element_type=F32) + bias_ref[:, d * 2 * D_RG:(d + 1) * 2 * D_RG]
        r = jax.nn.sigmoid(z[:, 0:D_RG])
        ig = jax.nn.sigmoid(z[:, D_RG:2 * D_RG])
        log_a = -RG_C * r * softplus_neg_lam[d:d + 1, :]
        a = jnp.exp(log_a)
        u = jnp.sqrt(_neg_expm1(2.0 * log_a)) * (ig * xc)
        return a, u

    a_f, u_f = recurrence_terms(conv(xf, xf_prev, xf_next, i == 0, i == nb - 1), 0)
    a_b, u_b = recurrence_terms(conv(xb, xb_prev, xb_next, i == nb - 1, i == 0), 1)
    seg = tb // SUBLANES
    pitch = seg + RG_SEG_PAD
    ncol = D_RG // LANES
    for lc in range(ncol):
        lanes = slice(lc * LANES, (lc + 1) * LANES)
        for s in range(SUBLANES):
            src = slice(s * seg, (s + 1) * seg)
            dst = slice(s * pitch, s * pitch + seg)
            af_sc[lc, dst, :], uf_sc[lc, dst, :] = a_f[src, lanes], u_f[src, lanes]
            ab_sc[lc, dst, :], ub_sc[lc, dst, :] = a_b[src, lanes], u_b[src, lanes]
    slab = lambda k: (slice(None), pl.ds(k, SUBLANES, stride=pitch), slice(None))
    hf = jnp.zeros((ncol, SUBLANES, LANES), F32)
    hb = jnp.zeros((ncol, SUBLANES, LANES), F32)
    pf = jnp.ones((ncol, SUBLANES, LANES), F32)
    pb = jnp.ones((ncol, SUBLANES, LANES), F32)
    for k in range(seg):
        kb = seg - 1 - k
        ak = af_sc[slab(k)]
        hf = ak * hf + uf_sc[slab(k)]
        pf = pf * ak
        hf_sc[slab(k)] = hf
        pf_sc[slab(k)] = pf
        ak = ab_sc[slab(kb)]
        hb = ak * hb + ub_sc[slab(kb)]
        pb = pb * ak
        hb_sc[slab(kb)] = hb
        pb_sc[slab(kb)] = pb
    for lc in range(ncol):
        lanes = slice(lc * LANES, (lc + 1) * LANES)
        c = carry[0:1, lanes]
        cin_f = []
        for s in range(SUBLANES):
            cin_f.append(c)
            c = pf[lc, s:s + 1, :] * c + hf[lc, s:s + 1, :]
        carry[0:1, lanes] = c
        c = carry[1:2, lanes]
        cin_b = [None] * SUBLANES
        for s in reversed(range(SUBLANES)):
            cin_b[s] = c
            c = pb[lc, s:s + 1, :] * c + hb[lc, s:s + 1, :]
        carry[1:2, lanes] = c
        for s in range(SUBLANES):
            rows = slice(s * seg, (s + 1) * seg)
            src = slice(s * pitch, s * pitch + seg)
            hf_ref[rows, lanes] = hf_sc[lc, src, :] + pf_sc[lc, src, :] * cin_f[s]
            hb_ref[rows, lanes] = hb_sc[lc, src, :] + pb_sc[lc, src, :] * cin_b[s]

    if emit_state:
        @pl.when(i == nb - 1)
        def _():
            hfin_ref[0] = carry[...]


def _rglru(xr, h0, wbd, bias, lam, cw, cb, b, t, tb, emit_state):
    nb = t // tb
    n = b * t
    r8 = tb // SUBLANES
    last8 = n // SUBLANES - 1
    fwd = lambda bi, i: (bi * nb + i, 0)
    bwd = lambda bi, i: (bi * nb + nb - 1 - i, 0)
    fwd_prev = lambda bi, i: (jnp.maximum((bi * nb + i) * r8 - 1, 0), 0)
    fwd_next = lambda bi, i: (jnp.minimum((bi * nb + i + 1) * r8, last8), 0)
    bwd_prev = lambda bi, i: (jnp.maximum((bi * nb + nb - 1 - i) * r8 - 1, 0), 0)
    bwd_next = lambda bi, i: (jnp.minimum((bi * nb + nb - i) * r8, last8), 0)
    const = lambda bi, i: (0, 0)
    h_idx = (lambda bi, i: (bi, 0, 0)) if h0.shape[0] > 1 else (lambda bi, i: (0, 0, 0))
    halo = lambda m: pl.BlockSpec((SUBLANES, D_RG), m)
    in_specs = [pl.BlockSpec((tb, D_RG), fwd), halo(fwd_prev), halo(fwd_next),
                pl.BlockSpec((tb, D_RG), bwd), halo(bwd_prev), halo(bwd_next),
                pl.BlockSpec((1, 2, D_RG), h_idx),
                pl.BlockSpec((D_RG, 4 * D_RG), const), pl.BlockSpec((1, 4 * D_RG), const),
                pl.BlockSpec((2, D_RG), const), pl.BlockSpec((4, D_RG), const), pl.BlockSpec((1, D_RG), const)]
    out_specs = [pl.BlockSpec((tb, D_RG), fwd), pl.BlockSpec((tb, D_RG), bwd)]
    out_shape = [jax.ShapeDtypeStruct((n, D_RG), F32)] * 2
    if emit_state:
        out_specs.append(pl.BlockSpec((1, 2, D_RG), lambda bi, i: (bi, 0, 0)))
        out_shape.append(jax.ShapeDtypeStruct((b, 2, D_RG), F32))
    return pl.pallas_call(
        functools.partial(_rglru_kernel, tb, nb, emit_state),
        grid=(b, nb),
        in_specs=in_specs,
        out_specs=out_specs,
        out_shape=out_shape,
        scratch_shapes=[pltpu.VMEM((2, D_RG), F32)]
        + [pltpu.VMEM((D_RG // LANES, tb + SUBLANES * RG_SEG_PAD, LANES), F32)] * 8,
        name="rglru",
        compiler_params=_params("arbitrary", "arbitrary"),
    )(xr, xr, xr, xr, xr, xr, h0, wbd, bias, lam, cw, cb)


def _route(s, sb):
    tm = s.shape[1]
    neg = -jnp.inf
    sub = lax.broadcasted_iota(jnp.int32, (GROUP, tm), 0)
    blocks = [sb[gi * GROUP:(gi + 1) * GROUP, :] for gi in range(N_GROUPS)]
    gscore = []
    for blk in blocks:
        m1 = jnp.max(blk, axis=0, keepdims=True)
        first = jnp.min(jnp.where(blk == m1, sub, GROUP), axis=0, keepdims=True)
        m2 = jnp.max(jnp.where(sub == first, neg, blk), axis=0, keepdims=True)
        gscore.append(m1 + m2)
    masked = []
    for gi in range(N_GROUPS):
        rank = jnp.zeros((1, tm), F32)
        for gj in range(N_GROUPS):
            if gj == gi:
                continue
            ahead = (gscore[gj] >= gscore[gi]) if gj < gi else (gscore[gj] > gscore[gi])
            rank = rank + jnp.where(ahead, 1.0, 0.0)
        masked.append(jnp.where(rank < TOPK_GROUPS, blocks[gi], neg))
    v = jnp.concatenate(masked, axis=0)
    eid = lax.broadcasted_iota(jnp.int32, (N_EXPERTS, tm), 0)
    sel = jnp.zeros((N_EXPERTS, tm), F32)
    for _ in range(TOP_K):
        mx = jnp.max(v, axis=0, keepdims=True)
        idx = jnp.min(jnp.where(v == mx, eid, N_EXPERTS), axis=0, keepdims=True)
        pick = eid == idx
        sel = jnp.where(pick, 1.0, sel)
        v = jnp.where(pick, neg, v)
    ws = s * sel
    return ws / jnp.sum(ws, axis=0, keepdims=True) * ROUTED_SCALE


def _mix_out_kernel(hmf_ref, hmb_ref, o_ref, hrf_ref, hrb_ref, gr_ref, x_ref, pos_ref, mod_ref,
                    mln_ref, rgn_ref, wo_ml_ref, wo_rg_ref, n2_ref, rwt_ref, rb_ref,
                    x1_ref, hn2_ref, w_ref):
    hm = hmf_ref[...] + hmb_ref[...]
    heads = []
    for h in range(N_HEADS):
        seg = hm[:, h * HEAD:(h + 1) * HEAD]
        heads.append(seg * lax.rsqrt(jnp.mean(seg * seg, axis=-1, keepdims=True) + EPS))
    y_ml = jnp.concatenate(heads, axis=1) * mln_ref[...] * jax.nn.sigmoid(o_ref[...])
    y_rg = _rms(hrf_ref[...] + hrb_ref[...], rgn_ref[...]) * jax.nn.gelu(gr_ref[...])
    mix = (jnp.dot(y_ml.astype(BF16), wo_ml_ref[...], preferred_element_type=F32)
           + jnp.dot(y_rg.astype(BF16), wo_rg_ref[...], preferred_element_type=F32))
    x1 = x_ref[...] + pos_ref[...] + mod_ref[0, 2:3, :] * mix
    x1_ref[...] = x1
    hn2 = _rms(x1, n2_ref[...]) * (1.0 + mod_ref[0, 4:5, :]) + mod_ref[0, 3:4, :]
    hn2_ref[...] = hn2.astype(BF16)
    logits_t = lax.dot_general(rwt_ref[...], hn2, (((1,), (1,)), ((), ())), precision=HIGHEST,
                               preferred_element_type=F32)
    s = jax.nn.sigmoid(logits_t)
    w_ref[...] = _route(s, s + rb_ref[...]).T


def _mix_out(hmf, hmb, o, hrf, hrb, gr, x2d, pos, mod, t, tm, mln, rgn, wo_ml, wo_rg, norm2, rwt, rbias):
    n = x2d.shape[0]
    npos = pos.shape[0] // tm
    tok = lambda i: (i, 0)
    const = lambda i: (0, 0)
    return pl.pallas_call(
        _mix_out_kernel,
        grid=(n // tm,),
        in_specs=[pl.BlockSpec((tm, D_ML), tok)] * 6 + [
            pl.BlockSpec((tm, D_MODEL), tok),
            pl.BlockSpec((tm, D_MODEL), lambda i: (i % npos, 0)),
            pl.BlockSpec((1, N_MOD, D_MODEL), _mod_index(mod.shape[0], tm, t)),
            pl.BlockSpec((1, D_ML), const), pl.BlockSpec((1, D_RG), const),
            pl.BlockSpec((D_ML, D_MODEL), const), pl.BlockSpec((D_RG, D_MODEL), const),
            pl.BlockSpec((1, D_MODEL), const),
            pl.BlockSpec((N_EXPERTS, D_MODEL), const), pl.BlockSpec((N_EXPERTS, 1), const),
        ],
        out_specs=[pl.BlockSpec((tm, D_MODEL), tok), pl.BlockSpec((tm, D_MODEL), tok),
                   pl.BlockSpec((tm, N_EXPERTS), tok)],
        out_shape=[jax.ShapeDtypeStruct((n, D_MODEL), F32), jax.ShapeDtypeStruct((n, D_MODEL), BF16),
                   jax.ShapeDtypeStruct((n, N_EXPERTS), F32)],
        name="mix_out",
        compiler_params=_params("arbitrary"),
    )(hmf, hmb, o, hrf, hrb, gr, x2d, pos, mod, mln, rgn, wo_ml, wo_rg, norm2, rwt, rbias)


def _swiglu(x, w13):
    h = jnp.dot(x, w13, preferred_element_type=F32)
    return _silu(h[:, 0:D_EXPERT]) * h[:, D_EXPERT:2 * D_EXPERT]


def _moe_kernel(hn2_ref, w_ref, w13_ref, w2_ref, sw13_ref, sw2_ref, x1_ref, mod_ref, nf_ref, y_ref, acc):
    step = pl.program_id(1)
    tm = hn2_ref.shape[0]
    rb = tm // MOE_ROW_SPLIT
    w2cat = w2_ref[...].reshape(MOE_EXPERTS_PER_STEP * D_EXPERT, D_MODEL)

    @pl.when(step == 0)
    def _():
        for r in range(MOE_ROW_SPLIT):
            rows = slice(r * rb, (r + 1) * rb)
            acc[rows, :] = jnp.dot(_swiglu(hn2_ref[rows, :], sw13_ref[...]).astype(BF16), sw2_ref[...],
                                   preferred_element_type=F32)

    for r in range(MOE_ROW_SPLIT):
        rows = slice(r * rb, (r + 1) * rb)
        x = hn2_ref[rows, :]
        w = w_ref[rows, :]
        lane = lax.broadcasted_iota(jnp.int32, w.shape, 1)
        hs = []
        for j in range(MOE_EXPERTS_PER_STEP):
            e = step * MOE_EXPERTS_PER_STEP + j
            wcol = jnp.sum(jnp.where(lane == e, w, 0.0), axis=1, keepdims=True)
            hs.append((_swiglu(x, w13_ref[j]) * wcol).astype(BF16))
        acc[rows, :] += jnp.dot(jnp.concatenate(hs, axis=1), w2cat, preferred_element_type=F32)

    @pl.when(step == N_EXPERTS // MOE_EXPERTS_PER_STEP - 1)
    def _():
        x2 = x1_ref[...] + mod_ref[0, 5:6, :] * acc[...]
        y_ref[...] = _rms(x2, nf_ref[...])


def _moe(hn2, wroute, w13, w2, sw13, sw2, x1, mod, t, tm, norm_final):
    n = hn2.shape[0]
    tok = lambda i, e: (i, 0)
    const = lambda i, e: (0, 0)
    mod_idx = _mod_index(mod.shape[0], tm, t)
    eps = MOE_EXPERTS_PER_STEP
    return pl.pallas_call(
        _moe_kernel,
        grid=(n // tm, N_EXPERTS // eps),
        in_specs=[
            pl.BlockSpec((tm, D_MODEL), tok),
            pl.BlockSpec((tm, N_EXPERTS), tok),
            pl.BlockSpec((eps, D_MODEL, 2 * D_EXPERT), lambda i, e: (e, 0, 0)),
            pl.BlockSpec((eps, D_EXPERT, D_MODEL), lambda i, e: (e, 0, 0)),
            pl.BlockSpec((D_MODEL, 2 * D_EXPERT), const),
            pl.BlockSpec((D_EXPERT, D_MODEL), const),
            pl.BlockSpec((tm, D_MODEL), tok),
            pl.BlockSpec((1, N_MOD, D_MODEL), lambda i, e: mod_idx(i)),
            pl.BlockSpec((1, D_MODEL), const),
        ],
        out_specs=pl.BlockSpec((tm, D_MODEL), tok),
        out_shape=jax.ShapeDtypeStruct((n, D_MODEL), F32),
        scratch_shapes=[pltpu.VMEM((tm, D_MODEL), F32)],
        name="moe",
        compiler_params=_params("arbitrary", "arbitrary"),
    )(hn2, wroute, w13, w2, sw13, sw2, x1, mod, norm_final)


def _grid_pos_embed(n_tokens, dim):
    rows_n = n_tokens // GRID_W
    row = jnp.repeat(jnp.arange(rows_n), GRID_W).astype(F32)
    col = jnp.tile(jnp.arange(GRID_W), rows_n).astype(F32)
    quarter = dim // 4
    omega = 1.0 / (POS_BASE ** (jnp.arange(quarter, dtype=F32) / quarter))
    ra = row[:, None] * omega
    ca = col[:, None] * omega
    return jnp.concatenate([jnp.sin(ra), jnp.cos(ra), jnp.sin(ca), jnp.cos(ca)], axis=-1)


def _block_diag(w):
    eye = jnp.eye(N_RG_BLOCKS, dtype=w.dtype)
    return jnp.einsum('nij,nm->nimj', w, eye).reshape(D_RG, D_RG)


def _layer_weights(l, norm1, w_in, mlstm_gate_bias, mlstm_norm, rg_conv_w, rg_conv_b, rg_wa, rg_ba, rg_wx,
                   rg_bx, rg_lambda, rg_norm, w_out, norm2, router_w, router_bias, exp_w1, exp_w3, exp_w2,
                   shared_w1, shared_w3, shared_w2):
    wi = w_in[l]
    c0, c1, c2 = 4 * D_ML, 4 * D_ML + N_GATE, 4 * D_ML + N_GATE + D_RG
    wg = wi[:, c0:c1]
    wg_hi = wg.astype(BF16)
    wg_lo = (wg - wg_hi.astype(F32)).astype(BF16)
    zcols = lambda w: jnp.zeros((D_MODEL, w), BF16)
    return dict(
        norm1=norm1[l].reshape(1, D_MODEL),
        wq=wi[:, :c0].astype(BF16),
        wr=jnp.concatenate([wi[:, c1:].astype(BF16), wg_hi, wg_lo, zcols(LANES - 2 * N_GATE)], axis=1),
        wgh=jnp.concatenate([wg_hi, zcols(LANES - N_GATE)], axis=1),
        gbias=jnp.pad(mlstm_gate_bias[l].reshape(1, N_GATE), ((0, 0), (0, LANES - N_GATE))),
        mln=mlstm_norm[l].reshape(1, D_ML),
        cw=rg_conv_w[l], cb=rg_conv_b[l].reshape(1, D_RG),
        wbd=jnp.concatenate([_block_diag(rg_wa[l, 0]), _block_diag(rg_wx[l, 0]),
                             _block_diag(rg_wa[l, 1]), _block_diag(rg_wx[l, 1])], axis=1).astype(BF16),
        rbias=jnp.concatenate([rg_ba[l, 0], rg_bx[l, 0], rg_ba[l, 1], rg_bx[l, 1]]).reshape(1, 4 * D_RG),
        lam=rg_lambda[l], rgn=rg_norm[l].reshape(1, D_RG),
        wo_ml=w_out[l, :D_ML].astype(BF16), wo_rg=w_out[l, D_ML:].astype(BF16),
        norm2=norm2[l].reshape(1, D_MODEL),
        rwt=router_w[l].T, rb=router_bias[l].reshape(N_EXPERTS, 1),
        w13=jnp.concatenate([exp_w1[l], exp_w3[l]], axis=-1).astype(BF16),
        w2=exp_w2[l].astype(BF16),
        sw13=jnp.concatenate([shared_w1[l], shared_w3[l]], axis=-1).astype(BF16),
        sw2=shared_w2[l].astype(BF16),
    )


def _trunk_layer(x2d, pos, mod, c0, n0, m0, h0, lw, b, t, emit_state, norm_final):
    tm = min(512, t) if mod.shape[0] > 1 else 512
    q, k, v, o, xr, gr, g, gt = _in_proj(x2d, pos, mod, t, tm, lw['norm1'], lw['wq'], lw['wr'], lw['wgh'],
                                         lw['gbias'])
    ml = _mlstm(q, k, v, g, gt, c0, n0, m0, b, t, min(256, t), emit_state)
    rg = _rglru(xr, h0, lw['wbd'], lw['rbias'], lw['lam'], lw['cw'], lw['cb'], b, t, min(512, t), emit_state)
    x1, hn2, wroute = _mix_out(ml[0], ml[1], o, rg[0], rg[1], gr, x2d, pos, mod, t, tm, lw['mln'], lw['rgn'],
                               lw['wo_ml'], lw['wo_rg'], lw['norm2'], lw['rwt'], lw['rb'])
    tm_moe = min(1024, t) if mod.shape[0] > 1 else 1024
    y = _moe(hn2, wroute, lw['w13'], lw['w2'], lw['sw13'], lw['sw2'], x1, mod, t, tm_moe, norm_final)
    return y, ml[2:], rg[2:]


def kernel(x_prompt, x_sample, c, state_mlstm_C, state_mlstm_n, state_mlstm_m, state_rglru_h, c_ctx, w_ada, b_ada, norm1, w_in, mlstm_gate_bias, mlstm_norm, rg_conv_w, rg_conv_b, rg_wa, rg_ba, rg_wx, rg_bx, rg_lambda, rg_norm, w_out, norm2, router_w, router_bias, exp_w1, exp_w3, exp_w2, shared_w1, shared_w3, shared_w2, norm_final):
    bp, tp, _ = x_prompt.shape
    bs, ts, _ = x_sample.shape
    depth = w_ada.shape[0]
    assert depth == 1, "the final norm is fused into the single layer's MoE kernel"
    nd = 2 * N_HEADS
    l = 0
    lw = _layer_weights(l, norm1, w_in, mlstm_gate_bias, mlstm_norm, rg_conv_w, rg_conv_b, rg_wa, rg_ba, rg_wx,
                        rg_bx, rg_lambda, rg_norm, w_out, norm2, router_w, router_bias, exp_w1, exp_w3, exp_w2,
                        shared_w1, shared_w3, shared_w2)
    nf = norm_final.reshape(1, D_MODEL)
    cvecs = jnp.concatenate([c_ctx[None], c, jnp.zeros((SUBLANES - 1 - bs, D_MODEL), F32)], axis=0)
    mod = _ada(cvecs, w_ada[l], b_ada[l]).reshape(SUBLANES, N_MOD, D_MODEL)

    yp, (cc, nc_, mc), (hc,) = _trunk_layer(
        x_prompt.reshape(bp * tp, D_MODEL), jnp.zeros((512, D_MODEL), F32), mod[0:1],
        jnp.zeros((1, nd, HEAD, HEAD), F32), jnp.zeros((1, nd, HEAD), F32), jnp.zeros((1, nd, HEAD), F32),
        jnp.zeros((1, 2, D_RG), F32), lw, bp, tp, True, nf)
    ys, _, _ = _trunk_layer(
        x_sample.reshape(bs * ts, D_MODEL), _grid_pos_embed(ts, D_MODEL), mod[1:1 + bs],
        state_mlstm_C[:, l].reshape(bs, nd, HEAD, HEAD), state_mlstm_n[:, l].reshape(bs, nd, HEAD),
        jnp.broadcast_to(state_mlstm_m[:, l].reshape(bs, nd, 1), (bs, nd, HEAD)),
        state_rglru_h[:, l], lw, bs, ts, False, nf)

    y_prompt = yp.reshape(bp, tp, D_MODEL)
    y_sample = ys.reshape(bs, ts, D_MODEL)
    new_c = cc.reshape(bp, 1, 2, N_HEADS, HEAD, HEAD)
    new_n = nc_.reshape(bp, 1, 2, N_HEADS, HEAD)
    new_m = mc[:, :, 0].reshape(bp, 1, 2, N_HEADS)
    new_h = hc.reshape(bp, 1, 2, D_RG)
    return (y_prompt, y_sample, new_c, new_n, new_m, new_h)
```

```python
import functools

import jax
import jax.numpy as jnp
from jax import lax
from jax.experimental import pallas as pl
from jax.experimental.pallas import tpu as pltpu
from jax.experimental.pallas import tpu_sc as plsc

F32 = jnp.float32
BF16 = jnp.bfloat16
I32 = jnp.int32
HIGHEST = lax.Precision.HIGHEST

D_MODEL = 1024
N_MOD = 6
D_ML = 512
N_HEADS = 4
HEAD = 128
D_RG = 512
N_RG_BLOCKS = 8
RG_BLOCK = 64
RG_C = 8.0
N_GATE = 16
N_EXPERTS = 64
N_GROUPS = 8
GROUP = 8
TOPK_GROUPS = 4
TOP_K = 8
D_EXPERT = 256
ROUTED_SCALE = 2.5
EPS = 1e-6
GRID_W = 64
POS_BASE = 10000.0

RG_SEG_PAD = 8
SC_WINDOW = 128
D_PACK = D_MODEL // 2

SUBLANES = 8
LANES = 128
VMEM_LIMIT = 48 * 1024 * 1024


def _params(*sem):
    return pltpu.CompilerParams(dimension_semantics=sem, vmem_limit_bytes=VMEM_LIMIT)


def _tiles(t, per_sequence_mod):
    cap = t if per_sequence_mod else 1 << 30
    return dict(
        tok=min(512, cap),
        chunk=min(256, t),
        scan=min(512, t),
        expert_rows=256 if t <= 256 else 512,
    )


def _silu(x):
    return x * jax.nn.sigmoid(x)


def _softplus(x):
    return jnp.maximum(x, 0.0) + jnp.log1p(jnp.exp(-jnp.abs(x)))


def _rms(x, g):
    return x * lax.rsqrt(jnp.mean(x * x, axis=-1, keepdims=True) + EPS) * g


def _pack_bf16_pairs(x):
    w = x.shape[1] // 2
    hi = lax.bitcast_convert_type(x[:, :w].astype(BF16).astype(F32), I32)
    lo = lax.bitcast_convert_type(x[:, w:].astype(BF16).astype(F32), I32)
    return hi | lax.shift_right_logical(lo, jnp.full(lo.shape, 16, I32))


def _unpack_bf16_pairs(p):
    hi = lax.bitcast_convert_type(p & jnp.int32(-65536), F32)
    lo = lax.bitcast_convert_type(lax.shift_left(p, jnp.full(p.shape, 16, I32)), F32)
    return hi, lo


def _ada_kernel(c_ref, w_ref, b_ref, o_ref):
    s = _silu(c_ref[...])
    o_ref[...] = jnp.dot(s, w_ref[...], precision=HIGHEST, preferred_element_type=F32) + b_ref[...]


def _ada(cvecs, w_ada, b_ada):
    n_out = w_ada.shape[1]
    tn = 1536
    return pl.pallas_call(
        _ada_kernel,
        grid=(n_out // tn,),
        in_specs=[
            pl.BlockSpec((SUBLANES, D_MODEL), lambda j: (0, 0)),
            pl.BlockSpec((D_MODEL, tn), lambda j: (0, j)),
            pl.BlockSpec((1, tn), lambda j: (0, j)),
        ],
        out_specs=pl.BlockSpec((SUBLANES, tn), lambda j: (0, j)),
        out_shape=jax.ShapeDtypeStruct((SUBLANES, n_out), F32),
        name="ada",
        compiler_params=_params("arbitrary"),
    )(cvecs, w_ada, b_ada.reshape(1, n_out))


def _in_proj_kernel(x_ref, pos_ref, mod_ref, n1_ref, wq_ref, wr_ref, wgh_ref, gb_ref,
                    q_ref, k_ref, v_ref, o_ref, xr_ref, gr_ref, g_ref, gt_ref):
    x = x_ref[...] + pos_ref[...]
    hn = _rms(x, n1_ref[...]) * (1.0 + mod_ref[0, 1:2, :]) + mod_ref[0, 0:1, :]
    hb = hn.astype(BF16)
    z = jnp.dot(hb, wq_ref[...], preferred_element_type=F32)
    q_ref[...] = z[:, 0:D_ML]
    k_ref[...] = z[:, D_ML:2 * D_ML] * (HEAD ** -0.5)
    v_ref[...] = z[:, 2 * D_ML:3 * D_ML]
    o_ref[...] = z[:, 3 * D_ML:4 * D_ML]
    zr = jnp.dot(hb, wr_ref[...], preferred_element_type=F32)
    xr_ref[...] = zr[:, 0:D_RG]
    gr_ref[...] = zr[:, D_RG:2 * D_RG]
    zg = zr[:, 2 * D_RG:2 * D_RG + LANES]
    h_lo = (hn - hb.astype(F32)).astype(BF16)
    g = (zg + pltpu.roll(zg, LANES - N_GATE, 1)
         + jnp.dot(h_lo, wgh_ref[...], preferred_element_type=F32) + gb_ref[...])
    col = lax.broadcasted_iota(I32, g.shape, 1)
    g = jnp.where((col & 4) != 0, -_softplus(-g), g)
    g_ref[...] = g[:, 0:N_GATE]
    gt_ref[...] = g.T[0:N_GATE, :]


def _mod_index(bm, tm, t):
    if bm > 1:
        return lambda i: ((i * tm) // t, 0, 0)
    return lambda i: (0, 0, 0)


def _in_proj(x2d, pos, mod, t, tm, norm1, wq, wr, wgh, gbias):
    n = x2d.shape[0]
    npos = pos.shape[0] // tm
    tok = lambda i: (i, 0)
    const = lambda i: (0, 0)
    f = lambda w: jax.ShapeDtypeStruct((n, w), F32)
    return pl.pallas_call(
        _in_proj_kernel,
        grid=(n // tm,),
        in_specs=[
            pl.BlockSpec((tm, D_MODEL), tok),
            pl.BlockSpec((tm, D_MODEL), lambda i: (i % npos, 0)),
            pl.BlockSpec((1, N_MOD, D_MODEL), _mod_index(mod.shape[0], tm, t)),
            pl.BlockSpec((1, D_MODEL), const),
            pl.BlockSpec((D_MODEL, 4 * D_ML), const),
            pl.BlockSpec((D_MODEL, 2 * D_RG + LANES), const),
            pl.BlockSpec((D_MODEL, LANES), const),
            pl.BlockSpec((1, LANES), const),
        ],
        out_specs=[pl.BlockSpec((tm, D_ML), tok)] * 4 + [pl.BlockSpec((tm, D_RG), tok)] * 2
        + [pl.BlockSpec((tm, N_GATE), tok), pl.BlockSpec((N_GATE, tm), lambda i: (0, i))],
        out_shape=[f(D_ML)] * 4 + [f(D_RG)] * 2 + [f(N_GATE), jax.ShapeDtypeStruct((N_GATE, n), F32)],
        name="in_proj",
        compiler_params=_params("arbitrary"),
    )(x2d, pos, mod, norm1, wq, wr, wgh, gbias)


def _mlstm_kernel(chunk, nc, emit_state, *refs):
    (qf, kf, vf, gf, gtf, qb, kb, vb, gb, gtb, c0_ref, n0_ref, m0_ref) = refs[:13]
    if emit_state:
        hf_ref, hb_ref, c_out, n_out, m_out, c_sc, n_sc, m_sc = refs[13:]
    else:
        hf_ref, hb_ref, c_sc, n_sc, m_sc = refs[13:]
    i = pl.program_id(1)

    @pl.when(i == 0)
    def _():
        c_sc[...] = c0_ref[0]
        n_sc[...] = n0_ref[0]
        m_sc[...] = m0_ref[0]

    row = lax.broadcasted_iota(I32, (chunk, chunk), 0)
    col = lax.broadcasted_iota(I32, (chunk, chunk), 1)
    hd = []
    for d, (q_ref, k_ref, v_ref, g_ref, gt_ref, h_ref) in enumerate(
            ((qf, kf, vf, gf, gtf, hf_ref), (qb, kb, vb, gb, gtb, hb_ref))):
        tri = (col <= row) if d == 0 else (col >= row)
        trif = tri.astype(F32)
        g = g_ref[...]
        gt = gt_ref[...]
        bcol = jnp.dot(trif, g, precision=HIGHEST, preferred_element_type=F32)
        brow = lax.dot_general(gt, trif, (((1,), (1,)), ((), ())), precision=HIGHEST,
                               preferred_element_type=F32)
        blast = bcol[chunk - 1:chunk, :] if d == 0 else bcol[0:1, :]
        for h in range(N_HEADS):
            ci = d * 8 + h
            cf = d * 8 + 4 + h
            j = d * N_HEADS + h
            hd.append(dict(
                j=j, sl=slice(h * HEAD, (h + 1) * HEAD), tri=tri, q_ref=q_ref, k_ref=k_ref, v_ref=v_ref,
                h_ref=h_ref, b_col=bcol[:, cf:cf + 1], b_row=brow[cf:cf + 1, :], ig_col=g[:, ci:ci + 1],
                ig_row=gt[ci:ci + 1, :], b_last=blast[:, cf:cf + 1], m_prev=m_sc[j:j + 1, 0:1],
                c_prev=c_sc[j], n_prev=n_sc[j:j + 1, :]))
    for x in hd:
        log_d = jnp.where(x['tri'], x['b_col'] - x['b_row'] + x['ig_row'], -jnp.inf)
        inter = x['b_col'] + x['m_prev']
        m_t = jnp.maximum(inter, jnp.max(log_d, axis=1, keepdims=True))
        x['dm'] = jnp.exp(log_d - m_t)
        x['w_inter'] = jnp.exp(inter - m_t)
        x['floor'] = jnp.exp(-m_t)
    for x in hd:
        x['q'] = x['q_ref'][:, x['sl']]
        x['k'] = x['k_ref'][:, x['sl']]
        x['q16'] = x['q'].astype(BF16)
        x['v16'] = x['v_ref'][:, x['sl']].astype(BF16)
        x['s'] = lax.dot_general(x['q16'], x['k'].astype(BF16), (((1,), (1,)), ((), ())),
                                 preferred_element_type=F32) * x['dm']
    for x in hd:
        num = (x['w_inter'] * jnp.dot(x['q16'], x['c_prev'].astype(BF16), preferred_element_type=F32)
               + jnp.dot(x['s'].astype(BF16), x['v16'], preferred_element_type=F32))
        den = (x['w_inter'] * jnp.sum(x['q'] * x['n_prev'], axis=1, keepdims=True)
               + jnp.sum(x['s'], axis=1, keepdims=True))
        x['h_ref'][:, x['sl']] = num / jnp.maximum(jnp.abs(den), x['floor'])
    for x in hd:
        j = x['j']
        log_w_col = x['b_last'] - x['b_col'] + x['ig_col']
        log_w_row = x['b_last'] - x['b_row'] + x['ig_row']
        m_new = jnp.maximum(x['b_last'] + x['m_prev'], jnp.max(log_w_row, axis=1, keepdims=True))
        decay = jnp.exp(x['b_last'] + x['m_prev'] - m_new)
        kw = x['k'] * jnp.exp(log_w_col - m_new)
        c_sc[j] = decay * x['c_prev'] + lax.dot_general(
            kw.astype(BF16), x['v16'], (((0,), (0,)), ((), ())), preferred_element_type=F32)
        n_sc[j:j + 1, :] = decay * x['n_prev'] + jnp.sum(kw, axis=0, keepdims=True)
        m_sc[j:j + 1, :] = jnp.broadcast_to(m_new, (1, HEAD))

    if emit_state:
        @pl.when(i == nc - 1)
        def _():
            c_out[0] = c_sc[...]
            n_out[0] = n_sc[...]
            m_out[0] = m_sc[...]


def _state_index(bm):
    if bm > 1:
        return lambda b, i: (b,) + (0,) * 3, lambda b, i: (b, 0, 0)
    return lambda b, i: (0,) * 4, lambda b, i: (0, 0, 0)


def _mlstm(q, k, v, g, gt, c0, n0, m0, b, t, chunk, emit_state):
    nc = t // chunk
    n = b * t
    nd = 2 * N_HEADS
    fwd = lambda bi, i: (bi * nc + i, 0)
    bwd = lambda bi, i: (bi * nc + nc - 1 - i, 0)
    fwd_t = lambda bi, i: (0, bi * nc + i)
    bwd_t = lambda bi, i: (0, bi * nc + nc - 1 - i)
    c_idx, n_idx = _state_index(c0.shape[0])
    tok = lambda m: [pl.BlockSpec((chunk, D_ML), m)] * 3 + [pl.BlockSpec((chunk, N_GATE), m)]
    in_specs = (tok(fwd) + [pl.BlockSpec((N_GATE, chunk), fwd_t)]
                + tok(bwd) + [pl.BlockSpec((N_GATE, chunk), bwd_t)]
                + [pl.BlockSpec((1, nd, HEAD, HEAD), c_idx),
                   pl.BlockSpec((1, nd, HEAD), n_idx), pl.BlockSpec((1, nd, HEAD), n_idx)])
    out_specs = [pl.BlockSpec((chunk, D_ML), fwd), pl.BlockSpec((chunk, D_ML), bwd)]
    out_shape = [jax.ShapeDtypeStruct((n, D_ML), F32)] * 2
    if emit_state:
        out_specs += [pl.BlockSpec((1, nd, HEAD, HEAD), lambda bi, i: (bi, 0, 0, 0)),
                      pl.BlockSpec((1, nd, HEAD), lambda bi, i: (bi, 0, 0)),
                      pl.BlockSpec((1, nd, HEAD), lambda bi, i: (bi, 0, 0))]
        out_shape += [jax.ShapeDtypeStruct((b, nd, HEAD, HEAD), F32),
                      jax.ShapeDtypeStruct((b, nd, HEAD), F32),
                      jax.ShapeDtypeStruct((b, nd, HEAD), F32)]
    return pl.pallas_call(
        functools.partial(_mlstm_kernel, chunk, nc, emit_state),
        grid=(b, nc),
        in_specs=in_specs,
        out_specs=out_specs,
        out_shape=out_shape,
        scratch_shapes=[pltpu.VMEM((nd, HEAD, HEAD), F32), pltpu.VMEM((nd, HEAD), F32),
                        pltpu.VMEM((nd, HEAD), F32)],
        name="mlstm",
        compiler_params=_params("arbitrary", "arbitrary"),
    )(q, k, v, g, gt, q, k, v, g, gt, c0, n0, m0)


def _neg_expm1(x):
    u = jnp.exp(x)
    near = jnp.where(u == 1.0, x, (u - 1.0) * x / jnp.log(u))
    return -jnp.where(x < -0.5, u - 1.0, near)


def _rglru_kernel(tb, nb, emit_state, *refs):
    (xf, xf_prev, xf_next, xb, xb_prev, xb_next, h0_ref, w_ref, bias_ref, lam_ref, cw_ref, cb_ref) = refs[:12]
    n_out = 3 if emit_state else 2
    hf_ref, hb_ref = refs[12:14]
    hfin_ref = refs[14] if emit_state else None
    carry, af_sc, uf_sc, ab_sc, ub_sc, hf_sc, pf_sc, hb_sc, pb_sc = refs[12 + n_out:]
    i = pl.program_id(1)

    @pl.when(i == 0)
    def _():
        carry[...] = h0_ref[0]

    row = lax.broadcasted_iota(I32, (tb, D_RG), 0)
    cw = cw_ref[...]
    softplus_neg_lam = _softplus(-lam_ref[...])

    def conv(main_ref, prev_ref, next_ref, first, last):
        main = main_ref[...]
        prev = jnp.where(first, 0.0, prev_ref[...])
        nxt = jnp.where(last, 0.0, next_ref[...])
        xm2 = jnp.where(row == 0, prev[6:7, :], jnp.where(row == 1, prev[7:8, :], pltpu.roll(main, 2, 0)))
        xm1 = jnp.where(row == 0, prev[7:8, :], pltpu.roll(main, 1, 0))
        xp1 = jnp.where(row == tb - 1, nxt[0:1, :], pltpu.roll(main, tb - 1, 0))
        return cb_ref[...] + xm2 * cw[0:1, :] + xm1 * cw[1:2, :] + main * cw[2:3, :] + xp1 * cw[3:4, :]

    def recurrence_terms(xc, d):
        z = jnp.dot(xc.astype(BF16), w_ref[:, d * 2 * D_RG:(d + 1) * 2 * D_RG],
                    preferred_element_type=F32) + bias_ref[:, d * 2 * D_RG:(d + 1) * 2 * D_RG]
        r = jax.nn.sigmoid(z[:, 0:D_RG])
        ig = jax.nn.sigmoid(z[:, D_RG:2 * D_RG])
        log_a = -RG_C * r * softplus_neg_lam[d:d + 1, :]
        a = jnp.exp(log_a)
        u = jnp.sqrt(_neg_expm1(2.0 * log_a)) * (ig * xc)
        return a, u

    a_f, u_f = recurrence_terms(conv(xf, xf_prev, xf_next, i == 0, i == nb - 1), 0)
    a_b, u_b = recurrence_terms(conv(xb, xb_prev, xb_next, i == nb - 1, i == 0), 1)
    seg = tb // SUBLANES
    pitch = seg + RG_SEG_PAD
    ncol = D_RG // LANES
    for lc in range(ncol):
        lanes = slice(lc * LANES, (lc + 1) * LANES)
        for s in range(SUBLANES):
            src = slice(s * seg, (s + 1) * seg)
            dst = slice(s * pitch, s * pitch + seg)
            af_sc[lc, dst, :], uf_sc[lc, dst, :] = a_f[src, lanes], u_f[src, lanes]
            ab_sc[lc, dst, :], ub_sc[lc, dst, :] = a_b[src, lanes], u_b[src, lanes]
    slab = lambda k: (slice(None), pl.ds(k, SUBLANES, stride=pitch), slice(None))
    hf = jnp.zeros((ncol, SUBLANES, LANES), F32)
    hb = jnp.zeros((ncol, SUBLANES, LANES), F32)
    pf = jnp.ones((ncol, SUBLANES, LANES), F32)
    pb = jnp.ones((ncol, SUBLANES, LANES), F32)
    for k in range(seg):
        kb = seg - 1 - k
        ak = af_sc[slab(k)]
        hf = ak * hf + uf_sc[slab(k)]
        pf = pf * ak
        hf_sc[slab(k)] = hf
        pf_sc[slab(k)] = pf
        ak = ab_sc[slab(kb)]
        hb = ak * hb + ub_sc[slab(kb)]
        pb = pb * ak
        hb_sc[slab(kb)] = hb
        pb_sc[slab(kb)] = pb
    for lc in range(ncol):
        lanes = slice(lc * LANES, (lc + 1) * LANES)
        c = carry[0:1, lanes]
        cin_f = []
        for s in range(SUBLANES):
            cin_f.append(c)
            c = pf[lc, s:s + 1, :] * c + hf[lc, s:s + 1, :]
        carry[0:1, lanes] = c
        c = carry[1:2, lanes]
        cin_b = [None] * SUBLANES
        for s in reversed(range(SUBLANES)):
            cin_b[s] = c
            c = pb[lc, s:s + 1, :] * c + hb[lc, s:s + 1, :]
        carry[1:2, lanes] = c
        for s in range(SUBLANES):
            rows = slice(s * seg, (s + 1) * seg)
            src = slice(s * pitch, s * pitch + seg)
            hf_ref[rows, lanes] = hf_sc[lc, src, :] + pf_sc[lc, src, :] * cin_f[s]
            hb_ref[rows, lanes] = hb_sc[lc, src, :] + pb_sc[lc, src, :] * cin_b[s]

    if emit_state:
        @pl.when(i == nb - 1)
        def _():
            hfin_ref[0] = carry[...]


def _rglru(xr, h0, wbd, bias, lam, cw, cb, b, t, tb, emit_state):
    nb = t // tb
    n = b * t
    r8 = tb // SUBLANES
    last8 = n // SUBLANES - 1
    fwd = lambda bi, i: (bi * nb + i, 0)
    bwd = lambda bi, i: (bi * nb + nb - 1 - i, 0)
    fwd_prev = lambda bi, i: (jnp.maximum((bi * nb + i) * r8 - 1, 0), 0)
    fwd_next = lambda bi, i: (jnp.minimum((bi * nb + i + 1) * r8, last8), 0)
    bwd_prev = lambda bi, i: (jnp.maximum((bi * nb + nb - 1 - i) * r8 - 1, 0), 0)
    bwd_next = lambda bi, i: (jnp.minimum((bi * nb + nb - i) * r8, last8), 0)
    const = lambda bi, i: (0, 0)
    h_idx = (lambda bi, i: (bi, 0, 0)) if h0.shape[0] > 1 else (lambda bi, i: (0, 0, 0))
    halo = lambda m: pl.BlockSpec((SUBLANES, D_RG), m)
    in_specs = [pl.BlockSpec((tb, D_RG), fwd), halo(fwd_prev), halo(fwd_next),
                pl.BlockSpec((tb, D_RG), bwd), halo(bwd_prev), halo(bwd_next),
                pl.BlockSpec((1, 2, D_RG), h_idx),
                pl.BlockSpec((D_RG, 4 * D_RG), const), pl.BlockSpec((1, 4 * D_RG), const),
                pl.BlockSpec((2, D_RG), const), pl.BlockSpec((4, D_RG), const), pl.BlockSpec((1, D_RG), const)]
    out_specs = [pl.BlockSpec((tb, D_RG), fwd), pl.BlockSpec((tb, D_RG), bwd)]
    out_shape = [jax.ShapeDtypeStruct((n, D_RG), F32)] * 2
    if emit_state:
        out_specs.append(pl.BlockSpec((1, 2, D_RG), lambda bi, i: (bi, 0, 0)))
        out_shape.append(jax.ShapeDtypeStruct((b, 2, D_RG), F32))
    return pl.pallas_call(
        functools.partial(_rglru_kernel, tb, nb, emit_state),
        grid=(b, nb),
        in_specs=in_specs,
        out_specs=out_specs,
        out_shape=out_shape,
        scratch_shapes=[pltpu.VMEM((2, D_RG), F32)]
        + [pltpu.VMEM((D_RG // LANES, tb + SUBLANES * RG_SEG_PAD, LANES), F32)] * 8,
        name="rglru",
        compiler_params=_params("arbitrary", "arbitrary"),
    )(xr, xr, xr, xr, xr, xr, h0, wbd, bias, lam, cw, cb)


def _route(s, sb):
    tm = s.shape[1]
    neg = -jnp.inf
    sub = lax.broadcasted_iota(I32, (GROUP, tm), 0)
    blocks = [sb[gi * GROUP:(gi + 1) * GROUP, :] for gi in range(N_GROUPS)]
    gscore = []
    for blk in blocks:
        m1 = jnp.max(blk, axis=0, keepdims=True)
        first = jnp.min(jnp.where(blk == m1, sub, GROUP), axis=0, keepdims=True)
        m2 = jnp.max(jnp.where(sub == first, neg, blk), axis=0, keepdims=True)
        gscore.append(m1 + m2)
    masked = []
    for gi in range(N_GROUPS):
        rank = jnp.zeros((1, tm), F32)
        for gj in range(N_GROUPS):
            if gj == gi:
                continue
            ahead = (gscore[gj] >= gscore[gi]) if gj < gi else (gscore[gj] > gscore[gi])
            rank = rank + jnp.where(ahead, 1.0, 0.0)
        masked.append(jnp.where(rank < TOPK_GROUPS, blocks[gi], neg))
    v = jnp.concatenate(masked, axis=0)
    eid = lax.broadcasted_iota(I32, (N_EXPERTS, tm), 0)
    sel = jnp.zeros((N_EXPERTS, tm), F32)
    picks = []
    for _ in range(TOP_K):
        mx = jnp.max(v, axis=0, keepdims=True)
        idx = jnp.min(jnp.where(v == mx, eid, N_EXPERTS), axis=0, keepdims=True)
        pick = eid == idx
        picks.append(pick)
        sel = jnp.where(pick, 1.0, sel)
        v = jnp.where(pick, neg, v)
    ws = s * sel
    return ws / jnp.sum(ws, axis=0, keepdims=True) * ROUTED_SCALE, sel, picks


def _mix_out_kernel(hmf_ref, hmb_ref, o_ref, hrf_ref, hrb_ref, gr_ref, x_ref, pos_ref, mod_ref,
                    mln_ref, rgn_ref, wo_ml_ref, wo_rg_ref, n2_ref, rwt_ref, rb_ref,
                    x1_ref, hn2p_ref, ek_ref, pk_ref, wtok_ref, cnt_ref, cnt_sc):
    i = pl.program_id(0)
    tm = x_ref.shape[0]

    @pl.when(i == 0)
    def _():
        cnt_sc[...] = jnp.zeros_like(cnt_sc)

    hm = hmf_ref[...] + hmb_ref[...]
    heads = []
    for h in range(N_HEADS):
        seg = hm[:, h * HEAD:(h + 1) * HEAD]
        heads.append(seg * lax.rsqrt(jnp.mean(seg * seg, axis=-1, keepdims=True) + EPS))
    y_ml = jnp.concatenate(heads, axis=1) * mln_ref[...] * jax.nn.sigmoid(o_ref[...])
    y_rg = _rms(hrf_ref[...] + hrb_ref[...], rgn_ref[...]) * jax.nn.gelu(gr_ref[...])
    mix = (jnp.dot(y_ml.astype(BF16), wo_ml_ref[...], preferred_element_type=F32)
           + jnp.dot(y_rg.astype(BF16), wo_rg_ref[...], preferred_element_type=F32))
    x1 = x_ref[...] + pos_ref[...] + mod_ref[0, 2:3, :] * mix
    x1_ref[...] = x1
    hn2 = _rms(x1, n2_ref[...]) * (1.0 + mod_ref[0, 4:5, :]) + mod_ref[0, 3:4, :]
    hn2p_ref[...] = _pack_bf16_pairs(hn2)
    logits_t = lax.dot_general(rwt_ref[...], hn2, (((1,), (1,)), ((), ())), precision=HIGHEST,
                               preferred_element_type=F32)
    s = jax.nn.sigmoid(logits_t)
    wt, sel, picks = _route(s, s + rb_ref[...])

    earlier = (lax.broadcasted_iota(I32, (tm, tm), 0) < lax.broadcasted_iota(I32, (tm, tm), 1))
    prefix = jnp.dot(sel.astype(BF16), earlier.astype(BF16), preferred_element_type=F32)
    pos_all = cnt_sc[:, 0:1] + prefix
    eid = lax.broadcasted_iota(I32, (N_EXPERTS, tm), 0)
    eid_f = eid.astype(F32)
    row8 = lax.broadcasted_iota(I32, (TOP_K, tm), 0)
    ek = jnp.zeros((TOP_K, tm), F32)
    pk = jnp.zeros((TOP_K, tm), F32)
    wk = jnp.zeros((N_EXPERTS, tm), F32)
    for k, pick in enumerate(picks):
        take = lambda a: jnp.sum(jnp.where(pick, a, 0.0), axis=0, keepdims=True)
        ek = jnp.where(row8 == k, take(eid_f), ek)
        pk = jnp.where(row8 == k, take(pos_all), pk)
        wk = jnp.where(eid == k, take(wt), wk)
    ek_ref[...] = ek.astype(I32)
    pk_ref[...] = pk.astype(I32)
    wtok_ref[...] = wk.T
    cnt_sc[...] += jnp.broadcast_to(jnp.sum(sel, axis=1, keepdims=True), cnt_sc.shape)

    @pl.when(i == pl.num_programs(0) - 1)
    def _():
        cnt_ref[...] = cnt_sc[...].astype(I32)


def _mix_out(hmf, hmb, o, hrf, hrb, gr, x2d, pos, mod, t, tm, mln, rgn, wo_ml, wo_rg, norm2, rwt, rbias):
    n = x2d.shape[0]
    npos = pos.shape[0] // tm
    tok = lambda i: (i, 0)
    tok_t = lambda i: (0, i)
    const = lambda i: (0, 0)
    return pl.pallas_call(
        _mix_out_kernel,
        grid=(n // tm,),
        in_specs=[pl.BlockSpec((tm, D_ML), tok)] * 6 + [
            pl.BlockSpec((tm, D_MODEL), tok),
            pl.BlockSpec((tm, D_MODEL), lambda i: (i % npos, 0)),
            pl.BlockSpec((1, N_MOD, D_MODEL), _mod_index(mod.shape[0], tm, t)),
            pl.BlockSpec((1, D_ML), const), pl.BlockSpec((1, D_RG), const),
            pl.BlockSpec((D_ML, D_MODEL), const), pl.BlockSpec((D_RG, D_MODEL), const),
            pl.BlockSpec((1, D_MODEL), const),
            pl.BlockSpec((N_EXPERTS, D_MODEL), const), pl.BlockSpec((N_EXPERTS, 1), const),
        ],
        out_specs=[pl.BlockSpec((tm, D_MODEL), tok), pl.BlockSpec((tm, D_PACK), tok),
                   pl.BlockSpec((TOP_K, tm), tok_t), pl.BlockSpec((TOP_K, tm), tok_t),
                   pl.BlockSpec((tm, N_EXPERTS), tok), pl.BlockSpec((N_EXPERTS, LANES), const)],
        out_shape=[jax.ShapeDtypeStruct((n, D_MODEL), F32), jax.ShapeDtypeStruct((n, D_PACK), I32),
                   jax.ShapeDtypeStruct((TOP_K, n), I32), jax.ShapeDtypeStruct((TOP_K, n), I32),
                   jax.ShapeDtypeStruct((n, N_EXPERTS), F32), jax.ShapeDtypeStruct((N_EXPERTS, LANES), I32)],
        scratch_shapes=[pltpu.VMEM((N_EXPERTS, LANES), F32)],
        name="mix_out",
        compiler_params=_params("arbitrary"),
    )(hmf, hmb, o, hrf, hrb, gr, x2d, pos, mod, mln, rgn, wo_ml, wo_rg, norm2, rwt, rbias)


def _sc_mesh():
    return plsc.VectorSubcoreMesh(core_axis_name="core", subcore_axis_name="subcore")


def _sc_worker():
    info = plsc.get_sparse_core_info()
    return lax.axis_index("subcore") * info.num_cores + lax.axis_index("core"), info.num_cores * info.num_subcores


def _sc_dispatch(xp, dest3, n_slots):
    n, w = xp.shape
    nwin = n // SC_WINDOW

    @pl.kernel(out_type=jax.ShapeDtypeStruct((n_slots, w), xp.dtype), mesh=_sc_mesh(),
               scratch_types=[pltpu.VMEM((SC_WINDOW, w), xp.dtype), pltpu.VMEM((TOP_K, SC_WINDOW), I32)],
               name="sc_dispatch")
    def k(x_hbm, i_hbm, o_hbm, x_v, i_v):
        wid, nworkers = _sc_worker()
        per = nwin // nworkers

        @pl.loop(0, per)
        def _(s):
            win = wid * per + s
            pltpu.sync_copy(x_hbm.at[pl.ds(win * SC_WINDOW, SC_WINDOW)], x_v)
            pltpu.sync_copy(i_hbm.at[win], i_v)
            for j in range(TOP_K):
                pltpu.sync_copy(x_v, o_hbm.at[i_v.at[j]])

    return k(xp, dest3)


def _sc_combine_gather(ys, dest3):
    nwin = dest3.shape[0]
    w = ys.shape[1]

    @pl.kernel(out_type=jax.ShapeDtypeStruct((nwin, TOP_K, SC_WINDOW, w), ys.dtype), mesh=_sc_mesh(),
               scratch_types=[pltpu.VMEM((SC_WINDOW, w), ys.dtype), pltpu.VMEM((TOP_K, SC_WINDOW), I32)],
               name="sc_combine")
    def k(y_hbm, i_hbm, o_hbm, y_v, i_v):
        wid, nworkers = _sc_worker()
        per = nwin // nworkers

        @pl.loop(0, per)
        def _(s):
            win = wid * per + s
            pltpu.sync_copy(i_hbm.at[win], i_v)
            for j in range(TOP_K):
                pltpu.sync_copy(y_hbm.at[i_v.at[j]], y_v)
                pltpu.sync_copy(y_v, o_hbm.at[win, j])

    return k(ys, dest3)


def _swiglu(x, w13):
    h = jnp.dot(x, w13, preferred_element_type=F32)
    return _silu(h[:, 0:D_EXPERT]) * h[:, D_EXPERT:2 * D_EXPERT]


def _unpack_rows_bf16(p):
    hi, lo = _unpack_bf16_pairs(p)
    return jnp.concatenate([hi.astype(BF16), lo.astype(BF16)], axis=1)


def _expert_kernel(be_ref, nu_ref, x_ref, w13_ref, w2_ref, y_ref):
    @pl.when(pl.program_id(0) < nu_ref[0])
    def _():
        rows = x_ref.shape[0] // 2
        for r in range(2):
            sl = slice(r * rows, (r + 1) * rows)
            h = _swiglu(_unpack_rows_bf16(x_ref[sl, :]), w13_ref[0])
            y_ref[sl, :] = _pack_bf16_pairs(jnp.dot(h.astype(BF16), w2_ref[0], preferred_element_type=F32))


def _experts(xs, block_expert, n_used, w13, w2, rows):
    nb = xs.shape[0] // rows
    return pl.pallas_call(
        _expert_kernel,
        grid_spec=pltpu.PrefetchScalarGridSpec(
            num_scalar_prefetch=2,
            grid=(nb,),
            in_specs=[
                pl.BlockSpec((rows, D_PACK), lambda b, be, nu: (b, 0)),
                pl.BlockSpec((1, D_MODEL, 2 * D_EXPERT), lambda b, be, nu: (be[b], 0, 0)),
                pl.BlockSpec((1, D_EXPERT, D_MODEL), lambda b, be, nu: (be[b], 0, 0)),
            ],
            out_specs=pl.BlockSpec((rows, D_PACK), lambda b, be, nu: (b, 0)),
        ),
        out_shape=jax.ShapeDtypeStruct(xs.shape, I32),
        name="experts",
        compiler_params=_params("arbitrary"),
    )(block_expert, n_used, xs, w13, w2)


def _moe_out_kernel(yk_ref, hn2p_ref, wtok_ref, sw13_ref, sw2_ref, x1_ref, mod_ref, nf_ref, y_ref):
    shared = jnp.dot(_swiglu(_unpack_rows_bf16(hn2p_ref[...]), sw13_ref[...]).astype(BF16), sw2_ref[...],
                     preferred_element_type=F32)
    w = wtok_ref[...]
    parts = []
    for wi in range(yk_ref.shape[0]):
        rows = slice(wi * SC_WINDOW, (wi + 1) * SC_WINDOW)
        a_hi = shared[rows, 0:D_PACK]
        a_lo = shared[rows, D_PACK:D_MODEL]
        for k in range(TOP_K):
            y_hi, y_lo = _unpack_bf16_pairs(yk_ref[wi, k])
            wc = w[rows, k:k + 1]
            a_hi = a_hi + wc * y_hi
            a_lo = a_lo + wc * y_lo
        parts.append(jnp.concatenate([a_hi, a_lo], axis=1))
    x2 = x1_ref[...] + mod_ref[0, 5:6, :] * jnp.concatenate(parts, axis=0)
    y_ref[...] = _rms(x2, nf_ref[...])


def _moe_out(yk, hn2p, wtok, sw13, sw2, x1, mod, t, tm, norm_final):
    n = hn2p.shape[0]
    tok = lambda i: (i, 0)
    const = lambda i: (0, 0)
    return pl.pallas_call(
        _moe_out_kernel,
        grid=(n // tm,),
        in_specs=[
            pl.BlockSpec((tm // SC_WINDOW, TOP_K, SC_WINDOW, D_PACK), lambda i: (i, 0, 0, 0)),
            pl.BlockSpec((tm, D_PACK), tok),
            pl.BlockSpec((tm, N_EXPERTS), tok),
            pl.BlockSpec((D_MODEL, 2 * D_EXPERT), const),
            pl.BlockSpec((D_EXPERT, D_MODEL), const),
            pl.BlockSpec((tm, D_MODEL), tok),
            pl.BlockSpec((1, N_MOD, D_MODEL), _mod_index(mod.shape[0], tm, t)),
            pl.BlockSpec((1, D_MODEL), const),
        ],
        out_specs=pl.BlockSpec((tm, D_MODEL), tok),
        out_shape=jax.ShapeDtypeStruct((n, D_MODEL), F32),
        name="moe_out",
        compiler_params=_params("arbitrary"),
    )(yk, hn2p, wtok, sw13, sw2, x1, mod, norm_final)


def _dispatch_plan(cnt, ek, pk, rows):
    n = ek.shape[1]
    nb = n * TOP_K // rows + N_EXPERTS
    padded = (cnt + rows - 1) // rows * rows
    ends = jnp.cumsum(padded)
    dest = jnp.take(ends - padded, ek) + pk
    dest3 = dest.reshape(TOP_K, n // SC_WINDOW, SC_WINDOW).transpose(1, 0, 2)
    block_end = ends // rows
    block_expert = jnp.minimum(jnp.searchsorted(block_end, jnp.arange(nb, dtype=I32), side='right'),
                               N_EXPERTS - 1).astype(I32)
    return dest3, block_expert, block_end[-1:].astype(I32), nb * rows


def _grid_pos_embed(n_tokens, dim):
    rows_n = n_tokens // GRID_W
    row = jnp.repeat(jnp.arange(rows_n), GRID_W).astype(F32)
    col = jnp.tile(jnp.arange(GRID_W), rows_n).astype(F32)
    quarter = dim // 4
    omega = 1.0 / (POS_BASE ** (jnp.arange(quarter, dtype=F32) / quarter))
    ra = row[:, None] * omega
    ca = col[:, None] * omega
    return jnp.concatenate([jnp.sin(ra), jnp.cos(ra), jnp.sin(ca), jnp.cos(ca)], axis=-1)


def _block_diag(w):
    eye = jnp.eye(N_RG_BLOCKS, dtype=w.dtype)
    return jnp.einsum('nij,nm->nimj', w, eye).reshape(D_RG, D_RG)


def _layer_weights(l, norm1, w_in, mlstm_gate_bias, mlstm_norm, rg_conv_w, rg_conv_b, rg_wa, rg_ba, rg_wx,
                   rg_bx, rg_lambda, rg_norm, w_out, norm2, router_w, router_bias, exp_w1, exp_w3, exp_w2,
                   shared_w1, shared_w3, shared_w2):
    wi = w_in[l]
    c0, c1 = 4 * D_ML, 4 * D_ML + N_GATE
    wg = wi[:, c0:c1]
    wg_hi = wg.astype(BF16)
    wg_lo = (wg - wg_hi.astype(F32)).astype(BF16)
    zcols = lambda w: jnp.zeros((D_MODEL, w), BF16)
    return dict(
        norm1=norm1[l].reshape(1, D_MODEL),
        wq=wi[:, :c0].astype(BF16),
        wr=jnp.concatenate([wi[:, c1:].astype(BF16), wg_hi, wg_lo, zcols(LANES - 2 * N_GATE)], axis=1),
        wgh=jnp.concatenate([wg_hi, zcols(LANES - N_GATE)], axis=1),
        gbias=jnp.pad(mlstm_gate_bias[l].reshape(1, N_GATE), ((0, 0), (0, LANES - N_GATE))),
        mln=mlstm_norm[l].reshape(1, D_ML),
        cw=rg_conv_w[l], cb=rg_conv_b[l].reshape(1, D_RG),
        wbd=jnp.concatenate([_block_diag(rg_wa[l, 0]), _block_diag(rg_wx[l, 0]),
                             _block_diag(rg_wa[l, 1]), _block_diag(rg_wx[l, 1])], axis=1).astype(BF16),
        rbias=jnp.concatenate([rg_ba[l, 0], rg_bx[l, 0], rg_ba[l, 1], rg_bx[l, 1]]).reshape(1, 4 * D_RG),
        lam=rg_lambda[l], rgn=rg_norm[l].reshape(1, D_RG),
        wo_ml=w_out[l, :D_ML].astype(BF16), wo_rg=w_out[l, D_ML:].astype(BF16),
        norm2=norm2[l].reshape(1, D_MODEL),
        rwt=router_w[l].T, rb=router_bias[l].reshape(N_EXPERTS, 1),
        w13=jnp.concatenate([exp_w1[l], exp_w3[l]], axis=-1).astype(BF16),
        w2=exp_w2[l].astype(BF16),
        sw13=jnp.concatenate([shared_w1[l], shared_w3[l]], axis=-1).astype(BF16),
        sw2=shared_w2[l].astype(BF16),
    )


def _trunk_layer(x2d, pos, mod, c0, n0, m0, h0, lw, b, t, emit_state, norm_final):
    tl = _tiles(t, mod.shape[0] > 1)
    tm = tl['tok']
    q, k, v, o, xr, gr, g, gt = _in_proj(x2d, pos, mod, t, tm, lw['norm1'], lw['wq'], lw['wr'], lw['wgh'],
                                         lw['gbias'])
    ml = _mlstm(q, k, v, g, gt, c0, n0, m0, b, t, tl['chunk'], emit_state)
    rg = _rglru(xr, h0, lw['wbd'], lw['rbias'], lw['lam'], lw['cw'], lw['cb'], b, t, tl['scan'], emit_state)
    x1, hn2p, ek, pk, wtok, cnt = _mix_out(ml[0], ml[1], o, rg[0], rg[1], gr, x2d, pos, mod, t, tm, lw['mln'],
                                           lw['rgn'], lw['wo_ml'], lw['wo_rg'], lw['norm2'], lw['rwt'], lw['rb'])
    dest3, block_expert, n_used, n_slots = _dispatch_plan(cnt[:, 0], ek, pk, tl['expert_rows'])
    xs = _sc_dispatch(hn2p, dest3, n_slots)
    ys = _experts(xs, block_expert, n_used, lw['w13'], lw['w2'], tl['expert_rows'])
    yk = _sc_combine_gather(ys, dest3)
    y = _moe_out(yk, hn2p, wtok, lw['sw13'], lw['sw2'], x1, mod, t, tm, norm_final)
    return y, ml[2:], rg[2:]


def kernel(x_prompt, x_sample, c, state_mlstm_C, state_mlstm_n, state_mlstm_m, state_rglru_h, c_ctx, w_ada, b_ada, norm1, w_in, mlstm_gate_bias, mlstm_norm, rg_conv_w, rg_conv_b, rg_wa, rg_ba, rg_wx, rg_bx, rg_lambda, rg_norm, w_out, norm2, router_w, router_bias, exp_w1, exp_w3, exp_w2, shared_w1, shared_w3, shared_w2, norm_final):
    bp, tp, _ = x_prompt.shape
    bs, ts, _ = x_sample.shape
    depth = w_ada.shape[0]
    assert depth == 1, "the final norm is fused into the single layer's MoE output kernel"
    nd = 2 * N_HEADS
    l = 0
    lw = _layer_weights(l, norm1, w_in, mlstm_gate_bias, mlstm_norm, rg_conv_w, rg_conv_b, rg_wa, rg_ba, rg_wx,
                        rg_bx, rg_lambda, rg_norm, w_out, norm2, router_w, router_bias, exp_w1, exp_w3, exp_w2,
                        shared_w1, shared_w3, shared_w2)
    nf = norm_final.reshape(1, D_MODEL)
    cvecs = jnp.concatenate([c_ctx[None], c, jnp.zeros((SUBLANES - 1 - bs, D_MODEL), F32)], axis=0)
    mod = _ada(cvecs, w_ada[l], b_ada[l]).reshape(SUBLANES, N_MOD, D_MODEL)

    yp, (cc, nc_, mc), (hc,) = _trunk_layer(
        x_prompt.reshape(bp * tp, D_MODEL), jnp.zeros((_tiles(tp, False)['tok'], D_MODEL), F32), mod[0:1],
        jnp.zeros((1, nd, HEAD, HEAD), F32), jnp.zeros((1, nd, HEAD), F32), jnp.zeros((1, nd, HEAD), F32),
        jnp.zeros((1, 2, D_RG), F32), lw, bp, tp, True, nf)
    ys, _, _ = _trunk_layer(
        x_sample.reshape(bs * ts, D_MODEL), _grid_pos_embed(ts, D_MODEL), mod[1:1 + bs],
        state_mlstm_C[:, l].reshape(bs, nd, HEAD, HEAD), state_mlstm_n[:, l].reshape(bs, nd, HEAD),
        jnp.broadcast_to(state_mlstm_m[:, l].reshape(bs, nd, 1), (bs, nd, HEAD)),
        state_rglru_h[:, l], lw, bs, ts, False, nf)

    y_prompt = yp.reshape(bp, tp, D_MODEL)
    y_sample = ys.reshape(bs, ts, D_MODEL)
    new_c = cc.reshape(bp, 1, 2, N_HEADS, HEAD, HEAD)
    new_n = nc_.reshape(bp, 1, 2, N_HEADS, HEAD)
    new_m = mc[:, :, 0].reshape(bp, 1, 2, N_HEADS)
    new_h = hc.reshape(bp, 1, 2, D_RG)
    return (y_prompt, y_sample, new_c, new_n, new_m, new_h)
```

```python
import functools

import jax
import jax.numpy as jnp
from jax import lax
from jax.experimental import pallas as pl
from jax.experimental.pallas import tpu as pltpu
from jax.experimental.pallas import tpu_sc as plsc

F32 = jnp.float32
BF16 = jnp.bfloat16
I32 = jnp.int32
HIGHEST = lax.Precision.HIGHEST

D_MODEL = 1024
N_MOD = 6
D_ML = 512
N_HEADS = 4
HEAD = 128
D_RG = 512
N_RG_BLOCKS = 8
RG_BLOCK = 64
RG_C = 8.0
N_GATE = 16
N_EXPERTS = 64
N_GROUPS = 8
GROUP = 8
TOPK_GROUPS = 4
TOP_K = 8
D_EXPERT = 256
ROUTED_SCALE = 2.5
EPS = 1e-6
GRID_W = 64
POS_BASE = 10000.0

RG_SEG_PAD = 8
SC_WINDOW = 128
D_PACK = D_MODEL // 2

SUBLANES = 8
LANES = 128
VMEM_LIMIT = 48 * 1024 * 1024


def _params(*sem):
    return pltpu.CompilerParams(dimension_semantics=sem, vmem_limit_bytes=VMEM_LIMIT)


def _tiles(t, per_sequence_mod):
    cap = t if per_sequence_mod else 1 << 30
    return dict(
        tok=min(512, cap),
        chunk=min(256, t),
        scan=min(512, t),
        expert_rows=256 if t <= 256 else 512,
    )


def _silu(x):
    return x * jax.nn.sigmoid(x)


def _softplus(x):
    return jnp.maximum(x, 0.0) + jnp.log1p(jnp.exp(-jnp.abs(x)))


def _rms(x, g):
    return x * lax.rsqrt(jnp.mean(x * x, axis=-1, keepdims=True) + EPS) * g


def _pack_bf16_pairs(x):
    w = x.shape[1] // 2
    hi = lax.bitcast_convert_type(x[:, :w].astype(BF16).astype(F32), I32)
    lo = lax.bitcast_convert_type(x[:, w:].astype(BF16).astype(F32), I32)
    return hi | lax.shift_right_logical(lo, jnp.full(lo.shape, 16, I32))


def _unpack_bf16_pairs(p):
    hi = lax.bitcast_convert_type(p & jnp.int32(-65536), F32)
    lo = lax.bitcast_convert_type(lax.shift_left(p, jnp.full(p.shape, 16, I32)), F32)
    return hi, lo


def _ada_kernel(c_ref, w_ref, b_ref, o_ref):
    s = _silu(c_ref[...])
    o_ref[...] = jnp.dot(s, w_ref[...], precision=HIGHEST, preferred_element_type=F32) + b_ref[...]


def _ada(cvecs, w_ada, b_ada):
    n_out = w_ada.shape[1]
    tn = 1536
    return pl.pallas_call(
        _ada_kernel,
        grid=(n_out // tn,),
        in_specs=[
            pl.BlockSpec((SUBLANES, D_MODEL), lambda j: (0, 0)),
            pl.BlockSpec((D_MODEL, tn), lambda j: (0, j)),
            pl.BlockSpec((1, tn), lambda j: (0, j)),
        ],
        out_specs=pl.BlockSpec((SUBLANES, tn), lambda j: (0, j)),
        out_shape=jax.ShapeDtypeStruct((SUBLANES, n_out), F32),
        name="ada",
        compiler_params=_params("arbitrary"),
    )(cvecs, w_ada, b_ada.reshape(1, n_out))


def _in_proj_kernel(x_ref, pos_ref, mod_ref, n1_ref, wq_ref, wr_ref, wgh_ref, gb_ref,
                    q_ref, k_ref, v_ref, o_ref, xr_ref, gr_ref, g_ref, gt_ref):
    x = x_ref[...] + pos_ref[...]
    hn = _rms(x, n1_ref[...]) * (1.0 + mod_ref[0, 1:2, :]) + mod_ref[0, 0:1, :]
    hb = hn.astype(BF16)
    z = jnp.dot(hb, wq_ref[...], preferred_element_type=F32)
    q_ref[...] = z[:, 0:D_ML]
    k_ref[...] = z[:, D_ML:2 * D_ML] * (HEAD ** -0.5)
    v_ref[...] = z[:, 2 * D_ML:3 * D_ML]
    o_ref[...] = z[:, 3 * D_ML:4 * D_ML]
    zr = jnp.dot(hb, wr_ref[...], preferred_element_type=F32)
    xr_ref[...] = zr[:, 0:D_RG]
    gr_ref[...] = zr[:, D_RG:2 * D_RG]
    zg = zr[:, 2 * D_RG:2 * D_RG + LANES]
    h_lo = (hn - hb.astype(F32)).astype(BF16)
    g = (zg + pltpu.roll(zg, LANES - N_GATE, 1)
         + jnp.dot(h_lo, wgh_ref[...], preferred_element_type=F32) + gb_ref[...])
    col = lax.broadcasted_iota(I32, g.shape, 1)
    g = jnp.where((col & 4) != 0, -_softplus(-g), g)
    g_ref[...] = g[:, 0:N_GATE]
    gt_ref[...] = g.T[0:N_GATE, :]


def _mod_index(bm, tm, t):
    if bm > 1:
        return lambda i: ((i * tm) // t, 0, 0)
    return lambda i: (0, 0, 0)


def _in_proj(x2d, pos, mod, t, tm, norm1, wq, wr, wgh, gbias):
    n = x2d.shape[0]
    npos = pos.shape[0] // tm
    tok = lambda i: (i, 0)
    const = lambda i: (0, 0)
    f = lambda w: jax.ShapeDtypeStruct((n, w), F32)
    return pl.pallas_call(
        _in_proj_kernel,
        grid=(n // tm,),
        in_specs=[
            pl.BlockSpec((tm, D_MODEL), tok),
            pl.BlockSpec((tm, D_MODEL), lambda i: (i % npos, 0)),
            pl.BlockSpec((1, N_MOD, D_MODEL), _mod_index(mod.shape[0], tm, t)),
            pl.BlockSpec((1, D_MODEL), const),
            pl.BlockSpec((D_MODEL, 4 * D_ML), const),
            pl.BlockSpec((D_MODEL, 2 * D_RG + LANES), const),
            pl.BlockSpec((D_MODEL, LANES), const),
            pl.BlockSpec((1, LANES), const),
        ],
        out_specs=[pl.BlockSpec((tm, D_ML), tok)] * 4 + [pl.BlockSpec((tm, D_RG), tok)] * 2
        + [pl.BlockSpec((tm, N_GATE), tok), pl.BlockSpec((N_GATE, tm), lambda i: (0, i))],
        out_shape=[f(D_ML)] * 4 + [f(D_RG)] * 2 + [f(N_GATE), jax.ShapeDtypeStruct((N_GATE, n), F32)],
        name="in_proj",
        compiler_params=_params("arbitrary"),
    )(x2d, pos, mod, norm1, wq, wr, wgh, gbias)


def _mlstm_kernel(chunk, nc, emit_state, *refs):
    (qf, kf, vf, gf, gtf, qb, kb, vb, gb, gtb, c0_ref, n0_ref, m0_ref) = refs[:13]
    if emit_state:
        hf_ref, hb_ref, c_out, n_out, m_out, c_sc, n_sc, m_sc = refs[13:]
    else:
        hf_ref, hb_ref, c_sc, n_sc, m_sc = refs[13:]
    i = pl.program_id(1)

    @pl.when(i == 0)
    def _():
        c_sc[...] = c0_ref[0]
        n_sc[...] = n0_ref[0]
        m_sc[...] = m0_ref[0]

    row = lax.broadcasted_iota(I32, (chunk, chunk), 0)
    col = lax.broadcasted_iota(I32, (chunk, chunk), 1)
    hd = []
    for d, (q_ref, k_ref, v_ref, g_ref, gt_ref, h_ref) in enumerate(
            ((qf, kf, vf, gf, gtf, hf_ref), (qb, kb, vb, gb, gtb, hb_ref))):
        tri = (col <= row) if d == 0 else (col >= row)
        trif = tri.astype(F32)
        g = g_ref[...]
        gt = gt_ref[...]
        bcol = jnp.dot(trif, g, precision=HIGHEST, preferred_element_type=F32)
        brow = lax.dot_general(gt, trif, (((1,), (1,)), ((), ())), precision=HIGHEST,
                               preferred_element_type=F32)
        blast = bcol[chunk - 1:chunk, :] if d == 0 else bcol[0:1, :]
        for h in range(N_HEADS):
            ci = d * 8 + h
            cf = d * 8 + 4 + h
            j = d * N_HEADS + h
            hd.append(dict(
                j=j, sl=slice(h * HEAD, (h + 1) * HEAD), tri=tri, q_ref=q_ref, k_ref=k_ref, v_ref=v_ref,
                h_ref=h_ref, b_col=bcol[:, cf:cf + 1], b_row=brow[cf:cf + 1, :], ig_col=g[:, ci:ci + 1],
                ig_row=gt[ci:ci + 1, :], b_last=blast[:, cf:cf + 1], m_prev=m_sc[j:j + 1, 0:1],
                c_prev=c_sc[j], n_prev=n_sc[j:j + 1, :]))
    for x in hd:
        log_d = jnp.where(x['tri'], x['b_col'] - x['b_row'] + x['ig_row'], -jnp.inf)
        inter = x['b_col'] + x['m_prev']
        m_t = jnp.maximum(inter, jnp.max(log_d, axis=1, keepdims=True))
        x['dm'] = jnp.exp(log_d - m_t)
        x['w_inter'] = jnp.exp(inter - m_t)
        x['floor'] = jnp.exp(-m_t)
    for x in hd:
        x['q'] = x['q_ref'][:, x['sl']]
        x['k'] = x['k_ref'][:, x['sl']]
        x['q16'] = x['q'].astype(BF16)
        x['v16'] = x['v_ref'][:, x['sl']].astype(BF16)
        x['s'] = lax.dot_general(x['q16'], x['k'].astype(BF16), (((1,), (1,)), ((), ())),
                                 preferred_element_type=F32) * x['dm']
    for x in hd:
        num = (x['w_inter'] * jnp.dot(x['q16'], x['c_prev'].astype(BF16), preferred_element_type=F32)
               + jnp.dot(x['s'].astype(BF16), x['v16'], preferred_element_type=F32))
        den = (x['w_inter'] * jnp.sum(x['q'] * x['n_prev'], axis=1, keepdims=True)
               + jnp.sum(x['s'], axis=1, keepdims=True))
        x['h_ref'][:, x['sl']] = num / jnp.maximum(jnp.abs(den), x['floor'])
    for x in hd:
        j = x['j']
        log_w_col = x['b_last'] - x['b_col'] + x['ig_col']
        log_w_row = x['b_last'] - x['b_row'] + x['ig_row']
        m_new = jnp.maximum(x['b_last'] + x['m_prev'], jnp.max(log_w_row, axis=1, keepdims=True))
        decay = jnp.exp(x['b_last'] + x['m_prev'] - m_new)
        kw = x['k'] * jnp.exp(log_w_col - m_new)
        c_sc[j] = decay * x['c_prev'] + lax.dot_general(
            kw.astype(BF16), x['v16'], (((0,), (0,)), ((), ())), preferred_element_type=F32)
        n_sc[j:j + 1, :] = decay * x['n_prev'] + jnp.sum(kw, axis=0, keepdims=True)
        m_sc[j:j + 1, :] = jnp.broadcast_to(m_new, (1, HEAD))

    if emit_state:
        @pl.when(i == nc - 1)
        def _():
            c_out[0] = c_sc[...]
            n_out[0] = n_sc[...]
            m_out[0] = m_sc[...]


def _state_index(bm):
    if bm > 1:
        return lambda b, i: (b,) + (0,) * 3, lambda b, i: (b, 0, 0)
    return lambda b, i: (0,) * 4, lambda b, i: (0, 0, 0)


def _mlstm(q, k, v, g, gt, c0, n0, m0, b, t, chunk, emit_state):
    nc = t // chunk
    n = b * t
    nd = 2 * N_HEADS
    fwd = lambda bi, i: (bi * nc + i, 0)
    bwd = lambda bi, i: (bi * nc + nc - 1 - i, 0)
    fwd_t = lambda bi, i: (0, bi * nc + i)
    bwd_t = lambda bi, i: (0, bi * nc + nc - 1 - i)
    c_idx, n_idx = _state_index(c0.shape[0])
    tok = lambda m: [pl.BlockSpec((chunk, D_ML), m)] * 3 + [pl.BlockSpec((chunk, N_GATE), m)]
    in_specs = (tok(fwd) + [pl.BlockSpec((N_GATE, chunk), fwd_t)]
                + tok(bwd) + [pl.BlockSpec((N_GATE, chunk), bwd_t)]
                + [pl.BlockSpec((1, nd, HEAD, HEAD), c_idx),
                   pl.BlockSpec((1, nd, HEAD), n_idx), pl.BlockSpec((1, nd, HEAD), n_idx)])
    out_specs = [pl.BlockSpec((chunk, D_ML), fwd), pl.BlockSpec((chunk, D_ML), bwd)]
    out_shape = [jax.ShapeDtypeStruct((n, D_ML), F32)] * 2
    if emit_state:
        out_specs += [pl.BlockSpec((1, nd, HEAD, HEAD), lambda bi, i: (bi, 0, 0, 0)),
                      pl.BlockSpec((1, nd, HEAD), lambda bi, i: (bi, 0, 0)),
                      pl.BlockSpec((1, nd, HEAD), lambda bi, i: (bi, 0, 0))]
        out_shape += [jax.ShapeDtypeStruct((b, nd, HEAD, HEAD), F32),
                      jax.ShapeDtypeStruct((b, nd, HEAD), F32),
                      jax.ShapeDtypeStruct((b, nd, HEAD), F32)]
    return pl.pallas_call(
        functools.partial(_mlstm_kernel, chunk, nc, emit_state),
        grid=(b, nc),
        in_specs=in_specs,
        out_specs=out_specs,
        out_shape=out_shape,
        scratch_shapes=[pltpu.VMEM((nd, HEAD, HEAD), F32), pltpu.VMEM((nd, HEAD), F32),
                        pltpu.VMEM((nd, HEAD), F32)],
        name="mlstm",
        compiler_params=_params("arbitrary", "arbitrary"),
    )(q, k, v, g, gt, q, k, v, g, gt, c0, n0, m0)


def _neg_expm1(x):
    u = jnp.exp(x)
    near = jnp.where(u == 1.0, x, (u - 1.0) * x / jnp.log(u))
    return -jnp.where(x < -0.5, u - 1.0, near)


def _rglru_kernel(tb, nb, emit_state, *refs):
    (xf, xf_prev, xf_next, xb, xb_prev, xb_next, h0_ref, w_ref, bias_ref, lam_ref, cw_ref, cb_ref) = refs[:12]
    n_out = 3 if emit_state else 2
    hf_ref, hb_ref = refs[12:14]
    hfin_ref = refs[14] if emit_state else None
    carry, af_sc, uf_sc, ab_sc, ub_sc, hf_sc, pf_sc, hb_sc, pb_sc = refs[12 + n_out:]
    i = pl.program_id(1)

    @pl.when(i == 0)
    def _():
        carry[...] = h0_ref[0]

    row = lax.broadcasted_iota(I32, (tb, D_RG), 0)
    cw = cw_ref[...]
    softplus_neg_lam = _softplus(-lam_ref[...])

    def conv(main_ref, prev_ref, next_ref, first, last):
        main = main_ref[...]
        prev = jnp.where(first, 0.0, prev_ref[...])
        nxt = jnp.where(last, 0.0, next_ref[...])
        xm2 = jnp.where(row == 0, prev[6:7, :], jnp.where(row == 1, prev[7:8, :], pltpu.roll(main, 2, 0)))
        xm1 = jnp.where(row == 0, prev[7:8, :], pltpu.roll(main, 1, 0))
        xp1 = jnp.where(row == tb - 1, nxt[0:1, :], pltpu.roll(main, tb - 1, 0))
        return cb_ref[...] + xm2 * cw[0:1, :] + xm1 * cw[1:2, :] + main * cw[2:3, :] + xp1 * cw[3:4, :]

    def recurrence_terms(xc, d):
        z = jnp.dot(xc.astype(BF16), w_ref[:, d * 2 * D_RG:(d + 1) * 2 * D_RG],
                    preferred_element_type=F32) + bias_ref[:, d * 2 * D_RG:(d + 1) * 2 * D_RG]
        r = jax.nn.sigmoid(z[:, 0:D_RG])
        ig = jax.nn.sigmoid(z[:, D_RG:2 * D_RG])
        log_a = -RG_C * r * softplus_neg_lam[d:d + 1, :]
        a = jnp.exp(log_a)
        u = jnp.sqrt(_neg_expm1(2.0 * log_a)) * (ig * xc)
        return a, u

    a_f, u_f = recurrence_terms(conv(xf, xf_prev, xf_next, i == 0, i == nb - 1), 0)
    a_b, u_b = recurrence_terms(conv(xb, xb_prev, xb_next, i == nb - 1, i == 0), 1)
    seg = tb // SUBLANES
    pitch = seg + RG_SEG_PAD
    ncol = D_RG // LANES
    for lc in range(ncol):
        lanes = slice(lc * LANES, (lc + 1) * LANES)
        for s in range(SUBLANES):
            src = slice(s * seg, (s + 1) * seg)
            dst = slice(s * pitch, s * pitch + seg)
            af_sc[lc, dst, :], uf_sc[lc, dst, :] = a_f[src, lanes], u_f[src, lanes]
            ab_sc[lc, dst, :], ub_sc[lc, dst, :] = a_b[src, lanes], u_b[src, lanes]
    slab = lambda k: (slice(None), pl.ds(k, SUBLANES, stride=pitch), slice(None))
    hf = jnp.zeros((ncol, SUBLANES, LANES), F32)
    hb = jnp.zeros((ncol, SUBLANES, LANES), F32)
    pf = jnp.ones((ncol, SUBLANES, LANES), F32)
    pb = jnp.ones((ncol, SUBLANES, LANES), F32)
    for k in range(seg):
        kb = seg - 1 - k
        ak = af_sc[slab(k)]
        hf = ak * hf + uf_sc[slab(k)]
        pf = pf * ak
        hf_sc[slab(k)] = hf
        pf_sc[slab(k)] = pf
        ak = ab_sc[slab(kb)]
        hb = ak * hb + ub_sc[slab(kb)]
        pb = pb * ak
        hb_sc[slab(kb)] = hb
        pb_sc[slab(kb)] = pb
    for lc in range(ncol):
        lanes = slice(lc * LANES, (lc + 1) * LANES)
        c = carry[0:1, lanes]
        cin_f = []
        for s in range(SUBLANES):
            cin_f.append(c)
            c = pf[lc, s:s + 1, :] * c + hf[lc, s:s + 1, :]
        carry[0:1, lanes] = c
        c = carry[1:2, lanes]
        cin_b = [None] * SUBLANES
        for s in reversed(range(SUBLANES)):
            cin_b[s] = c
            c = pb[lc, s:s + 1, :] * c + hb[lc, s:s + 1, :]
        carry[1:2, lanes] = c
        for s in range(SUBLANES):
            rows = slice(s * seg, (s + 1) * seg)
            src = slice(s * pitch, s * pitch + seg)
            hf_ref[rows, lanes] = hf_sc[lc, src, :] + pf_sc[lc, src, :] * cin_f[s]
            hb_ref[rows, lanes] = hb_sc[lc, src, :] + pb_sc[lc, src, :] * cin_b[s]

    if emit_state:
        @pl.when(i == nb - 1)
        def _():
            hfin_ref[0] = carry[...]


def _rglru(xr, h0, wbd, bias, lam, cw, cb, b, t, tb, emit_state):
    nb = t // tb
    n = b * t
    r8 = tb // SUBLANES
    last8 = n // SUBLANES - 1
    fwd = lambda bi, i: (bi * nb + i, 0)
    bwd = lambda bi, i: (bi * nb + nb - 1 - i, 0)
    fwd_prev = lambda bi, i: (jnp.maximum((bi * nb + i) * r8 - 1, 0), 0)
    fwd_next = lambda bi, i: (jnp.minimum((bi * nb + i + 1) * r8, last8), 0)
    bwd_prev = lambda bi, i: (jnp.maximum((bi * nb + nb - 1 - i) * r8 - 1, 0), 0)
    bwd_next = lambda bi, i: (jnp.minimum((bi * nb + nb - i) * r8, last8), 0)
    const = lambda bi, i: (0, 0)
    h_idx = (lambda bi, i: (bi, 0, 0)) if h0.shape[0] > 1 else (lambda bi, i: (0, 0, 0))
    halo = lambda m: pl.BlockSpec((SUBLANES, D_RG), m)
    in_specs = [pl.BlockSpec((tb, D_RG), fwd), halo(fwd_prev), halo(fwd_next),
                pl.BlockSpec((tb, D_RG), bwd), halo(bwd_prev), halo(bwd_next),
                pl.BlockSpec((1, 2, D_RG), h_idx),
                pl.BlockSpec((D_RG, 4 * D_RG), const), pl.BlockSpec((1, 4 * D_RG), const),
                pl.BlockSpec((2, D_RG), const), pl.BlockSpec((4, D_RG), const), pl.BlockSpec((1, D_RG), const)]
    out_specs = [pl.BlockSpec((tb, D_RG), fwd), pl.BlockSpec((tb, D_RG), bwd)]
    out_shape = [jax.ShapeDtypeStruct((n, D_RG), F32)] * 2
    if emit_state:
        out_specs.append(pl.BlockSpec((1, 2, D_RG), lambda bi, i: (bi, 0, 0)))
        out_shape.append(jax.ShapeDtypeStruct((b, 2, D_RG), F32))
    return pl.pallas_call(
        functools.partial(_rglru_kernel, tb, nb, emit_state),
        grid=(b, nb),
        in_specs=in_specs,
        out_specs=out_specs,
        out_shape=out_shape,
        scratch_shapes=[pltpu.VMEM((2, D_RG), F32)]
        + [pltpu.VMEM((D_RG // LANES, tb + SUBLANES * RG_SEG_PAD, LANES), F32)] * 8,
        name="rglru",
        compiler_params=_params("arbitrary", "arbitrary"),
    )(xr, xr, xr, xr, xr, xr, h0, wbd, bias, lam, cw, cb)


def _route(s, sb):
    tm = s.shape[1]
    neg = -jnp.inf
    sub = lax.broadcasted_iota(I32, (GROUP, tm), 0)
    blocks = [sb[gi * GROUP:(gi + 1) * GROUP, :] for gi in range(N_GROUPS)]
    gscore = []
    for blk in blocks:
        m1 = jnp.max(blk, axis=0, keepdims=True)
        first = jnp.min(jnp.where(blk == m1, sub, GROUP), axis=0, keepdims=True)
        m2 = jnp.max(jnp.where(sub == first, neg, blk), axis=0, keepdims=True)
        gscore.append(m1 + m2)
    masked = []
    for gi in range(N_GROUPS):
        rank = jnp.zeros((1, tm), F32)
        for gj in range(N_GROUPS):
            if gj == gi:
                continue
            ahead = (gscore[gj] >= gscore[gi]) if gj < gi else (gscore[gj] > gscore[gi])
            rank = rank + jnp.where(ahead, 1.0, 0.0)
        masked.append(jnp.where(rank < TOPK_GROUPS, blocks[gi], neg))
    v = jnp.concatenate(masked, axis=0)
    eid = lax.broadcasted_iota(I32, (N_EXPERTS, tm), 0)
    sel = jnp.zeros((N_EXPERTS, tm), F32)
    picks = []
    for _ in range(TOP_K):
        mx = jnp.max(v, axis=0, keepdims=True)
        idx = jnp.min(jnp.where(v == mx, eid, N_EXPERTS), axis=0, keepdims=True)
        pick = eid == idx
        picks.append(pick)
        sel = jnp.where(pick, 1.0, sel)
        v = jnp.where(pick, neg, v)
    ws = s * sel
    return ws / jnp.sum(ws, axis=0, keepdims=True) * ROUTED_SCALE, sel, picks


def _mix_out_kernel(hmf_ref, hmb_ref, o_ref, hrf_ref, hrb_ref, gr_ref, x_ref, pos_ref, mod_ref,
                    mln_ref, rgn_ref, wo_ml_ref, wo_rg_ref, n2_ref, rwt_ref, rb_ref,
                    x1_ref, hn2p_ref, ek_ref, pk_ref, wtok_ref, cnt_ref, cnt_sc):
    i = pl.program_id(0)
    tm = x_ref.shape[0]

    @pl.when(i == 0)
    def _():
        cnt_sc[...] = jnp.zeros_like(cnt_sc)

    hm = hmf_ref[...] + hmb_ref[...]
    heads = []
    for h in range(N_HEADS):
        seg = hm[:, h * HEAD:(h + 1) * HEAD]
        heads.append(seg * lax.rsqrt(jnp.mean(seg * seg, axis=-1, keepdims=True) + EPS))
    y_ml = jnp.concatenate(heads, axis=1) * mln_ref[...] * jax.nn.sigmoid(o_ref[...])
    y_rg = _rms(hrf_ref[...] + hrb_ref[...], rgn_ref[...]) * jax.nn.gelu(gr_ref[...])
    mix = (jnp.dot(y_ml.astype(BF16), wo_ml_ref[...], preferred_element_type=F32)
           + jnp.dot(y_rg.astype(BF16), wo_rg_ref[...], preferred_element_type=F32))
    x1 = x_ref[...] + pos_ref[...] + mod_ref[0, 2:3, :] * mix
    x1_ref[...] = x1
    hn2 = _rms(x1, n2_ref[...]) * (1.0 + mod_ref[0, 4:5, :]) + mod_ref[0, 3:4, :]
    hn2p_ref[...] = _pack_bf16_pairs(hn2)
    logits_t = lax.dot_general(rwt_ref[...], hn2, (((1,), (1,)), ((), ())), precision=HIGHEST,
                               preferred_element_type=F32)
    s = jax.nn.sigmoid(logits_t)
    wt, sel, picks = _route(s, s + rb_ref[...])

    earlier = (lax.broadcasted_iota(I32, (tm, tm), 0) < lax.broadcasted_iota(I32, (tm, tm), 1))
    prefix = jnp.dot(sel.astype(BF16), earlier.astype(BF16), preferred_element_type=F32)
    pos_all = cnt_sc[:, 0:1] + prefix
    eid = lax.broadcasted_iota(I32, (N_EXPERTS, tm), 0)
    eid_f = eid.astype(F32)
    row8 = lax.broadcasted_iota(I32, (TOP_K, tm), 0)
    ek = jnp.zeros((TOP_K, tm), F32)
    pk = jnp.zeros((TOP_K, tm), F32)
    wk = jnp.zeros((N_EXPERTS, tm), F32)
    for k, pick in enumerate(picks):
        take = lambda a: jnp.sum(jnp.where(pick, a, 0.0), axis=0, keepdims=True)
        ek = jnp.where(row8 == k, take(eid_f), ek)
        pk = jnp.where(row8 == k, take(pos_all), pk)
        wk = jnp.where(eid == k, take(wt), wk)
    ek_ref[...] = ek.astype(I32)
    pk_ref[...] = pk.astype(I32)
    wtok_ref[...] = wk.T
    cnt_sc[...] += jnp.broadcast_to(jnp.sum(sel, axis=1, keepdims=True), cnt_sc.shape)

    @pl.when(i == pl.num_programs(0) - 1)
    def _():
        cnt_ref[...] = cnt_sc[...].astype(I32)


def _mix_out(hmf, hmb, o, hrf, hrb, gr, x2d, pos, mod, t, tm, mln, rgn, wo_ml, wo_rg, norm2, rwt, rbias):
    n = x2d.shape[0]
    npos = pos.shape[0] // tm
    tok = lambda i: (i, 0)
    tok_t = lambda i: (0, i)
    const = lambda i: (0, 0)
    return pl.pallas_call(
        _mix_out_kernel,
        grid=(n // tm,),
        in_specs=[pl.BlockSpec((tm, D_ML), tok)] * 6 + [
            pl.BlockSpec((tm, D_MODEL), tok),
            pl.BlockSpec((tm, D_MODEL), lambda i: (i % npos, 0)),
            pl.BlockSpec((1, N_MOD, D_MODEL), _mod_index(mod.shape[0], tm, t)),
            pl.BlockSpec((1, D_ML), const), pl.BlockSpec((1, D_RG), const),
            pl.BlockSpec((D_ML, D_MODEL), const), pl.BlockSpec((D_RG, D_MODEL), const),
            pl.BlockSpec((1, D_MODEL), const),
            pl.BlockSpec((N_EXPERTS, D_MODEL), const), pl.BlockSpec((N_EXPERTS, 1), const),
        ],
        out_specs=[pl.BlockSpec((tm, D_MODEL), tok), pl.BlockSpec((tm, D_PACK), tok),
                   pl.BlockSpec((TOP_K, tm), tok_t), pl.BlockSpec((TOP_K, tm), tok_t),
                   pl.BlockSpec((tm, N_EXPERTS), tok), pl.BlockSpec((N_EXPERTS, LANES), const)],
        out_shape=[jax.ShapeDtypeStruct((n, D_MODEL), F32), jax.ShapeDtypeStruct((n, D_PACK), I32),
                   jax.ShapeDtypeStruct((TOP_K, n), I32), jax.ShapeDtypeStruct((TOP_K, n), I32),
                   jax.ShapeDtypeStruct((n, N_EXPERTS), F32), jax.ShapeDtypeStruct((N_EXPERTS, LANES), I32)],
        scratch_shapes=[pltpu.VMEM((N_EXPERTS, LANES), F32)],
        name="mix_out",
        compiler_params=_params("arbitrary"),
    )(hmf, hmb, o, hrf, hrb, gr, x2d, pos, mod, mln, rgn, wo_ml, wo_rg, norm2, rwt, rbias)


def _sc_mesh():
    return plsc.VectorSubcoreMesh(core_axis_name="core", subcore_axis_name="subcore")


def _sc_worker():
    info = plsc.get_sparse_core_info()
    return lax.axis_index("subcore") * info.num_cores + lax.axis_index("core"), info.num_cores * info.num_subcores


def _sc_dispatch(xp, dest3, n_slots):
    n, w = xp.shape
    nwin = n // SC_WINDOW

    @pl.kernel(out_type=jax.ShapeDtypeStruct((n_slots, w), xp.dtype), mesh=_sc_mesh(),
               scratch_types=[pltpu.VMEM((SC_WINDOW, w), xp.dtype), pltpu.VMEM((TOP_K, SC_WINDOW), I32)],
               name="sc_dispatch")
    def k(x_hbm, i_hbm, o_hbm, x_v, i_v):
        wid, nworkers = _sc_worker()
        per = nwin // nworkers

        @pl.loop(0, per)
        def _(s):
            win = wid * per + s
            pltpu.sync_copy(x_hbm.at[pl.ds(win * SC_WINDOW, SC_WINDOW)], x_v)
            pltpu.sync_copy(i_hbm.at[win], i_v)
            for j in range(TOP_K):
                pltpu.sync_copy(x_v, o_hbm.at[i_v.at[j]])

    return k(xp, dest3)


def _sc_combine_gather(ys, dest3):
    nwin = dest3.shape[0]
    w = ys.shape[1]

    @pl.kernel(out_type=jax.ShapeDtypeStruct((nwin, TOP_K, SC_WINDOW, w), ys.dtype), mesh=_sc_mesh(),
               scratch_types=[pltpu.VMEM((SC_WINDOW, w), ys.dtype), pltpu.VMEM((TOP_K, SC_WINDOW), I32)],
               name="sc_combine")
    def k(y_hbm, i_hbm, o_hbm, y_v, i_v):
        wid, nworkers = _sc_worker()
        per = nwin // nworkers

        @pl.loop(0, per)
        def _(s):
            win = wid * per + s
            pltpu.sync_copy(i_hbm.at[win], i_v)
            for j in range(TOP_K):
                pltpu.sync_copy(y_hbm.at[i_v.at[j]], y_v)
                pltpu.sync_copy(y_v, o_hbm.at[win, j])

    return k(ys, dest3)


def _swiglu(x, w13):
    h = jnp.dot(x, w13, preferred_element_type=F32)
    return _silu(h[:, 0:D_EXPERT]) * h[:, D_EXPERT:2 * D_EXPERT]


def _unpack_rows_bf16(p):
    hi, lo = _unpack_bf16_pairs(p)
    return jnp.concatenate([hi.astype(BF16), lo.astype(BF16)], axis=1)


def _expert_kernel(be_ref, nu_ref, x_ref, w13_ref, w2_ref, y_ref):
    @pl.when(pl.program_id(0) < nu_ref[0])
    def _():
        rows = x_ref.shape[0] // 2
        for r in range(2):
            sl = slice(r * rows, (r + 1) * rows)
            h = _swiglu(_unpack_rows_bf16(x_ref[sl, :]), w13_ref[0])
            y_ref[sl, :] = _pack_bf16_pairs(jnp.dot(h.astype(BF16), w2_ref[0], preferred_element_type=F32))


def _experts(xs, block_expert, n_used, w13, w2, rows):
    nb = xs.shape[0] // rows
    return pl.pallas_call(
        _expert_kernel,
        grid_spec=pltpu.PrefetchScalarGridSpec(
            num_scalar_prefetch=2,
            grid=(nb,),
            in_specs=[
                pl.BlockSpec((rows, D_PACK), lambda b, be, nu: (b, 0)),
                pl.BlockSpec((1, D_MODEL, 2 * D_EXPERT), lambda b, be, nu: (be[b], 0, 0)),
                pl.BlockSpec((1, D_EXPERT, D_MODEL), lambda b, be, nu: (be[b], 0, 0)),
            ],
            out_specs=pl.BlockSpec((rows, D_PACK), lambda b, be, nu: (b, 0)),
        ),
        out_shape=jax.ShapeDtypeStruct(xs.shape, I32),
        name="experts",
        compiler_params=_params("arbitrary"),
    )(block_expert, n_used, xs, w13, w2)


def _moe_out_kernel(yk_ref, hn2p_ref, wtok_ref, sw13_ref, sw2_ref, x1_ref, mod_ref, nf_ref, y_ref):
    shared = jnp.dot(_swiglu(_unpack_rows_bf16(hn2p_ref[...]), sw13_ref[...]).astype(BF16), sw2_ref[...],
                     preferred_element_type=F32)
    w = wtok_ref[...]
    parts = []
    for wi in range(yk_ref.shape[0]):
        rows = slice(wi * SC_WINDOW, (wi + 1) * SC_WINDOW)
        a_hi = shared[rows, 0:D_PACK]
        a_lo = shared[rows, D_PACK:D_MODEL]
        for k in range(TOP_K):
            y_hi, y_lo = _unpack_bf16_pairs(yk_ref[wi, k])
            wc = w[rows, k:k + 1]
            a_hi = a_hi + wc * y_hi
            a_lo = a_lo + wc * y_lo
        parts.append(jnp.concatenate([a_hi, a_lo], axis=1))
    x2 = x1_ref[...] + mod_ref[0, 5:6, :] * jnp.concatenate(parts, axis=0)
    y_ref[...] = _rms(x2, nf_ref[...])


def _moe_out(yk, hn2p, wtok, sw13, sw2, x1, mod, t, tm, norm_final):
    n = hn2p.shape[0]
    tok = lambda i: (i, 0)
    const = lambda i: (0, 0)
    return pl.pallas_call(
        _moe_out_kernel,
        grid=(n // tm,),
        in_specs=[
            pl.BlockSpec((tm // SC_WINDOW, TOP_K, SC_WINDOW, D_PACK), lambda i: (i, 0, 0, 0)),
            pl.BlockSpec((tm, D_PACK), tok),
            pl.BlockSpec((tm, N_EXPERTS), tok),
            pl.BlockSpec((D_MODEL, 2 * D_EXPERT), const),
            pl.BlockSpec((D_EXPERT, D_MODEL), const),
            pl.BlockSpec((tm, D_MODEL), tok),
            pl.BlockSpec((1, N_MOD, D_MODEL), _mod_index(mod.shape[0], tm, t)),
            pl.BlockSpec((1, D_MODEL), const),
        ],
        out_specs=pl.BlockSpec((tm, D_MODEL), tok),
        out_shape=jax.ShapeDtypeStruct((n, D_MODEL), F32),
        name="moe_out",
        compiler_params=_params("arbitrary"),
    )(yk, hn2p, wtok, sw13, sw2, x1, mod, norm_final)


def _dispatch_plan(cnt, ek, pk, rows):
    n = ek.shape[1]
    nb = n * TOP_K // rows + N_EXPERTS
    padded = (cnt + rows - 1) // rows * rows
    ends = jnp.cumsum(padded)
    experts = jnp.arange(N_EXPERTS, dtype=I32)
    first_row = jnp.sum(jnp.where(ek[:, :, None] == experts, ends - padded, 0), axis=-1)
    dest3 = (first_row + pk).reshape(TOP_K, n // SC_WINDOW, SC_WINDOW).transpose(1, 0, 2)
    block_end = ends // rows
    block_expert = jnp.minimum(jnp.sum(block_end[None, :] <= jnp.arange(nb, dtype=I32)[:, None], axis=1),
                               N_EXPERTS - 1).astype(I32)
    return dest3, block_expert, block_end[-1:].astype(I32), nb * rows


def _grid_pos_embed(n_tokens, dim):
    rows_n = n_tokens // GRID_W
    row = jnp.repeat(jnp.arange(rows_n), GRID_W).astype(F32)
    col = jnp.tile(jnp.arange(GRID_W), rows_n).astype(F32)
    quarter = dim // 4
    omega = 1.0 / (POS_BASE ** (jnp.arange(quarter, dtype=F32) / quarter))
    ra = row[:, None] * omega
    ca = col[:, None] * omega
    return jnp.concatenate([jnp.sin(ra), jnp.cos(ra), jnp.sin(ca), jnp.cos(ca)], axis=-1)


def _block_diag(w):
    eye = jnp.eye(N_RG_BLOCKS, dtype=w.dtype)
    return jnp.einsum('nij,nm->nimj', w, eye).reshape(D_RG, D_RG)


def _layer_weights(l, norm1, w_in, mlstm_gate_bias, mlstm_norm, rg_conv_w, rg_conv_b, rg_wa, rg_ba, rg_wx,
                   rg_bx, rg_lambda, rg_norm, w_out, norm2, router_w, router_bias, exp_w1, exp_w3, exp_w2,
                   shared_w1, shared_w3, shared_w2):
    wi = w_in[l]
    c0, c1 = 4 * D_ML, 4 * D_ML + N_GATE
    wg = wi[:, c0:c1]
    wg_hi = wg.astype(BF16)
    wg_lo = (wg - wg_hi.astype(F32)).astype(BF16)
    zcols = lambda w: jnp.zeros((D_MODEL, w), BF16)
    return dict(
        norm1=norm1[l].reshape(1, D_MODEL),
        wq=wi[:, :c0].astype(BF16),
        wr=jnp.concatenate([wi[:, c1:].astype(BF16), wg_hi, wg_lo, zcols(LANES - 2 * N_GATE)], axis=1),
        wgh=jnp.concatenate([wg_hi, zcols(LANES - N_GATE)], axis=1),
        gbias=jnp.pad(mlstm_gate_bias[l].reshape(1, N_GATE), ((0, 0), (0, LANES - N_GATE))),
        mln=mlstm_norm[l].reshape(1, D_ML),
        cw=rg_conv_w[l], cb=rg_conv_b[l].reshape(1, D_RG),
        wbd=jnp.concatenate([_block_diag(rg_wa[l, 0]), _block_diag(rg_wx[l, 0]),
                             _block_diag(rg_wa[l, 1]), _block_diag(rg_wx[l, 1])], axis=1).astype(BF16),
        rbias=jnp.concatenate([rg_ba[l, 0], rg_bx[l, 0], rg_ba[l, 1], rg_bx[l, 1]]).reshape(1, 4 * D_RG),
        lam=rg_lambda[l], rgn=rg_norm[l].reshape(1, D_RG),
        wo_ml=w_out[l, :D_ML].astype(BF16), wo_rg=w_out[l, D_ML:].astype(BF16),
        norm2=norm2[l].reshape(1, D_MODEL),
        rwt=router_w[l].T, rb=router_bias[l].reshape(N_EXPERTS, 1),
        w13=jnp.concatenate([exp_w1[l], exp_w3[l]], axis=-1).astype(BF16),
        w2=exp_w2[l].astype(BF16),
        sw13=jnp.concatenate([shared_w1[l], shared_w3[l]], axis=-1).astype(BF16),
        sw2=shared_w2[l].astype(BF16),
    )


def _trunk_layer(x2d, pos, mod, c0, n0, m0, h0, lw, b, t, emit_state, norm_final):
    tl = _tiles(t, mod.shape[0] > 1)
    tm = tl['tok']
    q, k, v, o, xr, gr, g, gt = _in_proj(x2d, pos, mod, t, tm, lw['norm1'], lw['wq'], lw['wr'], lw['wgh'],
                                         lw['gbias'])
    ml = _mlstm(q, k, v, g, gt, c0, n0, m0, b, t, tl['chunk'], emit_state)
    rg = _rglru(xr, h0, lw['wbd'], lw['rbias'], lw['lam'], lw['cw'], lw['cb'], b, t, tl['scan'], emit_state)
    x1, hn2p, ek, pk, wtok, cnt = _mix_out(ml[0], ml[1], o, rg[0], rg[1], gr, x2d, pos, mod, t, tm, lw['mln'],
                                           lw['rgn'], lw['wo_ml'], lw['wo_rg'], lw['norm2'], lw['rwt'], lw['rb'])
    dest3, block_expert, n_used, n_slots = _dispatch_plan(cnt[:, 0], ek, pk, tl['expert_rows'])
    xs = _sc_dispatch(hn2p, dest3, n_slots)
    ys = _experts(xs, block_expert, n_used, lw['w13'], lw['w2'], tl['expert_rows'])
    yk = _sc_combine_gather(ys, dest3)
    y = _moe_out(yk, hn2p, wtok, lw['sw13'], lw['sw2'], x1, mod, t, tm, norm_final)
    return y, ml[2:], rg[2:]


def kernel(x_prompt, x_sample, c, state_mlstm_C, state_mlstm_n, state_mlstm_m, state_rglru_h, c_ctx, w_ada, b_ada, norm1, w_in, mlstm_gate_bias, mlstm_norm, rg_conv_w, rg_conv_b, rg_wa, rg_ba, rg_wx, rg_bx, rg_lambda, rg_norm, w_out, norm2, router_w, router_bias, exp_w1, exp_w3, exp_w2, shared_w1, shared_w3, shared_w2, norm_final):
    bp, tp, _ = x_prompt.shape
    bs, ts, _ = x_sample.shape
    depth = w_ada.shape[0]
    assert depth == 1, "the final norm is fused into the single layer's MoE output kernel"
    nd = 2 * N_HEADS
    l = 0
    lw = _layer_weights(l, norm1, w_in, mlstm_gate_bias, mlstm_norm, rg_conv_w, rg_conv_b, rg_wa, rg_ba, rg_wx,
                        rg_bx, rg_lambda, rg_norm, w_out, norm2, router_w, router_bias, exp_w1, exp_w3, exp_w2,
                        shared_w1, shared_w3, shared_w2)
    nf = norm_final.reshape(1, D_MODEL)
    cvecs = jnp.concatenate([c_ctx[None], c, jnp.zeros((SUBLANES - 1 - bs, D_MODEL), F32)], axis=0)
    mod = _ada(cvecs, w_ada[l], b_ada[l]).reshape(SUBLANES, N_MOD, D_MODEL)

    yp, (cc, nc_, mc), (hc,) = _trunk_layer(
        x_prompt.reshape(bp * tp, D_MODEL), jnp.zeros((_tiles(tp, False)['tok'], D_MODEL), F32), mod[0:1],
        jnp.zeros((1, nd, HEAD, HEAD), F32), jnp.zeros((1, nd, HEAD), F32), jnp.zeros((1, nd, HEAD), F32),
        jnp.zeros((1, 2, D_RG), F32), lw, bp, tp, True, nf)
    ys, _, _ = _trunk_layer(
        x_sample.reshape(bs * ts, D_MODEL), _grid_pos_embed(ts, D_MODEL), mod[1:1 + bs],
        state_mlstm_C[:, l].reshape(bs, nd, HEAD, HEAD), state_mlstm_n[:, l].reshape(bs, nd, HEAD),
        jnp.broadcast_to(state_mlstm_m[:, l].reshape(bs, nd, 1), (bs, nd, HEAD)),
        state_rglru_h[:, l], lw, bs, ts, False, nf)

    y_prompt = yp.reshape(bp, tp, D_MODEL)
    y_sample = ys.reshape(bs, ts, D_MODEL)
    new_c = cc.reshape(bp, 1, 2, N_HEADS, HEAD, HEAD)
    new_n = nc_.reshape(bp, 1, 2, N_HEADS, HEAD)
    new_m = mc[:, :, 0].reshape(bp, 1, 2, N_HEADS)
    new_h = hc.reshape(bp, 1, 2, D_RG)
    return (y_prompt, y_sample, new_c, new_n, new_m, new_h)
```

```python
import functools

import jax
import jax.numpy as jnp
from jax import lax
from jax.experimental import pallas as pl
from jax.experimental.pallas import tpu as pltpu
from jax.experimental.pallas import tpu_sc as plsc

F32 = jnp.float32
BF16 = jnp.bfloat16
I32 = jnp.int32
HIGHEST = lax.Precision.HIGHEST

D_MODEL = 1024
N_MOD = 6
D_ML = 512
N_HEADS = 4
HEAD = 128
D_RG = 512
N_RG_BLOCKS = 8
RG_BLOCK = 64
RG_C = 8.0
N_GATE = 16
N_EXPERTS = 64
N_GROUPS = 8
GROUP = 8
TOPK_GROUPS = 4
TOP_K = 8
D_EXPERT = 256
ROUTED_SCALE = 2.5
EPS = 1e-6
GRID_W = 64
POS_BASE = 10000.0

RG_SEG_PAD = 8
SC_WINDOW = 128
D_PACK = D_MODEL // 2
EXPERT_ROWS = 512
EXPERT_BLOCKS_PER_STEP = 4

SUBLANES = 8
LANES = 128
VMEM_LIMIT = 48 * 1024 * 1024


def _params(*sem):
    return pltpu.CompilerParams(dimension_semantics=sem, vmem_limit_bytes=VMEM_LIMIT)


def _tiles(t, per_sequence_mod):
    cap = t if per_sequence_mod else 1 << 30
    return dict(
        tok=min(512, cap),
        chunk=min(256, t),
        scan=min(512, t),
    )


def _silu(x):
    return x * jax.nn.sigmoid(x)


def _softplus(x):
    return jnp.maximum(x, 0.0) + jnp.log1p(jnp.exp(-jnp.abs(x)))


def _rms(x, g):
    return x * lax.rsqrt(jnp.mean(x * x, axis=-1, keepdims=True) + EPS) * g


def _pack_bf16_pairs(x):
    w = x.shape[1] // 2
    hi = lax.bitcast_convert_type(x[:, :w].astype(BF16).astype(F32), I32)
    lo = lax.bitcast_convert_type(x[:, w:].astype(BF16).astype(F32), I32)
    return hi | lax.shift_right_logical(lo, jnp.full(lo.shape, 16, I32))


def _unpack_bf16_pairs(p):
    hi = lax.bitcast_convert_type(p & jnp.int32(-65536), F32)
    lo = lax.bitcast_convert_type(lax.shift_left(p, jnp.full(p.shape, 16, I32)), F32)
    return hi, lo


def _ada_kernel(c_ref, w_ref, b_ref, o_ref):
    s = _silu(c_ref[...])
    o_ref[...] = jnp.dot(s, w_ref[...], precision=HIGHEST, preferred_element_type=F32) + b_ref[...]


def _ada(cvecs, w_ada, b_ada):
    n_out = w_ada.shape[1]
    tn = 1536
    return pl.pallas_call(
        _ada_kernel,
        grid=(n_out // tn,),
        in_specs=[
            pl.BlockSpec((SUBLANES, D_MODEL), lambda j: (0, 0)),
            pl.BlockSpec((D_MODEL, tn), lambda j: (0, j)),
            pl.BlockSpec((1, tn), lambda j: (0, j)),
        ],
        out_specs=pl.BlockSpec((SUBLANES, tn), lambda j: (0, j)),
        out_shape=jax.ShapeDtypeStruct((SUBLANES, n_out), F32),
        name="ada",
        compiler_params=_params("arbitrary"),
    )(cvecs, w_ada, b_ada.reshape(1, n_out))


def _in_proj_kernel(x_ref, pos_ref, mod_ref, n1_ref, wq_ref, wr_ref, wgh_ref, gb_ref,
                    q_ref, k_ref, v_ref, o_ref, xr_ref, gr_ref, g_ref, gt_ref):
    x = x_ref[...] + pos_ref[...]
    hn = _rms(x, n1_ref[...]) * (1.0 + mod_ref[0, 1:2, :]) + mod_ref[0, 0:1, :]
    hb = hn.astype(BF16)
    z = jnp.dot(hb, wq_ref[...], preferred_element_type=F32)
    q_ref[...] = z[:, 0:D_ML]
    k_ref[...] = z[:, D_ML:2 * D_ML] * (HEAD ** -0.5)
    v_ref[...] = z[:, 2 * D_ML:3 * D_ML]
    o_ref[...] = z[:, 3 * D_ML:4 * D_ML]
    zr = jnp.dot(hb, wr_ref[...], preferred_element_type=F32)
    xr_ref[...] = zr[:, 0:D_RG]
    gr_ref[...] = zr[:, D_RG:2 * D_RG]
    zg = zr[:, 2 * D_RG:2 * D_RG + LANES]
    h_lo = (hn - hb.astype(F32)).astype(BF16)
    g = (zg + pltpu.roll(zg, LANES - N_GATE, 1)
         + jnp.dot(h_lo, wgh_ref[...], preferred_element_type=F32) + gb_ref[...])
    col = lax.broadcasted_iota(I32, g.shape, 1)
    g = jnp.where((col & 4) != 0, -_softplus(-g), g)
    g_ref[...] = g[:, 0:N_GATE]
    gt_ref[...] = g.T[0:N_GATE, :]


def _mod_index(bm, tm, t):
    if bm > 1:
        return lambda i: ((i * tm) // t, 0, 0)
    return lambda i: (0, 0, 0)


def _in_proj(x2d, pos, mod, t, tm, norm1, wq, wr, wgh, gbias):
    n = x2d.shape[0]
    npos = pos.shape[0] // tm
    tok = lambda i: (i, 0)
    const = lambda i: (0, 0)
    f = lambda w: jax.ShapeDtypeStruct((n, w), F32)
    return pl.pallas_call(
        _in_proj_kernel,
        grid=(n // tm,),
        in_specs=[
            pl.BlockSpec((tm, D_MODEL), tok),
            pl.BlockSpec((tm, D_MODEL), lambda i: (i % npos, 0)),
            pl.BlockSpec((1, N_MOD, D_MODEL), _mod_index(mod.shape[0], tm, t)),
            pl.BlockSpec((1, D_MODEL), const),
            pl.BlockSpec((D_MODEL, 4 * D_ML), const),
            pl.BlockSpec((D_MODEL, 2 * D_RG + LANES), const),
            pl.BlockSpec((D_MODEL, LANES), const),
            pl.BlockSpec((1, LANES), const),
        ],
        out_specs=[pl.BlockSpec((tm, D_ML), tok)] * 4 + [pl.BlockSpec((tm, D_RG), tok)] * 2
        + [pl.BlockSpec((tm, N_GATE), tok), pl.BlockSpec((N_GATE, tm), lambda i: (0, i))],
        out_shape=[f(D_ML)] * 4 + [f(D_RG)] * 2 + [f(N_GATE), jax.ShapeDtypeStruct((N_GATE, n), F32)],
        name="in_proj",
        compiler_params=_params("arbitrary"),
    )(x2d, pos, mod, norm1, wq, wr, wgh, gbias)


def _mlstm_kernel(chunk, nc, emit_state, *refs):
    (qf, kf, vf, gf, gtf, qb, kb, vb, gb, gtb, c0_ref, n0_ref, m0_ref) = refs[:13]
    if emit_state:
        hf_ref, hb_ref, c_out, n_out, m_out, c_sc, n_sc, m_sc = refs[13:]
    else:
        hf_ref, hb_ref, c_sc, n_sc, m_sc = refs[13:]
    i = pl.program_id(1)

    @pl.when(i == 0)
    def _():
        c_sc[...] = c0_ref[0]
        n_sc[...] = n0_ref[0]
        m_sc[...] = m0_ref[0]

    row = lax.broadcasted_iota(I32, (chunk, chunk), 0)
    col = lax.broadcasted_iota(I32, (chunk, chunk), 1)
    hd = []
    for d, (q_ref, k_ref, v_ref, g_ref, gt_ref, h_ref) in enumerate(
            ((qf, kf, vf, gf, gtf, hf_ref), (qb, kb, vb, gb, gtb, hb_ref))):
        tri = (col <= row) if d == 0 else (col >= row)
        trif = tri.astype(F32)
        g = g_ref[...]
        gt = gt_ref[...]
        bcol = jnp.dot(trif, g, precision=HIGHEST, preferred_element_type=F32)
        brow = lax.dot_general(gt, trif, (((1,), (1,)), ((), ())), precision=HIGHEST,
                               preferred_element_type=F32)
        blast = bcol[chunk - 1:chunk, :] if d == 0 else bcol[0:1, :]
        for h in range(N_HEADS):
            ci = d * 8 + h
            cf = d * 8 + 4 + h
            j = d * N_HEADS + h
            hd.append(dict(
                j=j, sl=slice(h * HEAD, (h + 1) * HEAD), tri=tri, q_ref=q_ref, k_ref=k_ref, v_ref=v_ref,
                h_ref=h_ref, b_col=bcol[:, cf:cf + 1], b_row=brow[cf:cf + 1, :], ig_col=g[:, ci:ci + 1],
                ig_row=gt[ci:ci + 1, :], b_last=blast[:, cf:cf + 1], m_prev=m_sc[j:j + 1, 0:1],
                c_prev=c_sc[j], n_prev=n_sc[j:j + 1, :]))
    for x in hd:
        log_d = jnp.where(x['tri'], x['b_col'] - x['b_row'] + x['ig_row'], -jnp.inf)
        inter = x['b_col'] + x['m_prev']
        m_t = jnp.maximum(inter, jnp.max(log_d, axis=1, keepdims=True))
        x['dm'] = jnp.exp(log_d - m_t)
        x['w_inter'] = jnp.exp(inter - m_t)
        x['floor'] = jnp.exp(-m_t)
    for x in hd:
        x['q'] = x['q_ref'][:, x['sl']]
        x['k'] = x['k_ref'][:, x['sl']]
        x['q16'] = x['q'].astype(BF16)
        x['v16'] = x['v_ref'][:, x['sl']].astype(BF16)
        x['s'] = lax.dot_general(x['q16'], x['k'].astype(BF16), (((1,), (1,)), ((), ())),
                                 preferred_element_type=F32) * x['dm']
    for x in hd:
        num = (x['w_inter'] * jnp.dot(x['q16'], x['c_prev'].astype(BF16), preferred_element_type=F32)
               + jnp.dot(x['s'].astype(BF16), x['v16'], preferred_element_type=F32))
        den = (x['w_inter'] * jnp.sum(x['q'] * x['n_prev'], axis=1, keepdims=True)
               + jnp.sum(x['s'], axis=1, keepdims=True))
        x['h_ref'][:, x['sl']] = num / jnp.maximum(jnp.abs(den), x['floor'])
    for x in hd:
        j = x['j']
        log_w_col = x['b_last'] - x['b_col'] + x['ig_col']
        log_w_row = x['b_last'] - x['b_row'] + x['ig_row']
        m_new = jnp.maximum(x['b_last'] + x['m_prev'], jnp.max(log_w_row, axis=1, keepdims=True))
        decay = jnp.exp(x['b_last'] + x['m_prev'] - m_new)
        kw = x['k'] * jnp.exp(log_w_col - m_new)
        c_sc[j] = decay * x['c_prev'] + lax.dot_general(
            kw.astype(BF16), x['v16'], (((0,), (0,)), ((), ())), preferred_element_type=F32)
        n_sc[j:j + 1, :] = decay * x['n_prev'] + jnp.sum(kw, axis=0, keepdims=True)
        m_sc[j:j + 1, :] = jnp.broadcast_to(m_new, (1, HEAD))

    if emit_state:
        @pl.when(i == nc - 1)
        def _():
            c_out[0] = c_sc[...]
            n_out[0] = n_sc[...]
            m_out[0] = m_sc[...]


def _state_index(bm):
    if bm > 1:
        return lambda b, i: (b,) + (0,) * 3, lambda b, i: (b, 0, 0)
    return lambda b, i: (0,) * 4, lambda b, i: (0, 0, 0)


def _mlstm(q, k, v, g, gt, c0, n0, m0, b, t, chunk, emit_state):
    nc = t // chunk
    n = b * t
    nd = 2 * N_HEADS
    fwd = lambda bi, i: (bi * nc + i, 0)
    bwd = lambda bi, i: (bi * nc + nc - 1 - i, 0)
    fwd_t = lambda bi, i: (0, bi * nc + i)
    bwd_t = lambda bi, i: (0, bi * nc + nc - 1 - i)
    c_idx, n_idx = _state_index(c0.shape[0])
    tok = lambda m: [pl.BlockSpec((chunk, D_ML), m)] * 3 + [pl.BlockSpec((chunk, N_GATE), m)]
    in_specs = (tok(fwd) + [pl.BlockSpec((N_GATE, chunk), fwd_t)]
                + tok(bwd) + [pl.BlockSpec((N_GATE, chunk), bwd_t)]
                + [pl.BlockSpec((1, nd, HEAD, HEAD), c_idx),
                   pl.BlockSpec((1, nd, HEAD), n_idx), pl.BlockSpec((1, nd, HEAD), n_idx)])
    out_specs = [pl.BlockSpec((chunk, D_ML), fwd), pl.BlockSpec((chunk, D_ML), bwd)]
    out_shape = [jax.ShapeDtypeStruct((n, D_ML), F32)] * 2
    if emit_state:
        out_specs += [pl.BlockSpec((1, nd, HEAD, HEAD), lambda bi, i: (bi, 0, 0, 0)),
                      pl.BlockSpec((1, nd, HEAD), lambda bi, i: (bi, 0, 0)),
                      pl.BlockSpec((1, nd, HEAD), lambda bi, i: (bi, 0, 0))]
        out_shape += [jax.ShapeDtypeStruct((b, nd, HEAD, HEAD), F32),
                      jax.ShapeDtypeStruct((b, nd, HEAD), F32),
                      jax.ShapeDtypeStruct((b, nd, HEAD), F32)]
    return pl.pallas_call(
        functools.partial(_mlstm_kernel, chunk, nc, emit_state),
        grid=(b, nc),
        in_specs=in_specs,
        out_specs=out_specs,
        out_shape=out_shape,
        scratch_shapes=[pltpu.VMEM((nd, HEAD, HEAD), F32), pltpu.VMEM((nd, HEAD), F32),
                        pltpu.VMEM((nd, HEAD), F32)],
        name="mlstm",
        compiler_params=_params("arbitrary", "arbitrary"),
    )(q, k, v, g, gt, q, k, v, g, gt, c0, n0, m0)


def _neg_expm1(x):
    u = jnp.exp(x)
    near = jnp.where(u == 1.0, x, (u - 1.0) * x / jnp.log(u))
    return -jnp.where(x < -0.5, u - 1.0, near)


def _rglru_kernel(tb, nb, emit_state, *refs):
    (xf, xf_prev, xf_next, xb, xb_prev, xb_next, h0_ref, w_ref, bias_ref, lam_ref, cw_ref, cb_ref) = refs[:12]
    n_out = 3 if emit_state else 2
    hf_ref, hb_ref = refs[12:14]
    hfin_ref = refs[14] if emit_state else None
    carry, af_sc, uf_sc, ab_sc, ub_sc, hf_sc, pf_sc, hb_sc, pb_sc = refs[12 + n_out:]
    i = pl.program_id(1)

    @pl.when(i == 0)
    def _():
        carry[...] = h0_ref[0]

    row = lax.broadcasted_iota(I32, (tb, D_RG), 0)
    cw = cw_ref[...]
    softplus_neg_lam = _softplus(-lam_ref[...])

    def conv(main_ref, prev_ref, next_ref, first, last):
        main = main_ref[...]
        prev = jnp.where(first, 0.0, prev_ref[...])
        nxt = jnp.where(last, 0.0, next_ref[...])
        xm2 = jnp.where(row == 0, prev[6:7, :], jnp.where(row == 1, prev[7:8, :], pltpu.roll(main, 2, 0)))
        xm1 = jnp.where(row == 0, prev[7:8, :], pltpu.roll(main, 1, 0))
        xp1 = jnp.where(row == tb - 1, nxt[0:1, :], pltpu.roll(main, tb - 1, 0))
        return cb_ref[...] + xm2 * cw[0:1, :] + xm1 * cw[1:2, :] + main * cw[2:3, :] + xp1 * cw[3:4, :]

    def recurrence_terms(xc, d):
        z = jnp.dot(xc.astype(BF16), w_ref[:, d * 2 * D_RG:(d + 1) * 2 * D_RG],
                    preferred_element_type=F32) + bias_ref[:, d * 2 * D_RG:(d + 1) * 2 * D_RG]
        r = jax.nn.sigmoid(z[:, 0:D_RG])
        ig = jax.nn.sigmoid(z[:, D_RG:2 * D_RG])
        log_a = -RG_C * r * softplus_neg_lam[d:d + 1, :]
        a = jnp.exp(log_a)
        u = jnp.sqrt(_neg_expm1(2.0 * log_a)) * (ig * xc)
        return a, u

    a_f, u_f = recurrence_terms(conv(xf, xf_prev, xf_next, i == 0, i == nb - 1), 0)
    a_b, u_b = recurrence_terms(conv(xb, xb_prev, xb_next, i == nb - 1, i == 0), 1)
    seg = tb // SUBLANES
    pitch = seg + RG_SEG_PAD
    ncol = D_RG // LANES
    for lc in range(ncol):
        lanes = slice(lc * LANES, (lc + 1) * LANES)
        for s in range(SUBLANES):
            src = slice(s * seg, (s + 1) * seg)
            dst = slice(s * pitch, s * pitch + seg)
            af_sc[lc, dst, :], uf_sc[lc, dst, :] = a_f[src, lanes], u_f[src, lanes]
            ab_sc[lc, dst, :], ub_sc[lc, dst, :] = a_b[src, lanes], u_b[src, lanes]
    slab = lambda k: (slice(None), pl.ds(k, SUBLANES, stride=pitch), slice(None))
    hf = jnp.zeros((ncol, SUBLANES, LANES), F32)
    hb = jnp.zeros((ncol, SUBLANES, LANES), F32)
    pf = jnp.ones((ncol, SUBLANES, LANES), F32)
    pb = jnp.ones((ncol, SUBLANES, LANES), F32)
    for k in range(seg):
        kb = seg - 1 - k
        ak = af_sc[slab(k)]
        hf = ak * hf + uf_sc[slab(k)]
        pf = pf * ak
        hf_sc[slab(k)] = hf
        pf_sc[slab(k)] = pf
        ak = ab_sc[slab(kb)]
        hb = ak * hb + ub_sc[slab(kb)]
        pb = pb * ak
        hb_sc[slab(kb)] = hb
        pb_sc[slab(kb)] = pb
    for lc in range(ncol):
        lanes = slice(lc * LANES, (lc + 1) * LANES)
        c = carry[0:1, lanes]
        cin_f = []
        for s in range(SUBLANES):
            cin_f.append(c)
            c = pf[lc, s:s + 1, :] * c + hf[lc, s:s + 1, :]
        carry[0:1, lanes] = c
        c = carry[1:2, lanes]
        cin_b = [None] * SUBLANES
        for s in reversed(range(SUBLANES)):
            cin_b[s] = c
            c = pb[lc, s:s + 1, :] * c + hb[lc, s:s + 1, :]
        carry[1:2, lanes] = c
        for s in range(SUBLANES):
            rows = slice(s * seg, (s + 1) * seg)
            src = slice(s * pitch, s * pitch + seg)
            hf_ref[rows, lanes] = hf_sc[lc, src, :] + pf_sc[lc, src, :] * cin_f[s]
            hb_ref[rows, lanes] = hb_sc[lc, src, :] + pb_sc[lc, src, :] * cin_b[s]

    if emit_state:
        @pl.when(i == nb - 1)
        def _():
            hfin_ref[0] = carry[...]


def _rglru(xr, h0, wbd, bias, lam, cw, cb, b, t, tb, emit_state):
    nb = t // tb
    n = b * t
    r8 = tb // SUBLANES
    last8 = n // SUBLANES - 1
    fwd = lambda bi, i: (bi * nb + i, 0)
    bwd = lambda bi, i: (bi * nb + nb - 1 - i, 0)
    fwd_prev = lambda bi, i: (jnp.maximum((bi * nb + i) * r8 - 1, 0), 0)
    fwd_next = lambda bi, i: (jnp.minimum((bi * nb + i + 1) * r8, last8), 0)
    bwd_prev = lambda bi, i: (jnp.maximum((bi * nb + nb - 1 - i) * r8 - 1, 0), 0)
    bwd_next = lambda bi, i: (jnp.minimum((bi * nb + nb - i) * r8, last8), 0)
    const = lambda bi, i: (0, 0)
    h_idx = (lambda bi, i: (bi, 0, 0)) if h0.shape[0] > 1 else (lambda bi, i: (0, 0, 0))
    halo = lambda m: pl.BlockSpec((SUBLANES, D_RG), m)
    in_specs = [pl.BlockSpec((tb, D_RG), fwd), halo(fwd_prev), halo(fwd_next),
                pl.BlockSpec((tb, D_RG), bwd), halo(bwd_prev), halo(bwd_next),
                pl.BlockSpec((1, 2, D_RG), h_idx),
                pl.BlockSpec((D_RG, 4 * D_RG), const), pl.BlockSpec((1, 4 * D_RG), const),
                pl.BlockSpec((2, D_RG), const), pl.BlockSpec((4, D_RG), const), pl.BlockSpec((1, D_RG), const)]
    out_specs = [pl.BlockSpec((tb, D_RG), fwd), pl.BlockSpec((tb, D_RG), bwd)]
    out_shape = [jax.ShapeDtypeStruct((n, D_RG), F32)] * 2
    if emit_state:
        out_specs.append(pl.BlockSpec((1, 2, D_RG), lambda bi, i: (bi, 0, 0)))
        out_shape.append(jax.ShapeDtypeStruct((b, 2, D_RG), F32))
    return pl.pallas_call(
        functools.partial(_rglru_kernel, tb, nb, emit_state),
        grid=(b, nb),
        in_specs=in_specs,
        out_specs=out_specs,
        out_shape=out_shape,
        scratch_shapes=[pltpu.VMEM((2, D_RG), F32)]
        + [pltpu.VMEM((D_RG // LANES, tb + SUBLANES * RG_SEG_PAD, LANES), F32)] * 8,
        name="rglru",
        compiler_params=_params("arbitrary", "arbitrary"),
    )(xr, xr, xr, xr, xr, xr, h0, wbd, bias, lam, cw, cb)


def _route(s, sb):
    tm = s.shape[1]
    neg = -jnp.inf
    sub = lax.broadcasted_iota(I32, (GROUP, tm), 0)
    blocks = [sb[gi * GROUP:(gi + 1) * GROUP, :] for gi in range(N_GROUPS)]
    gscore = []
    for blk in blocks:
        m1 = jnp.max(blk, axis=0, keepdims=True)
        first = jnp.min(jnp.where(blk == m1, sub, GROUP), axis=0, keepdims=True)
        m2 = jnp.max(jnp.where(sub == first, neg, blk), axis=0, keepdims=True)
        gscore.append(m1 + m2)
    masked = []
    for gi in range(N_GROUPS):
        rank = jnp.zeros((1, tm), F32)
        for gj in range(N_GROUPS):
            if gj == gi:
                continue
            ahead = (gscore[gj] >= gscore[gi]) if gj < gi else (gscore[gj] > gscore[gi])
            rank = rank + jnp.where(ahead, 1.0, 0.0)
        masked.append(jnp.where(rank < TOPK_GROUPS, blocks[gi], neg))
    v = jnp.concatenate(masked, axis=0)
    eid = lax.broadcasted_iota(I32, (N_EXPERTS, tm), 0)
    sel = jnp.zeros((N_EXPERTS, tm), F32)
    picks = []
    for _ in range(TOP_K):
        mx = jnp.max(v, axis=0, keepdims=True)
        idx = jnp.min(jnp.where(v == mx, eid, N_EXPERTS), axis=0, keepdims=True)
        pick = eid == idx
        picks.append(pick)
        sel = jnp.where(pick, 1.0, sel)
        v = jnp.where(pick, neg, v)
    ws = s * sel
    return ws / jnp.sum(ws, axis=0, keepdims=True) * ROUTED_SCALE, sel, picks


def _mix_out_kernel(hmf_ref, hmb_ref, o_ref, hrf_ref, hrb_ref, gr_ref, x_ref, pos_ref, mod_ref,
                    mln_ref, rgn_ref, wo_ml_ref, wo_rg_ref, n2_ref, rwt_ref, rb_ref, cnt0_ref,
                    x1_ref, hn2p_ref, ek_ref, pk_ref, wtok_ref, cnt_ref, cnt_sc):
    i = pl.program_id(0)
    tm = x_ref.shape[0]

    @pl.when(i == 0)
    def _():
        cnt_sc[...] = cnt0_ref[...].astype(F32)

    hm = hmf_ref[...] + hmb_ref[...]
    heads = []
    for h in range(N_HEADS):
        seg = hm[:, h * HEAD:(h + 1) * HEAD]
        heads.append(seg * lax.rsqrt(jnp.mean(seg * seg, axis=-1, keepdims=True) + EPS))
    y_ml = jnp.concatenate(heads, axis=1) * mln_ref[...] * jax.nn.sigmoid(o_ref[...])
    y_rg = _rms(hrf_ref[...] + hrb_ref[...], rgn_ref[...]) * jax.nn.gelu(gr_ref[...])
    mix = (jnp.dot(y_ml.astype(BF16), wo_ml_ref[...], preferred_element_type=F32)
           + jnp.dot(y_rg.astype(BF16), wo_rg_ref[...], preferred_element_type=F32))
    x1 = x_ref[...] + pos_ref[...] + mod_ref[0, 2:3, :] * mix
    x1_ref[...] = x1
    hn2 = _rms(x1, n2_ref[...]) * (1.0 + mod_ref[0, 4:5, :]) + mod_ref[0, 3:4, :]
    hn2p_ref[...] = _pack_bf16_pairs(hn2)
    logits_t = lax.dot_general(rwt_ref[...], hn2, (((1,), (1,)), ((), ())), precision=HIGHEST,
                               preferred_element_type=F32)
    s = jax.nn.sigmoid(logits_t)
    wt, sel, picks = _route(s, s + rb_ref[...])

    earlier = (lax.broadcasted_iota(I32, (tm, tm), 0) < lax.broadcasted_iota(I32, (tm, tm), 1))
    prefix = jnp.dot(sel.astype(BF16), earlier.astype(BF16), preferred_element_type=F32)
    pos_all = cnt_sc[:, 0:1] + prefix
    eid = lax.broadcasted_iota(I32, (N_EXPERTS, tm), 0)
    eid_f = eid.astype(F32)
    row8 = lax.broadcasted_iota(I32, (TOP_K, tm), 0)
    ek = jnp.zeros((TOP_K, tm), F32)
    pk = jnp.zeros((TOP_K, tm), F32)
    wk = jnp.zeros((N_EXPERTS, tm), F32)
    for k, pick in enumerate(picks):
        take = lambda a: jnp.sum(jnp.where(pick, a, 0.0), axis=0, keepdims=True)
        ek = jnp.where(row8 == k, take(eid_f), ek)
        pk = jnp.where(row8 == k, take(pos_all), pk)
        wk = jnp.where(eid == k, take(wt), wk)
    ek_ref[...] = ek.astype(I32)
    pk_ref[...] = pk.astype(I32)
    wtok_ref[...] = wk.T
    cnt_sc[...] += jnp.broadcast_to(jnp.sum(sel, axis=1, keepdims=True), cnt_sc.shape)

    @pl.when(i == pl.num_programs(0) - 1)
    def _():
        cnt_ref[...] = cnt_sc[...].astype(I32)


def _mix_out(hmf, hmb, o, hrf, hrb, gr, x2d, pos, mod, t, tm, mln, rgn, wo_ml, wo_rg, norm2, rwt, rbias, cnt0):
    n = x2d.shape[0]
    npos = pos.shape[0] // tm
    tok = lambda i: (i, 0)
    tok_t = lambda i: (0, i)
    const = lambda i: (0, 0)
    return pl.pallas_call(
        _mix_out_kernel,
        grid=(n // tm,),
        in_specs=[pl.BlockSpec((tm, D_ML), tok)] * 6 + [
            pl.BlockSpec((tm, D_MODEL), tok),
            pl.BlockSpec((tm, D_MODEL), lambda i: (i % npos, 0)),
            pl.BlockSpec((1, N_MOD, D_MODEL), _mod_index(mod.shape[0], tm, t)),
            pl.BlockSpec((1, D_ML), const), pl.BlockSpec((1, D_RG), const),
            pl.BlockSpec((D_ML, D_MODEL), const), pl.BlockSpec((D_RG, D_MODEL), const),
            pl.BlockSpec((1, D_MODEL), const),
            pl.BlockSpec((N_EXPERTS, D_MODEL), const), pl.BlockSpec((N_EXPERTS, 1), const),
            pl.BlockSpec((N_EXPERTS, LANES), const),
        ],
        out_specs=[pl.BlockSpec((tm, D_MODEL), tok), pl.BlockSpec((tm, D_PACK), tok),
                   pl.BlockSpec((TOP_K, tm), tok_t), pl.BlockSpec((TOP_K, tm), tok_t),
                   pl.BlockSpec((tm, N_EXPERTS), tok), pl.BlockSpec((N_EXPERTS, LANES), const)],
        out_shape=[jax.ShapeDtypeStruct((n, D_MODEL), F32), jax.ShapeDtypeStruct((n, D_PACK), I32),
                   jax.ShapeDtypeStruct((TOP_K, n), I32), jax.ShapeDtypeStruct((TOP_K, n), I32),
                   jax.ShapeDtypeStruct((n, N_EXPERTS), F32), jax.ShapeDtypeStruct((N_EXPERTS, LANES), I32)],
        scratch_shapes=[pltpu.VMEM((N_EXPERTS, LANES), F32)],
        name="mix_out",
        compiler_params=_params("arbitrary"),
    )(hmf, hmb, o, hrf, hrb, gr, x2d, pos, mod, mln, rgn, wo_ml, wo_rg, norm2, rwt, rbias, cnt0)


def _sc_mesh():
    return plsc.VectorSubcoreMesh(core_axis_name="core", subcore_axis_name="subcore")


def _sc_worker():
    info = plsc.get_sparse_core_info()
    return lax.axis_index("subcore") * info.num_cores + lax.axis_index("core"), info.num_cores * info.num_subcores


def _sc_dispatch(xp, dest3, n_slots):
    n, w = xp.shape
    nwin = n // SC_WINDOW

    @pl.kernel(out_type=jax.ShapeDtypeStruct((n_slots, w), xp.dtype), mesh=_sc_mesh(),
               scratch_types=[pltpu.VMEM((SC_WINDOW, w), xp.dtype), pltpu.VMEM((TOP_K, SC_WINDOW), I32)],
               name="sc_dispatch")
    def k(x_hbm, i_hbm, o_hbm, x_v, i_v):
        wid, nworkers = _sc_worker()
        per = nwin // nworkers

        @pl.loop(0, per)
        def _(s):
            win = wid * per + s
            pltpu.sync_copy(x_hbm.at[pl.ds(win * SC_WINDOW, SC_WINDOW)], x_v)
            pltpu.sync_copy(i_hbm.at[win], i_v)
            for j in range(TOP_K):
                pltpu.sync_copy(x_v, o_hbm.at[i_v.at[j]])

    return k(xp, dest3)


def _sc_combine_gather(ys, dest3):
    nwin = dest3.shape[0]
    w = ys.shape[1]

    @pl.kernel(out_type=jax.ShapeDtypeStruct((nwin, TOP_K, SC_WINDOW, w), ys.dtype), mesh=_sc_mesh(),
               scratch_types=[pltpu.VMEM((SC_WINDOW, w), ys.dtype), pltpu.VMEM((TOP_K, SC_WINDOW), I32)],
               name="sc_combine")
    def k(y_hbm, i_hbm, o_hbm, y_v, i_v):
        wid, nworkers = _sc_worker()
        per = nwin // nworkers

        @pl.loop(0, per)
        def _(s):
            win = wid * per + s
            pltpu.sync_copy(i_hbm.at[win], i_v)
            for j in range(TOP_K):
                pltpu.sync_copy(y_hbm.at[i_v.at[j]], y_v)
                pltpu.sync_copy(y_v, o_hbm.at[win, j])

    return k(ys, dest3)


def _swiglu(x, w13):
    h = jnp.dot(x, w13, preferred_element_type=F32)
    return _silu(h[:, 0:D_EXPERT]) * h[:, D_EXPERT:2 * D_EXPERT]


def _unpack_rows_bf16(p):
    hi, lo = _unpack_bf16_pairs(p)
    return jnp.concatenate([hi.astype(BF16), lo.astype(BF16)], axis=1)


def _expert_kernel(rows, be_ref, nu_ref, x_ref, *refs):
    y_ref = refs[-1]

    @pl.when(pl.program_id(0) * EXPERT_BLOCKS_PER_STEP < nu_ref[0])
    def _():
        for g in range(EXPERT_BLOCKS_PER_STEP):
            w13_ref, w2_ref = refs[2 * g], refs[2 * g + 1]
            sl = slice(g * rows, (g + 1) * rows)
            h = _swiglu(_unpack_rows_bf16(x_ref[sl, :]), w13_ref[0])
            y_ref[sl, :] = _pack_bf16_pairs(jnp.dot(h.astype(BF16), w2_ref[0], preferred_element_type=F32))


def _experts(xs, block_expert, n_used, w13, w2, rows):
    g = EXPERT_BLOCKS_PER_STEP
    nb = xs.shape[0] // rows
    w_specs = []
    for j in range(g):
        w_specs += [pl.BlockSpec((1, D_MODEL, 2 * D_EXPERT), lambda b, be, nu, j=j: (be[b * g + j], 0, 0)),
                    pl.BlockSpec((1, D_EXPERT, D_MODEL), lambda b, be, nu, j=j: (be[b * g + j], 0, 0))]
    return pl.pallas_call(
        functools.partial(_expert_kernel, rows),
        grid_spec=pltpu.PrefetchScalarGridSpec(
            num_scalar_prefetch=2,
            grid=(nb // g,),
            in_specs=[pl.BlockSpec((g * rows, D_PACK), lambda b, be, nu: (b, 0))] + w_specs,
            out_specs=pl.BlockSpec((g * rows, D_PACK), lambda b, be, nu: (b, 0)),
        ),
        out_shape=jax.ShapeDtypeStruct(xs.shape, I32),
        name="experts",
        compiler_params=_params("arbitrary"),
    )(block_expert, n_used, xs, *([w13, w2] * g))


def _moe_out_kernel(yk_ref, hn2p_ref, wtok_ref, sw13_ref, sw2_ref, x1_ref, mod_ref, nf_ref, y_ref):
    shared = jnp.dot(_swiglu(_unpack_rows_bf16(hn2p_ref[...]), sw13_ref[...]).astype(BF16), sw2_ref[...],
                     preferred_element_type=F32)
    w = wtok_ref[...]
    parts = []
    for wi in range(yk_ref.shape[0]):
        rows = slice(wi * SC_WINDOW, (wi + 1) * SC_WINDOW)
        a_hi = shared[rows, 0:D_PACK]
        a_lo = shared[rows, D_PACK:D_MODEL]
        for k in range(TOP_K):
            y_hi, y_lo = _unpack_bf16_pairs(yk_ref[wi, k])
            wc = w[rows, k:k + 1]
            a_hi = a_hi + wc * y_hi
            a_lo = a_lo + wc * y_lo
        parts.append(jnp.concatenate([a_hi, a_lo], axis=1))
    x2 = x1_ref[...] + mod_ref[0, 5:6, :] * jnp.concatenate(parts, axis=0)
    y_ref[...] = _rms(x2, nf_ref[...])


def _moe_out(yk, tok0, hn2p, wtok, sw13, sw2, x1, mod, t, tm, norm_final):
    n = hn2p.shape[0]
    tok = lambda i: (i, 0)
    const = lambda i: (0, 0)
    blk0 = tok0 // tm
    return pl.pallas_call(
        _moe_out_kernel,
        grid=(n // tm,),
        in_specs=[
            pl.BlockSpec((tm // SC_WINDOW, TOP_K, SC_WINDOW, D_PACK), lambda i: (i + blk0, 0, 0, 0)),
            pl.BlockSpec((tm, D_PACK), tok),
            pl.BlockSpec((tm, N_EXPERTS), tok),
            pl.BlockSpec((D_MODEL, 2 * D_EXPERT), const),
            pl.BlockSpec((D_EXPERT, D_MODEL), const),
            pl.BlockSpec((tm, D_MODEL), tok),
            pl.BlockSpec((1, N_MOD, D_MODEL), _mod_index(mod.shape[0], tm, t)),
            pl.BlockSpec((1, D_MODEL), const),
        ],
        out_specs=pl.BlockSpec((tm, D_MODEL), tok),
        out_shape=jax.ShapeDtypeStruct((n, D_MODEL), F32),
        name="moe_out",
        compiler_params=_params("arbitrary"),
    )(yk, hn2p, wtok, sw13, sw2, x1, mod, norm_final)


def _dispatch_plan(cnt, ek, pk, rows):
    n = ek.shape[1]
    nb = n * TOP_K // rows + N_EXPERTS
    padded = (cnt + rows - 1) // rows * rows
    ends = jnp.cumsum(padded)
    experts = jnp.arange(N_EXPERTS, dtype=I32)
    first_row = jnp.sum(jnp.where(ek[:, :, None] == experts, ends - padded, 0), axis=-1)
    dest3 = (first_row + pk).reshape(TOP_K, n // SC_WINDOW, SC_WINDOW).transpose(1, 0, 2)
    block_end = ends // rows
    block_expert = jnp.minimum(jnp.sum(block_end[None, :] <= jnp.arange(nb, dtype=I32)[:, None], axis=1),
                               N_EXPERTS - 1).astype(I32)
    return dest3, block_expert, block_end[-1:].astype(I32), nb * rows


def _grid_pos_embed(n_tokens, dim):
    rows_n = n_tokens // GRID_W
    row = jnp.repeat(jnp.arange(rows_n), GRID_W).astype(F32)
    col = jnp.tile(jnp.arange(GRID_W), rows_n).astype(F32)
    quarter = dim // 4
    omega = 1.0 / (POS_BASE ** (jnp.arange(quarter, dtype=F32) / quarter))
    ra = row[:, None] * omega
    ca = col[:, None] * omega
    return jnp.concatenate([jnp.sin(ra), jnp.cos(ra), jnp.sin(ca), jnp.cos(ca)], axis=-1)


def _block_diag(w):
    eye = jnp.eye(N_RG_BLOCKS, dtype=w.dtype)
    return jnp.einsum('nij,nm->nimj', w, eye).reshape(D_RG, D_RG)


def _layer_weights(l, norm1, w_in, mlstm_gate_bias, mlstm_norm, rg_conv_w, rg_conv_b, rg_wa, rg_ba, rg_wx,
                   rg_bx, rg_lambda, rg_norm, w_out, norm2, router_w, router_bias, exp_w1, exp_w3, exp_w2,
                   shared_w1, shared_w3, shared_w2):
    wi = w_in[l]
    c0, c1 = 4 * D_ML, 4 * D_ML + N_GATE
    wg = wi[:, c0:c1]
    wg_hi = wg.astype(BF16)
    wg_lo = (wg - wg_hi.astype(F32)).astype(BF16)
    zcols = lambda w: jnp.zeros((D_MODEL, w), BF16)
    return dict(
        norm1=norm1[l].reshape(1, D_MODEL),
        wq=wi[:, :c0].astype(BF16),
        wr=jnp.concatenate([wi[:, c1:].astype(BF16), wg_hi, wg_lo, zcols(LANES - 2 * N_GATE)], axis=1),
        wgh=jnp.concatenate([wg_hi, zcols(LANES - N_GATE)], axis=1),
        gbias=jnp.pad(mlstm_gate_bias[l].reshape(1, N_GATE), ((0, 0), (0, LANES - N_GATE))),
        mln=mlstm_norm[l].reshape(1, D_ML),
        cw=rg_conv_w[l], cb=rg_conv_b[l].reshape(1, D_RG),
        wbd=jnp.concatenate([_block_diag(rg_wa[l, 0]), _block_diag(rg_wx[l, 0]),
                             _block_diag(rg_wa[l, 1]), _block_diag(rg_wx[l, 1])], axis=1).astype(BF16),
        rbias=jnp.concatenate([rg_ba[l, 0], rg_bx[l, 0], rg_ba[l, 1], rg_bx[l, 1]]).reshape(1, 4 * D_RG),
        lam=rg_lambda[l], rgn=rg_norm[l].reshape(1, D_RG),
        wo_ml=w_out[l, :D_ML].astype(BF16), wo_rg=w_out[l, D_ML:].astype(BF16),
        norm2=norm2[l].reshape(1, D_MODEL),
        rwt=router_w[l].T, rb=router_bias[l].reshape(N_EXPERTS, 1),
        w13=jnp.concatenate([exp_w1[l], exp_w3[l]], axis=-1).astype(BF16),
        w2=exp_w2[l].astype(BF16),
        sw13=jnp.concatenate([shared_w1[l], shared_w3[l]], axis=-1).astype(BF16),
        sw2=shared_w2[l].astype(BF16),
    )


def _mixers(x2d, pos, mod, c0, n0, m0, h0, cnt0, lw, b, t, emit_state):
    tl = _tiles(t, mod.shape[0] > 1)
    tm = tl['tok']
    q, k, v, o, xr, gr, g, gt = _in_proj(x2d, pos, mod, t, tm, lw['norm1'], lw['wq'], lw['wr'], lw['wgh'],
                                         lw['gbias'])
    ml = _mlstm(q, k, v, g, gt, c0, n0, m0, b, t, tl['chunk'], emit_state)
    rg = _rglru(xr, h0, lw['wbd'], lw['rbias'], lw['lam'], lw['cw'], lw['cb'], b, t, tl['scan'], emit_state)
    routed = _mix_out(ml[0], ml[1], o, rg[0], rg[1], gr, x2d, pos, mod, t, tm, lw['mln'], lw['rgn'],
                      lw['wo_ml'], lw['wo_rg'], lw['norm2'], lw['rwt'], lw['rb'], cnt0)
    return routed, ml[2:], rg[2:]


def _routed_experts(paths, lw):
    cnt = paths[-1][5][:, 0]
    hn2p = jnp.concatenate([p[1] for p in paths], axis=0)
    ek = jnp.concatenate([p[2] for p in paths], axis=1)
    pk = jnp.concatenate([p[3] for p in paths], axis=1)
    dest3, block_expert, n_used, n_slots = _dispatch_plan(cnt, ek, pk, EXPERT_ROWS)
    xs = _sc_dispatch(hn2p, dest3, n_slots)
    ys = _experts(xs, block_expert, n_used, lw['w13'], lw['w2'], EXPERT_ROWS)
    return _sc_combine_gather(ys, dest3)


def kernel(x_prompt, x_sample, c, state_mlstm_C, state_mlstm_n, state_mlstm_m, state_rglru_h, c_ctx, w_ada, b_ada, norm1, w_in, mlstm_gate_bias, mlstm_norm, rg_conv_w, rg_conv_b, rg_wa, rg_ba, rg_wx, rg_bx, rg_lambda, rg_norm, w_out, norm2, router_w, router_bias, exp_w1, exp_w3, exp_w2, shared_w1, shared_w3, shared_w2, norm_final):
    bp, tp, _ = x_prompt.shape
    bs, ts, _ = x_sample.shape
    depth = w_ada.shape[0]
    assert depth == 1, "the final norm is fused into the single layer's MoE output kernel"
    nd = 2 * N_HEADS
    l = 0
    lw = _layer_weights(l, norm1, w_in, mlstm_gate_bias, mlstm_norm, rg_conv_w, rg_conv_b, rg_wa, rg_ba, rg_wx,
                        rg_bx, rg_lambda, rg_norm, w_out, norm2, router_w, router_bias, exp_w1, exp_w3, exp_w2,
                        shared_w1, shared_w3, shared_w2)
    nf = norm_final.reshape(1, D_MODEL)
    cvecs = jnp.concatenate([c_ctx[None], c, jnp.zeros((SUBLANES - 1 - bs, D_MODEL), F32)], axis=0)
    mod = _ada(cvecs, w_ada[l], b_ada[l]).reshape(SUBLANES, N_MOD, D_MODEL)

    mod_p, mod_s = mod[0:1], mod[1:1 + bs]
    tm_p, tm_s = _tiles(tp, False)['tok'], _tiles(ts, True)['tok']
    rp, (cc, nc_, mc), (hc,) = _mixers(
        x_prompt.reshape(bp * tp, D_MODEL), jnp.zeros((tm_p, D_MODEL), F32), mod_p,
        jnp.zeros((1, nd, HEAD, HEAD), F32), jnp.zeros((1, nd, HEAD), F32), jnp.zeros((1, nd, HEAD), F32),
        jnp.zeros((1, 2, D_RG), F32), jnp.zeros((N_EXPERTS, LANES), I32), lw, bp, tp, True)
    rs, _, _ = _mixers(
        x_sample.reshape(bs * ts, D_MODEL), _grid_pos_embed(ts, D_MODEL), mod_s,
        state_mlstm_C[:, l].reshape(bs, nd, HEAD, HEAD), state_mlstm_n[:, l].reshape(bs, nd, HEAD),
        jnp.broadcast_to(state_mlstm_m[:, l].reshape(bs, nd, 1), (bs, nd, HEAD)),
        state_rglru_h[:, l], rp[5], lw, bs, ts, False)
    yk = _routed_experts([rp, rs], lw)
    yp = _moe_out(yk, 0, rp[1], rp[4], lw['sw13'], lw['sw2'], rp[0], mod_p, tp, tm_p, nf)
    ys = _moe_out(yk, bp * tp, rs[1], rs[4], lw['sw13'], lw['sw2'], rs[0], mod_s, ts, tm_s, nf)

    y_prompt = yp.reshape(bp, tp, D_MODEL)
    y_sample = ys.reshape(bs, ts, D_MODEL)
    new_c = cc.reshape(bp, 1, 2, N_HEADS, HEAD, HEAD)
    new_n = nc_.reshape(bp, 1, 2, N_HEADS, HEAD)
    new_m = mc[:, :, 0].reshape(bp, 1, 2, N_HEADS)
    new_h = hc.reshape(bp, 1, 2, D_RG)
    return (y_prompt, y_sample, new_c, new_n, new_m, new_h)
```

```python
import functools

import jax
import jax.numpy as jnp
from jax import lax
from jax.experimental import pallas as pl
from jax.experimental.pallas import tpu as pltpu
from jax.experimental.pallas import tpu_sc as plsc

F32 = jnp.float32
BF16 = jnp.bfloat16
I32 = jnp.int32
HIGHEST = lax.Precision.HIGHEST

D_MODEL = 1024
N_MOD = 6
D_ML = 512
N_HEADS = 4
HEAD = 128
D_RG = 512
N_RG_BLOCKS = 8
RG_BLOCK = 64
RG_C = 8.0
N_GATE = 16
N_EXPERTS = 64
N_GROUPS = 8
GROUP = 8
TOPK_GROUPS = 4
TOP_K = 8
D_EXPERT = 256
ROUTED_SCALE = 2.5
EPS = 1e-6
GRID_W = 64
POS_BASE = 10000.0

RG_SEG_PAD = 8
SC_WINDOW = 128
D_PACK = D_MODEL // 2
EXPERT_ROWS = 512
EXPERT_BLOCKS_PER_STEP = 4

SUBLANES = 8
LANES = 128
VMEM_LIMIT = 48 * 1024 * 1024
VMEM_LIMIT_EXPERTS = 56 * 1024 * 1024


def _params(*sem, vmem=VMEM_LIMIT):
    return pltpu.CompilerParams(dimension_semantics=sem, vmem_limit_bytes=vmem)


def _tiles(t, per_sequence_mod):
    cap = t if per_sequence_mod else 1 << 30
    return dict(
        tok=min(512, cap),
        chunk=min(256, t),
        scan=min(512, t),
    )


def _silu(x):
    return x * jax.nn.sigmoid(x)


def _softplus(x):
    return jnp.maximum(x, 0.0) + jnp.log1p(jnp.exp(-jnp.abs(x)))


def _rms(x, g):
    return x * lax.rsqrt(jnp.mean(x * x, axis=-1, keepdims=True) + EPS) * g


def _pack_bf16_pairs(x):
    w = x.shape[1] // 2
    hi = lax.bitcast_convert_type(x[:, :w].astype(BF16).astype(F32), I32)
    lo = lax.bitcast_convert_type(x[:, w:].astype(BF16).astype(F32), I32)
    return hi | lax.shift_right_logical(lo, jnp.full(lo.shape, 16, I32))


def _unpack_bf16_pairs(p):
    hi = lax.bitcast_convert_type(p & jnp.int32(-65536), F32)
    lo = lax.bitcast_convert_type(lax.shift_left(p, jnp.full(p.shape, 16, I32)), F32)
    return hi, lo


def _ada_kernel(c_ref, w_ref, b_ref, o_ref):
    s = _silu(c_ref[...])
    o_ref[...] = jnp.dot(s, w_ref[...], precision=HIGHEST, preferred_element_type=F32) + b_ref[...]


def _ada(cvecs, w_ada, b_ada):
    n_out = w_ada.shape[1]
    tn = 1536
    return pl.pallas_call(
        _ada_kernel,
        grid=(n_out // tn,),
        in_specs=[
            pl.BlockSpec((SUBLANES, D_MODEL), lambda j: (0, 0)),
            pl.BlockSpec((D_MODEL, tn), lambda j: (0, j)),
            pl.BlockSpec((1, tn), lambda j: (0, j)),
        ],
        out_specs=pl.BlockSpec((SUBLANES, tn), lambda j: (0, j)),
        out_shape=jax.ShapeDtypeStruct((SUBLANES, n_out), F32),
        name="ada",
        compiler_params=_params("arbitrary"),
    )(cvecs, w_ada, b_ada.reshape(1, n_out))


def _in_proj_kernel(x_ref, pos_ref, mod_ref, n1_ref, wq_ref, wr_ref, wgh_ref, gb_ref,
                    q_ref, k_ref, v_ref, o_ref, xr_ref, gr_ref, g_ref, gt_ref):
    x = x_ref[...] + pos_ref[...]
    hn = _rms(x, n1_ref[...]) * (1.0 + mod_ref[0, 1:2, :]) + mod_ref[0, 0:1, :]
    hb = hn.astype(BF16)
    z = jnp.dot(hb, wq_ref[...], preferred_element_type=F32)
    q_ref[...] = z[:, 0:D_ML]
    k_ref[...] = z[:, D_ML:2 * D_ML] * (HEAD ** -0.5)
    v_ref[...] = z[:, 2 * D_ML:3 * D_ML]
    o_ref[...] = z[:, 3 * D_ML:4 * D_ML]
    zr = jnp.dot(hb, wr_ref[...], preferred_element_type=F32)
    xr_ref[...] = zr[:, 0:D_RG]
    gr_ref[...] = zr[:, D_RG:2 * D_RG]
    zg = zr[:, 2 * D_RG:2 * D_RG + LANES]
    h_lo = (hn - hb.astype(F32)).astype(BF16)
    g = (zg + pltpu.roll(zg, LANES - N_GATE, 1)
         + jnp.dot(h_lo, wgh_ref[...], preferred_element_type=F32) + gb_ref[...])
    col = lax.broadcasted_iota(I32, g.shape, 1)
    g = jnp.where((col & 4) != 0, -_softplus(-g), g)
    g_ref[...] = g[:, 0:N_GATE]
    gt_ref[...] = g.T[0:N_GATE, :]


def _mod_index(bm, tm, t):
    if bm > 1:
        return lambda i: ((i * tm) // t, 0, 0)
    return lambda i: (0, 0, 0)


def _in_proj(x2d, pos, mod, t, tm, norm1, wq, wr, wgh, gbias):
    n = x2d.shape[0]
    npos = pos.shape[0] // tm
    tok = lambda i: (i, 0)
    const = lambda i: (0, 0)
    f = lambda w: jax.ShapeDtypeStruct((n, w), F32)
    return pl.pallas_call(
        _in_proj_kernel,
        grid=(n // tm,),
        in_specs=[
            pl.BlockSpec((tm, D_MODEL), tok),
            pl.BlockSpec((tm, D_MODEL), lambda i: (i % npos, 0)),
            pl.BlockSpec((1, N_MOD, D_MODEL), _mod_index(mod.shape[0], tm, t)),
            pl.BlockSpec((1, D_MODEL), const),
            pl.BlockSpec((D_MODEL, 4 * D_ML), const),
            pl.BlockSpec((D_MODEL, 2 * D_RG + LANES), const),
            pl.BlockSpec((D_MODEL, LANES), const),
            pl.BlockSpec((1, LANES), const),
        ],
        out_specs=[pl.BlockSpec((tm, D_ML), tok)] * 4 + [pl.BlockSpec((tm, D_RG), tok)] * 2
        + [pl.BlockSpec((tm, N_GATE), tok), pl.BlockSpec((N_GATE, tm), lambda i: (0, i))],
        out_shape=[f(D_ML)] * 4 + [f(D_RG)] * 2 + [f(N_GATE), jax.ShapeDtypeStruct((N_GATE, n), F32)],
        name="in_proj",
        compiler_params=_params("arbitrary"),
    )(x2d, pos, mod, norm1, wq, wr, wgh, gbias)


def _mlstm_kernel(chunk, nc, emit_state, *refs):
    (qf, kf, vf, gf, gtf, qb, kb, vb, gb, gtb, c0_ref, n0_ref, m0_ref) = refs[:13]
    if emit_state:
        hf_ref, hb_ref, c_out, n_out, m_out, c_sc, n_sc, m_sc = refs[13:]
    else:
        hf_ref, hb_ref, c_sc, n_sc, m_sc = refs[13:]
    i = pl.program_id(1)

    @pl.when(i == 0)
    def _():
        c_sc[...] = c0_ref[0]
        n_sc[...] = n0_ref[0]
        m_sc[...] = m0_ref[0]

    row = lax.broadcasted_iota(I32, (chunk, chunk), 0)
    col = lax.broadcasted_iota(I32, (chunk, chunk), 1)
    hd = []
    for d, (q_ref, k_ref, v_ref, g_ref, gt_ref, h_ref) in enumerate(
            ((qf, kf, vf, gf, gtf, hf_ref), (qb, kb, vb, gb, gtb, hb_ref))):
        tri = (col <= row) if d == 0 else (col >= row)
        trif = tri.astype(F32)
        g = g_ref[...]
        gt = gt_ref[...]
        bcol = jnp.dot(trif, g, precision=HIGHEST, preferred_element_type=F32)
        brow = lax.dot_general(gt, trif, (((1,), (1,)), ((), ())), precision=HIGHEST,
                               preferred_element_type=F32)
        blast = bcol[chunk - 1:chunk, :] if d == 0 else bcol[0:1, :]
        for h in range(N_HEADS):
            ci = d * 8 + h
            cf = d * 8 + 4 + h
            j = d * N_HEADS + h
            hd.append(dict(
                j=j, sl=slice(h * HEAD, (h + 1) * HEAD), tri=tri, q_ref=q_ref, k_ref=k_ref, v_ref=v_ref,
                h_ref=h_ref, b_col=bcol[:, cf:cf + 1], b_row=brow[cf:cf + 1, :], ig_col=g[:, ci:ci + 1],
                ig_row=gt[ci:ci + 1, :], b_last=blast[:, cf:cf + 1], m_prev=m_sc[j:j + 1, 0:1],
                c_prev=c_sc[j], n_prev=n_sc[j:j + 1, :]))
    for x in hd:
        log_d = jnp.where(x['tri'], x['b_col'] - x['b_row'] + x['ig_row'], -jnp.inf)
        inter = x['b_col'] + x['m_prev']
        m_t = jnp.maximum(inter, jnp.max(log_d, axis=1, keepdims=True))
        x['dm'] = jnp.exp(log_d - m_t)
        x['w_inter'] = jnp.exp(inter - m_t)
        x['floor'] = jnp.exp(-m_t)
    for x in hd:
        x['q'] = x['q_ref'][:, x['sl']]
        x['k'] = x['k_ref'][:, x['sl']]
        x['q16'] = x['q'].astype(BF16)
        x['v16'] = x['v_ref'][:, x['sl']].astype(BF16)
        x['s'] = lax.dot_general(x['q16'], x['k'].astype(BF16), (((1,), (1,)), ((), ())),
                                 preferred_element_type=F32) * x['dm']
    for x in hd:
        num = (x['w_inter'] * jnp.dot(x['q16'], x['c_prev'].astype(BF16), preferred_element_type=F32)
               + jnp.dot(x['s'].astype(BF16), x['v16'], preferred_element_type=F32))
        den = (x['w_inter'] * jnp.sum(x['q'] * x['n_prev'], axis=1, keepdims=True)
               + jnp.sum(x['s'], axis=1, keepdims=True))
        x['h_ref'][:, x['sl']] = num / jnp.maximum(jnp.abs(den), x['floor'])
    for x in hd:
        j = x['j']
        log_w_col = x['b_last'] - x['b_col'] + x['ig_col']
        log_w_row = x['b_last'] - x['b_row'] + x['ig_row']
        m_new = jnp.maximum(x['b_last'] + x['m_prev'], jnp.max(log_w_row, axis=1, keepdims=True))
        decay = jnp.exp(x['b_last'] + x['m_prev'] - m_new)
        kw = x['k'] * jnp.exp(log_w_col - m_new)
        c_sc[j] = decay * x['c_prev'] + lax.dot_general(
            kw.astype(BF16), x['v16'], (((0,), (0,)), ((), ())), preferred_element_type=F32)
        n_sc[j:j + 1, :] = decay * x['n_prev'] + jnp.sum(kw, axis=0, keepdims=True)
        m_sc[j:j + 1, :] = jnp.broadcast_to(m_new, (1, HEAD))

    if emit_state:
        @pl.when(i == nc - 1)
        def _():
            c_out[0] = c_sc[...]
            n_out[0] = n_sc[...]
            m_out[0] = m_sc[...]


def _state_index(bm):
    if bm > 1:
        return lambda b, i: (b,) + (0,) * 3, lambda b, i: (b, 0, 0)
    return lambda b, i: (0,) * 4, lambda b, i: (0, 0, 0)


def _mlstm(q, k, v, g, gt, c0, n0, m0, b, t, chunk, emit_state):
    nc = t // chunk
    n = b * t
    nd = 2 * N_HEADS
    fwd = lambda bi, i: (bi * nc + i, 0)
    bwd = lambda bi, i: (bi * nc + nc - 1 - i, 0)
    fwd_t = lambda bi, i: (0, bi * nc + i)
    bwd_t = lambda bi, i: (0, bi * nc + nc - 1 - i)
    c_idx, n_idx = _state_index(c0.shape[0])
    tok = lambda m: [pl.BlockSpec((chunk, D_ML), m)] * 3 + [pl.BlockSpec((chunk, N_GATE), m)]
    in_specs = (tok(fwd) + [pl.BlockSpec((N_GATE, chunk), fwd_t)]
                + tok(bwd) + [pl.BlockSpec((N_GATE, chunk), bwd_t)]
                + [pl.BlockSpec((1, nd, HEAD, HEAD), c_idx),
                   pl.BlockSpec((1, nd, HEAD), n_idx), pl.BlockSpec((1, nd, HEAD), n_idx)])
    out_specs = [pl.BlockSpec((chunk, D_ML), fwd), pl.BlockSpec((chunk, D_ML), bwd)]
    out_shape = [jax.ShapeDtypeStruct((n, D_ML), F32)] * 2
    if emit_state:
        out_specs += [pl.BlockSpec((1, nd, HEAD, HEAD), lambda bi, i: (bi, 0, 0, 0)),
                      pl.BlockSpec((1, nd, HEAD), lambda bi, i: (bi, 0, 0)),
                      pl.BlockSpec((1, nd, HEAD), lambda bi, i: (bi, 0, 0))]
        out_shape += [jax.ShapeDtypeStruct((b, nd, HEAD, HEAD), F32),
                      jax.ShapeDtypeStruct((b, nd, HEAD), F32),
                      jax.ShapeDtypeStruct((b, nd, HEAD), F32)]
    return pl.pallas_call(
        functools.partial(_mlstm_kernel, chunk, nc, emit_state),
        grid=(b, nc),
        in_specs=in_specs,
        out_specs=out_specs,
        out_shape=out_shape,
        scratch_shapes=[pltpu.VMEM((nd, HEAD, HEAD), F32), pltpu.VMEM((nd, HEAD), F32),
                        pltpu.VMEM((nd, HEAD), F32)],
        name="mlstm",
        compiler_params=_params("arbitrary", "arbitrary"),
    )(q, k, v, g, gt, q, k, v, g, gt, c0, n0, m0)


def _neg_expm1(x):
    u = jnp.exp(x)
    near = jnp.where(u == 1.0, x, (u - 1.0) * x / jnp.log(u))
    return -jnp.where(x < -0.5, u - 1.0, near)


def _rglru_kernel(tb, nb, emit_state, *refs):
    (xf, xf_prev, xf_next, xb, xb_prev, xb_next, h0_ref, w_ref, bias_ref, lam_ref, cw_ref, cb_ref) = refs[:12]
    n_out = 3 if emit_state else 2
    hf_ref, hb_ref = refs[12:14]
    hfin_ref = refs[14] if emit_state else None
    carry, af_sc, uf_sc, ab_sc, ub_sc, hf_sc, pf_sc, hb_sc, pb_sc = refs[12 + n_out:]
    i = pl.program_id(1)

    @pl.when(i == 0)
    def _():
        carry[...] = h0_ref[0]

    row = lax.broadcasted_iota(I32, (tb, D_RG), 0)
    cw = cw_ref[...]
    softplus_neg_lam = _softplus(-lam_ref[...])

    def conv(main_ref, prev_ref, next_ref, first, last):
        main = main_ref[...]
        prev = jnp.where(first, 0.0, prev_ref[...])
        nxt = jnp.where(last, 0.0, next_ref[...])
        xm2 = jnp.where(row == 0, prev[6:7, :], jnp.where(row == 1, prev[7:8, :], pltpu.roll(main, 2, 0)))
        xm1 = jnp.where(row == 0, prev[7:8, :], pltpu.roll(main, 1, 0))
        xp1 = jnp.where(row == tb - 1, nxt[0:1, :], pltpu.roll(main, tb - 1, 0))
        return cb_ref[...] + xm2 * cw[0:1, :] + xm1 * cw[1:2, :] + main * cw[2:3, :] + xp1 * cw[3:4, :]

    def recurrence_terms(xc, d):
        z = jnp.dot(xc.astype(BF16), w_ref[:, d * 2 * D_RG:(d + 1) * 2 * D_RG],
                    preferred_element_type=F32) + bias_ref[:, d * 2 * D_RG:(d + 1) * 2 * D_RG]
        r = jax.nn.sigmoid(z[:, 0:D_RG])
        ig = jax.nn.sigmoid(z[:, D_RG:2 * D_RG])
        log_a = -RG_C * r * softplus_neg_lam[d:d + 1, :]
        a = jnp.exp(log_a)
        u = jnp.sqrt(_neg_expm1(2.0 * log_a)) * (ig * xc)
        return a, u

    a_f, u_f = recurrence_terms(conv(xf, xf_prev, xf_next, i == 0, i == nb - 1), 0)
    a_b, u_b = recurrence_terms(conv(xb, xb_prev, xb_next, i == nb - 1, i == 0), 1)
    seg = tb // SUBLANES
    pitch = seg + RG_SEG_PAD
    ncol = D_RG // LANES
    for lc in range(ncol):
        lanes = slice(lc * LANES, (lc + 1) * LANES)
        for s in range(SUBLANES):
            src = slice(s * seg, (s + 1) * seg)
            dst = slice(s * pitch, s * pitch + seg)
            af_sc[lc, dst, :], uf_sc[lc, dst, :] = a_f[src, lanes], u_f[src, lanes]
            ab_sc[lc, dst, :], ub_sc[lc, dst, :] = a_b[src, lanes], u_b[src, lanes]
    slab = lambda k: (slice(None), pl.ds(k, SUBLANES, stride=pitch), slice(None))
    hf = jnp.zeros((ncol, SUBLANES, LANES), F32)
    hb = jnp.zeros((ncol, SUBLANES, LANES), F32)
    pf = jnp.ones((ncol, SUBLANES, LANES), F32)
    pb = jnp.ones((ncol, SUBLANES, LANES), F32)
    for k in range(seg):
        kb = seg - 1 - k
        ak = af_sc[slab(k)]
        hf = ak * hf + uf_sc[slab(k)]
        pf = pf * ak
        hf_sc[slab(k)] = hf
        pf_sc[slab(k)] = pf
        ak = ab_sc[slab(kb)]
        hb = ak * hb + ub_sc[slab(kb)]
        pb = pb * ak
        hb_sc[slab(kb)] = hb
        pb_sc[slab(kb)] = pb
    for lc in range(ncol):
        lanes = slice(lc * LANES, (lc + 1) * LANES)
        c = carry[0:1, lanes]
        cin_f = []
        for s in range(SUBLANES):
            cin_f.append(c)
            c = pf[lc, s:s + 1, :] * c + hf[lc, s:s + 1, :]
        carry[0:1, lanes] = c
        c = carry[1:2, lanes]
        cin_b = [None] * SUBLANES
        for s in reversed(range(SUBLANES)):
            cin_b[s] = c
            c = pb[lc, s:s + 1, :] * c + hb[lc, s:s + 1, :]
        carry[1:2, lanes] = c
        for s in range(SUBLANES):
            rows = slice(s * seg, (s + 1) * seg)
            src = slice(s * pitch, s * pitch + seg)
            hf_ref[rows, lanes] = hf_sc[lc, src, :] + pf_sc[lc, src, :] * cin_f[s]
            hb_ref[rows, lanes] = hb_sc[lc, src, :] + pb_sc[lc, src, :] * cin_b[s]

    if emit_state:
        @pl.when(i == nb - 1)
        def _():
            hfin_ref[0] = carry[...]


def _rglru(xr, h0, wbd, bias, lam, cw, cb, b, t, tb, emit_state):
    nb = t // tb
    n = b * t
    r8 = tb // SUBLANES
    last8 = n // SUBLANES - 1
    fwd = lambda bi, i: (bi * nb + i, 0)
    bwd = lambda bi, i: (bi * nb + nb - 1 - i, 0)
    fwd_prev = lambda bi, i: (jnp.maximum((bi * nb + i) * r8 - 1, 0), 0)
    fwd_next = lambda bi, i: (jnp.minimum((bi * nb + i + 1) * r8, last8), 0)
    bwd_prev = lambda bi, i: (jnp.maximum((bi * nb + nb - 1 - i) * r8 - 1, 0), 0)
    bwd_next = lambda bi, i: (jnp.minimum((bi * nb + nb - i) * r8, last8), 0)
    const = lambda bi, i: (0, 0)
    h_idx = (lambda bi, i: (bi, 0, 0)) if h0.shape[0] > 1 else (lambda bi, i: (0, 0, 0))
    halo = lambda m: pl.BlockSpec((SUBLANES, D_RG), m)
    in_specs = [pl.BlockSpec((tb, D_RG), fwd), halo(fwd_prev), halo(fwd_next),
                pl.BlockSpec((tb, D_RG), bwd), halo(bwd_prev), halo(bwd_next),
                pl.BlockSpec((1, 2, D_RG), h_idx),
                pl.BlockSpec((D_RG, 4 * D_RG), const), pl.BlockSpec((1, 4 * D_RG), const),
                pl.BlockSpec((2, D_RG), const), pl.BlockSpec((4, D_RG), const), pl.BlockSpec((1, D_RG), const)]
    out_specs = [pl.BlockSpec((tb, D_RG), fwd), pl.BlockSpec((tb, D_RG), bwd)]
    out_shape = [jax.ShapeDtypeStruct((n, D_RG), F32)] * 2
    if emit_state:
        out_specs.append(pl.BlockSpec((1, 2, D_RG), lambda bi, i: (bi, 0, 0)))
        out_shape.append(jax.ShapeDtypeStruct((b, 2, D_RG), F32))
    return pl.pallas_call(
        functools.partial(_rglru_kernel, tb, nb, emit_state),
        grid=(b, nb),
        in_specs=in_specs,
        out_specs=out_specs,
        out_shape=out_shape,
        scratch_shapes=[pltpu.VMEM((2, D_RG), F32)]
        + [pltpu.VMEM((D_RG // LANES, tb + SUBLANES * RG_SEG_PAD, LANES), F32)] * 8,
        name="rglru",
        compiler_params=_params("arbitrary", "arbitrary"),
    )(xr, xr, xr, xr, xr, xr, h0, wbd, bias, lam, cw, cb)


def _route(s, sb):
    tm = s.shape[1]
    neg = -jnp.inf
    sub = lax.broadcasted_iota(I32, (GROUP, tm), 0)
    blocks = [sb[gi * GROUP:(gi + 1) * GROUP, :] for gi in range(N_GROUPS)]
    gscore = []
    for blk in blocks:
        m1 = jnp.max(blk, axis=0, keepdims=True)
        first = jnp.min(jnp.where(blk == m1, sub, GROUP), axis=0, keepdims=True)
        m2 = jnp.max(jnp.where(sub == first, neg, blk), axis=0, keepdims=True)
        gscore.append(m1 + m2)
    masked = []
    for gi in range(N_GROUPS):
        rank = jnp.zeros((1, tm), F32)
        for gj in range(N_GROUPS):
            if gj == gi:
                continue
            ahead = (gscore[gj] >= gscore[gi]) if gj < gi else (gscore[gj] > gscore[gi])
            rank = rank + jnp.where(ahead, 1.0, 0.0)
        masked.append(jnp.where(rank < TOPK_GROUPS, blocks[gi], neg))
    v = jnp.concatenate(masked, axis=0)
    eid = lax.broadcasted_iota(I32, (N_EXPERTS, tm), 0)
    sel = jnp.zeros((N_EXPERTS, tm), F32)
    picks = []
    for _ in range(TOP_K):
        mx = jnp.max(v, axis=0, keepdims=True)
        idx = jnp.min(jnp.where(v == mx, eid, N_EXPERTS), axis=0, keepdims=True)
        pick = eid == idx
        picks.append(pick)
        sel = jnp.where(pick, 1.0, sel)
        v = jnp.where(pick, neg, v)
    ws = s * sel
    return ws / jnp.sum(ws, axis=0, keepdims=True) * ROUTED_SCALE, sel, picks


def _mix_out_kernel(hmf_ref, hmb_ref, o_ref, hrf_ref, hrb_ref, gr_ref, x_ref, pos_ref, mod_ref,
                    mln_ref, rgn_ref, wo_ml_ref, wo_rg_ref, n2_ref, rwt_ref, rb_ref, cnt0_ref,
                    x1_ref, hn2p_ref, ek_ref, pk_ref, wtok_ref, cnt_ref, cnt_sc):
    i = pl.program_id(0)
    tm = x_ref.shape[0]

    @pl.when(i == 0)
    def _():
        cnt_sc[...] = cnt0_ref[...].astype(F32)

    hm = hmf_ref[...] + hmb_ref[...]
    heads = []
    for h in range(N_HEADS):
        seg = hm[:, h * HEAD:(h + 1) * HEAD]
        heads.append(seg * lax.rsqrt(jnp.mean(seg * seg, axis=-1, keepdims=True) + EPS))
    y_ml = jnp.concatenate(heads, axis=1) * mln_ref[...] * jax.nn.sigmoid(o_ref[...])
    y_rg = _rms(hrf_ref[...] + hrb_ref[...], rgn_ref[...]) * jax.nn.gelu(gr_ref[...])
    mix = (jnp.dot(y_ml.astype(BF16), wo_ml_ref[...], preferred_element_type=F32)
           + jnp.dot(y_rg.astype(BF16), wo_rg_ref[...], preferred_element_type=F32))
    x1 = x_ref[...] + pos_ref[...] + mod_ref[0, 2:3, :] * mix
    x1_ref[...] = x1
    hn2 = _rms(x1, n2_ref[...]) * (1.0 + mod_ref[0, 4:5, :]) + mod_ref[0, 3:4, :]
    hn2p_ref[...] = _pack_bf16_pairs(hn2)
    logits_t = lax.dot_general(rwt_ref[...], hn2, (((1,), (1,)), ((), ())), precision=HIGHEST,
                               preferred_element_type=F32)
    s = jax.nn.sigmoid(logits_t)
    wt, sel, picks = _route(s, s + rb_ref[...])

    earlier = (lax.broadcasted_iota(I32, (tm, tm), 0) < lax.broadcasted_iota(I32, (tm, tm), 1))
    prefix = jnp.dot(sel.astype(BF16), earlier.astype(BF16), preferred_element_type=F32)
    pos_all = cnt_sc[:, 0:1] + prefix
    eid = lax.broadcasted_iota(I32, (N_EXPERTS, tm), 0)
    eid_f = eid.astype(F32)
    row8 = lax.broadcasted_iota(I32, (TOP_K, tm), 0)
    ek = jnp.zeros((TOP_K, tm), F32)
    pk = jnp.zeros((TOP_K, tm), F32)
    wk = jnp.zeros((N_EXPERTS, tm), F32)
    for k, pick in enumerate(picks):
        take = lambda a: jnp.sum(jnp.where(pick, a, 0.0), axis=0, keepdims=True)
        ek = jnp.where(row8 == k, take(eid_f), ek)
        pk = jnp.where(row8 == k, take(pos_all), pk)
        wk = jnp.where(eid == k, take(wt), wk)
    ek_ref[...] = ek.astype(I32)
    pk_ref[...] = pk.astype(I32)
    wtok_ref[...] = wk.T
    cnt_sc[...] += jnp.broadcast_to(jnp.sum(sel, axis=1, keepdims=True), cnt_sc.shape)

    @pl.when(i == pl.num_programs(0) - 1)
    def _():
        cnt_ref[...] = cnt_sc[...].astype(I32)


def _mix_out(hmf, hmb, o, hrf, hrb, gr, x2d, pos, mod, t, tm, mln, rgn, wo_ml, wo_rg, norm2, rwt, rbias, cnt0):
    n = x2d.shape[0]
    npos = pos.shape[0] // tm
    tok = lambda i: (i, 0)
    tok_t = lambda i: (0, i)
    const = lambda i: (0, 0)
    return pl.pallas_call(
        _mix_out_kernel,
        grid=(n // tm,),
        in_specs=[pl.BlockSpec((tm, D_ML), tok)] * 6 + [
            pl.BlockSpec((tm, D_MODEL), tok),
            pl.BlockSpec((tm, D_MODEL), lambda i: (i % npos, 0)),
            pl.BlockSpec((1, N_MOD, D_MODEL), _mod_index(mod.shape[0], tm, t)),
            pl.BlockSpec((1, D_ML), const), pl.BlockSpec((1, D_RG), const),
            pl.BlockSpec((D_ML, D_MODEL), const), pl.BlockSpec((D_RG, D_MODEL), const),
            pl.BlockSpec((1, D_MODEL), const),
            pl.BlockSpec((N_EXPERTS, D_MODEL), const), pl.BlockSpec((N_EXPERTS, 1), const),
            pl.BlockSpec((N_EXPERTS, LANES), const),
        ],
        out_specs=[pl.BlockSpec((tm, D_MODEL), tok), pl.BlockSpec((tm, D_PACK), tok),
                   pl.BlockSpec((TOP_K, tm), tok_t), pl.BlockSpec((TOP_K, tm), tok_t),
                   pl.BlockSpec((tm, N_EXPERTS), tok), pl.BlockSpec((N_EXPERTS, LANES), const)],
        out_shape=[jax.ShapeDtypeStruct((n, D_MODEL), F32), jax.ShapeDtypeStruct((n, D_PACK), I32),
                   jax.ShapeDtypeStruct((TOP_K, n), I32), jax.ShapeDtypeStruct((TOP_K, n), I32),
                   jax.ShapeDtypeStruct((n, N_EXPERTS), F32), jax.ShapeDtypeStruct((N_EXPERTS, LANES), I32)],
        scratch_shapes=[pltpu.VMEM((N_EXPERTS, LANES), F32)],
        name="mix_out",
        compiler_params=_params("arbitrary"),
    )(hmf, hmb, o, hrf, hrb, gr, x2d, pos, mod, mln, rgn, wo_ml, wo_rg, norm2, rwt, rbias, cnt0)


def _sc_mesh():
    return plsc.VectorSubcoreMesh(core_axis_name="core", subcore_axis_name="subcore")


def _sc_worker():
    info = plsc.get_sparse_core_info()
    return lax.axis_index("subcore") * info.num_cores + lax.axis_index("core"), info.num_cores * info.num_subcores


def _sc_dispatch(xp, dest3, n_slots):
    n, w = xp.shape
    nwin = n // SC_WINDOW

    @pl.kernel(out_type=jax.ShapeDtypeStruct((n_slots, w), xp.dtype), mesh=_sc_mesh(),
               scratch_types=[pltpu.VMEM((SC_WINDOW, w), xp.dtype), pltpu.VMEM((TOP_K, SC_WINDOW), I32)],
               name="sc_dispatch")
    def k(x_hbm, i_hbm, o_hbm, x_v, i_v):
        wid, nworkers = _sc_worker()
        per = nwin // nworkers

        @pl.loop(0, per)
        def _(s):
            win = wid * per + s
            pltpu.sync_copy(x_hbm.at[pl.ds(win * SC_WINDOW, SC_WINDOW)], x_v)
            pltpu.sync_copy(i_hbm.at[win], i_v)
            for j in range(TOP_K):
                pltpu.sync_copy(x_v, o_hbm.at[i_v.at[j]])

    return k(xp, dest3)


def _sc_combine_gather(ys, dest3):
    nwin = dest3.shape[0]
    w = ys.shape[1]

    @pl.kernel(out_type=jax.ShapeDtypeStruct((nwin, TOP_K, SC_WINDOW, w), ys.dtype), mesh=_sc_mesh(),
               scratch_types=[pltpu.VMEM((SC_WINDOW, w), ys.dtype), pltpu.VMEM((TOP_K, SC_WINDOW), I32)],
               name="sc_combine")
    def k(y_hbm, i_hbm, o_hbm, y_v, i_v):
        wid, nworkers = _sc_worker()
        per = nwin // nworkers

        @pl.loop(0, per)
        def _(s):
            win = wid * per + s
            pltpu.sync_copy(i_hbm.at[win], i_v)
            for j in range(TOP_K):
                pltpu.sync_copy(y_hbm.at[i_v.at[j]], y_v)
                pltpu.sync_copy(y_v, o_hbm.at[win, j])

    return k(ys, dest3)


def _swiglu(x, w13):
    h = jnp.dot(x, w13, preferred_element_type=F32)
    return _silu(h[:, 0:D_EXPERT]) * h[:, D_EXPERT:2 * D_EXPERT]


def _unpack_rows_bf16(p):
    hi, lo = _unpack_bf16_pairs(p)
    return jnp.concatenate([hi.astype(BF16), lo.astype(BF16)], axis=1)


def _expert_kernel(rows, be_ref, nu_ref, x_ref, *refs):
    y_ref = refs[-1]

    @pl.when(pl.program_id(0) * EXPERT_BLOCKS_PER_STEP < nu_ref[0])
    def _():
        for g in range(EXPERT_BLOCKS_PER_STEP):
            w1_ref, w3_ref, w2_ref = refs[3 * g:3 * g + 3]
            sl = slice(g * rows, (g + 1) * rows)
            x = _unpack_rows_bf16(x_ref[sl, :])
            h = (_silu(jnp.dot(x, w1_ref[0].astype(BF16), preferred_element_type=F32))
                 * jnp.dot(x, w3_ref[0].astype(BF16), preferred_element_type=F32))
            y_ref[sl, :] = _pack_bf16_pairs(
                jnp.dot(h.astype(BF16), w2_ref[0].astype(BF16), preferred_element_type=F32))


def _experts(xs, block_expert, n_used, w1, w3, w2, rows):
    g = EXPERT_BLOCKS_PER_STEP
    nb = xs.shape[0] // rows
    w_specs = []
    for j in range(g):
        up = pl.BlockSpec((1, D_MODEL, D_EXPERT), lambda b, be, nu, j=j: (be[b * g + j], 0, 0))
        w_specs += [up, up, pl.BlockSpec((1, D_EXPERT, D_MODEL), lambda b, be, nu, j=j: (be[b * g + j], 0, 0))]
    return pl.pallas_call(
        functools.partial(_expert_kernel, rows),
        grid_spec=pltpu.PrefetchScalarGridSpec(
            num_scalar_prefetch=2,
            grid=(nb // g,),
            in_specs=[pl.BlockSpec((g * rows, D_PACK), lambda b, be, nu: (b, 0))] + w_specs,
            out_specs=pl.BlockSpec((g * rows, D_PACK), lambda b, be, nu: (b, 0)),
        ),
        out_shape=jax.ShapeDtypeStruct(xs.shape, I32),
        name="experts",
        compiler_params=_params("arbitrary", vmem=VMEM_LIMIT_EXPERTS),
    )(block_expert, n_used, xs, *([w1, w3, w2] * g))


def _moe_out_kernel(yk_ref, hn2p_ref, wtok_ref, sw13_ref, sw2_ref, x1_ref, mod_ref, nf_ref, y_ref):
    shared = jnp.dot(_swiglu(_unpack_rows_bf16(hn2p_ref[...]), sw13_ref[...]).astype(BF16), sw2_ref[...],
                     preferred_element_type=F32)
    w = wtok_ref[...]
    parts = []
    for wi in range(yk_ref.shape[0]):
        rows = slice(wi * SC_WINDOW, (wi + 1) * SC_WINDOW)
        a_hi = shared[rows, 0:D_PACK]
        a_lo = shared[rows, D_PACK:D_MODEL]
        for k in range(TOP_K):
            y_hi, y_lo = _unpack_bf16_pairs(yk_ref[wi, k])
            wc = w[rows, k:k + 1]
            a_hi = a_hi + wc * y_hi
            a_lo = a_lo + wc * y_lo
        parts.append(jnp.concatenate([a_hi, a_lo], axis=1))
    x2 = x1_ref[...] + mod_ref[0, 5:6, :] * jnp.concatenate(parts, axis=0)
    y_ref[...] = _rms(x2, nf_ref[...])


def _moe_out(yk, tok0, hn2p, wtok, sw13, sw2, x1, mod, t, tm, norm_final):
    n = hn2p.shape[0]
    tok = lambda i: (i, 0)
    const = lambda i: (0, 0)
    blk0 = tok0 // tm
    return pl.pallas_call(
        _moe_out_kernel,
        grid=(n // tm,),
        in_specs=[
            pl.BlockSpec((tm // SC_WINDOW, TOP_K, SC_WINDOW, D_PACK), lambda i: (i + blk0, 0, 0, 0)),
            pl.BlockSpec((tm, D_PACK), tok),
            pl.BlockSpec((tm, N_EXPERTS), tok),
            pl.BlockSpec((D_MODEL, 2 * D_EXPERT), const),
            pl.BlockSpec((D_EXPERT, D_MODEL), const),
            pl.BlockSpec((tm, D_MODEL), tok),
            pl.BlockSpec((1, N_MOD, D_MODEL), _mod_index(mod.shape[0], tm, t)),
            pl.BlockSpec((1, D_MODEL), const),
        ],
        out_specs=pl.BlockSpec((tm, D_MODEL), tok),
        out_shape=jax.ShapeDtypeStruct((n, D_MODEL), F32),
        name="moe_out",
        compiler_params=_params("arbitrary"),
    )(yk, hn2p, wtok, sw13, sw2, x1, mod, norm_final)


def _dispatch_plan(cnt, ek, pk, rows):
    n = ek.shape[1]
    nb = n * TOP_K // rows + N_EXPERTS
    padded = (cnt + rows - 1) // rows * rows
    ends = jnp.cumsum(padded)
    experts = jnp.arange(N_EXPERTS, dtype=I32)
    first_row = jnp.sum(jnp.where(ek[:, :, None] == experts, ends - padded, 0), axis=-1)
    dest3 = (first_row + pk).reshape(TOP_K, n // SC_WINDOW, SC_WINDOW).transpose(1, 0, 2)
    block_end = ends // rows
    block_expert = jnp.minimum(jnp.sum(block_end[None, :] <= jnp.arange(nb, dtype=I32)[:, None], axis=1),
                               N_EXPERTS - 1).astype(I32)
    return dest3, block_expert, block_end[-1:].astype(I32), nb * rows


def _grid_pos_embed(n_tokens, dim):
    rows_n = n_tokens // GRID_W
    row = jnp.repeat(jnp.arange(rows_n), GRID_W).astype(F32)
    col = jnp.tile(jnp.arange(GRID_W), rows_n).astype(F32)
    quarter = dim // 4
    omega = 1.0 / (POS_BASE ** (jnp.arange(quarter, dtype=F32) / quarter))
    ra = row[:, None] * omega
    ca = col[:, None] * omega
    return jnp.concatenate([jnp.sin(ra), jnp.cos(ra), jnp.sin(ca), jnp.cos(ca)], axis=-1)


def _block_diag(w):
    eye = jnp.eye(N_RG_BLOCKS, dtype=w.dtype)
    return jnp.einsum('nij,nm->nimj', w, eye).reshape(D_RG, D_RG)


def _layer_weights(l, norm1, w_in, mlstm_gate_bias, mlstm_norm, rg_conv_w, rg_conv_b, rg_wa, rg_ba, rg_wx,
                   rg_bx, rg_lambda, rg_norm, w_out, norm2, router_w, router_bias, exp_w1, exp_w3, exp_w2,
                   shared_w1, shared_w3, shared_w2):
    wi = w_in[l]
    c0, c1 = 4 * D_ML, 4 * D_ML + N_GATE
    wg = wi[:, c0:c1]
    wg_hi = wg.astype(BF16)
    wg_lo = (wg - wg_hi.astype(F32)).astype(BF16)
    zcols = lambda w: jnp.zeros((D_MODEL, w), BF16)
    return dict(
        norm1=norm1[l].reshape(1, D_MODEL),
        wq=wi[:, :c0].astype(BF16),
        wr=jnp.concatenate([wi[:, c1:].astype(BF16), wg_hi, wg_lo, zcols(LANES - 2 * N_GATE)], axis=1),
        wgh=jnp.concatenate([wg_hi, zcols(LANES - N_GATE)], axis=1),
        gbias=jnp.pad(mlstm_gate_bias[l].reshape(1, N_GATE), ((0, 0), (0, LANES - N_GATE))),
        mln=mlstm_norm[l].reshape(1, D_ML),
        cw=rg_conv_w[l], cb=rg_conv_b[l].reshape(1, D_RG),
        wbd=jnp.concatenate([_block_diag(rg_wa[l, 0]), _block_diag(rg_wx[l, 0]),
                             _block_diag(rg_wa[l, 1]), _block_diag(rg_wx[l, 1])], axis=1).astype(BF16),
        rbias=jnp.concatenate([rg_ba[l, 0], rg_bx[l, 0], rg_ba[l, 1], rg_bx[l, 1]]).reshape(1, 4 * D_RG),
        lam=rg_lambda[l], rgn=rg_norm[l].reshape(1, D_RG),
        wo_ml=w_out[l, :D_ML].astype(BF16), wo_rg=w_out[l, D_ML:].astype(BF16),
        norm2=norm2[l].reshape(1, D_MODEL),
        rwt=router_w[l].T, rb=router_bias[l].reshape(N_EXPERTS, 1),
        w1=exp_w1[l], w3=exp_w3[l], w2=exp_w2[l],
        sw13=jnp.concatenate([shared_w1[l], shared_w3[l]], axis=-1).astype(BF16),
        sw2=shared_w2[l].astype(BF16),
    )


def _mixers(x2d, pos, mod, c0, n0, m0, h0, cnt0, lw, b, t, emit_state):
    tl = _tiles(t, mod.shape[0] > 1)
    tm = tl['tok']
    q, k, v, o, xr, gr, g, gt = _in_proj(x2d, pos, mod, t, tm, lw['norm1'], lw['wq'], lw['wr'], lw['wgh'],
                                         lw['gbias'])
    ml = _mlstm(q, k, v, g, gt, c0, n0, m0, b, t, tl['chunk'], emit_state)
    rg = _rglru(xr, h0, lw['wbd'], lw['rbias'], lw['lam'], lw['cw'], lw['cb'], b, t, tl['scan'], emit_state)
    routed = _mix_out(ml[0], ml[1], o, rg[0], rg[1], gr, x2d, pos, mod, t, tm, lw['mln'], lw['rgn'],
                      lw['wo_ml'], lw['wo_rg'], lw['norm2'], lw['rwt'], lw['rb'], cnt0)
    return routed, ml[2:], rg[2:]


def _routed_experts(paths, lw):
    cnt = paths[-1][5][:, 0]
    hn2p = jnp.concatenate([p[1] for p in paths], axis=0)
    ek = jnp.concatenate([p[2] for p in paths], axis=1)
    pk = jnp.concatenate([p[3] for p in paths], axis=1)
    dest3, block_expert, n_used, n_slots = _dispatch_plan(cnt, ek, pk, EXPERT_ROWS)
    xs = _sc_dispatch(hn2p, dest3, n_slots)
    ys = _experts(xs, block_expert, n_used, lw['w1'], lw['w3'], lw['w2'], EXPERT_ROWS)
    return _sc_combine_gather(ys, dest3)


def kernel(x_prompt, x_sample, c, state_mlstm_C, state_mlstm_n, state_mlstm_m, state_rglru_h, c_ctx, w_ada, b_ada, norm1, w_in, mlstm_gate_bias, mlstm_norm, rg_conv_w, rg_conv_b, rg_wa, rg_ba, rg_wx, rg_bx, rg_lambda, rg_norm, w_out, norm2, router_w, router_bias, exp_w1, exp_w3, exp_w2, shared_w1, shared_w3, shared_w2, norm_final):
    bp, tp, _ = x_prompt.shape
    bs, ts, _ = x_sample.shape
    depth = w_ada.shape[0]
    assert depth == 1, "the final norm is fused into the single layer's MoE output kernel"
    nd = 2 * N_HEADS
    l = 0
    lw = _layer_weights(l, norm1, w_in, mlstm_gate_bias, mlstm_norm, rg_conv_w, rg_conv_b, rg_wa, rg_ba, rg_wx,
                        rg_bx, rg_lambda, rg_norm, w_out, norm2, router_w, router_bias, exp_w1, exp_w3, exp_w2,
                        shared_w1, shared_w3, shared_w2)
    nf = norm_final.reshape(1, D_MODEL)
    cvecs = jnp.concatenate([c_ctx[None], c, jnp.zeros((SUBLANES - 1 - bs, D_MODEL), F32)], axis=0)
    mod = _ada(cvecs, w_ada[l], b_ada[l]).reshape(SUBLANES, N_MOD, D_MODEL)

    mod_p, mod_s = mod[0:1], mod[1:1 + bs]
    tm_p, tm_s = _tiles(tp, False)['tok'], _tiles(ts, True)['tok']
    rp, (cc, nc_, mc), (hc,) = _mixers(
        x_prompt.reshape(bp * tp, D_MODEL), jnp.zeros((tm_p, D_MODEL), F32), mod_p,
        jnp.zeros((1, nd, HEAD, HEAD), F32), jnp.zeros((1, nd, HEAD), F32), jnp.zeros((1, nd, HEAD), F32),
        jnp.zeros((1, 2, D_RG), F32), jnp.zeros((N_EXPERTS, LANES), I32), lw, bp, tp, True)
    rs, _, _ = _mixers(
        x_sample.reshape(bs * ts, D_MODEL), _grid_pos_embed(ts, D_MODEL), mod_s,
        state_mlstm_C[:, l].reshape(bs, nd, HEAD, HEAD), state_mlstm_n[:, l].reshape(bs, nd, HEAD),
        jnp.broadcast_to(state_mlstm_m[:, l].reshape(bs, nd, 1), (bs, nd, HEAD)),
        state_rglru_h[:, l], jnp.zeros((N_EXPERTS, LANES), I32), lw, bs, ts, False)
    yp = _moe_out(_routed_experts([rp], lw), 0, rp[1], rp[4], lw['sw13'], lw['sw2'], rp[0], mod_p, tp, tm_p, nf)
    ys = _moe_out(_routed_experts([rs], lw), 0, rs[1], rs[4], lw['sw13'], lw['sw2'], rs[0], mod_s, ts, tm_s, nf)

    y_prompt = yp.reshape(bp, tp, D_MODEL)
    y_sample = ys.reshape(bs, ts, D_MODEL)
    new_c = cc.reshape(bp, 1, 2, N_HEADS, HEAD, HEAD)
    new_n = nc_.reshape(bp, 1, 2, N_HEADS, HEAD)
    new_m = mc[:, :, 0].reshape(bp, 1, 2, N_HEADS)
    new_h = hc.reshape(bp, 1, 2, D_RG)
    return (y_prompt, y_sample, new_c, new_n, new_m, new_h)
```

```python
import functools

import jax
import jax.numpy as jnp
from jax import lax
from jax.experimental import pallas as pl
from jax.experimental.pallas import tpu as pltpu
from jax.experimental.pallas import tpu_sc as plsc

F32 = jnp.float32
BF16 = jnp.bfloat16
I32 = jnp.int32
HIGHEST = lax.Precision.HIGHEST

D_MODEL = 1024
N_MOD = 6
D_ML = 512
N_HEADS = 4
HEAD = 128
D_RG = 512
N_RG_BLOCKS = 8
RG_BLOCK = 64
RG_C = 8.0
N_GATE = 16
N_EXPERTS = 64
N_GROUPS = 8
GROUP = 8
TOPK_GROUPS = 4
TOP_K = 8
D_EXPERT = 256
ROUTED_SCALE = 2.5
EPS = 1e-6
GRID_W = 64
POS_BASE = 10000.0

RG_SEG_PAD = 8
SC_WINDOW = 128
D_PACK = D_MODEL // 2
EXPERT_ROWS = 512
EXPERT_BLOCKS_PER_STEP = 4

SUBLANES = 8
LANES = 128
VMEM_LIMIT = 48 * 1024 * 1024
VMEM_LIMIT_EXPERTS = 56 * 1024 * 1024


def _params(*sem, vmem=VMEM_LIMIT):
    return pltpu.CompilerParams(dimension_semantics=sem, vmem_limit_bytes=vmem)


def _tiles(t, per_sequence_mod):
    cap = t if per_sequence_mod else 1 << 30
    return dict(
        tok=min(512, cap),
        chunk=min(256, t),
        scan=min(512, t),
    )


def _silu(x):
    return x * jax.nn.sigmoid(x)


def _softplus(x):
    return jnp.maximum(x, 0.0) + jnp.log1p(jnp.exp(-jnp.abs(x)))


def _rms(x, g):
    return x * lax.rsqrt(jnp.mean(x * x, axis=-1, keepdims=True) + EPS) * g


def _pack_bf16_pairs(x):
    w = x.shape[1] // 2
    hi = lax.bitcast_convert_type(x[:, :w].astype(BF16).astype(F32), I32)
    lo = lax.bitcast_convert_type(x[:, w:].astype(BF16).astype(F32), I32)
    return hi | lax.shift_right_logical(lo, jnp.full(lo.shape, 16, I32))


def _unpack_bf16_pairs(p):
    hi = lax.bitcast_convert_type(p & jnp.int32(-65536), F32)
    lo = lax.bitcast_convert_type(lax.shift_left(p, jnp.full(p.shape, 16, I32)), F32)
    return hi, lo


def _ada_kernel(c_ref, w_ref, b_ref, o_ref):
    s = _silu(c_ref[...])
    o_ref[...] = jnp.dot(s, w_ref[...], precision=HIGHEST, preferred_element_type=F32) + b_ref[...]


def _ada(cvecs, w_ada, b_ada):
    n_out = w_ada.shape[1]
    tn = 1536
    return pl.pallas_call(
        _ada_kernel,
        grid=(n_out // tn,),
        in_specs=[
            pl.BlockSpec((SUBLANES, D_MODEL), lambda j: (0, 0)),
            pl.BlockSpec((D_MODEL, tn), lambda j: (0, j)),
            pl.BlockSpec((1, tn), lambda j: (0, j)),
        ],
        out_specs=pl.BlockSpec((SUBLANES, tn), lambda j: (0, j)),
        out_shape=jax.ShapeDtypeStruct((SUBLANES, n_out), F32),
        name="ada",
        compiler_params=_params("arbitrary"),
    )(cvecs, w_ada, b_ada.reshape(1, n_out))


def _pos_tile(rt_ref, ct_ref):
    left = jnp.concatenate([jnp.broadcast_to(rt_ref[r:r + 1, :], (GRID_W, rt_ref.shape[1]))
                            for r in range(rt_ref.shape[0])], axis=0)
    right = jnp.concatenate([ct_ref[...]] * rt_ref.shape[0], axis=0)
    return jnp.concatenate([left, right], axis=1)


def _in_proj_kernel(x_ref, rt_ref, ct_ref, mod_ref, n1_ref, wq_ref, wr_ref, wgh_ref, gb_ref,
                    qt_ref, k_ref, kt_ref, v_ref, vt_ref, ot_ref, xr_ref, gr_ref, g_ref, gt_ref):
    x = x_ref[...] + _pos_tile(rt_ref, ct_ref)
    hn = _rms(x, n1_ref[...]) * (1.0 + mod_ref[0, 1:2, :]) + mod_ref[0, 0:1, :]
    hb = hn.astype(BF16)
    z = jnp.dot(hb, wq_ref[...], preferred_element_type=F32)
    k = z[:, D_ML:2 * D_ML] * (HEAD ** -0.5)
    v = z[:, 2 * D_ML:3 * D_ML]
    qt_ref[...] = z[:, 0:D_ML].T.astype(BF16)
    k_ref[...] = k.astype(BF16)
    kt_ref[...] = k.T.astype(BF16)
    v_ref[...] = v.astype(BF16)
    vt_ref[...] = v.T.astype(BF16)
    ot_ref[...] = z[:, 3 * D_ML:4 * D_ML].T
    zr = jnp.dot(hb, wr_ref[...], preferred_element_type=F32)
    xr_ref[...] = zr[:, 0:D_RG]
    gr_ref[...] = zr[:, D_RG:2 * D_RG]
    zg = zr[:, 2 * D_RG:2 * D_RG + LANES]
    h_lo = (hn - hb.astype(F32)).astype(BF16)
    g = (zg + pltpu.roll(zg, LANES - N_GATE, 1)
         + jnp.dot(h_lo, wgh_ref[...], preferred_element_type=F32) + gb_ref[...])
    col = lax.broadcasted_iota(I32, g.shape, 1)
    g = jnp.where((col & 4) != 0, -_softplus(-g), g)
    g_ref[...] = g[:, 0:N_GATE]
    gt_ref[...] = g.T[0:N_GATE, :]


def _mod_index(bm, tm, t):
    if bm > 1:
        return lambda i: ((i * tm) // t, 0, 0)
    return lambda i: (0, 0, 0)


def _pos_specs(rtab, ctab, tm):
    rows = tm // GRID_W
    period = rtab.shape[0] // rows
    return [pl.BlockSpec((rows, D_MODEL // 2), lambda i: (i % period, 0)),
            pl.BlockSpec((GRID_W, D_MODEL // 2), lambda i: (0, 0))]


def _in_proj(x2d, rtab, ctab, mod, t, tm, norm1, wq, wr, wgh, gbias):
    n = x2d.shape[0]
    tok = lambda i: (i, 0)
    tok_t = lambda i: (0, i)
    const = lambda i: (0, 0)
    f = lambda w: jax.ShapeDtypeStruct((n, w), F32)
    row16 = jax.ShapeDtypeStruct((n, D_ML), BF16)
    col16 = jax.ShapeDtypeStruct((D_ML, n), BF16)
    return pl.pallas_call(
        _in_proj_kernel,
        grid=(n // tm,),
        in_specs=[pl.BlockSpec((tm, D_MODEL), tok)] + _pos_specs(rtab, ctab, tm) + [
            pl.BlockSpec((1, N_MOD, D_MODEL), _mod_index(mod.shape[0], tm, t)),
            pl.BlockSpec((1, D_MODEL), const),
            pl.BlockSpec((D_MODEL, 4 * D_ML), const),
            pl.BlockSpec((D_MODEL, 2 * D_RG + LANES), const),
            pl.BlockSpec((D_MODEL, LANES), const),
            pl.BlockSpec((1, LANES), const),
        ],
        out_specs=[pl.BlockSpec((D_ML, tm), tok_t), pl.BlockSpec((tm, D_ML), tok), pl.BlockSpec((D_ML, tm), tok_t),
                   pl.BlockSpec((tm, D_ML), tok), pl.BlockSpec((D_ML, tm), tok_t), pl.BlockSpec((D_ML, tm), tok_t),
                   pl.BlockSpec((tm, D_RG), tok), pl.BlockSpec((tm, D_RG), tok),
                   pl.BlockSpec((tm, N_GATE), tok), pl.BlockSpec((N_GATE, tm), tok_t)],
        out_shape=[col16, row16, col16, row16, col16, jax.ShapeDtypeStruct((D_ML, n), F32),
                   f(D_RG), f(D_RG), f(N_GATE), jax.ShapeDtypeStruct((N_GATE, n), F32)],
        name="in_proj",
        compiler_params=_params("arbitrary"),
    )(x2d, rtab, ctab, mod, norm1, wq, wr, wgh, gbias)


def _mlstm_kernel(chunk, nc, emit_state, *refs):
    (qtf, kf, ktf, vf, vtf, gf, gtf, qtb, kb, ktb, vb, vtb, gb, gtb, c0_ref, n0_ref, m0_ref) = refs[:17]
    if emit_state:
        hf_ref, hb_ref, c_out, n_out, m_out, c_sc, n_sc, m_sc = refs[17:]
    else:
        hf_ref, hb_ref, c_sc, n_sc, m_sc = refs[17:]
    i = pl.program_id(1)

    @pl.when(i == 0)
    def _():
        c_sc[...] = c0_ref[0]
        n_sc[...] = n0_ref[0]
        m_sc[...] = m0_ref[0]

    key = lax.broadcasted_iota(I32, (chunk, chunk), 0)
    qry = lax.broadcasted_iota(I32, (chunk, chunk), 1)
    hd = []
    for d, (qt_ref, k_ref, kt_ref, v_ref, vt_ref, g_ref, gt_ref, h_ref) in enumerate(
            ((qtf, kf, ktf, vf, vtf, gf, gtf, hf_ref), (qtb, kb, ktb, vb, vtb, gb, gtb, hb_ref))):
        tri = (key <= qry) if d == 0 else (key >= qry)
        tri_t = (qry <= key) if d == 0 else (qry >= key)
        g = g_ref[...]
        gt = gt_ref[...]
        brow = jnp.dot(gt, tri.astype(F32), precision=HIGHEST, preferred_element_type=F32)
        bcol = jnp.dot(tri_t.astype(F32), g, precision=HIGHEST, preferred_element_type=F32)
        blast = bcol[chunk - 1:chunk, :] if d == 0 else bcol[0:1, :]
        for h in range(N_HEADS):
            ci = d * 8 + h
            cf = d * 8 + 4 + h
            j = d * N_HEADS + h
            sl = slice(h * HEAD, (h + 1) * HEAD)
            hd.append(dict(
                j=j, sl=sl, tri=tri, h_ref=h_ref, qt=qt_ref[sl, :], k=k_ref[:, sl], kt=kt_ref[sl, :],
                v=v_ref[:, sl], vt=vt_ref[sl, :], b_row=brow[cf:cf + 1, :],
                gate_col=g[:, ci:ci + 1] - bcol[:, cf:cf + 1], gate_row=gt[ci:ci + 1, :] - brow[cf:cf + 1, :],
                b_last=blast[:, cf:cf + 1], m_prev=m_sc[j:j + 1, 0:1], c_prev=c_sc[j], n_prev=n_sc[j:j + 1, :]))
    for x in hd:
        top = jnp.max(jnp.where(x['tri'], x['gate_col'], -jnp.inf), axis=0, keepdims=True)
        mx = jnp.maximum(x['m_prev'], top)
        x['dm'] = jnp.exp(jnp.where(x['tri'], x['gate_col'] - mx, -jnp.inf))
        x['w_inter'] = jnp.exp(x['m_prev'] - mx)
        x['floor'] = jnp.exp(-(x['b_row'] + mx))
    for x in hd:
        x['st'] = jnp.dot(x['k'], x['qt'], preferred_element_type=F32) * x['dm']
    for x in hd:
        inter = lax.dot_general(x['c_prev'].astype(BF16), x['qt'], (((0,), (0,)), ((), ())),
                                preferred_element_type=F32)
        num = x['w_inter'] * inter + jnp.dot(x['vt'], x['st'].astype(BF16), preferred_element_type=F32)
        qn = jnp.dot(jnp.broadcast_to(x['n_prev'], (SUBLANES, HEAD)).astype(BF16), x['qt'],
                     preferred_element_type=F32)[0:1, :]
        den = x['w_inter'] * qn + jnp.sum(x['st'], axis=0, keepdims=True)
        x['h_ref'][x['sl'], :] = num / jnp.maximum(jnp.abs(den), x['floor'])
    for x in hd:
        j = x['j']
        log_w = x['b_last'] + x['gate_row']
        m_new = jnp.maximum(x['b_last'] + x['m_prev'], jnp.max(log_w, axis=1, keepdims=True))
        decay = jnp.exp(x['b_last'] + x['m_prev'] - m_new)
        w_row = jnp.exp(log_w - m_new)
        kwt = (x['kt'].astype(F32) * w_row).astype(BF16)
        c_sc[j] = decay * x['c_prev'] + jnp.dot(kwt, x['v'], preferred_element_type=F32)
        n_sc[j:j + 1, :] = decay * x['n_prev'] + jnp.dot(
            jnp.broadcast_to(w_row, (SUBLANES, chunk)).astype(BF16), x['k'], preferred_element_type=F32)[0:1, :]
        m_sc[j:j + 1, :] = jnp.broadcast_to(m_new, (1, HEAD))

    if emit_state:
        @pl.when(i == nc - 1)
        def _():
            c_out[0] = c_sc[...]
            n_out[0] = n_sc[...]
            m_out[0] = m_sc[...]


def _state_index(bm):
    if bm > 1:
        return lambda b, i: (b,) + (0,) * 3, lambda b, i: (b, 0, 0)
    return lambda b, i: (0,) * 4, lambda b, i: (0, 0, 0)


def _mlstm(qt, k, kt, v, vt, g, gt, c0, n0, m0, b, t, chunk, emit_state):
    nc = t // chunk
    n = b * t
    nd = 2 * N_HEADS
    fwd = lambda bi, i: (bi * nc + i, 0)
    bwd = lambda bi, i: (bi * nc + nc - 1 - i, 0)
    fwd_t = lambda bi, i: (0, bi * nc + i)
    bwd_t = lambda bi, i: (0, bi * nc + nc - 1 - i)
    c_idx, n_idx = _state_index(c0.shape[0])
    rows = lambda m: pl.BlockSpec((chunk, D_ML), m)
    cols = lambda m: pl.BlockSpec((D_ML, chunk), m)
    one_dir = lambda m, mt: [cols(mt), rows(m), cols(mt), rows(m), cols(mt),
                             pl.BlockSpec((chunk, N_GATE), m), pl.BlockSpec((N_GATE, chunk), mt)]
    in_specs = (one_dir(fwd, fwd_t) + one_dir(bwd, bwd_t)
                + [pl.BlockSpec((1, nd, HEAD, HEAD), c_idx),
                   pl.BlockSpec((1, nd, HEAD), n_idx), pl.BlockSpec((1, nd, HEAD), n_idx)])
    out_specs = [cols(fwd_t), cols(bwd_t)]
    out_shape = [jax.ShapeDtypeStruct((D_ML, n), F32)] * 2
    if emit_state:
        out_specs += [pl.BlockSpec((1, nd, HEAD, HEAD), lambda bi, i: (bi, 0, 0, 0)),
                      pl.BlockSpec((1, nd, HEAD), lambda bi, i: (bi, 0, 0)),
                      pl.BlockSpec((1, nd, HEAD), lambda bi, i: (bi, 0, 0))]
        out_shape += [jax.ShapeDtypeStruct((b, nd, HEAD, HEAD), F32),
                      jax.ShapeDtypeStruct((b, nd, HEAD), F32),
                      jax.ShapeDtypeStruct((b, nd, HEAD), F32)]
    return pl.pallas_call(
        functools.partial(_mlstm_kernel, chunk, nc, emit_state),
        grid=(b, nc),
        in_specs=in_specs,
        out_specs=out_specs,
        out_shape=out_shape,
        scratch_shapes=[pltpu.VMEM((nd, HEAD, HEAD), F32), pltpu.VMEM((nd, HEAD), F32),
                        pltpu.VMEM((nd, HEAD), F32)],
        name="mlstm",
        compiler_params=_params("arbitrary", "arbitrary"),
    )(qt, k, kt, v, vt, g, gt, qt, k, kt, v, vt, g, gt, c0, n0, m0)


def _neg_expm1(x):
    u = jnp.exp(x)
    near = jnp.where(u == 1.0, x, (u - 1.0) * x / jnp.log(u))
    return -jnp.where(x < -0.5, u - 1.0, near)


def _rglru_kernel(tb, nb, emit_state, *refs):
    (xf, xf_prev, xf_next, xb, xb_prev, xb_next, h0_ref, w_ref, bias_ref, lam_ref, cw_ref, cb_ref) = refs[:12]
    n_out = 3 if emit_state else 2
    hf_ref, hb_ref = refs[12:14]
    hfin_ref = refs[14] if emit_state else None
    carry, af_sc, uf_sc, ab_sc, ub_sc, hf_sc, pf_sc, hb_sc, pb_sc = refs[12 + n_out:]
    i = pl.program_id(1)

    @pl.when(i == 0)
    def _():
        carry[...] = h0_ref[0]

    row = lax.broadcasted_iota(I32, (tb, D_RG), 0)
    cw = cw_ref[...]
    softplus_neg_lam = _softplus(-lam_ref[...])

    def conv(main_ref, prev_ref, next_ref, first, last):
        main = main_ref[...]
        prev = jnp.where(first, 0.0, prev_ref[...])
        nxt = jnp.where(last, 0.0, next_ref[...])
        xm2 = jnp.where(row == 0, prev[6:7, :], jnp.where(row == 1, prev[7:8, :], pltpu.roll(main, 2, 0)))
        xm1 = jnp.where(row == 0, prev[7:8, :], pltpu.roll(main, 1, 0))
        xp1 = jnp.where(row == tb - 1, nxt[0:1, :], pltpu.roll(main, tb - 1, 0))
        return cb_ref[...] + xm2 * cw[0:1, :] + xm1 * cw[1:2, :] + main * cw[2:3, :] + xp1 * cw[3:4, :]

    def recurrence_terms(xc, d):
        z = jnp.dot(xc.astype(BF16), w_ref[:, d * 2 * D_RG:(d + 1) * 2 * D_RG],
                    preferred_element_type=F32) + bias_ref[:, d * 2 * D_RG:(d + 1) * 2 * D_RG]
        r = jax.nn.sigmoid(z[:, 0:D_RG])
        ig = jax.nn.sigmoid(z[:, D_RG:2 * D_RG])
        log_a = -RG_C * r * softplus_neg_lam[d:d + 1, :]
        a = jnp.exp(log_a)
        u = jnp.sqrt(_neg_expm1(2.0 * log_a)) * (ig * xc)
        return a, u

    a_f, u_f = recurrence_terms(conv(xf, xf_prev, xf_next, i == 0, i == nb - 1), 0)
    a_b, u_b = recurrence_terms(conv(xb, xb_prev, xb_next, i == nb - 1, i == 0), 1)
    seg = tb // SUBLANES
    pitch = seg + RG_SEG_PAD
    ncol = D_RG // LANES
    for lc in range(ncol):
        lanes = slice(lc * LANES, (lc + 1) * LANES)
        for s in range(SUBLANES):
            src = slice(s * seg, (s + 1) * seg)
            dst = slice(s * pitch, s * pitch + seg)
            af_sc[lc, dst, :], uf_sc[lc, dst, :] = a_f[src, lanes], u_f[src, lanes]
            ab_sc[lc, dst, :], ub_sc[lc, dst, :] = a_b[src, lanes], u_b[src, lanes]
    slab = lambda k: (slice(None), pl.ds(k, SUBLANES, stride=pitch), slice(None))
    hf = jnp.zeros((ncol, SUBLANES, LANES), F32)
    hb = jnp.zeros((ncol, SUBLANES, LANES), F32)
    pf = jnp.ones((ncol, SUBLANES, LANES), F32)
    pb = jnp.ones((ncol, SUBLANES, LANES), F32)
    for k in range(seg):
        kb = seg - 1 - k
        ak = af_sc[slab(k)]
        hf = ak * hf + uf_sc[slab(k)]
        pf = pf * ak
        hf_sc[slab(k)] = hf
        pf_sc[slab(k)] = pf
        ak = ab_sc[slab(kb)]
        hb = ak * hb + ub_sc[slab(kb)]
        pb = pb * ak
        hb_sc[slab(kb)] = hb
        pb_sc[slab(kb)] = pb
    for lc in range(ncol):
        lanes = slice(lc * LANES, (lc + 1) * LANES)
        c = carry[0:1, lanes]
        cin_f = []
        for s in range(SUBLANES):
            cin_f.append(c)
            c = pf[lc, s:s + 1, :] * c + hf[lc, s:s + 1, :]
        carry[0:1, lanes] = c
        c = carry[1:2, lanes]
        cin_b = [None] * SUBLANES
        for s in reversed(range(SUBLANES)):
            cin_b[s] = c
            c = pb[lc, s:s + 1, :] * c + hb[lc, s:s + 1, :]
        carry[1:2, lanes] = c
        for s in range(SUBLANES):
            rows = slice(s * seg, (s + 1) * seg)
            src = slice(s * pitch, s * pitch + seg)
            hf_ref[rows, lanes] = hf_sc[lc, src, :] + pf_sc[lc, src, :] * cin_f[s]
            hb_ref[rows, lanes] = hb_sc[lc, src, :] + pb_sc[lc, src, :] * cin_b[s]

    if emit_state:
        @pl.when(i == nb - 1)
        def _():
            hfin_ref[0] = carry[...]


def _rglru(xr, h0, wbd, bias, lam, cw, cb, b, t, tb, emit_state):
    nb = t // tb
    n = b * t
    r8 = tb // SUBLANES
    last8 = n // SUBLANES - 1
    fwd = lambda bi, i: (bi * nb + i, 0)
    bwd = lambda bi, i: (bi * nb + nb - 1 - i, 0)
    fwd_prev = lambda bi, i: (jnp.maximum((bi * nb + i) * r8 - 1, 0), 0)
    fwd_next = lambda bi, i: (jnp.minimum((bi * nb + i + 1) * r8, last8), 0)
    bwd_prev = lambda bi, i: (jnp.maximum((bi * nb + nb - 1 - i) * r8 - 1, 0), 0)
    bwd_next = lambda bi, i: (jnp.minimum((bi * nb + nb - i) * r8, last8), 0)
    const = lambda bi, i: (0, 0)
    h_idx = (lambda bi, i: (bi, 0, 0)) if h0.shape[0] > 1 else (lambda bi, i: (0, 0, 0))
    halo = lambda m: pl.BlockSpec((SUBLANES, D_RG), m)
    in_specs = [pl.BlockSpec((tb, D_RG), fwd), halo(fwd_prev), halo(fwd_next),
                pl.BlockSpec((tb, D_RG), bwd), halo(bwd_prev), halo(bwd_next),
                pl.BlockSpec((1, 2, D_RG), h_idx),
                pl.BlockSpec((D_RG, 4 * D_RG), const), pl.BlockSpec((1, 4 * D_RG), const),
                pl.BlockSpec((2, D_RG), const), pl.BlockSpec((4, D_RG), const), pl.BlockSpec((1, D_RG), const)]
    out_specs = [pl.BlockSpec((tb, D_RG), fwd), pl.BlockSpec((tb, D_RG), bwd)]
    out_shape = [jax.ShapeDtypeStruct((n, D_RG), F32)] * 2
    if emit_state:
        out_specs.append(pl.BlockSpec((1, 2, D_RG), lambda bi, i: (bi, 0, 0)))
        out_shape.append(jax.ShapeDtypeStruct((b, 2, D_RG), F32))
    return pl.pallas_call(
        functools.partial(_rglru_kernel, tb, nb, emit_state),
        grid=(b, nb),
        in_specs=in_specs,
        out_specs=out_specs,
        out_shape=out_shape,
        scratch_shapes=[pltpu.VMEM((2, D_RG), F32)]
        + [pltpu.VMEM((D_RG // LANES, tb + SUBLANES * RG_SEG_PAD, LANES), F32)] * 8,
        name="rglru",
        compiler_params=_params("arbitrary", "arbitrary"),
    )(xr, xr, xr, xr, xr, xr, h0, wbd, bias, lam, cw, cb)


def _route(s, sb):
    tm = s.shape[1]
    neg = -jnp.inf
    sub = lax.broadcasted_iota(I32, (GROUP, tm), 0)
    blocks = [sb[gi * GROUP:(gi + 1) * GROUP, :] for gi in range(N_GROUPS)]
    gscore = []
    for blk in blocks:
        m1 = jnp.max(blk, axis=0, keepdims=True)
        first = jnp.min(jnp.where(blk == m1, sub, GROUP), axis=0, keepdims=True)
        m2 = jnp.max(jnp.where(sub == first, neg, blk), axis=0, keepdims=True)
        gscore.append(m1 + m2)
    masked = []
    for gi in range(N_GROUPS):
        rank = jnp.zeros((1, tm), F32)
        for gj in range(N_GROUPS):
            if gj == gi:
                continue
            ahead = (gscore[gj] >= gscore[gi]) if gj < gi else (gscore[gj] > gscore[gi])
            rank = rank + jnp.where(ahead, 1.0, 0.0)
        masked.append(jnp.where(rank < TOPK_GROUPS, blocks[gi], neg))
    v = jnp.concatenate(masked, axis=0)
    eid = lax.broadcasted_iota(I32, (N_EXPERTS, tm), 0)
    sel = jnp.zeros((N_EXPERTS, tm), F32)
    picks = []
    for _ in range(TOP_K):
        mx = jnp.max(v, axis=0, keepdims=True)
        idx = jnp.min(jnp.where(v == mx, eid, N_EXPERTS), axis=0, keepdims=True)
        pick = eid == idx
        picks.append(pick)
        sel = jnp.where(pick, 1.0, sel)
        v = jnp.where(pick, neg, v)
    ws = s * sel
    return ws / jnp.sum(ws, axis=0, keepdims=True) * ROUTED_SCALE, sel, picks


def _mix_out_kernel(hmf_ref, hmb_ref, ot_ref, hrf_ref, hrb_ref, gr_ref, x_ref, rt_ref, ct_ref, mod_ref,
                    mln_ref, rgn_ref, wo_ml_ref, wo_rg_ref, n2_ref, rwt_ref, rb_ref, cnt0_ref,
                    x1_ref, hn2p_ref, ek_ref, pk_ref, wtok_ref, cnt_ref, cnt_sc):
    i = pl.program_id(0)
    tm = x_ref.shape[0]

    @pl.when(i == 0)
    def _():
        cnt_sc[...] = cnt0_ref[...].astype(F32)

    hm = hmf_ref[...] + hmb_ref[...]
    heads = []
    for h in range(N_HEADS):
        seg = hm[h * HEAD:(h + 1) * HEAD, :]
        heads.append(seg * lax.rsqrt(jnp.mean(seg * seg, axis=0, keepdims=True) + EPS))
    y_ml_t = jnp.concatenate(heads, axis=0) * mln_ref[...] * jax.nn.sigmoid(ot_ref[...])
    y_rg = _rms(hrf_ref[...] + hrb_ref[...], rgn_ref[...]) * jax.nn.gelu(gr_ref[...])
    mix = (lax.dot_general(y_ml_t.astype(BF16), wo_ml_ref[...], (((0,), (0,)), ((), ())),
                           preferred_element_type=F32)
           + jnp.dot(y_rg.astype(BF16), wo_rg_ref[...], preferred_element_type=F32))
    x1 = x_ref[...] + _pos_tile(rt_ref, ct_ref) + mod_ref[0, 2:3, :] * mix
    x1_ref[...] = x1
    hn2 = _rms(x1, n2_ref[...]) * (1.0 + mod_ref[0, 4:5, :]) + mod_ref[0, 3:4, :]
    hn2p_ref[...] = _pack_bf16_pairs(hn2)
    logits_t = lax.dot_general(rwt_ref[...], hn2, (((1,), (1,)), ((), ())), precision=HIGHEST,
                               preferred_element_type=F32)
    s = jax.nn.sigmoid(logits_t)
    wt, sel, picks = _route(s, s + rb_ref[...])

    earlier = (lax.broadcasted_iota(I32, (tm, tm), 0) < lax.broadcasted_iota(I32, (tm, tm), 1))
    prefix = jnp.dot(sel.astype(BF16), earlier.astype(BF16), preferred_element_type=F32)
    pos_all = cnt_sc[:, 0:1] + prefix
    eid = lax.broadcasted_iota(I32, (N_EXPERTS, tm), 0)
    eid_f = eid.astype(F32)
    row8 = lax.broadcasted_iota(I32, (TOP_K, tm), 0)
    ek = jnp.zeros((TOP_K, tm), F32)
    pk = jnp.zeros((TOP_K, tm), F32)
    wk = jnp.zeros((N_EXPERTS, tm), F32)
    for k, pick in enumerate(picks):
        take = lambda a: jnp.sum(jnp.where(pick, a, 0.0), axis=0, keepdims=True)
        ek = jnp.where(row8 == k, take(eid_f), ek)
        pk = jnp.where(row8 == k, take(pos_all), pk)
        wk = jnp.where(eid == k, take(wt), wk)
    ek_ref[...] = ek.astype(I32)
    pk_ref[...] = pk.astype(I32)
    wtok_ref[...] = wk.T
    cnt_sc[...] += jnp.broadcast_to(jnp.sum(sel, axis=1, keepdims=True), cnt_sc.shape)

    @pl.when(i == pl.num_programs(0) - 1)
    def _():
        cnt_ref[...] = cnt_sc[...].astype(I32)


def _mix_out(hmf, hmb, ot, hrf, hrb, gr, x2d, rtab, ctab, mod, t, tm, mln, rgn, wo_ml, wo_rg, norm2, rwt, rbias,
             cnt0):
    n = x2d.shape[0]
    tok = lambda i: (i, 0)
    tok_t = lambda i: (0, i)
    const = lambda i: (0, 0)
    return pl.pallas_call(
        _mix_out_kernel,
        grid=(n // tm,),
        in_specs=[pl.BlockSpec((D_ML, tm), tok_t)] * 3 + [pl.BlockSpec((tm, D_RG), tok)] * 3 + [
            pl.BlockSpec((tm, D_MODEL), tok)] + _pos_specs(rtab, ctab, tm) + [
            pl.BlockSpec((1, N_MOD, D_MODEL), _mod_index(mod.shape[0], tm, t)),
            pl.BlockSpec((D_ML, 1), const), pl.BlockSpec((1, D_RG), const),
            pl.BlockSpec((D_ML, D_MODEL), const), pl.BlockSpec((D_RG, D_MODEL), const),
            pl.BlockSpec((1, D_MODEL), const),
            pl.BlockSpec((N_EXPERTS, D_MODEL), const), pl.BlockSpec((N_EXPERTS, 1), const),
            pl.BlockSpec((N_EXPERTS, LANES), const),
        ],
        out_specs=[pl.BlockSpec((tm, D_MODEL), tok), pl.BlockSpec((tm, D_PACK), tok),
                   pl.BlockSpec((TOP_K, tm), tok_t), pl.BlockSpec((TOP_K, tm), tok_t),
                   pl.BlockSpec((tm, N_EXPERTS), tok), pl.BlockSpec((N_EXPERTS, LANES), const)],
        out_shape=[jax.ShapeDtypeStruct((n, D_MODEL), F32), jax.ShapeDtypeStruct((n, D_PACK), I32),
                   jax.ShapeDtypeStruct((TOP_K, n), I32), jax.ShapeDtypeStruct((TOP_K, n), I32),
                   jax.ShapeDtypeStruct((n, N_EXPERTS), F32), jax.ShapeDtypeStruct((N_EXPERTS, LANES), I32)],
        scratch_shapes=[pltpu.VMEM((N_EXPERTS, LANES), F32)],
        name="mix_out",
        compiler_params=_params("arbitrary"),
    )(hmf, hmb, ot, hrf, hrb, gr, x2d, rtab, ctab, mod, mln, rgn, wo_ml, wo_rg, norm2, rwt, rbias, cnt0)


def _sc_mesh():
    return plsc.VectorSubcoreMesh(core_axis_name="core", subcore_axis_name="subcore")


def _sc_worker():
    info = plsc.get_sparse_core_info()
    return lax.axis_index("subcore") * info.num_cores + lax.axis_index("core"), info.num_cores * info.num_subcores


def _sc_dispatch(xp, dest3, n_slots):
    n, w = xp.shape
    nwin = n // SC_WINDOW

    @pl.kernel(out_type=jax.ShapeDtypeStruct((n_slots, w), xp.dtype), mesh=_sc_mesh(),
               scratch_types=[pltpu.VMEM((SC_WINDOW, w), xp.dtype), pltpu.VMEM((TOP_K, SC_WINDOW), I32)],
               name="sc_dispatch")
    def k(x_hbm, i_hbm, o_hbm, x_v, i_v):
        wid, nworkers = _sc_worker()
        per = nwin // nworkers

        @pl.loop(0, per)
        def _(s):
            win = wid * per + s
            pltpu.sync_copy(x_hbm.at[pl.ds(win * SC_WINDOW, SC_WINDOW)], x_v)
            pltpu.sync_copy(i_hbm.at[win], i_v)
            for j in range(TOP_K):
                pltpu.sync_copy(x_v, o_hbm.at[i_v.at[j]])

    return k(xp, dest3)


def _sc_combine_gather(ys, dest3):
    nwin = dest3.shape[0]
    w = ys.shape[1]

    @pl.kernel(out_type=jax.ShapeDtypeStruct((nwin, TOP_K, SC_WINDOW, w), ys.dtype), mesh=_sc_mesh(),
               scratch_types=[pltpu.VMEM((SC_WINDOW, w), ys.dtype), pltpu.VMEM((TOP_K, SC_WINDOW), I32)],
               name="sc_combine")
    def k(y_hbm, i_hbm, o_hbm, y_v, i_v):
        wid, nworkers = _sc_worker()
        per = nwin // nworkers

        @pl.loop(0, per)
        def _(s):
            win = wid * per + s
            pltpu.sync_copy(i_hbm.at[win], i_v)
            for j in range(TOP_K):
                pltpu.sync_copy(y_hbm.at[i_v.at[j]], y_v)
                pltpu.sync_copy(y_v, o_hbm.at[win, j])

    return k(ys, dest3)


def _swiglu(x, w13):
    h = jnp.dot(x, w13, preferred_element_type=F32)
    return _silu(h[:, 0:D_EXPERT]) * h[:, D_EXPERT:2 * D_EXPERT]


def _unpack_rows_bf16(p):
    hi, lo = _unpack_bf16_pairs(p)
    return jnp.concatenate([hi.astype(BF16), lo.astype(BF16)], axis=1)


def _expert_kernel(rows, be_ref, nu_ref, x_ref, *refs):
    y_ref = refs[-1]

    @pl.when(pl.program_id(0) * EXPERT_BLOCKS_PER_STEP < nu_ref[0])
    def _():
        for g in range(EXPERT_BLOCKS_PER_STEP):
            w1_ref, w3_ref, w2_ref = refs[3 * g:3 * g + 3]
            sl = slice(g * rows, (g + 1) * rows)
            x = _unpack_rows_bf16(x_ref[sl, :])
            h = (_silu(jnp.dot(x, w1_ref[0].astype(BF16), preferred_element_type=F32))
                 * jnp.dot(x, w3_ref[0].astype(BF16), preferred_element_type=F32))
            y_ref[sl, :] = _pack_bf16_pairs(
                jnp.dot(h.astype(BF16), w2_ref[0].astype(BF16), preferred_element_type=F32))


def _experts(xs, block_expert, n_used, w1, w3, w2, rows):
    g = EXPERT_BLOCKS_PER_STEP
    nb = xs.shape[0] // rows
    w_specs = []
    for j in range(g):
        up = pl.BlockSpec((1, D_MODEL, D_EXPERT), lambda b, be, nu, j=j: (be[b * g + j], 0, 0))
        w_specs += [up, up, pl.BlockSpec((1, D_EXPERT, D_MODEL), lambda b, be, nu, j=j: (be[b * g + j], 0, 0))]
    return pl.pallas_call(
        functools.partial(_expert_kernel, rows),
        grid_spec=pltpu.PrefetchScalarGridSpec(
            num_scalar_prefetch=2,
            grid=(nb // g,),
            in_specs=[pl.BlockSpec((g * rows, D_PACK), lambda b, be, nu: (b, 0))] + w_specs,
            out_specs=pl.BlockSpec((g * rows, D_PACK), lambda b, be, nu: (b, 0)),
        ),
        out_shape=jax.ShapeDtypeStruct(xs.shape, I32),
        name="experts",
        compiler_params=_params("arbitrary", vmem=VMEM_LIMIT_EXPERTS),
    )(block_expert, n_used, xs, *([w1, w3, w2] * g))


def _moe_out_kernel(yk_ref, hn2p_ref, wtok_ref, sw13_ref, sw2_ref, x1_ref, mod_ref, nf_ref, y_ref):
    shared = jnp.dot(_swiglu(_unpack_rows_bf16(hn2p_ref[...]), sw13_ref[...]).astype(BF16), sw2_ref[...],
                     preferred_element_type=F32)
    w = wtok_ref[...]
    parts = []
    for wi in range(yk_ref.shape[0]):
        rows = slice(wi * SC_WINDOW, (wi + 1) * SC_WINDOW)
        a_hi = shared[rows, 0:D_PACK]
        a_lo = shared[rows, D_PACK:D_MODEL]
        for k in range(TOP_K):
            y_hi, y_lo = _unpack_bf16_pairs(yk_ref[wi, k])
            wc = w[rows, k:k + 1]
            a_hi = a_hi + wc * y_hi
            a_lo = a_lo + wc * y_lo
        parts.append(jnp.concatenate([a_hi, a_lo], axis=1))
    x2 = x1_ref[...] + mod_ref[0, 5:6, :] * jnp.concatenate(parts, axis=0)
    y_ref[...] = _rms(x2, nf_ref[...])


def _moe_out(yk, tok0, hn2p, wtok, sw13, sw2, x1, mod, t, tm, norm_final):
    n = hn2p.shape[0]
    tok = lambda i: (i, 0)
    const = lambda i: (0, 0)
    blk0 = tok0 // tm
    return pl.pallas_call(
        _moe_out_kernel,
        grid=(n // tm,),
        in_specs=[
            pl.BlockSpec((tm // SC_WINDOW, TOP_K, SC_WINDOW, D_PACK), lambda i: (i + blk0, 0, 0, 0)),
            pl.BlockSpec((tm, D_PACK), tok),
            pl.BlockSpec((tm, N_EXPERTS), tok),
            pl.BlockSpec((D_MODEL, 2 * D_EXPERT), const),
            pl.BlockSpec((D_EXPERT, D_MODEL), const),
            pl.BlockSpec((tm, D_MODEL), tok),
            pl.BlockSpec((1, N_MOD, D_MODEL), _mod_index(mod.shape[0], tm, t)),
            pl.BlockSpec((1, D_MODEL), const),
        ],
        out_specs=pl.BlockSpec((tm, D_MODEL), tok),
        out_shape=jax.ShapeDtypeStruct((n, D_MODEL), F32),
        name="moe_out",
        compiler_params=_params("arbitrary"),
    )(yk, hn2p, wtok, sw13, sw2, x1, mod, norm_final)


def _dispatch_plan(cnt, ek, pk, rows):
    n = ek.shape[1]
    nb = n * TOP_K // rows + N_EXPERTS
    padded = (cnt + rows - 1) // rows * rows
    ends = jnp.cumsum(padded)
    experts = jnp.arange(N_EXPERTS, dtype=I32)
    first_row = jnp.sum(jnp.where(ek[:, :, None] == experts, ends - padded, 0), axis=-1)
    dest3 = (first_row + pk).reshape(TOP_K, n // SC_WINDOW, SC_WINDOW).transpose(1, 0, 2)
    block_end = ends // rows
    block_expert = jnp.minimum(jnp.sum(block_end[None, :] <= jnp.arange(nb, dtype=I32)[:, None], axis=1),
                               N_EXPERTS - 1).astype(I32)
    return dest3, block_expert, block_end[-1:].astype(I32), nb * rows


def _grid_pos_tables(n_tokens, dim):
    quarter = dim // 4
    omega = 1.0 / (POS_BASE ** (jnp.arange(quarter, dtype=F32) / quarter))
    ra = jnp.arange(n_tokens // GRID_W).astype(F32)[:, None] * omega
    ca = jnp.arange(GRID_W).astype(F32)[:, None] * omega
    return (jnp.concatenate([jnp.sin(ra), jnp.cos(ra)], axis=-1),
            jnp.concatenate([jnp.sin(ca), jnp.cos(ca)], axis=-1))


def _block_diag(w):
    eye = jnp.eye(N_RG_BLOCKS, dtype=w.dtype)
    return jnp.einsum('nij,nm->nimj', w, eye).reshape(D_RG, D_RG)


def _layer_weights(l, norm1, w_in, mlstm_gate_bias, mlstm_norm, rg_conv_w, rg_conv_b, rg_wa, rg_ba, rg_wx,
                   rg_bx, rg_lambda, rg_norm, w_out, norm2, router_w, router_bias, exp_w1, exp_w3, exp_w2,
                   shared_w1, shared_w3, shared_w2):
    wi = w_in[l]
    c0, c1 = 4 * D_ML, 4 * D_ML + N_GATE
    wg = wi[:, c0:c1]
    wg_hi = wg.astype(BF16)
    wg_lo = (wg - wg_hi.astype(F32)).astype(BF16)
    zcols = lambda w: jnp.zeros((D_MODEL, w), BF16)
    return dict(
        norm1=norm1[l].reshape(1, D_MODEL),
        wq=wi[:, :c0].astype(BF16),
        wr=jnp.concatenate([wi[:, c1:].astype(BF16), wg_hi, wg_lo, zcols(LANES - 2 * N_GATE)], axis=1),
        wgh=jnp.concatenate([wg_hi, zcols(LANES - N_GATE)], axis=1),
        gbias=jnp.pad(mlstm_gate_bias[l].reshape(1, N_GATE), ((0, 0), (0, LANES - N_GATE))),
        mln=mlstm_norm[l].reshape(D_ML, 1),
        cw=rg_conv_w[l], cb=rg_conv_b[l].reshape(1, D_RG),
        wbd=jnp.concatenate([_block_diag(rg_wa[l, 0]), _block_diag(rg_wx[l, 0]),
                             _block_diag(rg_wa[l, 1]), _block_diag(rg_wx[l, 1])], axis=1).astype(BF16),
        rbias=jnp.concatenate([rg_ba[l, 0], rg_bx[l, 0], rg_ba[l, 1], rg_bx[l, 1]]).reshape(1, 4 * D_RG),
        lam=rg_lambda[l], rgn=rg_norm[l].reshape(1, D_RG),
        wo_ml=w_out[l, :D_ML].astype(BF16), wo_rg=w_out[l, D_ML:].astype(BF16),
        norm2=norm2[l].reshape(1, D_MODEL),
        rwt=router_w[l].T, rb=router_bias[l].reshape(N_EXPERTS, 1),
        w1=exp_w1[l], w3=exp_w3[l], w2=exp_w2[l],
        sw13=jnp.concatenate([shared_w1[l], shared_w3[l]], axis=-1).astype(BF16),
        sw2=shared_w2[l].astype(BF16),
    )


def _mixers(x2d, pos_tables, mod, c0, n0, m0, h0, cnt0, lw, b, t, emit_state):
    tl = _tiles(t, mod.shape[0] > 1)
    tm = tl['tok']
    rtab, ctab = pos_tables
    qt, k, kt, v, vt, ot, xr, gr, g, gt = _in_proj(x2d, rtab, ctab, mod, t, tm, lw['norm1'], lw['wq'], lw['wr'],
                                                   lw['wgh'], lw['gbias'])
    ml = _mlstm(qt, k, kt, v, vt, g, gt, c0, n0, m0, b, t, tl['chunk'], emit_state)
    rg = _rglru(xr, h0, lw['wbd'], lw['rbias'], lw['lam'], lw['cw'], lw['cb'], b, t, tl['scan'], emit_state)
    routed = _mix_out(ml[0], ml[1], ot, rg[0], rg[1], gr, x2d, rtab, ctab, mod, t, tm, lw['mln'], lw['rgn'],
                      lw['wo_ml'], lw['wo_rg'], lw['norm2'], lw['rwt'], lw['rb'], cnt0)
    return routed, ml[2:], rg[2:]


def _routed_experts(paths, lw):
    cnt = paths[-1][5][:, 0]
    hn2p = jnp.concatenate([p[1] for p in paths], axis=0)
    ek = jnp.concatenate([p[2] for p in paths], axis=1)
    pk = jnp.concatenate([p[3] for p in paths], axis=1)
    dest3, block_expert, n_used, n_slots = _dispatch_plan(cnt, ek, pk, EXPERT_ROWS)
    xs = _sc_dispatch(hn2p, dest3, n_slots)
    ys = _experts(xs, block_expert, n_used, lw['w1'], lw['w3'], lw['w2'], EXPERT_ROWS)
    return _sc_combine_gather(ys, dest3)


def kernel(x_prompt, x_sample, c, state_mlstm_C, state_mlstm_n, state_mlstm_m, state_rglru_h, c_ctx, w_ada, b_ada, norm1, w_in, mlstm_gate_bias, mlstm_norm, rg_conv_w, rg_conv_b, rg_wa, rg_ba, rg_wx, rg_bx, rg_lambda, rg_norm, w_out, norm2, router_w, router_bias, exp_w1, exp_w3, exp_w2, shared_w1, shared_w3, shared_w2, norm_final):
    bp, tp, _ = x_prompt.shape
    bs, ts, _ = x_sample.shape
    depth = w_ada.shape[0]
    assert depth == 1, "the final norm is fused into the single layer's MoE output kernel"
    nd = 2 * N_HEADS
    l = 0
    lw = _layer_weights(l, norm1, w_in, mlstm_gate_bias, mlstm_norm, rg_conv_w, rg_conv_b, rg_wa, rg_ba, rg_wx,
                        rg_bx, rg_lambda, rg_norm, w_out, norm2, router_w, router_bias, exp_w1, exp_w3, exp_w2,
                        shared_w1, shared_w3, shared_w2)
    nf = norm_final.reshape(1, D_MODEL)
    cvecs = jnp.concatenate([c_ctx[None], c, jnp.zeros((SUBLANES - 1 - bs, D_MODEL), F32)], axis=0)
    mod = _ada(cvecs, w_ada[l], b_ada[l]).reshape(SUBLANES, N_MOD, D_MODEL)

    mod_p, mod_s = mod[0:1], mod[1:1 + bs]
    tm_p, tm_s = _tiles(tp, False)['tok'], _tiles(ts, True)['tok']
    rp, (cc, nc_, mc), (hc,) = _mixers(
        x_prompt.reshape(bp * tp, D_MODEL),
        (jnp.zeros((tm_p // GRID_W, D_MODEL // 2), F32), jnp.zeros((GRID_W, D_MODEL // 2), F32)), mod_p,
        jnp.zeros((1, nd, HEAD, HEAD), F32), jnp.zeros((1, nd, HEAD), F32), jnp.zeros((1, nd, HEAD), F32),
        jnp.zeros((1, 2, D_RG), F32), jnp.zeros((N_EXPERTS, LANES), I32), lw, bp, tp, True)
    rs, _, _ = _mixers(
        x_sample.reshape(bs * ts, D_MODEL), _grid_pos_tables(ts, D_MODEL), mod_s,
        state_mlstm_C[:, l].reshape(bs, nd, HEAD, HEAD), state_mlstm_n[:, l].reshape(bs, nd, HEAD),
        jnp.broadcast_to(state_mlstm_m[:, l].reshape(bs, nd, 1), (bs, nd, HEAD)),
        state_rglru_h[:, l], jnp.zeros((N_EXPERTS, LANES), I32), lw, bs, ts, False)
    yp = _moe_out(_routed_experts([rp], lw), 0, rp[1], rp[4], lw['sw13'], lw['sw2'], rp[0], mod_p, tp, tm_p, nf)
    ys = _moe_out(_routed_experts([rs], lw), 0, rs[1], rs[4], lw['sw13'], lw['sw2'], rs[0], mod_s, ts, tm_s, nf)

    y_prompt = yp.reshape(bp, tp, D_MODEL)
    y_sample = ys.reshape(bs, ts, D_MODEL)
    new_c = cc.reshape(bp, 1, 2, N_HEADS, HEAD, HEAD)
    new_n = nc_.reshape(bp, 1, 2, N_HEADS, HEAD)
    new_m = mc[:, :, 0].reshape(bp, 1, 2, N_HEADS)
    new_h = hc.reshape(bp, 1, 2, D_RG)
    return (y_prompt, y_sample, new_c, new_n, new_m, new_h)
```

```python
import functools

import jax
import jax.numpy as jnp
from jax import lax
from jax.experimental import pallas as pl
from jax.experimental.pallas import tpu as pltpu
from jax.experimental.pallas import tpu_sc as plsc

F32 = jnp.float32
BF16 = jnp.bfloat16
I32 = jnp.int32
HIGHEST = lax.Precision.HIGHEST

D_MODEL = 1024
N_MOD = 6
D_ML = 512
N_HEADS = 4
HEAD = 128
D_RG = 512
N_RG_BLOCKS = 8
RG_BLOCK = 64
RG_C = 8.0
N_GATE = 16
N_EXPERTS = 64
N_GROUPS = 8
GROUP = 8
TOPK_GROUPS = 4
TOP_K = 8
D_EXPERT = 256
ROUTED_SCALE = 2.5
EPS = 1e-6
GRID_W = 64
POS_BASE = 10000.0

RG_SEG_PAD = 8
SC_WINDOW = 128
D_PACK = D_MODEL // 2
EXPERT_ROWS = 512

SUBLANES = 8
LANES = 128
VMEM_LIMIT = 48 * 1024 * 1024


def _params(*sem, vmem=VMEM_LIMIT):
    return pltpu.CompilerParams(dimension_semantics=sem, vmem_limit_bytes=vmem)


def _tiles(t, per_sequence_mod):
    cap = t if per_sequence_mod else 1 << 30
    return dict(
        tok=min(512, cap),
        chunk=min(256, t),
        scan=min(512, t),
    )


def _silu(x):
    return x * jax.nn.sigmoid(x)


def _softplus(x):
    return jnp.maximum(x, 0.0) + jnp.log1p(jnp.exp(-jnp.abs(x)))


def _rms(x, g):
    return x * lax.rsqrt(jnp.mean(x * x, axis=-1, keepdims=True) + EPS) * g


def _pack_bf16_pairs(x):
    w = x.shape[1] // 2
    hi = lax.bitcast_convert_type(x[:, :w].astype(BF16).astype(F32), I32)
    lo = lax.bitcast_convert_type(x[:, w:].astype(BF16).astype(F32), I32)
    return hi | lax.shift_right_logical(lo, jnp.full(lo.shape, 16, I32))


def _unpack_bf16_pairs(p):
    hi = lax.bitcast_convert_type(p & jnp.int32(-65536), F32)
    lo = lax.bitcast_convert_type(lax.shift_left(p, jnp.full(p.shape, 16, I32)), F32)
    return hi, lo


def _ada_kernel(c_ref, w_ref, b_ref, o_ref):
    s = _silu(c_ref[...])
    o_ref[...] = jnp.dot(s, w_ref[...], precision=HIGHEST, preferred_element_type=F32) + b_ref[...]


def _ada(cvecs, w_ada, b_ada):
    n_out = w_ada.shape[1]
    tn = 1536
    return pl.pallas_call(
        _ada_kernel,
        grid=(n_out // tn,),
        in_specs=[
            pl.BlockSpec((SUBLANES, D_MODEL), lambda j: (0, 0)),
            pl.BlockSpec((D_MODEL, tn), lambda j: (0, j)),
            pl.BlockSpec((1, tn), lambda j: (0, j)),
        ],
        out_specs=pl.BlockSpec((SUBLANES, tn), lambda j: (0, j)),
        out_shape=jax.ShapeDtypeStruct((SUBLANES, n_out), F32),
        name="ada",
        compiler_params=_params("arbitrary"),
    )(cvecs, w_ada, b_ada.reshape(1, n_out))


def _pos_tile(rt_ref, ct_ref):
    left = jnp.concatenate([jnp.broadcast_to(rt_ref[r:r + 1, :], (GRID_W, rt_ref.shape[1]))
                            for r in range(rt_ref.shape[0])], axis=0)
    right = jnp.concatenate([ct_ref[...]] * rt_ref.shape[0], axis=0)
    return jnp.concatenate([left, right], axis=1)


def _in_proj_kernel(x_ref, rt_ref, ct_ref, mod_ref, n1_ref, wq_ref, wr_ref, wgh_ref, gb_ref,
                    qt_ref, k_ref, kt_ref, v_ref, vt_ref, ot_ref, xr_ref, gr_ref, g_ref, gt_ref):
    x = x_ref[...] + _pos_tile(rt_ref, ct_ref)
    hn = _rms(x, n1_ref[...]) * (1.0 + mod_ref[0, 1:2, :]) + mod_ref[0, 0:1, :]
    hb = hn.astype(BF16)
    z = jnp.dot(hb, wq_ref[...], preferred_element_type=F32)
    k = z[:, D_ML:2 * D_ML] * (HEAD ** -0.5)
    v = z[:, 2 * D_ML:3 * D_ML]
    qt_ref[...] = z[:, 0:D_ML].T.astype(BF16)
    k_ref[...] = k.astype(BF16)
    kt_ref[...] = k.T.astype(BF16)
    v_ref[...] = v.astype(BF16)
    vt_ref[...] = v.T.astype(BF16)
    ot_ref[...] = z[:, 3 * D_ML:4 * D_ML].T
    zr = jnp.dot(hb, wr_ref[...], preferred_element_type=F32)
    xr_ref[...] = zr[:, 0:D_RG]
    gr_ref[...] = zr[:, D_RG:2 * D_RG]
    zg = zr[:, 2 * D_RG:2 * D_RG + LANES]
    h_lo = (hn - hb.astype(F32)).astype(BF16)
    g = (zg + pltpu.roll(zg, LANES - N_GATE, 1)
         + jnp.dot(h_lo, wgh_ref[...], preferred_element_type=F32) + gb_ref[...])
    col = lax.broadcasted_iota(I32, g.shape, 1)
    g = jnp.where((col & 4) != 0, -_softplus(-g), g)
    g_ref[...] = g[:, 0:N_GATE]
    gt_ref[...] = g.T[0:N_GATE, :]


def _mod_index(bm, tm, t):
    if bm > 1:
        return lambda i: ((i * tm) // t, 0, 0)
    return lambda i: (0, 0, 0)


def _pos_specs(rtab, ctab, tm):
    rows = tm // GRID_W
    period = rtab.shape[0] // rows
    return [pl.BlockSpec((rows, D_MODEL // 2), lambda i: (i % period, 0)),
            pl.BlockSpec((GRID_W, D_MODEL // 2), lambda i: (0, 0))]


def _in_proj(x2d, rtab, ctab, mod, t, tm, norm1, wq, wr, wgh, gbias):
    n = x2d.shape[0]
    tok = lambda i: (i, 0)
    tok_t = lambda i: (0, i)
    const = lambda i: (0, 0)
    f = lambda w: jax.ShapeDtypeStruct((n, w), F32)
    row16 = jax.ShapeDtypeStruct((n, D_ML), BF16)
    col16 = jax.ShapeDtypeStruct((D_ML, n), BF16)
    return pl.pallas_call(
        _in_proj_kernel,
        grid=(n // tm,),
        in_specs=[pl.BlockSpec((tm, D_MODEL), tok)] + _pos_specs(rtab, ctab, tm) + [
            pl.BlockSpec((1, N_MOD, D_MODEL), _mod_index(mod.shape[0], tm, t)),
            pl.BlockSpec((1, D_MODEL), const),
            pl.BlockSpec((D_MODEL, 4 * D_ML), const),
            pl.BlockSpec((D_MODEL, 2 * D_RG + LANES), const),
            pl.BlockSpec((D_MODEL, LANES), const),
            pl.BlockSpec((1, LANES), const),
        ],
        out_specs=[pl.BlockSpec((D_ML, tm), tok_t), pl.BlockSpec((tm, D_ML), tok), pl.BlockSpec((D_ML, tm), tok_t),
                   pl.BlockSpec((tm, D_ML), tok), pl.BlockSpec((D_ML, tm), tok_t), pl.BlockSpec((D_ML, tm), tok_t),
                   pl.BlockSpec((tm, D_RG), tok), pl.BlockSpec((tm, D_RG), tok),
                   pl.BlockSpec((tm, N_GATE), tok), pl.BlockSpec((N_GATE, tm), tok_t)],
        out_shape=[col16, row16, col16, row16, col16, jax.ShapeDtypeStruct((D_ML, n), F32),
                   f(D_RG), f(D_RG), f(N_GATE), jax.ShapeDtypeStruct((N_GATE, n), F32)],
        name="in_proj",
        compiler_params=_params("arbitrary"),
    )(x2d, rtab, ctab, mod, norm1, wq, wr, wgh, gbias)


def _mlstm_kernel(chunk, nc, emit_state, *refs):
    (qtf, kf, ktf, vf, vtf, gf, gtf, qtb, kb, ktb, vb, vtb, gb, gtb, c0_ref, n0_ref, m0_ref) = refs[:17]
    if emit_state:
        hf_ref, hb_ref, c_out, n_out, m_out, c_sc, n_sc, m_sc = refs[17:]
    else:
        hf_ref, hb_ref, c_sc, n_sc, m_sc = refs[17:]
    i = pl.program_id(1)

    @pl.when(i == 0)
    def _():
        c_sc[...] = c0_ref[0]
        n_sc[...] = n0_ref[0]
        m_sc[...] = m0_ref[0]

    key = lax.broadcasted_iota(I32, (chunk, chunk), 0)
    qry = lax.broadcasted_iota(I32, (chunk, chunk), 1)
    hd = []
    for d, (qt_ref, k_ref, kt_ref, v_ref, vt_ref, g_ref, gt_ref, h_ref) in enumerate(
            ((qtf, kf, ktf, vf, vtf, gf, gtf, hf_ref), (qtb, kb, ktb, vb, vtb, gb, gtb, hb_ref))):
        tri = (key <= qry) if d == 0 else (key >= qry)
        tri_t = (qry <= key) if d == 0 else (qry >= key)
        g = g_ref[...]
        gt = gt_ref[...]
        brow = jnp.dot(gt, tri.astype(F32), precision=HIGHEST, preferred_element_type=F32)
        bcol = jnp.dot(tri_t.astype(F32), g, precision=HIGHEST, preferred_element_type=F32)
        blast = bcol[chunk - 1:chunk, :] if d == 0 else bcol[0:1, :]
        for h in range(N_HEADS):
            ci = d * 8 + h
            cf = d * 8 + 4 + h
            j = d * N_HEADS + h
            sl = slice(h * HEAD, (h + 1) * HEAD)
            hd.append(dict(
                j=j, sl=sl, tri=tri, h_ref=h_ref, qt=qt_ref[sl, :], k=k_ref[:, sl], kt=kt_ref[sl, :],
                v=v_ref[:, sl], vt=vt_ref[sl, :], b_row=brow[cf:cf + 1, :],
                gate_col=g[:, ci:ci + 1] - bcol[:, cf:cf + 1], gate_row=gt[ci:ci + 1, :] - brow[cf:cf + 1, :],
                b_last=blast[:, cf:cf + 1], m_prev=m_sc[j:j + 1, 0:1], c_prev=c_sc[j], n_prev=n_sc[j:j + 1, :]))
    for x in hd:
        top = jnp.max(jnp.where(x['tri'], x['gate_col'], -jnp.inf), axis=0, keepdims=True)
        mx = jnp.maximum(x['m_prev'], top)
        x['dm'] = jnp.exp(jnp.where(x['tri'], x['gate_col'] - mx, -jnp.inf))
        x['w_inter'] = jnp.exp(x['m_prev'] - mx)
        x['floor'] = jnp.exp(-(x['b_row'] + mx))
    for x in hd:
        x['st'] = jnp.dot(x['k'], x['qt'], preferred_element_type=F32) * x['dm']
    for x in hd:
        inter = lax.dot_general(x['c_prev'].astype(BF16), x['qt'], (((0,), (0,)), ((), ())),
                                preferred_element_type=F32)
        num = x['w_inter'] * inter + jnp.dot(x['vt'], x['st'].astype(BF16), preferred_element_type=F32)
        qn = jnp.dot(jnp.broadcast_to(x['n_prev'], (SUBLANES, HEAD)).astype(BF16), x['qt'],
                     preferred_element_type=F32)[0:1, :]
        den = x['w_inter'] * qn + jnp.sum(x['st'], axis=0, keepdims=True)
        x['h_ref'][x['sl'], :] = num / jnp.maximum(jnp.abs(den), x['floor'])
    for x in hd:
        j = x['j']
        log_w = x['b_last'] + x['gate_row']
        m_new = jnp.maximum(x['b_last'] + x['m_prev'], jnp.max(log_w, axis=1, keepdims=True))
        decay = jnp.exp(x['b_last'] + x['m_prev'] - m_new)
        w_row = jnp.exp(log_w - m_new)
        kwt = (x['kt'].astype(F32) * w_row).astype(BF16)
        c_sc[j] = decay * x['c_prev'] + jnp.dot(kwt, x['v'], preferred_element_type=F32)
        n_sc[j:j + 1, :] = decay * x['n_prev'] + jnp.dot(
            jnp.broadcast_to(w_row, (SUBLANES, chunk)).astype(BF16), x['k'], preferred_element_type=F32)[0:1, :]
        m_sc[j:j + 1, :] = jnp.broadcast_to(m_new, (1, HEAD))

    if emit_state:
        @pl.when(i == nc - 1)
        def _():
            c_out[0] = c_sc[...]
            n_out[0] = n_sc[...]
            m_out[0] = m_sc[...]


def _state_index(bm):
    if bm > 1:
        return lambda b, i: (b,) + (0,) * 3, lambda b, i: (b, 0, 0)
    return lambda b, i: (0,) * 4, lambda b, i: (0, 0, 0)


def _mlstm(qt, k, kt, v, vt, g, gt, c0, n0, m0, b, t, chunk, emit_state):
    nc = t // chunk
    n = b * t
    nd = 2 * N_HEADS
    fwd = lambda bi, i: (bi * nc + i, 0)
    bwd = lambda bi, i: (bi * nc + nc - 1 - i, 0)
    fwd_t = lambda bi, i: (0, bi * nc + i)
    bwd_t = lambda bi, i: (0, bi * nc + nc - 1 - i)
    c_idx, n_idx = _state_index(c0.shape[0])
    rows = lambda m: pl.BlockSpec((chunk, D_ML), m)
    cols = lambda m: pl.BlockSpec((D_ML, chunk), m)
    one_dir = lambda m, mt: [cols(mt), rows(m), cols(mt), rows(m), cols(mt),
                             pl.BlockSpec((chunk, N_GATE), m), pl.BlockSpec((N_GATE, chunk), mt)]
    in_specs = (one_dir(fwd, fwd_t) + one_dir(bwd, bwd_t)
                + [pl.BlockSpec((1, nd, HEAD, HEAD), c_idx),
                   pl.BlockSpec((1, nd, HEAD), n_idx), pl.BlockSpec((1, nd, HEAD), n_idx)])
    out_specs = [cols(fwd_t), cols(bwd_t)]
    out_shape = [jax.ShapeDtypeStruct((D_ML, n), F32)] * 2
    if emit_state:
        out_specs += [pl.BlockSpec((1, nd, HEAD, HEAD), lambda bi, i: (bi, 0, 0, 0)),
                      pl.BlockSpec((1, nd, HEAD), lambda bi, i: (bi, 0, 0)),
                      pl.BlockSpec((1, nd, HEAD), lambda bi, i: (bi, 0, 0))]
        out_shape += [jax.ShapeDtypeStruct((b, nd, HEAD, HEAD), F32),
                      jax.ShapeDtypeStruct((b, nd, HEAD), F32),
                      jax.ShapeDtypeStruct((b, nd, HEAD), F32)]
    return pl.pallas_call(
        functools.partial(_mlstm_kernel, chunk, nc, emit_state),
        grid=(b, nc),
        in_specs=in_specs,
        out_specs=out_specs,
        out_shape=out_shape,
        scratch_shapes=[pltpu.VMEM((nd, HEAD, HEAD), F32), pltpu.VMEM((nd, HEAD), F32),
                        pltpu.VMEM((nd, HEAD), F32)],
        name="mlstm",
        compiler_params=_params("arbitrary", "arbitrary"),
    )(qt, k, kt, v, vt, g, gt, qt, k, kt, v, vt, g, gt, c0, n0, m0)


def _neg_expm1(x):
    u = jnp.exp(x)
    near = jnp.where(u == 1.0, x, (u - 1.0) * x / jnp.log(u))
    return -jnp.where(x < -0.5, u - 1.0, near)


def _rglru_kernel(tb, nb, emit_state, *refs):
    (xf, xf_prev, xf_next, xb, xb_prev, xb_next, h0_ref, w_ref, bias_ref, lam_ref, cw_ref, cb_ref) = refs[:12]
    n_out = 3 if emit_state else 2
    hf_ref, hb_ref = refs[12:14]
    hfin_ref = refs[14] if emit_state else None
    carry, af_sc, uf_sc, ab_sc, ub_sc, hf_sc, pf_sc, hb_sc, pb_sc = refs[12 + n_out:]
    i = pl.program_id(1)

    @pl.when(i == 0)
    def _():
        carry[...] = h0_ref[0]

    row = lax.broadcasted_iota(I32, (tb, D_RG), 0)
    cw = cw_ref[...]
    softplus_neg_lam = _softplus(-lam_ref[...])

    def conv(main_ref, prev_ref, next_ref, first, last):
        main = main_ref[...]
        prev = jnp.where(first, 0.0, prev_ref[...])
        nxt = jnp.where(last, 0.0, next_ref[...])
        xm2 = jnp.where(row == 0, prev[6:7, :], jnp.where(row == 1, prev[7:8, :], pltpu.roll(main, 2, 0)))
        xm1 = jnp.where(row == 0, prev[7:8, :], pltpu.roll(main, 1, 0))
        xp1 = jnp.where(row == tb - 1, nxt[0:1, :], pltpu.roll(main, tb - 1, 0))
        return cb_ref[...] + xm2 * cw[0:1, :] + xm1 * cw[1:2, :] + main * cw[2:3, :] + xp1 * cw[3:4, :]

    def recurrence_terms(xc, d):
        z = jnp.dot(xc.astype(BF16), w_ref[:, d * 2 * D_RG:(d + 1) * 2 * D_RG],
                    preferred_element_type=F32) + bias_ref[:, d * 2 * D_RG:(d + 1) * 2 * D_RG]
        r = jax.nn.sigmoid(z[:, 0:D_RG])
        ig = jax.nn.sigmoid(z[:, D_RG:2 * D_RG])
        log_a = -RG_C * r * softplus_neg_lam[d:d + 1, :]
        a = jnp.exp(log_a)
        u = jnp.sqrt(_neg_expm1(2.0 * log_a)) * (ig * xc)
        return a, u

    a_f, u_f = recurrence_terms(conv(xf, xf_prev, xf_next, i == 0, i == nb - 1), 0)
    a_b, u_b = recurrence_terms(conv(xb, xb_prev, xb_next, i == nb - 1, i == 0), 1)
    seg = tb // SUBLANES
    pitch = seg + RG_SEG_PAD
    ncol = D_RG // LANES
    for lc in range(ncol):
        lanes = slice(lc * LANES, (lc + 1) * LANES)
        for s in range(SUBLANES):
            src = slice(s * seg, (s + 1) * seg)
            dst = slice(s * pitch, s * pitch + seg)
            af_sc[lc, dst, :], uf_sc[lc, dst, :] = a_f[src, lanes], u_f[src, lanes]
            ab_sc[lc, dst, :], ub_sc[lc, dst, :] = a_b[src, lanes], u_b[src, lanes]
    slab = lambda k: (slice(None), pl.ds(k, SUBLANES, stride=pitch), slice(None))
    hf = jnp.zeros((ncol, SUBLANES, LANES), F32)
    hb = jnp.zeros((ncol, SUBLANES, LANES), F32)
    pf = jnp.ones((ncol, SUBLANES, LANES), F32)
    pb = jnp.ones((ncol, SUBLANES, LANES), F32)
    for k in range(seg):
        kb = seg - 1 - k
        ak = af_sc[slab(k)]
        hf = ak * hf + uf_sc[slab(k)]
        pf = pf * ak
        hf_sc[slab(k)] = hf
        pf_sc[slab(k)] = pf
        ak = ab_sc[slab(kb)]
        hb = ak * hb + ub_sc[slab(kb)]
        pb = pb * ak
        hb_sc[slab(kb)] = hb
        pb_sc[slab(kb)] = pb
    for lc in range(ncol):
        lanes = slice(lc * LANES, (lc + 1) * LANES)
        c = carry[0:1, lanes]
        cin_f = []
        for s in range(SUBLANES):
            cin_f.append(c)
            c = pf[lc, s:s + 1, :] * c + hf[lc, s:s + 1, :]
        carry[0:1, lanes] = c
        c = carry[1:2, lanes]
        cin_b = [None] * SUBLANES
        for s in reversed(range(SUBLANES)):
            cin_b[s] = c
            c = pb[lc, s:s + 1, :] * c + hb[lc, s:s + 1, :]
        carry[1:2, lanes] = c
        for s in range(SUBLANES):
            rows = slice(s * seg, (s + 1) * seg)
            src = slice(s * pitch, s * pitch + seg)
            hf_ref[rows, lanes] = hf_sc[lc, src, :] + pf_sc[lc, src, :] * cin_f[s]
            hb_ref[rows, lanes] = hb_sc[lc, src, :] + pb_sc[lc, src, :] * cin_b[s]

    if emit_state:
        @pl.when(i == nb - 1)
        def _():
            hfin_ref[0] = carry[...]


def _rglru(xr, h0, wbd, bias, lam, cw, cb, b, t, tb, emit_state):
    nb = t // tb
    n = b * t
    r8 = tb // SUBLANES
    last8 = n // SUBLANES - 1
    fwd = lambda bi, i: (bi * nb + i, 0)
    bwd = lambda bi, i: (bi * nb + nb - 1 - i, 0)
    fwd_prev = lambda bi, i: (jnp.maximum((bi * nb + i) * r8 - 1, 0), 0)
    fwd_next = lambda bi, i: (jnp.minimum((bi * nb + i + 1) * r8, last8), 0)
    bwd_prev = lambda bi, i: (jnp.maximum((bi * nb + nb - 1 - i) * r8 - 1, 0), 0)
    bwd_next = lambda bi, i: (jnp.minimum((bi * nb + nb - i) * r8, last8), 0)
    const = lambda bi, i: (0, 0)
    h_idx = (lambda bi, i: (bi, 0, 0)) if h0.shape[0] > 1 else (lambda bi, i: (0, 0, 0))
    halo = lambda m: pl.BlockSpec((SUBLANES, D_RG), m)
    in_specs = [pl.BlockSpec((tb, D_RG), fwd), halo(fwd_prev), halo(fwd_next),
                pl.BlockSpec((tb, D_RG), bwd), halo(bwd_prev), halo(bwd_next),
                pl.BlockSpec((1, 2, D_RG), h_idx),
                pl.BlockSpec((D_RG, 4 * D_RG), const), pl.BlockSpec((1, 4 * D_RG), const),
                pl.BlockSpec((2, D_RG), const), pl.BlockSpec((4, D_RG), const), pl.BlockSpec((1, D_RG), const)]
    out_specs = [pl.BlockSpec((tb, D_RG), fwd), pl.BlockSpec((tb, D_RG), bwd)]
    out_shape = [jax.ShapeDtypeStruct((n, D_RG), F32)] * 2
    if emit_state:
        out_specs.append(pl.BlockSpec((1, 2, D_RG), lambda bi, i: (bi, 0, 0)))
        out_shape.append(jax.ShapeDtypeStruct((b, 2, D_RG), F32))
    return pl.pallas_call(
        functools.partial(_rglru_kernel, tb, nb, emit_state),
        grid=(b, nb),
        in_specs=in_specs,
        out_specs=out_specs,
        out_shape=out_shape,
        scratch_shapes=[pltpu.VMEM((2, D_RG), F32)]
        + [pltpu.VMEM((D_RG // LANES, tb + SUBLANES * RG_SEG_PAD, LANES), F32)] * 8,
        name="rglru",
        compiler_params=_params("arbitrary", "arbitrary"),
    )(xr, xr, xr, xr, xr, xr, h0, wbd, bias, lam, cw, cb)


def _route(s, sb):
    tm = s.shape[1]
    neg = -jnp.inf
    sub = lax.broadcasted_iota(I32, (GROUP, tm), 0)
    blocks = [sb[gi * GROUP:(gi + 1) * GROUP, :] for gi in range(N_GROUPS)]
    gscore = []
    for blk in blocks:
        m1 = jnp.max(blk, axis=0, keepdims=True)
        first = jnp.min(jnp.where(blk == m1, sub, GROUP), axis=0, keepdims=True)
        m2 = jnp.max(jnp.where(sub == first, neg, blk), axis=0, keepdims=True)
        gscore.append(m1 + m2)
    masked = []
    for gi in range(N_GROUPS):
        rank = jnp.zeros((1, tm), F32)
        for gj in range(N_GROUPS):
            if gj == gi:
                continue
            ahead = (gscore[gj] >= gscore[gi]) if gj < gi else (gscore[gj] > gscore[gi])
            rank = rank + jnp.where(ahead, 1.0, 0.0)
        masked.append(jnp.where(rank < TOPK_GROUPS, blocks[gi], neg))
    v = jnp.concatenate(masked, axis=0)
    eid = lax.broadcasted_iota(I32, (N_EXPERTS, tm), 0)
    sel = jnp.zeros((N_EXPERTS, tm), F32)
    picks = []
    for _ in range(TOP_K):
        mx = jnp.max(v, axis=0, keepdims=True)
        idx = jnp.min(jnp.where(v == mx, eid, N_EXPERTS), axis=0, keepdims=True)
        pick = eid == idx
        picks.append(pick)
        sel = jnp.where(pick, 1.0, sel)
        v = jnp.where(pick, neg, v)
    ws = s * sel
    return ws / jnp.sum(ws, axis=0, keepdims=True) * ROUTED_SCALE, sel, picks


def _mix_out_kernel(hmf_ref, hmb_ref, ot_ref, hrf_ref, hrb_ref, gr_ref, x_ref, rt_ref, ct_ref, mod_ref,
                    mln_ref, rgn_ref, wo_ml_ref, wo_rg_ref, n2_ref, rwt_ref, rb_ref, cnt0_ref,
                    x1_ref, hn2p_ref, ek_ref, pk_ref, wtok_ref, cnt_ref, cnt_sc):
    i = pl.program_id(0)
    tm = x_ref.shape[0]

    @pl.when(i == 0)
    def _():
        cnt_sc[...] = cnt0_ref[...].astype(F32)

    hm = hmf_ref[...] + hmb_ref[...]
    heads = []
    for h in range(N_HEADS):
        seg = hm[h * HEAD:(h + 1) * HEAD, :]
        heads.append(seg * lax.rsqrt(jnp.mean(seg * seg, axis=0, keepdims=True) + EPS))
    y_ml_t = jnp.concatenate(heads, axis=0) * mln_ref[...] * jax.nn.sigmoid(ot_ref[...])
    y_rg = _rms(hrf_ref[...] + hrb_ref[...], rgn_ref[...]) * jax.nn.gelu(gr_ref[...])
    mix = (lax.dot_general(y_ml_t.astype(BF16), wo_ml_ref[...], (((0,), (0,)), ((), ())),
                           preferred_element_type=F32)
           + jnp.dot(y_rg.astype(BF16), wo_rg_ref[...], preferred_element_type=F32))
    x1 = x_ref[...] + _pos_tile(rt_ref, ct_ref) + mod_ref[0, 2:3, :] * mix
    x1_ref[...] = x1
    hn2 = _rms(x1, n2_ref[...]) * (1.0 + mod_ref[0, 4:5, :]) + mod_ref[0, 3:4, :]
    hn2p_ref[...] = _pack_bf16_pairs(hn2)
    logits_t = lax.dot_general(rwt_ref[...], hn2, (((1,), (1,)), ((), ())), precision=HIGHEST,
                               preferred_element_type=F32)
    s = jax.nn.sigmoid(logits_t)
    wt, sel, picks = _route(s, s + rb_ref[...])

    earlier = (lax.broadcasted_iota(I32, (tm, tm), 0) < lax.broadcasted_iota(I32, (tm, tm), 1))
    prefix = jnp.dot(sel.astype(BF16), earlier.astype(BF16), preferred_element_type=F32)
    pos_all = cnt_sc[:, 0:1] + prefix
    eid = lax.broadcasted_iota(I32, (N_EXPERTS, tm), 0)
    eid_f = eid.astype(F32)
    row8 = lax.broadcasted_iota(I32, (TOP_K, tm), 0)
    ek = jnp.zeros((TOP_K, tm), F32)
    pk = jnp.zeros((TOP_K, tm), F32)
    wk = jnp.zeros((N_EXPERTS, tm), F32)
    for k, pick in enumerate(picks):
        take = lambda a: jnp.sum(jnp.where(pick, a, 0.0), axis=0, keepdims=True)
        ek = jnp.where(row8 == k, take(eid_f), ek)
        pk = jnp.where(row8 == k, take(pos_all), pk)
        wk = jnp.where(eid == k, take(wt), wk)
    ek_ref[...] = ek.astype(I32)
    pk_ref[...] = pk.astype(I32)
    wtok_ref[...] = wk.T
    cnt_sc[...] += jnp.broadcast_to(jnp.sum(sel, axis=1, keepdims=True), cnt_sc.shape)

    @pl.when(i == pl.num_programs(0) - 1)
    def _():
        cnt_ref[...] = cnt_sc[...].astype(I32)


def _mix_out(hmf, hmb, ot, hrf, hrb, gr, x2d, rtab, ctab, mod, t, tm, mln, rgn, wo_ml, wo_rg, norm2, rwt, rbias,
             cnt0):
    n = x2d.shape[0]
    tok = lambda i: (i, 0)
    tok_t = lambda i: (0, i)
    const = lambda i: (0, 0)
    return pl.pallas_call(
        _mix_out_kernel,
        grid=(n // tm,),
        in_specs=[pl.BlockSpec((D_ML, tm), tok_t)] * 3 + [pl.BlockSpec((tm, D_RG), tok)] * 3 + [
            pl.BlockSpec((tm, D_MODEL), tok)] + _pos_specs(rtab, ctab, tm) + [
            pl.BlockSpec((1, N_MOD, D_MODEL), _mod_index(mod.shape[0], tm, t)),
            pl.BlockSpec((D_ML, 1), const), pl.BlockSpec((1, D_RG), const),
            pl.BlockSpec((D_ML, D_MODEL), const), pl.BlockSpec((D_RG, D_MODEL), const),
            pl.BlockSpec((1, D_MODEL), const),
            pl.BlockSpec((N_EXPERTS, D_MODEL), const), pl.BlockSpec((N_EXPERTS, 1), const),
            pl.BlockSpec((N_EXPERTS, LANES), const),
        ],
        out_specs=[pl.BlockSpec((tm, D_MODEL), tok), pl.BlockSpec((tm, D_PACK), tok),
                   pl.BlockSpec((TOP_K, tm), tok_t), pl.BlockSpec((TOP_K, tm), tok_t),
                   pl.BlockSpec((tm, N_EXPERTS), tok), pl.BlockSpec((N_EXPERTS, LANES), const)],
        out_shape=[jax.ShapeDtypeStruct((n, D_MODEL), F32), jax.ShapeDtypeStruct((n, D_PACK), I32),
                   jax.ShapeDtypeStruct((TOP_K, n), I32), jax.ShapeDtypeStruct((TOP_K, n), I32),
                   jax.ShapeDtypeStruct((n, N_EXPERTS), F32), jax.ShapeDtypeStruct((N_EXPERTS, LANES), I32)],
        scratch_shapes=[pltpu.VMEM((N_EXPERTS, LANES), F32)],
        name="mix_out",
        compiler_params=_params("arbitrary"),
    )(hmf, hmb, ot, hrf, hrb, gr, x2d, rtab, ctab, mod, mln, rgn, wo_ml, wo_rg, norm2, rwt, rbias, cnt0)


def _sc_mesh():
    return plsc.VectorSubcoreMesh(core_axis_name="core", subcore_axis_name="subcore")


def _sc_worker():
    info = plsc.get_sparse_core_info()
    return lax.axis_index("subcore") * info.num_cores + lax.axis_index("core"), info.num_cores * info.num_subcores


def _sc_dispatch(xp, dest3, n_slots):
    n, w = xp.shape
    nwin = n // SC_WINDOW

    @pl.kernel(out_type=jax.ShapeDtypeStruct((n_slots, w), xp.dtype), mesh=_sc_mesh(),
               scratch_types=[pltpu.VMEM((SC_WINDOW, w), xp.dtype), pltpu.VMEM((TOP_K, SC_WINDOW), I32)],
               name="sc_dispatch")
    def k(x_hbm, i_hbm, o_hbm, x_v, i_v):
        wid, nworkers = _sc_worker()
        per = nwin // nworkers

        @pl.loop(0, per)
        def _(s):
            win = wid * per + s
            pltpu.sync_copy(x_hbm.at[pl.ds(win * SC_WINDOW, SC_WINDOW)], x_v)
            pltpu.sync_copy(i_hbm.at[win], i_v)
            for j in range(TOP_K):
                pltpu.sync_copy(x_v, o_hbm.at[i_v.at[j]])

    return k(xp, dest3)


def _sc_combine_gather(ys, dest3):
    nwin = dest3.shape[0]
    w = ys.shape[1]

    @pl.kernel(out_type=jax.ShapeDtypeStruct((nwin, TOP_K, SC_WINDOW, w), ys.dtype), mesh=_sc_mesh(),
               scratch_types=[pltpu.VMEM((SC_WINDOW, w), ys.dtype), pltpu.VMEM((TOP_K, SC_WINDOW), I32)],
               name="sc_combine")
    def k(y_hbm, i_hbm, o_hbm, y_v, i_v):
        wid, nworkers = _sc_worker()
        per = nwin // nworkers

        @pl.loop(0, per)
        def _(s):
            win = wid * per + s
            pltpu.sync_copy(i_hbm.at[win], i_v)
            for j in range(TOP_K):
                pltpu.sync_copy(y_hbm.at[i_v.at[j]], y_v)
                pltpu.sync_copy(y_v, o_hbm.at[win, j])

    return k(ys, dest3)


def _swiglu(x, w13):
    h = jnp.dot(x, w13, preferred_element_type=F32)
    return _silu(h[:, 0:D_EXPERT]) * h[:, D_EXPERT:2 * D_EXPERT]


def _unpack_rows_bf16(p):
    hi, lo = _unpack_bf16_pairs(p)
    return jnp.concatenate([hi.astype(BF16), lo.astype(BF16)], axis=1)


def _expert_kernel(rows, start_ref, nblk_ref, xs_hbm, w1_ref, w3_ref, w2_ref, ys_hbm, xbuf, ybuf, sem_in, sem_out):
    e = pl.program_id(0)
    n = nblk_ref[e]
    s0 = start_ref[e]
    niter = lax.shift_right_logical(n + 1, 1)
    w1 = w1_ref[0].astype(BF16)
    w3 = w3_ref[0].astype(BF16)
    w2 = w2_ref[0].astype(BF16)

    def x_copy(it, slot):
        first = pl.multiple_of((s0 + 2 * it) * rows, rows)
        return pltpu.make_async_copy(xs_hbm.at[pl.ds(first, 2 * rows)], xbuf.at[slot], sem_in.at[slot])

    def y_copy(it, slot, half):
        first = pl.multiple_of((s0 + 2 * it + half) * rows, rows)
        return pltpu.make_async_copy(ybuf.at[slot, pl.ds(half * rows, rows)], ys_hbm.at[pl.ds(first, rows)],
                                     sem_out.at[slot, half])

    @pl.when(niter > 0)
    def _():
        x_copy(0, 0).start()

    def body(it, carry):
        slot = it & 1
        x_copy(it, slot).wait()

        @pl.when(it + 1 < niter)
        def _():
            x_copy(it + 1, 1 - slot).start()

        @pl.when(it >= 2)
        def _():
            y_copy(it - 2, slot, 0).wait()
            y_copy(it - 2, slot, 1).wait()

        for half in range(2):
            sl = pl.ds(half * rows, rows)
            x = _unpack_rows_bf16(xbuf[slot, sl, :])
            h = (_silu(jnp.dot(x, w1, preferred_element_type=F32)) * jnp.dot(x, w3, preferred_element_type=F32))
            ybuf[slot, sl, :] = _pack_bf16_pairs(jnp.dot(h.astype(BF16), w2, preferred_element_type=F32))
        y_copy(it, slot, 0).start()

        @pl.when(2 * it + 1 < n)
        def _():
            y_copy(it, slot, 1).start()

        return carry

    lax.fori_loop(0, niter, body, 0)

    @pl.when(niter >= 2)
    def _():
        it = niter - 2
        y_copy(it, it & 1, 0).wait()
        y_copy(it, it & 1, 1).wait()

    @pl.when(niter >= 1)
    def _():
        it = niter - 1
        y_copy(it, it & 1, 0).wait()

        @pl.when(2 * it + 1 < n)
        def _():
            y_copy(it, it & 1, 1).wait()


def _experts(xs, block_start, block_count, w1, w3, w2, rows):
    up = pl.BlockSpec((1, D_MODEL, D_EXPERT), lambda e, st, nb: (e, 0, 0))
    return pl.pallas_call(
        functools.partial(_expert_kernel, rows),
        grid_spec=pltpu.PrefetchScalarGridSpec(
            num_scalar_prefetch=2,
            grid=(N_EXPERTS,),
            in_specs=[pl.BlockSpec(memory_space=pl.ANY), up, up,
                      pl.BlockSpec((1, D_EXPERT, D_MODEL), lambda e, st, nb: (e, 0, 0))],
            out_specs=pl.BlockSpec(memory_space=pl.ANY),
            scratch_shapes=[pltpu.VMEM((2, 2 * rows, D_PACK), I32), pltpu.VMEM((2, 2 * rows, D_PACK), I32),
                            pltpu.SemaphoreType.DMA((2,)), pltpu.SemaphoreType.DMA((2, 2))],
        ),
        out_shape=jax.ShapeDtypeStruct(xs.shape, I32),
        name="experts",
        compiler_params=_params("arbitrary"),
    )(block_start, block_count, xs, w1, w3, w2)


def _moe_out_kernel(yk_ref, hn2p_ref, wtok_ref, sw13_ref, sw2_ref, x1_ref, mod_ref, nf_ref, y_ref):
    shared = jnp.dot(_swiglu(_unpack_rows_bf16(hn2p_ref[...]), sw13_ref[...]).astype(BF16), sw2_ref[...],
                     preferred_element_type=F32)
    w = wtok_ref[...]
    parts = []
    for wi in range(yk_ref.shape[0]):
        rows = slice(wi * SC_WINDOW, (wi + 1) * SC_WINDOW)
        a_hi = shared[rows, 0:D_PACK]
        a_lo = shared[rows, D_PACK:D_MODEL]
        for k in range(TOP_K):
            y_hi, y_lo = _unpack_bf16_pairs(yk_ref[wi, k])
            wc = w[rows, k:k + 1]
            a_hi = a_hi + wc * y_hi
            a_lo = a_lo + wc * y_lo
        parts.append(jnp.concatenate([a_hi, a_lo], axis=1))
    x2 = x1_ref[...] + mod_ref[0, 5:6, :] * jnp.concatenate(parts, axis=0)
    y_ref[...] = _rms(x2, nf_ref[...])


def _moe_out(yk, tok0, hn2p, wtok, sw13, sw2, x1, mod, t, tm, norm_final):
    n = hn2p.shape[0]
    tok = lambda i: (i, 0)
    const = lambda i: (0, 0)
    blk0 = tok0 // tm
    return pl.pallas_call(
        _moe_out_kernel,
        grid=(n // tm,),
        in_specs=[
            pl.BlockSpec((tm // SC_WINDOW, TOP_K, SC_WINDOW, D_PACK), lambda i: (i + blk0, 0, 0, 0)),
            pl.BlockSpec((tm, D_PACK), tok),
            pl.BlockSpec((tm, N_EXPERTS), tok),
            pl.BlockSpec((D_MODEL, 2 * D_EXPERT), const),
            pl.BlockSpec((D_EXPERT, D_MODEL), const),
            pl.BlockSpec((tm, D_MODEL), tok),
            pl.BlockSpec((1, N_MOD, D_MODEL), _mod_index(mod.shape[0], tm, t)),
            pl.BlockSpec((1, D_MODEL), const),
        ],
        out_specs=pl.BlockSpec((tm, D_MODEL), tok),
        out_shape=jax.ShapeDtypeStruct((n, D_MODEL), F32),
        name="moe_out",
        compiler_params=_params("arbitrary"),
    )(yk, hn2p, wtok, sw13, sw2, x1, mod, norm_final)


def _dispatch_plan(cnt, ek, pk, rows):
    n = ek.shape[1]
    nb = n * TOP_K // rows + N_EXPERTS
    nblk = (cnt + rows - 1) // rows
    start = jnp.cumsum(nblk) - nblk
    experts = jnp.arange(N_EXPERTS, dtype=I32)
    first_row = jnp.sum(jnp.where(ek[:, :, None] == experts, start * rows, 0), axis=-1)
    dest3 = (first_row + pk).reshape(TOP_K, n // SC_WINDOW, SC_WINDOW).transpose(1, 0, 2)
    return dest3, start.astype(I32), nblk.astype(I32), (nb + 1) * rows


def _grid_pos_tables(n_tokens, dim):
    quarter = dim // 4
    omega = 1.0 / (POS_BASE ** (jnp.arange(quarter, dtype=F32) / quarter))
    ra = jnp.arange(n_tokens // GRID_W).astype(F32)[:, None] * omega
    ca = jnp.arange(GRID_W).astype(F32)[:, None] * omega
    return (jnp.concatenate([jnp.sin(ra), jnp.cos(ra)], axis=-1),
            jnp.concatenate([jnp.sin(ca), jnp.cos(ca)], axis=-1))


def _block_diag(w):
    eye = jnp.eye(N_RG_BLOCKS, dtype=w.dtype)
    return jnp.einsum('nij,nm->nimj', w, eye).reshape(D_RG, D_RG)


def _layer_weights(l, norm1, w_in, mlstm_gate_bias, mlstm_norm, rg_conv_w, rg_conv_b, rg_wa, rg_ba, rg_wx,
                   rg_bx, rg_lambda, rg_norm, w_out, norm2, router_w, router_bias, exp_w1, exp_w3, exp_w2,
                   shared_w1, shared_w3, shared_w2):
    wi = w_in[l]
    c0, c1 = 4 * D_ML, 4 * D_ML + N_GATE
    wg = wi[:, c0:c1]
    wg_hi = wg.astype(BF16)
    wg_lo = (wg - wg_hi.astype(F32)).astype(BF16)
    zcols = lambda w: jnp.zeros((D_MODEL, w), BF16)
    return dict(
        norm1=norm1[l].reshape(1, D_MODEL),
        wq=wi[:, :c0].astype(BF16),
        wr=jnp.concatenate([wi[:, c1:].astype(BF16), wg_hi, wg_lo, zcols(LANES - 2 * N_GATE)], axis=1),
        wgh=jnp.concatenate([wg_hi, zcols(LANES - N_GATE)], axis=1),
        gbias=jnp.pad(mlstm_gate_bias[l].reshape(1, N_GATE), ((0, 0), (0, LANES - N_GATE))),
        mln=mlstm_norm[l].reshape(D_ML, 1),
        cw=rg_conv_w[l], cb=rg_conv_b[l].reshape(1, D_RG),
        wbd=jnp.concatenate([_block_diag(rg_wa[l, 0]), _block_diag(rg_wx[l, 0]),
                             _block_diag(rg_wa[l, 1]), _block_diag(rg_wx[l, 1])], axis=1).astype(BF16),
        rbias=jnp.concatenate([rg_ba[l, 0], rg_bx[l, 0], rg_ba[l, 1], rg_bx[l, 1]]).reshape(1, 4 * D_RG),
        lam=rg_lambda[l], rgn=rg_norm[l].reshape(1, D_RG),
        wo_ml=w_out[l, :D_ML].astype(BF16), wo_rg=w_out[l, D_ML:].astype(BF16),
        norm2=norm2[l].reshape(1, D_MODEL),
        rwt=router_w[l].T, rb=router_bias[l].reshape(N_EXPERTS, 1),
        w1=exp_w1[l], w3=exp_w3[l], w2=exp_w2[l],
        sw13=jnp.concatenate([shared_w1[l], shared_w3[l]], axis=-1).astype(BF16),
        sw2=shared_w2[l].astype(BF16),
    )


def _mixers(x2d, pos_tables, mod, c0, n0, m0, h0, cnt0, lw, b, t, emit_state):
    tl = _tiles(t, mod.shape[0] > 1)
    tm = tl['tok']
    rtab, ctab = pos_tables
    qt, k, kt, v, vt, ot, xr, gr, g, gt = _in_proj(x2d, rtab, ctab, mod, t, tm, lw['norm1'], lw['wq'], lw['wr'],
                                                   lw['wgh'], lw['gbias'])
    ml = _mlstm(qt, k, kt, v, vt, g, gt, c0, n0, m0, b, t, tl['chunk'], emit_state)
    rg = _rglru(xr, h0, lw['wbd'], lw['rbias'], lw['lam'], lw['cw'], lw['cb'], b, t, tl['scan'], emit_state)
    routed = _mix_out(ml[0], ml[1], ot, rg[0], rg[1], gr, x2d, rtab, ctab, mod, t, tm, lw['mln'], lw['rgn'],
                      lw['wo_ml'], lw['wo_rg'], lw['norm2'], lw['rwt'], lw['rb'], cnt0)
    return routed, ml[2:], rg[2:]


def _routed_experts(paths, lw):
    cnt = paths[-1][5][:, 0]
    hn2p = jnp.concatenate([p[1] for p in paths], axis=0)
    ek = jnp.concatenate([p[2] for p in paths], axis=1)
    pk = jnp.concatenate([p[3] for p in paths], axis=1)
    dest3, block_start, block_count, n_slots = _dispatch_plan(cnt, ek, pk, EXPERT_ROWS)
    xs = _sc_dispatch(hn2p, dest3, n_slots)
    ys = _experts(xs, block_start, block_count, lw['w1'], lw['w3'], lw['w2'], EXPERT_ROWS)
    return _sc_combine_gather(ys, dest3)


def kernel(x_prompt, x_sample, c, state_mlstm_C, state_mlstm_n, state_mlstm_m, state_rglru_h, c_ctx, w_ada, b_ada, norm1, w_in, mlstm_gate_bias, mlstm_norm, rg_conv_w, rg_conv_b, rg_wa, rg_ba, rg_wx, rg_bx, rg_lambda, rg_norm, w_out, norm2, router_w, router_bias, exp_w1, exp_w3, exp_w2, shared_w1, shared_w3, shared_w2, norm_final):
    bp, tp, _ = x_prompt.shape
    bs, ts, _ = x_sample.shape
    depth = w_ada.shape[0]
    assert depth == 1, "the final norm is fused into the single layer's MoE output kernel"
    nd = 2 * N_HEADS
    l = 0
    lw = _layer_weights(l, norm1, w_in, mlstm_gate_bias, mlstm_norm, rg_conv_w, rg_conv_b, rg_wa, rg_ba, rg_wx,
                        rg_bx, rg_lambda, rg_norm, w_out, norm2, router_w, router_bias, exp_w1, exp_w3, exp_w2,
                        shared_w1, shared_w3, shared_w2)
    nf = norm_final.reshape(1, D_MODEL)
    cvecs = jnp.concatenate([c_ctx[None], c, jnp.zeros((SUBLANES - 1 - bs, D_MODEL), F32)], axis=0)
    mod = _ada(cvecs, w_ada[l], b_ada[l]).reshape(SUBLANES, N_MOD, D_MODEL)

    mod_p, mod_s = mod[0:1], mod[1:1 + bs]
    tm_p, tm_s = _tiles(tp, False)['tok'], _tiles(ts, True)['tok']
    rp, (cc, nc_, mc), (hc,) = _mixers(
        x_prompt.reshape(bp * tp, D_MODEL),
        (jnp.zeros((tm_p // GRID_W, D_MODEL // 2), F32), jnp.zeros((GRID_W, D_MODEL // 2), F32)), mod_p,
        jnp.zeros((1, nd, HEAD, HEAD), F32), jnp.zeros((1, nd, HEAD), F32), jnp.zeros((1, nd, HEAD), F32),
        jnp.zeros((1, 2, D_RG), F32), jnp.zeros((N_EXPERTS, LANES), I32), lw, bp, tp, True)
    rs, _, _ = _mixers(
        x_sample.reshape(bs * ts, D_MODEL), _grid_pos_tables(ts, D_MODEL), mod_s,
        state_mlstm_C[:, l].reshape(bs, nd, HEAD, HEAD), state_mlstm_n[:, l].reshape(bs, nd, HEAD),
        jnp.broadcast_to(state_mlstm_m[:, l].reshape(bs, nd, 1), (bs, nd, HEAD)),
        state_rglru_h[:, l], jnp.zeros((N_EXPERTS, LANES), I32), lw, bs, ts, False)
    yp = _moe_out(_routed_experts([rp], lw), 0, rp[1], rp[4], lw['sw13'], lw['sw2'], rp[0], mod_p, tp, tm_p, nf)
    ys = _moe_out(_routed_experts([rs], lw), 0, rs[1], rs[4], lw['sw13'], lw['sw2'], rs[0], mod_s, ts, tm_s, nf)

    y_prompt = yp.reshape(bp, tp, D_MODEL)
    y_sample = ys.reshape(bs, ts, D_MODEL)
    new_c = cc.reshape(bp, 1, 2, N_HEADS, HEAD, HEAD)
    new_n = nc_.reshape(bp, 1, 2, N_HEADS, HEAD)
    new_m = mc[:, :, 0].reshape(bp, 1, 2, N_HEADS)
    new_h = hc.reshape(bp, 1, 2, D_RG)
    return (y_prompt, y_sample, new_c, new_n, new_m, new_h)
```

```python
import functools

import jax
import jax.numpy as jnp
from jax import lax
from jax.experimental import pallas as pl
from jax.experimental.pallas import tpu as pltpu
from jax.experimental.pallas import tpu_sc as plsc

F32 = jnp.float32
BF16 = jnp.bfloat16
I32 = jnp.int32
HIGHEST = lax.Precision.HIGHEST

D_MODEL = 1024
N_MOD = 6
D_ML = 512
N_HEADS = 4
HEAD = 128
D_RG = 512
N_RG_BLOCKS = 8
RG_BLOCK = 64
RG_C = 8.0
N_GATE = 16
N_EXPERTS = 64
N_GROUPS = 8
GROUP = 8
TOPK_GROUPS = 4
TOP_K = 8
D_EXPERT = 256
ROUTED_SCALE = 2.5
EPS = 1e-6
GRID_W = 64
POS_BASE = 10000.0

RG_SEG_PAD = 8
SC_WINDOW = 128
D_PACK = D_MODEL // 2
EXPERT_ROWS = 512
EXPERT_BLOCKS_PER_STEP = 4
EXPERT_RING = 8

SUBLANES = 8
LANES = 128
VMEM_LIMIT = 48 * 1024 * 1024
VMEM_LIMIT_EXPERTS = 56 * 1024 * 1024


def _params(*sem, vmem=VMEM_LIMIT):
    return pltpu.CompilerParams(dimension_semantics=sem, vmem_limit_bytes=vmem)


def _tiles(t, per_sequence_mod):
    cap = t if per_sequence_mod else 1 << 30
    return dict(
        tok=min(512, cap),
        chunk=min(256, t),
        scan=min(512, t),
    )


def _silu(x):
    return x * jax.nn.sigmoid(x)


def _softplus(x):
    return jnp.maximum(x, 0.0) + jnp.log1p(jnp.exp(-jnp.abs(x)))


def _rms(x, g):
    return x * lax.rsqrt(jnp.mean(x * x, axis=-1, keepdims=True) + EPS) * g


def _pack_bf16_pairs(x):
    w = x.shape[1] // 2
    hi = lax.bitcast_convert_type(x[:, :w].astype(BF16).astype(F32), I32)
    lo = lax.bitcast_convert_type(x[:, w:].astype(BF16).astype(F32), I32)
    return hi | lax.shift_right_logical(lo, jnp.full(lo.shape, 16, I32))


def _unpack_bf16_pairs(p):
    hi = lax.bitcast_convert_type(p & jnp.int32(-65536), F32)
    lo = lax.bitcast_convert_type(lax.shift_left(p, jnp.full(p.shape, 16, I32)), F32)
    return hi, lo


def _ada_kernel(c_ref, w_ref, b_ref, o_ref):
    s = _silu(c_ref[...])
    o_ref[...] = jnp.dot(s, w_ref[...], precision=HIGHEST, preferred_element_type=F32) + b_ref[...]


def _ada(cvecs, w_ada, b_ada):
    n_out = w_ada.shape[1]
    tn = 1536
    return pl.pallas_call(
        _ada_kernel,
        grid=(n_out // tn,),
        in_specs=[
            pl.BlockSpec((SUBLANES, D_MODEL), lambda j: (0, 0)),
            pl.BlockSpec((D_MODEL, tn), lambda j: (0, j)),
            pl.BlockSpec((1, tn), lambda j: (0, j)),
        ],
        out_specs=pl.BlockSpec((SUBLANES, tn), lambda j: (0, j)),
        out_shape=jax.ShapeDtypeStruct((SUBLANES, n_out), F32),
        name="ada",
        compiler_params=_params("arbitrary"),
    )(cvecs, w_ada, b_ada.reshape(1, n_out))


def _pos_tile(rt_ref, ct_ref):
    left = jnp.concatenate([jnp.broadcast_to(rt_ref[r:r + 1, :], (GRID_W, rt_ref.shape[1]))
                            for r in range(rt_ref.shape[0])], axis=0)
    right = jnp.concatenate([ct_ref[...]] * rt_ref.shape[0], axis=0)
    return jnp.concatenate([left, right], axis=1)


def _in_proj_kernel(x_ref, rt_ref, ct_ref, mod_ref, n1_ref, wq_ref, wr_ref, wgh_ref, gb_ref,
                    qt_ref, k_ref, kt_ref, v_ref, vt_ref, ot_ref, xr_ref, gr_ref, g_ref, gt_ref):
    x = x_ref[...] + _pos_tile(rt_ref, ct_ref)
    hn = _rms(x, n1_ref[...]) * (1.0 + mod_ref[0, 1:2, :]) + mod_ref[0, 0:1, :]
    hb = hn.astype(BF16)
    z = jnp.dot(hb, wq_ref[...], preferred_element_type=F32)
    k = z[:, D_ML:2 * D_ML] * (HEAD ** -0.5)
    v = z[:, 2 * D_ML:3 * D_ML]
    qt_ref[...] = z[:, 0:D_ML].T.astype(BF16)
    k_ref[...] = k.astype(BF16)
    kt_ref[...] = k.T.astype(BF16)
    v_ref[...] = v.astype(BF16)
    vt_ref[...] = v.T.astype(BF16)
    ot_ref[...] = z[:, 3 * D_ML:4 * D_ML].T
    zr = jnp.dot(hb, wr_ref[...], preferred_element_type=F32)
    xr_ref[...] = zr[:, 0:D_RG]
    gr_ref[...] = zr[:, D_RG:2 * D_RG]
    zg = zr[:, 2 * D_RG:2 * D_RG + LANES]
    h_lo = (hn - hb.astype(F32)).astype(BF16)
    g = (zg + pltpu.roll(zg, LANES - N_GATE, 1)
         + jnp.dot(h_lo, wgh_ref[...], preferred_element_type=F32) + gb_ref[...])
    col = lax.broadcasted_iota(I32, g.shape, 1)
    g = jnp.where((col & 4) != 0, -_softplus(-g), g)
    g_ref[...] = g[:, 0:N_GATE]
    gt_ref[...] = g.T[0:N_GATE, :]


def _mod_index(bm, tm, t):
    if bm > 1:
        return lambda i: ((i * tm) // t, 0, 0)
    return lambda i: (0, 0, 0)


def _pos_specs(rtab, ctab, tm):
    rows = tm // GRID_W
    period = rtab.shape[0] // rows
    return [pl.BlockSpec((rows, D_MODEL // 2), lambda i: (i % period, 0)),
            pl.BlockSpec((GRID_W, D_MODEL // 2), lambda i: (0, 0))]


def _in_proj(x2d, rtab, ctab, mod, t, tm, norm1, wq, wr, wgh, gbias):
    n = x2d.shape[0]
    tok = lambda i: (i, 0)
    tok_t = lambda i: (0, i)
    const = lambda i: (0, 0)
    f = lambda w: jax.ShapeDtypeStruct((n, w), F32)
    row16 = jax.ShapeDtypeStruct((n, D_ML), BF16)
    col16 = jax.ShapeDtypeStruct((D_ML, n), BF16)
    return pl.pallas_call(
        _in_proj_kernel,
        grid=(n // tm,),
        in_specs=[pl.BlockSpec((tm, D_MODEL), tok)] + _pos_specs(rtab, ctab, tm) + [
            pl.BlockSpec((1, N_MOD, D_MODEL), _mod_index(mod.shape[0], tm, t)),
            pl.BlockSpec((1, D_MODEL), const),
            pl.BlockSpec((D_MODEL, 4 * D_ML), const),
            pl.BlockSpec((D_MODEL, 2 * D_RG + LANES), const),
            pl.BlockSpec((D_MODEL, LANES), const),
            pl.BlockSpec((1, LANES), const),
        ],
        out_specs=[pl.BlockSpec((D_ML, tm), tok_t), pl.BlockSpec((tm, D_ML), tok), pl.BlockSpec((D_ML, tm), tok_t),
                   pl.BlockSpec((tm, D_ML), tok), pl.BlockSpec((D_ML, tm), tok_t), pl.BlockSpec((D_ML, tm), tok_t),
                   pl.BlockSpec((tm, D_RG), tok), pl.BlockSpec((tm, D_RG), tok),
                   pl.BlockSpec((tm, N_GATE), tok), pl.BlockSpec((N_GATE, tm), tok_t)],
        out_shape=[col16, row16, col16, row16, col16, jax.ShapeDtypeStruct((D_ML, n), F32),
                   f(D_RG), f(D_RG), f(N_GATE), jax.ShapeDtypeStruct((N_GATE, n), F32)],
        name="in_proj",
        compiler_params=_params("arbitrary"),
    )(x2d, rtab, ctab, mod, norm1, wq, wr, wgh, gbias)


def _mlstm_kernel(chunk, nc, emit_state, *refs):
    (qtf, kf, ktf, vf, vtf, gf, gtf, qtb, kb, ktb, vb, vtb, gb, gtb, c0_ref, n0_ref, m0_ref) = refs[:17]
    if emit_state:
        hf_ref, hb_ref, c_out, n_out, m_out, c_sc, n_sc, m_sc = refs[17:]
    else:
        hf_ref, hb_ref, c_sc, n_sc, m_sc = refs[17:]
    i = pl.program_id(1)

    @pl.when(i == 0)
    def _():
        c_sc[...] = c0_ref[0]
        n_sc[...] = n0_ref[0]
        m_sc[...] = m0_ref[0]

    key = lax.broadcasted_iota(I32, (chunk, chunk), 0)
    qry = lax.broadcasted_iota(I32, (chunk, chunk), 1)
    hd = []
    for d, (qt_ref, k_ref, kt_ref, v_ref, vt_ref, g_ref, gt_ref, h_ref) in enumerate(
            ((qtf, kf, ktf, vf, vtf, gf, gtf, hf_ref), (qtb, kb, ktb, vb, vtb, gb, gtb, hb_ref))):
        tri = (key <= qry) if d == 0 else (key >= qry)
        tri_t = (qry <= key) if d == 0 else (qry >= key)
        g = g_ref[...]
        gt = gt_ref[...]
        brow = jnp.dot(gt, tri.astype(F32), precision=HIGHEST, preferred_element_type=F32)
        bcol = jnp.dot(tri_t.astype(F32), g, precision=HIGHEST, preferred_element_type=F32)
        blast = bcol[chunk - 1:chunk, :] if d == 0 else bcol[0:1, :]
        for h in range(N_HEADS):
            ci = d * 8 + h
            cf = d * 8 + 4 + h
            j = d * N_HEADS + h
            sl = slice(h * HEAD, (h + 1) * HEAD)
            hd.append(dict(
                j=j, sl=sl, tri=tri, h_ref=h_ref, qt=qt_ref[sl, :], k=k_ref[:, sl], kt=kt_ref[sl, :],
                v=v_ref[:, sl], vt=vt_ref[sl, :], b_row=brow[cf:cf + 1, :],
                gate_col=g[:, ci:ci + 1] - bcol[:, cf:cf + 1], gate_row=gt[ci:ci + 1, :] - brow[cf:cf + 1, :],
                b_last=blast[:, cf:cf + 1], m_prev=m_sc[j:j + 1, 0:1], c_prev=c_sc[j], n_prev=n_sc[j:j + 1, :]))
    for x in hd:
        top = jnp.max(jnp.where(x['tri'], x['gate_col'], -jnp.inf), axis=0, keepdims=True)
        mx = jnp.maximum(x['m_prev'], top)
        x['dm'] = jnp.exp(jnp.where(x['tri'], x['gate_col'] - mx, -jnp.inf))
        x['w_inter'] = jnp.exp(x['m_prev'] - mx)
        x['floor'] = jnp.exp(-(x['b_row'] + mx))
    for x in hd:
        x['st'] = jnp.dot(x['k'], x['qt'], preferred_element_type=F32) * x['dm']
    for x in hd:
        inter = lax.dot_general(x['c_prev'].astype(BF16), x['qt'], (((0,), (0,)), ((), ())),
                                preferred_element_type=F32)
        num = x['w_inter'] * inter + jnp.dot(x['vt'], x['st'].astype(BF16), preferred_element_type=F32)
        qn = jnp.dot(jnp.broadcast_to(x['n_prev'], (SUBLANES, HEAD)).astype(BF16), x['qt'],
                     preferred_element_type=F32)[0:1, :]
        den = x['w_inter'] * qn + jnp.sum(x['st'], axis=0, keepdims=True)
        x['h_ref'][x['sl'], :] = num / jnp.maximum(jnp.abs(den), x['floor'])
    for x in hd:
        j = x['j']
        log_w = x['b_last'] + x['gate_row']
        m_new = jnp.maximum(x['b_last'] + x['m_prev'], jnp.max(log_w, axis=1, keepdims=True))
        decay = jnp.exp(x['b_last'] + x['m_prev'] - m_new)
        w_row = jnp.exp(log_w - m_new)
        kwt = (x['kt'].astype(F32) * w_row).astype(BF16)
        c_sc[j] = decay * x['c_prev'] + jnp.dot(kwt, x['v'], preferred_element_type=F32)
        n_sc[j:j + 1, :] = decay * x['n_prev'] + jnp.dot(
            jnp.broadcast_to(w_row, (SUBLANES, chunk)).astype(BF16), x['k'], preferred_element_type=F32)[0:1, :]
        m_sc[j:j + 1, :] = jnp.broadcast_to(m_new, (1, HEAD))

    if emit_state:
        @pl.when(i == nc - 1)
        def _():
            c_out[0] = c_sc[...]
            n_out[0] = n_sc[...]
            m_out[0] = m_sc[...]


def _state_index(bm):
    if bm > 1:
        return lambda b, i: (b,) + (0,) * 3, lambda b, i: (b, 0, 0)
    return lambda b, i: (0,) * 4, lambda b, i: (0, 0, 0)


def _mlstm(qt, k, kt, v, vt, g, gt, c0, n0, m0, b, t, chunk, emit_state):
    nc = t // chunk
    n = b * t
    nd = 2 * N_HEADS
    fwd = lambda bi, i: (bi * nc + i, 0)
    bwd = lambda bi, i: (bi * nc + nc - 1 - i, 0)
    fwd_t = lambda bi, i: (0, bi * nc + i)
    bwd_t = lambda bi, i: (0, bi * nc + nc - 1 - i)
    c_idx, n_idx = _state_index(c0.shape[0])
    rows = lambda m: pl.BlockSpec((chunk, D_ML), m)
    cols = lambda m: pl.BlockSpec((D_ML, chunk), m)
    one_dir = lambda m, mt: [cols(mt), rows(m), cols(mt), rows(m), cols(mt),
                             pl.BlockSpec((chunk, N_GATE), m), pl.BlockSpec((N_GATE, chunk), mt)]
    in_specs = (one_dir(fwd, fwd_t) + one_dir(bwd, bwd_t)
                + [pl.BlockSpec((1, nd, HEAD, HEAD), c_idx),
                   pl.BlockSpec((1, nd, HEAD), n_idx), pl.BlockSpec((1, nd, HEAD), n_idx)])
    out_specs = [cols(fwd_t), cols(bwd_t)]
    out_shape = [jax.ShapeDtypeStruct((D_ML, n), F32)] * 2
    if emit_state:
        out_specs += [pl.BlockSpec((1, nd, HEAD, HEAD), lambda bi, i: (bi, 0, 0, 0)),
                      pl.BlockSpec((1, nd, HEAD), lambda bi, i: (bi, 0, 0)),
                      pl.BlockSpec((1, nd, HEAD), lambda bi, i: (bi, 0, 0))]
        out_shape += [jax.ShapeDtypeStruct((b, nd, HEAD, HEAD), F32),
                      jax.ShapeDtypeStruct((b, nd, HEAD), F32),
                      jax.ShapeDtypeStruct((b, nd, HEAD), F32)]
    return pl.pallas_call(
        functools.partial(_mlstm_kernel, chunk, nc, emit_state),
        grid=(b, nc),
        in_specs=in_specs,
        out_specs=out_specs,
        out_shape=out_shape,
        scratch_shapes=[pltpu.VMEM((nd, HEAD, HEAD), F32), pltpu.VMEM((nd, HEAD), F32),
                        pltpu.VMEM((nd, HEAD), F32)],
        name="mlstm",
        compiler_params=_params("arbitrary", "arbitrary"),
    )(qt, k, kt, v, vt, g, gt, qt, k, kt, v, vt, g, gt, c0, n0, m0)


def _neg_expm1(x):
    u = jnp.exp(x)
    near = jnp.where(u == 1.0, x, (u - 1.0) * x / jnp.log(u))
    return -jnp.where(x < -0.5, u - 1.0, near)


def _rglru_kernel(tb, nb, emit_state, *refs):
    (xf, xf_prev, xf_next, xb, xb_prev, xb_next, h0_ref, w_ref, bias_ref, lam_ref, cw_ref, cb_ref) = refs[:12]
    n_out = 3 if emit_state else 2
    hf_ref, hb_ref = refs[12:14]
    hfin_ref = refs[14] if emit_state else None
    carry, af_sc, uf_sc, ab_sc, ub_sc, hf_sc, pf_sc, hb_sc, pb_sc = refs[12 + n_out:]
    i = pl.program_id(1)

    @pl.when(i == 0)
    def _():
        carry[...] = h0_ref[0]

    row = lax.broadcasted_iota(I32, (tb, D_RG), 0)
    cw = cw_ref[...]
    softplus_neg_lam = _softplus(-lam_ref[...])

    def conv(main_ref, prev_ref, next_ref, first, last):
        main = main_ref[...]
        prev = jnp.where(first, 0.0, prev_ref[...])
        nxt = jnp.where(last, 0.0, next_ref[...])
        xm2 = jnp.where(row == 0, prev[6:7, :], jnp.where(row == 1, prev[7:8, :], pltpu.roll(main, 2, 0)))
        xm1 = jnp.where(row == 0, prev[7:8, :], pltpu.roll(main, 1, 0))
        xp1 = jnp.where(row == tb - 1, nxt[0:1, :], pltpu.roll(main, tb - 1, 0))
        return cb_ref[...] + xm2 * cw[0:1, :] + xm1 * cw[1:2, :] + main * cw[2:3, :] + xp1 * cw[3:4, :]

    def recurrence_terms(xc, d):
        z = jnp.dot(xc.astype(BF16), w_ref[:, d * 2 * D_RG:(d + 1) * 2 * D_RG],
                    preferred_element_type=F32) + bias_ref[:, d * 2 * D_RG:(d + 1) * 2 * D_RG]
        r = jax.nn.sigmoid(z[:, 0:D_RG])
        ig = jax.nn.sigmoid(z[:, D_RG:2 * D_RG])
        log_a = -RG_C * r * softplus_neg_lam[d:d + 1, :]
        a = jnp.exp(log_a)
        u = jnp.sqrt(_neg_expm1(2.0 * log_a)) * (ig * xc)
        return a, u

    a_f, u_f = recurrence_terms(conv(xf, xf_prev, xf_next, i == 0, i == nb - 1), 0)
    a_b, u_b = recurrence_terms(conv(xb, xb_prev, xb_next, i == nb - 1, i == 0), 1)
    seg = tb // SUBLANES
    pitch = seg + RG_SEG_PAD
    ncol = D_RG // LANES
    for lc in range(ncol):
        lanes = slice(lc * LANES, (lc + 1) * LANES)
        for s in range(SUBLANES):
            src = slice(s * seg, (s + 1) * seg)
            dst = slice(s * pitch, s * pitch + seg)
            af_sc[lc, dst, :], uf_sc[lc, dst, :] = a_f[src, lanes], u_f[src, lanes]
            ab_sc[lc, dst, :], ub_sc[lc, dst, :] = a_b[src, lanes], u_b[src, lanes]
    slab = lambda k: (slice(None), pl.ds(k, SUBLANES, stride=pitch), slice(None))
    hf = jnp.zeros((ncol, SUBLANES, LANES), F32)
    hb = jnp.zeros((ncol, SUBLANES, LANES), F32)
    pf = jnp.ones((ncol, SUBLANES, LANES), F32)
    pb = jnp.ones((ncol, SUBLANES, LANES), F32)
    for k in range(seg):
        kb = seg - 1 - k
        ak = af_sc[slab(k)]
        hf = ak * hf + uf_sc[slab(k)]
        pf = pf * ak
        hf_sc[slab(k)] = hf
        pf_sc[slab(k)] = pf
        ak = ab_sc[slab(kb)]
        hb = ak * hb + ub_sc[slab(kb)]
        pb = pb * ak
        hb_sc[slab(kb)] = hb
        pb_sc[slab(kb)] = pb
    for lc in range(ncol):
        lanes = slice(lc * LANES, (lc + 1) * LANES)
        c = carry[0:1, lanes]
        cin_f = []
        for s in range(SUBLANES):
            cin_f.append(c)
            c = pf[lc, s:s + 1, :] * c + hf[lc, s:s + 1, :]
        carry[0:1, lanes] = c
        c = carry[1:2, lanes]
        cin_b = [None] * SUBLANES
        for s in reversed(range(SUBLANES)):
            cin_b[s] = c
            c = pb[lc, s:s + 1, :] * c + hb[lc, s:s + 1, :]
        carry[1:2, lanes] = c
        for s in range(SUBLANES):
            rows = slice(s * seg, (s + 1) * seg)
            src = slice(s * pitch, s * pitch + seg)
            hf_ref[rows, lanes] = hf_sc[lc, src, :] + pf_sc[lc, src, :] * cin_f[s]
            hb_ref[rows, lanes] = hb_sc[lc, src, :] + pb_sc[lc, src, :] * cin_b[s]

    if emit_state:
        @pl.when(i == nb - 1)
        def _():
            hfin_ref[0] = carry[...]


def _rglru(xr, h0, wbd, bias, lam, cw, cb, b, t, tb, emit_state):
    nb = t // tb
    n = b * t
    r8 = tb // SUBLANES
    last8 = n // SUBLANES - 1
    fwd = lambda bi, i: (bi * nb + i, 0)
    bwd = lambda bi, i: (bi * nb + nb - 1 - i, 0)
    fwd_prev = lambda bi, i: (jnp.maximum((bi * nb + i) * r8 - 1, 0), 0)
    fwd_next = lambda bi, i: (jnp.minimum((bi * nb + i + 1) * r8, last8), 0)
    bwd_prev = lambda bi, i: (jnp.maximum((bi * nb + nb - 1 - i) * r8 - 1, 0), 0)
    bwd_next = lambda bi, i: (jnp.minimum((bi * nb + nb - i) * r8, last8), 0)
    const = lambda bi, i: (0, 0)
    h_idx = (lambda bi, i: (bi, 0, 0)) if h0.shape[0] > 1 else (lambda bi, i: (0, 0, 0))
    halo = lambda m: pl.BlockSpec((SUBLANES, D_RG), m)
    in_specs = [pl.BlockSpec((tb, D_RG), fwd), halo(fwd_prev), halo(fwd_next),
                pl.BlockSpec((tb, D_RG), bwd), halo(bwd_prev), halo(bwd_next),
                pl.BlockSpec((1, 2, D_RG), h_idx),
                pl.BlockSpec((D_RG, 4 * D_RG), const), pl.BlockSpec((1, 4 * D_RG), const),
                pl.BlockSpec((2, D_RG), const), pl.BlockSpec((4, D_RG), const), pl.BlockSpec((1, D_RG), const)]
    out_specs = [pl.BlockSpec((tb, D_RG), fwd), pl.BlockSpec((tb, D_RG), bwd)]
    out_shape = [jax.ShapeDtypeStruct((n, D_RG), F32)] * 2
    if emit_state:
        out_specs.append(pl.BlockSpec((1, 2, D_RG), lambda bi, i: (bi, 0, 0)))
        out_shape.append(jax.ShapeDtypeStruct((b, 2, D_RG), F32))
    return pl.pallas_call(
        functools.partial(_rglru_kernel, tb, nb, emit_state),
        grid=(b, nb),
        in_specs=in_specs,
        out_specs=out_specs,
        out_shape=out_shape,
        scratch_shapes=[pltpu.VMEM((2, D_RG), F32)]
        + [pltpu.VMEM((D_RG // LANES, tb + SUBLANES * RG_SEG_PAD, LANES), F32)] * 8,
        name="rglru",
        compiler_params=_params("arbitrary", "arbitrary"),
    )(xr, xr, xr, xr, xr, xr, h0, wbd, bias, lam, cw, cb)


def _route(s, sb):
    tm = s.shape[1]
    neg = -jnp.inf
    sub = lax.broadcasted_iota(I32, (GROUP, tm), 0)
    blocks = [sb[gi * GROUP:(gi + 1) * GROUP, :] for gi in range(N_GROUPS)]
    gscore = []
    for blk in blocks:
        m1 = jnp.max(blk, axis=0, keepdims=True)
        first = jnp.min(jnp.where(blk == m1, sub, GROUP), axis=0, keepdims=True)
        m2 = jnp.max(jnp.where(sub == first, neg, blk), axis=0, keepdims=True)
        gscore.append(m1 + m2)
    masked = []
    for gi in range(N_GROUPS):
        rank = jnp.zeros((1, tm), F32)
        for gj in range(N_GROUPS):
            if gj == gi:
                continue
            ahead = (gscore[gj] >= gscore[gi]) if gj < gi else (gscore[gj] > gscore[gi])
            rank = rank + jnp.where(ahead, 1.0, 0.0)
        masked.append(jnp.where(rank < TOPK_GROUPS, blocks[gi], neg))
    v = jnp.concatenate(masked, axis=0)
    eid = lax.broadcasted_iota(I32, (N_EXPERTS, tm), 0)
    sel = jnp.zeros((N_EXPERTS, tm), F32)
    picks = []
    for _ in range(TOP_K):
        mx = jnp.max(v, axis=0, keepdims=True)
        idx = jnp.min(jnp.where(v == mx, eid, N_EXPERTS), axis=0, keepdims=True)
        pick = eid == idx
        picks.append(pick)
        sel = jnp.where(pick, 1.0, sel)
        v = jnp.where(pick, neg, v)
    ws = s * sel
    return ws / jnp.sum(ws, axis=0, keepdims=True) * ROUTED_SCALE, sel, picks


def _mix_out_kernel(hmf_ref, hmb_ref, ot_ref, hrf_ref, hrb_ref, gr_ref, x_ref, rt_ref, ct_ref, mod_ref,
                    mln_ref, rgn_ref, wo_ml_ref, wo_rg_ref, n2_ref, rwt_ref, rb_ref, cnt0_ref,
                    x1_ref, hn2p_ref, ek_ref, pk_ref, wtok_ref, cnt_ref, cnt_sc):
    i = pl.program_id(0)
    tm = x_ref.shape[0]

    @pl.when(i == 0)
    def _():
        cnt_sc[...] = cnt0_ref[...].astype(F32)

    hm = hmf_ref[...] + hmb_ref[...]
    heads = []
    for h in range(N_HEADS):
        seg = hm[h * HEAD:(h + 1) * HEAD, :]
        heads.append(seg * lax.rsqrt(jnp.mean(seg * seg, axis=0, keepdims=True) + EPS))
    y_ml_t = jnp.concatenate(heads, axis=0) * mln_ref[...] * jax.nn.sigmoid(ot_ref[...])
    y_rg = _rms(hrf_ref[...] + hrb_ref[...], rgn_ref[...]) * jax.nn.gelu(gr_ref[...])
    mix = (lax.dot_general(y_ml_t.astype(BF16), wo_ml_ref[...], (((0,), (0,)), ((), ())),
                           preferred_element_type=F32)
           + jnp.dot(y_rg.astype(BF16), wo_rg_ref[...], preferred_element_type=F32))
    x1 = x_ref[...] + _pos_tile(rt_ref, ct_ref) + mod_ref[0, 2:3, :] * mix
    x1_ref[...] = x1
    hn2 = _rms(x1, n2_ref[...]) * (1.0 + mod_ref[0, 4:5, :]) + mod_ref[0, 3:4, :]
    hn2p_ref[...] = _pack_bf16_pairs(hn2)
    logits_t = lax.dot_general(rwt_ref[...], hn2, (((1,), (1,)), ((), ())), precision=HIGHEST,
                               preferred_element_type=F32)
    s = jax.nn.sigmoid(logits_t)
    wt, sel, picks = _route(s, s + rb_ref[...])

    earlier = (lax.broadcasted_iota(I32, (tm, tm), 0) < lax.broadcasted_iota(I32, (tm, tm), 1))
    prefix = jnp.dot(sel.astype(BF16), earlier.astype(BF16), preferred_element_type=F32)
    pos_all = cnt_sc[:, 0:1] + prefix
    eid = lax.broadcasted_iota(I32, (N_EXPERTS, tm), 0)
    eid_f = eid.astype(F32)
    row8 = lax.broadcasted_iota(I32, (TOP_K, tm), 0)
    ek = jnp.zeros((TOP_K, tm), F32)
    pk = jnp.zeros((TOP_K, tm), F32)
    wk = jnp.zeros((N_EXPERTS, tm), F32)
    for k, pick in enumerate(picks):
        take = lambda a: jnp.sum(jnp.where(pick, a, 0.0), axis=0, keepdims=True)
        ek = jnp.where(row8 == k, take(eid_f), ek)
        pk = jnp.where(row8 == k, take(pos_all), pk)
        wk = jnp.where(eid == k, take(wt), wk)
    ek_ref[...] = ek.astype(I32)
    pk_ref[...] = pk.astype(I32)
    wtok_ref[...] = wk.T
    cnt_sc[...] += jnp.broadcast_to(jnp.sum(sel, axis=1, keepdims=True), cnt_sc.shape)

    @pl.when(i == pl.num_programs(0) - 1)
    def _():
        cnt_ref[...] = cnt_sc[...].astype(I32)


def _mix_out(hmf, hmb, ot, hrf, hrb, gr, x2d, rtab, ctab, mod, t, tm, mln, rgn, wo_ml, wo_rg, norm2, rwt, rbias,
             cnt0):
    n = x2d.shape[0]
    tok = lambda i: (i, 0)
    tok_t = lambda i: (0, i)
    const = lambda i: (0, 0)
    return pl.pallas_call(
        _mix_out_kernel,
        grid=(n // tm,),
        in_specs=[pl.BlockSpec((D_ML, tm), tok_t)] * 3 + [pl.BlockSpec((tm, D_RG), tok)] * 3 + [
            pl.BlockSpec((tm, D_MODEL), tok)] + _pos_specs(rtab, ctab, tm) + [
            pl.BlockSpec((1, N_MOD, D_MODEL), _mod_index(mod.shape[0], tm, t)),
            pl.BlockSpec((D_ML, 1), const), pl.BlockSpec((1, D_RG), const),
            pl.BlockSpec((D_ML, D_MODEL), const), pl.BlockSpec((D_RG, D_MODEL), const),
            pl.BlockSpec((1, D_MODEL), const),
            pl.BlockSpec((N_EXPERTS, D_MODEL), const), pl.BlockSpec((N_EXPERTS, 1), const),
            pl.BlockSpec((N_EXPERTS, LANES), const),
        ],
        out_specs=[pl.BlockSpec((tm, D_MODEL), tok), pl.BlockSpec((tm, D_PACK), tok),
                   pl.BlockSpec((TOP_K, tm), tok_t), pl.BlockSpec((TOP_K, tm), tok_t),
                   pl.BlockSpec((tm, N_EXPERTS), tok), pl.BlockSpec((N_EXPERTS, LANES), const)],
        out_shape=[jax.ShapeDtypeStruct((n, D_MODEL), F32), jax.ShapeDtypeStruct((n, D_PACK), I32),
                   jax.ShapeDtypeStruct((TOP_K, n), I32), jax.ShapeDtypeStruct((TOP_K, n), I32),
                   jax.ShapeDtypeStruct((n, N_EXPERTS), F32), jax.ShapeDtypeStruct((N_EXPERTS, LANES), I32)],
        scratch_shapes=[pltpu.VMEM((N_EXPERTS, LANES), F32)],
        name="mix_out",
        compiler_params=_params("arbitrary"),
    )(hmf, hmb, ot, hrf, hrb, gr, x2d, rtab, ctab, mod, mln, rgn, wo_ml, wo_rg, norm2, rwt, rbias, cnt0)


def _sc_mesh():
    return plsc.VectorSubcoreMesh(core_axis_name="core", subcore_axis_name="subcore")


def _sc_worker():
    info = plsc.get_sparse_core_info()
    return lax.axis_index("subcore") * info.num_cores + lax.axis_index("core"), info.num_cores * info.num_subcores


def _sc_dispatch(xp, dest3, n_slots):
    n, w = xp.shape
    nwin = n // SC_WINDOW

    @pl.kernel(out_type=jax.ShapeDtypeStruct((n_slots, w), xp.dtype), mesh=_sc_mesh(),
               scratch_types=[pltpu.VMEM((SC_WINDOW, w), xp.dtype), pltpu.VMEM((TOP_K, SC_WINDOW), I32)],
               name="sc_dispatch")
    def k(x_hbm, i_hbm, o_hbm, x_v, i_v):
        wid, nworkers = _sc_worker()
        per = nwin // nworkers

        @pl.loop(0, per)
        def _(s):
            win = wid * per + s
            pltpu.sync_copy(x_hbm.at[pl.ds(win * SC_WINDOW, SC_WINDOW)], x_v)
            pltpu.sync_copy(i_hbm.at[win], i_v)
            for j in range(TOP_K):
                pltpu.sync_copy(x_v, o_hbm.at[i_v.at[j]])

    return k(xp, dest3)


def _sc_combine_gather(ys, dest3):
    nwin = dest3.shape[0]
    w = ys.shape[1]

    @pl.kernel(out_type=jax.ShapeDtypeStruct((nwin, TOP_K, SC_WINDOW, w), ys.dtype), mesh=_sc_mesh(),
               scratch_types=[pltpu.VMEM((SC_WINDOW, w), ys.dtype), pltpu.VMEM((TOP_K, SC_WINDOW), I32)],
               name="sc_combine")
    def k(y_hbm, i_hbm, o_hbm, y_v, i_v):
        wid, nworkers = _sc_worker()
        per = nwin // nworkers

        @pl.loop(0, per)
        def _(s):
            win = wid * per + s
            pltpu.sync_copy(i_hbm.at[win], i_v)
            for j in range(TOP_K):
                pltpu.sync_copy(y_hbm.at[i_v.at[j]], y_v)
                pltpu.sync_copy(y_v, o_hbm.at[win, j])

    return k(ys, dest3)


def _swiglu(x, w13):
    h = jnp.dot(x, w13, preferred_element_type=F32)
    return _silu(h[:, 0:D_EXPERT]) * h[:, D_EXPERT:2 * D_EXPERT]


def _unpack_rows_bf16(p):
    hi, lo = _unpack_bf16_pairs(p)
    return jnp.concatenate([hi.astype(BF16), lo.astype(BF16)], axis=1)


def _expert_kernel(rows, rank_ref, expert_ref, nu_ref, nr_ref, x_ref, w1_hbm, w3_hbm, w2_hbm, y_ref,
                   w1buf, w3buf, w2buf, sem, progress):
    b = pl.program_id(0)
    g = EXPERT_BLOCKS_PER_STEP
    n_ranks = nr_ref[0]

    @pl.when(b == 0)
    def _():
        progress[0] = 0
        progress[1] = 0

    def copies(r):
        slot = lax.rem(r, EXPERT_RING)
        e = expert_ref[r]
        return (pltpu.make_async_copy(w1_hbm.at[e], w1buf.at[slot], sem.at[slot, 0]),
                pltpu.make_async_copy(w3_hbm.at[e], w3buf.at[slot], sem.at[slot, 1]),
                pltpu.make_async_copy(w2_hbm.at[e], w2buf.at[slot], sem.at[slot, 2]))

    def start(r, carry):
        for cp in copies(r):
            cp.start()
        return carry

    def wait(r, carry):
        for cp in copies(r):
            cp.wait()
        return carry

    first = rank_ref[b * g]
    last = rank_ref[b * g + g - 1]
    started = jnp.minimum(first + EXPERT_RING, n_ranks)
    lax.fori_loop(progress[0], started, start, 0)
    progress[0] = jnp.maximum(progress[0], started)
    needed = jnp.where(b == pl.num_programs(0) - 1, progress[0], jnp.minimum(last + 1, n_ranks))
    lax.fori_loop(progress[1], needed, wait, 0)
    progress[1] = jnp.maximum(progress[1], needed)

    @pl.when(b * g < nu_ref[0])
    def _():
        for j in range(g):
            slot = lax.rem(rank_ref[b * g + j], EXPERT_RING)
            sl = slice(j * rows, (j + 1) * rows)
            x = _unpack_rows_bf16(x_ref[sl, :])
            h = (_silu(jnp.dot(x, w1buf[slot].astype(BF16), preferred_element_type=F32))
                 * jnp.dot(x, w3buf[slot].astype(BF16), preferred_element_type=F32))
            y_ref[sl, :] = _pack_bf16_pairs(
                jnp.dot(h.astype(BF16), w2buf[slot].astype(BF16), preferred_element_type=F32))


def _experts(xs, block_rank, rank_expert, n_used, n_ranks, w1, w3, w2, rows):
    g = EXPERT_BLOCKS_PER_STEP
    nb = xs.shape[0] // rows
    tok = lambda b, *_: (b, 0)
    return pl.pallas_call(
        functools.partial(_expert_kernel, rows),
        grid_spec=pltpu.PrefetchScalarGridSpec(
            num_scalar_prefetch=4,
            grid=(nb // g,),
            in_specs=[pl.BlockSpec((g * rows, D_PACK), tok)] + [pl.BlockSpec(memory_space=pl.ANY)] * 3,
            out_specs=pl.BlockSpec((g * rows, D_PACK), tok),
            scratch_shapes=[pltpu.VMEM((EXPERT_RING, D_MODEL, D_EXPERT), F32),
                            pltpu.VMEM((EXPERT_RING, D_MODEL, D_EXPERT), F32),
                            pltpu.VMEM((EXPERT_RING, D_EXPERT, D_MODEL), F32),
                            pltpu.SemaphoreType.DMA((EXPERT_RING, 3)), pltpu.SMEM((2,), I32)],
        ),
        out_shape=jax.ShapeDtypeStruct(xs.shape, I32),
        name="experts",
        compiler_params=_params("arbitrary", vmem=VMEM_LIMIT_EXPERTS),
    )(block_rank, rank_expert, n_used, n_ranks, xs, w1, w3, w2)


def _moe_out_kernel(yk_ref, hn2p_ref, wtok_ref, sw13_ref, sw2_ref, x1_ref, mod_ref, nf_ref, y_ref):
    shared = jnp.dot(_swiglu(_unpack_rows_bf16(hn2p_ref[...]), sw13_ref[...]).astype(BF16), sw2_ref[...],
                     preferred_element_type=F32)
    w = wtok_ref[...]
    parts = []
    for wi in range(yk_ref.shape[0]):
        rows = slice(wi * SC_WINDOW, (wi + 1) * SC_WINDOW)
        a_hi = shared[rows, 0:D_PACK]
        a_lo = shared[rows, D_PACK:D_MODEL]
        for k in range(TOP_K):
            y_hi, y_lo = _unpack_bf16_pairs(yk_ref[wi, k])
            wc = w[rows, k:k + 1]
            a_hi = a_hi + wc * y_hi
            a_lo = a_lo + wc * y_lo
        parts.append(jnp.concatenate([a_hi, a_lo], axis=1))
    x2 = x1_ref[...] + mod_ref[0, 5:6, :] * jnp.concatenate(parts, axis=0)
    y_ref[...] = _rms(x2, nf_ref[...])


def _moe_out(yk, tok0, hn2p, wtok, sw13, sw2, x1, mod, t, tm, norm_final):
    n = hn2p.shape[0]
    tok = lambda i: (i, 0)
    const = lambda i: (0, 0)
    blk0 = tok0 // tm
    return pl.pallas_call(
        _moe_out_kernel,
        grid=(n // tm,),
        in_specs=[
            pl.BlockSpec((tm // SC_WINDOW, TOP_K, SC_WINDOW, D_PACK), lambda i: (i + blk0, 0, 0, 0)),
            pl.BlockSpec((tm, D_PACK), tok),
            pl.BlockSpec((tm, N_EXPERTS), tok),
            pl.BlockSpec((D_MODEL, 2 * D_EXPERT), const),
            pl.BlockSpec((D_EXPERT, D_MODEL), const),
            pl.BlockSpec((tm, D_MODEL), tok),
            pl.BlockSpec((1, N_MOD, D_MODEL), _mod_index(mod.shape[0], tm, t)),
            pl.BlockSpec((1, D_MODEL), const),
        ],
        out_specs=pl.BlockSpec((tm, D_MODEL), tok),
        out_shape=jax.ShapeDtypeStruct((n, D_MODEL), F32),
        name="moe_out",
        compiler_params=_params("arbitrary"),
    )(yk, hn2p, wtok, sw13, sw2, x1, mod, norm_final)


def _dispatch_plan(cnt, ek, pk, rows):
    n = ek.shape[1]
    nb = n * TOP_K // rows + N_EXPERTS
    nblk = (cnt + rows - 1) // rows
    block_end = jnp.cumsum(nblk)
    experts = jnp.arange(N_EXPERTS, dtype=I32)
    first_row = jnp.sum(jnp.where(ek[:, :, None] == experts, (block_end - nblk) * rows, 0), axis=-1)
    dest3 = (first_row + pk).reshape(TOP_K, n // SC_WINDOW, SC_WINDOW).transpose(1, 0, 2)
    owns = nblk > 0
    n_ranks = jnp.sum(owns.astype(I32))
    blocks = jnp.arange(nb, dtype=I32)[:, None]
    block_rank = jnp.minimum(jnp.sum((owns & (block_end <= blocks)).astype(I32), axis=1), n_ranks - 1)
    rank_expert = jnp.minimum(jnp.sum((jnp.cumsum(owns.astype(I32)) <= experts[:, None]).astype(I32), axis=1),
                              N_EXPERTS - 1)
    return (dest3, block_rank.astype(I32), rank_expert.astype(I32), block_end[-1:].astype(I32),
            n_ranks.reshape(1).astype(I32), nb * rows)


def _grid_pos_tables(n_tokens, dim):
    quarter = dim // 4
    omega = 1.0 / (POS_BASE ** (jnp.arange(quarter, dtype=F32) / quarter))
    ra = jnp.arange(n_tokens // GRID_W).astype(F32)[:, None] * omega
    ca = jnp.arange(GRID_W).astype(F32)[:, None] * omega
    return (jnp.concatenate([jnp.sin(ra), jnp.cos(ra)], axis=-1),
            jnp.concatenate([jnp.sin(ca), jnp.cos(ca)], axis=-1))


def _block_diag(w):
    eye = jnp.eye(N_RG_BLOCKS, dtype=w.dtype)
    return jnp.einsum('nij,nm->nimj', w, eye).reshape(D_RG, D_RG)


def _layer_weights(l, norm1, w_in, mlstm_gate_bias, mlstm_norm, rg_conv_w, rg_conv_b, rg_wa, rg_ba, rg_wx,
                   rg_bx, rg_lambda, rg_norm, w_out, norm2, router_w, router_bias, exp_w1, exp_w3, exp_w2,
                   shared_w1, shared_w3, shared_w2):
    wi = w_in[l]
    c0, c1 = 4 * D_ML, 4 * D_ML + N_GATE
    wg = wi[:, c0:c1]
    wg_hi = wg.astype(BF16)
    wg_lo = (wg - wg_hi.astype(F32)).astype(BF16)
    zcols = lambda w: jnp.zeros((D_MODEL, w), BF16)
    return dict(
        norm1=norm1[l].reshape(1, D_MODEL),
        wq=wi[:, :c0].astype(BF16),
        wr=jnp.concatenate([wi[:, c1:].astype(BF16), wg_hi, wg_lo, zcols(LANES - 2 * N_GATE)], axis=1),
        wgh=jnp.concatenate([wg_hi, zcols(LANES - N_GATE)], axis=1),
        gbias=jnp.pad(mlstm_gate_bias[l].reshape(1, N_GATE), ((0, 0), (0, LANES - N_GATE))),
        mln=mlstm_norm[l].reshape(D_ML, 1),
        cw=rg_conv_w[l], cb=rg_conv_b[l].reshape(1, D_RG),
        wbd=jnp.concatenate([_block_diag(rg_wa[l, 0]), _block_diag(rg_wx[l, 0]),
                             _block_diag(rg_wa[l, 1]), _block_diag(rg_wx[l, 1])], axis=1).astype(BF16),
        rbias=jnp.concatenate([rg_ba[l, 0], rg_bx[l, 0], rg_ba[l, 1], rg_bx[l, 1]]).reshape(1, 4 * D_RG),
        lam=rg_lambda[l], rgn=rg_norm[l].reshape(1, D_RG),
        wo_ml=w_out[l, :D_ML].astype(BF16), wo_rg=w_out[l, D_ML:].astype(BF16),
        norm2=norm2[l].reshape(1, D_MODEL),
        rwt=router_w[l].T, rb=router_bias[l].reshape(N_EXPERTS, 1),
        w1=exp_w1[l], w3=exp_w3[l], w2=exp_w2[l],
        sw13=jnp.concatenate([shared_w1[l], shared_w3[l]], axis=-1).astype(BF16),
        sw2=shared_w2[l].astype(BF16),
    )


def _mixers(x2d, pos_tables, mod, c0, n0, m0, h0, cnt0, lw, b, t, emit_state):
    tl = _tiles(t, mod.shape[0] > 1)
    tm = tl['tok']
    rtab, ctab = pos_tables
    qt, k, kt, v, vt, ot, xr, gr, g, gt = _in_proj(x2d, rtab, ctab, mod, t, tm, lw['norm1'], lw['wq'], lw['wr'],
                                                   lw['wgh'], lw['gbias'])
    ml = _mlstm(qt, k, kt, v, vt, g, gt, c0, n0, m0, b, t, tl['chunk'], emit_state)
    rg = _rglru(xr, h0, lw['wbd'], lw['rbias'], lw['lam'], lw['cw'], lw['cb'], b, t, tl['scan'], emit_state)
    routed = _mix_out(ml[0], ml[1], ot, rg[0], rg[1], gr, x2d, rtab, ctab, mod, t, tm, lw['mln'], lw['rgn'],
                      lw['wo_ml'], lw['wo_rg'], lw['norm2'], lw['rwt'], lw['rb'], cnt0)
    return routed, ml[2:], rg[2:]


def _routed_experts(paths, lw):
    cnt = paths[-1][5][:, 0]
    hn2p = jnp.concatenate([p[1] for p in paths], axis=0)
    ek = jnp.concatenate([p[2] for p in paths], axis=1)
    pk = jnp.concatenate([p[3] for p in paths], axis=1)
    dest3, block_rank, rank_expert, n_used, n_ranks, n_slots = _dispatch_plan(cnt, ek, pk, EXPERT_ROWS)
    xs = _sc_dispatch(hn2p, dest3, n_slots)
    ys = _experts(xs, block_rank, rank_expert, n_used, n_ranks, lw['w1'], lw['w3'], lw['w2'], EXPERT_ROWS)
    return _sc_combine_gather(ys, dest3)


def kernel(x_prompt, x_sample, c, state_mlstm_C, state_mlstm_n, state_mlstm_m, state_rglru_h, c_ctx, w_ada, b_ada, norm1, w_in, mlstm_gate_bias, mlstm_norm, rg_conv_w, rg_conv_b, rg_wa, rg_ba, rg_wx, rg_bx, rg_lambda, rg_norm, w_out, norm2, router_w, router_bias, exp_w1, exp_w3, exp_w2, shared_w1, shared_w3, shared_w2, norm_final):
    bp, tp, _ = x_prompt.shape
    bs, ts, _ = x_sample.shape
    depth = w_ada.shape[0]
    assert depth == 1, "the final norm is fused into the single layer's MoE output kernel"
    nd = 2 * N_HEADS
    l = 0
    lw = _layer_weights(l, norm1, w_in, mlstm_gate_bias, mlstm_norm, rg_conv_w, rg_conv_b, rg_wa, rg_ba, rg_wx,
                        rg_bx, rg_lambda, rg_norm, w_out, norm2, router_w, router_bias, exp_w1, exp_w3, exp_w2,
                        shared_w1, shared_w3, shared_w2)
    nf = norm_final.reshape(1, D_MODEL)
    cvecs = jnp.concatenate([c_ctx[None], c, jnp.zeros((SUBLANES - 1 - bs, D_MODEL), F32)], axis=0)
    mod = _ada(cvecs, w_ada[l], b_ada[l]).reshape(SUBLANES, N_MOD, D_MODEL)

    mod_p, mod_s = mod[0:1], mod[1:1 + bs]
    tm_p, tm_s = _tiles(tp, False)['tok'], _tiles(ts, True)['tok']
    rp, (cc, nc_, mc), (hc,) = _mixers(
        x_prompt.reshape(bp * tp, D_MODEL),
        (jnp.zeros((tm_p // GRID_W, D_MODEL // 2), F32), jnp.zeros((GRID_W, D_MODEL // 2), F32)), mod_p,
        jnp.zeros((1, nd, HEAD, HEAD), F32), jnp.zeros((1, nd, HEAD), F32), jnp.zeros((1, nd, HEAD), F32),
        jnp.zeros((1, 2, D_RG), F32), jnp.zeros((N_EXPERTS, LANES), I32), lw, bp, tp, True)
    rs, _, _ = _mixers(
        x_sample.reshape(bs * ts, D_MODEL), _grid_pos_tables(ts, D_MODEL), mod_s,
        state_mlstm_C[:, l].reshape(bs, nd, HEAD, HEAD), state_mlstm_n[:, l].reshape(bs, nd, HEAD),
        jnp.broadcast_to(state_mlstm_m[:, l].reshape(bs, nd, 1), (bs, nd, HEAD)),
        state_rglru_h[:, l], jnp.zeros((N_EXPERTS, LANES), I32), lw, bs, ts, False)
    yp = _moe_out(_routed_experts([rp], lw), 0, rp[1], rp[4], lw['sw13'], lw['sw2'], rp[0], mod_p, tp, tm_p, nf)
    ys = _moe_out(_routed_experts([rs], lw), 0, rs[1], rs[4], lw['sw13'], lw['sw2'], rs[0], mod_s, ts, tm_s, nf)

    y_prompt = yp.reshape(bp, tp, D_MODEL)
    y_sample = ys.reshape(bs, ts, D_MODEL)
    new_c = cc.reshape(bp, 1, 2, N_HEADS, HEAD, HEAD)
    new_n = nc_.reshape(bp, 1, 2, N_HEADS, HEAD)
    new_m = mc[:, :, 0].reshape(bp, 1, 2, N_HEADS)
    new_h = hc.reshape(bp, 1, 2, D_RG)
    return (y_prompt, y_sample, new_c, new_n, new_m, new_h)
```

```python
import functools

import jax
import jax.numpy as jnp
from jax import lax
from jax.experimental import pallas as pl
from jax.experimental.pallas import tpu as pltpu
from jax.experimental.pallas import tpu_sc as plsc

F32 = jnp.float32
BF16 = jnp.bfloat16
I32 = jnp.int32
HIGHEST = lax.Precision.HIGHEST

D_MODEL = 1024
N_MOD = 6
D_ML = 512
N_HEADS = 4
HEAD = 128
D_RG = 512
N_RG_BLOCKS = 8
RG_BLOCK = 64
RG_C = 8.0
N_GATE = 16
N_EXPERTS = 64
N_GROUPS = 8
GROUP = 8
TOPK_GROUPS = 4
TOP_K = 8
D_EXPERT = 256
ROUTED_SCALE = 2.5
EPS = 1e-6
GRID_W = 64
POS_BASE = 10000.0

RG_SEG_PAD = 8
SC_WINDOW = 128
D_PACK = D_MODEL // 2
EXPERT_ROWS = 512
EXPERT_BLOCKS_PER_STEP = 4
EXPERT_RING = 8

SUBLANES = 8
LANES = 128
VMEM_LIMIT = 48 * 1024 * 1024
VMEM_LIMIT_EXPERTS = 56 * 1024 * 1024


def _params(*sem, vmem=VMEM_LIMIT):
    return pltpu.CompilerParams(dimension_semantics=sem, vmem_limit_bytes=vmem)


def _tiles(t, per_sequence_mod):
    cap = t if per_sequence_mod else 1 << 30
    return dict(
        tok=min(512, cap),
        chunk=min(256, t),
        scan=min(512, t),
    )


def _silu(x):
    return x * jax.nn.sigmoid(x)


def _softplus(x):
    return jnp.maximum(x, 0.0) + jnp.log1p(jnp.exp(-jnp.abs(x)))


def _rms(x, g):
    return x * lax.rsqrt(jnp.mean(x * x, axis=-1, keepdims=True) + EPS) * g


def _bf16_pieces(x):
    hi = x.astype(BF16)
    r = x - hi.astype(F32)
    mid = r.astype(BF16)
    return hi, mid, (r - mid.astype(F32)).astype(BF16)


def _pack_bf16_pairs(x):
    w = x.shape[1] // 2
    hi = lax.bitcast_convert_type(x[:, :w].astype(BF16).astype(F32), I32)
    lo = lax.bitcast_convert_type(x[:, w:].astype(BF16).astype(F32), I32)
    return hi | lax.shift_right_logical(lo, jnp.full(lo.shape, 16, I32))


def _unpack_bf16_pairs(p):
    hi = lax.bitcast_convert_type(p & jnp.int32(-65536), F32)
    lo = lax.bitcast_convert_type(lax.shift_left(p, jnp.full(p.shape, 16, I32)), F32)
    return hi, lo


def _ada_kernel(c_ref, w_ref, b_ref, o_ref):
    s = _silu(c_ref[...])
    o_ref[...] = jnp.dot(s, w_ref[...], precision=HIGHEST, preferred_element_type=F32) + b_ref[...]


def _ada(cvecs, w_ada, b_ada):
    n_out = w_ada.shape[1]
    tn = 1536
    return pl.pallas_call(
        _ada_kernel,
        grid=(n_out // tn,),
        in_specs=[
            pl.BlockSpec((SUBLANES, D_MODEL), lambda j: (0, 0)),
            pl.BlockSpec((D_MODEL, tn), lambda j: (0, j)),
            pl.BlockSpec((1, tn), lambda j: (0, j)),
        ],
        out_specs=pl.BlockSpec((SUBLANES, tn), lambda j: (0, j)),
        out_shape=jax.ShapeDtypeStruct((SUBLANES, n_out), F32),
        name="ada",
        compiler_params=_params("arbitrary"),
    )(cvecs, w_ada, b_ada.reshape(1, n_out))


def _pos_tile(rt_ref, ct_ref):
    left = jnp.concatenate([jnp.broadcast_to(rt_ref[r:r + 1, :], (GRID_W, rt_ref.shape[1]))
                            for r in range(rt_ref.shape[0])], axis=0)
    right = jnp.concatenate([ct_ref[...]] * rt_ref.shape[0], axis=0)
    return jnp.concatenate([left, right], axis=1)


def _in_proj_kernel(x_ref, rt_ref, ct_ref, mod_ref, n1_ref, wq_ref, wr_ref, wgh_ref, gb_ref,
                    qt_ref, k_ref, kt_ref, v_ref, vt_ref, ot_ref, xr_ref, gr_ref, g_ref, gt_ref):
    x = x_ref[...] + _pos_tile(rt_ref, ct_ref)
    hn = _rms(x, n1_ref[...]) * (1.0 + mod_ref[0, 1:2, :]) + mod_ref[0, 0:1, :]
    hb = hn.astype(BF16)
    z = jnp.dot(hb, wq_ref[...], preferred_element_type=F32)
    k = z[:, D_ML:2 * D_ML] * (HEAD ** -0.5)
    v = z[:, 2 * D_ML:3 * D_ML]
    qt_ref[...] = z[:, 0:D_ML].T.astype(BF16)
    k_ref[...] = k.astype(BF16)
    kt_ref[...] = k.T.astype(BF16)
    v_ref[...] = v.astype(BF16)
    vt_ref[...] = v.T.astype(BF16)
    ot_ref[...] = z[:, 3 * D_ML:4 * D_ML].T
    zr = jnp.dot(hb, wr_ref[...], preferred_element_type=F32)
    xr_ref[...] = zr[:, 0:D_RG]
    gr_ref[...] = zr[:, D_RG:2 * D_RG]
    zg = zr[:, 2 * D_RG:2 * D_RG + LANES]
    h_lo = (hn - hb.astype(F32)).astype(BF16)
    g = (zg + pltpu.roll(zg, LANES - N_GATE, 1)
         + jnp.dot(h_lo, wgh_ref[...], preferred_element_type=F32) + gb_ref[...])
    col = lax.broadcasted_iota(I32, g.shape, 1)
    g = jnp.where((col & 4) != 0, -_softplus(-g), g)
    g_ref[...] = g[:, 0:N_GATE]
    gt_ref[...] = g.T[0:N_GATE, :]


def _mod_index(bm, tm, t):
    if bm > 1:
        return lambda i: ((i * tm) // t, 0, 0)
    return lambda i: (0, 0, 0)


def _pos_specs(rtab, ctab, tm):
    rows = tm // GRID_W
    period = rtab.shape[0] // rows
    return [pl.BlockSpec((rows, D_MODEL // 2), lambda i: (i % period, 0)),
            pl.BlockSpec((GRID_W, D_MODEL // 2), lambda i: (0, 0))]


def _in_proj(x2d, rtab, ctab, mod, t, tm, norm1, wq, wr, wgh, gbias):
    n = x2d.shape[0]
    tok = lambda i: (i, 0)
    tok_t = lambda i: (0, i)
    const = lambda i: (0, 0)
    f = lambda w: jax.ShapeDtypeStruct((n, w), F32)
    row16 = jax.ShapeDtypeStruct((n, D_ML), BF16)
    col16 = jax.ShapeDtypeStruct((D_ML, n), BF16)
    return pl.pallas_call(
        _in_proj_kernel,
        grid=(n // tm,),
        in_specs=[pl.BlockSpec((tm, D_MODEL), tok)] + _pos_specs(rtab, ctab, tm) + [
            pl.BlockSpec((1, N_MOD, D_MODEL), _mod_index(mod.shape[0], tm, t)),
            pl.BlockSpec((1, D_MODEL), const),
            pl.BlockSpec((D_MODEL, 4 * D_ML), const),
            pl.BlockSpec((D_MODEL, 2 * D_RG + LANES), const),
            pl.BlockSpec((D_MODEL, LANES), const),
            pl.BlockSpec((1, LANES), const),
        ],
        out_specs=[pl.BlockSpec((D_ML, tm), tok_t), pl.BlockSpec((tm, D_ML), tok), pl.BlockSpec((D_ML, tm), tok_t),
                   pl.BlockSpec((tm, D_ML), tok), pl.BlockSpec((D_ML, tm), tok_t), pl.BlockSpec((D_ML, tm), tok_t),
                   pl.BlockSpec((tm, D_RG), tok), pl.BlockSpec((tm, D_RG), tok),
                   pl.BlockSpec((tm, N_GATE), tok), pl.BlockSpec((N_GATE, tm), tok_t)],
        out_shape=[col16, row16, col16, row16, col16, jax.ShapeDtypeStruct((D_ML, n), F32),
                   f(D_RG), f(D_RG), f(N_GATE), jax.ShapeDtypeStruct((N_GATE, n), F32)],
        name="in_proj",
        compiler_params=_params("arbitrary"),
    )(x2d, rtab, ctab, mod, norm1, wq, wr, wgh, gbias)


def _mlstm_kernel(chunk, nc, emit_state, *refs):
    (qtf, kf, ktf, vf, vtf, gf, gtf, qtb, kb, ktb, vb, vtb, gb, gtb, c0_ref, n0_ref, m0_ref) = refs[:17]
    if emit_state:
        hf_ref, hb_ref, c_out, n_out, m_out, c_sc, n_sc, m_sc = refs[17:]
    else:
        hf_ref, hb_ref, c_sc, n_sc, m_sc = refs[17:]
    i = pl.program_id(1)

    @pl.when(i == 0)
    def _():
        c_sc[...] = c0_ref[0]
        n_sc[...] = n0_ref[0]
        m_sc[...] = m0_ref[0]

    key = lax.broadcasted_iota(I32, (chunk, chunk), 0)
    qry = lax.broadcasted_iota(I32, (chunk, chunk), 1)
    hd = []
    for d, (qt_ref, k_ref, kt_ref, v_ref, vt_ref, g_ref, gt_ref, h_ref) in enumerate(
            ((qtf, kf, ktf, vf, vtf, gf, gtf, hf_ref), (qtb, kb, ktb, vb, vtb, gb, gtb, hb_ref))):
        tri = (key <= qry) if d == 0 else (key >= qry)
        tri_t = (qry <= key) if d == 0 else (qry >= key)
        g = g_ref[...]
        gt = gt_ref[...]
        rows3 = jnp.dot(jnp.concatenate(_bf16_pieces(gt), axis=0), tri.astype(BF16), preferred_element_type=F32)
        brow = rows3[0:N_GATE] + rows3[N_GATE:2 * N_GATE] + rows3[2 * N_GATE:3 * N_GATE]
        tri_t16 = tri_t.astype(BF16)
        bcol = sum(jnp.dot(tri_t16, piece, preferred_element_type=F32) for piece in _bf16_pieces(g))
        blast = bcol[chunk - 1:chunk, :] if d == 0 else bcol[0:1, :]
        for h in range(N_HEADS):
            ci = d * 8 + h
            cf = d * 8 + 4 + h
            j = d * N_HEADS + h
            sl = slice(h * HEAD, (h + 1) * HEAD)
            hd.append(dict(
                j=j, sl=sl, tri=tri, h_ref=h_ref, qt=qt_ref[sl, :], k=k_ref[:, sl], kt=kt_ref[sl, :],
                v=v_ref[:, sl], vt=vt_ref[sl, :], b_row=brow[cf:cf + 1, :],
                gate_col=g[:, ci:ci + 1] - bcol[:, cf:cf + 1], gate_row=gt[ci:ci + 1, :] - brow[cf:cf + 1, :],
                b_last=blast[:, cf:cf + 1], m_prev=m_sc[j:j + 1, 0:1], c_prev=c_sc[j], n_prev=n_sc[j:j + 1, :]))
    for x in hd:
        top = jnp.max(jnp.where(x['tri'], x['gate_col'], -jnp.inf), axis=0, keepdims=True)
        mx = jnp.maximum(x['m_prev'], top)
        x['dm'] = jnp.exp(jnp.where(x['tri'], x['gate_col'] - mx, -jnp.inf))
        x['w_inter'] = jnp.exp(x['m_prev'] - mx)
        x['floor'] = jnp.exp(-(x['b_row'] + mx))
    for x in hd:
        x['st'] = jnp.dot(x['k'], x['qt'], preferred_element_type=F32) * x['dm']
    for x in hd:
        inter = lax.dot_general(x['c_prev'].astype(BF16), x['qt'], (((0,), (0,)), ((), ())),
                                preferred_element_type=F32)
        num = x['w_inter'] * inter + jnp.dot(x['vt'], x['st'].astype(BF16), preferred_element_type=F32)
        qn = jnp.dot(jnp.broadcast_to(x['n_prev'], (SUBLANES, HEAD)).astype(BF16), x['qt'],
                     preferred_element_type=F32)[0:1, :]
        den = x['w_inter'] * qn + jnp.sum(x['st'], axis=0, keepdims=True)
        x['h_ref'][x['sl'], :] = num / jnp.maximum(jnp.abs(den), x['floor'])
    for x in hd:
        j = x['j']
        log_w = x['b_last'] + x['gate_row']
        m_new = jnp.maximum(x['b_last'] + x['m_prev'], jnp.max(log_w, axis=1, keepdims=True))
        decay = jnp.exp(x['b_last'] + x['m_prev'] - m_new)
        w_row = jnp.exp(log_w - m_new)
        kwt = (x['kt'].astype(F32) * w_row).astype(BF16)
        c_sc[j] = decay * x['c_prev'] + jnp.dot(kwt, x['v'], preferred_element_type=F32)
        n_sc[j:j + 1, :] = decay * x['n_prev'] + jnp.dot(
            jnp.broadcast_to(w_row, (SUBLANES, chunk)).astype(BF16), x['k'], preferred_element_type=F32)[0:1, :]
        m_sc[j:j + 1, :] = jnp.broadcast_to(m_new, (1, HEAD))

    if emit_state:
        @pl.when(i == nc - 1)
        def _():
            c_out[0] = c_sc[...]
            n_out[0] = n_sc[...]
            m_out[0] = m_sc[...]


def _state_index(bm):
    if bm > 1:
        return lambda b, i: (b,) + (0,) * 3, lambda b, i: (b, 0, 0)
    return lambda b, i: (0,) * 4, lambda b, i: (0, 0, 0)


def _mlstm(qt, k, kt, v, vt, g, gt, c0, n0, m0, b, t, chunk, emit_state):
    nc = t // chunk
    n = b * t
    nd = 2 * N_HEADS
    fwd = lambda bi, i: (bi * nc + i, 0)
    bwd = lambda bi, i: (bi * nc + nc - 1 - i, 0)
    fwd_t = lambda bi, i: (0, bi * nc + i)
    bwd_t = lambda bi, i: (0, bi * nc + nc - 1 - i)
    c_idx, n_idx = _state_index(c0.shape[0])
    rows = lambda m: pl.BlockSpec((chunk, D_ML), m)
    cols = lambda m: pl.BlockSpec((D_ML, chunk), m)
    one_dir = lambda m, mt: [cols(mt), rows(m), cols(mt), rows(m), cols(mt),
                             pl.BlockSpec((chunk, N_GATE), m), pl.BlockSpec((N_GATE, chunk), mt)]
    in_specs = (one_dir(fwd, fwd_t) + one_dir(bwd, bwd_t)
                + [pl.BlockSpec((1, nd, HEAD, HEAD), c_idx),
                   pl.BlockSpec((1, nd, HEAD), n_idx), pl.BlockSpec((1, nd, HEAD), n_idx)])
    out_specs = [cols(fwd_t), cols(bwd_t)]
    out_shape = [jax.ShapeDtypeStruct((D_ML, n), F32)] * 2
    if emit_state:
        out_specs += [pl.BlockSpec((1, nd, HEAD, HEAD), lambda bi, i: (bi, 0, 0, 0)),
                      pl.BlockSpec((1, nd, HEAD), lambda bi, i: (bi, 0, 0)),
                      pl.BlockSpec((1, nd, HEAD), lambda bi, i: (bi, 0, 0))]
        out_shape += [jax.ShapeDtypeStruct((b, nd, HEAD, HEAD), F32),
                      jax.ShapeDtypeStruct((b, nd, HEAD), F32),
                      jax.ShapeDtypeStruct((b, nd, HEAD), F32)]
    return pl.pallas_call(
        functools.partial(_mlstm_kernel, chunk, nc, emit_state),
        grid=(b, nc),
        in_specs=in_specs,
        out_specs=out_specs,
        out_shape=out_shape,
        scratch_shapes=[pltpu.VMEM((nd, HEAD, HEAD), F32), pltpu.VMEM((nd, HEAD), F32),
                        pltpu.VMEM((nd, HEAD), F32)],
        name="mlstm",
        compiler_params=_params("arbitrary", "arbitrary"),
    )(qt, k, kt, v, vt, g, gt, qt, k, kt, v, vt, g, gt, c0, n0, m0)


def _neg_expm1(x, u):
    near = jnp.where(u == 1.0, x, (u - 1.0) * x / jnp.log(u))
    return -jnp.where(x < -0.5, u - 1.0, near)


def _rglru_kernel(tb, nb, emit_state, *refs):
    (xf, xf_prev, xf_next, xb, xb_prev, xb_next, h0_ref, w_ref, bias_ref, lam_ref, cw_ref, cb_ref) = refs[:12]
    n_out = 3 if emit_state else 2
    hf_ref, hb_ref = refs[12:14]
    hfin_ref = refs[14] if emit_state else None
    carry, af_sc, uf_sc, ab_sc, ub_sc, hf_sc, pf_sc, hb_sc, pb_sc = refs[12 + n_out:]
    i = pl.program_id(1)

    @pl.when(i == 0)
    def _():
        carry[...] = h0_ref[0]

    row8 = lax.broadcasted_iota(I32, (SUBLANES, D_RG), 0)
    cw = cw_ref[...]
    softplus_neg_lam = _softplus(-lam_ref[...])

    def taps(xm2, xm1, x0, xp1):
        return cb_ref[...] + xm2 * cw[0:1, :] + xm1 * cw[1:2, :] + x0 * cw[2:3, :] + xp1 * cw[3:4, :]

    def conv(main_ref, prev_ref, next_ref, first, last):
        main = main_ref[...]
        prev = jnp.where(first, 0.0, prev_ref[...])
        nxt = jnp.where(last, 0.0, next_ref[...])
        body = taps(pltpu.roll(main, 2, 0), pltpu.roll(main, 1, 0), main, pltpu.roll(main, tb - 1, 0))
        e = SUBLANES
        head, tail = main[0:e, :], main[tb - e:tb, :]
        before_tail = main[tb - 2 * e:tb - e, :]
        fix_head = taps(
            jnp.where(row8 == 0, prev[6:7, :], jnp.where(row8 == 1, prev[7:8, :], pltpu.roll(head, 2, 0))),
            jnp.where(row8 == 0, prev[7:8, :], pltpu.roll(head, 1, 0)), head,
            jnp.where(row8 == e - 1, main[e:e + 1, :], pltpu.roll(head, e - 1, 0)))
        fix_tail = taps(
            jnp.where(row8 == 0, before_tail[6:7, :],
                      jnp.where(row8 == 1, before_tail[7:8, :], pltpu.roll(tail, 2, 0))),
            jnp.where(row8 == 0, before_tail[7:8, :], pltpu.roll(tail, 1, 0)), tail,
            jnp.where(row8 == e - 1, nxt[0:1, :], pltpu.roll(tail, e - 1, 0)))
        return jnp.concatenate([fix_head, body[e:tb - e, :], fix_tail], axis=0)

    def recurrence_terms(xc, d):
        z = jnp.dot(xc.astype(BF16), w_ref[:, d * 2 * D_RG:(d + 1) * 2 * D_RG],
                    preferred_element_type=F32) + bias_ref[:, d * 2 * D_RG:(d + 1) * 2 * D_RG]
        r = jax.nn.sigmoid(z[:, 0:D_RG])
        ig = jax.nn.sigmoid(z[:, D_RG:2 * D_RG])
        log_a = -RG_C * r * softplus_neg_lam[d:d + 1, :]
        a = jnp.exp(log_a)
        u = jnp.sqrt(_neg_expm1(2.0 * log_a, a * a)) * (ig * xc)
        return a, u

    a_f, u_f = recurrence_terms(conv(xf, xf_prev, xf_next, i == 0, i == nb - 1), 0)
    a_b, u_b = recurrence_terms(conv(xb, xb_prev, xb_next, i == nb - 1, i == 0), 1)
    seg = tb // SUBLANES
    pitch = seg + RG_SEG_PAD
    ncol = D_RG // LANES
    for lc in range(ncol):
        lanes = slice(lc * LANES, (lc + 1) * LANES)
        for s in range(SUBLANES):
            src = slice(s * seg, (s + 1) * seg)
            dst = slice(s * pitch, s * pitch + seg)
            af_sc[lc, dst, :], uf_sc[lc, dst, :] = a_f[src, lanes], u_f[src, lanes]
            ab_sc[lc, dst, :], ub_sc[lc, dst, :] = a_b[src, lanes], u_b[src, lanes]
    slab = lambda k: (slice(None), pl.ds(k, SUBLANES, stride=pitch), slice(None))
    hf = jnp.zeros((ncol, SUBLANES, LANES), F32)
    hb = jnp.zeros((ncol, SUBLANES, LANES), F32)
    pf = jnp.ones((ncol, SUBLANES, LANES), F32)
    pb = jnp.ones((ncol, SUBLANES, LANES), F32)
    for k in range(seg):
        kb = seg - 1 - k
        ak = af_sc[slab(k)]
        hf = ak * hf + uf_sc[slab(k)]
        pf = pf * ak
        hf_sc[slab(k)] = hf
        pf_sc[slab(k)] = pf
        ak = ab_sc[slab(kb)]
        hb = ak * hb + ub_sc[slab(kb)]
        pb = pb * ak
        hb_sc[slab(kb)] = hb
        pb_sc[slab(kb)] = pb
    for lc in range(ncol):
        lanes = slice(lc * LANES, (lc + 1) * LANES)
        c = carry[0:1, lanes]
        cin_f = []
        for s in range(SUBLANES):
            cin_f.append(c)
            c = pf[lc, s:s + 1, :] * c + hf[lc, s:s + 1, :]
        carry[0:1, lanes] = c
        c = carry[1:2, lanes]
        cin_b = [None] * SUBLANES
        for s in reversed(range(SUBLANES)):
            cin_b[s] = c
            c = pb[lc, s:s + 1, :] * c + hb[lc, s:s + 1, :]
        carry[1:2, lanes] = c
        for s in range(SUBLANES):
            rows = slice(s * seg, (s + 1) * seg)
            src = slice(s * pitch, s * pitch + seg)
            hf_ref[rows, lanes] = hf_sc[lc, src, :] + pf_sc[lc, src, :] * cin_f[s]
            hb_ref[rows, lanes] = hb_sc[lc, src, :] + pb_sc[lc, src, :] * cin_b[s]

    if emit_state:
        @pl.when(i == nb - 1)
        def _():
            hfin_ref[0] = carry[...]


def _rglru(xr, h0, wbd, bias, lam, cw, cb, b, t, tb, emit_state):
    nb = t // tb
    n = b * t
    r8 = tb // SUBLANES
    last8 = n // SUBLANES - 1
    fwd = lambda bi, i: (bi * nb + i, 0)
    bwd = lambda bi, i: (bi * nb + nb - 1 - i, 0)
    fwd_prev = lambda bi, i: (jnp.maximum((bi * nb + i) * r8 - 1, 0), 0)
    fwd_next = lambda bi, i: (jnp.minimum((bi * nb + i + 1) * r8, last8), 0)
    bwd_prev = lambda bi, i: (jnp.maximum((bi * nb + nb - 1 - i) * r8 - 1, 0), 0)
    bwd_next = lambda bi, i: (jnp.minimum((bi * nb + nb - i) * r8, last8), 0)
    const = lambda bi, i: (0, 0)
    h_idx = (lambda bi, i: (bi, 0, 0)) if h0.shape[0] > 1 else (lambda bi, i: (0, 0, 0))
    halo = lambda m: pl.BlockSpec((SUBLANES, D_RG), m)
    in_specs = [pl.BlockSpec((tb, D_RG), fwd), halo(fwd_prev), halo(fwd_next),
                pl.BlockSpec((tb, D_RG), bwd), halo(bwd_prev), halo(bwd_next),
                pl.BlockSpec((1, 2, D_RG), h_idx),
                pl.BlockSpec((D_RG, 4 * D_RG), const), pl.BlockSpec((1, 4 * D_RG), const),
                pl.BlockSpec((2, D_RG), const), pl.BlockSpec((4, D_RG), const), pl.BlockSpec((1, D_RG), const)]
    out_specs = [pl.BlockSpec((tb, D_RG), fwd), pl.BlockSpec((tb, D_RG), bwd)]
    out_shape = [jax.ShapeDtypeStruct((n, D_RG), F32)] * 2
    if emit_state:
        out_specs.append(pl.BlockSpec((1, 2, D_RG), lambda bi, i: (bi, 0, 0)))
        out_shape.append(jax.ShapeDtypeStruct((b, 2, D_RG), F32))
    return pl.pallas_call(
        functools.partial(_rglru_kernel, tb, nb, emit_state),
        grid=(b, nb),
        in_specs=in_specs,
        out_specs=out_specs,
        out_shape=out_shape,
        scratch_shapes=[pltpu.VMEM((2, D_RG), F32)]
        + [pltpu.VMEM((D_RG // LANES, tb + SUBLANES * RG_SEG_PAD, LANES), F32)] * 8,
        name="rglru",
        compiler_params=_params("arbitrary", "arbitrary"),
    )(xr, xr, xr, xr, xr, xr, h0, wbd, bias, lam, cw, cb)


def _route(s, sb):
    tm = s.shape[1]
    neg = -jnp.inf
    sub = lax.broadcasted_iota(I32, (GROUP, tm), 0)
    blocks = [sb[gi * GROUP:(gi + 1) * GROUP, :] for gi in range(N_GROUPS)]
    gscore = []
    for blk in blocks:
        m1 = jnp.max(blk, axis=0, keepdims=True)
        first = jnp.min(jnp.where(blk == m1, sub, GROUP), axis=0, keepdims=True)
        m2 = jnp.max(jnp.where(sub == first, neg, blk), axis=0, keepdims=True)
        gscore.append(m1 + m2)
    masked = []
    for gi in range(N_GROUPS):
        rank = jnp.zeros((1, tm), F32)
        for gj in range(N_GROUPS):
            if gj == gi:
                continue
            ahead = (gscore[gj] >= gscore[gi]) if gj < gi else (gscore[gj] > gscore[gi])
            rank = rank + jnp.where(ahead, 1.0, 0.0)
        masked.append(jnp.where(rank < TOPK_GROUPS, blocks[gi], neg))
    v = jnp.concatenate(masked, axis=0)
    eid = lax.broadcasted_iota(I32, (N_EXPERTS, tm), 0)
    sel = jnp.zeros((N_EXPERTS, tm), F32)
    picks = []
    for _ in range(TOP_K):
        mx = jnp.max(v, axis=0, keepdims=True)
        idx = jnp.min(jnp.where(v == mx, eid, N_EXPERTS), axis=0, keepdims=True)
        pick = eid == idx
        picks.append(pick)
        sel = jnp.where(pick, 1.0, sel)
        v = jnp.where(pick, neg, v)
    ws = s * sel
    return ws / jnp.sum(ws, axis=0, keepdims=True) * ROUTED_SCALE, sel, picks


def _mix_out_kernel(hmf_ref, hmb_ref, ot_ref, hrf_ref, hrb_ref, gr_ref, x_ref, rt_ref, ct_ref, mod_ref,
                    mln_ref, rgn_ref, wo_ml_ref, wo_rg_ref, n2_ref, rwt_ref, rb_ref, cnt0_ref,
                    x1_ref, hn2p_ref, ek_ref, pk_ref, wtok_ref, cnt_ref, cnt_sc):
    i = pl.program_id(0)
    tm = x_ref.shape[0]

    @pl.when(i == 0)
    def _():
        cnt_sc[...] = cnt0_ref[...].astype(F32)

    hm = hmf_ref[...] + hmb_ref[...]
    heads = []
    for h in range(N_HEADS):
        seg = hm[h * HEAD:(h + 1) * HEAD, :]
        heads.append(seg * lax.rsqrt(jnp.mean(seg * seg, axis=0, keepdims=True) + EPS))
    y_ml_t = jnp.concatenate(heads, axis=0) * mln_ref[...] * jax.nn.sigmoid(ot_ref[...])
    y_rg = _rms(hrf_ref[...] + hrb_ref[...], rgn_ref[...]) * jax.nn.gelu(gr_ref[...])
    mix = (lax.dot_general(y_ml_t.astype(BF16), wo_ml_ref[...], (((0,), (0,)), ((), ())),
                           preferred_element_type=F32)
           + jnp.dot(y_rg.astype(BF16), wo_rg_ref[...], preferred_element_type=F32))
    x1 = x_ref[...] + _pos_tile(rt_ref, ct_ref) + mod_ref[0, 2:3, :] * mix
    x1_ref[...] = x1
    hn2 = _rms(x1, n2_ref[...]) * (1.0 + mod_ref[0, 4:5, :]) + mod_ref[0, 3:4, :]
    hn2p_ref[...] = _pack_bf16_pairs(hn2)
    hb = hn2.astype(BF16)
    h_lo = (hn2 - hb.astype(F32)).astype(BF16)
    nt = (((1,), (1,)), ((), ()))
    two = lax.dot_general(rwt_ref[...], hb, nt, preferred_element_type=F32)
    logits_t = (two[0:N_EXPERTS, :] + two[N_EXPERTS:2 * N_EXPERTS, :]
                + lax.dot_general(rwt_ref[0:N_EXPERTS, :], h_lo, nt, preferred_element_type=F32))
    s = jax.nn.sigmoid(logits_t)
    wt, sel, picks = _route(s, s + rb_ref[...])

    earlier = (lax.broadcasted_iota(I32, (tm, tm), 0) < lax.broadcasted_iota(I32, (tm, tm), 1))
    prefix = jnp.dot(sel.astype(BF16), earlier.astype(BF16), preferred_element_type=F32)
    pos_all = cnt_sc[:, 0:1] + prefix
    eid = lax.broadcasted_iota(I32, (N_EXPERTS, tm), 0)
    eid_f = eid.astype(F32)
    row8 = lax.broadcasted_iota(I32, (TOP_K, tm), 0)
    ek = jnp.zeros((TOP_K, tm), F32)
    pk = jnp.zeros((TOP_K, tm), F32)
    wk = jnp.zeros((N_EXPERTS, tm), F32)
    for k, pick in enumerate(picks):
        take = lambda a: jnp.sum(jnp.where(pick, a, 0.0), axis=0, keepdims=True)
        ek = jnp.where(row8 == k, take(eid_f), ek)
        pk = jnp.where(row8 == k, take(pos_all), pk)
        wk = jnp.where(eid == k, take(wt), wk)
    ek_ref[...] = ek.astype(I32)
    pk_ref[...] = pk.astype(I32)
    wtok_ref[...] = wk.T
    cnt_sc[...] += jnp.broadcast_to(jnp.sum(sel, axis=1, keepdims=True), cnt_sc.shape)

    @pl.when(i == pl.num_programs(0) - 1)
    def _():
        cnt_ref[...] = cnt_sc[...].astype(I32)


def _mix_out(hmf, hmb, ot, hrf, hrb, gr, x2d, rtab, ctab, mod, t, tm, mln, rgn, wo_ml, wo_rg, norm2, rwt, rbias,
             cnt0):
    n = x2d.shape[0]
    tok = lambda i: (i, 0)
    tok_t = lambda i: (0, i)
    const = lambda i: (0, 0)
    return pl.pallas_call(
        _mix_out_kernel,
        grid=(n // tm,),
        in_specs=[pl.BlockSpec((D_ML, tm), tok_t)] * 3 + [pl.BlockSpec((tm, D_RG), tok)] * 3 + [
            pl.BlockSpec((tm, D_MODEL), tok)] + _pos_specs(rtab, ctab, tm) + [
            pl.BlockSpec((1, N_MOD, D_MODEL), _mod_index(mod.shape[0], tm, t)),
            pl.BlockSpec((D_ML, 1), const), pl.BlockSpec((1, D_RG), const),
            pl.BlockSpec((D_ML, D_MODEL), const), pl.BlockSpec((D_RG, D_MODEL), const),
            pl.BlockSpec((1, D_MODEL), const),
            pl.BlockSpec((2 * N_EXPERTS, D_MODEL), const), pl.BlockSpec((N_EXPERTS, 1), const),
            pl.BlockSpec((N_EXPERTS, LANES), const),
        ],
        out_specs=[pl.BlockSpec((tm, D_MODEL), tok), pl.BlockSpec((tm, D_PACK), tok),
                   pl.BlockSpec((TOP_K, tm), tok_t), pl.BlockSpec((TOP_K, tm), tok_t),
                   pl.BlockSpec((tm, N_EXPERTS), tok), pl.BlockSpec((N_EXPERTS, LANES), const)],
        out_shape=[jax.ShapeDtypeStruct((n, D_MODEL), F32), jax.ShapeDtypeStruct((n, D_PACK), I32),
                   jax.ShapeDtypeStruct((TOP_K, n), I32), jax.ShapeDtypeStruct((TOP_K, n), I32),
                   jax.ShapeDtypeStruct((n, N_EXPERTS), F32), jax.ShapeDtypeStruct((N_EXPERTS, LANES), I32)],
        scratch_shapes=[pltpu.VMEM((N_EXPERTS, LANES), F32)],
        name="mix_out",
        compiler_params=_params("arbitrary"),
    )(hmf, hmb, ot, hrf, hrb, gr, x2d, rtab, ctab, mod, mln, rgn, wo_ml, wo_rg, norm2, rwt, rbias, cnt0)


def _sc_mesh():
    return plsc.VectorSubcoreMesh(core_axis_name="core", subcore_axis_name="subcore")


def _sc_worker():
    info = plsc.get_sparse_core_info()
    return lax.axis_index("subcore") * info.num_cores + lax.axis_index("core"), info.num_cores * info.num_subcores


def _sc_dispatch(xp, dest3, n_slots):
    n, w = xp.shape
    nwin = n // SC_WINDOW

    @pl.kernel(out_type=jax.ShapeDtypeStruct((n_slots, w), xp.dtype), mesh=_sc_mesh(),
               scratch_types=[pltpu.VMEM((SC_WINDOW, w), xp.dtype), pltpu.VMEM((TOP_K, SC_WINDOW), I32)],
               name="sc_dispatch")
    def k(x_hbm, i_hbm, o_hbm, x_v, i_v):
        wid, nworkers = _sc_worker()
        per = nwin // nworkers

        @pl.loop(0, per)
        def _(s):
            win = wid * per + s
            pltpu.sync_copy(x_hbm.at[pl.ds(win * SC_WINDOW, SC_WINDOW)], x_v)
            pltpu.sync_copy(i_hbm.at[win], i_v)
            for j in range(TOP_K):
                pltpu.sync_copy(x_v, o_hbm.at[i_v.at[j]])

    return k(xp, dest3)


def _sc_combine_gather(ys, dest3):
    nwin = dest3.shape[0]
    w = ys.shape[1]

    @pl.kernel(out_type=jax.ShapeDtypeStruct((nwin, TOP_K, SC_WINDOW, w), ys.dtype), mesh=_sc_mesh(),
               scratch_types=[pltpu.VMEM((SC_WINDOW, w), ys.dtype), pltpu.VMEM((TOP_K, SC_WINDOW), I32)],
               name="sc_combine")
    def k(y_hbm, i_hbm, o_hbm, y_v, i_v):
        wid, nworkers = _sc_worker()
        per = nwin // nworkers

        @pl.loop(0, per)
        def _(s):
            win = wid * per + s
            pltpu.sync_copy(i_hbm.at[win], i_v)
            for j in range(TOP_K):
                pltpu.sync_copy(y_hbm.at[i_v.at[j]], y_v)
                pltpu.sync_copy(y_v, o_hbm.at[win, j])

    return k(ys, dest3)


def _swiglu(x, w13):
    h = jnp.dot(x, w13, preferred_element_type=F32)
    return _silu(h[:, 0:D_EXPERT]) * h[:, D_EXPERT:2 * D_EXPERT]


def _unpack_rows_bf16(p):
    hi, lo = _unpack_bf16_pairs(p)
    return jnp.concatenate([hi.astype(BF16), lo.astype(BF16)], axis=1)


def _expert_kernel(rows, rank_ref, expert_ref, nu_ref, nr_ref, x_ref, w1_hbm, w3_hbm, w2_hbm, y_ref,
                   w1buf, w3buf, w2buf, sem, progress):
    b = pl.program_id(0)
    g = EXPERT_BLOCKS_PER_STEP
    n_ranks = nr_ref[0]

    @pl.when(b == 0)
    def _():
        progress[0] = 0
        progress[1] = 0

    def copies(r):
        slot = lax.rem(r, EXPERT_RING)
        e = expert_ref[r]
        return (pltpu.make_async_copy(w1_hbm.at[e], w1buf.at[slot], sem.at[slot, 0]),
                pltpu.make_async_copy(w3_hbm.at[e], w3buf.at[slot], sem.at[slot, 1]),
                pltpu.make_async_copy(w2_hbm.at[e], w2buf.at[slot], sem.at[slot, 2]))

    def start(r, carry):
        for cp in copies(r):
            cp.start()
        return carry

    def wait(r, carry):
        for cp in copies(r):
            cp.wait()
        return carry

    first = rank_ref[b * g]
    last = rank_ref[b * g + g - 1]
    started = jnp.minimum(first + EXPERT_RING, n_ranks)
    lax.fori_loop(progress[0], started, start, 0)
    progress[0] = jnp.maximum(progress[0], started)
    needed = jnp.where(b == pl.num_programs(0) - 1, progress[0], jnp.minimum(last + 1, n_ranks))
    lax.fori_loop(progress[1], needed, wait, 0)
    progress[1] = jnp.maximum(progress[1], needed)

    @pl.when(b * g < nu_ref[0])
    def _():
        for j in range(g):
            slot = lax.rem(rank_ref[b * g + j], EXPERT_RING)
            sl = slice(j * rows, (j + 1) * rows)
            x = _unpack_rows_bf16(x_ref[sl, :])
            h = (_silu(jnp.dot(x, w1buf[slot].astype(BF16), preferred_element_type=F32))
                 * jnp.dot(x, w3buf[slot].astype(BF16), preferred_element_type=F32))
            y_ref[sl, :] = _pack_bf16_pairs(
                jnp.dot(h.astype(BF16), w2buf[slot].astype(BF16), preferred_element_type=F32))


def _experts(xs, block_rank, rank_expert, n_used, n_ranks, w1, w3, w2, rows):
    g = EXPERT_BLOCKS_PER_STEP
    nb = xs.shape[0] // rows
    tok = lambda b, *_: (b, 0)
    return pl.pallas_call(
        functools.partial(_expert_kernel, rows),
        grid_spec=pltpu.PrefetchScalarGridSpec(
            num_scalar_prefetch=4,
            grid=(nb // g,),
            in_specs=[pl.BlockSpec((g * rows, D_PACK), tok)] + [pl.BlockSpec(memory_space=pl.ANY)] * 3,
            out_specs=pl.BlockSpec((g * rows, D_PACK), tok),
            scratch_shapes=[pltpu.VMEM((EXPERT_RING, D_MODEL, D_EXPERT), F32),
                            pltpu.VMEM((EXPERT_RING, D_MODEL, D_EXPERT), F32),
                            pltpu.VMEM((EXPERT_RING, D_EXPERT, D_MODEL), F32),
                            pltpu.SemaphoreType.DMA((EXPERT_RING, 3)), pltpu.SMEM((2,), I32)],
        ),
        out_shape=jax.ShapeDtypeStruct(xs.shape, I32),
        name="experts",
        compiler_params=_params("arbitrary", vmem=VMEM_LIMIT_EXPERTS),
    )(block_rank, rank_expert, n_used, n_ranks, xs, w1, w3, w2)


def _moe_out_kernel(yk_ref, hn2p_ref, wtok_ref, sw13_ref, sw2_ref, x1_ref, mod_ref, nf_ref, y_ref):
    shared = jnp.dot(_swiglu(_unpack_rows_bf16(hn2p_ref[...]), sw13_ref[...]).astype(BF16), sw2_ref[...],
                     preferred_element_type=F32)
    w = wtok_ref[...]
    parts = []
    for wi in range(yk_ref.shape[0]):
        rows = slice(wi * SC_WINDOW, (wi + 1) * SC_WINDOW)
        a_hi = shared[rows, 0:D_PACK]
        a_lo = shared[rows, D_PACK:D_MODEL]
        for k in range(TOP_K):
            y_hi, y_lo = _unpack_bf16_pairs(yk_ref[wi, k])
            wc = w[rows, k:k + 1]
            a_hi = a_hi + wc * y_hi
            a_lo = a_lo + wc * y_lo
        parts.append(jnp.concatenate([a_hi, a_lo], axis=1))
    x2 = x1_ref[...] + mod_ref[0, 5:6, :] * jnp.concatenate(parts, axis=0)
    y_ref[...] = _rms(x2, nf_ref[...])


def _moe_out(yk, tok0, hn2p, wtok, sw13, sw2, x1, mod, t, tm, norm_final):
    n = hn2p.shape[0]
    tok = lambda i: (i, 0)
    const = lambda i: (0, 0)
    blk0 = tok0 // tm
    return pl.pallas_call(
        _moe_out_kernel,
        grid=(n // tm,),
        in_specs=[
            pl.BlockSpec((tm // SC_WINDOW, TOP_K, SC_WINDOW, D_PACK), lambda i: (i + blk0, 0, 0, 0)),
            pl.BlockSpec((tm, D_PACK), tok),
            pl.BlockSpec((tm, N_EXPERTS), tok),
            pl.BlockSpec((D_MODEL, 2 * D_EXPERT), const),
            pl.BlockSpec((D_EXPERT, D_MODEL), const),
            pl.BlockSpec((tm, D_MODEL), tok),
            pl.BlockSpec((1, N_MOD, D_MODEL), _mod_index(mod.shape[0], tm, t)),
            pl.BlockSpec((1, D_MODEL), const),
        ],
        out_specs=pl.BlockSpec((tm, D_MODEL), tok),
        out_shape=jax.ShapeDtypeStruct((n, D_MODEL), F32),
        name="moe_out",
        compiler_params=_params("arbitrary"),
    )(yk, hn2p, wtok, sw13, sw2, x1, mod, norm_final)


def _dispatch_plan(cnt, ek, pk, rows):
    n = ek.shape[1]
    nb = n * TOP_K // rows + N_EXPERTS
    nblk = (cnt + rows - 1) // rows
    block_end = jnp.cumsum(nblk)
    experts = jnp.arange(N_EXPERTS, dtype=I32)
    first_row = jnp.sum(jnp.where(ek[:, :, None] == experts, (block_end - nblk) * rows, 0), axis=-1)
    dest3 = (first_row + pk).reshape(TOP_K, n // SC_WINDOW, SC_WINDOW).transpose(1, 0, 2)
    owns = nblk > 0
    n_ranks = jnp.sum(owns.astype(I32))
    blocks = jnp.arange(nb, dtype=I32)[:, None]
    block_rank = jnp.minimum(jnp.sum((owns & (block_end <= blocks)).astype(I32), axis=1), n_ranks - 1)
    rank_expert = jnp.minimum(jnp.sum((jnp.cumsum(owns.astype(I32)) <= experts[:, None]).astype(I32), axis=1),
                              N_EXPERTS - 1)
    return (dest3, block_rank.astype(I32), rank_expert.astype(I32), block_end[-1:].astype(I32),
            n_ranks.reshape(1).astype(I32), nb * rows)


def _grid_pos_tables(n_tokens, dim):
    quarter = dim // 4
    omega = 1.0 / (POS_BASE ** (jnp.arange(quarter, dtype=F32) / quarter))
    ra = jnp.arange(n_tokens // GRID_W).astype(F32)[:, None] * omega
    ca = jnp.arange(GRID_W).astype(F32)[:, None] * omega
    return (jnp.concatenate([jnp.sin(ra), jnp.cos(ra)], axis=-1),
            jnp.concatenate([jnp.sin(ca), jnp.cos(ca)], axis=-1))


def _hi_lo_rows(w):
    hi = w.astype(BF16)
    return jnp.concatenate([hi, (w - hi.astype(F32)).astype(BF16)], axis=0)


def _block_diag(w):
    eye = jnp.eye(N_RG_BLOCKS, dtype=w.dtype)
    return jnp.einsum('nij,nm->nimj', w, eye).reshape(D_RG, D_RG)


def _layer_weights(l, norm1, w_in, mlstm_gate_bias, mlstm_norm, rg_conv_w, rg_conv_b, rg_wa, rg_ba, rg_wx,
                   rg_bx, rg_lambda, rg_norm, w_out, norm2, router_w, router_bias, exp_w1, exp_w3, exp_w2,
                   shared_w1, shared_w3, shared_w2):
    wi = w_in[l]
    c0, c1 = 4 * D_ML, 4 * D_ML + N_GATE
    wg = wi[:, c0:c1]
    wg_hi = wg.astype(BF16)
    wg_lo = (wg - wg_hi.astype(F32)).astype(BF16)
    zcols = lambda w: jnp.zeros((D_MODEL, w), BF16)
    return dict(
        norm1=norm1[l].reshape(1, D_MODEL),
        wq=wi[:, :c0].astype(BF16),
        wr=jnp.concatenate([wi[:, c1:].astype(BF16), wg_hi, wg_lo, zcols(LANES - 2 * N_GATE)], axis=1),
        wgh=jnp.concatenate([wg_hi, zcols(LANES - N_GATE)], axis=1),
        gbias=jnp.pad(mlstm_gate_bias[l].reshape(1, N_GATE), ((0, 0), (0, LANES - N_GATE))),
        mln=mlstm_norm[l].reshape(D_ML, 1),
        cw=rg_conv_w[l], cb=rg_conv_b[l].reshape(1, D_RG),
        wbd=jnp.concatenate([_block_diag(rg_wa[l, 0]), _block_diag(rg_wx[l, 0]),
                             _block_diag(rg_wa[l, 1]), _block_diag(rg_wx[l, 1])], axis=1).astype(BF16),
        rbias=jnp.concatenate([rg_ba[l, 0], rg_bx[l, 0], rg_ba[l, 1], rg_bx[l, 1]]).reshape(1, 4 * D_RG),
        lam=rg_lambda[l], rgn=rg_norm[l].reshape(1, D_RG),
        wo_ml=w_out[l, :D_ML].astype(BF16), wo_rg=w_out[l, D_ML:].astype(BF16),
        norm2=norm2[l].reshape(1, D_MODEL),
        rwt=_hi_lo_rows(router_w[l].T), rb=router_bias[l].reshape(N_EXPERTS, 1),
        w1=exp_w1[l], w3=exp_w3[l], w2=exp_w2[l],
        sw13=jnp.concatenate([shared_w1[l], shared_w3[l]], axis=-1).astype(BF16),
        sw2=shared_w2[l].astype(BF16),
    )


def _mixers(x2d, pos_tables, mod, c0, n0, m0, h0, cnt0, lw, b, t, emit_state):
    tl = _tiles(t, mod.shape[0] > 1)
    tm = tl['tok']
    rtab, ctab = pos_tables
    qt, k, kt, v, vt, ot, xr, gr, g, gt = _in_proj(x2d, rtab, ctab, mod, t, tm, lw['norm1'], lw['wq'], lw['wr'],
                                                   lw['wgh'], lw['gbias'])
    ml = _mlstm(qt, k, kt, v, vt, g, gt, c0, n0, m0, b, t, tl['chunk'], emit_state)
    rg = _rglru(xr, h0, lw['wbd'], lw['rbias'], lw['lam'], lw['cw'], lw['cb'], b, t, tl['scan'], emit_state)
    routed = _mix_out(ml[0], ml[1], ot, rg[0], rg[1], gr, x2d, rtab, ctab, mod, t, tm, lw['mln'], lw['rgn'],
                      lw['wo_ml'], lw['wo_rg'], lw['norm2'], lw['rwt'], lw['rb'], cnt0)
    return routed, ml[2:], rg[2:]


def _routed_experts(paths, lw):
    cnt = paths[-1][5][:, 0]
    hn2p = jnp.concatenate([p[1] for p in paths], axis=0)
    ek = jnp.concatenate([p[2] for p in paths], axis=1)
    pk = jnp.concatenate([p[3] for p in paths], axis=1)
    dest3, block_rank, rank_expert, n_used, n_ranks, n_slots = _dispatch_plan(cnt, ek, pk, EXPERT_ROWS)
    xs = _sc_dispatch(hn2p, dest3, n_slots)
    ys = _experts(xs, block_rank, rank_expert, n_used, n_ranks, lw['w1'], lw['w3'], lw['w2'], EXPERT_ROWS)
    return _sc_combine_gather(ys, dest3)


def kernel(x_prompt, x_sample, c, state_mlstm_C, state_mlstm_n, state_mlstm_m, state_rglru_h, c_ctx, w_ada, b_ada, norm1, w_in, mlstm_gate_bias, mlstm_norm, rg_conv_w, rg_conv_b, rg_wa, rg_ba, rg_wx, rg_bx, rg_lambda, rg_norm, w_out, norm2, router_w, router_bias, exp_w1, exp_w3, exp_w2, shared_w1, shared_w3, shared_w2, norm_final):
    bp, tp, _ = x_prompt.shape
    bs, ts, _ = x_sample.shape
    depth = w_ada.shape[0]
    assert depth == 1, "the final norm is fused into the single layer's MoE output kernel"
    nd = 2 * N_HEADS
    l = 0
    lw = _layer_weights(l, norm1, w_in, mlstm_gate_bias, mlstm_norm, rg_conv_w, rg_conv_b, rg_wa, rg_ba, rg_wx,
                        rg_bx, rg_lambda, rg_norm, w_out, norm2, router_w, router_bias, exp_w1, exp_w3, exp_w2,
                        shared_w1, shared_w3, shared_w2)
    nf = norm_final.reshape(1, D_MODEL)
    cvecs = jnp.concatenate([c_ctx[None], c, jnp.zeros((SUBLANES - 1 - bs, D_MODEL), F32)], axis=0)
    mod = _ada(cvecs, w_ada[l], b_ada[l]).reshape(SUBLANES, N_MOD, D_MODEL)

    mod_p, mod_s = mod[0:1], mod[1:1 + bs]
    tm_p, tm_s = _tiles(tp, False)['tok'], _tiles(ts, True)['tok']
    rp, (cc, nc_, mc), (hc,) = _mixers(
        x_prompt.reshape(bp * tp, D_MODEL),
        (jnp.zeros((tm_p // GRID_W, D_MODEL // 2), F32), jnp.zeros((GRID_W, D_MODEL // 2), F32)), mod_p,
        jnp.zeros((1, nd, HEAD, HEAD), F32), jnp.zeros((1, nd, HEAD), F32), jnp.zeros((1, nd, HEAD), F32),
        jnp.zeros((1, 2, D_RG), F32), jnp.zeros((N_EXPERTS, LANES), I32), lw, bp, tp, True)
    rs, _, _ = _mixers(
        x_sample.reshape(bs * ts, D_MODEL), _grid_pos_tables(ts, D_MODEL), mod_s,
        state_mlstm_C[:, l].reshape(bs, nd, HEAD, HEAD), state_mlstm_n[:, l].reshape(bs, nd, HEAD),
        jnp.broadcast_to(state_mlstm_m[:, l].reshape(bs, nd, 1), (bs, nd, HEAD)),
        state_rglru_h[:, l], jnp.zeros((N_EXPERTS, LANES), I32), lw, bs, ts, False)
    yp = _moe_out(_routed_experts([rp], lw), 0, rp[1], rp[4], lw['sw13'], lw['sw2'], rp[0], mod_p, tp, tm_p, nf)
    ys = _moe_out(_routed_experts([rs], lw), 0, rs[1], rs[4], lw['sw13'], lw['sw2'], rs[0], mod_s, ts, tm_s, nf)

    y_prompt = yp.reshape(bp, tp, D_MODEL)
    y_sample = ys.reshape(bs, ts, D_MODEL)
    new_c = cc.reshape(bp, 1, 2, N_HEADS, HEAD, HEAD)
    new_n = nc_.reshape(bp, 1, 2, N_HEADS, HEAD)
    new_m = mc[:, :, 0].reshape(bp, 1, 2, N_HEADS)
    new_h = hc.reshape(bp, 1, 2, D_RG)
    return (y_prompt, y_sample, new_c, new_n, new_m, new_h)
```

```python
import functools

import jax
import jax.numpy as jnp
from jax import lax
from jax.experimental import pallas as pl
from jax.experimental.pallas import tpu as pltpu
from jax.experimental.pallas import tpu_sc as plsc

F32 = jnp.float32
BF16 = jnp.bfloat16
I32 = jnp.int32
HIGHEST = lax.Precision.HIGHEST

D_MODEL = 1024
N_MOD = 6
D_ML = 512
N_HEADS = 4
HEAD = 128
D_RG = 512
N_RG_BLOCKS = 8
RG_BLOCK = 64
RG_C = 8.0
N_GATE = 16
N_EXPERTS = 64
N_GROUPS = 8
GROUP = 8
TOPK_GROUPS = 4
TOP_K = 8
D_EXPERT = 256
ROUTED_SCALE = 2.5
EPS = 1e-6
GRID_W = 64
POS_BASE = 10000.0

RG_SEG_PAD = 8
SC_WINDOW = 128
D_PACK = D_MODEL // 2
EXPERT_ROWS = 256
EXPERT_BLOCKS_PER_STEP = 8
EXPERT_RING = 8

SUBLANES = 8
LANES = 128
VMEM_LIMIT = 48 * 1024 * 1024
VMEM_LIMIT_EXPERTS = 56 * 1024 * 1024


def _params(*sem, vmem=VMEM_LIMIT):
    return pltpu.CompilerParams(dimension_semantics=sem, vmem_limit_bytes=vmem)


def _tiles(t, per_sequence_mod):
    cap = t if per_sequence_mod else 1 << 30
    return dict(
        tok=min(512, cap),
        chunk=min(256, t),
        scan=min(512, t),
    )


def _silu(x):
    return x * jax.nn.sigmoid(x)


def _softplus(x):
    return jnp.maximum(x, 0.0) + jnp.log1p(jnp.exp(-jnp.abs(x)))


def _rms(x, g):
    return x * lax.rsqrt(jnp.mean(x * x, axis=-1, keepdims=True) + EPS) * g


def _bf16_pieces(x):
    hi = x.astype(BF16)
    r = x - hi.astype(F32)
    mid = r.astype(BF16)
    return hi, mid, (r - mid.astype(F32)).astype(BF16)


def _pack_bf16_pairs(x):
    w = x.shape[1] // 2
    hi = lax.bitcast_convert_type(x[:, :w].astype(BF16).astype(F32), I32)
    lo = lax.bitcast_convert_type(x[:, w:].astype(BF16).astype(F32), I32)
    return hi | lax.shift_right_logical(lo, jnp.full(lo.shape, 16, I32))


def _unpack_bf16_pairs(p):
    hi = lax.bitcast_convert_type(p & jnp.int32(-65536), F32)
    lo = lax.bitcast_convert_type(lax.shift_left(p, jnp.full(p.shape, 16, I32)), F32)
    return hi, lo


def _ada_kernel(c_ref, w_ref, b_ref, o_ref):
    s = _silu(c_ref[...])
    o_ref[...] = jnp.dot(s, w_ref[...], precision=HIGHEST, preferred_element_type=F32) + b_ref[...]


def _ada(cvecs, w_ada, b_ada):
    n_out = w_ada.shape[1]
    tn = 1536
    return pl.pallas_call(
        _ada_kernel,
        grid=(n_out // tn,),
        in_specs=[
            pl.BlockSpec((SUBLANES, D_MODEL), lambda j: (0, 0)),
            pl.BlockSpec((D_MODEL, tn), lambda j: (0, j)),
            pl.BlockSpec((1, tn), lambda j: (0, j)),
        ],
        out_specs=pl.BlockSpec((SUBLANES, tn), lambda j: (0, j)),
        out_shape=jax.ShapeDtypeStruct((SUBLANES, n_out), F32),
        name="ada",
        compiler_params=_params("arbitrary"),
    )(cvecs, w_ada, b_ada.reshape(1, n_out))


def _pos_tile(rt_ref, ct_ref):
    left = jnp.concatenate([jnp.broadcast_to(rt_ref[r:r + 1, :], (GRID_W, rt_ref.shape[1]))
                            for r in range(rt_ref.shape[0])], axis=0)
    right = jnp.concatenate([ct_ref[...]] * rt_ref.shape[0], axis=0)
    return jnp.concatenate([left, right], axis=1)


def _in_proj_kernel(x_ref, rt_ref, ct_ref, mod_ref, n1_ref, wq_ref, wr_ref, wgh_ref, gb_ref,
                    qt_ref, k_ref, kt_ref, v_ref, vt_ref, ot_ref, xr_ref, gr_ref, g_ref, gt_ref):
    x = x_ref[...] + _pos_tile(rt_ref, ct_ref)
    hn = _rms(x, n1_ref[...]) * (1.0 + mod_ref[0, 1:2, :]) + mod_ref[0, 0:1, :]
    hb = hn.astype(BF16)
    z = jnp.dot(hb, wq_ref[...], preferred_element_type=F32)
    k = z[:, D_ML:2 * D_ML] * (HEAD ** -0.5)
    v = z[:, 2 * D_ML:3 * D_ML]
    qt_ref[...] = z[:, 0:D_ML].T.astype(BF16)
    k_ref[...] = k.astype(BF16)
    kt_ref[...] = k.T.astype(BF16)
    v_ref[...] = v.astype(BF16)
    vt_ref[...] = v.T.astype(BF16)
    ot_ref[...] = z[:, 3 * D_ML:4 * D_ML].T
    zr = jnp.dot(hb, wr_ref[...], preferred_element_type=F32)
    xr_ref[...] = zr[:, 0:D_RG]
    gr_ref[...] = zr[:, D_RG:2 * D_RG]
    zg = zr[:, 2 * D_RG:2 * D_RG + LANES]
    h_lo = (hn - hb.astype(F32)).astype(BF16)
    g = (zg + pltpu.roll(zg, LANES - N_GATE, 1)
         + jnp.dot(h_lo, wgh_ref[...], preferred_element_type=F32) + gb_ref[...])
    col = lax.broadcasted_iota(I32, g.shape, 1)
    g = jnp.where((col & 4) != 0, -_softplus(-g), g)
    g_ref[...] = g[:, 0:N_GATE]
    gt_ref[...] = g.T[0:N_GATE, :]


def _mod_index(bm, tm, t):
    if bm > 1:
        return lambda i: ((i * tm) // t, 0, 0)
    return lambda i: (0, 0, 0)


def _pos_specs(rtab, ctab, tm):
    rows = tm // GRID_W
    period = rtab.shape[0] // rows
    return [pl.BlockSpec((rows, D_MODEL // 2), lambda i: (i % period, 0)),
            pl.BlockSpec((GRID_W, D_MODEL // 2), lambda i: (0, 0))]


def _in_proj(x2d, rtab, ctab, mod, t, tm, norm1, wq, wr, wgh, gbias):
    n = x2d.shape[0]
    tok = lambda i: (i, 0)
    tok_t = lambda i: (0, i)
    const = lambda i: (0, 0)
    f = lambda w: jax.ShapeDtypeStruct((n, w), F32)
    row16 = jax.ShapeDtypeStruct((n, D_ML), BF16)
    col16 = jax.ShapeDtypeStruct((D_ML, n), BF16)
    return pl.pallas_call(
        _in_proj_kernel,
        grid=(n // tm,),
        in_specs=[pl.BlockSpec((tm, D_MODEL), tok)] + _pos_specs(rtab, ctab, tm) + [
            pl.BlockSpec((1, N_MOD, D_MODEL), _mod_index(mod.shape[0], tm, t)),
            pl.BlockSpec((1, D_MODEL), const),
            pl.BlockSpec((D_MODEL, 4 * D_ML), const),
            pl.BlockSpec((D_MODEL, 2 * D_RG + LANES), const),
            pl.BlockSpec((D_MODEL, LANES), const),
            pl.BlockSpec((1, LANES), const),
        ],
        out_specs=[pl.BlockSpec((D_ML, tm), tok_t), pl.BlockSpec((tm, D_ML), tok), pl.BlockSpec((D_ML, tm), tok_t),
                   pl.BlockSpec((tm, D_ML), tok), pl.BlockSpec((D_ML, tm), tok_t), pl.BlockSpec((D_ML, tm), tok_t),
                   pl.BlockSpec((tm, D_RG), tok), pl.BlockSpec((tm, D_RG), tok),
                   pl.BlockSpec((tm, N_GATE), tok), pl.BlockSpec((N_GATE, tm), tok_t)],
        out_shape=[col16, row16, col16, row16, col16, jax.ShapeDtypeStruct((D_ML, n), F32),
                   f(D_RG), f(D_RG), f(N_GATE), jax.ShapeDtypeStruct((N_GATE, n), F32)],
        name="in_proj",
        compiler_params=_params("arbitrary"),
    )(x2d, rtab, ctab, mod, norm1, wq, wr, wgh, gbias)


def _mlstm_kernel(chunk, nc, emit_state, *refs):
    (qtf, kf, ktf, vf, vtf, gf, gtf, qtb, kb, ktb, vb, vtb, gb, gtb, c0_ref, n0_ref, m0_ref) = refs[:17]
    if emit_state:
        hf_ref, hb_ref, c_out, n_out, m_out, c_sc, n_sc, m_sc = refs[17:]
    else:
        hf_ref, hb_ref, c_sc, n_sc, m_sc = refs[17:]
    i = pl.program_id(1)

    @pl.when(i == 0)
    def _():
        c_sc[...] = c0_ref[0]
        n_sc[...] = n0_ref[0]
        m_sc[...] = m0_ref[0]

    key = lax.broadcasted_iota(I32, (chunk, chunk), 0)
    qry = lax.broadcasted_iota(I32, (chunk, chunk), 1)
    hd = []
    for d, (qt_ref, k_ref, kt_ref, v_ref, vt_ref, g_ref, gt_ref, h_ref) in enumerate(
            ((qtf, kf, ktf, vf, vtf, gf, gtf, hf_ref), (qtb, kb, ktb, vb, vtb, gb, gtb, hb_ref))):
        tri = (key <= qry) if d == 0 else (key >= qry)
        tri_t = (qry <= key) if d == 0 else (qry >= key)
        g = g_ref[...]
        gt = gt_ref[...]
        rows3 = jnp.dot(jnp.concatenate(_bf16_pieces(gt), axis=0), tri.astype(BF16), preferred_element_type=F32)
        brow = rows3[0:N_GATE] + rows3[N_GATE:2 * N_GATE] + rows3[2 * N_GATE:3 * N_GATE]
        tri_t16 = tri_t.astype(BF16)
        bcol = sum(jnp.dot(tri_t16, piece, preferred_element_type=F32) for piece in _bf16_pieces(g))
        blast = bcol[chunk - 1:chunk, :] if d == 0 else bcol[0:1, :]
        for h in range(N_HEADS):
            ci = d * 8 + h
            cf = d * 8 + 4 + h
            j = d * N_HEADS + h
            sl = slice(h * HEAD, (h + 1) * HEAD)
            hd.append(dict(
                j=j, sl=sl, tri=tri, h_ref=h_ref, qt=qt_ref[sl, :], k=k_ref[:, sl], kt=kt_ref[sl, :],
                v=v_ref[:, sl], vt=vt_ref[sl, :], b_row=brow[cf:cf + 1, :],
                gate_col=g[:, ci:ci + 1] - bcol[:, cf:cf + 1], gate_row=gt[ci:ci + 1, :] - brow[cf:cf + 1, :],
                b_last=blast[:, cf:cf + 1], m_prev=m_sc[j:j + 1, 0:1], c_prev=c_sc[j], n_prev=n_sc[j:j + 1, :]))
    for x in hd:
        top = jnp.max(jnp.where(x['tri'], x['gate_col'], -jnp.inf), axis=0, keepdims=True)
        mx = jnp.maximum(x['m_prev'], top)
        x['dm'] = jnp.exp(jnp.where(x['tri'], x['gate_col'] - mx, -jnp.inf))
        x['w_inter'] = jnp.exp(x['m_prev'] - mx)
        x['floor'] = jnp.exp(-(x['b_row'] + mx))
    for x in hd:
        x['st'] = jnp.dot(x['k'], x['qt'], preferred_element_type=F32) * x['dm']
    for x in hd:
        inter = lax.dot_general(x['c_prev'].astype(BF16), x['qt'], (((0,), (0,)), ((), ())),
                                preferred_element_type=F32)
        num = x['w_inter'] * inter + jnp.dot(x['vt'], x['st'].astype(BF16), preferred_element_type=F32)
        qn = jnp.dot(jnp.broadcast_to(x['n_prev'], (SUBLANES, HEAD)).astype(BF16), x['qt'],
                     preferred_element_type=F32)[0:1, :]
        den = x['w_inter'] * qn + jnp.sum(x['st'], axis=0, keepdims=True)
        x['h_ref'][x['sl'], :] = num / jnp.maximum(jnp.abs(den), x['floor'])
    for x in hd:
        j = x['j']
        log_w = x['b_last'] + x['gate_row']
        m_new = jnp.maximum(x['b_last'] + x['m_prev'], jnp.max(log_w, axis=1, keepdims=True))
        decay = jnp.exp(x['b_last'] + x['m_prev'] - m_new)
        w_row = jnp.exp(log_w - m_new)
        kwt = (x['kt'].astype(F32) * w_row).astype(BF16)
        c_sc[j] = decay * x['c_prev'] + jnp.dot(kwt, x['v'], preferred_element_type=F32)
        n_sc[j:j + 1, :] = decay * x['n_prev'] + jnp.dot(
            jnp.broadcast_to(w_row, (SUBLANES, chunk)).astype(BF16), x['k'], preferred_element_type=F32)[0:1, :]
        m_sc[j:j + 1, :] = jnp.broadcast_to(m_new, (1, HEAD))

    if emit_state:
        @pl.when(i == nc - 1)
        def _():
            c_out[0] = c_sc[...]
            n_out[0] = n_sc[...]
            m_out[0] = m_sc[...]


def _state_index(bm):
    if bm > 1:
        return lambda b, i: (b,) + (0,) * 3, lambda b, i: (b, 0, 0)
    return lambda b, i: (0,) * 4, lambda b, i: (0, 0, 0)


def _mlstm(qt, k, kt, v, vt, g, gt, c0, n0, m0, b, t, chunk, emit_state):
    nc = t // chunk
    n = b * t
    nd = 2 * N_HEADS
    fwd = lambda bi, i: (bi * nc + i, 0)
    bwd = lambda bi, i: (bi * nc + nc - 1 - i, 0)
    fwd_t = lambda bi, i: (0, bi * nc + i)
    bwd_t = lambda bi, i: (0, bi * nc + nc - 1 - i)
    c_idx, n_idx = _state_index(c0.shape[0])
    rows = lambda m: pl.BlockSpec((chunk, D_ML), m)
    cols = lambda m: pl.BlockSpec((D_ML, chunk), m)
    one_dir = lambda m, mt: [cols(mt), rows(m), cols(mt), rows(m), cols(mt),
                             pl.BlockSpec((chunk, N_GATE), m), pl.BlockSpec((N_GATE, chunk), mt)]
    in_specs = (one_dir(fwd, fwd_t) + one_dir(bwd, bwd_t)
                + [pl.BlockSpec((1, nd, HEAD, HEAD), c_idx),
                   pl.BlockSpec((1, nd, HEAD), n_idx), pl.BlockSpec((1, nd, HEAD), n_idx)])
    out_specs = [cols(fwd_t), cols(bwd_t)]
    out_shape = [jax.ShapeDtypeStruct((D_ML, n), F32)] * 2
    if emit_state:
        out_specs += [pl.BlockSpec((1, nd, HEAD, HEAD), lambda bi, i: (bi, 0, 0, 0)),
                      pl.BlockSpec((1, nd, HEAD), lambda bi, i: (bi, 0, 0)),
                      pl.BlockSpec((1, nd, HEAD), lambda bi, i: (bi, 0, 0))]
        out_shape += [jax.ShapeDtypeStruct((b, nd, HEAD, HEAD), F32),
                      jax.ShapeDtypeStruct((b, nd, HEAD), F32),
                      jax.ShapeDtypeStruct((b, nd, HEAD), F32)]
    return pl.pallas_call(
        functools.partial(_mlstm_kernel, chunk, nc, emit_state),
        grid=(b, nc),
        in_specs=in_specs,
        out_specs=out_specs,
        out_shape=out_shape,
        scratch_shapes=[pltpu.VMEM((nd, HEAD, HEAD), F32), pltpu.VMEM((nd, HEAD), F32),
                        pltpu.VMEM((nd, HEAD), F32)],
        name="mlstm",
        compiler_params=_params("arbitrary", "arbitrary"),
    )(qt, k, kt, v, vt, g, gt, qt, k, kt, v, vt, g, gt, c0, n0, m0)


def _neg_expm1(x, u):
    near = jnp.where(u == 1.0, x, (u - 1.0) * x / jnp.log(u))
    return -jnp.where(x < -0.5, u - 1.0, near)


def _rglru_kernel(tb, nb, emit_state, *refs):
    (xf, xf_prev, xf_next, xb, xb_prev, xb_next, h0_ref, w_ref, bias_ref, lam_ref, cw_ref, cb_ref) = refs[:12]
    n_out = 3 if emit_state else 2
    hf_ref, hb_ref = refs[12:14]
    hfin_ref = refs[14] if emit_state else None
    carry, af_sc, uf_sc, ab_sc, ub_sc, hf_sc, pf_sc, hb_sc, pb_sc = refs[12 + n_out:]
    i = pl.program_id(1)

    @pl.when(i == 0)
    def _():
        carry[...] = h0_ref[0]

    row8 = lax.broadcasted_iota(I32, (SUBLANES, D_RG), 0)
    cw = cw_ref[...]
    log_a_per_r = -RG_C * _softplus(-lam_ref[...])

    def taps(xm2, xm1, x0, xp1):
        return cb_ref[...] + xm2 * cw[0:1, :] + xm1 * cw[1:2, :] + x0 * cw[2:3, :] + xp1 * cw[3:4, :]

    def conv(main_ref, prev_ref, next_ref, first, last):
        main = main_ref[...]
        prev = jnp.where(first, 0.0, prev_ref[...])
        nxt = jnp.where(last, 0.0, next_ref[...])
        body = taps(pltpu.roll(main, 2, 0), pltpu.roll(main, 1, 0), main, pltpu.roll(main, tb - 1, 0))
        e = SUBLANES
        head, tail = main[0:e, :], main[tb - e:tb, :]
        before_tail = main[tb - 2 * e:tb - e, :]
        fix_head = taps(
            jnp.where(row8 == 0, prev[6:7, :], jnp.where(row8 == 1, prev[7:8, :], pltpu.roll(head, 2, 0))),
            jnp.where(row8 == 0, prev[7:8, :], pltpu.roll(head, 1, 0)), head,
            jnp.where(row8 == e - 1, main[e:e + 1, :], pltpu.roll(head, e - 1, 0)))
        fix_tail = taps(
            jnp.where(row8 == 0, before_tail[6:7, :],
                      jnp.where(row8 == 1, before_tail[7:8, :], pltpu.roll(tail, 2, 0))),
            jnp.where(row8 == 0, before_tail[7:8, :], pltpu.roll(tail, 1, 0)), tail,
            jnp.where(row8 == e - 1, nxt[0:1, :], pltpu.roll(tail, e - 1, 0)))
        return jnp.concatenate([fix_head, body[e:tb - e, :], fix_tail], axis=0)

    def recurrence_terms(xc, d):
        z = jnp.dot(xc.astype(BF16), w_ref[:, d * 2 * D_RG:(d + 1) * 2 * D_RG],
                    preferred_element_type=F32) + bias_ref[:, d * 2 * D_RG:(d + 1) * 2 * D_RG]
        r = jax.nn.sigmoid(z[:, 0:D_RG])
        ig = jax.nn.sigmoid(z[:, D_RG:2 * D_RG])
        a = jnp.exp(r * log_a_per_r[d:d + 1, :])
        u = jnp.sqrt(_neg_expm1(r * (2.0 * log_a_per_r[d:d + 1, :]), a * a)) * (ig * xc)
        return a, u

    a_f, u_f = recurrence_terms(conv(xf, xf_prev, xf_next, i == 0, i == nb - 1), 0)
    a_b, u_b = recurrence_terms(conv(xb, xb_prev, xb_next, i == nb - 1, i == 0), 1)
    seg = tb // SUBLANES
    pitch = seg + RG_SEG_PAD
    ncol = D_RG // LANES
    for lc in range(ncol):
        lanes = slice(lc * LANES, (lc + 1) * LANES)
        for s in range(SUBLANES):
            src = slice(s * seg, (s + 1) * seg)
            dst = slice(s * pitch, s * pitch + seg)
            af_sc[lc, dst, :], uf_sc[lc, dst, :] = a_f[src, lanes], u_f[src, lanes]
            ab_sc[lc, dst, :], ub_sc[lc, dst, :] = a_b[src, lanes], u_b[src, lanes]
    slab = lambda k: (slice(None), pl.ds(k, SUBLANES, stride=pitch), slice(None))
    hf = jnp.zeros((ncol, SUBLANES, LANES), F32)
    hb = jnp.zeros((ncol, SUBLANES, LANES), F32)
    pf = jnp.ones((ncol, SUBLANES, LANES), F32)
    pb = jnp.ones((ncol, SUBLANES, LANES), F32)
    for k in range(seg):
        kb = seg - 1 - k
        ak = af_sc[slab(k)]
        hf = ak * hf + uf_sc[slab(k)]
        pf = pf * ak
        hf_sc[slab(k)] = hf
        pf_sc[slab(k)] = pf
        ak = ab_sc[slab(kb)]
        hb = ak * hb + ub_sc[slab(kb)]
        pb = pb * ak
        hb_sc[slab(kb)] = hb
        pb_sc[slab(kb)] = pb
    for lc in range(ncol):
        lanes = slice(lc * LANES, (lc + 1) * LANES)
        c = carry[0:1, lanes]
        cin_f = []
        for s in range(SUBLANES):
            cin_f.append(c)
            c = pf[lc, s:s + 1, :] * c + hf[lc, s:s + 1, :]
        carry[0:1, lanes] = c
        c = carry[1:2, lanes]
        cin_b = [None] * SUBLANES
        for s in reversed(range(SUBLANES)):
            cin_b[s] = c
            c = pb[lc, s:s + 1, :] * c + hb[lc, s:s + 1, :]
        carry[1:2, lanes] = c
        for s in range(SUBLANES):
            rows = slice(s * seg, (s + 1) * seg)
            src = slice(s * pitch, s * pitch + seg)
            hf_ref[rows, lanes] = hf_sc[lc, src, :] + pf_sc[lc, src, :] * cin_f[s]
            hb_ref[rows, lanes] = hb_sc[lc, src, :] + pb_sc[lc, src, :] * cin_b[s]

    if emit_state:
        @pl.when(i == nb - 1)
        def _():
            hfin_ref[0] = carry[...]


def _rglru(xr, h0, wbd, bias, lam, cw, cb, b, t, tb, emit_state):
    nb = t // tb
    n = b * t
    r8 = tb // SUBLANES
    last8 = n // SUBLANES - 1
    fwd = lambda bi, i: (bi * nb + i, 0)
    bwd = lambda bi, i: (bi * nb + nb - 1 - i, 0)
    fwd_prev = lambda bi, i: (jnp.maximum((bi * nb + i) * r8 - 1, 0), 0)
    fwd_next = lambda bi, i: (jnp.minimum((bi * nb + i + 1) * r8, last8), 0)
    bwd_prev = lambda bi, i: (jnp.maximum((bi * nb + nb - 1 - i) * r8 - 1, 0), 0)
    bwd_next = lambda bi, i: (jnp.minimum((bi * nb + nb - i) * r8, last8), 0)
    const = lambda bi, i: (0, 0)
    h_idx = (lambda bi, i: (bi, 0, 0)) if h0.shape[0] > 1 else (lambda bi, i: (0, 0, 0))
    halo = lambda m: pl.BlockSpec((SUBLANES, D_RG), m)
    in_specs = [pl.BlockSpec((tb, D_RG), fwd), halo(fwd_prev), halo(fwd_next),
                pl.BlockSpec((tb, D_RG), bwd), halo(bwd_prev), halo(bwd_next),
                pl.BlockSpec((1, 2, D_RG), h_idx),
                pl.BlockSpec((D_RG, 4 * D_RG), const), pl.BlockSpec((1, 4 * D_RG), const),
                pl.BlockSpec((2, D_RG), const), pl.BlockSpec((4, D_RG), const), pl.BlockSpec((1, D_RG), const)]
    out_specs = [pl.BlockSpec((tb, D_RG), fwd), pl.BlockSpec((tb, D_RG), bwd)]
    out_shape = [jax.ShapeDtypeStruct((n, D_RG), F32)] * 2
    if emit_state:
        out_specs.append(pl.BlockSpec((1, 2, D_RG), lambda bi, i: (bi, 0, 0)))
        out_shape.append(jax.ShapeDtypeStruct((b, 2, D_RG), F32))
    return pl.pallas_call(
        functools.partial(_rglru_kernel, tb, nb, emit_state),
        grid=(b, nb),
        in_specs=in_specs,
        out_specs=out_specs,
        out_shape=out_shape,
        scratch_shapes=[pltpu.VMEM((2, D_RG), F32)]
        + [pltpu.VMEM((D_RG // LANES, tb + SUBLANES * RG_SEG_PAD, LANES), F32)] * 8,
        name="rglru",
        compiler_params=_params("arbitrary", "arbitrary"),
    )(xr, xr, xr, xr, xr, xr, h0, wbd, bias, lam, cw, cb)


def _route(s, sb):
    tm = s.shape[1]
    neg = -jnp.inf
    sub = lax.broadcasted_iota(I32, (GROUP, tm), 0)
    blocks = [sb[gi * GROUP:(gi + 1) * GROUP, :] for gi in range(N_GROUPS)]
    gscore = []
    for blk in blocks:
        m1 = jnp.max(blk, axis=0, keepdims=True)
        first = jnp.min(jnp.where(blk == m1, sub, GROUP), axis=0, keepdims=True)
        m2 = jnp.max(jnp.where(sub == first, neg, blk), axis=0, keepdims=True)
        gscore.append(m1 + m2)
    masked = []
    for gi in range(N_GROUPS):
        rank = jnp.zeros((1, tm), F32)
        for gj in range(N_GROUPS):
            if gj == gi:
                continue
            ahead = (gscore[gj] >= gscore[gi]) if gj < gi else (gscore[gj] > gscore[gi])
            rank = rank + jnp.where(ahead, 1.0, 0.0)
        masked.append(jnp.where(rank < TOPK_GROUPS, blocks[gi], neg))
    v = jnp.concatenate(masked, axis=0)
    eid = lax.broadcasted_iota(I32, (N_EXPERTS, tm), 0)
    sel = jnp.zeros((N_EXPERTS, tm), F32)
    picks = []
    for _ in range(TOP_K):
        mx = jnp.max(v, axis=0, keepdims=True)
        idx = jnp.min(jnp.where(v == mx, eid, N_EXPERTS), axis=0, keepdims=True)
        pick = eid == idx
        picks.append(pick)
        sel = jnp.where(pick, 1.0, sel)
        v = jnp.where(pick, neg, v)
    ws = s * sel
    return ws / jnp.sum(ws, axis=0, keepdims=True) * ROUTED_SCALE, sel, picks


def _mix_out_kernel(hmf_ref, hmb_ref, ot_ref, hrf_ref, hrb_ref, gr_ref, x_ref, rt_ref, ct_ref, mod_ref,
                    mln_ref, rgn_ref, wo_ml_ref, wo_rg_ref, n2_ref, rwt_ref, rb_ref, cnt0_ref,
                    x1_ref, hn2p_ref, ek_ref, pk_ref, wtok_ref, cnt_ref, cnt_sc):
    i = pl.program_id(0)
    tm = x_ref.shape[0]

    @pl.when(i == 0)
    def _():
        cnt_sc[...] = cnt0_ref[...].astype(F32)

    hm = hmf_ref[...] + hmb_ref[...]
    heads = []
    for h in range(N_HEADS):
        seg = hm[h * HEAD:(h + 1) * HEAD, :]
        heads.append(seg * lax.rsqrt(jnp.mean(seg * seg, axis=0, keepdims=True) + EPS))
    y_ml_t = jnp.concatenate(heads, axis=0) * mln_ref[...] * jax.nn.sigmoid(ot_ref[...])
    y_rg = _rms(hrf_ref[...] + hrb_ref[...], rgn_ref[...]) * jax.nn.gelu(gr_ref[...])
    mix = (lax.dot_general(y_ml_t.astype(BF16), wo_ml_ref[...], (((0,), (0,)), ((), ())),
                           preferred_element_type=F32)
           + jnp.dot(y_rg.astype(BF16), wo_rg_ref[...], preferred_element_type=F32))
    x1 = x_ref[...] + _pos_tile(rt_ref, ct_ref) + mod_ref[0, 2:3, :] * mix
    x1_ref[...] = x1
    hn2 = _rms(x1, n2_ref[...]) * (1.0 + mod_ref[0, 4:5, :]) + mod_ref[0, 3:4, :]
    hn2p_ref[...] = _pack_bf16_pairs(hn2)
    hb = hn2.astype(BF16)
    h_lo = (hn2 - hb.astype(F32)).astype(BF16)
    nt = (((1,), (1,)), ((), ()))
    two = lax.dot_general(rwt_ref[...], hb, nt, preferred_element_type=F32)
    logits_t = (two[0:N_EXPERTS, :] + two[N_EXPERTS:2 * N_EXPERTS, :]
                + lax.dot_general(rwt_ref[0:N_EXPERTS, :], h_lo, nt, preferred_element_type=F32))
    s = jax.nn.sigmoid(logits_t)
    wt, sel, picks = _route(s, s + rb_ref[...])

    earlier = (lax.broadcasted_iota(I32, (tm, tm), 0) < lax.broadcasted_iota(I32, (tm, tm), 1))
    prefix = jnp.dot(sel.astype(BF16), earlier.astype(BF16), preferred_element_type=F32)
    pos_all = cnt_sc[:, 0:1] + prefix
    eid = lax.broadcasted_iota(I32, (N_EXPERTS, tm), 0)
    eid_f = eid.astype(F32)
    row8 = lax.broadcasted_iota(I32, (TOP_K, tm), 0)
    ek = jnp.zeros((TOP_K, tm), F32)
    pk = jnp.zeros((TOP_K, tm), F32)
    wk = jnp.zeros((N_EXPERTS, tm), F32)
    for k, pick in enumerate(picks):
        take = lambda a: jnp.sum(jnp.where(pick, a, 0.0), axis=0, keepdims=True)
        ek = jnp.where(row8 == k, take(eid_f), ek)
        pk = jnp.where(row8 == k, take(pos_all), pk)
        wk = jnp.where(eid == k, take(wt), wk)
    ek_ref[...] = ek.astype(I32)
    pk_ref[...] = pk.astype(I32)
    wtok_ref[...] = wk.T
    cnt_sc[...] += jnp.broadcast_to(jnp.sum(sel, axis=1, keepdims=True), cnt_sc.shape)

    @pl.when(i == pl.num_programs(0) - 1)
    def _():
        cnt_ref[...] = cnt_sc[...].astype(I32)


def _mix_out(hmf, hmb, ot, hrf, hrb, gr, x2d, rtab, ctab, mod, t, tm, mln, rgn, wo_ml, wo_rg, norm2, rwt, rbias,
             cnt0):
    n = x2d.shape[0]
    tok = lambda i: (i, 0)
    tok_t = lambda i: (0, i)
    const = lambda i: (0, 0)
    return pl.pallas_call(
        _mix_out_kernel,
        grid=(n // tm,),
        in_specs=[pl.BlockSpec((D_ML, tm), tok_t)] * 3 + [pl.BlockSpec((tm, D_RG), tok)] * 3 + [
            pl.BlockSpec((tm, D_MODEL), tok)] + _pos_specs(rtab, ctab, tm) + [
            pl.BlockSpec((1, N_MOD, D_MODEL), _mod_index(mod.shape[0], tm, t)),
            pl.BlockSpec((D_ML, 1), const), pl.BlockSpec((1, D_RG), const),
            pl.BlockSpec((D_ML, D_MODEL), const), pl.BlockSpec((D_RG, D_MODEL), const),
            pl.BlockSpec((1, D_MODEL), const),
            pl.BlockSpec((2 * N_EXPERTS, D_MODEL), const), pl.BlockSpec((N_EXPERTS, 1), const),
            pl.BlockSpec((N_EXPERTS, LANES), const),
        ],
        out_specs=[pl.BlockSpec((tm, D_MODEL), tok), pl.BlockSpec((tm, D_PACK), tok),
                   pl.BlockSpec((TOP_K, tm), tok_t), pl.BlockSpec((TOP_K, tm), tok_t),
                   pl.BlockSpec((tm, N_EXPERTS), tok), pl.BlockSpec((N_EXPERTS, LANES), const)],
        out_shape=[jax.ShapeDtypeStruct((n, D_MODEL), F32), jax.ShapeDtypeStruct((n, D_PACK), I32),
                   jax.ShapeDtypeStruct((TOP_K, n), I32), jax.ShapeDtypeStruct((TOP_K, n), I32),
                   jax.ShapeDtypeStruct((n, N_EXPERTS), F32), jax.ShapeDtypeStruct((N_EXPERTS, LANES), I32)],
        scratch_shapes=[pltpu.VMEM((N_EXPERTS, LANES), F32)],
        name="mix_out",
        compiler_params=_params("arbitrary"),
    )(hmf, hmb, ot, hrf, hrb, gr, x2d, rtab, ctab, mod, mln, rgn, wo_ml, wo_rg, norm2, rwt, rbias, cnt0)


def _sc_mesh():
    return plsc.VectorSubcoreMesh(core_axis_name="core", subcore_axis_name="subcore")


def _sc_worker():
    info = plsc.get_sparse_core_info()
    return lax.axis_index("subcore") * info.num_cores + lax.axis_index("core"), info.num_cores * info.num_subcores


def _sc_dispatch(xp, dest3, n_slots):
    n, w = xp.shape
    nwin = n // SC_WINDOW

    @pl.kernel(out_type=jax.ShapeDtypeStruct((n_slots, w), xp.dtype), mesh=_sc_mesh(),
               scratch_types=[pltpu.VMEM((SC_WINDOW, w), xp.dtype), pltpu.VMEM((TOP_K, SC_WINDOW), I32)],
               name="sc_dispatch")
    def k(x_hbm, i_hbm, o_hbm, x_v, i_v):
        wid, nworkers = _sc_worker()
        per = nwin // nworkers

        @pl.loop(0, per)
        def _(s):
            win = wid * per + s
            pltpu.sync_copy(x_hbm.at[pl.ds(win * SC_WINDOW, SC_WINDOW)], x_v)
            pltpu.sync_copy(i_hbm.at[win], i_v)
            for j in range(TOP_K):
                pltpu.sync_copy(x_v, o_hbm.at[i_v.at[j]])

    return k(xp, dest3)


def _sc_combine_gather(ys, dest3):
    nwin = dest3.shape[0]
    w = ys.shape[1]

    @pl.kernel(out_type=jax.ShapeDtypeStruct((nwin, TOP_K, SC_WINDOW, w), ys.dtype), mesh=_sc_mesh(),
               scratch_types=[pltpu.VMEM((SC_WINDOW, w), ys.dtype), pltpu.VMEM((TOP_K, SC_WINDOW), I32)],
               name="sc_combine")
    def k(y_hbm, i_hbm, o_hbm, y_v, i_v):
        wid, nworkers = _sc_worker()
        per = nwin // nworkers

        @pl.loop(0, per)
        def _(s):
            win = wid * per + s
            pltpu.sync_copy(i_hbm.at[win], i_v)
            for j in range(TOP_K):
                pltpu.sync_copy(y_hbm.at[i_v.at[j]], y_v)
                pltpu.sync_copy(y_v, o_hbm.at[win, j])

    return k(ys, dest3)


def _swiglu(x, w13):
    h = jnp.dot(x, w13, preferred_element_type=F32)
    return _silu(h[:, 0:D_EXPERT]) * h[:, D_EXPERT:2 * D_EXPERT]


def _unpack_rows_bf16(p):
    hi, lo = _unpack_bf16_pairs(p)
    return jnp.concatenate([hi.astype(BF16), lo.astype(BF16)], axis=1)


def _expert_kernel(rows, rank_ref, expert_ref, nu_ref, nr_ref, x_ref, w1_hbm, w3_hbm, w2_hbm, y_ref,
                   w1buf, w3buf, w2buf, sem, progress):
    b = pl.program_id(0)
    g = EXPERT_BLOCKS_PER_STEP
    n_ranks = nr_ref[0]

    @pl.when(b == 0)
    def _():
        progress[0] = 0
        progress[1] = 0

    def copies(r):
        slot = lax.rem(r, EXPERT_RING)
        e = expert_ref[r]
        return (pltpu.make_async_copy(w1_hbm.at[e], w1buf.at[slot], sem.at[slot, 0]),
                pltpu.make_async_copy(w3_hbm.at[e], w3buf.at[slot], sem.at[slot, 1]),
                pltpu.make_async_copy(w2_hbm.at[e], w2buf.at[slot], sem.at[slot, 2]))

    def start(r, carry):
        for cp in copies(r):
            cp.start()
        return carry

    def wait(r, carry):
        for cp in copies(r):
            cp.wait()
        return carry

    first = rank_ref[b * g]
    last = rank_ref[b * g + g - 1]
    started = jnp.minimum(first + EXPERT_RING, n_ranks)
    lax.fori_loop(progress[0], started, start, 0)
    progress[0] = jnp.maximum(progress[0], started)
    needed = jnp.where(b == pl.num_programs(0) - 1, progress[0], jnp.minimum(last + 1, n_ranks))
    lax.fori_loop(progress[1], needed, wait, 0)
    progress[1] = jnp.maximum(progress[1], needed)

    @pl.when(b * g < nu_ref[0])
    def _():
        for j in range(g):
            slot = lax.rem(rank_ref[b * g + j], EXPERT_RING)
            sl = slice(j * rows, (j + 1) * rows)
            x = _unpack_rows_bf16(x_ref[sl, :])
            h = (_silu(jnp.dot(x, w1buf[slot].astype(BF16), preferred_element_type=F32))
                 * jnp.dot(x, w3buf[slot].astype(BF16), preferred_element_type=F32))
            y_ref[sl, :] = _pack_bf16_pairs(
                jnp.dot(h.astype(BF16), w2buf[slot].astype(BF16), preferred_element_type=F32))


def _experts(xs, block_rank, rank_expert, n_used, n_ranks, w1, w3, w2, rows):
    g = EXPERT_BLOCKS_PER_STEP
    nb = xs.shape[0] // rows
    tok = lambda b, *_: (b, 0)
    return pl.pallas_call(
        functools.partial(_expert_kernel, rows),
        grid_spec=pltpu.PrefetchScalarGridSpec(
            num_scalar_prefetch=4,
            grid=(nb // g,),
            in_specs=[pl.BlockSpec((g * rows, D_PACK), tok)] + [pl.BlockSpec(memory_space=pl.ANY)] * 3,
            out_specs=pl.BlockSpec((g * rows, D_PACK), tok),
            scratch_shapes=[pltpu.VMEM((EXPERT_RING, D_MODEL, D_EXPERT), F32),
                            pltpu.VMEM((EXPERT_RING, D_MODEL, D_EXPERT), F32),
                            pltpu.VMEM((EXPERT_RING, D_EXPERT, D_MODEL), F32),
                            pltpu.SemaphoreType.DMA((EXPERT_RING, 3)), pltpu.SMEM((2,), I32)],
        ),
        out_shape=jax.ShapeDtypeStruct(xs.shape, I32),
        name="experts",
        compiler_params=_params("arbitrary", vmem=VMEM_LIMIT_EXPERTS),
    )(block_rank, rank_expert, n_used, n_ranks, xs, w1, w3, w2)


def _moe_out_kernel(yk_ref, hn2p_ref, wtok_ref, sw13_ref, sw2_ref, x1_ref, mod_ref, nf_ref, y_ref):
    shared = jnp.dot(_swiglu(_unpack_rows_bf16(hn2p_ref[...]), sw13_ref[...]).astype(BF16), sw2_ref[...],
                     preferred_element_type=F32)
    w = wtok_ref[...]
    parts = []
    for wi in range(yk_ref.shape[0]):
        rows = slice(wi * SC_WINDOW, (wi + 1) * SC_WINDOW)
        a_hi = shared[rows, 0:D_PACK]
        a_lo = shared[rows, D_PACK:D_MODEL]
        for k in range(TOP_K):
            y_hi, y_lo = _unpack_bf16_pairs(yk_ref[wi, k])
            wc = w[rows, k:k + 1]
            a_hi = a_hi + wc * y_hi
            a_lo = a_lo + wc * y_lo
        parts.append(jnp.concatenate([a_hi, a_lo], axis=1))
    x2 = x1_ref[...] + mod_ref[0, 5:6, :] * jnp.concatenate(parts, axis=0)
    y_ref[...] = _rms(x2, nf_ref[...])


def _moe_out(yk, tok0, hn2p, wtok, sw13, sw2, x1, mod, t, tm, norm_final):
    n = hn2p.shape[0]
    tok = lambda i: (i, 0)
    const = lambda i: (0, 0)
    blk0 = tok0 // tm
    return pl.pallas_call(
        _moe_out_kernel,
        grid=(n // tm,),
        in_specs=[
            pl.BlockSpec((tm // SC_WINDOW, TOP_K, SC_WINDOW, D_PACK), lambda i: (i + blk0, 0, 0, 0)),
            pl.BlockSpec((tm, D_PACK), tok),
            pl.BlockSpec((tm, N_EXPERTS), tok),
            pl.BlockSpec((D_MODEL, 2 * D_EXPERT), const),
            pl.BlockSpec((D_EXPERT, D_MODEL), const),
            pl.BlockSpec((tm, D_MODEL), tok),
            pl.BlockSpec((1, N_MOD, D_MODEL), _mod_index(mod.shape[0], tm, t)),
            pl.BlockSpec((1, D_MODEL), const),
        ],
        out_specs=pl.BlockSpec((tm, D_MODEL), tok),
        out_shape=jax.ShapeDtypeStruct((n, D_MODEL), F32),
        name="moe_out",
        compiler_params=_params("arbitrary"),
    )(yk, hn2p, wtok, sw13, sw2, x1, mod, norm_final)


def _dispatch_plan(cnt, ek, pk, rows):
    n = ek.shape[1]
    nb = n * TOP_K // rows + N_EXPERTS
    nblk = (cnt + rows - 1) // rows
    block_end = jnp.cumsum(nblk)
    experts = jnp.arange(N_EXPERTS, dtype=I32)
    first_row = jnp.sum(jnp.where(ek[:, :, None] == experts, (block_end - nblk) * rows, 0), axis=-1)
    dest3 = (first_row + pk).reshape(TOP_K, n // SC_WINDOW, SC_WINDOW).transpose(1, 0, 2)
    owns = nblk > 0
    n_ranks = jnp.sum(owns.astype(I32))
    blocks = jnp.arange(nb, dtype=I32)[:, None]
    block_rank = jnp.minimum(jnp.sum((owns & (block_end <= blocks)).astype(I32), axis=1), n_ranks - 1)
    rank_expert = jnp.minimum(jnp.sum((jnp.cumsum(owns.astype(I32)) <= experts[:, None]).astype(I32), axis=1),
                              N_EXPERTS - 1)
    return (dest3, block_rank.astype(I32), rank_expert.astype(I32), block_end[-1:].astype(I32),
            n_ranks.reshape(1).astype(I32), nb * rows)


def _grid_pos_tables(n_tokens, dim):
    quarter = dim // 4
    omega = 1.0 / (POS_BASE ** (jnp.arange(quarter, dtype=F32) / quarter))
    ra = jnp.arange(n_tokens // GRID_W).astype(F32)[:, None] * omega
    ca = jnp.arange(GRID_W).astype(F32)[:, None] * omega
    return (jnp.concatenate([jnp.sin(ra), jnp.cos(ra)], axis=-1),
            jnp.concatenate([jnp.sin(ca), jnp.cos(ca)], axis=-1))


def _hi_lo_rows(w):
    hi = w.astype(BF16)
    return jnp.concatenate([hi, (w - hi.astype(F32)).astype(BF16)], axis=0)


def _block_diag(w):
    eye = jnp.eye(N_RG_BLOCKS, dtype=w.dtype)
    return jnp.einsum('nij,nm->nimj', w, eye).reshape(D_RG, D_RG)


def _layer_weights(l, norm1, w_in, mlstm_gate_bias, mlstm_norm, rg_conv_w, rg_conv_b, rg_wa, rg_ba, rg_wx,
                   rg_bx, rg_lambda, rg_norm, w_out, norm2, router_w, router_bias, exp_w1, exp_w3, exp_w2,
                   shared_w1, shared_w3, shared_w2):
    wi = w_in[l]
    c0, c1 = 4 * D_ML, 4 * D_ML + N_GATE
    wg = wi[:, c0:c1]
    wg_hi = wg.astype(BF16)
    wg_lo = (wg - wg_hi.astype(F32)).astype(BF16)
    zcols = lambda w: jnp.zeros((D_MODEL, w), BF16)
    return dict(
        norm1=norm1[l].reshape(1, D_MODEL),
        wq=wi[:, :c0].astype(BF16),
        wr=jnp.concatenate([wi[:, c1:].astype(BF16), wg_hi, wg_lo, zcols(LANES - 2 * N_GATE)], axis=1),
        wgh=jnp.concatenate([wg_hi, zcols(LANES - N_GATE)], axis=1),
        gbias=jnp.pad(mlstm_gate_bias[l].reshape(1, N_GATE), ((0, 0), (0, LANES - N_GATE))),
        mln=mlstm_norm[l].reshape(D_ML, 1),
        cw=rg_conv_w[l], cb=rg_conv_b[l].reshape(1, D_RG),
        wbd=jnp.concatenate([_block_diag(rg_wa[l, 0]), _block_diag(rg_wx[l, 0]),
                             _block_diag(rg_wa[l, 1]), _block_diag(rg_wx[l, 1])], axis=1).astype(BF16),
        rbias=jnp.concatenate([rg_ba[l, 0], rg_bx[l, 0], rg_ba[l, 1], rg_bx[l, 1]]).reshape(1, 4 * D_RG),
        lam=rg_lambda[l], rgn=rg_norm[l].reshape(1, D_RG),
        wo_ml=w_out[l, :D_ML].astype(BF16), wo_rg=w_out[l, D_ML:].astype(BF16),
        norm2=norm2[l].reshape(1, D_MODEL),
        rwt=_hi_lo_rows(router_w[l].T), rb=router_bias[l].reshape(N_EXPERTS, 1),
        w1=exp_w1[l], w3=exp_w3[l], w2=exp_w2[l],
        sw13=jnp.concatenate([shared_w1[l], shared_w3[l]], axis=-1).astype(BF16),
        sw2=shared_w2[l].astype(BF16),
    )


def _mixers(x2d, pos_tables, mod, c0, n0, m0, h0, cnt0, lw, b, t, emit_state):
    tl = _tiles(t, mod.shape[0] > 1)
    tm = tl['tok']
    rtab, ctab = pos_tables
    qt, k, kt, v, vt, ot, xr, gr, g, gt = _in_proj(x2d, rtab, ctab, mod, t, tm, lw['norm1'], lw['wq'], lw['wr'],
                                                   lw['wgh'], lw['gbias'])
    ml = _mlstm(qt, k, kt, v, vt, g, gt, c0, n0, m0, b, t, tl['chunk'], emit_state)
    rg = _rglru(xr, h0, lw['wbd'], lw['rbias'], lw['lam'], lw['cw'], lw['cb'], b, t, tl['scan'], emit_state)
    routed = _mix_out(ml[0], ml[1], ot, rg[0], rg[1], gr, x2d, rtab, ctab, mod, t, tm, lw['mln'], lw['rgn'],
                      lw['wo_ml'], lw['wo_rg'], lw['norm2'], lw['rwt'], lw['rb'], cnt0)
    return routed, ml[2:], rg[2:]


def _routed_experts(paths, lw):
    cnt = paths[-1][5][:, 0]
    hn2p = jnp.concatenate([p[1] for p in paths], axis=0)
    ek = jnp.concatenate([p[2] for p in paths], axis=1)
    pk = jnp.concatenate([p[3] for p in paths], axis=1)
    dest3, block_rank, rank_expert, n_used, n_ranks, n_slots = _dispatch_plan(cnt, ek, pk, EXPERT_ROWS)
    xs = _sc_dispatch(hn2p, dest3, n_slots)
    ys = _experts(xs, block_rank, rank_expert, n_used, n_ranks, lw['w1'], lw['w3'], lw['w2'], EXPERT_ROWS)
    return _sc_combine_gather(ys, dest3)


def kernel(x_prompt, x_sample, c, state_mlstm_C, state_mlstm_n, state_mlstm_m, state_rglru_h, c_ctx, w_ada, b_ada, norm1, w_in, mlstm_gate_bias, mlstm_norm, rg_conv_w, rg_conv_b, rg_wa, rg_ba, rg_wx, rg_bx, rg_lambda, rg_norm, w_out, norm2, router_w, router_bias, exp_w1, exp_w3, exp_w2, shared_w1, shared_w3, shared_w2, norm_final):
    bp, tp, _ = x_prompt.shape
    bs, ts, _ = x_sample.shape
    depth = w_ada.shape[0]
    assert depth == 1, "the final norm is fused into the single layer's MoE output kernel"
    nd = 2 * N_HEADS
    l = 0
    lw = _layer_weights(l, norm1, w_in, mlstm_gate_bias, mlstm_norm, rg_conv_w, rg_conv_b, rg_wa, rg_ba, rg_wx,
                        rg_bx, rg_lambda, rg_norm, w_out, norm2, router_w, router_bias, exp_w1, exp_w3, exp_w2,
                        shared_w1, shared_w3, shared_w2)
    nf = norm_final.reshape(1, D_MODEL)
    cvecs = jnp.concatenate([c_ctx[None], c, jnp.zeros((SUBLANES - 1 - bs, D_MODEL), F32)], axis=0)
    mod = _ada(cvecs, w_ada[l], b_ada[l]).reshape(SUBLANES, N_MOD, D_MODEL)

    mod_p, mod_s = mod[0:1], mod[1:1 + bs]
    tm_p, tm_s = _tiles(tp, False)['tok'], _tiles(ts, True)['tok']
    rp, (cc, nc_, mc), (hc,) = _mixers(
        x_prompt.reshape(bp * tp, D_MODEL),
        (jnp.zeros((tm_p // GRID_W, D_MODEL // 2), F32), jnp.zeros((GRID_W, D_MODEL // 2), F32)), mod_p,
        jnp.zeros((1, nd, HEAD, HEAD), F32), jnp.zeros((1, nd, HEAD), F32), jnp.zeros((1, nd, HEAD), F32),
        jnp.zeros((1, 2, D_RG), F32), jnp.zeros((N_EXPERTS, LANES), I32), lw, bp, tp, True)
    rs, _, _ = _mixers(
        x_sample.reshape(bs * ts, D_MODEL), _grid_pos_tables(ts, D_MODEL), mod_s,
        state_mlstm_C[:, l].reshape(bs, nd, HEAD, HEAD), state_mlstm_n[:, l].reshape(bs, nd, HEAD),
        jnp.broadcast_to(state_mlstm_m[:, l].reshape(bs, nd, 1), (bs, nd, HEAD)),
        state_rglru_h[:, l], jnp.zeros((N_EXPERTS, LANES), I32), lw, bs, ts, False)
    yp = _moe_out(_routed_experts([rp], lw), 0, rp[1], rp[4], lw['sw13'], lw['sw2'], rp[0], mod_p, tp, tm_p, nf)
    ys = _moe_out(_routed_experts([rs], lw), 0, rs[1], rs[4], lw['sw13'], lw['sw2'], rs[0], mod_s, ts, tm_s, nf)

    y_prompt = yp.reshape(bp, tp, D_MODEL)
    y_sample = ys.reshape(bs, ts, D_MODEL)
    new_c = cc.reshape(bp, 1, 2, N_HEADS, HEAD, HEAD)
    new_n = nc_.reshape(bp, 1, 2, N_HEADS, HEAD)
    new_m = mc[:, :, 0].reshape(bp, 1, 2, N_HEADS)
    new_h = hc.reshape(bp, 1, 2, D_RG)
    return (y_prompt, y_sample, new_c, new_n, new_m, new_h)
```

```python
import functools

import jax
import jax.numpy as jnp
from jax import lax
from jax.experimental import pallas as pl
from jax.experimental.pallas import tpu as pltpu
from jax.experimental.pallas import tpu_sc as plsc

F32 = jnp.float32
BF16 = jnp.bfloat16
I32 = jnp.int32
HIGHEST = lax.Precision.HIGHEST

D_MODEL = 1024
N_MOD = 6
D_ML = 512
N_HEADS = 4
HEAD = 128
D_RG = 512
N_RG_BLOCKS = 8
RG_BLOCK = 64
RG_C = 8.0
N_GATE = 16
N_EXPERTS = 64
N_GROUPS = 8
GROUP = 8
TOPK_GROUPS = 4
TOP_K = 8
D_EXPERT = 256
ROUTED_SCALE = 2.5
EPS = 1e-6
GRID_W = 64
POS_BASE = 10000.0

RG_SEG_PAD = 8
SC_WINDOW = 128
D_PACK = D_MODEL // 2
EXPERT_ROWS = 512
EXPERT_BLOCKS_PER_STEP = 4
EXPERT_RING = 8

SUBLANES = 8
LANES = 128
VMEM_LIMIT = 48 * 1024 * 1024
VMEM_LIMIT_EXPERTS = 56 * 1024 * 1024


def _params(*sem, vmem=VMEM_LIMIT):
    return pltpu.CompilerParams(dimension_semantics=sem, vmem_limit_bytes=vmem)


def _tiles(t, per_sequence_mod):
    cap = t if per_sequence_mod else 1 << 30
    return dict(
        tok=min(512, cap),
        chunk=min(256, t),
        scan=min(512, t),
    )


def _silu(x):
    return x * jax.nn.sigmoid(x)


def _softplus(x):
    return jnp.maximum(x, 0.0) + jnp.log1p(jnp.exp(-jnp.abs(x)))


def _rms(x, g):
    return x * lax.rsqrt(jnp.mean(x * x, axis=-1, keepdims=True) + EPS) * g


def _bf16_pieces(x):
    hi = x.astype(BF16)
    r = x - hi.astype(F32)
    mid = r.astype(BF16)
    return hi, mid, (r - mid.astype(F32)).astype(BF16)


def _pack_bf16_pairs(x):
    w = x.shape[1] // 2
    hi = lax.bitcast_convert_type(x[:, :w].astype(BF16).astype(F32), I32)
    lo = lax.bitcast_convert_type(x[:, w:].astype(BF16).astype(F32), I32)
    return hi | lax.shift_right_logical(lo, jnp.full(lo.shape, 16, I32))


def _unpack_bf16_pairs(p):
    hi = lax.bitcast_convert_type(p & jnp.int32(-65536), F32)
    lo = lax.bitcast_convert_type(lax.shift_left(p, jnp.full(p.shape, 16, I32)), F32)
    return hi, lo


def _ada_kernel(c_ref, w_ref, b_ref, o_ref):
    s = _silu(c_ref[...])
    o_ref[...] = jnp.dot(s, w_ref[...], precision=HIGHEST, preferred_element_type=F32) + b_ref[...]


def _ada(cvecs, w_ada, b_ada):
    n_out = w_ada.shape[1]
    tn = 1536
    return pl.pallas_call(
        _ada_kernel,
        grid=(n_out // tn,),
        in_specs=[
            pl.BlockSpec((SUBLANES, D_MODEL), lambda j: (0, 0)),
            pl.BlockSpec((D_MODEL, tn), lambda j: (0, j)),
            pl.BlockSpec((1, tn), lambda j: (0, j)),
        ],
        out_specs=pl.BlockSpec((SUBLANES, tn), lambda j: (0, j)),
        out_shape=jax.ShapeDtypeStruct((SUBLANES, n_out), F32),
        name="ada",
        compiler_params=_params("arbitrary"),
    )(cvecs, w_ada, b_ada.reshape(1, n_out))


def _pos_tile(rt_ref, ct_ref):
    left = jnp.concatenate([jnp.broadcast_to(rt_ref[r:r + 1, :], (GRID_W, rt_ref.shape[1]))
                            for r in range(rt_ref.shape[0])], axis=0)
    right = jnp.concatenate([ct_ref[...]] * rt_ref.shape[0], axis=0)
    return jnp.concatenate([left, right], axis=1)


def _in_proj_kernel(x_ref, rt_ref, ct_ref, mod_ref, n1_ref, wq_ref, wr_ref, wgh_ref, gb_ref,
                    qt_ref, k_ref, kt_ref, v_ref, vt_ref, ot_ref, xr_ref, gr_ref, g_ref, gt_ref):
    x = x_ref[...] + _pos_tile(rt_ref, ct_ref)
    hn = _rms(x, n1_ref[...]) * (1.0 + mod_ref[0, 1:2, :]) + mod_ref[0, 0:1, :]
    hb = hn.astype(BF16)
    z = jnp.dot(hb, wq_ref[...], preferred_element_type=F32)
    k = z[:, D_ML:2 * D_ML] * (HEAD ** -0.5)
    v = z[:, 2 * D_ML:3 * D_ML]
    qt_ref[...] = z[:, 0:D_ML].T.astype(BF16)
    k_ref[...] = k.astype(BF16)
    kt_ref[...] = k.T.astype(BF16)
    v_ref[...] = v.astype(BF16)
    vt_ref[...] = v.T.astype(BF16)
    ot_ref[...] = z[:, 3 * D_ML:4 * D_ML].T
    zr = jnp.dot(hb, wr_ref[...], preferred_element_type=F32)
    xr_ref[...] = zr[:, 0:D_RG]
    gr_ref[...] = zr[:, D_RG:2 * D_RG]
    zg = zr[:, 2 * D_RG:2 * D_RG + LANES]
    h_lo = (hn - hb.astype(F32)).astype(BF16)
    g = (zg + pltpu.roll(zg, LANES - N_GATE, 1)
         + jnp.dot(h_lo, wgh_ref[...], preferred_element_type=F32) + gb_ref[...])
    col = lax.broadcasted_iota(I32, g.shape, 1)
    g = jnp.where((col & 4) != 0, -_softplus(-g), g)
    g_ref[...] = g[:, 0:N_GATE]
    gt_ref[...] = g.T[0:N_GATE, :]


def _mod_index(bm, tm, t):
    if bm > 1:
        return lambda i: ((i * tm) // t, 0, 0)
    return lambda i: (0, 0, 0)


def _pos_specs(rtab, ctab, tm):
    rows = tm // GRID_W
    period = rtab.shape[0] // rows
    return [pl.BlockSpec((rows, D_MODEL // 2), lambda i: (i % period, 0)),
            pl.BlockSpec((GRID_W, D_MODEL // 2), lambda i: (0, 0))]


def _in_proj(x2d, rtab, ctab, mod, t, tm, norm1, wq, wr, wgh, gbias):
    n = x2d.shape[0]
    tok = lambda i: (i, 0)
    tok_t = lambda i: (0, i)
    const = lambda i: (0, 0)
    f = lambda w: jax.ShapeDtypeStruct((n, w), F32)
    row16 = jax.ShapeDtypeStruct((n, D_ML), BF16)
    col16 = jax.ShapeDtypeStruct((D_ML, n), BF16)
    return pl.pallas_call(
        _in_proj_kernel,
        grid=(n // tm,),
        in_specs=[pl.BlockSpec((tm, D_MODEL), tok)] + _pos_specs(rtab, ctab, tm) + [
            pl.BlockSpec((1, N_MOD, D_MODEL), _mod_index(mod.shape[0], tm, t)),
            pl.BlockSpec((1, D_MODEL), const),
            pl.BlockSpec((D_MODEL, 4 * D_ML), const),
            pl.BlockSpec((D_MODEL, 2 * D_RG + LANES), const),
            pl.BlockSpec((D_MODEL, LANES), const),
            pl.BlockSpec((1, LANES), const),
        ],
        out_specs=[pl.BlockSpec((D_ML, tm), tok_t), pl.BlockSpec((tm, D_ML), tok), pl.BlockSpec((D_ML, tm), tok_t),
                   pl.BlockSpec((tm, D_ML), tok), pl.BlockSpec((D_ML, tm), tok_t), pl.BlockSpec((D_ML, tm), tok_t),
                   pl.BlockSpec((tm, D_RG), tok), pl.BlockSpec((tm, D_RG), tok),
                   pl.BlockSpec((tm, N_GATE), tok), pl.BlockSpec((N_GATE, tm), tok_t)],
        out_shape=[col16, row16, col16, row16, col16, jax.ShapeDtypeStruct((D_ML, n), F32),
                   f(D_RG), f(D_RG), f(N_GATE), jax.ShapeDtypeStruct((N_GATE, n), F32)],
        name="in_proj",
        compiler_params=_params("arbitrary"),
    )(x2d, rtab, ctab, mod, norm1, wq, wr, wgh, gbias)


def _mlstm_kernel(chunk, nc, emit_state, *refs):
    (qtf, kf, ktf, vf, vtf, gf, gtf, qtb, kb, ktb, vb, vtb, gb, gtb, c0_ref, n0_ref, m0_ref) = refs[:17]
    if emit_state:
        hf_ref, hb_ref, c_out, n_out, m_out, c_sc, n_sc, m_sc = refs[17:]
    else:
        hf_ref, hb_ref, c_sc, n_sc, m_sc = refs[17:]
    i = pl.program_id(1)

    @pl.when(i == 0)
    def _():
        c_sc[...] = c0_ref[0]
        n_sc[...] = n0_ref[0]
        m_sc[...] = m0_ref[0]

    key = lax.broadcasted_iota(I32, (chunk, chunk), 0)
    qry = lax.broadcasted_iota(I32, (chunk, chunk), 1)
    hd = []
    for d, (qt_ref, k_ref, kt_ref, v_ref, vt_ref, g_ref, gt_ref, h_ref) in enumerate(
            ((qtf, kf, ktf, vf, vtf, gf, gtf, hf_ref), (qtb, kb, ktb, vb, vtb, gb, gtb, hb_ref))):
        tri = (key <= qry) if d == 0 else (key >= qry)
        tri_t = (qry <= key) if d == 0 else (qry >= key)
        g = g_ref[...]
        gt = gt_ref[...]
        rows3 = jnp.dot(jnp.concatenate(_bf16_pieces(gt), axis=0), tri.astype(BF16), preferred_element_type=F32)
        brow = rows3[0:N_GATE] + rows3[N_GATE:2 * N_GATE] + rows3[2 * N_GATE:3 * N_GATE]
        tri_t16 = tri_t.astype(BF16)
        bcol = sum(jnp.dot(tri_t16, piece, preferred_element_type=F32) for piece in _bf16_pieces(g))
        blast = bcol[chunk - 1:chunk, :] if d == 0 else bcol[0:1, :]
        for h in range(N_HEADS):
            ci = d * 8 + h
            cf = d * 8 + 4 + h
            j = d * N_HEADS + h
            sl = slice(h * HEAD, (h + 1) * HEAD)
            hd.append(dict(
                j=j, sl=sl, tri=tri, h_ref=h_ref, qt=qt_ref[sl, :], k=k_ref[:, sl], kt=kt_ref[sl, :],
                v=v_ref[:, sl], vt=vt_ref[sl, :], b_row=brow[cf:cf + 1, :],
                gate_col=g[:, ci:ci + 1] - bcol[:, cf:cf + 1], gate_row=gt[ci:ci + 1, :] - brow[cf:cf + 1, :],
                b_last=blast[:, cf:cf + 1], m_prev=m_sc[j:j + 1, 0:1], c_prev=c_sc[j], n_prev=n_sc[j:j + 1, :]))
    for x in hd:
        top = jnp.max(jnp.where(x['tri'], x['gate_col'], -jnp.inf), axis=0, keepdims=True)
        mx = jnp.maximum(x['m_prev'], top)
        x['dm'] = jnp.exp(jnp.where(x['tri'], x['gate_col'] - mx, -jnp.inf))
        x['w_inter'] = jnp.exp(x['m_prev'] - mx)
        x['floor'] = jnp.exp(-(x['b_row'] + mx))
    for x in hd:
        x['st'] = jnp.dot(x['k'], x['qt'], preferred_element_type=F32) * x['dm']
    for x in hd:
        inter = lax.dot_general(x['c_prev'].astype(BF16), x['qt'], (((0,), (0,)), ((), ())),
                                preferred_element_type=F32)
        num = x['w_inter'] * inter + jnp.dot(x['vt'], x['st'].astype(BF16), preferred_element_type=F32)
        qn = jnp.dot(jnp.broadcast_to(x['n_prev'], (SUBLANES, HEAD)).astype(BF16), x['qt'],
                     preferred_element_type=F32)[0:1, :]
        den = x['w_inter'] * qn + jnp.sum(x['st'], axis=0, keepdims=True)
        x['h_ref'][x['sl'], :] = num / jnp.maximum(jnp.abs(den), x['floor'])
    for x in hd:
        j = x['j']
        log_w = x['b_last'] + x['gate_row']
        m_new = jnp.maximum(x['b_last'] + x['m_prev'], jnp.max(log_w, axis=1, keepdims=True))
        decay = jnp.exp(x['b_last'] + x['m_prev'] - m_new)
        w_row = jnp.exp(log_w - m_new)
        kwt = (x['kt'].astype(F32) * w_row).astype(BF16)
        c_sc[j] = decay * x['c_prev'] + jnp.dot(kwt, x['v'], preferred_element_type=F32)
        n_sc[j:j + 1, :] = decay * x['n_prev'] + jnp.dot(
            jnp.broadcast_to(w_row, (SUBLANES, chunk)).astype(BF16), x['k'], preferred_element_type=F32)[0:1, :]
        m_sc[j:j + 1, :] = jnp.broadcast_to(m_new, (1, HEAD))

    if emit_state:
        @pl.when(i == nc - 1)
        def _():
            c_out[0] = c_sc[...]
            n_out[0] = n_sc[...]
            m_out[0] = m_sc[...]


def _state_index(bm):
    if bm > 1:
        return lambda b, i: (b,) + (0,) * 3, lambda b, i: (b, 0, 0)
    return lambda b, i: (0,) * 4, lambda b, i: (0, 0, 0)


def _mlstm(qt, k, kt, v, vt, g, gt, c0, n0, m0, b, t, chunk, emit_state):
    nc = t // chunk
    n = b * t
    nd = 2 * N_HEADS
    fwd = lambda bi, i: (bi * nc + i, 0)
    bwd = lambda bi, i: (bi * nc + nc - 1 - i, 0)
    fwd_t = lambda bi, i: (0, bi * nc + i)
    bwd_t = lambda bi, i: (0, bi * nc + nc - 1 - i)
    c_idx, n_idx = _state_index(c0.shape[0])
    rows = lambda m: pl.BlockSpec((chunk, D_ML), m)
    cols = lambda m: pl.BlockSpec((D_ML, chunk), m)
    one_dir = lambda m, mt: [cols(mt), rows(m), cols(mt), rows(m), cols(mt),
                             pl.BlockSpec((chunk, N_GATE), m), pl.BlockSpec((N_GATE, chunk), mt)]
    in_specs = (one_dir(fwd, fwd_t) + one_dir(bwd, bwd_t)
                + [pl.BlockSpec((1, nd, HEAD, HEAD), c_idx),
                   pl.BlockSpec((1, nd, HEAD), n_idx), pl.BlockSpec((1, nd, HEAD), n_idx)])
    out_specs = [cols(fwd_t), cols(bwd_t)]
    out_shape = [jax.ShapeDtypeStruct((D_ML, n), F32)] * 2
    if emit_state:
        out_specs += [pl.BlockSpec((1, nd, HEAD, HEAD), lambda bi, i: (bi, 0, 0, 0)),
                      pl.BlockSpec((1, nd, HEAD), lambda bi, i: (bi, 0, 0)),
                      pl.BlockSpec((1, nd, HEAD), lambda bi, i: (bi, 0, 0))]
        out_shape += [jax.ShapeDtypeStruct((b, nd, HEAD, HEAD), F32),
                      jax.ShapeDtypeStruct((b, nd, HEAD), F32),
                      jax.ShapeDtypeStruct((b, nd, HEAD), F32)]
    return pl.pallas_call(
        functools.partial(_mlstm_kernel, chunk, nc, emit_state),
        grid=(b, nc),
        in_specs=in_specs,
        out_specs=out_specs,
        out_shape=out_shape,
        scratch_shapes=[pltpu.VMEM((nd, HEAD, HEAD), F32), pltpu.VMEM((nd, HEAD), F32),
                        pltpu.VMEM((nd, HEAD), F32)],
        name="mlstm",
        compiler_params=_params("arbitrary", "arbitrary"),
    )(qt, k, kt, v, vt, g, gt, qt, k, kt, v, vt, g, gt, c0, n0, m0)


def _neg_expm1_2x(x):
    t = jnp.tanh(x)
    return -2.0 * t / (1.0 - t)


def _rglru_kernel(tb, nb, emit_state, *refs):
    (xf, xf_prev, xf_next, xb, xb_prev, xb_next, h0_ref, w_ref, bias_ref, lam_ref, cw_ref, cb_ref) = refs[:12]
    n_out = 3 if emit_state else 2
    hf_ref, hb_ref = refs[12:14]
    hfin_ref = refs[14] if emit_state else None
    carry, af_sc, uf_sc, ab_sc, ub_sc, hf_sc, pf_sc, hb_sc, pb_sc = refs[12 + n_out:]
    i = pl.program_id(1)

    @pl.when(i == 0)
    def _():
        carry[...] = h0_ref[0]

    row8 = lax.broadcasted_iota(I32, (SUBLANES, D_RG), 0)
    cw = cw_ref[...]
    log_a_per_r = -RG_C * _softplus(-lam_ref[...])

    def taps(xm2, xm1, x0, xp1):
        return cb_ref[...] + xm2 * cw[0:1, :] + xm1 * cw[1:2, :] + x0 * cw[2:3, :] + xp1 * cw[3:4, :]

    def conv(main_ref, prev_ref, next_ref, first, last):
        main = main_ref[...]
        prev = jnp.where(first, 0.0, prev_ref[...])
        nxt = jnp.where(last, 0.0, next_ref[...])
        body = taps(pltpu.roll(main, 2, 0), pltpu.roll(main, 1, 0), main, pltpu.roll(main, tb - 1, 0))
        e = SUBLANES
        head, tail = main[0:e, :], main[tb - e:tb, :]
        before_tail = main[tb - 2 * e:tb - e, :]
        fix_head = taps(
            jnp.where(row8 == 0, prev[6:7, :], jnp.where(row8 == 1, prev[7:8, :], pltpu.roll(head, 2, 0))),
            jnp.where(row8 == 0, prev[7:8, :], pltpu.roll(head, 1, 0)), head,
            jnp.where(row8 == e - 1, main[e:e + 1, :], pltpu.roll(head, e - 1, 0)))
        fix_tail = taps(
            jnp.where(row8 == 0, before_tail[6:7, :],
                      jnp.where(row8 == 1, before_tail[7:8, :], pltpu.roll(tail, 2, 0))),
            jnp.where(row8 == 0, before_tail[7:8, :], pltpu.roll(tail, 1, 0)), tail,
            jnp.where(row8 == e - 1, nxt[0:1, :], pltpu.roll(tail, e - 1, 0)))
        return jnp.concatenate([fix_head, body[e:tb - e, :], fix_tail], axis=0)

    def recurrence_terms(xc, d):
        z = jnp.dot(xc.astype(BF16), w_ref[:, d * 2 * D_RG:(d + 1) * 2 * D_RG],
                    preferred_element_type=F32) + bias_ref[:, d * 2 * D_RG:(d + 1) * 2 * D_RG]
        r = jax.nn.sigmoid(z[:, 0:D_RG])
        ig = jax.nn.sigmoid(z[:, D_RG:2 * D_RG])
        log_a = r * log_a_per_r[d:d + 1, :]
        a = jnp.exp(log_a)
        u = jnp.sqrt(_neg_expm1_2x(log_a)) * (ig * xc)
        return a, u

    a_f, u_f = recurrence_terms(conv(xf, xf_prev, xf_next, i == 0, i == nb - 1), 0)
    a_b, u_b = recurrence_terms(conv(xb, xb_prev, xb_next, i == nb - 1, i == 0), 1)
    seg = tb // SUBLANES
    pitch = seg + RG_SEG_PAD
    ncol = D_RG // LANES
    for lc in range(ncol):
        lanes = slice(lc * LANES, (lc + 1) * LANES)
        for s in range(SUBLANES):
            src = slice(s * seg, (s + 1) * seg)
            dst = slice(s * pitch, s * pitch + seg)
            af_sc[lc, dst, :], uf_sc[lc, dst, :] = a_f[src, lanes], u_f[src, lanes]
            ab_sc[lc, dst, :], ub_sc[lc, dst, :] = a_b[src, lanes], u_b[src, lanes]
    slab = lambda k: (slice(None), pl.ds(k, SUBLANES, stride=pitch), slice(None))
    hf = jnp.zeros((ncol, SUBLANES, LANES), F32)
    hb = jnp.zeros((ncol, SUBLANES, LANES), F32)
    pf = jnp.ones((ncol, SUBLANES, LANES), F32)
    pb = jnp.ones((ncol, SUBLANES, LANES), F32)
    for k in range(seg):
        kb = seg - 1 - k
        ak = af_sc[slab(k)]
        hf = ak * hf + uf_sc[slab(k)]
        pf = pf * ak
        hf_sc[slab(k)] = hf
        pf_sc[slab(k)] = pf
        ak = ab_sc[slab(kb)]
        hb = ak * hb + ub_sc[slab(kb)]
        pb = pb * ak
        hb_sc[slab(kb)] = hb
        pb_sc[slab(kb)] = pb
    for lc in range(ncol):
        lanes = slice(lc * LANES, (lc + 1) * LANES)
        c = carry[0:1, lanes]
        cin_f = []
        for s in range(SUBLANES):
            cin_f.append(c)
            c = pf[lc, s:s + 1, :] * c + hf[lc, s:s + 1, :]
        carry[0:1, lanes] = c
        c = carry[1:2, lanes]
        cin_b = [None] * SUBLANES
        for s in reversed(range(SUBLANES)):
            cin_b[s] = c
            c = pb[lc, s:s + 1, :] * c + hb[lc, s:s + 1, :]
        carry[1:2, lanes] = c
        for s in range(SUBLANES):
            rows = slice(s * seg, (s + 1) * seg)
            src = slice(s * pitch, s * pitch + seg)
            hf_ref[rows, lanes] = hf_sc[lc, src, :] + pf_sc[lc, src, :] * cin_f[s]
            hb_ref[rows, lanes] = hb_sc[lc, src, :] + pb_sc[lc, src, :] * cin_b[s]

    if emit_state:
        @pl.when(i == nb - 1)
        def _():
            hfin_ref[0] = carry[...]


def _rglru(xr, h0, wbd, bias, lam, cw, cb, b, t, tb, emit_state):
    nb = t // tb
    n = b * t
    r8 = tb // SUBLANES
    last8 = n // SUBLANES - 1
    fwd = lambda bi, i: (bi * nb + i, 0)
    bwd = lambda bi, i: (bi * nb + nb - 1 - i, 0)
    fwd_prev = lambda bi, i: (jnp.maximum((bi * nb + i) * r8 - 1, 0), 0)
    fwd_next = lambda bi, i: (jnp.minimum((bi * nb + i + 1) * r8, last8), 0)
    bwd_prev = lambda bi, i: (jnp.maximum((bi * nb + nb - 1 - i) * r8 - 1, 0), 0)
    bwd_next = lambda bi, i: (jnp.minimum((bi * nb + nb - i) * r8, last8), 0)
    const = lambda bi, i: (0, 0)
    h_idx = (lambda bi, i: (bi, 0, 0)) if h0.shape[0] > 1 else (lambda bi, i: (0, 0, 0))
    halo = lambda m: pl.BlockSpec((SUBLANES, D_RG), m)
    in_specs = [pl.BlockSpec((tb, D_RG), fwd), halo(fwd_prev), halo(fwd_next),
                pl.BlockSpec((tb, D_RG), bwd), halo(bwd_prev), halo(bwd_next),
                pl.BlockSpec((1, 2, D_RG), h_idx),
                pl.BlockSpec((D_RG, 4 * D_RG), const), pl.BlockSpec((1, 4 * D_RG), const),
                pl.BlockSpec((2, D_RG), const), pl.BlockSpec((4, D_RG), const), pl.BlockSpec((1, D_RG), const)]
    out_specs = [pl.BlockSpec((tb, D_RG), fwd), pl.BlockSpec((tb, D_RG), bwd)]
    out_shape = [jax.ShapeDtypeStruct((n, D_RG), F32)] * 2
    if emit_state:
        out_specs.append(pl.BlockSpec((1, 2, D_RG), lambda bi, i: (bi, 0, 0)))
        out_shape.append(jax.ShapeDtypeStruct((b, 2, D_RG), F32))
    return pl.pallas_call(
        functools.partial(_rglru_kernel, tb, nb, emit_state),
        grid=(b, nb),
        in_specs=in_specs,
        out_specs=out_specs,
        out_shape=out_shape,
        scratch_shapes=[pltpu.VMEM((2, D_RG), F32)]
        + [pltpu.VMEM((D_RG // LANES, tb + SUBLANES * RG_SEG_PAD, LANES), F32)] * 8,
        name="rglru",
        compiler_params=_params("arbitrary", "arbitrary"),
    )(xr, xr, xr, xr, xr, xr, h0, wbd, bias, lam, cw, cb)


def _route(s, sb):
    tm = s.shape[1]
    neg = -jnp.inf
    sub = lax.broadcasted_iota(I32, (GROUP, tm), 0)
    blocks = [sb[gi * GROUP:(gi + 1) * GROUP, :] for gi in range(N_GROUPS)]
    gscore = []
    for blk in blocks:
        m1 = jnp.max(blk, axis=0, keepdims=True)
        first = jnp.min(jnp.where(blk == m1, sub, GROUP), axis=0, keepdims=True)
        m2 = jnp.max(jnp.where(sub == first, neg, blk), axis=0, keepdims=True)
        gscore.append(m1 + m2)
    masked = []
    for gi in range(N_GROUPS):
        rank = jnp.zeros((1, tm), F32)
        for gj in range(N_GROUPS):
            if gj == gi:
                continue
            ahead = (gscore[gj] >= gscore[gi]) if gj < gi else (gscore[gj] > gscore[gi])
            rank = rank + jnp.where(ahead, 1.0, 0.0)
        masked.append(jnp.where(rank < TOPK_GROUPS, blocks[gi], neg))
    v = jnp.concatenate(masked, axis=0)
    eid = lax.broadcasted_iota(I32, (N_EXPERTS, tm), 0)
    sel = jnp.zeros((N_EXPERTS, tm), F32)
    picks = []
    for _ in range(TOP_K):
        mx = jnp.max(v, axis=0, keepdims=True)
        idx = jnp.min(jnp.where(v == mx, eid, N_EXPERTS), axis=0, keepdims=True)
        pick = eid == idx
        picks.append(pick)
        sel = jnp.where(pick, 1.0, sel)
        v = jnp.where(pick, neg, v)
    ws = s * sel
    return ws / jnp.sum(ws, axis=0, keepdims=True) * ROUTED_SCALE, sel, picks


def _mix_out_kernel(hmf_ref, hmb_ref, ot_ref, hrf_ref, hrb_ref, gr_ref, x_ref, rt_ref, ct_ref, mod_ref,
                    mln_ref, rgn_ref, wo_ml_ref, wo_rg_ref, n2_ref, rwt_ref, rb_ref, cnt0_ref,
                    x1_ref, hn2p_ref, ek_ref, pk_ref, wtok_ref, cnt_ref, cnt_sc):
    i = pl.program_id(0)
    tm = x_ref.shape[0]

    @pl.when(i == 0)
    def _():
        cnt_sc[...] = cnt0_ref[...].astype(F32)

    hm = hmf_ref[...] + hmb_ref[...]
    heads = []
    for h in range(N_HEADS):
        seg = hm[h * HEAD:(h + 1) * HEAD, :]
        heads.append(seg * lax.rsqrt(jnp.mean(seg * seg, axis=0, keepdims=True) + EPS))
    y_ml_t = jnp.concatenate(heads, axis=0) * mln_ref[...] * jax.nn.sigmoid(ot_ref[...])
    y_rg = _rms(hrf_ref[...] + hrb_ref[...], rgn_ref[...]) * jax.nn.gelu(gr_ref[...])
    mix = (lax.dot_general(y_ml_t.astype(BF16), wo_ml_ref[...], (((0,), (0,)), ((), ())),
                           preferred_element_type=F32)
           + jnp.dot(y_rg.astype(BF16), wo_rg_ref[...], preferred_element_type=F32))
    x1 = x_ref[...] + _pos_tile(rt_ref, ct_ref) + mod_ref[0, 2:3, :] * mix
    x1_ref[...] = x1
    hn2 = _rms(x1, n2_ref[...]) * (1.0 + mod_ref[0, 4:5, :]) + mod_ref[0, 3:4, :]
    hn2p_ref[...] = _pack_bf16_pairs(hn2)
    hb = hn2.astype(BF16)
    h_lo = (hn2 - hb.astype(F32)).astype(BF16)
    nt = (((1,), (1,)), ((), ()))
    two = lax.dot_general(rwt_ref[...], hb, nt, preferred_element_type=F32)
    logits_t = (two[0:N_EXPERTS, :] + two[N_EXPERTS:2 * N_EXPERTS, :]
                + lax.dot_general(rwt_ref[0:N_EXPERTS, :], h_lo, nt, preferred_element_type=F32))
    s = jax.nn.sigmoid(logits_t)
    wt, sel, picks = _route(s, s + rb_ref[...])

    earlier = (lax.broadcasted_iota(I32, (tm, tm), 0) < lax.broadcasted_iota(I32, (tm, tm), 1))
    prefix = jnp.dot(sel.astype(BF16), earlier.astype(BF16), preferred_element_type=F32)
    pos_all = cnt_sc[:, 0:1] + prefix
    eid = lax.broadcasted_iota(I32, (N_EXPERTS, tm), 0)
    eid_f = eid.astype(F32)
    row8 = lax.broadcasted_iota(I32, (TOP_K, tm), 0)
    ek = jnp.zeros((TOP_K, tm), F32)
    pk = jnp.zeros((TOP_K, tm), F32)
    wk = jnp.zeros((N_EXPERTS, tm), F32)
    for k, pick in enumerate(picks):
        take = lambda a: jnp.sum(jnp.where(pick, a, 0.0), axis=0, keepdims=True)
        ek = jnp.where(row8 == k, take(eid_f), ek)
        pk = jnp.where(row8 == k, take(pos_all), pk)
        wk = jnp.where(eid == k, take(wt), wk)
    ek_ref[...] = ek.astype(I32)
    pk_ref[...] = pk.astype(I32)
    wtok_ref[...] = wk.T
    cnt_sc[...] += jnp.broadcast_to(jnp.sum(sel, axis=1, keepdims=True), cnt_sc.shape)

    @pl.when(i == pl.num_programs(0) - 1)
    def _():
        cnt_ref[...] = cnt_sc[...].astype(I32)


def _mix_out(hmf, hmb, ot, hrf, hrb, gr, x2d, rtab, ctab, mod, t, tm, mln, rgn, wo_ml, wo_rg, norm2, rwt, rbias,
             cnt0):
    n = x2d.shape[0]
    tok = lambda i: (i, 0)
    tok_t = lambda i: (0, i)
    const = lambda i: (0, 0)
    return pl.pallas_call(
        _mix_out_kernel,
        grid=(n // tm,),
        in_specs=[pl.BlockSpec((D_ML, tm), tok_t)] * 3 + [pl.BlockSpec((tm, D_RG), tok)] * 3 + [
            pl.BlockSpec((tm, D_MODEL), tok)] + _pos_specs(rtab, ctab, tm) + [
            pl.BlockSpec((1, N_MOD, D_MODEL), _mod_index(mod.shape[0], tm, t)),
            pl.BlockSpec((D_ML, 1), const), pl.BlockSpec((1, D_RG), const),
            pl.BlockSpec((D_ML, D_MODEL), const), pl.BlockSpec((D_RG, D_MODEL), const),
            pl.BlockSpec((1, D_MODEL), const),
            pl.BlockSpec((2 * N_EXPERTS, D_MODEL), const), pl.BlockSpec((N_EXPERTS, 1), const),
            pl.BlockSpec((N_EXPERTS, LANES), const),
        ],
        out_specs=[pl.BlockSpec((tm, D_MODEL), tok), pl.BlockSpec((tm, D_PACK), tok),
                   pl.BlockSpec((TOP_K, tm), tok_t), pl.BlockSpec((TOP_K, tm), tok_t),
                   pl.BlockSpec((tm, N_EXPERTS), tok), pl.BlockSpec((N_EXPERTS, LANES), const)],
        out_shape=[jax.ShapeDtypeStruct((n, D_MODEL), F32), jax.ShapeDtypeStruct((n, D_PACK), I32),
                   jax.ShapeDtypeStruct((TOP_K, n), I32), jax.ShapeDtypeStruct((TOP_K, n), I32),
                   jax.ShapeDtypeStruct((n, N_EXPERTS), F32), jax.ShapeDtypeStruct((N_EXPERTS, LANES), I32)],
        scratch_shapes=[pltpu.VMEM((N_EXPERTS, LANES), F32)],
        name="mix_out",
        compiler_params=_params("arbitrary"),
    )(hmf, hmb, ot, hrf, hrb, gr, x2d, rtab, ctab, mod, mln, rgn, wo_ml, wo_rg, norm2, rwt, rbias, cnt0)


def _sc_mesh():
    return plsc.VectorSubcoreMesh(core_axis_name="core", subcore_axis_name="subcore")


def _sc_worker():
    info = plsc.get_sparse_core_info()
    return lax.axis_index("subcore") * info.num_cores + lax.axis_index("core"), info.num_cores * info.num_subcores


def _sc_dispatch(xp, dest3, n_slots):
    n, w = xp.shape
    nwin = n // SC_WINDOW

    @pl.kernel(out_type=jax.ShapeDtypeStruct((n_slots, w), xp.dtype), mesh=_sc_mesh(),
               scratch_types=[pltpu.VMEM((SC_WINDOW, w), xp.dtype), pltpu.VMEM((TOP_K, SC_WINDOW), I32)],
               name="sc_dispatch")
    def k(x_hbm, i_hbm, o_hbm, x_v, i_v):
        wid, nworkers = _sc_worker()
        per = nwin // nworkers

        @pl.loop(0, per)
        def _(s):
            win = wid * per + s
            pltpu.sync_copy(x_hbm.at[pl.ds(win * SC_WINDOW, SC_WINDOW)], x_v)
            pltpu.sync_copy(i_hbm.at[win], i_v)
            for j in range(TOP_K):
                pltpu.sync_copy(x_v, o_hbm.at[i_v.at[j]])

    return k(xp, dest3)


def _sc_combine_gather(ys, dest3):
    nwin = dest3.shape[0]
    w = ys.shape[1]

    @pl.kernel(out_type=jax.ShapeDtypeStruct((nwin, TOP_K, SC_WINDOW, w), ys.dtype), mesh=_sc_mesh(),
               scratch_types=[pltpu.VMEM((SC_WINDOW, w), ys.dtype), pltpu.VMEM((TOP_K, SC_WINDOW), I32)],
               name="sc_combine")
    def k(y_hbm, i_hbm, o_hbm, y_v, i_v):
        wid, nworkers = _sc_worker()
        per = nwin // nworkers

        @pl.loop(0, per)
        def _(s):
            win = wid * per + s
            pltpu.sync_copy(i_hbm.at[win], i_v)
            for j in range(TOP_K):
                pltpu.sync_copy(y_hbm.at[i_v.at[j]], y_v)
                pltpu.sync_copy(y_v, o_hbm.at[win, j])

    return k(ys, dest3)


def _swiglu(x, w13):
    h = jnp.dot(x, w13, preferred_element_type=F32)
    return _silu(h[:, 0:D_EXPERT]) * h[:, D_EXPERT:2 * D_EXPERT]


def _unpack_rows_bf16(p):
    hi, lo = _unpack_bf16_pairs(p)
    return jnp.concatenate([hi.astype(BF16), lo.astype(BF16)], axis=1)


def _expert_kernel(rows, rank_ref, expert_ref, nu_ref, nr_ref, x_ref, w1_hbm, w3_hbm, w2_hbm, y_ref,
                   w1buf, w3buf, w2buf, sem, progress):
    b = pl.program_id(0)
    g = EXPERT_BLOCKS_PER_STEP
    n_ranks = nr_ref[0]

    @pl.when(b == 0)
    def _():
        progress[0] = 0
        progress[1] = 0

    def copies(r):
        slot = lax.rem(r, EXPERT_RING)
        e = expert_ref[r]
        return (pltpu.make_async_copy(w1_hbm.at[e], w1buf.at[slot], sem.at[slot, 0]),
                pltpu.make_async_copy(w3_hbm.at[e], w3buf.at[slot], sem.at[slot, 1]),
                pltpu.make_async_copy(w2_hbm.at[e], w2buf.at[slot], sem.at[slot, 2]))

    def start(r, carry):
        for cp in copies(r):
            cp.start()
        return carry

    def wait(r, carry):
        for cp in copies(r):
            cp.wait()
        return carry

    first = rank_ref[b * g]
    last = rank_ref[b * g + g - 1]
    started = jnp.minimum(first + EXPERT_RING, n_ranks)
    lax.fori_loop(progress[0], started, start, 0)
    progress[0] = jnp.maximum(progress[0], started)
    needed = jnp.where(b == pl.num_programs(0) - 1, progress[0], jnp.minimum(last + 1, n_ranks))
    lax.fori_loop(progress[1], needed, wait, 0)
    progress[1] = jnp.maximum(progress[1], needed)

    @pl.when(b * g < nu_ref[0])
    def _():
        for j in range(g):
            slot = lax.rem(rank_ref[b * g + j], EXPERT_RING)
            sl = slice(j * rows, (j + 1) * rows)
            x = _unpack_rows_bf16(x_ref[sl, :])
            h = (_silu(jnp.dot(x, w1buf[slot].astype(BF16), preferred_element_type=F32))
                 * jnp.dot(x, w3buf[slot].astype(BF16), preferred_element_type=F32))
            y_ref[sl, :] = _pack_bf16_pairs(
                jnp.dot(h.astype(BF16), w2buf[slot].astype(BF16), preferred_element_type=F32))


def _experts(xs, block_rank, rank_expert, n_used, n_ranks, w1, w3, w2, rows):
    g = EXPERT_BLOCKS_PER_STEP
    nb = xs.shape[0] // rows
    tok = lambda b, *_: (b, 0)
    return pl.pallas_call(
        functools.partial(_expert_kernel, rows),
        grid_spec=pltpu.PrefetchScalarGridSpec(
            num_scalar_prefetch=4,
            grid=(nb // g,),
            in_specs=[pl.BlockSpec((g * rows, D_PACK), tok)] + [pl.BlockSpec(memory_space=pl.ANY)] * 3,
            out_specs=pl.BlockSpec((g * rows, D_PACK), tok),
            scratch_shapes=[pltpu.VMEM((EXPERT_RING, D_MODEL, D_EXPERT), F32),
                            pltpu.VMEM((EXPERT_RING, D_MODEL, D_EXPERT), F32),
                            pltpu.VMEM((EXPERT_RING, D_EXPERT, D_MODEL), F32),
                            pltpu.SemaphoreType.DMA((EXPERT_RING, 3)), pltpu.SMEM((2,), I32)],
        ),
        out_shape=jax.ShapeDtypeStruct(xs.shape, I32),
        name="experts",
        compiler_params=_params("arbitrary", vmem=VMEM_LIMIT_EXPERTS),
    )(block_rank, rank_expert, n_used, n_ranks, xs, w1, w3, w2)


def _moe_out_kernel(yk_ref, hn2p_ref, wtok_ref, sw13_ref, sw2_ref, x1_ref, mod_ref, nf_ref, y_ref):
    shared = jnp.dot(_swiglu(_unpack_rows_bf16(hn2p_ref[...]), sw13_ref[...]).astype(BF16), sw2_ref[...],
                     preferred_element_type=F32)
    w = wtok_ref[...]
    parts = []
    for wi in range(yk_ref.shape[0]):
        rows = slice(wi * SC_WINDOW, (wi + 1) * SC_WINDOW)
        a_hi = shared[rows, 0:D_PACK]
        a_lo = shared[rows, D_PACK:D_MODEL]
        for k in range(TOP_K):
            y_hi, y_lo = _unpack_bf16_pairs(yk_ref[wi, k])
            wc = w[rows, k:k + 1]
            a_hi = a_hi + wc * y_hi
            a_lo = a_lo + wc * y_lo
        parts.append(jnp.concatenate([a_hi, a_lo], axis=1))
    x2 = x1_ref[...] + mod_ref[0, 5:6, :] * jnp.concatenate(parts, axis=0)
    y_ref[...] = _rms(x2, nf_ref[...])


def _moe_out(yk, tok0, hn2p, wtok, sw13, sw2, x1, mod, t, tm, norm_final):
    n = hn2p.shape[0]
    tok = lambda i: (i, 0)
    const = lambda i: (0, 0)
    blk0 = tok0 // tm
    return pl.pallas_call(
        _moe_out_kernel,
        grid=(n // tm,),
        in_specs=[
            pl.BlockSpec((tm // SC_WINDOW, TOP_K, SC_WINDOW, D_PACK), lambda i: (i + blk0, 0, 0, 0)),
            pl.BlockSpec((tm, D_PACK), tok),
            pl.BlockSpec((tm, N_EXPERTS), tok),
            pl.BlockSpec((D_MODEL, 2 * D_EXPERT), const),
            pl.BlockSpec((D_EXPERT, D_MODEL), const),
            pl.BlockSpec((tm, D_MODEL), tok),
            pl.BlockSpec((1, N_MOD, D_MODEL), _mod_index(mod.shape[0], tm, t)),
            pl.BlockSpec((1, D_MODEL), const),
        ],
        out_specs=pl.BlockSpec((tm, D_MODEL), tok),
        out_shape=jax.ShapeDtypeStruct((n, D_MODEL), F32),
        name="moe_out",
        compiler_params=_params("arbitrary"),
    )(yk, hn2p, wtok, sw13, sw2, x1, mod, norm_final)


def _dispatch_plan(cnt, ek, pk, rows):
    n = ek.shape[1]
    nb = n * TOP_K // rows + N_EXPERTS
    nblk = (cnt + rows - 1) // rows
    block_end = jnp.cumsum(nblk)
    experts = jnp.arange(N_EXPERTS, dtype=I32)
    first_row = jnp.sum(jnp.where(ek[:, :, None] == experts, (block_end - nblk) * rows, 0), axis=-1)
    dest3 = (first_row + pk).reshape(TOP_K, n // SC_WINDOW, SC_WINDOW).transpose(1, 0, 2)
    owns = nblk > 0
    n_ranks = jnp.sum(owns.astype(I32))
    blocks = jnp.arange(nb, dtype=I32)[:, None]
    block_rank = jnp.minimum(jnp.sum((owns & (block_end <= blocks)).astype(I32), axis=1), n_ranks - 1)
    rank_expert = jnp.minimum(jnp.sum((jnp.cumsum(owns.astype(I32)) <= experts[:, None]).astype(I32), axis=1),
                              N_EXPERTS - 1)
    return (dest3, block_rank.astype(I32), rank_expert.astype(I32), block_end[-1:].astype(I32),
            n_ranks.reshape(1).astype(I32), nb * rows)


def _grid_pos_tables(n_tokens, dim):
    quarter = dim // 4
    omega = 1.0 / (POS_BASE ** (jnp.arange(quarter, dtype=F32) / quarter))
    ra = jnp.arange(n_tokens // GRID_W).astype(F32)[:, None] * omega
    ca = jnp.arange(GRID_W).astype(F32)[:, None] * omega
    return (jnp.concatenate([jnp.sin(ra), jnp.cos(ra)], axis=-1),
            jnp.concatenate([jnp.sin(ca), jnp.cos(ca)], axis=-1))


def _hi_lo_rows(w):
    hi = w.astype(BF16)
    return jnp.concatenate([hi, (w - hi.astype(F32)).astype(BF16)], axis=0)


def _block_diag(w):
    eye = jnp.eye(N_RG_BLOCKS, dtype=w.dtype)
    return jnp.einsum('nij,nm->nimj', w, eye).reshape(D_RG, D_RG)


def _layer_weights(l, norm1, w_in, mlstm_gate_bias, mlstm_norm, rg_conv_w, rg_conv_b, rg_wa, rg_ba, rg_wx,
                   rg_bx, rg_lambda, rg_norm, w_out, norm2, router_w, router_bias, exp_w1, exp_w3, exp_w2,
                   shared_w1, shared_w3, shared_w2):
    wi = w_in[l]
    c0, c1 = 4 * D_ML, 4 * D_ML + N_GATE
    wg = wi[:, c0:c1]
    wg_hi = wg.astype(BF16)
    wg_lo = (wg - wg_hi.astype(F32)).astype(BF16)
    zcols = lambda w: jnp.zeros((D_MODEL, w), BF16)
    return dict(
        norm1=norm1[l].reshape(1, D_MODEL),
        wq=wi[:, :c0].astype(BF16),
        wr=jnp.concatenate([wi[:, c1:].astype(BF16), wg_hi, wg_lo, zcols(LANES - 2 * N_GATE)], axis=1),
        wgh=jnp.concatenate([wg_hi, zcols(LANES - N_GATE)], axis=1),
        gbias=jnp.pad(mlstm_gate_bias[l].reshape(1, N_GATE), ((0, 0), (0, LANES - N_GATE))),
        mln=mlstm_norm[l].reshape(D_ML, 1),
        cw=rg_conv_w[l], cb=rg_conv_b[l].reshape(1, D_RG),
        wbd=jnp.concatenate([_block_diag(rg_wa[l, 0]), _block_diag(rg_wx[l, 0]),
                             _block_diag(rg_wa[l, 1]), _block_diag(rg_wx[l, 1])], axis=1).astype(BF16),
        rbias=jnp.concatenate([rg_ba[l, 0], rg_bx[l, 0], rg_ba[l, 1], rg_bx[l, 1]]).reshape(1, 4 * D_RG),
        lam=rg_lambda[l], rgn=rg_norm[l].reshape(1, D_RG),
        wo_ml=w_out[l, :D_ML].astype(BF16), wo_rg=w_out[l, D_ML:].astype(BF16),
        norm2=norm2[l].reshape(1, D_MODEL),
        rwt=_hi_lo_rows(router_w[l].T), rb=router_bias[l].reshape(N_EXPERTS, 1),
        w1=exp_w1[l], w3=exp_w3[l], w2=exp_w2[l],
        sw13=jnp.concatenate([shared_w1[l], shared_w3[l]], axis=-1).astype(BF16),
        sw2=shared_w2[l].astype(BF16),
    )


def _mixers(x2d, pos_tables, mod, c0, n0, m0, h0, cnt0, lw, b, t, emit_state):
    tl = _tiles(t, mod.shape[0] > 1)
    tm = tl['tok']
    rtab, ctab = pos_tables
    qt, k, kt, v, vt, ot, xr, gr, g, gt = _in_proj(x2d, rtab, ctab, mod, t, tm, lw['norm1'], lw['wq'], lw['wr'],
                                                   lw['wgh'], lw['gbias'])
    ml = _mlstm(qt, k, kt, v, vt, g, gt, c0, n0, m0, b, t, tl['chunk'], emit_state)
    rg = _rglru(xr, h0, lw['wbd'], lw['rbias'], lw['lam'], lw['cw'], lw['cb'], b, t, tl['scan'], emit_state)
    routed = _mix_out(ml[0], ml[1], ot, rg[0], rg[1], gr, x2d, rtab, ctab, mod, t, tm, lw['mln'], lw['rgn'],
                      lw['wo_ml'], lw['wo_rg'], lw['norm2'], lw['rwt'], lw['rb'], cnt0)
    return routed, ml[2:], rg[2:]


def _routed_experts(paths, lw):
    cnt = paths[-1][5][:, 0]
    hn2p = jnp.concatenate([p[1] for p in paths], axis=0)
    ek = jnp.concatenate([p[2] for p in paths], axis=1)
    pk = jnp.concatenate([p[3] for p in paths], axis=1)
    dest3, block_rank, rank_expert, n_used, n_ranks, n_slots = _dispatch_plan(cnt, ek, pk, EXPERT_ROWS)
    xs = _sc_dispatch(hn2p, dest3, n_slots)
    ys = _experts(xs, block_rank, rank_expert, n_used, n_ranks, lw['w1'], lw['w3'], lw['w2'], EXPERT_ROWS)
    return _sc_combine_gather(ys, dest3)


def kernel(x_prompt, x_sample, c, state_mlstm_C, state_mlstm_n, state_mlstm_m, state_rglru_h, c_ctx, w_ada, b_ada, norm1, w_in, mlstm_gate_bias, mlstm_norm, rg_conv_w, rg_conv_b, rg_wa, rg_ba, rg_wx, rg_bx, rg_lambda, rg_norm, w_out, norm2, router_w, router_bias, exp_w1, exp_w3, exp_w2, shared_w1, shared_w3, shared_w2, norm_final):
    bp, tp, _ = x_prompt.shape
    bs, ts, _ = x_sample.shape
    depth = w_ada.shape[0]
    assert depth == 1, "the final norm is fused into the single layer's MoE output kernel"
    nd = 2 * N_HEADS
    l = 0
    lw = _layer_weights(l, norm1, w_in, mlstm_gate_bias, mlstm_norm, rg_conv_w, rg_conv_b, rg_wa, rg_ba, rg_wx,
                        rg_bx, rg_lambda, rg_norm, w_out, norm2, router_w, router_bias, exp_w1, exp_w3, exp_w2,
                        shared_w1, shared_w3, shared_w2)
    nf = norm_final.reshape(1, D_MODEL)
    cvecs = jnp.concatenate([c_ctx[None], c, jnp.zeros((SUBLANES - 1 - bs, D_MODEL), F32)], axis=0)
    mod = _ada(cvecs, w_ada[l], b_ada[l]).reshape(SUBLANES, N_MOD, D_MODEL)

    mod_p, mod_s = mod[0:1], mod[1:1 + bs]
    tm_p, tm_s = _tiles(tp, False)['tok'], _tiles(ts, True)['tok']
    rp, (cc, nc_, mc), (hc,) = _mixers(
        x_prompt.reshape(bp * tp, D_MODEL),
        (jnp.zeros((tm_p // GRID_W, D_MODEL // 2), F32), jnp.zeros((GRID_W, D_MODEL // 2), F32)), mod_p,
        jnp.zeros((1, nd, HEAD, HEAD), F32), jnp.zeros((1, nd, HEAD), F32), jnp.zeros((1, nd, HEAD), F32),
        jnp.zeros((1, 2, D_RG), F32), jnp.zeros((N_EXPERTS, LANES), I32), lw, bp, tp, True)
    rs, _, _ = _mixers(
        x_sample.reshape(bs * ts, D_MODEL), _grid_pos_tables(ts, D_MODEL), mod_s,
        state_mlstm_C[:, l].reshape(bs, nd, HEAD, HEAD), state_mlstm_n[:, l].reshape(bs, nd, HEAD),
        jnp.broadcast_to(state_mlstm_m[:, l].reshape(bs, nd, 1), (bs, nd, HEAD)),
        state_rglru_h[:, l], jnp.zeros((N_EXPERTS, LANES), I32), lw, bs, ts, False)
    yp = _moe_out(_routed_experts([rp], lw), 0, rp[1], rp[4], lw['sw13'], lw['sw2'], rp[0], mod_p, tp, tm_p, nf)
    ys = _moe_out(_routed_experts([rs], lw), 0, rs[1], rs[4], lw['sw13'], lw['sw2'], rs[0], mod_s, ts, tm_s, nf)

    y_prompt = yp.reshape(bp, tp, D_MODEL)
    y_sample = ys.reshape(bs, ts, D_MODEL)
    new_c = cc.reshape(bp, 1, 2, N_HEADS, HEAD, HEAD)
    new_n = nc_.reshape(bp, 1, 2, N_HEADS, HEAD)
    new_m = mc[:, :, 0].reshape(bp, 1, 2, N_HEADS)
    new_h = hc.reshape(bp, 1, 2, D_RG)
    return (y_prompt, y_sample, new_c, new_n, new_m, new_h)
```

```python
import functools

import jax
import jax.numpy as jnp
from jax import lax
from jax.experimental import pallas as pl
from jax.experimental.pallas import tpu as pltpu
from jax.experimental.pallas import tpu_sc as plsc

F32 = jnp.float32
BF16 = jnp.bfloat16
I32 = jnp.int32
HIGHEST = lax.Precision.HIGHEST

D_MODEL = 1024
N_MOD = 6
D_ML = 512
N_HEADS = 4
HEAD = 128
D_RG = 512
N_RG_BLOCKS = 8
RG_BLOCK = 64
RG_C = 8.0
N_GATE = 16
N_EXPERTS = 64
N_GROUPS = 8
GROUP = 8
TOPK_GROUPS = 4
TOP_K = 8
D_EXPERT = 256
ROUTED_SCALE = 2.5
EPS = 1e-6
GRID_W = 64
POS_BASE = 10000.0

RG_SEG_PAD = 8
SC_WINDOW = 128
SC_PACK_WORDS = 65536
D_PACK = D_MODEL // 2
EXPERT_ROWS = 512
EXPERT_BLOCKS_PER_STEP = 4
EXPERT_RING = 8

SUBLANES = 8
LANES = 128
VMEM_LIMIT = 48 * 1024 * 1024


def _params(*sem, vmem=VMEM_LIMIT):
    return pltpu.CompilerParams(dimension_semantics=sem, vmem_limit_bytes=vmem)


def _tiles(t, per_sequence_mod):
    cap = t if per_sequence_mod else 1 << 30
    return dict(
        tok=min(512, cap),
        chunk=min(256, t),
        scan=min(512, t),
    )


def _silu(x):
    return x * jax.nn.sigmoid(x)


def _softplus(x):
    return jnp.maximum(x, 0.0) + jnp.log1p(jnp.exp(-jnp.abs(x)))


def _rms(x, g):
    return x * lax.rsqrt(jnp.mean(x * x, axis=-1, keepdims=True) + EPS) * g


def _bf16_pieces(x):
    hi = x.astype(BF16)
    r = x - hi.astype(F32)
    mid = r.astype(BF16)
    return hi, mid, (r - mid.astype(F32)).astype(BF16)


def _pack_bf16_pairs(x):
    w = x.shape[1] // 2
    hi = lax.bitcast_convert_type(x[:, :w].astype(BF16).astype(F32), I32)
    lo = lax.bitcast_convert_type(x[:, w:].astype(BF16).astype(F32), I32)
    return hi | lax.shift_right_logical(lo, jnp.full(lo.shape, 16, I32))


def _unpack_bf16_pairs(p):
    hi = lax.bitcast_convert_type(p & jnp.int32(-65536), F32)
    lo = lax.bitcast_convert_type(lax.shift_left(p, jnp.full(p.shape, 16, I32)), F32)
    return hi, lo


def _ada_kernel(c_ref, w_ref, b_ref, o_ref):
    s = _silu(c_ref[...])
    o_ref[...] = jnp.dot(s, w_ref[...], precision=HIGHEST, preferred_element_type=F32) + b_ref[...]


def _ada(cvecs, w_ada, b_ada):
    n_out = w_ada.shape[1]
    tn = 1536
    return pl.pallas_call(
        _ada_kernel,
        grid=(n_out // tn,),
        in_specs=[
            pl.BlockSpec((SUBLANES, D_MODEL), lambda j: (0, 0)),
            pl.BlockSpec((D_MODEL, tn), lambda j: (0, j)),
            pl.BlockSpec((1, tn), lambda j: (0, j)),
        ],
        out_specs=pl.BlockSpec((SUBLANES, tn), lambda j: (0, j)),
        out_shape=jax.ShapeDtypeStruct((SUBLANES, n_out), F32),
        name="ada",
        compiler_params=_params("arbitrary"),
    )(cvecs, w_ada, b_ada.reshape(1, n_out))


def _pos_tile(rt_ref, ct_ref):
    left = jnp.concatenate([jnp.broadcast_to(rt_ref[r:r + 1, :], (GRID_W, rt_ref.shape[1]))
                            for r in range(rt_ref.shape[0])], axis=0)
    right = jnp.concatenate([ct_ref[...]] * rt_ref.shape[0], axis=0)
    return jnp.concatenate([left, right], axis=1)


def _in_proj_kernel(x_ref, rt_ref, ct_ref, mod_ref, n1_ref, wq_ref, wr_ref, wgh_ref, gb_ref,
                    qt_ref, k_ref, kt_ref, v_ref, vt_ref, ot_ref, xr_ref, gr_ref, g_ref, gt_ref):
    x = x_ref[...] + _pos_tile(rt_ref, ct_ref)
    hn = _rms(x, n1_ref[...]) * (1.0 + mod_ref[0, 1:2, :]) + mod_ref[0, 0:1, :]
    hb = hn.astype(BF16)
    z = jnp.dot(hb, wq_ref[...], preferred_element_type=F32)
    k = z[:, D_ML:2 * D_ML] * (HEAD ** -0.5)
    v = z[:, 2 * D_ML:3 * D_ML]
    qt_ref[...] = z[:, 0:D_ML].T.astype(BF16)
    k_ref[...] = k.astype(BF16)
    kt_ref[...] = k.T.astype(BF16)
    v_ref[...] = v.astype(BF16)
    vt_ref[...] = v.T.astype(BF16)
    ot_ref[...] = z[:, 3 * D_ML:4 * D_ML].T
    zr = jnp.dot(hb, wr_ref[...], preferred_element_type=F32)
    xr_ref[...] = zr[:, 0:D_RG]
    gr_ref[...] = zr[:, D_RG:2 * D_RG]
    zg = zr[:, 2 * D_RG:2 * D_RG + LANES]
    h_lo = (hn - hb.astype(F32)).astype(BF16)
    g = (zg + pltpu.roll(zg, LANES - N_GATE, 1)
         + jnp.dot(h_lo, wgh_ref[...], preferred_element_type=F32) + gb_ref[...])
    col = lax.broadcasted_iota(I32, g.shape, 1)
    g = jnp.where((col & 4) != 0, -_softplus(-g), g)
    g_ref[...] = g[:, 0:N_GATE]
    gt_ref[...] = g.T[0:N_GATE, :]


def _mod_index(bm, tm, t):
    if bm > 1:
        return lambda i: ((i * tm) // t, 0, 0)
    return lambda i: (0, 0, 0)


def _pos_specs(rtab, ctab, tm):
    rows = tm // GRID_W
    period = rtab.shape[0] // rows
    return [pl.BlockSpec((rows, D_MODEL // 2), lambda i: (i % period, 0)),
            pl.BlockSpec((GRID_W, D_MODEL // 2), lambda i: (0, 0))]


def _in_proj(x2d, rtab, ctab, mod, t, tm, norm1, wq, wr, wgh, gbias):
    n = x2d.shape[0]
    tok = lambda i: (i, 0)
    tok_t = lambda i: (0, i)
    const = lambda i: (0, 0)
    f = lambda w: jax.ShapeDtypeStruct((n, w), F32)
    row16 = jax.ShapeDtypeStruct((n, D_ML), BF16)
    col16 = jax.ShapeDtypeStruct((D_ML, n), BF16)
    return pl.pallas_call(
        _in_proj_kernel,
        grid=(n // tm,),
        in_specs=[pl.BlockSpec((tm, D_MODEL), tok)] + _pos_specs(rtab, ctab, tm) + [
            pl.BlockSpec((1, N_MOD, D_MODEL), _mod_index(mod.shape[0], tm, t)),
            pl.BlockSpec((1, D_MODEL), const),
            pl.BlockSpec((D_MODEL, 4 * D_ML), const),
            pl.BlockSpec((D_MODEL, 2 * D_RG + LANES), const),
            pl.BlockSpec((D_MODEL, LANES), const),
            pl.BlockSpec((1, LANES), const),
        ],
        out_specs=[pl.BlockSpec((D_ML, tm), tok_t), pl.BlockSpec((tm, D_ML), tok), pl.BlockSpec((D_ML, tm), tok_t),
                   pl.BlockSpec((tm, D_ML), tok), pl.BlockSpec((D_ML, tm), tok_t), pl.BlockSpec((D_ML, tm), tok_t),
                   pl.BlockSpec((tm, D_RG), tok), pl.BlockSpec((tm, D_RG), tok),
                   pl.BlockSpec((tm, N_GATE), tok), pl.BlockSpec((N_GATE, tm), tok_t)],
        out_shape=[col16, row16, col16, row16, col16, jax.ShapeDtypeStruct((D_ML, n), F32),
                   f(D_RG), f(D_RG), f(N_GATE), jax.ShapeDtypeStruct((N_GATE, n), F32)],
        name="in_proj",
        compiler_params=_params("arbitrary"),
    )(x2d, rtab, ctab, mod, norm1, wq, wr, wgh, gbias)


def _mlstm_kernel(chunk, nc, emit_state, *refs):
    (qtf, kf, ktf, vf, vtf, gf, gtf, qtb, kb, ktb, vb, vtb, gb, gtb, c0_ref, n0_ref, m0_ref) = refs[:17]
    if emit_state:
        hf_ref, hb_ref, c_out, n_out, m_out, c_sc, n_sc, m_sc = refs[17:]
    else:
        hf_ref, hb_ref, c_sc, n_sc, m_sc = refs[17:]
    i = pl.program_id(1)

    @pl.when(i == 0)
    def _():
        c_sc[...] = c0_ref[0]
        n_sc[...] = n0_ref[0]
        m_sc[...] = m0_ref[0]

    key = lax.broadcasted_iota(I32, (chunk, chunk), 0)
    qry = lax.broadcasted_iota(I32, (chunk, chunk), 1)
    hd = []
    for d, (qt_ref, k_ref, kt_ref, v_ref, vt_ref, g_ref, gt_ref, h_ref) in enumerate(
            ((qtf, kf, ktf, vf, vtf, gf, gtf, hf_ref), (qtb, kb, ktb, vb, vtb, gb, gtb, hb_ref))):
        tri = (key <= qry) if d == 0 else (key >= qry)
        tri_t = (qry <= key) if d == 0 else (qry >= key)
        g = g_ref[...]
        gt = gt_ref[...]
        rows3 = jnp.dot(jnp.concatenate(_bf16_pieces(gt), axis=0), tri.astype(BF16), preferred_element_type=F32)
        brow = rows3[0:N_GATE] + rows3[N_GATE:2 * N_GATE] + rows3[2 * N_GATE:3 * N_GATE]
        tri_t16 = tri_t.astype(BF16)
        bcol = sum(jnp.dot(tri_t16, piece, preferred_element_type=F32) for piece in _bf16_pieces(g))
        blast = bcol[chunk - 1:chunk, :] if d == 0 else bcol[0:1, :]
        for h in range(N_HEADS):
            ci = d * 8 + h
            cf = d * 8 + 4 + h
            j = d * N_HEADS + h
            sl = slice(h * HEAD, (h + 1) * HEAD)
            hd.append(dict(
                j=j, sl=sl, tri=tri, h_ref=h_ref, qt=qt_ref[sl, :], k=k_ref[:, sl], kt=kt_ref[sl, :],
                v=v_ref[:, sl], vt=vt_ref[sl, :], b_row=brow[cf:cf + 1, :],
                gate_col=g[:, ci:ci + 1] - bcol[:, cf:cf + 1], gate_row=gt[ci:ci + 1, :] - brow[cf:cf + 1, :],
                b_last=blast[:, cf:cf + 1], m_prev=m_sc[j:j + 1, 0:1], c_prev=c_sc[j], n_prev=n_sc[j:j + 1, :]))
    for x in hd:
        top = jnp.max(jnp.where(x['tri'], x['gate_col'], -jnp.inf), axis=0, keepdims=True)
        mx = jnp.maximum(x['m_prev'], top)
        x['dm'] = jnp.exp(jnp.where(x['tri'], x['gate_col'] - mx, -jnp.inf))
        x['w_inter'] = jnp.exp(x['m_prev'] - mx)
        x['floor'] = jnp.exp(-(x['b_row'] + mx))
    for x in hd:
        x['st'] = jnp.dot(x['k'], x['qt'], preferred_element_type=F32) * x['dm']
    for x in hd:
        inter = lax.dot_general(x['c_prev'].astype(BF16), x['qt'], (((0,), (0,)), ((), ())),
                                preferred_element_type=F32)
        num = x['w_inter'] * inter + jnp.dot(x['vt'], x['st'].astype(BF16), preferred_element_type=F32)
        qn = jnp.dot(jnp.broadcast_to(x['n_prev'], (SUBLANES, HEAD)).astype(BF16), x['qt'],
                     preferred_element_type=F32)[0:1, :]
        den = x['w_inter'] * qn + jnp.sum(x['st'], axis=0, keepdims=True)
        x['h_ref'][x['sl'], :] = num / jnp.maximum(jnp.abs(den), x['floor'])
    for x in hd:
        j = x['j']
        log_w = x['b_last'] + x['gate_row']
        m_new = jnp.maximum(x['b_last'] + x['m_prev'], jnp.max(log_w, axis=1, keepdims=True))
        decay = jnp.exp(x['b_last'] + x['m_prev'] - m_new)
        w_row = jnp.exp(log_w - m_new)
        kwt = (x['kt'].astype(F32) * w_row).astype(BF16)
        c_sc[j] = decay * x['c_prev'] + jnp.dot(kwt, x['v'], preferred_element_type=F32)
        n_sc[j:j + 1, :] = decay * x['n_prev'] + jnp.dot(
            jnp.broadcast_to(w_row, (SUBLANES, chunk)).astype(BF16), x['k'], preferred_element_type=F32)[0:1, :]
        m_sc[j:j + 1, :] = jnp.broadcast_to(m_new, (1, HEAD))

    if emit_state:
        @pl.when(i == nc - 1)
        def _():
            c_out[0] = c_sc[...]
            n_out[0] = n_sc[...]
            m_out[0] = m_sc[...]


def _state_index(bm):
    if bm > 1:
        return lambda b, i: (b,) + (0,) * 3, lambda b, i: (b, 0, 0)
    return lambda b, i: (0,) * 4, lambda b, i: (0, 0, 0)


def _mlstm(qt, k, kt, v, vt, g, gt, c0, n0, m0, b, t, chunk, emit_state):
    nc = t // chunk
    n = b * t
    nd = 2 * N_HEADS
    fwd = lambda bi, i: (bi * nc + i, 0)
    bwd = lambda bi, i: (bi * nc + nc - 1 - i, 0)
    fwd_t = lambda bi, i: (0, bi * nc + i)
    bwd_t = lambda bi, i: (0, bi * nc + nc - 1 - i)
    c_idx, n_idx = _state_index(c0.shape[0])
    rows = lambda m: pl.BlockSpec((chunk, D_ML), m)
    cols = lambda m: pl.BlockSpec((D_ML, chunk), m)
    one_dir = lambda m, mt: [cols(mt), rows(m), cols(mt), rows(m), cols(mt),
                             pl.BlockSpec((chunk, N_GATE), m), pl.BlockSpec((N_GATE, chunk), mt)]
    in_specs = (one_dir(fwd, fwd_t) + one_dir(bwd, bwd_t)
                + [pl.BlockSpec((1, nd, HEAD, HEAD), c_idx),
                   pl.BlockSpec((1, nd, HEAD), n_idx), pl.BlockSpec((1, nd, HEAD), n_idx)])
    out_specs = [cols(fwd_t), cols(bwd_t)]
    out_shape = [jax.ShapeDtypeStruct((D_ML, n), F32)] * 2
    if emit_state:
        out_specs += [pl.BlockSpec((1, nd, HEAD, HEAD), lambda bi, i: (bi, 0, 0, 0)),
                      pl.BlockSpec((1, nd, HEAD), lambda bi, i: (bi, 0, 0)),
                      pl.BlockSpec((1, nd, HEAD), lambda bi, i: (bi, 0, 0))]
        out_shape += [jax.ShapeDtypeStruct((b, nd, HEAD, HEAD), F32),
                      jax.ShapeDtypeStruct((b, nd, HEAD), F32),
                      jax.ShapeDtypeStruct((b, nd, HEAD), F32)]
    return pl.pallas_call(
        functools.partial(_mlstm_kernel, chunk, nc, emit_state),
        grid=(b, nc),
        in_specs=in_specs,
        out_specs=out_specs,
        out_shape=out_shape,
        scratch_shapes=[pltpu.VMEM((nd, HEAD, HEAD), F32), pltpu.VMEM((nd, HEAD), F32),
                        pltpu.VMEM((nd, HEAD), F32)],
        name="mlstm",
        compiler_params=_params("arbitrary", "arbitrary"),
    )(qt, k, kt, v, vt, g, gt, qt, k, kt, v, vt, g, gt, c0, n0, m0)


def _neg_expm1_2x(x):
    t = jnp.tanh(x)
    return -2.0 * t / (1.0 - t)


def _rglru_kernel(tb, nb, emit_state, *refs):
    (xf, xf_prev, xf_next, xb, xb_prev, xb_next, h0_ref, w_ref, bias_ref, lam_ref, cw_ref, cb_ref) = refs[:12]
    n_out = 3 if emit_state else 2
    hf_ref, hb_ref = refs[12:14]
    hfin_ref = refs[14] if emit_state else None
    carry, af_sc, uf_sc, ab_sc, ub_sc, hf_sc, pf_sc, hb_sc, pb_sc = refs[12 + n_out:]
    i = pl.program_id(1)

    @pl.when(i == 0)
    def _():
        carry[...] = h0_ref[0]

    row8 = lax.broadcasted_iota(I32, (SUBLANES, D_RG), 0)
    cw = cw_ref[...]
    log_a_per_r = -RG_C * _softplus(-lam_ref[...])

    def taps(xm2, xm1, x0, xp1):
        return cb_ref[...] + xm2 * cw[0:1, :] + xm1 * cw[1:2, :] + x0 * cw[2:3, :] + xp1 * cw[3:4, :]

    def conv(main_ref, prev_ref, next_ref, first, last):
        main = main_ref[...]
        prev = jnp.where(first, 0.0, prev_ref[...])
        nxt = jnp.where(last, 0.0, next_ref[...])
        body = taps(pltpu.roll(main, 2, 0), pltpu.roll(main, 1, 0), main, pltpu.roll(main, tb - 1, 0))
        e = SUBLANES
        head, tail = main[0:e, :], main[tb - e:tb, :]
        before_tail = main[tb - 2 * e:tb - e, :]
        fix_head = taps(
            jnp.where(row8 == 0, prev[6:7, :], jnp.where(row8 == 1, prev[7:8, :], pltpu.roll(head, 2, 0))),
            jnp.where(row8 == 0, prev[7:8, :], pltpu.roll(head, 1, 0)), head,
            jnp.where(row8 == e - 1, main[e:e + 1, :], pltpu.roll(head, e - 1, 0)))
        fix_tail = taps(
            jnp.where(row8 == 0, before_tail[6:7, :],
                      jnp.where(row8 == 1, before_tail[7:8, :], pltpu.roll(tail, 2, 0))),
            jnp.where(row8 == 0, before_tail[7:8, :], pltpu.roll(tail, 1, 0)), tail,
            jnp.where(row8 == e - 1, nxt[0:1, :], pltpu.roll(tail, e - 1, 0)))
        return jnp.concatenate([fix_head, body[e:tb - e, :], fix_tail], axis=0)

    def recurrence_terms(xc, d):
        z = jnp.dot(xc.astype(BF16), w_ref[:, d * 2 * D_RG:(d + 1) * 2 * D_RG],
                    preferred_element_type=F32) + bias_ref[:, d * 2 * D_RG:(d + 1) * 2 * D_RG]
        r = jax.nn.sigmoid(z[:, 0:D_RG])
        ig = jax.nn.sigmoid(z[:, D_RG:2 * D_RG])
        log_a = r * log_a_per_r[d:d + 1, :]
        a = jnp.exp(log_a)
        u = jnp.sqrt(_neg_expm1_2x(log_a)) * (ig * xc)
        return a, u

    a_f, u_f = recurrence_terms(conv(xf, xf_prev, xf_next, i == 0, i == nb - 1), 0)
    a_b, u_b = recurrence_terms(conv(xb, xb_prev, xb_next, i == nb - 1, i == 0), 1)
    seg = tb // SUBLANES
    pitch = seg + RG_SEG_PAD
    ncol = D_RG // LANES
    for lc in range(ncol):
        lanes = slice(lc * LANES, (lc + 1) * LANES)
        for s in range(SUBLANES):
            src = slice(s * seg, (s + 1) * seg)
            dst = slice(s * pitch, s * pitch + seg)
            af_sc[lc, dst, :], uf_sc[lc, dst, :] = a_f[src, lanes], u_f[src, lanes]
            ab_sc[lc, dst, :], ub_sc[lc, dst, :] = a_b[src, lanes], u_b[src, lanes]
    slab = lambda k: (slice(None), pl.ds(k, SUBLANES, stride=pitch), slice(None))
    hf = jnp.zeros((ncol, SUBLANES, LANES), F32)
    hb = jnp.zeros((ncol, SUBLANES, LANES), F32)
    pf = jnp.ones((ncol, SUBLANES, LANES), F32)
    pb = jnp.ones((ncol, SUBLANES, LANES), F32)
    for k in range(seg):
        kb = seg - 1 - k
        ak = af_sc[slab(k)]
        hf = ak * hf + uf_sc[slab(k)]
        pf = pf * ak
        hf_sc[slab(k)] = hf
        pf_sc[slab(k)] = pf
        ak = ab_sc[slab(kb)]
        hb = ak * hb + ub_sc[slab(kb)]
        pb = pb * ak
        hb_sc[slab(kb)] = hb
        pb_sc[slab(kb)] = pb
    for lc in range(ncol):
        lanes = slice(lc * LANES, (lc + 1) * LANES)
        c = carry[0:1, lanes]
        cin_f = []
        for s in range(SUBLANES):
            cin_f.append(c)
            c = pf[lc, s:s + 1, :] * c + hf[lc, s:s + 1, :]
        carry[0:1, lanes] = c
        c = carry[1:2, lanes]
        cin_b = [None] * SUBLANES
        for s in reversed(range(SUBLANES)):
            cin_b[s] = c
            c = pb[lc, s:s + 1, :] * c + hb[lc, s:s + 1, :]
        carry[1:2, lanes] = c
        for s in range(SUBLANES):
            rows = slice(s * seg, (s + 1) * seg)
            src = slice(s * pitch, s * pitch + seg)
            hf_ref[rows, lanes] = hf_sc[lc, src, :] + pf_sc[lc, src, :] * cin_f[s]
            hb_ref[rows, lanes] = hb_sc[lc, src, :] + pb_sc[lc, src, :] * cin_b[s]

    if emit_state:
        @pl.when(i == nb - 1)
        def _():
            hfin_ref[0] = carry[...]


def _rglru(xr, h0, wbd, bias, lam, cw, cb, b, t, tb, emit_state):
    nb = t // tb
    n = b * t
    r8 = tb // SUBLANES
    last8 = n // SUBLANES - 1
    fwd = lambda bi, i: (bi * nb + i, 0)
    bwd = lambda bi, i: (bi * nb + nb - 1 - i, 0)
    fwd_prev = lambda bi, i: (jnp.maximum((bi * nb + i) * r8 - 1, 0), 0)
    fwd_next = lambda bi, i: (jnp.minimum((bi * nb + i + 1) * r8, last8), 0)
    bwd_prev = lambda bi, i: (jnp.maximum((bi * nb + nb - 1 - i) * r8 - 1, 0), 0)
    bwd_next = lambda bi, i: (jnp.minimum((bi * nb + nb - i) * r8, last8), 0)
    const = lambda bi, i: (0, 0)
    h_idx = (lambda bi, i: (bi, 0, 0)) if h0.shape[0] > 1 else (lambda bi, i: (0, 0, 0))
    halo = lambda m: pl.BlockSpec((SUBLANES, D_RG), m)
    in_specs = [pl.BlockSpec((tb, D_RG), fwd), halo(fwd_prev), halo(fwd_next),
                pl.BlockSpec((tb, D_RG), bwd), halo(bwd_prev), halo(bwd_next),
                pl.BlockSpec((1, 2, D_RG), h_idx),
                pl.BlockSpec((D_RG, 4 * D_RG), const), pl.BlockSpec((1, 4 * D_RG), const),
                pl.BlockSpec((2, D_RG), const), pl.BlockSpec((4, D_RG), const), pl.BlockSpec((1, D_RG), const)]
    out_specs = [pl.BlockSpec((tb, D_RG), fwd), pl.BlockSpec((tb, D_RG), bwd)]
    out_shape = [jax.ShapeDtypeStruct((n, D_RG), F32)] * 2
    if emit_state:
        out_specs.append(pl.BlockSpec((1, 2, D_RG), lambda bi, i: (bi, 0, 0)))
        out_shape.append(jax.ShapeDtypeStruct((b, 2, D_RG), F32))
    return pl.pallas_call(
        functools.partial(_rglru_kernel, tb, nb, emit_state),
        grid=(b, nb),
        in_specs=in_specs,
        out_specs=out_specs,
        out_shape=out_shape,
        scratch_shapes=[pltpu.VMEM((2, D_RG), F32)]
        + [pltpu.VMEM((D_RG // LANES, tb + SUBLANES * RG_SEG_PAD, LANES), F32)] * 8,
        name="rglru",
        compiler_params=_params("arbitrary", "arbitrary"),
    )(xr, xr, xr, xr, xr, xr, h0, wbd, bias, lam, cw, cb)


def _route(s, sb):
    tm = s.shape[1]
    neg = -jnp.inf
    sub = lax.broadcasted_iota(I32, (GROUP, tm), 0)
    blocks = [sb[gi * GROUP:(gi + 1) * GROUP, :] for gi in range(N_GROUPS)]
    gscore = []
    for blk in blocks:
        m1 = jnp.max(blk, axis=0, keepdims=True)
        first = jnp.min(jnp.where(blk == m1, sub, GROUP), axis=0, keepdims=True)
        m2 = jnp.max(jnp.where(sub == first, neg, blk), axis=0, keepdims=True)
        gscore.append(m1 + m2)
    masked = []
    for gi in range(N_GROUPS):
        rank = jnp.zeros((1, tm), F32)
        for gj in range(N_GROUPS):
            if gj == gi:
                continue
            ahead = (gscore[gj] >= gscore[gi]) if gj < gi else (gscore[gj] > gscore[gi])
            rank = rank + jnp.where(ahead, 1.0, 0.0)
        masked.append(jnp.where(rank < TOPK_GROUPS, blocks[gi], neg))
    v = jnp.concatenate(masked, axis=0)
    eid = lax.broadcasted_iota(I32, (N_EXPERTS, tm), 0)
    sel = jnp.zeros((N_EXPERTS, tm), F32)
    picks = []
    for _ in range(TOP_K):
        mx = jnp.max(v, axis=0, keepdims=True)
        idx = jnp.min(jnp.where(v == mx, eid, N_EXPERTS), axis=0, keepdims=True)
        pick = eid == idx
        picks.append(pick)
        sel = jnp.where(pick, 1.0, sel)
        v = jnp.where(pick, neg, v)
    ws = s * sel
    return ws / jnp.sum(ws, axis=0, keepdims=True) * ROUTED_SCALE, sel, picks


def _mix_out_kernel(hmf_ref, hmb_ref, ot_ref, hrf_ref, hrb_ref, gr_ref, x_ref, rt_ref, ct_ref, mod_ref,
                    mln_ref, rgn_ref, wo_ml_ref, wo_rg_ref, n2_ref, rwt_ref, rb_ref, cnt0_ref,
                    x1_ref, hn2p_ref, ek_ref, pk_ref, wtok_ref, cnt_ref, cnt_sc):
    i = pl.program_id(0)
    tm = x_ref.shape[0]

    @pl.when(i == 0)
    def _():
        cnt_sc[...] = cnt0_ref[...].astype(F32)

    hm = hmf_ref[...] + hmb_ref[...]
    heads = []
    for h in range(N_HEADS):
        seg = hm[h * HEAD:(h + 1) * HEAD, :]
        heads.append(seg * lax.rsqrt(jnp.mean(seg * seg, axis=0, keepdims=True) + EPS))
    y_ml_t = jnp.concatenate(heads, axis=0) * mln_ref[...] * jax.nn.sigmoid(ot_ref[...])
    y_rg = _rms(hrf_ref[...] + hrb_ref[...], rgn_ref[...]) * jax.nn.gelu(gr_ref[...])
    mix = (lax.dot_general(y_ml_t.astype(BF16), wo_ml_ref[...], (((0,), (0,)), ((), ())),
                           preferred_element_type=F32)
           + jnp.dot(y_rg.astype(BF16), wo_rg_ref[...], preferred_element_type=F32))
    x1 = x_ref[...] + _pos_tile(rt_ref, ct_ref) + mod_ref[0, 2:3, :] * mix
    x1_ref[...] = x1
    hn2 = _rms(x1, n2_ref[...]) * (1.0 + mod_ref[0, 4:5, :]) + mod_ref[0, 3:4, :]
    hn2p_ref[...] = _pack_bf16_pairs(hn2)
    hb = hn2.astype(BF16)
    h_lo = (hn2 - hb.astype(F32)).astype(BF16)
    nt = (((1,), (1,)), ((), ()))
    two = lax.dot_general(rwt_ref[...], hb, nt, preferred_element_type=F32)
    logits_t = (two[0:N_EXPERTS, :] + two[N_EXPERTS:2 * N_EXPERTS, :]
                + lax.dot_general(rwt_ref[0:N_EXPERTS, :], h_lo, nt, preferred_element_type=F32))
    s = jax.nn.sigmoid(logits_t)
    wt, sel, picks = _route(s, s + rb_ref[...])

    earlier = (lax.broadcasted_iota(I32, (tm, tm), 0) < lax.broadcasted_iota(I32, (tm, tm), 1))
    prefix = jnp.dot(sel.astype(BF16), earlier.astype(BF16), preferred_element_type=F32)
    pos_all = cnt_sc[:, 0:1] + prefix
    eid = lax.broadcasted_iota(I32, (N_EXPERTS, tm), 0)
    eid_f = eid.astype(F32)
    row8 = lax.broadcasted_iota(I32, (TOP_K, tm), 0)
    ek = jnp.zeros((TOP_K, tm), F32)
    pk = jnp.zeros((TOP_K, tm), F32)
    wk = jnp.zeros((N_EXPERTS, tm), F32)
    for k, pick in enumerate(picks):
        take = lambda a: jnp.sum(jnp.where(pick, a, 0.0), axis=0, keepdims=True)
        ek = jnp.where(row8 == k, take(eid_f), ek)
        pk = jnp.where(row8 == k, take(pos_all), pk)
        wk = jnp.where(eid == k, take(wt), wk)
    ek_ref[...] = ek.astype(I32)
    pk_ref[...] = pk.astype(I32)
    wtok_ref[...] = wk.T
    cnt_sc[...] += jnp.broadcast_to(jnp.sum(sel, axis=1, keepdims=True), cnt_sc.shape)

    @pl.when(i == pl.num_programs(0) - 1)
    def _():
        cnt_ref[...] = cnt_sc[...].astype(I32)


def _mix_out(hmf, hmb, ot, hrf, hrb, gr, x2d, rtab, ctab, mod, t, tm, mln, rgn, wo_ml, wo_rg, norm2, rwt, rbias,
             cnt0):
    n = x2d.shape[0]
    tok = lambda i: (i, 0)
    tok_t = lambda i: (0, i)
    const = lambda i: (0, 0)
    return pl.pallas_call(
        _mix_out_kernel,
        grid=(n // tm,),
        in_specs=[pl.BlockSpec((D_ML, tm), tok_t)] * 3 + [pl.BlockSpec((tm, D_RG), tok)] * 3 + [
            pl.BlockSpec((tm, D_MODEL), tok)] + _pos_specs(rtab, ctab, tm) + [
            pl.BlockSpec((1, N_MOD, D_MODEL), _mod_index(mod.shape[0], tm, t)),
            pl.BlockSpec((D_ML, 1), const), pl.BlockSpec((1, D_RG), const),
            pl.BlockSpec((D_ML, D_MODEL), const), pl.BlockSpec((D_RG, D_MODEL), const),
            pl.BlockSpec((1, D_MODEL), const),
            pl.BlockSpec((2 * N_EXPERTS, D_MODEL), const), pl.BlockSpec((N_EXPERTS, 1), const),
            pl.BlockSpec((N_EXPERTS, LANES), const),
        ],
        out_specs=[pl.BlockSpec((tm, D_MODEL), tok), pl.BlockSpec((tm, D_PACK), tok),
                   pl.BlockSpec((TOP_K, tm), tok_t), pl.BlockSpec((TOP_K, tm), tok_t),
                   pl.BlockSpec((tm, N_EXPERTS), tok), pl.BlockSpec((N_EXPERTS, LANES), const)],
        out_shape=[jax.ShapeDtypeStruct((n, D_MODEL), F32), jax.ShapeDtypeStruct((n, D_PACK), I32),
                   jax.ShapeDtypeStruct((TOP_K, n), I32), jax.ShapeDtypeStruct((TOP_K, n), I32),
                   jax.ShapeDtypeStruct((n, N_EXPERTS), F32), jax.ShapeDtypeStruct((N_EXPERTS, LANES), I32)],
        scratch_shapes=[pltpu.VMEM((N_EXPERTS, LANES), F32)],
        name="mix_out",
        compiler_params=_params("arbitrary"),
    )(hmf, hmb, ot, hrf, hrb, gr, x2d, rtab, ctab, mod, mln, rgn, wo_ml, wo_rg, norm2, rwt, rbias, cnt0)


def _sc_mesh():
    return plsc.VectorSubcoreMesh(core_axis_name="core", subcore_axis_name="subcore")


def _sc_worker():
    info = plsc.get_sparse_core_info()
    return lax.axis_index("subcore") * info.num_cores + lax.axis_index("core"), info.num_cores * info.num_subcores


def _sc_dispatch(xp, dest3, n_slots):
    n, w = xp.shape
    nwin = n // SC_WINDOW

    @pl.kernel(out_type=jax.ShapeDtypeStruct((n_slots, w), xp.dtype), mesh=_sc_mesh(),
               scratch_types=[pltpu.VMEM((SC_WINDOW, w), xp.dtype), pltpu.VMEM((TOP_K, SC_WINDOW), I32)],
               name="sc_dispatch")
    def k(x_hbm, i_hbm, o_hbm, x_v, i_v):
        wid, nworkers = _sc_worker()
        per = nwin // nworkers

        @pl.loop(0, per)
        def _(s):
            win = wid * per + s
            pltpu.sync_copy(x_hbm.at[pl.ds(win * SC_WINDOW, SC_WINDOW)], x_v)
            pltpu.sync_copy(i_hbm.at[win], i_v)
            for j in range(TOP_K):
                pltpu.sync_copy(x_v, o_hbm.at[i_v.at[j]])

    return k(xp, dest3)


def _sc_pack_rows(w):
    r, c = w.shape
    half = c // 2
    lanes = plsc.get_sparse_core_info().num_lanes
    chunk = SC_PACK_WORDS // c
    nchunks = r // chunk

    def rne_high(x):
        u = plsc.bitcast(x, I32)
        u = u + 0x7FFF + (lax.shift_right_logical(u, jnp.full(u.shape, 16, I32)) & 1)
        return u & jnp.int32(-65536)

    @pl.kernel(out_type=jax.ShapeDtypeStruct((r, half), I32), mesh=_sc_mesh(),
               scratch_types=[pltpu.VMEM((chunk, c), F32), pltpu.VMEM((chunk, half), I32)],
               compiler_params=pltpu.CompilerParams(needs_layout_passes=False), name="sc_pack")
    def k(w_hbm, o_hbm, in_v, out_v):
        wid, nworkers = _sc_worker()
        per = nchunks // nworkers

        @pl.loop(0, per)
        def _(s):
            first = (wid * per + s) * chunk
            pltpu.sync_copy(w_hbm.at[pl.ds(first, chunk)], in_v)

            @pl.loop(0, chunk)
            def _(row):
                for j in range(half // lanes):
                    at = lambda off: (row, pl.ds(off + j * lanes, lanes))
                    hi = rne_high(in_v.at[*at(0)][...])
                    lo = rne_high(in_v.at[*at(half)][...])
                    out_v.at[*at(0)][...] = hi | lax.shift_right_logical(lo, jnp.full(lo.shape, 16, I32))

            pltpu.sync_copy(out_v, o_hbm.at[pl.ds(first, chunk)])

    return k(w)


def _sc_combine_gather(ys, dest3):
    nwin = dest3.shape[0]
    w = ys.shape[1]

    @pl.kernel(out_type=jax.ShapeDtypeStruct((nwin, TOP_K, SC_WINDOW, w), ys.dtype), mesh=_sc_mesh(),
               scratch_types=[pltpu.VMEM((SC_WINDOW, w), ys.dtype), pltpu.VMEM((TOP_K, SC_WINDOW), I32)],
               name="sc_combine")
    def k(y_hbm, i_hbm, o_hbm, y_v, i_v):
        wid, nworkers = _sc_worker()
        per = nwin // nworkers

        @pl.loop(0, per)
        def _(s):
            win = wid * per + s
            pltpu.sync_copy(i_hbm.at[win], i_v)
            for j in range(TOP_K):
                pltpu.sync_copy(y_hbm.at[i_v.at[j]], y_v)
                pltpu.sync_copy(y_v, o_hbm.at[win, j])

    return k(ys, dest3)


def _swiglu(x, w13):
    h = jnp.dot(x, w13, preferred_element_type=F32)
    return _silu(h[:, 0:D_EXPERT]) * h[:, D_EXPERT:2 * D_EXPERT]


def _unpack_rows_bf16(p):
    hi, lo = _unpack_bf16_pairs(p)
    return jnp.concatenate([hi.astype(BF16), lo.astype(BF16)], axis=1)


def _expert_kernel(rows, rank_ref, expert_ref, nu_ref, nr_ref, x_ref, w1_hbm, w3_hbm, w2_hbm, y_ref,
                   w1buf, w3buf, w2buf, sem, progress):
    b = pl.program_id(0)
    g = EXPERT_BLOCKS_PER_STEP
    n_ranks = nr_ref[0]

    @pl.when(b == 0)
    def _():
        progress[0] = 0
        progress[1] = 0

    def copies(r):
        slot = lax.rem(r, EXPERT_RING)
        e = expert_ref[r]
        return (pltpu.make_async_copy(w1_hbm.at[e], w1buf.at[slot], sem.at[slot, 0]),
                pltpu.make_async_copy(w3_hbm.at[e], w3buf.at[slot], sem.at[slot, 1]),
                pltpu.make_async_copy(w2_hbm.at[e], w2buf.at[slot], sem.at[slot, 2]))

    def start(r, carry):
        for cp in copies(r):
            cp.start()
        return carry

    def wait(r, carry):
        for cp in copies(r):
            cp.wait()
        return carry

    first = rank_ref[b * g]
    last = rank_ref[b * g + g - 1]
    started = jnp.minimum(first + EXPERT_RING, n_ranks)
    lax.fori_loop(progress[0], started, start, 0)
    progress[0] = jnp.maximum(progress[0], started)
    needed = jnp.where(b == pl.num_programs(0) - 1, progress[0], jnp.minimum(last + 1, n_ranks))
    lax.fori_loop(progress[1], needed, wait, 0)
    progress[1] = jnp.maximum(progress[1], needed)

    @pl.when(b * g < nu_ref[0])
    def _():
        for j in range(g):
            slot = lax.rem(rank_ref[b * g + j], EXPERT_RING)
            sl = slice(j * rows, (j + 1) * rows)
            x = _unpack_rows_bf16(x_ref[sl, :])
            h = (_silu(jnp.dot(x, _unpack_rows_bf16(w1buf[slot]), preferred_element_type=F32))
                 * jnp.dot(x, _unpack_rows_bf16(w3buf[slot]), preferred_element_type=F32))
            y_ref[sl, :] = _pack_bf16_pairs(
                jnp.dot(h.astype(BF16), _unpack_rows_bf16(w2buf[slot]), preferred_element_type=F32))


def _experts(xs, block_rank, rank_expert, n_used, n_ranks, w1, w3, w2, rows):
    g = EXPERT_BLOCKS_PER_STEP
    nb = xs.shape[0] // rows
    tok = lambda b, *_: (b, 0)
    return pl.pallas_call(
        functools.partial(_expert_kernel, rows),
        grid_spec=pltpu.PrefetchScalarGridSpec(
            num_scalar_prefetch=4,
            grid=(nb // g,),
            in_specs=[pl.BlockSpec((g * rows, D_PACK), tok)] + [pl.BlockSpec(memory_space=pl.ANY)] * 3,
            out_specs=pl.BlockSpec((g * rows, D_PACK), tok),
            scratch_shapes=[pltpu.VMEM((EXPERT_RING, D_MODEL, D_EXPERT // 2), I32),
                            pltpu.VMEM((EXPERT_RING, D_MODEL, D_EXPERT // 2), I32),
                            pltpu.VMEM((EXPERT_RING, D_EXPERT, D_PACK), I32),
                            pltpu.SemaphoreType.DMA((EXPERT_RING, 3)), pltpu.SMEM((2,), I32)],
        ),
        out_shape=jax.ShapeDtypeStruct(xs.shape, I32),
        name="experts",
        compiler_params=_params("arbitrary"),
    )(block_rank, rank_expert, n_used, n_ranks, xs, w1, w3, w2)


def _moe_out_kernel(yk_ref, hn2p_ref, wtok_ref, sw13_ref, sw2_ref, x1_ref, mod_ref, nf_ref, y_ref):
    shared = jnp.dot(_swiglu(_unpack_rows_bf16(hn2p_ref[...]), sw13_ref[...]).astype(BF16), sw2_ref[...],
                     preferred_element_type=F32)
    w = wtok_ref[...]
    parts = []
    for wi in range(yk_ref.shape[0]):
        rows = slice(wi * SC_WINDOW, (wi + 1) * SC_WINDOW)
        a_hi = shared[rows, 0:D_PACK]
        a_lo = shared[rows, D_PACK:D_MODEL]
        for k in range(TOP_K):
            y_hi, y_lo = _unpack_bf16_pairs(yk_ref[wi, k])
            wc = w[rows, k:k + 1]
            a_hi = a_hi + wc * y_hi
            a_lo = a_lo + wc * y_lo
        parts.append(jnp.concatenate([a_hi, a_lo], axis=1))
    x2 = x1_ref[...] + mod_ref[0, 5:6, :] * jnp.concatenate(parts, axis=0)
    y_ref[...] = _rms(x2, nf_ref[...])


def _moe_out(yk, tok0, hn2p, wtok, sw13, sw2, x1, mod, t, tm, norm_final):
    n = hn2p.shape[0]
    tok = lambda i: (i, 0)
    const = lambda i: (0, 0)
    blk0 = tok0 // tm
    return pl.pallas_call(
        _moe_out_kernel,
        grid=(n // tm,),
        in_specs=[
            pl.BlockSpec((tm // SC_WINDOW, TOP_K, SC_WINDOW, D_PACK), lambda i: (i + blk0, 0, 0, 0)),
            pl.BlockSpec((tm, D_PACK), tok),
            pl.BlockSpec((tm, N_EXPERTS), tok),
            pl.BlockSpec((D_MODEL, 2 * D_EXPERT), const),
            pl.BlockSpec((D_EXPERT, D_MODEL), const),
            pl.BlockSpec((tm, D_MODEL), tok),
            pl.BlockSpec((1, N_MOD, D_MODEL), _mod_index(mod.shape[0], tm, t)),
            pl.BlockSpec((1, D_MODEL), const),
        ],
        out_specs=pl.BlockSpec((tm, D_MODEL), tok),
        out_shape=jax.ShapeDtypeStruct((n, D_MODEL), F32),
        name="moe_out",
        compiler_params=_params("arbitrary"),
    )(yk, hn2p, wtok, sw13, sw2, x1, mod, norm_final)


def _dispatch_plan(cnt, ek, pk, rows):
    n = ek.shape[1]
    nb = n * TOP_K // rows + N_EXPERTS
    nblk = (cnt + rows - 1) // rows
    block_end = jnp.cumsum(nblk)
    experts = jnp.arange(N_EXPERTS, dtype=I32)
    first_row = jnp.sum(jnp.where(ek[:, :, None] == experts, (block_end - nblk) * rows, 0), axis=-1)
    dest3 = (first_row + pk).reshape(TOP_K, n // SC_WINDOW, SC_WINDOW).transpose(1, 0, 2)
    owns = nblk > 0
    n_ranks = jnp.sum(owns.astype(I32))
    blocks = jnp.arange(nb, dtype=I32)[:, None]
    block_rank = jnp.minimum(jnp.sum((owns & (block_end <= blocks)).astype(I32), axis=1), n_ranks - 1)
    rank_expert = jnp.minimum(jnp.sum((jnp.cumsum(owns.astype(I32)) <= experts[:, None]).astype(I32), axis=1),
                              N_EXPERTS - 1)
    return (dest3, block_rank.astype(I32), rank_expert.astype(I32), block_end[-1:].astype(I32),
            n_ranks.reshape(1).astype(I32), nb * rows)


def _grid_pos_tables(n_tokens, dim):
    quarter = dim // 4
    omega = 1.0 / (POS_BASE ** (jnp.arange(quarter, dtype=F32) / quarter))
    ra = jnp.arange(n_tokens // GRID_W).astype(F32)[:, None] * omega
    ca = jnp.arange(GRID_W).astype(F32)[:, None] * omega
    return (jnp.concatenate([jnp.sin(ra), jnp.cos(ra)], axis=-1),
            jnp.concatenate([jnp.sin(ca), jnp.cos(ca)], axis=-1))


def _hi_lo_rows(w):
    hi = w.astype(BF16)
    return jnp.concatenate([hi, (w - hi.astype(F32)).astype(BF16)], axis=0)


def _block_diag(w):
    eye = jnp.eye(N_RG_BLOCKS, dtype=w.dtype)
    return jnp.einsum('nij,nm->nimj', w, eye).reshape(D_RG, D_RG)


def _layer_weights(l, norm1, w_in, mlstm_gate_bias, mlstm_norm, rg_conv_w, rg_conv_b, rg_wa, rg_ba, rg_wx,
                   rg_bx, rg_lambda, rg_norm, w_out, norm2, router_w, router_bias, exp_w1, exp_w3, exp_w2,
                   shared_w1, shared_w3, shared_w2):
    wi = w_in[l]
    c0, c1 = 4 * D_ML, 4 * D_ML + N_GATE
    wg = wi[:, c0:c1]
    wg_hi = wg.astype(BF16)
    wg_lo = (wg - wg_hi.astype(F32)).astype(BF16)
    zcols = lambda w: jnp.zeros((D_MODEL, w), BF16)
    return dict(
        norm1=norm1[l].reshape(1, D_MODEL),
        wq=wi[:, :c0].astype(BF16),
        wr=jnp.concatenate([wi[:, c1:].astype(BF16), wg_hi, wg_lo, zcols(LANES - 2 * N_GATE)], axis=1),
        wgh=jnp.concatenate([wg_hi, zcols(LANES - N_GATE)], axis=1),
        gbias=jnp.pad(mlstm_gate_bias[l].reshape(1, N_GATE), ((0, 0), (0, LANES - N_GATE))),
        mln=mlstm_norm[l].reshape(D_ML, 1),
        cw=rg_conv_w[l], cb=rg_conv_b[l].reshape(1, D_RG),
        wbd=jnp.concatenate([_block_diag(rg_wa[l, 0]), _block_diag(rg_wx[l, 0]),
                             _block_diag(rg_wa[l, 1]), _block_diag(rg_wx[l, 1])], axis=1).astype(BF16),
        rbias=jnp.concatenate([rg_ba[l, 0], rg_bx[l, 0], rg_ba[l, 1], rg_bx[l, 1]]).reshape(1, 4 * D_RG),
        lam=rg_lambda[l], rgn=rg_norm[l].reshape(1, D_RG),
        wo_ml=w_out[l, :D_ML].astype(BF16), wo_rg=w_out[l, D_ML:].astype(BF16),
        norm2=norm2[l].reshape(1, D_MODEL),
        rwt=_hi_lo_rows(router_w[l].T), rb=router_bias[l].reshape(N_EXPERTS, 1),
        w1=_sc_pack_rows(exp_w1[l].reshape(N_EXPERTS * D_MODEL, D_EXPERT)).reshape(N_EXPERTS, D_MODEL, -1),
        w3=_sc_pack_rows(exp_w3[l].reshape(N_EXPERTS * D_MODEL, D_EXPERT)).reshape(N_EXPERTS, D_MODEL, -1),
        w2=_sc_pack_rows(exp_w2[l].reshape(N_EXPERTS * D_EXPERT, D_MODEL)).reshape(N_EXPERTS, D_EXPERT, -1),
        sw13=jnp.concatenate([shared_w1[l], shared_w3[l]], axis=-1).astype(BF16),
        sw2=shared_w2[l].astype(BF16),
    )


def _mixers(x2d, pos_tables, mod, c0, n0, m0, h0, cnt0, lw, b, t, emit_state):
    tl = _tiles(t, mod.shape[0] > 1)
    tm = tl['tok']
    rtab, ctab = pos_tables
    qt, k, kt, v, vt, ot, xr, gr, g, gt = _in_proj(x2d, rtab, ctab, mod, t, tm, lw['norm1'], lw['wq'], lw['wr'],
                                                   lw['wgh'], lw['gbias'])
    ml = _mlstm(qt, k, kt, v, vt, g, gt, c0, n0, m0, b, t, tl['chunk'], emit_state)
    rg = _rglru(xr, h0, lw['wbd'], lw['rbias'], lw['lam'], lw['cw'], lw['cb'], b, t, tl['scan'], emit_state)
    routed = _mix_out(ml[0], ml[1], ot, rg[0], rg[1], gr, x2d, rtab, ctab, mod, t, tm, lw['mln'], lw['rgn'],
                      lw['wo_ml'], lw['wo_rg'], lw['norm2'], lw['rwt'], lw['rb'], cnt0)
    return routed, ml[2:], rg[2:]


def _routed_experts(paths, lw):
    cnt = paths[-1][5][:, 0]
    hn2p = jnp.concatenate([p[1] for p in paths], axis=0)
    ek = jnp.concatenate([p[2] for p in paths], axis=1)
    pk = jnp.concatenate([p[3] for p in paths], axis=1)
    dest3, block_rank, rank_expert, n_used, n_ranks, n_slots = _dispatch_plan(cnt, ek, pk, EXPERT_ROWS)
    xs = _sc_dispatch(hn2p, dest3, n_slots)
    ys = _experts(xs, block_rank, rank_expert, n_used, n_ranks, lw['w1'], lw['w3'], lw['w2'], EXPERT_ROWS)
    return _sc_combine_gather(ys, dest3)


def kernel(x_prompt, x_sample, c, state_mlstm_C, state_mlstm_n, state_mlstm_m, state_rglru_h, c_ctx, w_ada, b_ada, norm1, w_in, mlstm_gate_bias, mlstm_norm, rg_conv_w, rg_conv_b, rg_wa, rg_ba, rg_wx, rg_bx, rg_lambda, rg_norm, w_out, norm2, router_w, router_bias, exp_w1, exp_w3, exp_w2, shared_w1, shared_w3, shared_w2, norm_final):
    bp, tp, _ = x_prompt.shape
    bs, ts, _ = x_sample.shape
    depth = w_ada.shape[0]
    assert depth == 1, "the final norm is fused into the single layer's MoE output kernel"
    nd = 2 * N_HEADS
    l = 0
    lw = _layer_weights(l, norm1, w_in, mlstm_gate_bias, mlstm_norm, rg_conv_w, rg_conv_b, rg_wa, rg_ba, rg_wx,
                        rg_bx, rg_lambda, rg_norm, w_out, norm2, router_w, router_bias, exp_w1, exp_w3, exp_w2,
                        shared_w1, shared_w3, shared_w2)
    nf = norm_final.reshape(1, D_MODEL)
    cvecs = jnp.concatenate([c_ctx[None], c, jnp.zeros((SUBLANES - 1 - bs, D_MODEL), F32)], axis=0)
    mod = _ada(cvecs, w_ada[l], b_ada[l]).reshape(SUBLANES, N_MOD, D_MODEL)

    mod_p, mod_s = mod[0:1], mod[1:1 + bs]
    tm_p, tm_s = _tiles(tp, False)['tok'], _tiles(ts, True)['tok']
    rp, (cc, nc_, mc), (hc,) = _mixers(
        x_prompt.reshape(bp * tp, D_MODEL),
        (jnp.zeros((tm_p // GRID_W, D_MODEL // 2), F32), jnp.zeros((GRID_W, D_MODEL // 2), F32)), mod_p,
        jnp.zeros((1, nd, HEAD, HEAD), F32), jnp.zeros((1, nd, HEAD), F32), jnp.zeros((1, nd, HEAD), F32),
        jnp.zeros((1, 2, D_RG), F32), jnp.zeros((N_EXPERTS, LANES), I32), lw, bp, tp, True)
    rs, _, _ = _mixers(
        x_sample.reshape(bs * ts, D_MODEL), _grid_pos_tables(ts, D_MODEL), mod_s,
        state_mlstm_C[:, l].reshape(bs, nd, HEAD, HEAD), state_mlstm_n[:, l].reshape(bs, nd, HEAD),
        jnp.broadcast_to(state_mlstm_m[:, l].reshape(bs, nd, 1), (bs, nd, HEAD)),
        state_rglru_h[:, l], jnp.zeros((N_EXPERTS, LANES), I32), lw, bs, ts, False)
    yp = _moe_out(_routed_experts([rp], lw), 0, rp[1], rp[4], lw['sw13'], lw['sw2'], rp[0], mod_p, tp, tm_p, nf)
    ys = _moe_out(_routed_experts([rs], lw), 0, rs[1], rs[4], lw['sw13'], lw['sw2'], rs[0], mod_s, ts, tm_s, nf)

    y_prompt = yp.reshape(bp, tp, D_MODEL)
    y_sample = ys.reshape(bs, ts, D_MODEL)
    new_c = cc.reshape(bp, 1, 2, N_HEADS, HEAD, HEAD)
    new_n = nc_.reshape(bp, 1, 2, N_HEADS, HEAD)
    new_m = mc[:, :, 0].reshape(bp, 1, 2, N_HEADS)
    new_h = hc.reshape(bp, 1, 2, D_RG)
    return (y_prompt, y_sample, new_c, new_n, new_m, new_h)
```

```python
import functools

import jax
import jax.numpy as jnp
from jax import lax
from jax.experimental import pallas as pl
from jax.experimental.pallas import tpu as pltpu
from jax.experimental.pallas import tpu_sc as plsc

F32 = jnp.float32
BF16 = jnp.bfloat16
I32 = jnp.int32
HIGHEST = lax.Precision.HIGHEST

D_MODEL = 1024
N_MOD = 6
D_ML = 512
N_HEADS = 4
HEAD = 128
D_RG = 512
N_RG_BLOCKS = 8
RG_BLOCK = 64
RG_C = 8.0
N_GATE = 16
N_EXPERTS = 64
N_GROUPS = 8
GROUP = 8
TOPK_GROUPS = 4
TOP_K = 8
D_EXPERT = 256
ROUTED_SCALE = 2.5
EPS = 1e-6
GRID_W = 64
POS_BASE = 10000.0

RG_SEG_PAD = 8
SC_WINDOW = 128
SC_PACK_WORDS = 65536
D_PACK = D_MODEL // 2
EXPERT_ROWS = 512
EXPERT_BLOCKS_PER_STEP = 4
EXPERT_RING = 8

SUBLANES = 8
LANES = 128
VMEM_LIMIT = 48 * 1024 * 1024


def _params(*sem, vmem=VMEM_LIMIT):
    return pltpu.CompilerParams(dimension_semantics=sem, vmem_limit_bytes=vmem)


def _tiles(t, per_sequence_mod):
    cap = t if per_sequence_mod else 1 << 30
    return dict(
        tok=min(512, cap),
        chunk=min(256, t),
        scan=min(512, t),
    )


def _silu(x):
    return x * jax.nn.sigmoid(x)


def _softplus(x):
    return jnp.maximum(x, 0.0) + jnp.log1p(jnp.exp(-jnp.abs(x)))


def _rms(x, g):
    return x * lax.rsqrt(jnp.mean(x * x, axis=-1, keepdims=True) + EPS) * g


def _bf16_pieces(x):
    hi = x.astype(BF16)
    r = x - hi.astype(F32)
    mid = r.astype(BF16)
    return hi, mid, (r - mid.astype(F32)).astype(BF16)


def _pack_bf16_pairs(x):
    w = x.shape[1] // 2
    hi = lax.bitcast_convert_type(x[:, :w].astype(BF16).astype(F32), I32)
    lo = lax.bitcast_convert_type(x[:, w:].astype(BF16).astype(F32), I32)
    return hi | lax.shift_right_logical(lo, jnp.full(lo.shape, 16, I32))


def _unpack_bf16_pairs(p):
    hi = lax.bitcast_convert_type(p & jnp.int32(-65536), F32)
    lo = lax.bitcast_convert_type(lax.shift_left(p, jnp.full(p.shape, 16, I32)), F32)
    return hi, lo


def _ada_kernel(c_ref, w_ref, b_ref, o_ref):
    s = _silu(c_ref[...])
    o_ref[...] = jnp.dot(s, w_ref[...], precision=HIGHEST, preferred_element_type=F32) + b_ref[...]


def _ada(cvecs, w_ada, b_ada):
    n_out = w_ada.shape[1]
    tn = 1536
    return pl.pallas_call(
        _ada_kernel,
        grid=(n_out // tn,),
        in_specs=[
            pl.BlockSpec((SUBLANES, D_MODEL), lambda j: (0, 0)),
            pl.BlockSpec((D_MODEL, tn), lambda j: (0, j)),
            pl.BlockSpec((1, tn), lambda j: (0, j)),
        ],
        out_specs=pl.BlockSpec((SUBLANES, tn), lambda j: (0, j)),
        out_shape=jax.ShapeDtypeStruct((SUBLANES, n_out), F32),
        name="ada",
        compiler_params=_params("arbitrary"),
    )(cvecs, w_ada, b_ada.reshape(1, n_out))


def _pos_tile(rt_ref, ct_ref):
    left = jnp.concatenate([jnp.broadcast_to(rt_ref[r:r + 1, :], (GRID_W, rt_ref.shape[1]))
                            for r in range(rt_ref.shape[0])], axis=0)
    right = jnp.concatenate([ct_ref[...]] * rt_ref.shape[0], axis=0)
    return jnp.concatenate([left, right], axis=1)


def _in_proj_kernel(x_ref, rt_ref, ct_ref, mod_ref, n1_ref, wq_ref, wr_ref, wgh_ref, gb_ref,
                    qt_ref, k_ref, kt_ref, v_ref, vt_ref, ot_ref, xr_ref, gr_ref, g_ref, gt_ref):
    x = x_ref[...] + _pos_tile(rt_ref, ct_ref)
    hn = _rms(x, n1_ref[...]) * (1.0 + mod_ref[0, 1:2, :]) + mod_ref[0, 0:1, :]
    hb = hn.astype(BF16)
    z = jnp.dot(hb, wq_ref[...], preferred_element_type=F32)
    k = z[:, D_ML:2 * D_ML] * (HEAD ** -0.5)
    v = z[:, 2 * D_ML:3 * D_ML]
    qt_ref[...] = z[:, 0:D_ML].T.astype(BF16)
    k_ref[...] = k.astype(BF16)
    kt_ref[...] = k.T.astype(BF16)
    v_ref[...] = v.astype(BF16)
    vt_ref[...] = v.T.astype(BF16)
    ot_ref[...] = z[:, 3 * D_ML:4 * D_ML].T
    zr = jnp.dot(hb, wr_ref[...], preferred_element_type=F32)
    xr_ref[...] = zr[:, 0:D_RG]
    gr_ref[...] = zr[:, D_RG:2 * D_RG]
    zg = zr[:, 2 * D_RG:2 * D_RG + LANES]
    h_lo = (hn - hb.astype(F32)).astype(BF16)
    g = (zg + pltpu.roll(zg, LANES - N_GATE, 1)
         + jnp.dot(h_lo, wgh_ref[...], preferred_element_type=F32) + gb_ref[...])
    col = lax.broadcasted_iota(I32, g.shape, 1)
    g = jnp.where((col & 4) != 0, -_softplus(-g), g)
    g_ref[...] = g[:, 0:N_GATE]
    gt_ref[...] = g.T[0:N_GATE, :]


def _mod_index(bm, tm, t):
    if bm > 1:
        return lambda i: ((i * tm) // t, 0, 0)
    return lambda i: (0, 0, 0)


def _pos_specs(rtab, ctab, tm):
    rows = tm // GRID_W
    period = rtab.shape[0] // rows
    return [pl.BlockSpec((rows, D_MODEL // 2), lambda i: (i % period, 0)),
            pl.BlockSpec((GRID_W, D_MODEL // 2), lambda i: (0, 0))]


def _in_proj(x2d, rtab, ctab, mod, t, tm, norm1, wq, wr, wgh, gbias):
    n = x2d.shape[0]
    tok = lambda i: (i, 0)
    tok_t = lambda i: (0, i)
    const = lambda i: (0, 0)
    f = lambda w: jax.ShapeDtypeStruct((n, w), F32)
    row16 = jax.ShapeDtypeStruct((n, D_ML), BF16)
    col16 = jax.ShapeDtypeStruct((D_ML, n), BF16)
    return pl.pallas_call(
        _in_proj_kernel,
        grid=(n // tm,),
        in_specs=[pl.BlockSpec((tm, D_MODEL), tok)] + _pos_specs(rtab, ctab, tm) + [
            pl.BlockSpec((1, N_MOD, D_MODEL), _mod_index(mod.shape[0], tm, t)),
            pl.BlockSpec((1, D_MODEL), const),
            pl.BlockSpec((D_MODEL, 4 * D_ML), const),
            pl.BlockSpec((D_MODEL, 2 * D_RG + LANES), const),
            pl.BlockSpec((D_MODEL, LANES), const),
            pl.BlockSpec((1, LANES), const),
        ],
        out_specs=[pl.BlockSpec((D_ML, tm), tok_t), pl.BlockSpec((tm, D_ML), tok), pl.BlockSpec((D_ML, tm), tok_t),
                   pl.BlockSpec((tm, D_ML), tok), pl.BlockSpec((D_ML, tm), tok_t), pl.BlockSpec((D_ML, tm), tok_t),
                   pl.BlockSpec((tm, D_RG), tok), pl.BlockSpec((tm, D_RG), tok),
                   pl.BlockSpec((tm, N_GATE), tok), pl.BlockSpec((N_GATE, tm), tok_t)],
        out_shape=[col16, row16, col16, row16, col16, jax.ShapeDtypeStruct((D_ML, n), F32),
                   f(D_RG), f(D_RG), f(N_GATE), jax.ShapeDtypeStruct((N_GATE, n), F32)],
        name="in_proj",
        compiler_params=_params("arbitrary"),
    )(x2d, rtab, ctab, mod, norm1, wq, wr, wgh, gbias)


def _mlstm_kernel(chunk, nc, emit_state, *refs):
    (qtf, kf, ktf, vf, vtf, gf, gtf, qtb, kb, ktb, vb, vtb, gb, gtb, c0_ref, n0_ref, m0_ref) = refs[:17]
    if emit_state:
        hf_ref, hb_ref, c_out, n_out, m_out, c_sc, n_sc, m_sc = refs[17:]
    else:
        hf_ref, hb_ref, c_sc, n_sc, m_sc = refs[17:]
    i = pl.program_id(1)

    @pl.when(i == 0)
    def _():
        c_sc[...] = c0_ref[0]
        n_sc[...] = n0_ref[0]
        m_sc[...] = m0_ref[0]

    key = lax.broadcasted_iota(I32, (chunk, chunk), 0)
    qry = lax.broadcasted_iota(I32, (chunk, chunk), 1)
    hd = []
    for d, (qt_ref, k_ref, kt_ref, v_ref, vt_ref, g_ref, gt_ref, h_ref) in enumerate(
            ((qtf, kf, ktf, vf, vtf, gf, gtf, hf_ref), (qtb, kb, ktb, vb, vtb, gb, gtb, hb_ref))):
        tri = (key <= qry) if d == 0 else (key >= qry)
        tri_t = (qry <= key) if d == 0 else (qry >= key)
        g = g_ref[...]
        gt = gt_ref[...]
        rows3 = jnp.dot(jnp.concatenate(_bf16_pieces(gt), axis=0), tri.astype(BF16), preferred_element_type=F32)
        brow = rows3[0:N_GATE] + rows3[N_GATE:2 * N_GATE] + rows3[2 * N_GATE:3 * N_GATE]
        tri_t16 = tri_t.astype(BF16)
        bcol = sum(jnp.dot(tri_t16, piece, preferred_element_type=F32) for piece in _bf16_pieces(g))
        blast = bcol[chunk - 1:chunk, :] if d == 0 else bcol[0:1, :]
        for h in range(N_HEADS):
            ci = d * 8 + h
            cf = d * 8 + 4 + h
            j = d * N_HEADS + h
            sl = slice(h * HEAD, (h + 1) * HEAD)
            hd.append(dict(
                j=j, sl=sl, tri=tri, h_ref=h_ref, qt=qt_ref[sl, :], k=k_ref[:, sl], kt=kt_ref[sl, :],
                v=v_ref[:, sl], vt=vt_ref[sl, :], b_row=brow[cf:cf + 1, :],
                gate_col=g[:, ci:ci + 1] - bcol[:, cf:cf + 1], gate_row=gt[ci:ci + 1, :] - brow[cf:cf + 1, :],
                b_last=blast[:, cf:cf + 1], m_prev=m_sc[j:j + 1, 0:1], c_prev=c_sc[j], n_prev=n_sc[j:j + 1, :]))
    for x in hd:
        top = jnp.max(jnp.where(x['tri'], x['gate_col'], -jnp.inf), axis=0, keepdims=True)
        mx = jnp.maximum(x['m_prev'], top)
        x['dm'] = jnp.exp(jnp.where(x['tri'], x['gate_col'] - mx, -jnp.inf))
        x['w_inter'] = jnp.exp(x['m_prev'] - mx)
        x['floor'] = jnp.exp(-(x['b_row'] + mx))
    for x in hd:
        x['st'] = jnp.dot(x['k'], x['qt'], preferred_element_type=F32) * x['dm']
    for x in hd:
        inter = lax.dot_general(x['c_prev'].astype(BF16), x['qt'], (((0,), (0,)), ((), ())),
                                preferred_element_type=F32)
        num = x['w_inter'] * inter + jnp.dot(x['vt'], x['st'].astype(BF16), preferred_element_type=F32)
        qn = jnp.dot(jnp.broadcast_to(x['n_prev'], (SUBLANES, HEAD)).astype(BF16), x['qt'],
                     preferred_element_type=F32)[0:1, :]
        den = x['w_inter'] * qn + jnp.sum(x['st'], axis=0, keepdims=True)
        x['h_ref'][x['sl'], :] = num / jnp.maximum(jnp.abs(den), x['floor'])
    for x in hd:
        j = x['j']
        log_w = x['b_last'] + x['gate_row']
        m_new = jnp.maximum(x['b_last'] + x['m_prev'], jnp.max(log_w, axis=1, keepdims=True))
        decay = jnp.exp(x['b_last'] + x['m_prev'] - m_new)
        w_row = jnp.exp(log_w - m_new)
        kwt = (x['kt'].astype(F32) * w_row).astype(BF16)
        c_sc[j] = decay * x['c_prev'] + jnp.dot(kwt, x['v'], preferred_element_type=F32)
        n_sc[j:j + 1, :] = decay * x['n_prev'] + jnp.dot(
            jnp.broadcast_to(w_row, (SUBLANES, chunk)).astype(BF16), x['k'], preferred_element_type=F32)[0:1, :]
        m_sc[j:j + 1, :] = jnp.broadcast_to(m_new, (1, HEAD))

    if emit_state:
        @pl.when(i == nc - 1)
        def _():
            c_out[0] = c_sc[...]
            n_out[0] = n_sc[...]
            m_out[0] = m_sc[...]


def _state_index(bm):
    if bm > 1:
        return lambda b, i: (b,) + (0,) * 3, lambda b, i: (b, 0, 0)
    return lambda b, i: (0,) * 4, lambda b, i: (0, 0, 0)


def _mlstm(qt, k, kt, v, vt, g, gt, c0, n0, m0, b, t, chunk, emit_state):
    nc = t // chunk
    n = b * t
    nd = 2 * N_HEADS
    fwd = lambda bi, i: (bi * nc + i, 0)
    bwd = lambda bi, i: (bi * nc + nc - 1 - i, 0)
    fwd_t = lambda bi, i: (0, bi * nc + i)
    bwd_t = lambda bi, i: (0, bi * nc + nc - 1 - i)
    c_idx, n_idx = _state_index(c0.shape[0])
    rows = lambda m: pl.BlockSpec((chunk, D_ML), m)
    cols = lambda m: pl.BlockSpec((D_ML, chunk), m)
    one_dir = lambda m, mt: [cols(mt), rows(m), cols(mt), rows(m), cols(mt),
                             pl.BlockSpec((chunk, N_GATE), m), pl.BlockSpec((N_GATE, chunk), mt)]
    in_specs = (one_dir(fwd, fwd_t) + one_dir(bwd, bwd_t)
                + [pl.BlockSpec((1, nd, HEAD, HEAD), c_idx),
                   pl.BlockSpec((1, nd, HEAD), n_idx), pl.BlockSpec((1, nd, HEAD), n_idx)])
    out_specs = [cols(fwd_t), cols(bwd_t)]
    out_shape = [jax.ShapeDtypeStruct((D_ML, n), F32)] * 2
    if emit_state:
        out_specs += [pl.BlockSpec((1, nd, HEAD, HEAD), lambda bi, i: (bi, 0, 0, 0)),
                      pl.BlockSpec((1, nd, HEAD), lambda bi, i: (bi, 0, 0)),
                      pl.BlockSpec((1, nd, HEAD), lambda bi, i: (bi, 0, 0))]
        out_shape += [jax.ShapeDtypeStruct((b, nd, HEAD, HEAD), F32),
                      jax.ShapeDtypeStruct((b, nd, HEAD), F32),
                      jax.ShapeDtypeStruct((b, nd, HEAD), F32)]
    return pl.pallas_call(
        functools.partial(_mlstm_kernel, chunk, nc, emit_state),
        grid=(b, nc),
        in_specs=in_specs,
        out_specs=out_specs,
        out_shape=out_shape,
        scratch_shapes=[pltpu.VMEM((nd, HEAD, HEAD), F32), pltpu.VMEM((nd, HEAD), F32),
                        pltpu.VMEM((nd, HEAD), F32)],
        name="mlstm",
        compiler_params=_params("arbitrary", "arbitrary"),
    )(qt, k, kt, v, vt, g, gt, qt, k, kt, v, vt, g, gt, c0, n0, m0)


def _neg_expm1_2x(x):
    t = jnp.tanh(x)
    return -2.0 * t / (1.0 - t)


def _rglru_kernel(tb, nb, emit_state, *refs):
    (xf, xf_prev, xf_next, xb, xb_prev, xb_next, h0_ref, w_ref, bias_ref, lam_ref, cw_ref, cb_ref) = refs[:12]
    n_out = 3 if emit_state else 2
    hf_ref, hb_ref = refs[12:14]
    hfin_ref = refs[14] if emit_state else None
    carry, af_sc, uf_sc, ab_sc, ub_sc, hf_sc, pf_sc, hb_sc, pb_sc = refs[12 + n_out:]
    i = pl.program_id(1)

    @pl.when(i == 0)
    def _():
        carry[...] = h0_ref[0]

    row8 = lax.broadcasted_iota(I32, (SUBLANES, D_RG), 0)
    cw = cw_ref[...]
    log_a_per_r = -RG_C * _softplus(-lam_ref[...])

    def taps(xm2, xm1, x0, xp1):
        return cb_ref[...] + xm2 * cw[0:1, :] + xm1 * cw[1:2, :] + x0 * cw[2:3, :] + xp1 * cw[3:4, :]

    def conv(main_ref, prev_ref, next_ref, first, last):
        main = main_ref[...]
        prev = jnp.where(first, 0.0, prev_ref[...])
        nxt = jnp.where(last, 0.0, next_ref[...])
        body = taps(pltpu.roll(main, 2, 0), pltpu.roll(main, 1, 0), main, pltpu.roll(main, tb - 1, 0))
        e = SUBLANES
        head, tail = main[0:e, :], main[tb - e:tb, :]
        before_tail = main[tb - 2 * e:tb - e, :]
        fix_head = taps(
            jnp.where(row8 == 0, prev[6:7, :], jnp.where(row8 == 1, prev[7:8, :], pltpu.roll(head, 2, 0))),
            jnp.where(row8 == 0, prev[7:8, :], pltpu.roll(head, 1, 0)), head,
            jnp.where(row8 == e - 1, main[e:e + 1, :], pltpu.roll(head, e - 1, 0)))
        fix_tail = taps(
            jnp.where(row8 == 0, before_tail[6:7, :],
                      jnp.where(row8 == 1, before_tail[7:8, :], pltpu.roll(tail, 2, 0))),
            jnp.where(row8 == 0, before_tail[7:8, :], pltpu.roll(tail, 1, 0)), tail,
            jnp.where(row8 == e - 1, nxt[0:1, :], pltpu.roll(tail, e - 1, 0)))
        return jnp.concatenate([fix_head, body[e:tb - e, :], fix_tail], axis=0)

    def recurrence_terms(xc, d):
        z = jnp.dot(xc.astype(BF16), w_ref[:, d * 2 * D_RG:(d + 1) * 2 * D_RG],
                    preferred_element_type=F32) + bias_ref[:, d * 2 * D_RG:(d + 1) * 2 * D_RG]
        r = jax.nn.sigmoid(z[:, 0:D_RG])
        ig = jax.nn.sigmoid(z[:, D_RG:2 * D_RG])
        log_a = r * log_a_per_r[d:d + 1, :]
        a = jnp.exp(log_a)
        u = jnp.sqrt(_neg_expm1_2x(log_a)) * (ig * xc)
        return a, u

    a_f, u_f = recurrence_terms(conv(xf, xf_prev, xf_next, i == 0, i == nb - 1), 0)
    a_b, u_b = recurrence_terms(conv(xb, xb_prev, xb_next, i == nb - 1, i == 0), 1)
    seg = tb // SUBLANES
    pitch = seg + RG_SEG_PAD
    ncol = D_RG // LANES
    for lc in range(ncol):
        lanes = slice(lc * LANES, (lc + 1) * LANES)
        for s in range(SUBLANES):
            src = slice(s * seg, (s + 1) * seg)
            dst = slice(s * pitch, s * pitch + seg)
            af_sc[lc, dst, :], uf_sc[lc, dst, :] = a_f[src, lanes], u_f[src, lanes]
            ab_sc[lc, dst, :], ub_sc[lc, dst, :] = a_b[src, lanes], u_b[src, lanes]
    slab = lambda k: (slice(None), pl.ds(k, SUBLANES, stride=pitch), slice(None))
    hf = jnp.zeros((ncol, SUBLANES, LANES), F32)
    hb = jnp.zeros((ncol, SUBLANES, LANES), F32)
    pf = jnp.ones((ncol, SUBLANES, LANES), F32)
    pb = jnp.ones((ncol, SUBLANES, LANES), F32)
    for k in range(seg):
        kb = seg - 1 - k
        ak = af_sc[slab(k)]
        hf = ak * hf + uf_sc[slab(k)]
        pf = pf * ak
        hf_sc[slab(k)] = hf
        pf_sc[slab(k)] = pf
        ak = ab_sc[slab(kb)]
        hb = ak * hb + ub_sc[slab(kb)]
        pb = pb * ak
        hb_sc[slab(kb)] = hb
        pb_sc[slab(kb)] = pb
    for lc in range(ncol):
        lanes = slice(lc * LANES, (lc + 1) * LANES)
        c = carry[0:1, lanes]
        cin_f = []
        for s in range(SUBLANES):
            cin_f.append(c)
            c = pf[lc, s:s + 1, :] * c + hf[lc, s:s + 1, :]
        carry[0:1, lanes] = c
        c = carry[1:2, lanes]
        cin_b = [None] * SUBLANES
        for s in reversed(range(SUBLANES)):
            cin_b[s] = c
            c = pb[lc, s:s + 1, :] * c + hb[lc, s:s + 1, :]
        carry[1:2, lanes] = c
        for s in range(SUBLANES):
            rows = slice(s * seg, (s + 1) * seg)
            src = slice(s * pitch, s * pitch + seg)
            hf_ref[rows, lanes] = hf_sc[lc, src, :] + pf_sc[lc, src, :] * cin_f[s]
            hb_ref[rows, lanes] = hb_sc[lc, src, :] + pb_sc[lc, src, :] * cin_b[s]

    if emit_state:
        @pl.when(i == nb - 1)
        def _():
            hfin_ref[0] = carry[...]


def _rglru(xr, h0, wbd, bias, lam, cw, cb, b, t, tb, emit_state):
    nb = t // tb
    n = b * t
    r8 = tb // SUBLANES
    last8 = n // SUBLANES - 1
    fwd = lambda bi, i: (bi * nb + i, 0)
    bwd = lambda bi, i: (bi * nb + nb - 1 - i, 0)
    fwd_prev = lambda bi, i: (jnp.maximum((bi * nb + i) * r8 - 1, 0), 0)
    fwd_next = lambda bi, i: (jnp.minimum((bi * nb + i + 1) * r8, last8), 0)
    bwd_prev = lambda bi, i: (jnp.maximum((bi * nb + nb - 1 - i) * r8 - 1, 0), 0)
    bwd_next = lambda bi, i: (jnp.minimum((bi * nb + nb - i) * r8, last8), 0)
    const = lambda bi, i: (0, 0)
    h_idx = (lambda bi, i: (bi, 0, 0)) if h0.shape[0] > 1 else (lambda bi, i: (0, 0, 0))
    halo = lambda m: pl.BlockSpec((SUBLANES, D_RG), m)
    in_specs = [pl.BlockSpec((tb, D_RG), fwd), halo(fwd_prev), halo(fwd_next),
                pl.BlockSpec((tb, D_RG), bwd), halo(bwd_prev), halo(bwd_next),
                pl.BlockSpec((1, 2, D_RG), h_idx),
                pl.BlockSpec((D_RG, 4 * D_RG), const), pl.BlockSpec((1, 4 * D_RG), const),
                pl.BlockSpec((2, D_RG), const), pl.BlockSpec((4, D_RG), const), pl.BlockSpec((1, D_RG), const)]
    out_specs = [pl.BlockSpec((tb, D_RG), fwd), pl.BlockSpec((tb, D_RG), bwd)]
    out_shape = [jax.ShapeDtypeStruct((n, D_RG), F32)] * 2
    if emit_state:
        out_specs.append(pl.BlockSpec((1, 2, D_RG), lambda bi, i: (bi, 0, 0)))
        out_shape.append(jax.ShapeDtypeStruct((b, 2, D_RG), F32))
    return pl.pallas_call(
        functools.partial(_rglru_kernel, tb, nb, emit_state),
        grid=(b, nb),
        in_specs=in_specs,
        out_specs=out_specs,
        out_shape=out_shape,
        scratch_shapes=[pltpu.VMEM((2, D_RG), F32)]
        + [pltpu.VMEM((D_RG // LANES, tb + SUBLANES * RG_SEG_PAD, LANES), F32)] * 8,
        name="rglru",
        compiler_params=_params("arbitrary", "arbitrary"),
    )(xr, xr, xr, xr, xr, xr, h0, wbd, bias, lam, cw, cb)


def _route(s, sb):
    tm = s.shape[1]
    neg = -jnp.inf
    sub = lax.broadcasted_iota(I32, (GROUP, tm), 0)
    blocks = [sb[gi * GROUP:(gi + 1) * GROUP, :] for gi in range(N_GROUPS)]
    gscore = []
    for blk in blocks:
        m1 = jnp.max(blk, axis=0, keepdims=True)
        first = jnp.min(jnp.where(blk == m1, sub, GROUP), axis=0, keepdims=True)
        m2 = jnp.max(jnp.where(sub == first, neg, blk), axis=0, keepdims=True)
        gscore.append(m1 + m2)
    masked = []
    for gi in range(N_GROUPS):
        rank = jnp.zeros((1, tm), F32)
        for gj in range(N_GROUPS):
            if gj == gi:
                continue
            ahead = (gscore[gj] >= gscore[gi]) if gj < gi else (gscore[gj] > gscore[gi])
            rank = rank + jnp.where(ahead, 1.0, 0.0)
        masked.append(jnp.where(rank < TOPK_GROUPS, blocks[gi], neg))
    v = jnp.concatenate(masked, axis=0)
    eid = lax.broadcasted_iota(I32, (N_EXPERTS, tm), 0)
    sel = jnp.zeros((N_EXPERTS, tm), F32)
    picks = []
    for _ in range(TOP_K):
        mx = jnp.max(v, axis=0, keepdims=True)
        idx = jnp.min(jnp.where(v == mx, eid, N_EXPERTS), axis=0, keepdims=True)
        pick = eid == idx
        picks.append(pick)
        sel = jnp.where(pick, 1.0, sel)
        v = jnp.where(pick, neg, v)
    ws = s * sel
    return ws / jnp.sum(ws, axis=0, keepdims=True) * ROUTED_SCALE, sel, picks


def _mix_out_kernel(hmf_ref, hmb_ref, ot_ref, hrf_ref, hrb_ref, gr_ref, x_ref, rt_ref, ct_ref, mod_ref,
                    mln_ref, rgn_ref, wo_ml_ref, wo_rg_ref, n2_ref, rwt_ref, rb_ref, cnt0_ref,
                    x1_ref, hn2p_ref, ek_ref, pk_ref, wtok_ref, cnt_ref, cnt_sc):
    i = pl.program_id(0)
    tm = x_ref.shape[0]

    @pl.when(i == 0)
    def _():
        cnt_sc[...] = cnt0_ref[...].astype(F32)

    hm = hmf_ref[...] + hmb_ref[...]
    heads = []
    for h in range(N_HEADS):
        seg = hm[h * HEAD:(h + 1) * HEAD, :]
        heads.append(seg * lax.rsqrt(jnp.mean(seg * seg, axis=0, keepdims=True) + EPS))
    y_ml_t = jnp.concatenate(heads, axis=0) * mln_ref[...] * jax.nn.sigmoid(ot_ref[...])
    y_rg = _rms(hrf_ref[...] + hrb_ref[...], rgn_ref[...]) * jax.nn.gelu(gr_ref[...])
    mix = (lax.dot_general(y_ml_t.astype(BF16), wo_ml_ref[...], (((0,), (0,)), ((), ())),
                           preferred_element_type=F32)
           + jnp.dot(y_rg.astype(BF16), wo_rg_ref[...], preferred_element_type=F32))
    x1 = x_ref[...] + _pos_tile(rt_ref, ct_ref) + mod_ref[0, 2:3, :] * mix
    x1_ref[...] = x1
    hn2 = _rms(x1, n2_ref[...]) * (1.0 + mod_ref[0, 4:5, :]) + mod_ref[0, 3:4, :]
    hn2p_ref[...] = _pack_bf16_pairs(hn2)
    hb = hn2.astype(BF16)
    h_lo = (hn2 - hb.astype(F32)).astype(BF16)
    nt = (((1,), (1,)), ((), ()))
    two = lax.dot_general(rwt_ref[...], hb, nt, preferred_element_type=F32)
    logits_t = (two[0:N_EXPERTS, :] + two[N_EXPERTS:2 * N_EXPERTS, :]
                + lax.dot_general(rwt_ref[0:N_EXPERTS, :], h_lo, nt, preferred_element_type=F32))
    s = jax.nn.sigmoid(logits_t)
    wt, sel, picks = _route(s, s + rb_ref[...])

    earlier = (lax.broadcasted_iota(I32, (tm, tm), 0) < lax.broadcasted_iota(I32, (tm, tm), 1))
    prefix = jnp.dot(sel.astype(BF16), earlier.astype(BF16), preferred_element_type=F32)
    pos_all = cnt_sc[:, 0:1] + prefix
    eid = lax.broadcasted_iota(I32, (N_EXPERTS, tm), 0)
    eid_f = eid.astype(F32)
    row8 = lax.broadcasted_iota(I32, (TOP_K, tm), 0)
    ek = jnp.zeros((TOP_K, tm), F32)
    pk = jnp.zeros((TOP_K, tm), F32)
    wk = jnp.zeros((N_EXPERTS, tm), F32)
    for k, pick in enumerate(picks):
        take = lambda a: jnp.sum(jnp.where(pick, a, 0.0), axis=0, keepdims=True)
        ek = jnp.where(row8 == k, take(eid_f), ek)
        pk = jnp.where(row8 == k, take(pos_all), pk)
        wk = jnp.where(eid == k, take(wt), wk)
    ek_ref[...] = ek.astype(I32)
    pk_ref[...] = pk.astype(I32)
    wtok_ref[...] = wk.T
    cnt_sc[...] += jnp.broadcast_to(jnp.sum(sel, axis=1, keepdims=True), cnt_sc.shape)

    @pl.when(i == pl.num_programs(0) - 1)
    def _():
        cnt_ref[...] = cnt_sc[...].astype(I32)


def _mix_out(hmf, hmb, ot, hrf, hrb, gr, x2d, rtab, ctab, mod, t, tm, mln, rgn, wo_ml, wo_rg, norm2, rwt, rbias,
             cnt0):
    n = x2d.shape[0]
    tok = lambda i: (i, 0)
    tok_t = lambda i: (0, i)
    const = lambda i: (0, 0)
    return pl.pallas_call(
        _mix_out_kernel,
        grid=(n // tm,),
        in_specs=[pl.BlockSpec((D_ML, tm), tok_t)] * 3 + [pl.BlockSpec((tm, D_RG), tok)] * 3 + [
            pl.BlockSpec((tm, D_MODEL), tok)] + _pos_specs(rtab, ctab, tm) + [
            pl.BlockSpec((1, N_MOD, D_MODEL), _mod_index(mod.shape[0], tm, t)),
            pl.BlockSpec((D_ML, 1), const), pl.BlockSpec((1, D_RG), const),
            pl.BlockSpec((D_ML, D_MODEL), const), pl.BlockSpec((D_RG, D_MODEL), const),
            pl.BlockSpec((1, D_MODEL), const),
            pl.BlockSpec((2 * N_EXPERTS, D_MODEL), const), pl.BlockSpec((N_EXPERTS, 1), const),
            pl.BlockSpec((N_EXPERTS, LANES), const),
        ],
        out_specs=[pl.BlockSpec((tm, D_MODEL), tok), pl.BlockSpec((tm, D_PACK), tok),
                   pl.BlockSpec((TOP_K, tm), tok_t), pl.BlockSpec((TOP_K, tm), tok_t),
                   pl.BlockSpec((tm, N_EXPERTS), tok), pl.BlockSpec((N_EXPERTS, LANES), const)],
        out_shape=[jax.ShapeDtypeStruct((n, D_MODEL), F32), jax.ShapeDtypeStruct((n, D_PACK), I32),
                   jax.ShapeDtypeStruct((TOP_K, n), I32), jax.ShapeDtypeStruct((TOP_K, n), I32),
                   jax.ShapeDtypeStruct((n, N_EXPERTS), F32), jax.ShapeDtypeStruct((N_EXPERTS, LANES), I32)],
        scratch_shapes=[pltpu.VMEM((N_EXPERTS, LANES), F32)],
        name="mix_out",
        compiler_params=_params("arbitrary"),
    )(hmf, hmb, ot, hrf, hrb, gr, x2d, rtab, ctab, mod, mln, rgn, wo_ml, wo_rg, norm2, rwt, rbias, cnt0)


def _sc_mesh():
    return plsc.VectorSubcoreMesh(core_axis_name="core", subcore_axis_name="subcore")


def _sc_worker():
    info = plsc.get_sparse_core_info()
    return lax.axis_index("subcore") * info.num_cores + lax.axis_index("core"), info.num_cores * info.num_subcores


def _sc_dispatch(xp, dest3, n_slots):
    n, w = xp.shape
    nwin = n // SC_WINDOW

    @pl.kernel(out_type=jax.ShapeDtypeStruct((n_slots, w), xp.dtype), mesh=_sc_mesh(),
               scratch_types=[pltpu.VMEM((SC_WINDOW, w), xp.dtype), pltpu.VMEM((TOP_K, SC_WINDOW), I32)],
               name="sc_dispatch")
    def k(x_hbm, i_hbm, o_hbm, x_v, i_v):
        wid, nworkers = _sc_worker()
        per = nwin // nworkers

        @pl.loop(0, per)
        def _(s):
            win = wid * per + s
            pltpu.sync_copy(x_hbm.at[pl.ds(win * SC_WINDOW, SC_WINDOW)], x_v)
            pltpu.sync_copy(i_hbm.at[win], i_v)
            for j in range(TOP_K):
                pltpu.sync_copy(x_v, o_hbm.at[i_v.at[j]])

    return k(xp, dest3)


def _sc_pack_rows(w):
    r, c = w.shape
    half = c // 2
    lanes = plsc.get_sparse_core_info().num_lanes
    chunk = SC_PACK_WORDS // c
    nchunks = r // chunk

    def rne_high(x):
        u = plsc.bitcast(x, I32)
        u = u + 0x7FFF + (lax.shift_right_logical(u, jnp.full(u.shape, 16, I32)) & 1)
        return u & jnp.int32(-65536)

    @pl.kernel(out_type=jax.ShapeDtypeStruct((r, half), I32), mesh=_sc_mesh(),
               scratch_types=[pltpu.VMEM((chunk, c), F32), pltpu.VMEM((chunk, half), I32)],
               compiler_params=pltpu.CompilerParams(needs_layout_passes=False), name="sc_pack")
    def k(w_hbm, o_hbm, in_v, out_v):
        wid, nworkers = _sc_worker()
        per = nchunks // nworkers

        @pl.loop(0, per)
        def _(s):
            first = (wid * per + s) * chunk
            pltpu.sync_copy(w_hbm.at[pl.ds(first, chunk)], in_v)

            @pl.loop(0, chunk)
            def _(row):
                for j in range(half // lanes):
                    at = lambda off: (row, pl.ds(off + j * lanes, lanes))
                    hi = rne_high(in_v.at[*at(0)][...])
                    lo = rne_high(in_v.at[*at(half)][...])
                    out_v.at[*at(0)][...] = hi | lax.shift_right_logical(lo, jnp.full(lo.shape, 16, I32))

            pltpu.sync_copy(out_v, o_hbm.at[pl.ds(first, chunk)])

    return k(w)


def _sc_combine_gather(ys, dest3):
    nwin = dest3.shape[0]
    w = ys.shape[1]

    @pl.kernel(out_type=jax.ShapeDtypeStruct((nwin, TOP_K, SC_WINDOW, w), ys.dtype), mesh=_sc_mesh(),
               scratch_types=[pltpu.VMEM((SC_WINDOW, w), ys.dtype), pltpu.VMEM((TOP_K, SC_WINDOW), I32)],
               name="sc_combine")
    def k(y_hbm, i_hbm, o_hbm, y_v, i_v):
        wid, nworkers = _sc_worker()
        per = nwin // nworkers

        @pl.loop(0, per)
        def _(s):
            win = wid * per + s
            pltpu.sync_copy(i_hbm.at[win], i_v)
            for j in range(TOP_K):
                pltpu.sync_copy(y_hbm.at[i_v.at[j]], y_v)
                pltpu.sync_copy(y_v, o_hbm.at[win, j])

    return k(ys, dest3)


def _swiglu(x, w13):
    h = jnp.dot(x, w13, preferred_element_type=F32)
    return _silu(h[:, 0:D_EXPERT]) * h[:, D_EXPERT:2 * D_EXPERT]


def _unpack_rows_bf16(p):
    hi, lo = _unpack_bf16_pairs(p)
    return jnp.concatenate([hi.astype(BF16), lo.astype(BF16)], axis=1)


def _expert_kernel(rows, rank_ref, expert_ref, nu_ref, nr_ref, x_ref, w1_hbm, w3_hbm, w2_hbm, y_ref,
                   w1buf, w3buf, w2buf, sem, progress):
    b = pl.program_id(0)
    g = EXPERT_BLOCKS_PER_STEP
    n_ranks = nr_ref[0]

    @pl.when(b == 0)
    def _():
        progress[0] = 0
        progress[1] = 0

    def copies(r):
        slot = lax.rem(r, EXPERT_RING)
        e = expert_ref[r]
        return (pltpu.make_async_copy(w1_hbm.at[e], w1buf.at[slot], sem.at[slot, 0]),
                pltpu.make_async_copy(w3_hbm.at[e], w3buf.at[slot], sem.at[slot, 1]),
                pltpu.make_async_copy(w2_hbm.at[e], w2buf.at[slot], sem.at[slot, 2]))

    def start(r, carry):
        for cp in copies(r):
            cp.start()
        return carry

    def wait(r, carry):
        for cp in copies(r):
            cp.wait()
        return carry

    first = rank_ref[b * g]
    last = rank_ref[b * g + g - 1]
    started = jnp.minimum(first + EXPERT_RING, n_ranks)
    lax.fori_loop(progress[0], started, start, 0)
    progress[0] = jnp.maximum(progress[0], started)
    needed = jnp.where(b == pl.num_programs(0) - 1, progress[0], jnp.minimum(last + 1, n_ranks))
    lax.fori_loop(progress[1], needed, wait, 0)
    progress[1] = jnp.maximum(progress[1], needed)

    @pl.when(b * g < nu_ref[0])
    def _():
        for j in range(g):
            slot = lax.rem(rank_ref[b * g + j], EXPERT_RING)
            sl = slice(j * rows, (j + 1) * rows)
            x = _unpack_rows_bf16(x_ref[sl, :])
            h = (_silu(jnp.dot(x, _unpack_rows_bf16(w1buf[slot]), preferred_element_type=F32))
                 * jnp.dot(x, _unpack_rows_bf16(w3buf[slot]), preferred_element_type=F32))
            y_ref[sl, :] = _pack_bf16_pairs(
                jnp.dot(h.astype(BF16), _unpack_rows_bf16(w2buf[slot]), preferred_element_type=F32))


def _experts(xs, block_rank, rank_expert, n_used, n_ranks, w1, w3, w2, rows):
    g = EXPERT_BLOCKS_PER_STEP
    nb = xs.shape[0] // rows
    tok = lambda b, *_: (b, 0)
    return pl.pallas_call(
        functools.partial(_expert_kernel, rows),
        grid_spec=pltpu.PrefetchScalarGridSpec(
            num_scalar_prefetch=4,
            grid=(nb // g,),
            in_specs=[pl.BlockSpec((g * rows, D_PACK), tok)] + [pl.BlockSpec(memory_space=pl.ANY)] * 3,
            out_specs=pl.BlockSpec((g * rows, D_PACK), tok),
            scratch_shapes=[pltpu.VMEM((EXPERT_RING, D_MODEL, D_EXPERT // 2), I32),
                            pltpu.VMEM((EXPERT_RING, D_MODEL, D_EXPERT // 2), I32),
                            pltpu.VMEM((EXPERT_RING, D_EXPERT, D_PACK), I32),
                            pltpu.SemaphoreType.DMA((EXPERT_RING, 3)), pltpu.SMEM((2,), I32)],
        ),
        out_shape=jax.ShapeDtypeStruct(xs.shape, I32),
        name="experts",
        compiler_params=_params("arbitrary"),
    )(block_rank, rank_expert, n_used, n_ranks, xs, w1, w3, w2)


def _moe_out_kernel(yk_ref, hn2p_ref, wtok_ref, sw13_ref, sw2_ref, x1_ref, mod_ref, nf_ref, y_ref):
    shared = jnp.dot(_swiglu(_unpack_rows_bf16(hn2p_ref[...]), sw13_ref[...]).astype(BF16), sw2_ref[...],
                     preferred_element_type=F32)
    w = wtok_ref[...]
    parts = []
    for wi in range(yk_ref.shape[0]):
        rows = slice(wi * SC_WINDOW, (wi + 1) * SC_WINDOW)
        a_hi = shared[rows, 0:D_PACK]
        a_lo = shared[rows, D_PACK:D_MODEL]
        for k in range(TOP_K):
            y_hi, y_lo = _unpack_bf16_pairs(yk_ref[wi, k])
            wc = w[rows, k:k + 1]
            a_hi = a_hi + wc * y_hi
            a_lo = a_lo + wc * y_lo
        parts.append(jnp.concatenate([a_hi, a_lo], axis=1))
    x2 = x1_ref[...] + mod_ref[0, 5:6, :] * jnp.concatenate(parts, axis=0)
    y_ref[...] = _rms(x2, nf_ref[...])


def _moe_out(yk, tok0, hn2p, wtok, sw13, sw2, x1, mod, t, tm, norm_final):
    n = hn2p.shape[0]
    tok = lambda i: (i, 0)
    const = lambda i: (0, 0)
    blk0 = tok0 // tm
    return pl.pallas_call(
        _moe_out_kernel,
        grid=(n // tm,),
        in_specs=[
            pl.BlockSpec((tm // SC_WINDOW, TOP_K, SC_WINDOW, D_PACK), lambda i: (i + blk0, 0, 0, 0)),
            pl.BlockSpec((tm, D_PACK), tok),
            pl.BlockSpec((tm, N_EXPERTS), tok),
            pl.BlockSpec((D_MODEL, 2 * D_EXPERT), const),
            pl.BlockSpec((D_EXPERT, D_MODEL), const),
            pl.BlockSpec((tm, D_MODEL), tok),
            pl.BlockSpec((1, N_MOD, D_MODEL), _mod_index(mod.shape[0], tm, t)),
            pl.BlockSpec((1, D_MODEL), const),
        ],
        out_specs=pl.BlockSpec((tm, D_MODEL), tok),
        out_shape=jax.ShapeDtypeStruct((n, D_MODEL), F32),
        name="moe_out",
        compiler_params=_params("arbitrary"),
    )(yk, hn2p, wtok, sw13, sw2, x1, mod, norm_final)


def _dispatch_plan(cnt, ek, pk, rows):
    n = ek.shape[1]
    nb = n * TOP_K // rows + N_EXPERTS
    nblk = (cnt + rows - 1) // rows
    block_end = jnp.cumsum(nblk)
    experts = jnp.arange(N_EXPERTS, dtype=I32)
    first_row = jnp.sum(jnp.where(ek[:, :, None] == experts, (block_end - nblk) * rows, 0), axis=-1)
    dest3 = (first_row + pk).reshape(TOP_K, n // SC_WINDOW, SC_WINDOW).transpose(1, 0, 2)
    owns = nblk > 0
    n_ranks = jnp.sum(owns.astype(I32))
    blocks = jnp.arange(nb, dtype=I32)[:, None]
    block_rank = jnp.minimum(jnp.sum((owns & (block_end <= blocks)).astype(I32), axis=1), n_ranks - 1)
    rank_expert = jnp.minimum(jnp.sum((jnp.cumsum(owns.astype(I32)) <= experts[:, None]).astype(I32), axis=1),
                              N_EXPERTS - 1)
    return (dest3, block_rank.astype(I32), rank_expert.astype(I32), block_end[-1:].astype(I32),
            n_ranks.reshape(1).astype(I32), nb * rows)


def _grid_pos_tables(n_tokens, dim):
    quarter = dim // 4
    omega = 1.0 / (POS_BASE ** (jnp.arange(quarter, dtype=F32) / quarter))
    ra = jnp.arange(n_tokens // GRID_W).astype(F32)[:, None] * omega
    ca = jnp.arange(GRID_W).astype(F32)[:, None] * omega
    return (jnp.concatenate([jnp.sin(ra), jnp.cos(ra)], axis=-1),
            jnp.concatenate([jnp.sin(ca), jnp.cos(ca)], axis=-1))


def _hi_lo_rows(w):
    hi = w.astype(BF16)
    return jnp.concatenate([hi, (w - hi.astype(F32)).astype(BF16)], axis=0)


def _block_diag(w):
    eye = jnp.eye(N_RG_BLOCKS, dtype=w.dtype)
    return jnp.einsum('nij,nm->nimj', w, eye).reshape(D_RG, D_RG)


def _layer_weights(l, norm1, w_in, mlstm_gate_bias, mlstm_norm, rg_conv_w, rg_conv_b, rg_wa, rg_ba, rg_wx,
                   rg_bx, rg_lambda, rg_norm, w_out, norm2, router_w, router_bias, exp_w1, exp_w3, exp_w2,
                   shared_w1, shared_w3, shared_w2):
    wi = w_in[l]
    c0, c1 = 4 * D_ML, 4 * D_ML + N_GATE
    wg = wi[:, c0:c1]
    wg_hi = wg.astype(BF16)
    wg_lo = (wg - wg_hi.astype(F32)).astype(BF16)
    zcols = lambda w: jnp.zeros((D_MODEL, w), BF16)
    return dict(
        norm1=norm1[l].reshape(1, D_MODEL),
        wq=wi[:, :c0].astype(BF16),
        wr=jnp.concatenate([wi[:, c1:].astype(BF16), wg_hi, wg_lo, zcols(LANES - 2 * N_GATE)], axis=1),
        wgh=jnp.concatenate([wg_hi, zcols(LANES - N_GATE)], axis=1),
        gbias=jnp.pad(mlstm_gate_bias[l].reshape(1, N_GATE), ((0, 0), (0, LANES - N_GATE))),
        mln=mlstm_norm[l].reshape(D_ML, 1),
        cw=rg_conv_w[l], cb=rg_conv_b[l].reshape(1, D_RG),
        wbd=jnp.concatenate([_block_diag(rg_wa[l, 0]), _block_diag(rg_wx[l, 0]),
                             _block_diag(rg_wa[l, 1]), _block_diag(rg_wx[l, 1])], axis=1).astype(BF16),
        rbias=jnp.concatenate([rg_ba[l, 0], rg_bx[l, 0], rg_ba[l, 1], rg_bx[l, 1]]).reshape(1, 4 * D_RG),
        lam=rg_lambda[l], rgn=rg_norm[l].reshape(1, D_RG),
        wo_ml=w_out[l, :D_ML].astype(BF16), wo_rg=w_out[l, D_ML:].astype(BF16),
        norm2=norm2[l].reshape(1, D_MODEL),
        rwt=_hi_lo_rows(router_w[l].T), rb=router_bias[l].reshape(N_EXPERTS, 1),
        w1=_sc_pack_rows(exp_w1[l].reshape(N_EXPERTS * D_MODEL, D_EXPERT)).reshape(N_EXPERTS, D_MODEL, -1),
        w3=_sc_pack_rows(exp_w3[l].reshape(N_EXPERTS * D_MODEL, D_EXPERT)).reshape(N_EXPERTS, D_MODEL, -1),
        w2=_sc_pack_rows(exp_w2[l].reshape(N_EXPERTS * D_EXPERT, D_MODEL)).reshape(N_EXPERTS, D_EXPERT, -1),
        sw13=jnp.concatenate([shared_w1[l], shared_w3[l]], axis=-1).astype(BF16),
        sw2=shared_w2[l].astype(BF16),
    )


def _mixers(x2d, pos_tables, mod, c0, n0, m0, h0, cnt0, lw, b, t, emit_state):
    tl = _tiles(t, mod.shape[0] > 1)
    tm = tl['tok']
    rtab, ctab = pos_tables
    qt, k, kt, v, vt, ot, xr, gr, g, gt = _in_proj(x2d, rtab, ctab, mod, t, tm, lw['norm1'], lw['wq'], lw['wr'],
                                                   lw['wgh'], lw['gbias'])
    ml = _mlstm(qt, k, kt, v, vt, g, gt, c0, n0, m0, b, t, tl['chunk'], emit_state)
    rg = _rglru(xr, h0, lw['wbd'], lw['rbias'], lw['lam'], lw['cw'], lw['cb'], b, t, tl['scan'], emit_state)
    routed = _mix_out(ml[0], ml[1], ot, rg[0], rg[1], gr, x2d, rtab, ctab, mod, t, tm, lw['mln'], lw['rgn'],
                      lw['wo_ml'], lw['wo_rg'], lw['norm2'], lw['rwt'], lw['rb'], cnt0)
    return routed, ml[2:], rg[2:]


def _routed_experts(paths, lw):
    cnt = paths[-1][5][:, 0]
    hn2p = jnp.concatenate([p[1] for p in paths], axis=0)
    ek = jnp.concatenate([p[2] for p in paths], axis=1)
    pk = jnp.concatenate([p[3] for p in paths], axis=1)
    dest3, block_rank, rank_expert, n_used, n_ranks, n_slots = _dispatch_plan(cnt, ek, pk, EXPERT_ROWS)
    xs = _sc_dispatch(hn2p, dest3, n_slots)
    ys = _experts(xs, block_rank, rank_expert, n_used, n_ranks, lw['w1'], lw['w3'], lw['w2'], EXPERT_ROWS)
    return _sc_combine_gather(ys, dest3)


def kernel(x_prompt, x_sample, c, state_mlstm_C, state_mlstm_n, state_mlstm_m, state_rglru_h, c_ctx, w_ada, b_ada, norm1, w_in, mlstm_gate_bias, mlstm_norm, rg_conv_w, rg_conv_b, rg_wa, rg_ba, rg_wx, rg_bx, rg_lambda, rg_norm, w_out, norm2, router_w, router_bias, exp_w1, exp_w3, exp_w2, shared_w1, shared_w3, shared_w2, norm_final):
    bp, tp, _ = x_prompt.shape
    bs, ts, _ = x_sample.shape
    depth = w_ada.shape[0]
    assert depth == 1, "the final norm is fused into the single layer's MoE output kernel"
    nd = 2 * N_HEADS
    l = 0
    lw = _layer_weights(l, norm1, w_in, mlstm_gate_bias, mlstm_norm, rg_conv_w, rg_conv_b, rg_wa, rg_ba, rg_wx,
                        rg_bx, rg_lambda, rg_norm, w_out, norm2, router_w, router_bias, exp_w1, exp_w3, exp_w2,
                        shared_w1, shared_w3, shared_w2)
    nf = norm_final.reshape(1, D_MODEL)
    cvecs = jnp.concatenate([c_ctx[None], c, jnp.zeros((SUBLANES - 1 - bs, D_MODEL), F32)], axis=0)
    mod = _ada(cvecs, w_ada[l], b_ada[l]).reshape(SUBLANES, N_MOD, D_MODEL)

    xp2d, lw['w1'], lw['w3'], lw['w2'] = lax.optimization_barrier(
        (x_prompt.reshape(bp * tp, D_MODEL), lw['w1'], lw['w3'], lw['w2']))

    mod_p, mod_s = mod[0:1], mod[1:1 + bs]
    tm_p, tm_s = _tiles(tp, False)['tok'], _tiles(ts, True)['tok']
    rp, (cc, nc_, mc), (hc,) = _mixers(
        xp2d,
        (jnp.zeros((tm_p // GRID_W, D_MODEL // 2), F32), jnp.zeros((GRID_W, D_MODEL // 2), F32)), mod_p,
        jnp.zeros((1, nd, HEAD, HEAD), F32), jnp.zeros((1, nd, HEAD), F32), jnp.zeros((1, nd, HEAD), F32),
        jnp.zeros((1, 2, D_RG), F32), jnp.zeros((N_EXPERTS, LANES), I32), lw, bp, tp, True)
    rs, _, _ = _mixers(
        x_sample.reshape(bs * ts, D_MODEL), _grid_pos_tables(ts, D_MODEL), mod_s,
        state_mlstm_C[:, l].reshape(bs, nd, HEAD, HEAD), state_mlstm_n[:, l].reshape(bs, nd, HEAD),
        jnp.broadcast_to(state_mlstm_m[:, l].reshape(bs, nd, 1), (bs, nd, HEAD)),
        state_rglru_h[:, l], jnp.zeros((N_EXPERTS, LANES), I32), lw, bs, ts, False)
    yp = _moe_out(_routed_experts([rp], lw), 0, rp[1], rp[4], lw['sw13'], lw['sw2'], rp[0], mod_p, tp, tm_p, nf)
    ys = _moe_out(_routed_experts([rs], lw), 0, rs[1], rs[4], lw['sw13'], lw['sw2'], rs[0], mod_s, ts, tm_s, nf)

    y_prompt = yp.reshape(bp, tp, D_MODEL)
    y_sample = ys.reshape(bs, ts, D_MODEL)
    new_c = cc.reshape(bp, 1, 2, N_HEADS, HEAD, HEAD)
    new_n = nc_.reshape(bp, 1, 2, N_HEADS, HEAD)
    new_m = mc[:, :, 0].reshape(bp, 1, 2, N_HEADS)
    new_h = hc.reshape(bp, 1, 2, D_RG)
    return (y_prompt, y_sample, new_c, new_n, new_m, new_h)
```

```python
import functools

import jax
import jax.numpy as jnp
from jax import lax
from jax.experimental import pallas as pl
from jax.experimental.pallas import tpu as pltpu
from jax.experimental.pallas import tpu_sc as plsc

F32 = jnp.float32
BF16 = jnp.bfloat16
I32 = jnp.int32
HIGHEST = lax.Precision.HIGHEST

D_MODEL = 1024
N_MOD = 6
D_ML = 512
N_HEADS = 4
HEAD = 128
D_RG = 512
N_RG_BLOCKS = 8
RG_BLOCK = 64
RG_C = 8.0
N_GATE = 16
N_EXPERTS = 64
N_GROUPS = 8
GROUP = 8
TOPK_GROUPS = 4
TOP_K = 8
D_EXPERT = 256
ROUTED_SCALE = 2.5
EPS = 1e-6
GRID_W = 64
POS_BASE = 10000.0

RG_SEG_PAD = 8
SC_WINDOW = 128
SC_PACK_WORDS = 65536
D_PACK = D_MODEL // 2
EXPERT_ROWS = 512
EXPERT_BLOCKS_PER_STEP = 8
EXPERT_RING = 10

SUBLANES = 8
LANES = 128
VMEM_LIMIT = 48 * 1024 * 1024
VMEM_LIMIT_EXPERTS = 58 * 1024 * 1024


def _params(*sem, vmem=VMEM_LIMIT):
    return pltpu.CompilerParams(dimension_semantics=sem, vmem_limit_bytes=vmem)


def _tiles(t, per_sequence_mod):
    cap = t if per_sequence_mod else 1 << 30
    return dict(
        tok=min(512, cap),
        chunk=min(256, t),
        scan=min(512, t),
    )


def _silu(x):
    return x * jax.nn.sigmoid(x)


def _softplus(x):
    return jnp.maximum(x, 0.0) + jnp.log1p(jnp.exp(-jnp.abs(x)))


def _rms(x, g):
    return x * lax.rsqrt(jnp.mean(x * x, axis=-1, keepdims=True) + EPS) * g


def _bf16_pieces(x):
    hi = x.astype(BF16)
    r = x - hi.astype(F32)
    mid = r.astype(BF16)
    return hi, mid, (r - mid.astype(F32)).astype(BF16)


def _pack_bf16_pairs(x):
    w = x.shape[1] // 2
    hi = lax.bitcast_convert_type(x[:, :w].astype(BF16).astype(F32), I32)
    lo = lax.bitcast_convert_type(x[:, w:].astype(BF16).astype(F32), I32)
    return hi | lax.shift_right_logical(lo, jnp.full(lo.shape, 16, I32))


def _unpack_bf16_pairs(p):
    hi = lax.bitcast_convert_type(p & jnp.int32(-65536), F32)
    lo = lax.bitcast_convert_type(lax.shift_left(p, jnp.full(p.shape, 16, I32)), F32)
    return hi, lo


def _ada_kernel(c_ref, w_ref, b_ref, o_ref):
    s = _silu(c_ref[...])
    o_ref[...] = jnp.dot(s, w_ref[...], precision=HIGHEST, preferred_element_type=F32) + b_ref[...]


def _ada(cvecs, w_ada, b_ada):
    n_out = w_ada.shape[1]
    tn = 1536
    return pl.pallas_call(
        _ada_kernel,
        grid=(n_out // tn,),
        in_specs=[
            pl.BlockSpec((SUBLANES, D_MODEL), lambda j: (0, 0)),
            pl.BlockSpec((D_MODEL, tn), lambda j: (0, j)),
            pl.BlockSpec((1, tn), lambda j: (0, j)),
        ],
        out_specs=pl.BlockSpec((SUBLANES, tn), lambda j: (0, j)),
        out_shape=jax.ShapeDtypeStruct((SUBLANES, n_out), F32),
        name="ada",
        compiler_params=_params("arbitrary"),
    )(cvecs, w_ada, b_ada.reshape(1, n_out))


def _pos_tile(rt_ref, ct_ref):
    left = jnp.concatenate([jnp.broadcast_to(rt_ref[r:r + 1, :], (GRID_W, rt_ref.shape[1]))
                            for r in range(rt_ref.shape[0])], axis=0)
    right = jnp.concatenate([ct_ref[...]] * rt_ref.shape[0], axis=0)
    return jnp.concatenate([left, right], axis=1)


def _in_proj_kernel(x_ref, rt_ref, ct_ref, mod_ref, n1_ref, wq_ref, wr_ref, wgh_ref, gb_ref,
                    qt_ref, k_ref, kt_ref, v_ref, vt_ref, ot_ref, xr_ref, gr_ref, g_ref, gt_ref):
    x = x_ref[...] + _pos_tile(rt_ref, ct_ref)
    hn = _rms(x, n1_ref[...]) * (1.0 + mod_ref[0, 1:2, :]) + mod_ref[0, 0:1, :]
    hb = hn.astype(BF16)
    z = jnp.dot(hb, wq_ref[...], preferred_element_type=F32)
    k = z[:, D_ML:2 * D_ML] * (HEAD ** -0.5)
    v = z[:, 2 * D_ML:3 * D_ML]
    qt_ref[...] = z[:, 0:D_ML].T.astype(BF16)
    k_ref[...] = k.astype(BF16)
    kt_ref[...] = k.T.astype(BF16)
    v_ref[...] = v.astype(BF16)
    vt_ref[...] = v.T.astype(BF16)
    ot_ref[...] = z[:, 3 * D_ML:4 * D_ML].T
    zr = jnp.dot(hb, wr_ref[...], preferred_element_type=F32)
    xr_ref[...] = zr[:, 0:D_RG]
    gr_ref[...] = zr[:, D_RG:2 * D_RG]
    zg = zr[:, 2 * D_RG:2 * D_RG + LANES]
    h_lo = (hn - hb.astype(F32)).astype(BF16)
    g = (zg + pltpu.roll(zg, LANES - N_GATE, 1)
         + jnp.dot(h_lo, wgh_ref[...], preferred_element_type=F32) + gb_ref[...])
    col = lax.broadcasted_iota(I32, g.shape, 1)
    g = jnp.where((col & 4) != 0, -_softplus(-g), g)
    g_ref[...] = g[:, 0:N_GATE]
    gt_ref[...] = g.T[0:N_GATE, :]


def _mod_index(bm, tm, t):
    if bm > 1:
        return lambda i: ((i * tm) // t, 0, 0)
    return lambda i: (0, 0, 0)


def _pos_specs(rtab, ctab, tm):
    rows = tm // GRID_W
    period = rtab.shape[0] // rows
    return [pl.BlockSpec((rows, D_MODEL // 2), lambda i: (i % period, 0)),
            pl.BlockSpec((GRID_W, D_MODEL // 2), lambda i: (0, 0))]


def _in_proj(x2d, rtab, ctab, mod, t, tm, norm1, wq, wr, wgh, gbias):
    n = x2d.shape[0]
    tok = lambda i: (i, 0)
    tok_t = lambda i: (0, i)
    const = lambda i: (0, 0)
    f = lambda w: jax.ShapeDtypeStruct((n, w), F32)
    row16 = jax.ShapeDtypeStruct((n, D_ML), BF16)
    col16 = jax.ShapeDtypeStruct((D_ML, n), BF16)
    return pl.pallas_call(
        _in_proj_kernel,
        grid=(n // tm,),
        in_specs=[pl.BlockSpec((tm, D_MODEL), tok)] + _pos_specs(rtab, ctab, tm) + [
            pl.BlockSpec((1, N_MOD, D_MODEL), _mod_index(mod.shape[0], tm, t)),
            pl.BlockSpec((1, D_MODEL), const),
            pl.BlockSpec((D_MODEL, 4 * D_ML), const),
            pl.BlockSpec((D_MODEL, 2 * D_RG + LANES), const),
            pl.BlockSpec((D_MODEL, LANES), const),
            pl.BlockSpec((1, LANES), const),
        ],
        out_specs=[pl.BlockSpec((D_ML, tm), tok_t), pl.BlockSpec((tm, D_ML), tok), pl.BlockSpec((D_ML, tm), tok_t),
                   pl.BlockSpec((tm, D_ML), tok), pl.BlockSpec((D_ML, tm), tok_t), pl.BlockSpec((D_ML, tm), tok_t),
                   pl.BlockSpec((tm, D_RG), tok), pl.BlockSpec((tm, D_RG), tok),
                   pl.BlockSpec((tm, N_GATE), tok), pl.BlockSpec((N_GATE, tm), tok_t)],
        out_shape=[col16, row16, col16, row16, col16, jax.ShapeDtypeStruct((D_ML, n), F32),
                   f(D_RG), f(D_RG), f(N_GATE), jax.ShapeDtypeStruct((N_GATE, n), F32)],
        name="in_proj",
        compiler_params=_params("arbitrary"),
    )(x2d, rtab, ctab, mod, norm1, wq, wr, wgh, gbias)


def _mlstm_kernel(chunk, nc, emit_state, *refs):
    (qtf, kf, ktf, vf, vtf, gf, gtf, qtb, kb, ktb, vb, vtb, gb, gtb, c0_ref, n0_ref, m0_ref) = refs[:17]
    if emit_state:
        hf_ref, hb_ref, c_out, n_out, m_out, c_sc, n_sc, m_sc = refs[17:]
    else:
        hf_ref, hb_ref, c_sc, n_sc, m_sc = refs[17:]
    i = pl.program_id(1)

    @pl.when(i == 0)
    def _():
        c_sc[...] = c0_ref[0]
        n_sc[...] = n0_ref[0]
        m_sc[...] = m0_ref[0]

    key = lax.broadcasted_iota(I32, (chunk, chunk), 0)
    qry = lax.broadcasted_iota(I32, (chunk, chunk), 1)
    hd = []
    for d, (qt_ref, k_ref, kt_ref, v_ref, vt_ref, g_ref, gt_ref, h_ref) in enumerate(
            ((qtf, kf, ktf, vf, vtf, gf, gtf, hf_ref), (qtb, kb, ktb, vb, vtb, gb, gtb, hb_ref))):
        tri = (key <= qry) if d == 0 else (key >= qry)
        tri_t = (qry <= key) if d == 0 else (qry >= key)
        g = g_ref[...]
        gt = gt_ref[...]
        rows3 = jnp.dot(jnp.concatenate(_bf16_pieces(gt), axis=0), tri.astype(BF16), preferred_element_type=F32)
        brow = rows3[0:N_GATE] + rows3[N_GATE:2 * N_GATE] + rows3[2 * N_GATE:3 * N_GATE]
        tri_t16 = tri_t.astype(BF16)
        bcol = sum(jnp.dot(tri_t16, piece, preferred_element_type=F32) for piece in _bf16_pieces(g))
        blast = bcol[chunk - 1:chunk, :] if d == 0 else bcol[0:1, :]
        for h in range(N_HEADS):
            ci = d * 8 + h
            cf = d * 8 + 4 + h
            j = d * N_HEADS + h
            sl = slice(h * HEAD, (h + 1) * HEAD)
            hd.append(dict(
                j=j, sl=sl, tri=tri, h_ref=h_ref, qt=qt_ref[sl, :], k=k_ref[:, sl], kt=kt_ref[sl, :],
                v=v_ref[:, sl], vt=vt_ref[sl, :], b_row=brow[cf:cf + 1, :],
                gate_col=g[:, ci:ci + 1] - bcol[:, cf:cf + 1], gate_row=gt[ci:ci + 1, :] - brow[cf:cf + 1, :],
                b_last=blast[:, cf:cf + 1], m_prev=m_sc[j:j + 1, 0:1], c_prev=c_sc[j], n_prev=n_sc[j:j + 1, :]))
    for x in hd:
        top = jnp.max(jnp.where(x['tri'], x['gate_col'], -jnp.inf), axis=0, keepdims=True)
        mx = jnp.maximum(x['m_prev'], top)
        x['dm'] = jnp.exp(jnp.where(x['tri'], x['gate_col'] - mx, -jnp.inf))
        x['w_inter'] = jnp.exp(x['m_prev'] - mx)
        x['floor'] = jnp.exp(-(x['b_row'] + mx))
    for x in hd:
        x['st'] = jnp.dot(x['k'], x['qt'], preferred_element_type=F32) * x['dm']
    for x in hd:
        inter = lax.dot_general(x['c_prev'].astype(BF16), x['qt'], (((0,), (0,)), ((), ())),
                                preferred_element_type=F32)
        num = x['w_inter'] * inter + jnp.dot(x['vt'], x['st'].astype(BF16), preferred_element_type=F32)
        qn = jnp.dot(jnp.broadcast_to(x['n_prev'], (SUBLANES, HEAD)).astype(BF16), x['qt'],
                     preferred_element_type=F32)[0:1, :]
        den = x['w_inter'] * qn + jnp.sum(x['st'], axis=0, keepdims=True)
        x['h_ref'][x['sl'], :] = num / jnp.maximum(jnp.abs(den), x['floor'])
    for x in hd:
        j = x['j']
        log_w = x['b_last'] + x['gate_row']
        m_new = jnp.maximum(x['b_last'] + x['m_prev'], jnp.max(log_w, axis=1, keepdims=True))
        decay = jnp.exp(x['b_last'] + x['m_prev'] - m_new)
        w_row = jnp.exp(log_w - m_new)
        kwt = (x['kt'].astype(F32) * w_row).astype(BF16)
        c_sc[j] = decay * x['c_prev'] + jnp.dot(kwt, x['v'], preferred_element_type=F32)
        n_sc[j:j + 1, :] = decay * x['n_prev'] + jnp.dot(
            jnp.broadcast_to(w_row, (SUBLANES, chunk)).astype(BF16), x['k'], preferred_element_type=F32)[0:1, :]
        m_sc[j:j + 1, :] = jnp.broadcast_to(m_new, (1, HEAD))

    if emit_state:
        @pl.when(i == nc - 1)
        def _():
            c_out[0] = c_sc[...]
            n_out[0] = n_sc[...]
            m_out[0] = m_sc[...]


def _state_index(bm):
    if bm > 1:
        return lambda b, i: (b,) + (0,) * 3, lambda b, i: (b, 0, 0)
    return lambda b, i: (0,) * 4, lambda b, i: (0, 0, 0)


def _mlstm(qt, k, kt, v, vt, g, gt, c0, n0, m0, b, t, chunk, emit_state):
    nc = t // chunk
    n = b * t
    nd = 2 * N_HEADS
    fwd = lambda bi, i: (bi * nc + i, 0)
    bwd = lambda bi, i: (bi * nc + nc - 1 - i, 0)
    fwd_t = lambda bi, i: (0, bi * nc + i)
    bwd_t = lambda bi, i: (0, bi * nc + nc - 1 - i)
    c_idx, n_idx = _state_index(c0.shape[0])
    rows = lambda m: pl.BlockSpec((chunk, D_ML), m)
    cols = lambda m: pl.BlockSpec((D_ML, chunk), m)
    one_dir = lambda m, mt: [cols(mt), rows(m), cols(mt), rows(m), cols(mt),
                             pl.BlockSpec((chunk, N_GATE), m), pl.BlockSpec((N_GATE, chunk), mt)]
    in_specs = (one_dir(fwd, fwd_t) + one_dir(bwd, bwd_t)
                + [pl.BlockSpec((1, nd, HEAD, HEAD), c_idx),
                   pl.BlockSpec((1, nd, HEAD), n_idx), pl.BlockSpec((1, nd, HEAD), n_idx)])
    out_specs = [cols(fwd_t), cols(bwd_t)]
    out_shape = [jax.ShapeDtypeStruct((D_ML, n), F32)] * 2
    if emit_state:
        out_specs += [pl.BlockSpec((1, nd, HEAD, HEAD), lambda bi, i: (bi, 0, 0, 0)),
                      pl.BlockSpec((1, nd, HEAD), lambda bi, i: (bi, 0, 0)),
                      pl.BlockSpec((1, nd, HEAD), lambda bi, i: (bi, 0, 0))]
        out_shape += [jax.ShapeDtypeStruct((b, nd, HEAD, HEAD), F32),
                      jax.ShapeDtypeStruct((b, nd, HEAD), F32),
                      jax.ShapeDtypeStruct((b, nd, HEAD), F32)]
    return pl.pallas_call(
        functools.partial(_mlstm_kernel, chunk, nc, emit_state),
        grid=(b, nc),
        in_specs=in_specs,
        out_specs=out_specs,
        out_shape=out_shape,
        scratch_shapes=[pltpu.VMEM((nd, HEAD, HEAD), F32), pltpu.VMEM((nd, HEAD), F32),
                        pltpu.VMEM((nd, HEAD), F32)],
        name="mlstm",
        compiler_params=_params("arbitrary", "arbitrary"),
    )(qt, k, kt, v, vt, g, gt, qt, k, kt, v, vt, g, gt, c0, n0, m0)


def _neg_expm1_2x(x):
    t = jnp.tanh(x)
    return -2.0 * t / (1.0 - t)


def _rglru_kernel(tb, nb, emit_state, *refs):
    (xf, xf_prev, xf_next, xb, xb_prev, xb_next, h0_ref, w_ref, bias_ref, lam_ref, cw_ref, cb_ref) = refs[:12]
    n_out = 3 if emit_state else 2
    hf_ref, hb_ref = refs[12:14]
    hfin_ref = refs[14] if emit_state else None
    carry, af_sc, uf_sc, ab_sc, ub_sc, hf_sc, pf_sc, hb_sc, pb_sc = refs[12 + n_out:]
    i = pl.program_id(1)

    @pl.when(i == 0)
    def _():
        carry[...] = h0_ref[0]

    row8 = lax.broadcasted_iota(I32, (SUBLANES, D_RG), 0)
    cw = cw_ref[...]
    log_a_per_r = -RG_C * _softplus(-lam_ref[...])

    def taps(xm2, xm1, x0, xp1):
        return cb_ref[...] + xm2 * cw[0:1, :] + xm1 * cw[1:2, :] + x0 * cw[2:3, :] + xp1 * cw[3:4, :]

    def conv(main_ref, prev_ref, next_ref, first, last):
        main = main_ref[...]
        prev = jnp.where(first, 0.0, prev_ref[...])
        nxt = jnp.where(last, 0.0, next_ref[...])
        body = taps(pltpu.roll(main, 2, 0), pltpu.roll(main, 1, 0), main, pltpu.roll(main, tb - 1, 0))
        e = SUBLANES
        head, tail = main[0:e, :], main[tb - e:tb, :]
        before_tail = main[tb - 2 * e:tb - e, :]
        fix_head = taps(
            jnp.where(row8 == 0, prev[6:7, :], jnp.where(row8 == 1, prev[7:8, :], pltpu.roll(head, 2, 0))),
            jnp.where(row8 == 0, prev[7:8, :], pltpu.roll(head, 1, 0)), head,
            jnp.where(row8 == e - 1, main[e:e + 1, :], pltpu.roll(head, e - 1, 0)))
        fix_tail = taps(
            jnp.where(row8 == 0, before_tail[6:7, :],
                      jnp.where(row8 == 1, before_tail[7:8, :], pltpu.roll(tail, 2, 0))),
            jnp.where(row8 == 0, before_tail[7:8, :], pltpu.roll(tail, 1, 0)), tail,
            jnp.where(row8 == e - 1, nxt[0:1, :], pltpu.roll(tail, e - 1, 0)))
        return jnp.concatenate([fix_head, body[e:tb - e, :], fix_tail], axis=0)

    def recurrence_terms(xc, d):
        z = jnp.dot(xc.astype(BF16), w_ref[:, d * 2 * D_RG:(d + 1) * 2 * D_RG],
                    preferred_element_type=F32) + bias_ref[:, d * 2 * D_RG:(d + 1) * 2 * D_RG]
        r = jax.nn.sigmoid(z[:, 0:D_RG])
        ig = jax.nn.sigmoid(z[:, D_RG:2 * D_RG])
        log_a = r * log_a_per_r[d:d + 1, :]
        a = jnp.exp(log_a)
        u = jnp.sqrt(_neg_expm1_2x(log_a)) * (ig * xc)
        return a, u

    a_f, u_f = recurrence_terms(conv(xf, xf_prev, xf_next, i == 0, i == nb - 1), 0)
    a_b, u_b = recurrence_terms(conv(xb, xb_prev, xb_next, i == nb - 1, i == 0), 1)
    seg = tb // SUBLANES
    pitch = seg + RG_SEG_PAD
    ncol = D_RG // LANES
    for lc in range(ncol):
        lanes = slice(lc * LANES, (lc + 1) * LANES)
        for s in range(SUBLANES):
            src = slice(s * seg, (s + 1) * seg)
            dst = slice(s * pitch, s * pitch + seg)
            af_sc[lc, dst, :], uf_sc[lc, dst, :] = a_f[src, lanes], u_f[src, lanes]
            ab_sc[lc, dst, :], ub_sc[lc, dst, :] = a_b[src, lanes], u_b[src, lanes]
    slab = lambda k: (slice(None), pl.ds(k, SUBLANES, stride=pitch), slice(None))
    hf = jnp.zeros((ncol, SUBLANES, LANES), F32)
    hb = jnp.zeros((ncol, SUBLANES, LANES), F32)
    pf = jnp.ones((ncol, SUBLANES, LANES), F32)
    pb = jnp.ones((ncol, SUBLANES, LANES), F32)
    for k in range(seg):
        kb = seg - 1 - k
        ak = af_sc[slab(k)]
        hf = ak * hf + uf_sc[slab(k)]
        pf = pf * ak
        hf_sc[slab(k)] = hf
        pf_sc[slab(k)] = pf
        ak = ab_sc[slab(kb)]
        hb = ak * hb + ub_sc[slab(kb)]
        pb = pb * ak
        hb_sc[slab(kb)] = hb
        pb_sc[slab(kb)] = pb
    for lc in range(ncol):
        lanes = slice(lc * LANES, (lc + 1) * LANES)
        c = carry[0:1, lanes]
        cin_f = []
        for s in range(SUBLANES):
            cin_f.append(c)
            c = pf[lc, s:s + 1, :] * c + hf[lc, s:s + 1, :]
        carry[0:1, lanes] = c
        c = carry[1:2, lanes]
        cin_b = [None] * SUBLANES
        for s in reversed(range(SUBLANES)):
            cin_b[s] = c
            c = pb[lc, s:s + 1, :] * c + hb[lc, s:s + 1, :]
        carry[1:2, lanes] = c
        for s in range(SUBLANES):
            rows = slice(s * seg, (s + 1) * seg)
            src = slice(s * pitch, s * pitch + seg)
            hf_ref[rows, lanes] = hf_sc[lc, src, :] + pf_sc[lc, src, :] * cin_f[s]
            hb_ref[rows, lanes] = hb_sc[lc, src, :] + pb_sc[lc, src, :] * cin_b[s]

    if emit_state:
        @pl.when(i == nb - 1)
        def _():
            hfin_ref[0] = carry[...]


def _rglru(xr, h0, wbd, bias, lam, cw, cb, b, t, tb, emit_state):
    nb = t // tb
    n = b * t
    r8 = tb // SUBLANES
    last8 = n // SUBLANES - 1
    fwd = lambda bi, i: (bi * nb + i, 0)
    bwd = lambda bi, i: (bi * nb + nb - 1 - i, 0)
    fwd_prev = lambda bi, i: (jnp.maximum((bi * nb + i) * r8 - 1, 0), 0)
    fwd_next = lambda bi, i: (jnp.minimum((bi * nb + i + 1) * r8, last8), 0)
    bwd_prev = lambda bi, i: (jnp.maximum((bi * nb + nb - 1 - i) * r8 - 1, 0), 0)
    bwd_next = lambda bi, i: (jnp.minimum((bi * nb + nb - i) * r8, last8), 0)
    const = lambda bi, i: (0, 0)
    h_idx = (lambda bi, i: (bi, 0, 0)) if h0.shape[0] > 1 else (lambda bi, i: (0, 0, 0))
    halo = lambda m: pl.BlockSpec((SUBLANES, D_RG), m)
    in_specs = [pl.BlockSpec((tb, D_RG), fwd), halo(fwd_prev), halo(fwd_next),
                pl.BlockSpec((tb, D_RG), bwd), halo(bwd_prev), halo(bwd_next),
                pl.BlockSpec((1, 2, D_RG), h_idx),
                pl.BlockSpec((D_RG, 4 * D_RG), const), pl.BlockSpec((1, 4 * D_RG), const),
                pl.BlockSpec((2, D_RG), const), pl.BlockSpec((4, D_RG), const), pl.BlockSpec((1, D_RG), const)]
    out_specs = [pl.BlockSpec((tb, D_RG), fwd), pl.BlockSpec((tb, D_RG), bwd)]
    out_shape = [jax.ShapeDtypeStruct((n, D_RG), F32)] * 2
    if emit_state:
        out_specs.append(pl.BlockSpec((1, 2, D_RG), lambda bi, i: (bi, 0, 0)))
        out_shape.append(jax.ShapeDtypeStruct((b, 2, D_RG), F32))
    return pl.pallas_call(
        functools.partial(_rglru_kernel, tb, nb, emit_state),
        grid=(b, nb),
        in_specs=in_specs,
        out_specs=out_specs,
        out_shape=out_shape,
        scratch_shapes=[pltpu.VMEM((2, D_RG), F32)]
        + [pltpu.VMEM((D_RG // LANES, tb + SUBLANES * RG_SEG_PAD, LANES), F32)] * 8,
        name="rglru",
        compiler_params=_params("arbitrary", "arbitrary"),
    )(xr, xr, xr, xr, xr, xr, h0, wbd, bias, lam, cw, cb)


def _route(s, sb):
    tm = s.shape[1]
    neg = -jnp.inf
    sub = lax.broadcasted_iota(I32, (GROUP, tm), 0)
    blocks = [sb[gi * GROUP:(gi + 1) * GROUP, :] for gi in range(N_GROUPS)]
    gscore = []
    for blk in blocks:
        m1 = jnp.max(blk, axis=0, keepdims=True)
        first = jnp.min(jnp.where(blk == m1, sub, GROUP), axis=0, keepdims=True)
        m2 = jnp.max(jnp.where(sub == first, neg, blk), axis=0, keepdims=True)
        gscore.append(m1 + m2)
    masked = []
    for gi in range(N_GROUPS):
        rank = jnp.zeros((1, tm), F32)
        for gj in range(N_GROUPS):
            if gj == gi:
                continue
            ahead = (gscore[gj] >= gscore[gi]) if gj < gi else (gscore[gj] > gscore[gi])
            rank = rank + jnp.where(ahead, 1.0, 0.0)
        masked.append(jnp.where(rank < TOPK_GROUPS, blocks[gi], neg))
    v = jnp.concatenate(masked, axis=0)
    eid = lax.broadcasted_iota(I32, (N_EXPERTS, tm), 0)
    sel = jnp.zeros((N_EXPERTS, tm), F32)
    picks = []
    for _ in range(TOP_K):
        mx = jnp.max(v, axis=0, keepdims=True)
        idx = jnp.min(jnp.where(v == mx, eid, N_EXPERTS), axis=0, keepdims=True)
        pick = eid == idx
        picks.append(pick)
        sel = jnp.where(pick, 1.0, sel)
        v = jnp.where(pick, neg, v)
    ws = s * sel
    return ws / jnp.sum(ws, axis=0, keepdims=True) * ROUTED_SCALE, sel, picks


def _mix_out_kernel(hmf_ref, hmb_ref, ot_ref, hrf_ref, hrb_ref, gr_ref, x_ref, rt_ref, ct_ref, mod_ref,
                    mln_ref, rgn_ref, wo_ml_ref, wo_rg_ref, n2_ref, rwt_ref, rb_ref, cnt0_ref,
                    x1_ref, hn2p_ref, ek_ref, pk_ref, wtok_ref, cnt_ref, cnt_sc):
    i = pl.program_id(0)
    tm = x_ref.shape[0]

    @pl.when(i == 0)
    def _():
        cnt_sc[...] = cnt0_ref[...].astype(F32)

    hm = hmf_ref[...] + hmb_ref[...]
    heads = []
    for h in range(N_HEADS):
        seg = hm[h * HEAD:(h + 1) * HEAD, :]
        heads.append(seg * lax.rsqrt(jnp.mean(seg * seg, axis=0, keepdims=True) + EPS))
    y_ml_t = jnp.concatenate(heads, axis=0) * mln_ref[...] * jax.nn.sigmoid(ot_ref[...])
    y_rg = _rms(hrf_ref[...] + hrb_ref[...], rgn_ref[...]) * jax.nn.gelu(gr_ref[...])
    mix = (lax.dot_general(y_ml_t.astype(BF16), wo_ml_ref[...], (((0,), (0,)), ((), ())),
                           preferred_element_type=F32)
           + jnp.dot(y_rg.astype(BF16), wo_rg_ref[...], preferred_element_type=F32))
    x1 = x_ref[...] + _pos_tile(rt_ref, ct_ref) + mod_ref[0, 2:3, :] * mix
    x1_ref[...] = x1
    hn2 = _rms(x1, n2_ref[...]) * (1.0 + mod_ref[0, 4:5, :]) + mod_ref[0, 3:4, :]
    hn2p_ref[...] = _pack_bf16_pairs(hn2)
    hb = hn2.astype(BF16)
    h_lo = (hn2 - hb.astype(F32)).astype(BF16)
    nt = (((1,), (1,)), ((), ()))
    two = lax.dot_general(rwt_ref[...], hb, nt, preferred_element_type=F32)
    logits_t = (two[0:N_EXPERTS, :] + two[N_EXPERTS:2 * N_EXPERTS, :]
                + lax.dot_general(rwt_ref[0:N_EXPERTS, :], h_lo, nt, preferred_element_type=F32))
    s = jax.nn.sigmoid(logits_t)
    wt, sel, picks = _route(s, s + rb_ref[...])

    earlier = (lax.broadcasted_iota(I32, (tm, tm), 0) < lax.broadcasted_iota(I32, (tm, tm), 1))
    prefix = jnp.dot(sel.astype(BF16), earlier.astype(BF16), preferred_element_type=F32)
    pos_all = cnt_sc[:, 0:1] + prefix
    eid = lax.broadcasted_iota(I32, (N_EXPERTS, tm), 0)
    eid_f = eid.astype(F32)
    row8 = lax.broadcasted_iota(I32, (TOP_K, tm), 0)
    ek = jnp.zeros((TOP_K, tm), F32)
    pk = jnp.zeros((TOP_K, tm), F32)
    wk = jnp.zeros((N_EXPERTS, tm), F32)
    for k, pick in enumerate(picks):
        take = lambda a: jnp.sum(jnp.where(pick, a, 0.0), axis=0, keepdims=True)
        ek = jnp.where(row8 == k, take(eid_f), ek)
        pk = jnp.where(row8 == k, take(pos_all), pk)
        wk = jnp.where(eid == k, take(wt), wk)
    ek_ref[...] = ek.astype(I32)
    pk_ref[...] = pk.astype(I32)
    wtok_ref[...] = wk.T
    cnt_sc[...] += jnp.broadcast_to(jnp.sum(sel, axis=1, keepdims=True), cnt_sc.shape)

    @pl.when(i == pl.num_programs(0) - 1)
    def _():
        cnt_ref[...] = cnt_sc[...].astype(I32)


def _mix_out(hmf, hmb, ot, hrf, hrb, gr, x2d, rtab, ctab, mod, t, tm, mln, rgn, wo_ml, wo_rg, norm2, rwt, rbias,
             cnt0):
    n = x2d.shape[0]
    tok = lambda i: (i, 0)
    tok_t = lambda i: (0, i)
    const = lambda i: (0, 0)
    return pl.pallas_call(
        _mix_out_kernel,
        grid=(n // tm,),
        in_specs=[pl.BlockSpec((D_ML, tm), tok_t)] * 3 + [pl.BlockSpec((tm, D_RG), tok)] * 3 + [
            pl.BlockSpec((tm, D_MODEL), tok)] + _pos_specs(rtab, ctab, tm) + [
            pl.BlockSpec((1, N_MOD, D_MODEL), _mod_index(mod.shape[0], tm, t)),
            pl.BlockSpec((D_ML, 1), const), pl.BlockSpec((1, D_RG), const),
            pl.BlockSpec((D_ML, D_MODEL), const), pl.BlockSpec((D_RG, D_MODEL), const),
            pl.BlockSpec((1, D_MODEL), const),
            pl.BlockSpec((2 * N_EXPERTS, D_MODEL), const), pl.BlockSpec((N_EXPERTS, 1), const),
            pl.BlockSpec((N_EXPERTS, LANES), const),
        ],
        out_specs=[pl.BlockSpec((tm, D_MODEL), tok), pl.BlockSpec((tm, D_PACK), tok),
                   pl.BlockSpec((TOP_K, tm), tok_t), pl.BlockSpec((TOP_K, tm), tok_t),
                   pl.BlockSpec((tm, N_EXPERTS), tok), pl.BlockSpec((N_EXPERTS, LANES), const)],
        out_shape=[jax.ShapeDtypeStruct((n, D_MODEL), F32), jax.ShapeDtypeStruct((n, D_PACK), I32),
                   jax.ShapeDtypeStruct((TOP_K, n), I32), jax.ShapeDtypeStruct((TOP_K, n), I32),
                   jax.ShapeDtypeStruct((n, N_EXPERTS), F32), jax.ShapeDtypeStruct((N_EXPERTS, LANES), I32)],
        scratch_shapes=[pltpu.VMEM((N_EXPERTS, LANES), F32)],
        name="mix_out",
        compiler_params=_params("arbitrary"),
    )(hmf, hmb, ot, hrf, hrb, gr, x2d, rtab, ctab, mod, mln, rgn, wo_ml, wo_rg, norm2, rwt, rbias, cnt0)


def _sc_mesh():
    return plsc.VectorSubcoreMesh(core_axis_name="core", subcore_axis_name="subcore")


def _sc_worker():
    info = plsc.get_sparse_core_info()
    return lax.axis_index("subcore") * info.num_cores + lax.axis_index("core"), info.num_cores * info.num_subcores


def _sc_dispatch(xp, dest3, n_slots):
    n, w = xp.shape
    nwin = n // SC_WINDOW

    @pl.kernel(out_type=jax.ShapeDtypeStruct((n_slots, w), xp.dtype), mesh=_sc_mesh(),
               scratch_types=[pltpu.VMEM((SC_WINDOW, w), xp.dtype), pltpu.VMEM((TOP_K, SC_WINDOW), I32)],
               name="sc_dispatch")
    def k(x_hbm, i_hbm, o_hbm, x_v, i_v):
        wid, nworkers = _sc_worker()
        per = nwin // nworkers

        @pl.loop(0, per)
        def _(s):
            win = wid * per + s
            pltpu.sync_copy(x_hbm.at[pl.ds(win * SC_WINDOW, SC_WINDOW)], x_v)
            pltpu.sync_copy(i_hbm.at[win], i_v)
            for j in range(TOP_K):
                pltpu.sync_copy(x_v, o_hbm.at[i_v.at[j]])

    return k(xp, dest3)


def _sc_pack_rows(w):
    r, c = w.shape
    half = c // 2
    lanes = plsc.get_sparse_core_info().num_lanes
    chunk = SC_PACK_WORDS // c
    nchunks = r // chunk

    def rne_high(x):
        u = plsc.bitcast(x, I32)
        u = u + 0x7FFF + (lax.shift_right_logical(u, jnp.full(u.shape, 16, I32)) & 1)
        return u & jnp.int32(-65536)

    @pl.kernel(out_type=jax.ShapeDtypeStruct((r, half), I32), mesh=_sc_mesh(),
               scratch_types=[pltpu.VMEM((chunk, c), F32), pltpu.VMEM((chunk, half), I32)],
               compiler_params=pltpu.CompilerParams(needs_layout_passes=False), name="sc_pack")
    def k(w_hbm, o_hbm, in_v, out_v):
        wid, nworkers = _sc_worker()
        per = nchunks // nworkers

        @pl.loop(0, per)
        def _(s):
            first = (wid * per + s) * chunk
            pltpu.sync_copy(w_hbm.at[pl.ds(first, chunk)], in_v)

            @pl.loop(0, chunk)
            def _(row):
                for j in range(half // lanes):
                    at = lambda off: (row, pl.ds(off + j * lanes, lanes))
                    hi = rne_high(in_v.at[*at(0)][...])
                    lo = rne_high(in_v.at[*at(half)][...])
                    out_v.at[*at(0)][...] = hi | lax.shift_right_logical(lo, jnp.full(lo.shape, 16, I32))

            pltpu.sync_copy(out_v, o_hbm.at[pl.ds(first, chunk)])

    return k(w)


def _sc_combine_gather(ys, dest3):
    nwin = dest3.shape[0]
    w = ys.shape[1]

    @pl.kernel(out_type=jax.ShapeDtypeStruct((nwin, TOP_K, SC_WINDOW, w), ys.dtype), mesh=_sc_mesh(),
               scratch_types=[pltpu.VMEM((SC_WINDOW, w), ys.dtype), pltpu.VMEM((TOP_K, SC_WINDOW), I32)],
               name="sc_combine")
    def k(y_hbm, i_hbm, o_hbm, y_v, i_v):
        wid, nworkers = _sc_worker()
        per = nwin // nworkers

        @pl.loop(0, per)
        def _(s):
            win = wid * per + s
            pltpu.sync_copy(i_hbm.at[win], i_v)
            for j in range(TOP_K):
                pltpu.sync_copy(y_hbm.at[i_v.at[j]], y_v)
                pltpu.sync_copy(y_v, o_hbm.at[win, j])

    return k(ys, dest3)


def _swiglu(x, w13):
    h = jnp.dot(x, w13, preferred_element_type=F32)
    return _silu(h[:, 0:D_EXPERT]) * h[:, D_EXPERT:2 * D_EXPERT]


def _unpack_rows_bf16(p):
    hi, lo = _unpack_bf16_pairs(p)
    return jnp.concatenate([hi.astype(BF16), lo.astype(BF16)], axis=1)


def _expert_kernel(rows, rank_ref, expert_ref, nu_ref, nr_ref, x_ref, w1_hbm, w3_hbm, w2_hbm, y_ref,
                   w1buf, w3buf, w2buf, sem, progress):
    b = pl.program_id(0)
    g = EXPERT_BLOCKS_PER_STEP
    n_ranks = nr_ref[0]

    @pl.when(b == 0)
    def _():
        progress[0] = 0
        progress[1] = 0

    def copies(r):
        slot = lax.rem(r, EXPERT_RING)
        e = expert_ref[r]
        return (pltpu.make_async_copy(w1_hbm.at[e], w1buf.at[slot], sem.at[slot, 0]),
                pltpu.make_async_copy(w3_hbm.at[e], w3buf.at[slot], sem.at[slot, 1]),
                pltpu.make_async_copy(w2_hbm.at[e], w2buf.at[slot], sem.at[slot, 2]))

    def start(r, carry):
        for cp in copies(r):
            cp.start()
        return carry

    def wait(r, carry):
        for cp in copies(r):
            cp.wait()
        return carry

    first = rank_ref[b * g]
    last = rank_ref[b * g + g - 1]
    started = jnp.minimum(first + EXPERT_RING, n_ranks)
    lax.fori_loop(progress[0], started, start, 0)
    progress[0] = jnp.maximum(progress[0], started)
    needed = jnp.where(b == pl.num_programs(0) - 1, progress[0], jnp.minimum(last + 1, n_ranks))
    lax.fori_loop(progress[1], needed, wait, 0)
    progress[1] = jnp.maximum(progress[1], needed)

    @pl.when(b * g < nu_ref[0])
    def _():
        for j in range(g):
            slot = lax.rem(rank_ref[b * g + j], EXPERT_RING)
            sl = slice(j * rows, (j + 1) * rows)
            x = _unpack_rows_bf16(x_ref[sl, :])
            h = (_silu(jnp.dot(x, _unpack_rows_bf16(w1buf[slot]), preferred_element_type=F32))
                 * jnp.dot(x, _unpack_rows_bf16(w3buf[slot]), preferred_element_type=F32))
            y_ref[sl, :] = _pack_bf16_pairs(
                jnp.dot(h.astype(BF16), _unpack_rows_bf16(w2buf[slot]), preferred_element_type=F32))


def _experts(xs, block_rank, rank_expert, n_used, n_ranks, w1, w3, w2, rows):
    g = EXPERT_BLOCKS_PER_STEP
    nb = xs.shape[0] // rows
    tok = lambda b, *_: (b, 0)
    return pl.pallas_call(
        functools.partial(_expert_kernel, rows),
        grid_spec=pltpu.PrefetchScalarGridSpec(
            num_scalar_prefetch=4,
            grid=(nb // g,),
            in_specs=[pl.BlockSpec((g * rows, D_PACK), tok)] + [pl.BlockSpec(memory_space=pl.ANY)] * 3,
            out_specs=pl.BlockSpec((g * rows, D_PACK), tok),
            scratch_shapes=[pltpu.VMEM((EXPERT_RING, D_MODEL, D_EXPERT // 2), I32),
                            pltpu.VMEM((EXPERT_RING, D_MODEL, D_EXPERT // 2), I32),
                            pltpu.VMEM((EXPERT_RING, D_EXPERT, D_PACK), I32),
                            pltpu.SemaphoreType.DMA((EXPERT_RING, 3)), pltpu.SMEM((2,), I32)],
        ),
        out_shape=jax.ShapeDtypeStruct(xs.shape, I32),
        name="experts",
        compiler_params=_params("arbitrary", vmem=VMEM_LIMIT_EXPERTS),
    )(block_rank, rank_expert, n_used, n_ranks, xs, w1, w3, w2)


def _moe_out_kernel(yk_ref, hn2p_ref, wtok_ref, sw13_ref, sw2_ref, x1_ref, mod_ref, nf_ref, y_ref):
    shared = jnp.dot(_swiglu(_unpack_rows_bf16(hn2p_ref[...]), sw13_ref[...]).astype(BF16), sw2_ref[...],
                     preferred_element_type=F32)
    w = wtok_ref[...]
    parts = []
    for wi in range(yk_ref.shape[0]):
        rows = slice(wi * SC_WINDOW, (wi + 1) * SC_WINDOW)
        a_hi = shared[rows, 0:D_PACK]
        a_lo = shared[rows, D_PACK:D_MODEL]
        for k in range(TOP_K):
            y_hi, y_lo = _unpack_bf16_pairs(yk_ref[wi, k])
            wc = w[rows, k:k + 1]
            a_hi = a_hi + wc * y_hi
            a_lo = a_lo + wc * y_lo
        parts.append(jnp.concatenate([a_hi, a_lo], axis=1))
    x2 = x1_ref[...] + mod_ref[0, 5:6, :] * jnp.concatenate(parts, axis=0)
    y_ref[...] = _rms(x2, nf_ref[...])


def _moe_out(yk, tok0, hn2p, wtok, sw13, sw2, x1, mod, t, tm, norm_final):
    n = hn2p.shape[0]
    tok = lambda i: (i, 0)
    const = lambda i: (0, 0)
    blk0 = tok0 // tm
    return pl.pallas_call(
        _moe_out_kernel,
        grid=(n // tm,),
        in_specs=[
            pl.BlockSpec((tm // SC_WINDOW, TOP_K, SC_WINDOW, D_PACK), lambda i: (i + blk0, 0, 0, 0)),
            pl.BlockSpec((tm, D_PACK), tok),
            pl.BlockSpec((tm, N_EXPERTS), tok),
            pl.BlockSpec((D_MODEL, 2 * D_EXPERT), const),
            pl.BlockSpec((D_EXPERT, D_MODEL), const),
            pl.BlockSpec((tm, D_MODEL), tok),
            pl.BlockSpec((1, N_MOD, D_MODEL), _mod_index(mod.shape[0], tm, t)),
            pl.BlockSpec((1, D_MODEL), const),
        ],
        out_specs=pl.BlockSpec((tm, D_MODEL), tok),
        out_shape=jax.ShapeDtypeStruct((n, D_MODEL), F32),
        name="moe_out",
        compiler_params=_params("arbitrary"),
    )(yk, hn2p, wtok, sw13, sw2, x1, mod, norm_final)


def _dispatch_plan(cnt, ek, pk, rows):
    n = ek.shape[1]
    nb = n * TOP_K // rows + N_EXPERTS
    nblk = (cnt + rows - 1) // rows
    block_end = jnp.cumsum(nblk)
    experts = jnp.arange(N_EXPERTS, dtype=I32)
    first_row = jnp.sum(jnp.where(ek[:, :, None] == experts, (block_end - nblk) * rows, 0), axis=-1)
    dest3 = (first_row + pk).reshape(TOP_K, n // SC_WINDOW, SC_WINDOW).transpose(1, 0, 2)
    owns = nblk > 0
    n_ranks = jnp.sum(owns.astype(I32))
    blocks = jnp.arange(nb, dtype=I32)[:, None]
    block_rank = jnp.minimum(jnp.sum((owns & (block_end <= blocks)).astype(I32), axis=1), n_ranks - 1)
    rank_expert = jnp.minimum(jnp.sum((jnp.cumsum(owns.astype(I32)) <= experts[:, None]).astype(I32), axis=1),
                              N_EXPERTS - 1)
    return (dest3, block_rank.astype(I32), rank_expert.astype(I32), block_end[-1:].astype(I32),
            n_ranks.reshape(1).astype(I32), nb * rows)


def _grid_pos_tables(n_tokens, dim):
    quarter = dim // 4
    omega = 1.0 / (POS_BASE ** (jnp.arange(quarter, dtype=F32) / quarter))
    ra = jnp.arange(n_tokens // GRID_W).astype(F32)[:, None] * omega
    ca = jnp.arange(GRID_W).astype(F32)[:, None] * omega
    return (jnp.concatenate([jnp.sin(ra), jnp.cos(ra)], axis=-1),
            jnp.concatenate([jnp.sin(ca), jnp.cos(ca)], axis=-1))


def _hi_lo_rows(w):
    hi = w.astype(BF16)
    return jnp.concatenate([hi, (w - hi.astype(F32)).astype(BF16)], axis=0)


def _block_diag(w):
    eye = jnp.eye(N_RG_BLOCKS, dtype=w.dtype)
    return jnp.einsum('nij,nm->nimj', w, eye).reshape(D_RG, D_RG)


def _layer_weights(l, norm1, w_in, mlstm_gate_bias, mlstm_norm, rg_conv_w, rg_conv_b, rg_wa, rg_ba, rg_wx,
                   rg_bx, rg_lambda, rg_norm, w_out, norm2, router_w, router_bias, exp_w1, exp_w3, exp_w2,
                   shared_w1, shared_w3, shared_w2):
    wi = w_in[l]
    c0, c1 = 4 * D_ML, 4 * D_ML + N_GATE
    wg = wi[:, c0:c1]
    wg_hi = wg.astype(BF16)
    wg_lo = (wg - wg_hi.astype(F32)).astype(BF16)
    zcols = lambda w: jnp.zeros((D_MODEL, w), BF16)
    return dict(
        norm1=norm1[l].reshape(1, D_MODEL),
        wq=wi[:, :c0].astype(BF16),
        wr=jnp.concatenate([wi[:, c1:].astype(BF16), wg_hi, wg_lo, zcols(LANES - 2 * N_GATE)], axis=1),
        wgh=jnp.concatenate([wg_hi, zcols(LANES - N_GATE)], axis=1),
        gbias=jnp.pad(mlstm_gate_bias[l].reshape(1, N_GATE), ((0, 0), (0, LANES - N_GATE))),
        mln=mlstm_norm[l].reshape(D_ML, 1),
        cw=rg_conv_w[l], cb=rg_conv_b[l].reshape(1, D_RG),
        wbd=jnp.concatenate([_block_diag(rg_wa[l, 0]), _block_diag(rg_wx[l, 0]),
                             _block_diag(rg_wa[l, 1]), _block_diag(rg_wx[l, 1])], axis=1).astype(BF16),
        rbias=jnp.concatenate([rg_ba[l, 0], rg_bx[l, 0], rg_ba[l, 1], rg_bx[l, 1]]).reshape(1, 4 * D_RG),
        lam=rg_lambda[l], rgn=rg_norm[l].reshape(1, D_RG),
        wo_ml=w_out[l, :D_ML].astype(BF16), wo_rg=w_out[l, D_ML:].astype(BF16),
        norm2=norm2[l].reshape(1, D_MODEL),
        rwt=_hi_lo_rows(router_w[l].T), rb=router_bias[l].reshape(N_EXPERTS, 1),
        w1=_sc_pack_rows(exp_w1[l].reshape(N_EXPERTS * D_MODEL, D_EXPERT)).reshape(N_EXPERTS, D_MODEL, -1),
        w3=_sc_pack_rows(exp_w3[l].reshape(N_EXPERTS * D_MODEL, D_EXPERT)).reshape(N_EXPERTS, D_MODEL, -1),
        w2=_sc_pack_rows(exp_w2[l].reshape(N_EXPERTS * D_EXPERT, D_MODEL)).reshape(N_EXPERTS, D_EXPERT, -1),
        sw13=jnp.concatenate([shared_w1[l], shared_w3[l]], axis=-1).astype(BF16),
        sw2=shared_w2[l].astype(BF16),
    )


def _mixers(x2d, pos_tables, mod, c0, n0, m0, h0, cnt0, lw, b, t, emit_state):
    tl = _tiles(t, mod.shape[0] > 1)
    tm = tl['tok']
    rtab, ctab = pos_tables
    qt, k, kt, v, vt, ot, xr, gr, g, gt = _in_proj(x2d, rtab, ctab, mod, t, tm, lw['norm1'], lw['wq'], lw['wr'],
                                                   lw['wgh'], lw['gbias'])
    ml = _mlstm(qt, k, kt, v, vt, g, gt, c0, n0, m0, b, t, tl['chunk'], emit_state)
    rg = _rglru(xr, h0, lw['wbd'], lw['rbias'], lw['lam'], lw['cw'], lw['cb'], b, t, tl['scan'], emit_state)
    routed = _mix_out(ml[0], ml[1], ot, rg[0], rg[1], gr, x2d, rtab, ctab, mod, t, tm, lw['mln'], lw['rgn'],
                      lw['wo_ml'], lw['wo_rg'], lw['norm2'], lw['rwt'], lw['rb'], cnt0)
    return routed, ml[2:], rg[2:]


def _routed_experts(paths, lw):
    cnt = paths[-1][5][:, 0]
    hn2p = jnp.concatenate([p[1] for p in paths], axis=0)
    ek = jnp.concatenate([p[2] for p in paths], axis=1)
    pk = jnp.concatenate([p[3] for p in paths], axis=1)
    dest3, block_rank, rank_expert, n_used, n_ranks, n_slots = _dispatch_plan(cnt, ek, pk, EXPERT_ROWS)
    xs = _sc_dispatch(hn2p, dest3, n_slots)
    ys = _experts(xs, block_rank, rank_expert, n_used, n_ranks, lw['w1'], lw['w3'], lw['w2'], EXPERT_ROWS)
    return _sc_combine_gather(ys, dest3)


def kernel(x_prompt, x_sample, c, state_mlstm_C, state_mlstm_n, state_mlstm_m, state_rglru_h, c_ctx, w_ada, b_ada, norm1, w_in, mlstm_gate_bias, mlstm_norm, rg_conv_w, rg_conv_b, rg_wa, rg_ba, rg_wx, rg_bx, rg_lambda, rg_norm, w_out, norm2, router_w, router_bias, exp_w1, exp_w3, exp_w2, shared_w1, shared_w3, shared_w2, norm_final):
    bp, tp, _ = x_prompt.shape
    bs, ts, _ = x_sample.shape
    depth = w_ada.shape[0]
    assert depth == 1, "the final norm is fused into the single layer's MoE output kernel"
    nd = 2 * N_HEADS
    l = 0
    lw = _layer_weights(l, norm1, w_in, mlstm_gate_bias, mlstm_norm, rg_conv_w, rg_conv_b, rg_wa, rg_ba, rg_wx,
                        rg_bx, rg_lambda, rg_norm, w_out, norm2, router_w, router_bias, exp_w1, exp_w3, exp_w2,
                        shared_w1, shared_w3, shared_w2)
    nf = norm_final.reshape(1, D_MODEL)
    cvecs = jnp.concatenate([c_ctx[None], c, jnp.zeros((SUBLANES - 1 - bs, D_MODEL), F32)], axis=0)
    mod = _ada(cvecs, w_ada[l], b_ada[l]).reshape(SUBLANES, N_MOD, D_MODEL)

    xp2d, lw['w1'], lw['w3'], lw['w2'] = lax.optimization_barrier(
        (x_prompt.reshape(bp * tp, D_MODEL), lw['w1'], lw['w3'], lw['w2']))

    mod_p, mod_s = mod[0:1], mod[1:1 + bs]
    tm_p, tm_s = _tiles(tp, False)['tok'], _tiles(ts, True)['tok']
    rp, (cc, nc_, mc), (hc,) = _mixers(
        xp2d,
        (jnp.zeros((tm_p // GRID_W, D_MODEL // 2), F32), jnp.zeros((GRID_W, D_MODEL // 2), F32)), mod_p,
        jnp.zeros((1, nd, HEAD, HEAD), F32), jnp.zeros((1, nd, HEAD), F32), jnp.zeros((1, nd, HEAD), F32),
        jnp.zeros((1, 2, D_RG), F32), jnp.zeros((N_EXPERTS, LANES), I32), lw, bp, tp, True)
    rs, _, _ = _mixers(
        x_sample.reshape(bs * ts, D_MODEL), _grid_pos_tables(ts, D_MODEL), mod_s,
        state_mlstm_C[:, l].reshape(bs, nd, HEAD, HEAD), state_mlstm_n[:, l].reshape(bs, nd, HEAD),
        jnp.broadcast_to(state_mlstm_m[:, l].reshape(bs, nd, 1), (bs, nd, HEAD)),
        state_rglru_h[:, l], jnp.zeros((N_EXPERTS, LANES), I32), lw, bs, ts, False)
    yp = _moe_out(_routed_experts([rp], lw), 0, rp[1], rp[4], lw['sw13'], lw['sw2'], rp[0], mod_p, tp, tm_p, nf)
    ys = _moe_out(_routed_experts([rs], lw), 0, rs[1], rs[4], lw['sw13'], lw['sw2'], rs[0], mod_s, ts, tm_s, nf)

    y_prompt = yp.reshape(bp, tp, D_MODEL)
    y_sample = ys.reshape(bs, ts, D_MODEL)
    new_c = cc.reshape(bp, 1, 2, N_HEADS, HEAD, HEAD)
    new_n = nc_.reshape(bp, 1, 2, N_HEADS, HEAD)
    new_m = mc[:, :, 0].reshape(bp, 1, 2, N_HEADS)
    new_h = hc.reshape(bp, 1, 2, D_RG)
    return (y_prompt, y_sample, new_c, new_n, new_m, new_h)
```

```python
import functools

import jax
import jax.numpy as jnp
from jax import lax
from jax.experimental import pallas as pl
from jax.experimental.pallas import tpu as pltpu
from jax.experimental.pallas import tpu_sc as plsc

F32 = jnp.float32
BF16 = jnp.bfloat16
I32 = jnp.int32
HIGHEST = lax.Precision.HIGHEST

D_MODEL = 1024
N_MOD = 6
D_ML = 512
N_HEADS = 4
HEAD = 128
D_RG = 512
N_RG_BLOCKS = 8
RG_BLOCK = 64
RG_C = 8.0
N_GATE = 16
N_EXPERTS = 64
N_GROUPS = 8
GROUP = 8
TOPK_GROUPS = 4
TOP_K = 8
D_EXPERT = 256
ROUTED_SCALE = 2.5
EPS = 1e-6
GRID_W = 64
POS_BASE = 10000.0

RG_SEG_PAD = 8
SC_WINDOW = 128
SC_PACK_WORDS = 65536
D_PACK = D_MODEL // 2
EXPERT_ROWS = 512
EXPERT_BLOCKS_PER_STEP = 8
EXPERT_RING = 10

SUBLANES = 8
LANES = 128
VMEM_LIMIT = 48 * 1024 * 1024
VMEM_LIMIT_EXPERTS = 58 * 1024 * 1024


def _params(*sem, vmem=VMEM_LIMIT):
    return pltpu.CompilerParams(dimension_semantics=sem, vmem_limit_bytes=vmem)


def _tiles(t, per_sequence_mod):
    cap = t if per_sequence_mod else 1 << 30
    return dict(
        tok=min(512, cap),
        chunk=min(256, t),
        scan=min(512, t),
    )


def _silu(x):
    return x * jax.nn.sigmoid(x)


def _softplus(x):
    return jnp.maximum(x, 0.0) + jnp.log1p(jnp.exp(-jnp.abs(x)))


def _rms(x, g):
    return x * lax.rsqrt(jnp.mean(x * x, axis=-1, keepdims=True) + EPS) * g


def _bf16_pieces(x):
    hi = x.astype(BF16)
    r = x - hi.astype(F32)
    mid = r.astype(BF16)
    return hi, mid, (r - mid.astype(F32)).astype(BF16)


def _pack_bf16_pairs(x):
    w = x.shape[1] // 2
    hi = lax.bitcast_convert_type(x[:, :w].astype(BF16).astype(F32), I32)
    lo = lax.bitcast_convert_type(x[:, w:].astype(BF16).astype(F32), I32)
    return hi | lax.shift_right_logical(lo, jnp.full(lo.shape, 16, I32))


def _unpack_bf16_pairs(p):
    hi = lax.bitcast_convert_type(p & jnp.int32(-65536), F32)
    lo = lax.bitcast_convert_type(lax.shift_left(p, jnp.full(p.shape, 16, I32)), F32)
    return hi, lo


def _ada_kernel(c_ref, w_ref, b_ref, o_ref):
    s = _silu(c_ref[...])
    o_ref[...] = jnp.dot(s, w_ref[...], precision=HIGHEST, preferred_element_type=F32) + b_ref[...]


def _ada(cvecs, w_ada, b_ada):
    n_out = w_ada.shape[1]
    tn = 1536
    return pl.pallas_call(
        _ada_kernel,
        grid=(n_out // tn,),
        in_specs=[
            pl.BlockSpec((SUBLANES, D_MODEL), lambda j: (0, 0)),
            pl.BlockSpec((D_MODEL, tn), lambda j: (0, j)),
            pl.BlockSpec((1, tn), lambda j: (0, j)),
        ],
        out_specs=pl.BlockSpec((SUBLANES, tn), lambda j: (0, j)),
        out_shape=jax.ShapeDtypeStruct((SUBLANES, n_out), F32),
        name="ada",
        compiler_params=_params("arbitrary"),
    )(cvecs, w_ada, b_ada.reshape(1, n_out))


def _pos_tile(rt_ref, ct_ref):
    left = jnp.concatenate([jnp.broadcast_to(rt_ref[r:r + 1, :], (GRID_W, rt_ref.shape[1]))
                            for r in range(rt_ref.shape[0])], axis=0)
    right = jnp.concatenate([ct_ref[...]] * rt_ref.shape[0], axis=0)
    return jnp.concatenate([left, right], axis=1)


def _in_proj_kernel(x_ref, rt_ref, ct_ref, mod_ref, n1_ref, wq_ref, wr_ref, wgh_ref, gb_ref,
                    qt_ref, k_ref, kt_ref, v_ref, vt_ref, ot_ref, xr_ref, gr_ref, g_ref, gt_ref):
    x = x_ref[...] + _pos_tile(rt_ref, ct_ref)
    hn = _rms(x, n1_ref[...]) * (1.0 + mod_ref[0, 1:2, :]) + mod_ref[0, 0:1, :]
    hb = hn.astype(BF16)
    z = jnp.dot(hb, wq_ref[...], preferred_element_type=F32)
    k = z[:, D_ML:2 * D_ML] * (HEAD ** -0.5)
    v = z[:, 2 * D_ML:3 * D_ML]
    qt_ref[...] = z[:, 0:D_ML].T.astype(BF16)
    k_ref[...] = k.astype(BF16)
    kt_ref[...] = k.T.astype(BF16)
    v_ref[...] = v.astype(BF16)
    vt_ref[...] = v.T.astype(BF16)
    ot_ref[...] = z[:, 3 * D_ML:4 * D_ML].T
    zr = jnp.dot(hb, wr_ref[...], preferred_element_type=F32)
    xr_ref[...] = zr[:, 0:D_RG]
    gr_ref[...] = zr[:, D_RG:2 * D_RG]
    zg = zr[:, 2 * D_RG:2 * D_RG + LANES]
    h_lo = (hn - hb.astype(F32)).astype(BF16)
    g = (zg + pltpu.roll(zg, LANES - N_GATE, 1)
         + jnp.dot(h_lo, wgh_ref[...], preferred_element_type=F32) + gb_ref[...])
    col = lax.broadcasted_iota(I32, g.shape, 1)
    g = jnp.where((col & 4) != 0, -_softplus(-g), g)
    g_ref[...] = g[:, 0:N_GATE]
    gt_ref[...] = g.T[0:N_GATE, :]


def _mod_index(bm, tm, t):
    if bm > 1:
        return lambda i: ((i * tm) // t, 0, 0)
    return lambda i: (0, 0, 0)


def _pos_specs(rtab, ctab, tm):
    rows = tm // GRID_W
    period = rtab.shape[0] // rows
    return [pl.BlockSpec((rows, D_MODEL // 2), lambda i: (i % period, 0)),
            pl.BlockSpec((GRID_W, D_MODEL // 2), lambda i: (0, 0))]


def _in_proj(x2d, rtab, ctab, mod, t, tm, norm1, wq, wr, wgh, gbias):
    n = x2d.shape[0]
    tok = lambda i: (i, 0)
    tok_t = lambda i: (0, i)
    const = lambda i: (0, 0)
    f = lambda w: jax.ShapeDtypeStruct((n, w), F32)
    row16 = jax.ShapeDtypeStruct((n, D_ML), BF16)
    col16 = jax.ShapeDtypeStruct((D_ML, n), BF16)
    return pl.pallas_call(
        _in_proj_kernel,
        grid=(n // tm,),
        in_specs=[pl.BlockSpec((tm, D_MODEL), tok)] + _pos_specs(rtab, ctab, tm) + [
            pl.BlockSpec((1, N_MOD, D_MODEL), _mod_index(mod.shape[0], tm, t)),
            pl.BlockSpec((1, D_MODEL), const),
            pl.BlockSpec((D_MODEL, 4 * D_ML), const),
            pl.BlockSpec((D_MODEL, 2 * D_RG + LANES), const),
            pl.BlockSpec((D_MODEL, LANES), const),
            pl.BlockSpec((1, LANES), const),
        ],
        out_specs=[pl.BlockSpec((D_ML, tm), tok_t), pl.BlockSpec((tm, D_ML), tok), pl.BlockSpec((D_ML, tm), tok_t),
                   pl.BlockSpec((tm, D_ML), tok), pl.BlockSpec((D_ML, tm), tok_t), pl.BlockSpec((D_ML, tm), tok_t),
                   pl.BlockSpec((tm, D_RG), tok), pl.BlockSpec((tm, D_RG), tok),
                   pl.BlockSpec((tm, N_GATE), tok), pl.BlockSpec((N_GATE, tm), tok_t)],
        out_shape=[col16, row16, col16, row16, col16, jax.ShapeDtypeStruct((D_ML, n), F32),
                   f(D_RG), f(D_RG), f(N_GATE), jax.ShapeDtypeStruct((N_GATE, n), F32)],
        name="in_proj",
        compiler_params=_params("arbitrary"),
    )(x2d, rtab, ctab, mod, norm1, wq, wr, wgh, gbias)


def _mlstm_kernel(chunk, nc, emit_state, *refs):
    (qtf, kf, ktf, vf, vtf, gf, gtf, qtb, kb, ktb, vb, vtb, gb, gtb, c0_ref, n0_ref, m0_ref) = refs[:17]
    if emit_state:
        hf_ref, hb_ref, c_out, n_out, m_out, c_sc, n_sc, m_sc = refs[17:]
    else:
        hf_ref, hb_ref, c_sc, n_sc, m_sc = refs[17:]
    i = pl.program_id(1)

    @pl.when(i == 0)
    def _():
        c_sc[...] = c0_ref[0]
        n_sc[...] = n0_ref[0]
        m_sc[...] = m0_ref[0]

    key = lax.broadcasted_iota(I32, (chunk, chunk), 0)
    qry = lax.broadcasted_iota(I32, (chunk, chunk), 1)
    hd = []
    for d, (qt_ref, k_ref, kt_ref, v_ref, vt_ref, g_ref, gt_ref, h_ref) in enumerate(
            ((qtf, kf, ktf, vf, vtf, gf, gtf, hf_ref), (qtb, kb, ktb, vb, vtb, gb, gtb, hb_ref))):
        tri = (key <= qry) if d == 0 else (key >= qry)
        tri_t = (qry <= key) if d == 0 else (qry >= key)
        g = g_ref[...]
        gt = gt_ref[...]
        rows3 = jnp.dot(jnp.concatenate(_bf16_pieces(gt), axis=0), tri.astype(BF16), preferred_element_type=F32)
        brow = rows3[0:N_GATE] + rows3[N_GATE:2 * N_GATE] + rows3[2 * N_GATE:3 * N_GATE]
        tri_t16 = tri_t.astype(BF16)
        bcol = sum(jnp.dot(tri_t16, piece, preferred_element_type=F32) for piece in _bf16_pieces(g))
        blast = bcol[chunk - 1:chunk, :] if d == 0 else bcol[0:1, :]
        for h in range(N_HEADS):
            ci = d * 8 + h
            cf = d * 8 + 4 + h
            j = d * N_HEADS + h
            sl = slice(h * HEAD, (h + 1) * HEAD)
            hd.append(dict(
                j=j, sl=sl, tri=tri, h_ref=h_ref, qt=qt_ref[sl, :], k=k_ref[:, sl], kt=kt_ref[sl, :],
                v=v_ref[:, sl], vt=vt_ref[sl, :], b_row=brow[cf:cf + 1, :],
                gate_col=g[:, ci:ci + 1] - bcol[:, cf:cf + 1], gate_row=gt[ci:ci + 1, :] - brow[cf:cf + 1, :],
                b_last=blast[:, cf:cf + 1], m_prev=m_sc[j:j + 1, 0:1], c_prev=c_sc[j], n_prev=n_sc[j:j + 1, :]))
    for x in hd:
        top = jnp.max(jnp.where(x['tri'], x['gate_col'], -jnp.inf), axis=0, keepdims=True)
        mx = jnp.maximum(x['m_prev'], top)
        x['dm'] = jnp.exp(jnp.where(x['tri'], x['gate_col'] - mx, -jnp.inf))
        x['w_inter'] = jnp.exp(x['m_prev'] - mx)
        x['floor'] = jnp.exp(-(x['b_row'] + mx))
    for x in hd:
        x['st'] = jnp.dot(x['k'], x['qt'], preferred_element_type=F32) * x['dm']
    for x in hd:
        inter = lax.dot_general(x['c_prev'].astype(BF16), x['qt'], (((0,), (0,)), ((), ())),
                                preferred_element_type=F32)
        num = x['w_inter'] * inter + jnp.dot(x['vt'], x['st'].astype(BF16), preferred_element_type=F32)
        qn = jnp.dot(jnp.broadcast_to(x['n_prev'], (SUBLANES, HEAD)).astype(BF16), x['qt'],
                     preferred_element_type=F32)[0:1, :]
        den = x['w_inter'] * qn + jnp.sum(x['st'], axis=0, keepdims=True)
        x['h_ref'][x['sl'], :] = num / jnp.maximum(jnp.abs(den), x['floor'])
    for x in hd:
        j = x['j']
        log_w = x['b_last'] + x['gate_row']
        m_new = jnp.maximum(x['b_last'] + x['m_prev'], jnp.max(log_w, axis=1, keepdims=True))
        decay = jnp.exp(x['b_last'] + x['m_prev'] - m_new)
        w_row = jnp.exp(log_w - m_new)
        kwt = (x['kt'].astype(F32) * w_row).astype(BF16)
        c_sc[j] = decay * x['c_prev'] + jnp.dot(kwt, x['v'], preferred_element_type=F32)
        n_sc[j:j + 1, :] = decay * x['n_prev'] + jnp.dot(
            jnp.broadcast_to(w_row, (SUBLANES, chunk)).astype(BF16), x['k'], preferred_element_type=F32)[0:1, :]
        m_sc[j:j + 1, :] = jnp.broadcast_to(m_new, (1, HEAD))

    if emit_state:
        @pl.when(i == nc - 1)
        def _():
            c_out[0] = c_sc[...]
            n_out[0] = n_sc[...]
            m_out[0] = m_sc[...]


def _state_index(bm):
    if bm > 1:
        return lambda b, i: (b,) + (0,) * 3, lambda b, i: (b, 0, 0)
    return lambda b, i: (0,) * 4, lambda b, i: (0, 0, 0)


def _mlstm(qt, k, kt, v, vt, g, gt, c0, n0, m0, b, t, chunk, emit_state):
    nc = t // chunk
    n = b * t
    nd = 2 * N_HEADS
    fwd = lambda bi, i: (bi * nc + i, 0)
    bwd = lambda bi, i: (bi * nc + nc - 1 - i, 0)
    fwd_t = lambda bi, i: (0, bi * nc + i)
    bwd_t = lambda bi, i: (0, bi * nc + nc - 1 - i)
    c_idx, n_idx = _state_index(c0.shape[0])
    rows = lambda m: pl.BlockSpec((chunk, D_ML), m)
    cols = lambda m: pl.BlockSpec((D_ML, chunk), m)
    one_dir = lambda m, mt: [cols(mt), rows(m), cols(mt), rows(m), cols(mt),
                             pl.BlockSpec((chunk, N_GATE), m), pl.BlockSpec((N_GATE, chunk), mt)]
    in_specs = (one_dir(fwd, fwd_t) + one_dir(bwd, bwd_t)
                + [pl.BlockSpec((1, nd, HEAD, HEAD), c_idx),
                   pl.BlockSpec((1, nd, HEAD), n_idx), pl.BlockSpec((1, nd, HEAD), n_idx)])
    out_specs = [cols(fwd_t), cols(bwd_t)]
    out_shape = [jax.ShapeDtypeStruct((D_ML, n), F32)] * 2
    if emit_state:
        out_specs += [pl.BlockSpec((1, nd, HEAD, HEAD), lambda bi, i: (bi, 0, 0, 0)),
                      pl.BlockSpec((1, nd, HEAD), lambda bi, i: (bi, 0, 0)),
                      pl.BlockSpec((1, nd, HEAD), lambda bi, i: (bi, 0, 0))]
        out_shape += [jax.ShapeDtypeStruct((b, nd, HEAD, HEAD), F32),
                      jax.ShapeDtypeStruct((b, nd, HEAD), F32),
                      jax.ShapeDtypeStruct((b, nd, HEAD), F32)]
    return pl.pallas_call(
        functools.partial(_mlstm_kernel, chunk, nc, emit_state),
        grid=(b, nc),
        in_specs=in_specs,
        out_specs=out_specs,
        out_shape=out_shape,
        scratch_shapes=[pltpu.VMEM((nd, HEAD, HEAD), F32), pltpu.VMEM((nd, HEAD), F32),
                        pltpu.VMEM((nd, HEAD), F32)],
        name="mlstm",
        compiler_params=_params("arbitrary", "arbitrary"),
    )(qt, k, kt, v, vt, g, gt, qt, k, kt, v, vt, g, gt, c0, n0, m0)


def _neg_expm1_2x(x):
    t = jnp.tanh(x)
    return -2.0 * t / (1.0 - t)


def _rglru_kernel(tb, nb, emit_state, *refs):
    (xf, xf_prev, xf_next, xb, xb_prev, xb_next, h0_ref, w_ref, bias_ref, lam_ref, cw_ref, cb_ref) = refs[:12]
    n_out = 3 if emit_state else 2
    hf_ref, hb_ref = refs[12:14]
    hfin_ref = refs[14] if emit_state else None
    carry, af_sc, uf_sc, ab_sc, ub_sc, hf_sc, pf_sc, hb_sc, pb_sc = refs[12 + n_out:]
    i = pl.program_id(1)

    @pl.when(i == 0)
    def _():
        carry[...] = h0_ref[0]

    row8 = lax.broadcasted_iota(I32, (SUBLANES, D_RG), 0)
    cw = cw_ref[...]
    log_a_per_r = -RG_C * _softplus(-lam_ref[...])

    def taps(xm2, xm1, x0, xp1):
        return cb_ref[...] + xm2 * cw[0:1, :] + xm1 * cw[1:2, :] + x0 * cw[2:3, :] + xp1 * cw[3:4, :]

    def conv(main_ref, prev_ref, next_ref, first, last):
        main = main_ref[...]
        prev = jnp.where(first, 0.0, prev_ref[...])
        nxt = jnp.where(last, 0.0, next_ref[...])
        body = taps(pltpu.roll(main, 2, 0), pltpu.roll(main, 1, 0), main, pltpu.roll(main, tb - 1, 0))
        e = SUBLANES
        head, tail = main[0:e, :], main[tb - e:tb, :]
        before_tail = main[tb - 2 * e:tb - e, :]
        fix_head = taps(
            jnp.where(row8 == 0, prev[6:7, :], jnp.where(row8 == 1, prev[7:8, :], pltpu.roll(head, 2, 0))),
            jnp.where(row8 == 0, prev[7:8, :], pltpu.roll(head, 1, 0)), head,
            jnp.where(row8 == e - 1, main[e:e + 1, :], pltpu.roll(head, e - 1, 0)))
        fix_tail = taps(
            jnp.where(row8 == 0, before_tail[6:7, :],
                      jnp.where(row8 == 1, before_tail[7:8, :], pltpu.roll(tail, 2, 0))),
            jnp.where(row8 == 0, before_tail[7:8, :], pltpu.roll(tail, 1, 0)), tail,
            jnp.where(row8 == e - 1, nxt[0:1, :], pltpu.roll(tail, e - 1, 0)))
        return jnp.concatenate([fix_head, body[e:tb - e, :], fix_tail], axis=0)

    def recurrence_terms(xc, d):
        z = jnp.dot(xc.astype(BF16), w_ref[:, d * 2 * D_RG:(d + 1) * 2 * D_RG],
                    preferred_element_type=F32) + bias_ref[:, d * 2 * D_RG:(d + 1) * 2 * D_RG]
        r = 0.5 * jnp.tanh(0.5 * z[:, 0:D_RG]) + 0.5
        ig = 0.5 * jnp.tanh(0.5 * z[:, D_RG:2 * D_RG]) + 0.5
        log_a = r * log_a_per_r[d:d + 1, :]
        a = jnp.exp(log_a)
        u = jnp.sqrt(_neg_expm1_2x(log_a)) * (ig * xc)
        return a, u

    a_f, u_f = recurrence_terms(conv(xf, xf_prev, xf_next, i == 0, i == nb - 1), 0)
    a_b, u_b = recurrence_terms(conv(xb, xb_prev, xb_next, i == nb - 1, i == 0), 1)
    seg = tb // SUBLANES
    pitch = seg + RG_SEG_PAD
    ncol = D_RG // LANES
    for lc in range(ncol):
        lanes = slice(lc * LANES, (lc + 1) * LANES)
        for s in range(SUBLANES):
            src = slice(s * seg, (s + 1) * seg)
            dst = slice(s * pitch, s * pitch + seg)
            af_sc[lc, dst, :], uf_sc[lc, dst, :] = a_f[src, lanes], u_f[src, lanes]
            ab_sc[lc, dst, :], ub_sc[lc, dst, :] = a_b[src, lanes], u_b[src, lanes]
    slab = lambda k: (slice(None), pl.ds(k, SUBLANES, stride=pitch), slice(None))
    hf = jnp.zeros((ncol, SUBLANES, LANES), F32)
    hb = jnp.zeros((ncol, SUBLANES, LANES), F32)
    pf = jnp.ones((ncol, SUBLANES, LANES), F32)
    pb = jnp.ones((ncol, SUBLANES, LANES), F32)
    for k in range(seg):
        kb = seg - 1 - k
        ak = af_sc[slab(k)]
        hf = ak * hf + uf_sc[slab(k)]
        pf = pf * ak
        hf_sc[slab(k)] = hf
        pf_sc[slab(k)] = pf
        ak = ab_sc[slab(kb)]
        hb = ak * hb + ub_sc[slab(kb)]
        pb = pb * ak
        hb_sc[slab(kb)] = hb
        pb_sc[slab(kb)] = pb
    for lc in range(ncol):
        lanes = slice(lc * LANES, (lc + 1) * LANES)
        c = carry[0:1, lanes]
        cin_f = []
        for s in range(SUBLANES):
            cin_f.append(c)
            c = pf[lc, s:s + 1, :] * c + hf[lc, s:s + 1, :]
        carry[0:1, lanes] = c
        c = carry[1:2, lanes]
        cin_b = [None] * SUBLANES
        for s in reversed(range(SUBLANES)):
            cin_b[s] = c
            c = pb[lc, s:s + 1, :] * c + hb[lc, s:s + 1, :]
        carry[1:2, lanes] = c
        for s in range(SUBLANES):
            rows = slice(s * seg, (s + 1) * seg)
            src = slice(s * pitch, s * pitch + seg)
            hf_ref[rows, lanes] = hf_sc[lc, src, :] + pf_sc[lc, src, :] * cin_f[s]
            hb_ref[rows, lanes] = hb_sc[lc, src, :] + pb_sc[lc, src, :] * cin_b[s]

    if emit_state:
        @pl.when(i == nb - 1)
        def _():
            hfin_ref[0] = carry[...]


def _rglru(xr, h0, wbd, bias, lam, cw, cb, b, t, tb, emit_state):
    nb = t // tb
    n = b * t
    r8 = tb // SUBLANES
    last8 = n // SUBLANES - 1
    fwd = lambda bi, i: (bi * nb + i, 0)
    bwd = lambda bi, i: (bi * nb + nb - 1 - i, 0)
    fwd_prev = lambda bi, i: (jnp.maximum((bi * nb + i) * r8 - 1, 0), 0)
    fwd_next = lambda bi, i: (jnp.minimum((bi * nb + i + 1) * r8, last8), 0)
    bwd_prev = lambda bi, i: (jnp.maximum((bi * nb + nb - 1 - i) * r8 - 1, 0), 0)
    bwd_next = lambda bi, i: (jnp.minimum((bi * nb + nb - i) * r8, last8), 0)
    const = lambda bi, i: (0, 0)
    h_idx = (lambda bi, i: (bi, 0, 0)) if h0.shape[0] > 1 else (lambda bi, i: (0, 0, 0))
    halo = lambda m: pl.BlockSpec((SUBLANES, D_RG), m)
    in_specs = [pl.BlockSpec((tb, D_RG), fwd), halo(fwd_prev), halo(fwd_next),
                pl.BlockSpec((tb, D_RG), bwd), halo(bwd_prev), halo(bwd_next),
                pl.BlockSpec((1, 2, D_RG), h_idx),
                pl.BlockSpec((D_RG, 4 * D_RG), const), pl.BlockSpec((1, 4 * D_RG), const),
                pl.BlockSpec((2, D_RG), const), pl.BlockSpec((4, D_RG), const), pl.BlockSpec((1, D_RG), const)]
    out_specs = [pl.BlockSpec((tb, D_RG), fwd), pl.BlockSpec((tb, D_RG), bwd)]
    out_shape = [jax.ShapeDtypeStruct((n, D_RG), F32)] * 2
    if emit_state:
        out_specs.append(pl.BlockSpec((1, 2, D_RG), lambda bi, i: (bi, 0, 0)))
        out_shape.append(jax.ShapeDtypeStruct((b, 2, D_RG), F32))
    return pl.pallas_call(
        functools.partial(_rglru_kernel, tb, nb, emit_state),
        grid=(b, nb),
        in_specs=in_specs,
        out_specs=out_specs,
        out_shape=out_shape,
        scratch_shapes=[pltpu.VMEM((2, D_RG), F32)]
        + [pltpu.VMEM((D_RG // LANES, tb + SUBLANES * RG_SEG_PAD, LANES), F32)] * 8,
        name="rglru",
        compiler_params=_params("arbitrary", "arbitrary"),
    )(xr, xr, xr, xr, xr, xr, h0, wbd, bias, lam, cw, cb)


def _route(s, sb):
    tm = s.shape[1]
    neg = -jnp.inf
    sub = lax.broadcasted_iota(I32, (GROUP, tm), 0)
    blocks = [sb[gi * GROUP:(gi + 1) * GROUP, :] for gi in range(N_GROUPS)]
    gscore = []
    for blk in blocks:
        m1 = jnp.max(blk, axis=0, keepdims=True)
        first = jnp.min(jnp.where(blk == m1, sub, GROUP), axis=0, keepdims=True)
        m2 = jnp.max(jnp.where(sub == first, neg, blk), axis=0, keepdims=True)
        gscore.append(m1 + m2)
    masked = []
    for gi in range(N_GROUPS):
        rank = jnp.zeros((1, tm), F32)
        for gj in range(N_GROUPS):
            if gj == gi:
                continue
            ahead = (gscore[gj] >= gscore[gi]) if gj < gi else (gscore[gj] > gscore[gi])
            rank = rank + jnp.where(ahead, 1.0, 0.0)
        masked.append(jnp.where(rank < TOPK_GROUPS, blocks[gi], neg))
    v = jnp.concatenate(masked, axis=0)
    eid = lax.broadcasted_iota(I32, (N_EXPERTS, tm), 0)
    sel = jnp.zeros((N_EXPERTS, tm), F32)
    picks = []
    for _ in range(TOP_K):
        mx = jnp.max(v, axis=0, keepdims=True)
        idx = jnp.min(jnp.where(v == mx, eid, N_EXPERTS), axis=0, keepdims=True)
        pick = eid == idx
        picks.append(pick)
        sel = jnp.where(pick, 1.0, sel)
        v = jnp.where(pick, neg, v)
    ws = s * sel
    return ws / jnp.sum(ws, axis=0, keepdims=True) * ROUTED_SCALE, sel, picks


def _mix_out_kernel(hmf_ref, hmb_ref, ot_ref, hrf_ref, hrb_ref, gr_ref, x_ref, rt_ref, ct_ref, mod_ref,
                    mln_ref, rgn_ref, wo_ml_ref, wo_rg_ref, n2_ref, rwt_ref, rb_ref, cnt0_ref,
                    x1_ref, hn2p_ref, ek_ref, pk_ref, wtok_ref, cnt_ref, cnt_sc):
    i = pl.program_id(0)
    tm = x_ref.shape[0]

    @pl.when(i == 0)
    def _():
        cnt_sc[...] = cnt0_ref[...].astype(F32)

    hm = hmf_ref[...] + hmb_ref[...]
    heads = []
    for h in range(N_HEADS):
        seg = hm[h * HEAD:(h + 1) * HEAD, :]
        heads.append(seg * lax.rsqrt(jnp.mean(seg * seg, axis=0, keepdims=True) + EPS))
    y_ml_t = jnp.concatenate(heads, axis=0) * mln_ref[...] * jax.nn.sigmoid(ot_ref[...])
    y_rg = _rms(hrf_ref[...] + hrb_ref[...], rgn_ref[...]) * jax.nn.gelu(gr_ref[...])
    mix = (lax.dot_general(y_ml_t.astype(BF16), wo_ml_ref[...], (((0,), (0,)), ((), ())),
                           preferred_element_type=F32)
           + jnp.dot(y_rg.astype(BF16), wo_rg_ref[...], preferred_element_type=F32))
    x1 = x_ref[...] + _pos_tile(rt_ref, ct_ref) + mod_ref[0, 2:3, :] * mix
    x1_ref[...] = x1
    hn2 = _rms(x1, n2_ref[...]) * (1.0 + mod_ref[0, 4:5, :]) + mod_ref[0, 3:4, :]
    hn2p_ref[...] = _pack_bf16_pairs(hn2)
    hb = hn2.astype(BF16)
    h_lo = (hn2 - hb.astype(F32)).astype(BF16)
    nt = (((1,), (1,)), ((), ()))
    two = lax.dot_general(rwt_ref[...], hb, nt, preferred_element_type=F32)
    logits_t = (two[0:N_EXPERTS, :] + two[N_EXPERTS:2 * N_EXPERTS, :]
                + lax.dot_general(rwt_ref[0:N_EXPERTS, :], h_lo, nt, preferred_element_type=F32))
    s = jax.nn.sigmoid(logits_t)
    wt, sel, picks = _route(s, s + rb_ref[...])

    earlier = (lax.broadcasted_iota(I32, (tm, tm), 0) < lax.broadcasted_iota(I32, (tm, tm), 1))
    prefix = jnp.dot(sel.astype(BF16), earlier.astype(BF16), preferred_element_type=F32)
    pos_all = cnt_sc[:, 0:1] + prefix
    eid = lax.broadcasted_iota(I32, (N_EXPERTS, tm), 0)
    eid_f = eid.astype(F32)
    row8 = lax.broadcasted_iota(I32, (TOP_K, tm), 0)
    ek = jnp.zeros((TOP_K, tm), F32)
    pk = jnp.zeros((TOP_K, tm), F32)
    wk = jnp.zeros((N_EXPERTS, tm), F32)
    for k, pick in enumerate(picks):
        take = lambda a: jnp.sum(jnp.where(pick, a, 0.0), axis=0, keepdims=True)
        ek = jnp.where(row8 == k, take(eid_f), ek)
        pk = jnp.where(row8 == k, take(pos_all), pk)
        wk = jnp.where(eid == k, take(wt), wk)
    ek_ref[...] = ek.astype(I32)
    pk_ref[...] = pk.astype(I32)
    wtok_ref[...] = wk.T
    cnt_sc[...] += jnp.broadcast_to(jnp.sum(sel, axis=1, keepdims=True), cnt_sc.shape)

    @pl.when(i == pl.num_programs(0) - 1)
    def _():
        cnt_ref[...] = cnt_sc[...].astype(I32)


def _mix_out(hmf, hmb, ot, hrf, hrb, gr, x2d, rtab, ctab, mod, t, tm, mln, rgn, wo_ml, wo_rg, norm2, rwt, rbias,
             cnt0):
    n = x2d.shape[0]
    tok = lambda i: (i, 0)
    tok_t = lambda i: (0, i)
    const = lambda i: (0, 0)
    return pl.pallas_call(
        _mix_out_kernel,
        grid=(n // tm,),
        in_specs=[pl.BlockSpec((D_ML, tm), tok_t)] * 3 + [pl.BlockSpec((tm, D_RG), tok)] * 3 + [
            pl.BlockSpec((tm, D_MODEL), tok)] + _pos_specs(rtab, ctab, tm) + [
            pl.BlockSpec((1, N_MOD, D_MODEL), _mod_index(mod.shape[0], tm, t)),
            pl.BlockSpec((D_ML, 1), const), pl.BlockSpec((1, D_RG), const),
            pl.BlockSpec((D_ML, D_MODEL), const), pl.BlockSpec((D_RG, D_MODEL), const),
            pl.BlockSpec((1, D_MODEL), const),
            pl.BlockSpec((2 * N_EXPERTS, D_MODEL), const), pl.BlockSpec((N_EXPERTS, 1), const),
            pl.BlockSpec((N_EXPERTS, LANES), const),
        ],
        out_specs=[pl.BlockSpec((tm, D_MODEL), tok), pl.BlockSpec((tm, D_PACK), tok),
                   pl.BlockSpec((TOP_K, tm), tok_t), pl.BlockSpec((TOP_K, tm), tok_t),
                   pl.BlockSpec((tm, N_EXPERTS), tok), pl.BlockSpec((N_EXPERTS, LANES), const)],
        out_shape=[jax.ShapeDtypeStruct((n, D_MODEL), F32), jax.ShapeDtypeStruct((n, D_PACK), I32),
                   jax.ShapeDtypeStruct((TOP_K, n), I32), jax.ShapeDtypeStruct((TOP_K, n), I32),
                   jax.ShapeDtypeStruct((n, N_EXPERTS), F32), jax.ShapeDtypeStruct((N_EXPERTS, LANES), I32)],
        scratch_shapes=[pltpu.VMEM((N_EXPERTS, LANES), F32)],
        name="mix_out",
        compiler_params=_params("arbitrary"),
    )(hmf, hmb, ot, hrf, hrb, gr, x2d, rtab, ctab, mod, mln, rgn, wo_ml, wo_rg, norm2, rwt, rbias, cnt0)


def _sc_mesh():
    return plsc.VectorSubcoreMesh(core_axis_name="core", subcore_axis_name="subcore")


def _sc_worker():
    info = plsc.get_sparse_core_info()
    return lax.axis_index("subcore") * info.num_cores + lax.axis_index("core"), info.num_cores * info.num_subcores


def _sc_dispatch(xp, dest3, n_slots):
    n, w = xp.shape
    nwin = n // SC_WINDOW

    @pl.kernel(out_type=jax.ShapeDtypeStruct((n_slots, w), xp.dtype), mesh=_sc_mesh(),
               scratch_types=[pltpu.VMEM((SC_WINDOW, w), xp.dtype), pltpu.VMEM((TOP_K, SC_WINDOW), I32)],
               name="sc_dispatch")
    def k(x_hbm, i_hbm, o_hbm, x_v, i_v):
        wid, nworkers = _sc_worker()
        per = nwin // nworkers

        @pl.loop(0, per)
        def _(s):
            win = wid * per + s
            pltpu.sync_copy(x_hbm.at[pl.ds(win * SC_WINDOW, SC_WINDOW)], x_v)
            pltpu.sync_copy(i_hbm.at[win], i_v)
            for j in range(TOP_K):
                pltpu.sync_copy(x_v, o_hbm.at[i_v.at[j]])

    return k(xp, dest3)


def _sc_pack_rows(w):
    r, c = w.shape
    half = c // 2
    lanes = plsc.get_sparse_core_info().num_lanes
    chunk = SC_PACK_WORDS // c
    nchunks = r // chunk

    def rne_high(x):
        u = plsc.bitcast(x, I32)
        u = u + 0x7FFF + (lax.shift_right_logical(u, jnp.full(u.shape, 16, I32)) & 1)
        return u & jnp.int32(-65536)

    @pl.kernel(out_type=jax.ShapeDtypeStruct((r, half), I32), mesh=_sc_mesh(),
               scratch_types=[pltpu.VMEM((chunk, c), F32), pltpu.VMEM((chunk, half), I32)],
               compiler_params=pltpu.CompilerParams(needs_layout_passes=False), name="sc_pack")
    def k(w_hbm, o_hbm, in_v, out_v):
        wid, nworkers = _sc_worker()
        per = nchunks // nworkers

        @pl.loop(0, per)
        def _(s):
            first = (wid * per + s) * chunk
            pltpu.sync_copy(w_hbm.at[pl.ds(first, chunk)], in_v)

            @pl.loop(0, chunk)
            def _(row):
                for j in range(half // lanes):
                    at = lambda off: (row, pl.ds(off + j * lanes, lanes))
                    hi = rne_high(in_v.at[*at(0)][...])
                    lo = rne_high(in_v.at[*at(half)][...])
                    out_v.at[*at(0)][...] = hi | lax.shift_right_logical(lo, jnp.full(lo.shape, 16, I32))

            pltpu.sync_copy(out_v, o_hbm.at[pl.ds(first, chunk)])

    return k(w)


def _sc_combine_gather(ys, dest3):
    nwin = dest3.shape[0]
    w = ys.shape[1]

    @pl.kernel(out_type=jax.ShapeDtypeStruct((nwin, TOP_K, SC_WINDOW, w), ys.dtype), mesh=_sc_mesh(),
               scratch_types=[pltpu.VMEM((SC_WINDOW, w), ys.dtype), pltpu.VMEM((TOP_K, SC_WINDOW), I32)],
               name="sc_combine")
    def k(y_hbm, i_hbm, o_hbm, y_v, i_v):
        wid, nworkers = _sc_worker()
        per = nwin // nworkers

        @pl.loop(0, per)
        def _(s):
            win = wid * per + s
            pltpu.sync_copy(i_hbm.at[win], i_v)
            for j in range(TOP_K):
                pltpu.sync_copy(y_hbm.at[i_v.at[j]], y_v)
                pltpu.sync_copy(y_v, o_hbm.at[win, j])

    return k(ys, dest3)


def _swiglu(x, w13):
    h = jnp.dot(x, w13, preferred_element_type=F32)
    return _silu(h[:, 0:D_EXPERT]) * h[:, D_EXPERT:2 * D_EXPERT]


def _unpack_rows_bf16(p):
    hi, lo = _unpack_bf16_pairs(p)
    return jnp.concatenate([hi.astype(BF16), lo.astype(BF16)], axis=1)


def _expert_kernel(rows, rank_ref, expert_ref, nu_ref, nr_ref, x_ref, w1_hbm, w3_hbm, w2_hbm, y_ref,
                   w1buf, w3buf, w2buf, sem, progress):
    b = pl.program_id(0)
    g = EXPERT_BLOCKS_PER_STEP
    n_ranks = nr_ref[0]

    @pl.when(b == 0)
    def _():
        progress[0] = 0
        progress[1] = 0

    def copies(r):
        slot = lax.rem(r, EXPERT_RING)
        e = expert_ref[r]
        return (pltpu.make_async_copy(w1_hbm.at[e], w1buf.at[slot], sem.at[slot, 0]),
                pltpu.make_async_copy(w3_hbm.at[e], w3buf.at[slot], sem.at[slot, 1]),
                pltpu.make_async_copy(w2_hbm.at[e], w2buf.at[slot], sem.at[slot, 2]))

    def start(r, carry):
        for cp in copies(r):
            cp.start()
        return carry

    def wait(r, carry):
        for cp in copies(r):
            cp.wait()
        return carry

    first = rank_ref[b * g]
    last = rank_ref[b * g + g - 1]
    started = jnp.minimum(first + EXPERT_RING, n_ranks)
    lax.fori_loop(progress[0], started, start, 0)
    progress[0] = jnp.maximum(progress[0], started)
    needed = jnp.where(b == pl.num_programs(0) - 1, progress[0], jnp.minimum(last + 1, n_ranks))
    lax.fori_loop(progress[1], needed, wait, 0)
    progress[1] = jnp.maximum(progress[1], needed)

    @pl.when(b * g < nu_ref[0])
    def _():
        for j in range(g):
            slot = lax.rem(rank_ref[b * g + j], EXPERT_RING)
            sl = slice(j * rows, (j + 1) * rows)
            x = _unpack_rows_bf16(x_ref[sl, :])
            h = (_silu(jnp.dot(x, _unpack_rows_bf16(w1buf[slot]), preferred_element_type=F32))
                 * jnp.dot(x, _unpack_rows_bf16(w3buf[slot]), preferred_element_type=F32))
            y_ref[sl, :] = _pack_bf16_pairs(
                jnp.dot(h.astype(BF16), _unpack_rows_bf16(w2buf[slot]), preferred_element_type=F32))


def _experts(xs, block_rank, rank_expert, n_used, n_ranks, w1, w3, w2, rows):
    g = EXPERT_BLOCKS_PER_STEP
    nb = xs.shape[0] // rows
    tok = lambda b, *_: (b, 0)
    return pl.pallas_call(
        functools.partial(_expert_kernel, rows),
        grid_spec=pltpu.PrefetchScalarGridSpec(
            num_scalar_prefetch=4,
            grid=(nb // g,),
            in_specs=[pl.BlockSpec((g * rows, D_PACK), tok)] + [pl.BlockSpec(memory_space=pl.ANY)] * 3,
            out_specs=pl.BlockSpec((g * rows, D_PACK), tok),
            scratch_shapes=[pltpu.VMEM((EXPERT_RING, D_MODEL, D_EXPERT // 2), I32),
                            pltpu.VMEM((EXPERT_RING, D_MODEL, D_EXPERT // 2), I32),
                            pltpu.VMEM((EXPERT_RING, D_EXPERT, D_PACK), I32),
                            pltpu.SemaphoreType.DMA((EXPERT_RING, 3)), pltpu.SMEM((2,), I32)],
        ),
        out_shape=jax.ShapeDtypeStruct(xs.shape, I32),
        name="experts",
        compiler_params=_params("arbitrary", vmem=VMEM_LIMIT_EXPERTS),
    )(block_rank, rank_expert, n_used, n_ranks, xs, w1, w3, w2)


def _moe_out_kernel(yk_ref, hn2p_ref, wtok_ref, sw13_ref, sw2_ref, x1_ref, mod_ref, nf_ref, y_ref):
    shared = jnp.dot(_swiglu(_unpack_rows_bf16(hn2p_ref[...]), sw13_ref[...]).astype(BF16), sw2_ref[...],
                     preferred_element_type=F32)
    w = wtok_ref[...]
    parts = []
    for wi in range(yk_ref.shape[0]):
        rows = slice(wi * SC_WINDOW, (wi + 1) * SC_WINDOW)
        a_hi = shared[rows, 0:D_PACK]
        a_lo = shared[rows, D_PACK:D_MODEL]
        for k in range(TOP_K):
            y_hi, y_lo = _unpack_bf16_pairs(yk_ref[wi, k])
            wc = w[rows, k:k + 1]
            a_hi = a_hi + wc * y_hi
            a_lo = a_lo + wc * y_lo
        parts.append(jnp.concatenate([a_hi, a_lo], axis=1))
    x2 = x1_ref[...] + mod_ref[0, 5:6, :] * jnp.concatenate(parts, axis=0)
    y_ref[...] = _rms(x2, nf_ref[...])


def _moe_out(yk, tok0, hn2p, wtok, sw13, sw2, x1, mod, t, tm, norm_final):
    n = hn2p.shape[0]
    tok = lambda i: (i, 0)
    const = lambda i: (0, 0)
    blk0 = tok0 // tm
    return pl.pallas_call(
        _moe_out_kernel,
        grid=(n // tm,),
        in_specs=[
            pl.BlockSpec((tm // SC_WINDOW, TOP_K, SC_WINDOW, D_PACK), lambda i: (i + blk0, 0, 0, 0)),
            pl.BlockSpec((tm, D_PACK), tok),
            pl.BlockSpec((tm, N_EXPERTS), tok),
            pl.BlockSpec((D_MODEL, 2 * D_EXPERT), const),
            pl.BlockSpec((D_EXPERT, D_MODEL), const),
            pl.BlockSpec((tm, D_MODEL), tok),
            pl.BlockSpec((1, N_MOD, D_MODEL), _mod_index(mod.shape[0], tm, t)),
            pl.BlockSpec((1, D_MODEL), const),
        ],
        out_specs=pl.BlockSpec((tm, D_MODEL), tok),
        out_shape=jax.ShapeDtypeStruct((n, D_MODEL), F32),
        name="moe_out",
        compiler_params=_params("arbitrary"),
    )(yk, hn2p, wtok, sw13, sw2, x1, mod, norm_final)


def _dispatch_plan(cnt, ek, pk, rows):
    n = ek.shape[1]
    nb = n * TOP_K // rows + N_EXPERTS
    nblk = (cnt + rows - 1) // rows
    block_end = jnp.cumsum(nblk)
    experts = jnp.arange(N_EXPERTS, dtype=I32)
    first_row = jnp.sum(jnp.where(ek[:, :, None] == experts, (block_end - nblk) * rows, 0), axis=-1)
    dest3 = (first_row + pk).reshape(TOP_K, n // SC_WINDOW, SC_WINDOW).transpose(1, 0, 2)
    owns = nblk > 0
    n_ranks = jnp.sum(owns.astype(I32))
    blocks = jnp.arange(nb, dtype=I32)[:, None]
    block_rank = jnp.minimum(jnp.sum((owns & (block_end <= blocks)).astype(I32), axis=1), n_ranks - 1)
    rank_expert = jnp.minimum(jnp.sum((jnp.cumsum(owns.astype(I32)) <= experts[:, None]).astype(I32), axis=1),
                              N_EXPERTS - 1)
    return (dest3, block_rank.astype(I32), rank_expert.astype(I32), block_end[-1:].astype(I32),
            n_ranks.reshape(1).astype(I32), nb * rows)


def _grid_pos_tables(n_tokens, dim):
    quarter = dim // 4
    omega = 1.0 / (POS_BASE ** (jnp.arange(quarter, dtype=F32) / quarter))
    ra = jnp.arange(n_tokens // GRID_W).astype(F32)[:, None] * omega
    ca = jnp.arange(GRID_W).astype(F32)[:, None] * omega
    return (jnp.concatenate([jnp.sin(ra), jnp.cos(ra)], axis=-1),
            jnp.concatenate([jnp.sin(ca), jnp.cos(ca)], axis=-1))


def _hi_lo_rows(w):
    hi = w.astype(BF16)
    return jnp.concatenate([hi, (w - hi.astype(F32)).astype(BF16)], axis=0)


def _block_diag(w):
    eye = jnp.eye(N_RG_BLOCKS, dtype=w.dtype)
    return jnp.einsum('nij,nm->nimj', w, eye).reshape(D_RG, D_RG)


def _layer_weights(l, norm1, w_in, mlstm_gate_bias, mlstm_norm, rg_conv_w, rg_conv_b, rg_wa, rg_ba, rg_wx,
                   rg_bx, rg_lambda, rg_norm, w_out, norm2, router_w, router_bias, exp_w1, exp_w3, exp_w2,
                   shared_w1, shared_w3, shared_w2):
    wi = w_in[l]
    c0, c1 = 4 * D_ML, 4 * D_ML + N_GATE
    wg = wi[:, c0:c1]
    wg_hi = wg.astype(BF16)
    wg_lo = (wg - wg_hi.astype(F32)).astype(BF16)
    zcols = lambda w: jnp.zeros((D_MODEL, w), BF16)
    return dict(
        norm1=norm1[l].reshape(1, D_MODEL),
        wq=wi[:, :c0].astype(BF16),
        wr=jnp.concatenate([wi[:, c1:].astype(BF16), wg_hi, wg_lo, zcols(LANES - 2 * N_GATE)], axis=1),
        wgh=jnp.concatenate([wg_hi, zcols(LANES - N_GATE)], axis=1),
        gbias=jnp.pad(mlstm_gate_bias[l].reshape(1, N_GATE), ((0, 0), (0, LANES - N_GATE))),
        mln=mlstm_norm[l].reshape(D_ML, 1),
        cw=rg_conv_w[l], cb=rg_conv_b[l].reshape(1, D_RG),
        wbd=jnp.concatenate([_block_diag(rg_wa[l, 0]), _block_diag(rg_wx[l, 0]),
                             _block_diag(rg_wa[l, 1]), _block_diag(rg_wx[l, 1])], axis=1).astype(BF16),
        rbias=jnp.concatenate([rg_ba[l, 0], rg_bx[l, 0], rg_ba[l, 1], rg_bx[l, 1]]).reshape(1, 4 * D_RG),
        lam=rg_lambda[l], rgn=rg_norm[l].reshape(1, D_RG),
        wo_ml=w_out[l, :D_ML].astype(BF16), wo_rg=w_out[l, D_ML:].astype(BF16),
        norm2=norm2[l].reshape(1, D_MODEL),
        rwt=_hi_lo_rows(router_w[l].T), rb=router_bias[l].reshape(N_EXPERTS, 1),
        w1=_sc_pack_rows(exp_w1[l].reshape(N_EXPERTS * D_MODEL, D_EXPERT)).reshape(N_EXPERTS, D_MODEL, -1),
        w3=_sc_pack_rows(exp_w3[l].reshape(N_EXPERTS * D_MODEL, D_EXPERT)).reshape(N_EXPERTS, D_MODEL, -1),
        w2=_sc_pack_rows(exp_w2[l].reshape(N_EXPERTS * D_EXPERT, D_MODEL)).reshape(N_EXPERTS, D_EXPERT, -1),
        sw13=jnp.concatenate([shared_w1[l], shared_w3[l]], axis=-1).astype(BF16),
        sw2=shared_w2[l].astype(BF16),
    )


def _mixers(x2d, pos_tables, mod, c0, n0, m0, h0, cnt0, lw, b, t, emit_state):
    tl = _tiles(t, mod.shape[0] > 1)
    tm = tl['tok']
    rtab, ctab = pos_tables
    qt, k, kt, v, vt, ot, xr, gr, g, gt = _in_proj(x2d, rtab, ctab, mod, t, tm, lw['norm1'], lw['wq'], lw['wr'],
                                                   lw['wgh'], lw['gbias'])
    ml = _mlstm(qt, k, kt, v, vt, g, gt, c0, n0, m0, b, t, tl['chunk'], emit_state)
    rg = _rglru(xr, h0, lw['wbd'], lw['rbias'], lw['lam'], lw['cw'], lw['cb'], b, t, tl['scan'], emit_state)
    routed = _mix_out(ml[0], ml[1], ot, rg[0], rg[1], gr, x2d, rtab, ctab, mod, t, tm, lw['mln'], lw['rgn'],
                      lw['wo_ml'], lw['wo_rg'], lw['norm2'], lw['rwt'], lw['rb'], cnt0)
    return routed, ml[2:], rg[2:]


def _routed_experts(paths, lw):
    cnt = paths[-1][5][:, 0]
    hn2p = jnp.concatenate([p[1] for p in paths], axis=0)
    ek = jnp.concatenate([p[2] for p in paths], axis=1)
    pk = jnp.concatenate([p[3] for p in paths], axis=1)
    dest3, block_rank, rank_expert, n_used, n_ranks, n_slots = _dispatch_plan(cnt, ek, pk, EXPERT_ROWS)
    xs = _sc_dispatch(hn2p, dest3, n_slots)
    ys = _experts(xs, block_rank, rank_expert, n_used, n_ranks, lw['w1'], lw['w3'], lw['w2'], EXPERT_ROWS)
    return _sc_combine_gather(ys, dest3)


def kernel(x_prompt, x_sample, c, state_mlstm_C, state_mlstm_n, state_mlstm_m, state_rglru_h, c_ctx, w_ada, b_ada, norm1, w_in, mlstm_gate_bias, mlstm_norm, rg_conv_w, rg_conv_b, rg_wa, rg_ba, rg_wx, rg_bx, rg_lambda, rg_norm, w_out, norm2, router_w, router_bias, exp_w1, exp_w3, exp_w2, shared_w1, shared_w3, shared_w2, norm_final):
    bp, tp, _ = x_prompt.shape
    bs, ts, _ = x_sample.shape
    depth = w_ada.shape[0]
    assert depth == 1, "the final norm is fused into the single layer's MoE output kernel"
    nd = 2 * N_HEADS
    l = 0
    lw = _layer_weights(l, norm1, w_in, mlstm_gate_bias, mlstm_norm, rg_conv_w, rg_conv_b, rg_wa, rg_ba, rg_wx,
                        rg_bx, rg_lambda, rg_norm, w_out, norm2, router_w, router_bias, exp_w1, exp_w3, exp_w2,
                        shared_w1, shared_w3, shared_w2)
    nf = norm_final.reshape(1, D_MODEL)
    cvecs = jnp.concatenate([c_ctx[None], c, jnp.zeros((SUBLANES - 1 - bs, D_MODEL), F32)], axis=0)
    mod = _ada(cvecs, w_ada[l], b_ada[l]).reshape(SUBLANES, N_MOD, D_MODEL)

    xp2d, lw['w1'], lw['w3'], lw['w2'] = lax.optimization_barrier(
        (x_prompt.reshape(bp * tp, D_MODEL), lw['w1'], lw['w3'], lw['w2']))

    mod_p, mod_s = mod[0:1], mod[1:1 + bs]
    tm_p, tm_s = _tiles(tp, False)['tok'], _tiles(ts, True)['tok']
    rp, (cc, nc_, mc), (hc,) = _mixers(
        xp2d,
        (jnp.zeros((tm_p // GRID_W, D_MODEL // 2), F32), jnp.zeros((GRID_W, D_MODEL // 2), F32)), mod_p,
        jnp.zeros((1, nd, HEAD, HEAD), F32), jnp.zeros((1, nd, HEAD), F32), jnp.zeros((1, nd, HEAD), F32),
        jnp.zeros((1, 2, D_RG), F32), jnp.zeros((N_EXPERTS, LANES), I32), lw, bp, tp, True)
    rs, _, _ = _mixers(
        x_sample.reshape(bs * ts, D_MODEL), _grid_pos_tables(ts, D_MODEL), mod_s,
        state_mlstm_C[:, l].reshape(bs, nd, HEAD, HEAD), state_mlstm_n[:, l].reshape(bs, nd, HEAD),
        jnp.broadcast_to(state_mlstm_m[:, l].reshape(bs, nd, 1), (bs, nd, HEAD)),
        state_rglru_h[:, l], jnp.zeros((N_EXPERTS, LANES), I32), lw, bs, ts, False)
    yp = _moe_out(_routed_experts([rp], lw), 0, rp[1], rp[4], lw['sw13'], lw['sw2'], rp[0], mod_p, tp, tm_p, nf)
    ys = _moe_out(_routed_experts([rs], lw), 0, rs[1], rs[4], lw['sw13'], lw['sw2'], rs[0], mod_s, ts, tm_s, nf)

    y_prompt = yp.reshape(bp, tp, D_MODEL)
    y_sample = ys.reshape(bs, ts, D_MODEL)
    new_c = cc.reshape(bp, 1, 2, N_HEADS, HEAD, HEAD)
    new_n = nc_.reshape(bp, 1, 2, N_HEADS, HEAD)
    new_m = mc[:, :, 0].reshape(bp, 1, 2, N_HEADS)
    new_h = hc.reshape(bp, 1, 2, D_RG)
    return (y_prompt, y_sample, new_c, new_n, new_m, new_h)
```

```python
import functools

import jax
import jax.numpy as jnp
from jax import lax
from jax.experimental import pallas as pl
from jax.experimental.pallas import tpu as pltpu
from jax.experimental.pallas import tpu_sc as plsc

F32 = jnp.float32
BF16 = jnp.bfloat16
I32 = jnp.int32
HIGHEST = lax.Precision.HIGHEST

D_MODEL = 1024
N_MOD = 6
D_ML = 512
N_HEADS = 4
HEAD = 128
D_RG = 512
N_RG_BLOCKS = 8
RG_BLOCK = 64
RG_C = 8.0
N_GATE = 16
N_EXPERTS = 64
N_GROUPS = 8
GROUP = 8
TOPK_GROUPS = 4
TOP_K = 8
D_EXPERT = 256
ROUTED_SCALE = 2.5
EPS = 1e-6
GRID_W = 64
POS_BASE = 10000.0

RG_SEG_PAD = 8
SC_WINDOW = 128
SC_PACK_WORDS = 65536
D_PACK = D_MODEL // 2
EXPERT_ROWS = 512
EXPERT_BLOCKS_PER_STEP = 8
EXPERT_RING = 10

SUBLANES = 8
LANES = 128
VMEM_LIMIT = 48 * 1024 * 1024
VMEM_LIMIT_EXPERTS = 58 * 1024 * 1024


def _params(*sem, vmem=VMEM_LIMIT):
    return pltpu.CompilerParams(dimension_semantics=sem, vmem_limit_bytes=vmem)


def _tiles(t, per_sequence_mod):
    cap = t if per_sequence_mod else 1 << 30
    return dict(
        tok=min(512, cap),
        chunk=min(256, t),
        scan=min(512, t),
    )


def _silu(x):
    return x * jax.nn.sigmoid(x)


def _softplus(x):
    return jnp.maximum(x, 0.0) + jnp.log1p(jnp.exp(-jnp.abs(x)))


def _rms(x, g):
    return x * lax.rsqrt(jnp.mean(x * x, axis=-1, keepdims=True) + EPS) * g


def _bf16_pieces(x):
    hi = x.astype(BF16)
    r = x - hi.astype(F32)
    mid = r.astype(BF16)
    return hi, mid, (r - mid.astype(F32)).astype(BF16)


def _pack_bf16_pairs(x):
    w = x.shape[1] // 2
    hi = lax.bitcast_convert_type(x[:, :w].astype(BF16).astype(F32), I32)
    lo = lax.bitcast_convert_type(x[:, w:].astype(BF16).astype(F32), I32)
    return hi | lax.shift_right_logical(lo, jnp.full(lo.shape, 16, I32))


def _unpack_bf16_pairs(p):
    hi = lax.bitcast_convert_type(p & jnp.int32(-65536), F32)
    lo = lax.bitcast_convert_type(lax.shift_left(p, jnp.full(p.shape, 16, I32)), F32)
    return hi, lo


def _ada_kernel(c_ref, w_ref, b_ref, o_ref):
    s_hi, s_mid, s_lo = _bf16_pieces(_silu(c_ref[...]))
    w = w_ref[...]
    w_hi = w.astype(BF16)
    w_lo = (w - w_hi.astype(F32)).astype(BF16)
    dot = lambda a, b: jnp.dot(a, b, preferred_element_type=F32)
    o_ref[...] = dot(s_hi, w_hi) + dot(s_hi, w_lo) + dot(s_mid, w_hi) + dot(s_lo, w_hi) + b_ref[...]


def _ada(cvecs, w_ada, b_ada):
    n_out = w_ada.shape[1]
    tn = 1536
    return pl.pallas_call(
        _ada_kernel,
        grid=(n_out // tn,),
        in_specs=[
            pl.BlockSpec((SUBLANES, D_MODEL), lambda j: (0, 0)),
            pl.BlockSpec((D_MODEL, tn), lambda j: (0, j)),
            pl.BlockSpec((1, tn), lambda j: (0, j)),
        ],
        out_specs=pl.BlockSpec((SUBLANES, tn), lambda j: (0, j)),
        out_shape=jax.ShapeDtypeStruct((SUBLANES, n_out), F32),
        name="ada",
        compiler_params=_params("arbitrary"),
    )(cvecs, w_ada, b_ada.reshape(1, n_out))


def _pos_tile(rt_ref, ct_ref):
    left = jnp.concatenate([jnp.broadcast_to(rt_ref[r:r + 1, :], (GRID_W, rt_ref.shape[1]))
                            for r in range(rt_ref.shape[0])], axis=0)
    right = jnp.concatenate([ct_ref[...]] * rt_ref.shape[0], axis=0)
    return jnp.concatenate([left, right], axis=1)


def _in_proj_kernel(x_ref, rt_ref, ct_ref, mod_ref, n1_ref, wq_ref, wr_ref, wgh_ref, gb_ref,
                    qt_ref, k_ref, kt_ref, v_ref, vt_ref, ot_ref, xr_ref, gr_ref, g_ref, gt_ref):
    x = x_ref[...] + _pos_tile(rt_ref, ct_ref)
    hn = _rms(x, n1_ref[...]) * (1.0 + mod_ref[0, 1:2, :]) + mod_ref[0, 0:1, :]
    hb = hn.astype(BF16)
    z = jnp.dot(hb, wq_ref[...], preferred_element_type=F32)
    k = z[:, D_ML:2 * D_ML] * (HEAD ** -0.5)
    v = z[:, 2 * D_ML:3 * D_ML]
    qt_ref[...] = z[:, 0:D_ML].T.astype(BF16)
    k_ref[...] = k.astype(BF16)
    kt_ref[...] = k.T.astype(BF16)
    v_ref[...] = v.astype(BF16)
    vt_ref[...] = v.T.astype(BF16)
    ot_ref[...] = z[:, 3 * D_ML:4 * D_ML].T
    zr = jnp.dot(hb, wr_ref[...], preferred_element_type=F32)
    xr_ref[...] = zr[:, 0:D_RG]
    gr_ref[...] = zr[:, D_RG:2 * D_RG]
    zg = zr[:, 2 * D_RG:2 * D_RG + LANES]
    h_lo = (hn - hb.astype(F32)).astype(BF16)
    g = (zg + pltpu.roll(zg, LANES - N_GATE, 1)
         + jnp.dot(h_lo, wgh_ref[...], preferred_element_type=F32) + gb_ref[...])
    col = lax.broadcasted_iota(I32, g.shape, 1)
    g = jnp.where((col & 4) != 0, -_softplus(-g), g)
    g_ref[...] = g[:, 0:N_GATE]
    gt_ref[...] = g.T[0:N_GATE, :]


def _mod_index(bm, tm, t):
    if bm > 1:
        return lambda i: ((i * tm) // t, 0, 0)
    return lambda i: (0, 0, 0)


def _pos_specs(rtab, ctab, tm):
    rows = tm // GRID_W
    period = rtab.shape[0] // rows
    return [pl.BlockSpec((rows, D_MODEL // 2), lambda i: (i % period, 0)),
            pl.BlockSpec((GRID_W, D_MODEL // 2), lambda i: (0, 0))]


def _in_proj(x2d, rtab, ctab, mod, t, tm, norm1, wq, wr, wgh, gbias):
    n = x2d.shape[0]
    tok = lambda i: (i, 0)
    tok_t = lambda i: (0, i)
    const = lambda i: (0, 0)
    f = lambda w: jax.ShapeDtypeStruct((n, w), F32)
    row16 = jax.ShapeDtypeStruct((n, D_ML), BF16)
    col16 = jax.ShapeDtypeStruct((D_ML, n), BF16)
    return pl.pallas_call(
        _in_proj_kernel,
        grid=(n // tm,),
        in_specs=[pl.BlockSpec((tm, D_MODEL), tok)] + _pos_specs(rtab, ctab, tm) + [
            pl.BlockSpec((1, N_MOD, D_MODEL), _mod_index(mod.shape[0], tm, t)),
            pl.BlockSpec((1, D_MODEL), const),
            pl.BlockSpec((D_MODEL, 4 * D_ML), const),
            pl.BlockSpec((D_MODEL, 2 * D_RG + LANES), const),
            pl.BlockSpec((D_MODEL, LANES), const),
            pl.BlockSpec((1, LANES), const),
        ],
        out_specs=[pl.BlockSpec((D_ML, tm), tok_t), pl.BlockSpec((tm, D_ML), tok), pl.BlockSpec((D_ML, tm), tok_t),
                   pl.BlockSpec((tm, D_ML), tok), pl.BlockSpec((D_ML, tm), tok_t), pl.BlockSpec((D_ML, tm), tok_t),
                   pl.BlockSpec((tm, D_RG), tok), pl.BlockSpec((tm, D_RG), tok),
                   pl.BlockSpec((tm, N_GATE), tok), pl.BlockSpec((N_GATE, tm), tok_t)],
        out_shape=[col16, row16, col16, row16, col16, jax.ShapeDtypeStruct((D_ML, n), F32),
                   f(D_RG), f(D_RG), f(N_GATE), jax.ShapeDtypeStruct((N_GATE, n), F32)],
        name="in_proj",
        compiler_params=_params("arbitrary"),
    )(x2d, rtab, ctab, mod, norm1, wq, wr, wgh, gbias)


def _mlstm_kernel(chunk, nc, emit_state, *refs):
    (qtf, kf, ktf, vf, vtf, gf, gtf, qtb, kb, ktb, vb, vtb, gb, gtb, c0_ref, n0_ref, m0_ref) = refs[:17]
    if emit_state:
        hf_ref, hb_ref, c_out, n_out, m_out, c_sc, n_sc, m_sc = refs[17:]
    else:
        hf_ref, hb_ref, c_sc, n_sc, m_sc = refs[17:]
    i = pl.program_id(1)

    @pl.when(i == 0)
    def _():
        c_sc[...] = c0_ref[0]
        n_sc[...] = n0_ref[0]
        m_sc[...] = m0_ref[0]

    key = lax.broadcasted_iota(I32, (chunk, chunk), 0)
    qry = lax.broadcasted_iota(I32, (chunk, chunk), 1)
    hd = []
    for d, (qt_ref, k_ref, kt_ref, v_ref, vt_ref, g_ref, gt_ref, h_ref) in enumerate(
            ((qtf, kf, ktf, vf, vtf, gf, gtf, hf_ref), (qtb, kb, ktb, vb, vtb, gb, gtb, hb_ref))):
        tri = (key <= qry) if d == 0 else (key >= qry)
        tri_t = (qry <= key) if d == 0 else (qry >= key)
        g = g_ref[...]
        gt = gt_ref[...]
        rows3 = jnp.dot(jnp.concatenate(_bf16_pieces(gt), axis=0), tri.astype(BF16), preferred_element_type=F32)
        brow = rows3[0:N_GATE] + rows3[N_GATE:2 * N_GATE] + rows3[2 * N_GATE:3 * N_GATE]
        tri_t16 = tri_t.astype(BF16)
        bcol = sum(jnp.dot(tri_t16, piece, preferred_element_type=F32) for piece in _bf16_pieces(g))
        blast = bcol[chunk - 1:chunk, :] if d == 0 else bcol[0:1, :]
        for h in range(N_HEADS):
            ci = d * 8 + h
            cf = d * 8 + 4 + h
            j = d * N_HEADS + h
            sl = slice(h * HEAD, (h + 1) * HEAD)
            hd.append(dict(
                j=j, sl=sl, tri=tri, h_ref=h_ref, qt=qt_ref[sl, :], k=k_ref[:, sl], kt=kt_ref[sl, :],
                v=v_ref[:, sl], vt=vt_ref[sl, :], b_row=brow[cf:cf + 1, :],
                gate_col=g[:, ci:ci + 1] - bcol[:, cf:cf + 1], gate_row=gt[ci:ci + 1, :] - brow[cf:cf + 1, :],
                b_last=blast[:, cf:cf + 1], m_prev=m_sc[j:j + 1, 0:1], c_prev=c_sc[j], n_prev=n_sc[j:j + 1, :]))
    for x in hd:
        top = jnp.max(jnp.where(x['tri'], x['gate_col'], -jnp.inf), axis=0, keepdims=True)
        mx = jnp.maximum(x['m_prev'], top)
        x['dm'] = jnp.exp(jnp.where(x['tri'], x['gate_col'] - mx, -jnp.inf))
        x['w_inter'] = jnp.exp(x['m_prev'] - mx)
        x['floor'] = jnp.exp(-(x['b_row'] + mx))
    for x in hd:
        x['st'] = jnp.dot(x['k'], x['qt'], preferred_element_type=F32) * x['dm']
    for x in hd:
        inter = lax.dot_general(x['c_prev'].astype(BF16), x['qt'], (((0,), (0,)), ((), ())),
                                preferred_element_type=F32)
        num = x['w_inter'] * inter + jnp.dot(x['vt'], x['st'].astype(BF16), preferred_element_type=F32)
        qn = jnp.dot(jnp.broadcast_to(x['n_prev'], (SUBLANES, HEAD)).astype(BF16), x['qt'],
                     preferred_element_type=F32)[0:1, :]
        den = x['w_inter'] * qn + jnp.sum(x['st'], axis=0, keepdims=True)
        x['h_ref'][x['sl'], :] = num / jnp.maximum(jnp.abs(den), x['floor'])
    for x in hd:
        j = x['j']
        log_w = x['b_last'] + x['gate_row']
        m_new = jnp.maximum(x['b_last'] + x['m_prev'], jnp.max(log_w, axis=1, keepdims=True))
        decay = jnp.exp(x['b_last'] + x['m_prev'] - m_new)
        w_row = jnp.exp(log_w - m_new)
        kwt = (x['kt'].astype(F32) * w_row).astype(BF16)
        c_sc[j] = decay * x['c_prev'] + jnp.dot(kwt, x['v'], preferred_element_type=F32)
        n_sc[j:j + 1, :] = decay * x['n_prev'] + jnp.dot(
            jnp.broadcast_to(w_row, (SUBLANES, chunk)).astype(BF16), x['k'], preferred_element_type=F32)[0:1, :]
        m_sc[j:j + 1, :] = jnp.broadcast_to(m_new, (1, HEAD))

    if emit_state:
        @pl.when(i == nc - 1)
        def _():
            c_out[0] = c_sc[...]
            n_out[0] = n_sc[...]
            m_out[0] = m_sc[...]


def _state_index(bm):
    if bm > 1:
        return lambda b, i: (b,) + (0,) * 3, lambda b, i: (b, 0, 0)
    return lambda b, i: (0,) * 4, lambda b, i: (0, 0, 0)


def _mlstm(qt, k, kt, v, vt, g, gt, c0, n0, m0, b, t, chunk, emit_state):
    nc = t // chunk
    n = b * t
    nd = 2 * N_HEADS
    fwd = lambda bi, i: (bi * nc + i, 0)
    bwd = lambda bi, i: (bi * nc + nc - 1 - i, 0)
    fwd_t = lambda bi, i: (0, bi * nc + i)
    bwd_t = lambda bi, i: (0, bi * nc + nc - 1 - i)
    c_idx, n_idx = _state_index(c0.shape[0])
    rows = lambda m: pl.BlockSpec((chunk, D_ML), m)
    cols = lambda m: pl.BlockSpec((D_ML, chunk), m)
    one_dir = lambda m, mt: [cols(mt), rows(m), cols(mt), rows(m), cols(mt),
                             pl.BlockSpec((chunk, N_GATE), m), pl.BlockSpec((N_GATE, chunk), mt)]
    in_specs = (one_dir(fwd, fwd_t) + one_dir(bwd, bwd_t)
                + [pl.BlockSpec((1, nd, HEAD, HEAD), c_idx),
                   pl.BlockSpec((1, nd, HEAD), n_idx), pl.BlockSpec((1, nd, HEAD), n_idx)])
    out_specs = [cols(fwd_t), cols(bwd_t)]
    out_shape = [jax.ShapeDtypeStruct((D_ML, n), F32)] * 2
    if emit_state:
        out_specs += [pl.BlockSpec((1, nd, HEAD, HEAD), lambda bi, i: (bi, 0, 0, 0)),
                      pl.BlockSpec((1, nd, HEAD), lambda bi, i: (bi, 0, 0)),
                      pl.BlockSpec((1, nd, HEAD), lambda bi, i: (bi, 0, 0))]
        out_shape += [jax.ShapeDtypeStruct((b, nd, HEAD, HEAD), F32),
                      jax.ShapeDtypeStruct((b, nd, HEAD), F32),
                      jax.ShapeDtypeStruct((b, nd, HEAD), F32)]
    return pl.pallas_call(
        functools.partial(_mlstm_kernel, chunk, nc, emit_state),
        grid=(b, nc),
        in_specs=in_specs,
        out_specs=out_specs,
        out_shape=out_shape,
        scratch_shapes=[pltpu.VMEM((nd, HEAD, HEAD), F32), pltpu.VMEM((nd, HEAD), F32),
                        pltpu.VMEM((nd, HEAD), F32)],
        name="mlstm",
        compiler_params=_params("arbitrary", "arbitrary"),
    )(qt, k, kt, v, vt, g, gt, qt, k, kt, v, vt, g, gt, c0, n0, m0)


def _neg_expm1_2x(x):
    t = jnp.tanh(x)
    return -2.0 * t / (1.0 - t)


def _rglru_kernel(tb, nb, emit_state, *refs):
    (xf, xf_prev, xf_next, xb, xb_prev, xb_next, h0_ref, w_ref, bias_ref, lam_ref, cw_ref, cb_ref) = refs[:12]
    n_out = 3 if emit_state else 2
    hf_ref, hb_ref = refs[12:14]
    hfin_ref = refs[14] if emit_state else None
    carry, af_sc, uf_sc, ab_sc, ub_sc, hf_sc, pf_sc, hb_sc, pb_sc = refs[12 + n_out:]
    i = pl.program_id(1)

    @pl.when(i == 0)
    def _():
        carry[...] = h0_ref[0]

    row8 = lax.broadcasted_iota(I32, (SUBLANES, D_RG), 0)
    cw = cw_ref[...]
    log_a_per_r = -RG_C * _softplus(-lam_ref[...])

    def taps(xm2, xm1, x0, xp1):
        return cb_ref[...] + xm2 * cw[0:1, :] + xm1 * cw[1:2, :] + x0 * cw[2:3, :] + xp1 * cw[3:4, :]

    def conv(main_ref, prev_ref, next_ref, first, last):
        main = main_ref[...]
        prev = jnp.where(first, 0.0, prev_ref[...])
        nxt = jnp.where(last, 0.0, next_ref[...])
        body = taps(pltpu.roll(main, 2, 0), pltpu.roll(main, 1, 0), main, pltpu.roll(main, tb - 1, 0))
        e = SUBLANES
        head, tail = main[0:e, :], main[tb - e:tb, :]
        before_tail = main[tb - 2 * e:tb - e, :]
        fix_head = taps(
            jnp.where(row8 == 0, prev[6:7, :], jnp.where(row8 == 1, prev[7:8, :], pltpu.roll(head, 2, 0))),
            jnp.where(row8 == 0, prev[7:8, :], pltpu.roll(head, 1, 0)), head,
            jnp.where(row8 == e - 1, main[e:e + 1, :], pltpu.roll(head, e - 1, 0)))
        fix_tail = taps(
            jnp.where(row8 == 0, before_tail[6:7, :],
                      jnp.where(row8 == 1, before_tail[7:8, :], pltpu.roll(tail, 2, 0))),
            jnp.where(row8 == 0, before_tail[7:8, :], pltpu.roll(tail, 1, 0)), tail,
            jnp.where(row8 == e - 1, nxt[0:1, :], pltpu.roll(tail, e - 1, 0)))
        return jnp.concatenate([fix_head, body[e:tb - e, :], fix_tail], axis=0)

    def recurrence_terms(xc, d):
        z = jnp.dot(xc.astype(BF16), w_ref[:, d * 2 * D_RG:(d + 1) * 2 * D_RG],
                    preferred_element_type=F32) + bias_ref[:, d * 2 * D_RG:(d + 1) * 2 * D_RG]
        r = 0.5 * jnp.tanh(0.5 * z[:, 0:D_RG]) + 0.5
        ig = 0.5 * jnp.tanh(0.5 * z[:, D_RG:2 * D_RG]) + 0.5
        log_a = r * log_a_per_r[d:d + 1, :]
        a = jnp.exp(log_a)
        u = jnp.sqrt(_neg_expm1_2x(log_a)) * (ig * xc)
        return a, u

    a_f, u_f = recurrence_terms(conv(xf, xf_prev, xf_next, i == 0, i == nb - 1), 0)
    a_b, u_b = recurrence_terms(conv(xb, xb_prev, xb_next, i == nb - 1, i == 0), 1)
    seg = tb // SUBLANES
    pitch = seg + RG_SEG_PAD
    ncol = D_RG // LANES
    for lc in range(ncol):
        lanes = slice(lc * LANES, (lc + 1) * LANES)
        for s in range(SUBLANES):
            src = slice(s * seg, (s + 1) * seg)
            dst = slice(s * pitch, s * pitch + seg)
            af_sc[lc, dst, :], uf_sc[lc, dst, :] = a_f[src, lanes], u_f[src, lanes]
            ab_sc[lc, dst, :], ub_sc[lc, dst, :] = a_b[src, lanes], u_b[src, lanes]
    slab = lambda k: (slice(None), pl.ds(k, SUBLANES, stride=pitch), slice(None))
    hf = jnp.zeros((ncol, SUBLANES, LANES), F32)
    hb = jnp.zeros((ncol, SUBLANES, LANES), F32)
    pf = jnp.ones((ncol, SUBLANES, LANES), F32)
    pb = jnp.ones((ncol, SUBLANES, LANES), F32)
    for k in range(seg):
        kb = seg - 1 - k
        ak = af_sc[slab(k)]
        hf = ak * hf + uf_sc[slab(k)]
        pf = pf * ak
        hf_sc[slab(k)] = hf
        pf_sc[slab(k)] = pf
        ak = ab_sc[slab(kb)]
        hb = ak * hb + ub_sc[slab(kb)]
        pb = pb * ak
        hb_sc[slab(kb)] = hb
        pb_sc[slab(kb)] = pb
    for lc in range(ncol):
        lanes = slice(lc * LANES, (lc + 1) * LANES)
        c = carry[0:1, lanes]
        cin_f = []
        for s in range(SUBLANES):
            cin_f.append(c)
            c = pf[lc, s:s + 1, :] * c + hf[lc, s:s + 1, :]
        carry[0:1, lanes] = c
        c = carry[1:2, lanes]
        cin_b = [None] * SUBLANES
        for s in reversed(range(SUBLANES)):
            cin_b[s] = c
            c = pb[lc, s:s + 1, :] * c + hb[lc, s:s + 1, :]
        carry[1:2, lanes] = c
        for s in range(SUBLANES):
            rows = slice(s * seg, (s + 1) * seg)
            src = slice(s * pitch, s * pitch + seg)
            hf_ref[rows, lanes] = hf_sc[lc, src, :] + pf_sc[lc, src, :] * cin_f[s]
            hb_ref[rows, lanes] = hb_sc[lc, src, :] + pb_sc[lc, src, :] * cin_b[s]

    if emit_state:
        @pl.when(i == nb - 1)
        def _():
            hfin_ref[0] = carry[...]


def _rglru(xr, h0, wbd, bias, lam, cw, cb, b, t, tb, emit_state):
    nb = t // tb
    n = b * t
    r8 = tb // SUBLANES
    last8 = n // SUBLANES - 1
    fwd = lambda bi, i: (bi * nb + i, 0)
    bwd = lambda bi, i: (bi * nb + nb - 1 - i, 0)
    fwd_prev = lambda bi, i: (jnp.maximum((bi * nb + i) * r8 - 1, 0), 0)
    fwd_next = lambda bi, i: (jnp.minimum((bi * nb + i + 1) * r8, last8), 0)
    bwd_prev = lambda bi, i: (jnp.maximum((bi * nb + nb - 1 - i) * r8 - 1, 0), 0)
    bwd_next = lambda bi, i: (jnp.minimum((bi * nb + nb - i) * r8, last8), 0)
    const = lambda bi, i: (0, 0)
    h_idx = (lambda bi, i: (bi, 0, 0)) if h0.shape[0] > 1 else (lambda bi, i: (0, 0, 0))
    halo = lambda m: pl.BlockSpec((SUBLANES, D_RG), m)
    in_specs = [pl.BlockSpec((tb, D_RG), fwd), halo(fwd_prev), halo(fwd_next),
                pl.BlockSpec((tb, D_RG), bwd), halo(bwd_prev), halo(bwd_next),
                pl.BlockSpec((1, 2, D_RG), h_idx),
                pl.BlockSpec((D_RG, 4 * D_RG), const), pl.BlockSpec((1, 4 * D_RG), const),
                pl.BlockSpec((2, D_RG), const), pl.BlockSpec((4, D_RG), const), pl.BlockSpec((1, D_RG), const)]
    out_specs = [pl.BlockSpec((tb, D_RG), fwd), pl.BlockSpec((tb, D_RG), bwd)]
    out_shape = [jax.ShapeDtypeStruct((n, D_RG), F32)] * 2
    if emit_state:
        out_specs.append(pl.BlockSpec((1, 2, D_RG), lambda bi, i: (bi, 0, 0)))
        out_shape.append(jax.ShapeDtypeStruct((b, 2, D_RG), F32))
    return pl.pallas_call(
        functools.partial(_rglru_kernel, tb, nb, emit_state),
        grid=(b, nb),
        in_specs=in_specs,
        out_specs=out_specs,
        out_shape=out_shape,
        scratch_shapes=[pltpu.VMEM((2, D_RG), F32)]
        + [pltpu.VMEM((D_RG // LANES, tb + SUBLANES * RG_SEG_PAD, LANES), F32)] * 8,
        name="rglru",
        compiler_params=_params("arbitrary", "arbitrary"),
    )(xr, xr, xr, xr, xr, xr, h0, wbd, bias, lam, cw, cb)


def _route(s, sb):
    tm = s.shape[1]
    neg = -jnp.inf
    sub = lax.broadcasted_iota(I32, (GROUP, tm), 0)
    blocks = [sb[gi * GROUP:(gi + 1) * GROUP, :] for gi in range(N_GROUPS)]
    gscore = []
    for blk in blocks:
        m1 = jnp.max(blk, axis=0, keepdims=True)
        first = jnp.min(jnp.where(blk == m1, sub, GROUP), axis=0, keepdims=True)
        m2 = jnp.max(jnp.where(sub == first, neg, blk), axis=0, keepdims=True)
        gscore.append(m1 + m2)
    masked = []
    for gi in range(N_GROUPS):
        rank = jnp.zeros((1, tm), F32)
        for gj in range(N_GROUPS):
            if gj == gi:
                continue
            ahead = (gscore[gj] >= gscore[gi]) if gj < gi else (gscore[gj] > gscore[gi])
            rank = rank + jnp.where(ahead, 1.0, 0.0)
        masked.append(jnp.where(rank < TOPK_GROUPS, blocks[gi], neg))
    v = jnp.concatenate(masked, axis=0)
    eid = lax.broadcasted_iota(I32, (N_EXPERTS, tm), 0)
    sel = jnp.zeros((N_EXPERTS, tm), F32)
    picks = []
    for _ in range(TOP_K):
        mx = jnp.max(v, axis=0, keepdims=True)
        idx = jnp.min(jnp.where(v == mx, eid, N_EXPERTS), axis=0, keepdims=True)
        pick = eid == idx
        picks.append(pick)
        sel = jnp.where(pick, 1.0, sel)
        v = jnp.where(pick, neg, v)
    ws = s * sel
    return ws * (ROUTED_SCALE / jnp.sum(ws, axis=0, keepdims=True)), sel, picks


def _mix_out_kernel(hmf_ref, hmb_ref, ot_ref, hrf_ref, hrb_ref, gr_ref, x_ref, rt_ref, ct_ref, mod_ref,
                    mln_ref, rgn_ref, wo_ml_ref, wo_rg_ref, n2_ref, rwt_ref, rb_ref, cnt0_ref,
                    x1_ref, hn2p_ref, ek_ref, pk_ref, wtok_ref, cnt_ref, cnt_sc, earlier_sc):
    i = pl.program_id(0)
    tm = x_ref.shape[0]

    @pl.when(i == 0)
    def _():
        cnt_sc[...] = cnt0_ref[...].astype(F32)
        earlier = (lax.broadcasted_iota(I32, (tm, tm), 0) < lax.broadcasted_iota(I32, (tm, tm), 1))
        earlier_sc[...] = earlier.astype(BF16)

    hm = hmf_ref[...] + hmb_ref[...]
    heads = []
    for h in range(N_HEADS):
        seg = hm[h * HEAD:(h + 1) * HEAD, :]
        heads.append(seg * lax.rsqrt(jnp.mean(seg * seg, axis=0, keepdims=True) + EPS))
    y_ml_t = jnp.concatenate(heads, axis=0) * mln_ref[...] * jax.nn.sigmoid(ot_ref[...])
    y_rg = _rms(hrf_ref[...] + hrb_ref[...], rgn_ref[...]) * jax.nn.gelu(gr_ref[...])
    mix = (lax.dot_general(y_ml_t.astype(BF16), wo_ml_ref[...], (((0,), (0,)), ((), ())),
                           preferred_element_type=F32)
           + jnp.dot(y_rg.astype(BF16), wo_rg_ref[...], preferred_element_type=F32))
    x1 = x_ref[...] + _pos_tile(rt_ref, ct_ref) + mod_ref[0, 2:3, :] * mix
    x1_ref[...] = x1
    hn2 = _rms(x1, n2_ref[...]) * (1.0 + mod_ref[0, 4:5, :]) + mod_ref[0, 3:4, :]
    hn2p_ref[...] = _pack_bf16_pairs(hn2)
    hb = hn2.astype(BF16)
    h_lo = (hn2 - hb.astype(F32)).astype(BF16)
    nt = (((1,), (1,)), ((), ()))
    two = lax.dot_general(rwt_ref[...], hb, nt, preferred_element_type=F32)
    logits_t = (two[0:N_EXPERTS, :] + two[N_EXPERTS:2 * N_EXPERTS, :]
                + lax.dot_general(rwt_ref[0:N_EXPERTS, :], h_lo, nt, preferred_element_type=F32))
    s = jax.nn.sigmoid(logits_t)
    wt, sel, picks = _route(s, s + rb_ref[...])

    prefix = jnp.dot(sel.astype(BF16), earlier_sc[...], preferred_element_type=F32)
    pos_all = cnt_sc[:, 0:1] + prefix
    eid = lax.broadcasted_iota(I32, (N_EXPERTS, tm), 0)
    eid_f = eid.astype(F32)
    row8 = lax.broadcasted_iota(I32, (TOP_K, tm), 0)
    ek = jnp.zeros((TOP_K, tm), F32)
    pk = jnp.zeros((TOP_K, tm), F32)
    wk = jnp.zeros((N_EXPERTS, tm), F32)
    for k, pick in enumerate(picks):
        take = lambda a: jnp.sum(jnp.where(pick, a, 0.0), axis=0, keepdims=True)
        ek = jnp.where(row8 == k, take(eid_f), ek)
        pk = jnp.where(row8 == k, take(pos_all), pk)
        wk = jnp.where(eid == k, take(wt), wk)
    ek_ref[...] = ek.astype(I32)
    pk_ref[...] = pk.astype(I32)
    wtok_ref[...] = wk.T
    cnt_sc[...] += jnp.broadcast_to(jnp.sum(sel, axis=1, keepdims=True), cnt_sc.shape)

    @pl.when(i == pl.num_programs(0) - 1)
    def _():
        cnt_ref[...] = cnt_sc[...].astype(I32)


def _mix_out(hmf, hmb, ot, hrf, hrb, gr, x2d, rtab, ctab, mod, t, tm, mln, rgn, wo_ml, wo_rg, norm2, rwt, rbias,
             cnt0):
    n = x2d.shape[0]
    tok = lambda i: (i, 0)
    tok_t = lambda i: (0, i)
    const = lambda i: (0, 0)
    return pl.pallas_call(
        _mix_out_kernel,
        grid=(n // tm,),
        in_specs=[pl.BlockSpec((D_ML, tm), tok_t)] * 3 + [pl.BlockSpec((tm, D_RG), tok)] * 3 + [
            pl.BlockSpec((tm, D_MODEL), tok)] + _pos_specs(rtab, ctab, tm) + [
            pl.BlockSpec((1, N_MOD, D_MODEL), _mod_index(mod.shape[0], tm, t)),
            pl.BlockSpec((D_ML, 1), const), pl.BlockSpec((1, D_RG), const),
            pl.BlockSpec((D_ML, D_MODEL), const), pl.BlockSpec((D_RG, D_MODEL), const),
            pl.BlockSpec((1, D_MODEL), const),
            pl.BlockSpec((2 * N_EXPERTS, D_MODEL), const), pl.BlockSpec((N_EXPERTS, 1), const),
            pl.BlockSpec((N_EXPERTS, LANES), const),
        ],
        out_specs=[pl.BlockSpec((tm, D_MODEL), tok), pl.BlockSpec((tm, D_PACK), tok),
                   pl.BlockSpec((TOP_K, tm), tok_t), pl.BlockSpec((TOP_K, tm), tok_t),
                   pl.BlockSpec((tm, N_EXPERTS), tok), pl.BlockSpec((N_EXPERTS, LANES), const)],
        out_shape=[jax.ShapeDtypeStruct((n, D_MODEL), F32), jax.ShapeDtypeStruct((n, D_PACK), I32),
                   jax.ShapeDtypeStruct((TOP_K, n), I32), jax.ShapeDtypeStruct((TOP_K, n), I32),
                   jax.ShapeDtypeStruct((n, N_EXPERTS), F32), jax.ShapeDtypeStruct((N_EXPERTS, LANES), I32)],
        scratch_shapes=[pltpu.VMEM((N_EXPERTS, LANES), F32), pltpu.VMEM((tm, tm), BF16)],
        name="mix_out",
        compiler_params=_params("arbitrary"),
    )(hmf, hmb, ot, hrf, hrb, gr, x2d, rtab, ctab, mod, mln, rgn, wo_ml, wo_rg, norm2, rwt, rbias, cnt0)


def _sc_mesh():
    return plsc.VectorSubcoreMesh(core_axis_name="core", subcore_axis_name="subcore")


def _sc_worker():
    info = plsc.get_sparse_core_info()
    return lax.axis_index("subcore") * info.num_cores + lax.axis_index("core"), info.num_cores * info.num_subcores


def _sc_dispatch(xp, dest3, n_slots):
    n, w = xp.shape
    nwin = n // SC_WINDOW

    @pl.kernel(out_type=jax.ShapeDtypeStruct((n_slots, w), xp.dtype), mesh=_sc_mesh(),
               scratch_types=[pltpu.VMEM((SC_WINDOW, w), xp.dtype), pltpu.VMEM((TOP_K, SC_WINDOW), I32)],
               name="sc_dispatch")
    def k(x_hbm, i_hbm, o_hbm, x_v, i_v):
        wid, nworkers = _sc_worker()
        per = nwin // nworkers

        @pl.loop(0, per)
        def _(s):
            win = wid * per + s
            pltpu.sync_copy(x_hbm.at[pl.ds(win * SC_WINDOW, SC_WINDOW)], x_v)
            pltpu.sync_copy(i_hbm.at[win], i_v)
            for j in range(TOP_K):
                pltpu.sync_copy(x_v, o_hbm.at[i_v.at[j]])

    return k(xp, dest3)


def _sc_pack_rows(w):
    r, c = w.shape
    half = c // 2
    lanes = plsc.get_sparse_core_info().num_lanes
    chunk = SC_PACK_WORDS // c
    nchunks = r // chunk

    def rne_high(x):
        u = plsc.bitcast(x, I32)
        u = u + 0x7FFF + (lax.shift_right_logical(u, jnp.full(u.shape, 16, I32)) & 1)
        return u & jnp.int32(-65536)

    @pl.kernel(out_type=jax.ShapeDtypeStruct((r, half), I32), mesh=_sc_mesh(),
               scratch_types=[pltpu.VMEM((chunk, c), F32), pltpu.VMEM((chunk, half), I32)],
               compiler_params=pltpu.CompilerParams(needs_layout_passes=False), name="sc_pack")
    def k(w_hbm, o_hbm, in_v, out_v):
        wid, nworkers = _sc_worker()
        per = nchunks // nworkers

        @pl.loop(0, per)
        def _(s):
            first = (wid * per + s) * chunk
            pltpu.sync_copy(w_hbm.at[pl.ds(first, chunk)], in_v)

            @pl.loop(0, chunk)
            def _(row):
                for j in range(half // lanes):
                    at = lambda off: (row, pl.ds(off + j * lanes, lanes))
                    hi = rne_high(in_v.at[*at(0)][...])
                    lo = rne_high(in_v.at[*at(half)][...])
                    out_v.at[*at(0)][...] = hi | lax.shift_right_logical(lo, jnp.full(lo.shape, 16, I32))

            pltpu.sync_copy(out_v, o_hbm.at[pl.ds(first, chunk)])

    return k(w)


def _sc_combine_gather(ys, dest3):
    nwin = dest3.shape[0]
    w = ys.shape[1]

    @pl.kernel(out_type=jax.ShapeDtypeStruct((nwin, TOP_K, SC_WINDOW, w), ys.dtype), mesh=_sc_mesh(),
               scratch_types=[pltpu.VMEM((SC_WINDOW, w), ys.dtype), pltpu.VMEM((TOP_K, SC_WINDOW), I32)],
               name="sc_combine")
    def k(y_hbm, i_hbm, o_hbm, y_v, i_v):
        wid, nworkers = _sc_worker()
        per = nwin // nworkers

        @pl.loop(0, per)
        def _(s):
            win = wid * per + s
            pltpu.sync_copy(i_hbm.at[win], i_v)
            for j in range(TOP_K):
                pltpu.sync_copy(y_hbm.at[i_v.at[j]], y_v)
                pltpu.sync_copy(y_v, o_hbm.at[win, j])

    return k(ys, dest3)


def _swiglu(x, w13):
    h = jnp.dot(x, w13, preferred_element_type=F32)
    return _silu(h[:, 0:D_EXPERT]) * h[:, D_EXPERT:2 * D_EXPERT]


def _unpack_rows_bf16(p):
    hi, lo = _unpack_bf16_pairs(p)
    return jnp.concatenate([hi.astype(BF16), lo.astype(BF16)], axis=1)


def _expert_kernel(rows, rank_ref, expert_ref, nu_ref, nr_ref, x_ref, w1_hbm, w3_hbm, w2_hbm, y_ref,
                   w1buf, w3buf, w2buf, sem, progress):
    b = pl.program_id(0)
    g = EXPERT_BLOCKS_PER_STEP
    n_ranks = nr_ref[0]

    @pl.when(b == 0)
    def _():
        progress[0] = 0
        progress[1] = 0

    def copies(r):
        slot = lax.rem(r, EXPERT_RING)
        e = expert_ref[r]
        return (pltpu.make_async_copy(w1_hbm.at[e], w1buf.at[slot], sem.at[slot, 0]),
                pltpu.make_async_copy(w3_hbm.at[e], w3buf.at[slot], sem.at[slot, 1]),
                pltpu.make_async_copy(w2_hbm.at[e], w2buf.at[slot], sem.at[slot, 2]))

    def start(r, carry):
        for cp in copies(r):
            cp.start()
        return carry

    def wait(r, carry):
        for cp in copies(r):
            cp.wait()
        return carry

    first = rank_ref[b * g]
    last = rank_ref[b * g + g - 1]
    started = jnp.minimum(first + EXPERT_RING, n_ranks)
    lax.fori_loop(progress[0], started, start, 0)
    progress[0] = jnp.maximum(progress[0], started)
    needed = jnp.where(b == pl.num_programs(0) - 1, progress[0], jnp.minimum(last + 1, n_ranks))
    lax.fori_loop(progress[1], needed, wait, 0)
    progress[1] = jnp.maximum(progress[1], needed)

    @pl.when(b * g < nu_ref[0])
    def _():
        for j in range(g):
            slot = lax.rem(rank_ref[b * g + j], EXPERT_RING)
            sl = slice(j * rows, (j + 1) * rows)
            x = _unpack_rows_bf16(x_ref[sl, :])
            h = (_silu(jnp.dot(x, _unpack_rows_bf16(w1buf[slot]), preferred_element_type=F32))
                 * jnp.dot(x, _unpack_rows_bf16(w3buf[slot]), preferred_element_type=F32))
            y_ref[sl, :] = _pack_bf16_pairs(
                jnp.dot(h.astype(BF16), _unpack_rows_bf16(w2buf[slot]), preferred_element_type=F32))


def _experts(xs, block_rank, rank_expert, n_used, n_ranks, w1, w3, w2, rows):
    g = EXPERT_BLOCKS_PER_STEP
    nb = xs.shape[0] // rows
    tok = lambda b, *_: (b, 0)
    return pl.pallas_call(
        functools.partial(_expert_kernel, rows),
        grid_spec=pltpu.PrefetchScalarGridSpec(
            num_scalar_prefetch=4,
            grid=(nb // g,),
            in_specs=[pl.BlockSpec((g * rows, D_PACK), tok)] + [pl.BlockSpec(memory_space=pl.ANY)] * 3,
            out_specs=pl.BlockSpec((g * rows, D_PACK), tok),
            scratch_shapes=[pltpu.VMEM((EXPERT_RING, D_MODEL, D_EXPERT // 2), I32),
                            pltpu.VMEM((EXPERT_RING, D_MODEL, D_EXPERT // 2), I32),
                            pltpu.VMEM((EXPERT_RING, D_EXPERT, D_PACK), I32),
                            pltpu.SemaphoreType.DMA((EXPERT_RING, 3)), pltpu.SMEM((2,), I32)],
        ),
        out_shape=jax.ShapeDtypeStruct(xs.shape, I32),
        name="experts",
        compiler_params=_params("arbitrary", vmem=VMEM_LIMIT_EXPERTS),
    )(block_rank, rank_expert, n_used, n_ranks, xs, w1, w3, w2)


def _moe_out_kernel(yk_ref, hn2p_ref, wtok_ref, sw13_ref, sw2_ref, x1_ref, mod_ref, nf_ref, y_ref):
    shared = jnp.dot(_swiglu(_unpack_rows_bf16(hn2p_ref[...]), sw13_ref[...]).astype(BF16), sw2_ref[...],
                     preferred_element_type=F32)
    w = wtok_ref[...]
    parts = []
    for wi in range(yk_ref.shape[0]):
        rows = slice(wi * SC_WINDOW, (wi + 1) * SC_WINDOW)
        a_hi = shared[rows, 0:D_PACK]
        a_lo = shared[rows, D_PACK:D_MODEL]
        for k in range(TOP_K):
            y_hi, y_lo = _unpack_bf16_pairs(yk_ref[wi, k])
            wc = w[rows, k:k + 1]
            a_hi = a_hi + wc * y_hi
            a_lo = a_lo + wc * y_lo
        parts.append(jnp.concatenate([a_hi, a_lo], axis=1))
    x2 = x1_ref[...] + mod_ref[0, 5:6, :] * jnp.concatenate(parts, axis=0)
    y_ref[...] = _rms(x2, nf_ref[...])


def _moe_out(yk, tok0, hn2p, wtok, sw13, sw2, x1, mod, t, tm, norm_final):
    n = hn2p.shape[0]
    tok = lambda i: (i, 0)
    const = lambda i: (0, 0)
    blk0 = tok0 // tm
    return pl.pallas_call(
        _moe_out_kernel,
        grid=(n // tm,),
        in_specs=[
            pl.BlockSpec((tm // SC_WINDOW, TOP_K, SC_WINDOW, D_PACK), lambda i: (i + blk0, 0, 0, 0)),
            pl.BlockSpec((tm, D_PACK), tok),
            pl.BlockSpec((tm, N_EXPERTS), tok),
            pl.BlockSpec((D_MODEL, 2 * D_EXPERT), const),
            pl.BlockSpec((D_EXPERT, D_MODEL), const),
            pl.BlockSpec((tm, D_MODEL), tok),
            pl.BlockSpec((1, N_MOD, D_MODEL), _mod_index(mod.shape[0], tm, t)),
            pl.BlockSpec((1, D_MODEL), const),
        ],
        out_specs=pl.BlockSpec((tm, D_MODEL), tok),
        out_shape=jax.ShapeDtypeStruct((n, D_MODEL), F32),
        name="moe_out",
        compiler_params=_params("arbitrary"),
    )(yk, hn2p, wtok, sw13, sw2, x1, mod, norm_final)


def _dispatch_plan(cnt, ek, pk, rows):
    n = ek.shape[1]
    nb = n * TOP_K // rows + N_EXPERTS
    nblk = (cnt + rows - 1) // rows
    block_end = jnp.cumsum(nblk)
    experts = jnp.arange(N_EXPERTS, dtype=I32)
    first_row = jnp.sum(jnp.where(ek[:, :, None] == experts, (block_end - nblk) * rows, 0), axis=-1)
    dest3 = (first_row + pk).reshape(TOP_K, n // SC_WINDOW, SC_WINDOW).transpose(1, 0, 2)
    owns = nblk > 0
    n_ranks = jnp.sum(owns.astype(I32))
    blocks = jnp.arange(nb, dtype=I32)[:, None]
    block_rank = jnp.minimum(jnp.sum((owns & (block_end <= blocks)).astype(I32), axis=1), n_ranks - 1)
    rank_expert = jnp.minimum(jnp.sum((jnp.cumsum(owns.astype(I32)) <= experts[:, None]).astype(I32), axis=1),
                              N_EXPERTS - 1)
    return (dest3, block_rank.astype(I32), rank_expert.astype(I32), block_end[-1:].astype(I32),
            n_ranks.reshape(1).astype(I32), nb * rows)


def _grid_pos_tables(n_tokens, dim):
    quarter = dim // 4
    omega = 1.0 / (POS_BASE ** (jnp.arange(quarter, dtype=F32) / quarter))
    ra = jnp.arange(n_tokens // GRID_W).astype(F32)[:, None] * omega
    ca = jnp.arange(GRID_W).astype(F32)[:, None] * omega
    return (jnp.concatenate([jnp.sin(ra), jnp.cos(ra)], axis=-1),
            jnp.concatenate([jnp.sin(ca), jnp.cos(ca)], axis=-1))


def _hi_lo_rows(w):
    hi = w.astype(BF16)
    return jnp.concatenate([hi, (w - hi.astype(F32)).astype(BF16)], axis=0)


def _block_diag(w):
    eye = jnp.eye(N_RG_BLOCKS, dtype=w.dtype)
    return jnp.einsum('nij,nm->nimj', w, eye).reshape(D_RG, D_RG)


def _layer_weights(l, norm1, w_in, mlstm_gate_bias, mlstm_norm, rg_conv_w, rg_conv_b, rg_wa, rg_ba, rg_wx,
                   rg_bx, rg_lambda, rg_norm, w_out, norm2, router_w, router_bias, exp_w1, exp_w3, exp_w2,
                   shared_w1, shared_w3, shared_w2):
    wi = w_in[l]
    c0, c1 = 4 * D_ML, 4 * D_ML + N_GATE
    wi16 = wi.astype(BF16)
    wg = wi[:, c0:c1]
    wg_hi = wi16[:, c0:c1]
    wg_lo = (wg - wg_hi.astype(F32)).astype(BF16)
    zcols = lambda w: jnp.zeros((D_MODEL, w), BF16)
    return dict(
        norm1=norm1[l].reshape(1, D_MODEL),
        wq=wi16[:, :c0],
        wr=jnp.concatenate([wi16[:, c1:], wg_hi, wg_lo, zcols(LANES - 2 * N_GATE)], axis=1),
        wgh=jnp.concatenate([wg_hi, zcols(LANES - N_GATE)], axis=1),
        gbias=jnp.pad(mlstm_gate_bias[l].reshape(1, N_GATE), ((0, 0), (0, LANES - N_GATE))),
        mln=mlstm_norm[l].reshape(D_ML, 1),
        cw=rg_conv_w[l], cb=rg_conv_b[l].reshape(1, D_RG),
        wbd=jnp.concatenate([_block_diag(rg_wa[l, 0]), _block_diag(rg_wx[l, 0]),
                             _block_diag(rg_wa[l, 1]), _block_diag(rg_wx[l, 1])], axis=1).astype(BF16),
        rbias=jnp.concatenate([rg_ba[l, 0], rg_bx[l, 0], rg_ba[l, 1], rg_bx[l, 1]]).reshape(1, 4 * D_RG),
        lam=rg_lambda[l], rgn=rg_norm[l].reshape(1, D_RG),
        wo_ml=w_out[l, :D_ML].astype(BF16), wo_rg=w_out[l, D_ML:].astype(BF16),
        norm2=norm2[l].reshape(1, D_MODEL),
        rwt=_hi_lo_rows(router_w[l].T), rb=router_bias[l].reshape(N_EXPERTS, 1),
        w1=_sc_pack_rows(exp_w1[l].reshape(N_EXPERTS * D_MODEL, D_EXPERT)).reshape(N_EXPERTS, D_MODEL, -1),
        w3=_sc_pack_rows(exp_w3[l].reshape(N_EXPERTS * D_MODEL, D_EXPERT)).reshape(N_EXPERTS, D_MODEL, -1),
        w2=_sc_pack_rows(exp_w2[l].reshape(N_EXPERTS * D_EXPERT, D_MODEL)).reshape(N_EXPERTS, D_EXPERT, -1),
        sw13=jnp.concatenate([shared_w1[l], shared_w3[l]], axis=-1).astype(BF16),
        sw2=shared_w2[l].astype(BF16),
    )


def _mixers(x2d, pos_tables, mod, c0, n0, m0, h0, cnt0, lw, b, t, emit_state):
    tl = _tiles(t, mod.shape[0] > 1)
    tm = tl['tok']
    rtab, ctab = pos_tables
    qt, k, kt, v, vt, ot, xr, gr, g, gt = _in_proj(x2d, rtab, ctab, mod, t, tm, lw['norm1'], lw['wq'], lw['wr'],
                                                   lw['wgh'], lw['gbias'])
    ml = _mlstm(qt, k, kt, v, vt, g, gt, c0, n0, m0, b, t, tl['chunk'], emit_state)
    rg = _rglru(xr, h0, lw['wbd'], lw['rbias'], lw['lam'], lw['cw'], lw['cb'], b, t, tl['scan'], emit_state)
    routed = _mix_out(ml[0], ml[1], ot, rg[0], rg[1], gr, x2d, rtab, ctab, mod, t, tm, lw['mln'], lw['rgn'],
                      lw['wo_ml'], lw['wo_rg'], lw['norm2'], lw['rwt'], lw['rb'], cnt0)
    return routed, ml[2:], rg[2:]


def _routed_experts(paths, lw):
    cnt = paths[-1][5][:, 0]
    hn2p = jnp.concatenate([p[1] for p in paths], axis=0)
    ek = jnp.concatenate([p[2] for p in paths], axis=1)
    pk = jnp.concatenate([p[3] for p in paths], axis=1)
    dest3, block_rank, rank_expert, n_used, n_ranks, n_slots = _dispatch_plan(cnt, ek, pk, EXPERT_ROWS)
    xs = _sc_dispatch(hn2p, dest3, n_slots)
    ys = _experts(xs, block_rank, rank_expert, n_used, n_ranks, lw['w1'], lw['w3'], lw['w2'], EXPERT_ROWS)
    return _sc_combine_gather(ys, dest3)


def kernel(x_prompt, x_sample, c, state_mlstm_C, state_mlstm_n, state_mlstm_m, state_rglru_h, c_ctx, w_ada, b_ada, norm1, w_in, mlstm_gate_bias, mlstm_norm, rg_conv_w, rg_conv_b, rg_wa, rg_ba, rg_wx, rg_bx, rg_lambda, rg_norm, w_out, norm2, router_w, router_bias, exp_w1, exp_w3, exp_w2, shared_w1, shared_w3, shared_w2, norm_final):
    bp, tp, _ = x_prompt.shape
    bs, ts, _ = x_sample.shape
    depth = w_ada.shape[0]
    assert depth == 1, "the final norm is fused into the single layer's MoE output kernel"
    nd = 2 * N_HEADS
    l = 0
    lw = _layer_weights(l, norm1, w_in, mlstm_gate_bias, mlstm_norm, rg_conv_w, rg_conv_b, rg_wa, rg_ba, rg_wx,
                        rg_bx, rg_lambda, rg_norm, w_out, norm2, router_w, router_bias, exp_w1, exp_w3, exp_w2,
                        shared_w1, shared_w3, shared_w2)
    nf = norm_final.reshape(1, D_MODEL)
    cvecs = jnp.concatenate([c_ctx[None], c, jnp.zeros((SUBLANES - 1 - bs, D_MODEL), F32)], axis=0)
    mod = _ada(cvecs, w_ada[l], b_ada[l]).reshape(SUBLANES, N_MOD, D_MODEL)

    xp2d, lw['w1'], lw['w3'], lw['w2'] = lax.optimization_barrier(
        (x_prompt.reshape(bp * tp, D_MODEL), lw['w1'], lw['w3'], lw['w2']))

    mod_p, mod_s = mod[0:1], mod[1:1 + bs]
    tm_p, tm_s = _tiles(tp, False)['tok'], _tiles(ts, True)['tok']
    rp, (cc, nc_, mc), (hc,) = _mixers(
        xp2d,
        (jnp.zeros((tm_p // GRID_W, D_MODEL // 2), F32), jnp.zeros((GRID_W, D_MODEL // 2), F32)), mod_p,
        jnp.zeros((1, nd, HEAD, HEAD), F32), jnp.zeros((1, nd, HEAD), F32), jnp.zeros((1, nd, HEAD), F32),
        jnp.zeros((1, 2, D_RG), F32), jnp.zeros((N_EXPERTS, LANES), I32), lw, bp, tp, True)
    rs, _, _ = _mixers(
        x_sample.reshape(bs * ts, D_MODEL), _grid_pos_tables(ts, D_MODEL), mod_s,
        state_mlstm_C[:, l].reshape(bs, nd, HEAD, HEAD), state_mlstm_n[:, l].reshape(bs, nd, HEAD),
        jnp.broadcast_to(state_mlstm_m[:, l].reshape(bs, nd, 1), (bs, nd, HEAD)),
        state_rglru_h[:, l], jnp.zeros((N_EXPERTS, LANES), I32), lw, bs, ts, False)
    yp = _moe_out(_routed_experts([rp], lw), 0, rp[1], rp[4], lw['sw13'], lw['sw2'], rp[0], mod_p, tp, tm_p, nf)
    ys = _moe_out(_routed_experts([rs], lw), 0, rs[1], rs[4], lw['sw13'], lw['sw2'], rs[0], mod_s, ts, tm_s, nf)

    y_prompt = yp.reshape(bp, tp, D_MODEL)
    y_sample = ys.reshape(bs, ts, D_MODEL)
    new_c = cc.reshape(bp, 1, 2, N_HEADS, HEAD, HEAD)
    new_n = nc_.reshape(bp, 1, 2, N_HEADS, HEAD)
    new_m = mc[:, :, 0].reshape(bp, 1, 2, N_HEADS)
    new_h = hc.reshape(bp, 1, 2, D_RG)
    return (y_prompt, y_sample, new_c, new_n, new_m, new_h)
```

```python
import functools

import jax
import jax.numpy as jnp
from jax import lax
from jax.experimental import pallas as pl
from jax.experimental.pallas import tpu as pltpu
from jax.experimental.pallas import tpu_sc as plsc

F32 = jnp.float32
BF16 = jnp.bfloat16
I32 = jnp.int32
HIGHEST = lax.Precision.HIGHEST

D_MODEL = 1024
N_MOD = 6
D_ML = 512
N_HEADS = 4
HEAD = 128
D_RG = 512
N_RG_BLOCKS = 8
RG_BLOCK = 64
RG_C = 8.0
N_GATE = 16
N_EXPERTS = 64
N_GROUPS = 8
GROUP = 8
TOPK_GROUPS = 4
TOP_K = 8
D_EXPERT = 256
ROUTED_SCALE = 2.5
EPS = 1e-6
GRID_W = 64
POS_BASE = 10000.0

RG_SEG_PAD = 8
SC_WINDOW = 128
SC_PACK_WORDS = 65536
D_PACK = D_MODEL // 2
EXPERT_ROWS = 512
EXPERT_BLOCKS_PER_STEP = 8
EXPERT_RING = 10

SUBLANES = 8
LANES = 128
VMEM_LIMIT = 48 * 1024 * 1024
VMEM_LIMIT_EXPERTS = 58 * 1024 * 1024


def _params(*sem, vmem=VMEM_LIMIT):
    return pltpu.CompilerParams(dimension_semantics=sem, vmem_limit_bytes=vmem)


def _tiles(t, per_sequence_mod):
    cap = t if per_sequence_mod else 1 << 30
    return dict(
        tok=min(512, cap),
        chunk=min(256, t),
    )


def _silu(x):
    return x * jax.nn.sigmoid(x)


def _softplus(x):
    return jnp.maximum(x, 0.0) + jnp.log1p(jnp.exp(-jnp.abs(x)))


def _rms(x, g):
    return x * lax.rsqrt(jnp.mean(x * x, axis=-1, keepdims=True) + EPS) * g


def _bf16_pieces(x):
    hi = x.astype(BF16)
    r = x - hi.astype(F32)
    mid = r.astype(BF16)
    return hi, mid, (r - mid.astype(F32)).astype(BF16)


def _pack_bf16_pairs(x):
    w = x.shape[1] // 2
    hi = lax.bitcast_convert_type(x[:, :w].astype(BF16).astype(F32), I32)
    lo = lax.bitcast_convert_type(x[:, w:].astype(BF16).astype(F32), I32)
    return hi | lax.shift_right_logical(lo, jnp.full(lo.shape, 16, I32))


def _unpack_bf16_pairs(p):
    hi = lax.bitcast_convert_type(p & jnp.int32(-65536), F32)
    lo = lax.bitcast_convert_type(lax.shift_left(p, jnp.full(p.shape, 16, I32)), F32)
    return hi, lo


def _ada_kernel(c_ref, w_ref, b_ref, o_ref):
    s_hi, s_mid, s_lo = _bf16_pieces(_silu(c_ref[...]))
    w = w_ref[...]
    w_hi = w.astype(BF16)
    w_lo = (w - w_hi.astype(F32)).astype(BF16)
    dot = lambda a, b: jnp.dot(a, b, preferred_element_type=F32)
    o_ref[...] = dot(s_hi, w_hi) + dot(s_hi, w_lo) + dot(s_mid, w_hi) + dot(s_lo, w_hi) + b_ref[...]


def _ada(cvecs, w_ada, b_ada):
    n_out = w_ada.shape[1]
    tn = 1536
    return pl.pallas_call(
        _ada_kernel,
        grid=(n_out // tn,),
        in_specs=[
            pl.BlockSpec((SUBLANES, D_MODEL), lambda j: (0, 0)),
            pl.BlockSpec((D_MODEL, tn), lambda j: (0, j)),
            pl.BlockSpec((1, tn), lambda j: (0, j)),
        ],
        out_specs=pl.BlockSpec((SUBLANES, tn), lambda j: (0, j)),
        out_shape=jax.ShapeDtypeStruct((SUBLANES, n_out), F32),
        name="ada",
        compiler_params=_params("arbitrary"),
    )(cvecs, w_ada, b_ada.reshape(1, n_out))


def _pos_tile(rt_ref, ct_ref):
    left = jnp.concatenate([jnp.broadcast_to(rt_ref[r:r + 1, :], (GRID_W, rt_ref.shape[1]))
                            for r in range(rt_ref.shape[0])], axis=0)
    right = jnp.concatenate([ct_ref[...]] * rt_ref.shape[0], axis=0)
    return jnp.concatenate([left, right], axis=1)


def _in_proj_kernel(x_ref, rt_ref, ct_ref, mod_ref, n1_ref, wq_ref, wr_ref, wgh_ref, gb_ref,
                    qt_ref, k_ref, kt_ref, v_ref, vt_ref, ot_ref, xr_ref, gr_ref, g_ref, gt_ref):
    x = x_ref[...] + _pos_tile(rt_ref, ct_ref)
    hn = _rms(x, n1_ref[...]) * (1.0 + mod_ref[0, 1:2, :]) + mod_ref[0, 0:1, :]
    hb = hn.astype(BF16)
    z = jnp.dot(hb, wq_ref[...], preferred_element_type=F32)
    k = z[:, D_ML:2 * D_ML] * (HEAD ** -0.5)
    v = z[:, 2 * D_ML:3 * D_ML]
    qt_ref[...] = z[:, 0:D_ML].T.astype(BF16)
    k_ref[...] = k.astype(BF16)
    kt_ref[...] = k.T.astype(BF16)
    v_ref[...] = v.astype(BF16)
    vt_ref[...] = v.T.astype(BF16)
    ot_ref[...] = z[:, 3 * D_ML:4 * D_ML].T
    zr = jnp.dot(hb, wr_ref[...], preferred_element_type=F32)
    xr_ref[...] = zr[:, 0:D_RG]
    gr_ref[...] = zr[:, D_RG:2 * D_RG]
    zg = zr[:, 2 * D_RG:2 * D_RG + LANES]
    h_lo = (hn - hb.astype(F32)).astype(BF16)
    g = (zg + pltpu.roll(zg, LANES - N_GATE, 1)
         + jnp.dot(h_lo, wgh_ref[...], preferred_element_type=F32) + gb_ref[...])
    col = lax.broadcasted_iota(I32, g.shape, 1)
    g = jnp.where((col & 4) != 0, -_softplus(-g), g)
    g_ref[...] = g[:, 0:N_GATE]
    gt_ref[...] = g.T[0:N_GATE, :]


def _mod_index(bm, tm, t):
    if bm > 1:
        return lambda i: ((i * tm) // t, 0, 0)
    return lambda i: (0, 0, 0)


def _pos_specs(rtab, ctab, tm):
    rows = tm // GRID_W
    period = rtab.shape[0] // rows
    return [pl.BlockSpec((rows, D_MODEL // 2), lambda i: (i % period, 0)),
            pl.BlockSpec((GRID_W, D_MODEL // 2), lambda i: (0, 0))]


def _in_proj(x2d, rtab, ctab, mod, t, tm, norm1, wq, wr, wgh, gbias):
    n = x2d.shape[0]
    tok = lambda i: (i, 0)
    tok_t = lambda i: (0, i)
    const = lambda i: (0, 0)
    f = lambda w: jax.ShapeDtypeStruct((n, w), F32)
    row16 = jax.ShapeDtypeStruct((n, D_ML), BF16)
    col16 = jax.ShapeDtypeStruct((D_ML, n), BF16)
    return pl.pallas_call(
        _in_proj_kernel,
        grid=(n // tm,),
        in_specs=[pl.BlockSpec((tm, D_MODEL), tok)] + _pos_specs(rtab, ctab, tm) + [
            pl.BlockSpec((1, N_MOD, D_MODEL), _mod_index(mod.shape[0], tm, t)),
            pl.BlockSpec((1, D_MODEL), const),
            pl.BlockSpec((D_MODEL, 4 * D_ML), const),
            pl.BlockSpec((D_MODEL, 2 * D_RG + LANES), const),
            pl.BlockSpec((D_MODEL, LANES), const),
            pl.BlockSpec((1, LANES), const),
        ],
        out_specs=[pl.BlockSpec((D_ML, tm), tok_t), pl.BlockSpec((tm, D_ML), tok), pl.BlockSpec((D_ML, tm), tok_t),
                   pl.BlockSpec((tm, D_ML), tok), pl.BlockSpec((D_ML, tm), tok_t), pl.BlockSpec((D_ML, tm), tok_t),
                   pl.BlockSpec((tm, D_RG), tok), pl.BlockSpec((tm, D_RG), tok),
                   pl.BlockSpec((tm, N_GATE), tok), pl.BlockSpec((N_GATE, tm), tok_t)],
        out_shape=[col16, row16, col16, row16, col16, jax.ShapeDtypeStruct((D_ML, n), F32),
                   f(D_RG), f(D_RG), f(N_GATE), jax.ShapeDtypeStruct((N_GATE, n), F32)],
        name="in_proj",
        compiler_params=_params("arbitrary"),
    )(x2d, rtab, ctab, mod, norm1, wq, wr, wgh, gbias)


ALL_PARTS = ("init", "main", "final")


def _mlstm_kernel(chunk, nc, emit_state, *refs, parts=ALL_PARTS):
    (qtf, kf, ktf, vf, vtf, gf, gtf, qtb, kb, ktb, vb, vtb, gb, gtb, c0_ref, n0_ref, m0_ref) = refs[:17]
    if emit_state:
        hf_ref, hb_ref, c_out, n_out, m_out, c_sc, n_sc, m_sc = refs[17:]
    else:
        hf_ref, hb_ref, c_sc, n_sc, m_sc = refs[17:]
    i = pl.program_id(1)

    if "init" in parts:
        @pl.when(i == 0)
        def _():
            c_sc[...] = c0_ref[0]
            n_sc[...] = n0_ref[0]
            m_sc[...] = m0_ref[0]

    def write_state():
        if emit_state and "final" in parts:
            @pl.when(i == nc - 1)
            def _():
                c_out[0] = c_sc[...]
                n_out[0] = n_sc[...]
                m_out[0] = m_sc[...]

    if "main" not in parts:
        write_state()
        return

    key = lax.broadcasted_iota(I32, (chunk, chunk), 0)
    qry = lax.broadcasted_iota(I32, (chunk, chunk), 1)
    hd = []
    for d, (qt_ref, k_ref, kt_ref, v_ref, vt_ref, g_ref, gt_ref, h_ref) in enumerate(
            ((qtf, kf, ktf, vf, vtf, gf, gtf, hf_ref), (qtb, kb, ktb, vb, vtb, gb, gtb, hb_ref))):
        tri = (key <= qry) if d == 0 else (key >= qry)
        tri_t = (qry <= key) if d == 0 else (qry >= key)
        g = g_ref[...]
        gt = gt_ref[...]
        rows3 = jnp.dot(jnp.concatenate(_bf16_pieces(gt), axis=0), tri.astype(BF16), preferred_element_type=F32)
        brow = rows3[0:N_GATE] + rows3[N_GATE:2 * N_GATE] + rows3[2 * N_GATE:3 * N_GATE]
        tri_t16 = tri_t.astype(BF16)
        bcol = sum(jnp.dot(tri_t16, piece, preferred_element_type=F32) for piece in _bf16_pieces(g))
        blast = bcol[chunk - 1:chunk, :] if d == 0 else bcol[0:1, :]
        for h in range(N_HEADS):
            ci = d * 8 + h
            cf = d * 8 + 4 + h
            j = d * N_HEADS + h
            sl = slice(h * HEAD, (h + 1) * HEAD)
            hd.append(dict(
                j=j, sl=sl, tri=tri, h_ref=h_ref, qt=qt_ref[sl, :], k=k_ref[:, sl], kt=kt_ref[sl, :],
                v=v_ref[:, sl], vt=vt_ref[sl, :], b_row=brow[cf:cf + 1, :],
                gate_col=g[:, ci:ci + 1] - bcol[:, cf:cf + 1], gate_row=gt[ci:ci + 1, :] - brow[cf:cf + 1, :],
                b_last=blast[:, cf:cf + 1], m_prev=m_sc[j:j + 1, 0:1], c_prev=c_sc[j], n_prev=n_sc[j:j + 1, :]))
    for x in hd:
        top = jnp.max(jnp.where(x['tri'], x['gate_col'], -jnp.inf), axis=0, keepdims=True)
        mx = jnp.maximum(x['m_prev'], top)
        x['dm'] = jnp.exp(jnp.where(x['tri'], x['gate_col'] - mx, -jnp.inf))
        x['w_inter'] = jnp.exp(x['m_prev'] - mx)
        x['floor'] = jnp.exp(-(x['b_row'] + mx))
    for x in hd:
        x['st'] = jnp.dot(x['k'], x['qt'], preferred_element_type=F32) * x['dm']
    for x in hd:
        inter = lax.dot_general(x['c_prev'].astype(BF16), x['qt'], (((0,), (0,)), ((), ())),
                                preferred_element_type=F32)
        num = x['w_inter'] * inter + jnp.dot(x['vt'], x['st'].astype(BF16), preferred_element_type=F32)
        qn = jnp.dot(jnp.broadcast_to(x['n_prev'], (SUBLANES, HEAD)).astype(BF16), x['qt'],
                     preferred_element_type=F32)[0:1, :]
        den = x['w_inter'] * qn + jnp.sum(x['st'], axis=0, keepdims=True)
        x['h_ref'][x['sl'], :] = num / jnp.maximum(jnp.abs(den), x['floor'])
    for x in hd:
        j = x['j']
        log_w = x['b_last'] + x['gate_row']
        m_new = jnp.maximum(x['b_last'] + x['m_prev'], jnp.max(log_w, axis=1, keepdims=True))
        decay = jnp.exp(x['b_last'] + x['m_prev'] - m_new)
        w_row = jnp.exp(log_w - m_new)
        kwt = (x['kt'].astype(F32) * w_row).astype(BF16)
        c_sc[j] = decay * x['c_prev'] + jnp.dot(kwt, x['v'], preferred_element_type=F32)
        n_sc[j:j + 1, :] = decay * x['n_prev'] + jnp.dot(
            jnp.broadcast_to(w_row, (SUBLANES, chunk)).astype(BF16), x['k'], preferred_element_type=F32)[0:1, :]
        m_sc[j:j + 1, :] = jnp.broadcast_to(m_new, (1, HEAD))

    write_state()


def _state_index(bm):
    if bm > 1:
        return lambda b, i: (b,) + (0,) * 3, lambda b, i: (b, 0, 0)
    return lambda b, i: (0,) * 4, lambda b, i: (0, 0, 0)


def _mlstm_plan(qt, k, kt, v, vt, g, gt, c0, n0, m0, b, t, chunk, emit_state):
    nc = t // chunk
    n = b * t
    nd = 2 * N_HEADS
    fwd = lambda bi, i: (bi * nc + i, 0)
    bwd = lambda bi, i: (bi * nc + nc - 1 - i, 0)
    fwd_t = lambda bi, i: (0, bi * nc + i)
    bwd_t = lambda bi, i: (0, bi * nc + nc - 1 - i)
    c_idx, n_idx = _state_index(c0.shape[0])
    rows = lambda m: pl.BlockSpec((chunk, D_ML), m)
    cols = lambda m: pl.BlockSpec((D_ML, chunk), m)
    one_dir = lambda m, mt: [cols(mt), rows(m), cols(mt), rows(m), cols(mt),
                             pl.BlockSpec((chunk, N_GATE), m), pl.BlockSpec((N_GATE, chunk), mt)]
    in_specs = (one_dir(fwd, fwd_t) + one_dir(bwd, bwd_t)
                + [pl.BlockSpec((1, nd, HEAD, HEAD), c_idx),
                   pl.BlockSpec((1, nd, HEAD), n_idx), pl.BlockSpec((1, nd, HEAD), n_idx)])
    out_specs = [cols(fwd_t), cols(bwd_t)]
    out_shape = [jax.ShapeDtypeStruct((D_ML, n), F32)] * 2
    if emit_state:
        out_specs += [pl.BlockSpec((1, nd, HEAD, HEAD), lambda bi, i: (bi, 0, 0, 0)),
                      pl.BlockSpec((1, nd, HEAD), lambda bi, i: (bi, 0, 0)),
                      pl.BlockSpec((1, nd, HEAD), lambda bi, i: (bi, 0, 0))]
        out_shape += [jax.ShapeDtypeStruct((b, nd, HEAD, HEAD), F32),
                      jax.ShapeDtypeStruct((b, nd, HEAD), F32),
                      jax.ShapeDtypeStruct((b, nd, HEAD), F32)]
    return dict(
        kernel=functools.partial(_mlstm_kernel, chunk, nc, emit_state), in_specs=in_specs, out_specs=out_specs,
        out_shape=out_shape,
        scratch=[pltpu.VMEM((nd, HEAD, HEAD), F32), pltpu.VMEM((nd, HEAD), F32), pltpu.VMEM((nd, HEAD), F32)],
        args=(qt, k, kt, v, vt, g, gt, qt, k, kt, v, vt, g, gt, c0, n0, m0))


def _neg_expm1_2x(x):
    t = jnp.tanh(x)
    return -2.0 * t / (1.0 - t)


def _rglru_kernel(tb, nb, emit_state, *refs, parts=ALL_PARTS):
    (xf, xf_prev, xf_next, xb, xb_prev, xb_next, h0_ref, w_ref, bias_ref, lam_ref, cw_ref, cb_ref) = refs[:12]
    n_out = 3 if emit_state else 2
    hf_ref, hb_ref = refs[12:14]
    hfin_ref = refs[14] if emit_state else None
    carry, af_sc, uf_sc, ab_sc, ub_sc, hf_sc, pf_sc, hb_sc, pb_sc = refs[12 + n_out:]
    i = pl.program_id(1)

    if "init" in parts:
        @pl.when(i == 0)
        def _():
            carry[...] = h0_ref[0]

    def write_state():
        if emit_state and "final" in parts:
            @pl.when(i == nb - 1)
            def _():
                hfin_ref[0] = carry[...]

    if "main" not in parts:
        write_state()
        return

    row8 = lax.broadcasted_iota(I32, (SUBLANES, D_RG), 0)
    cw = cw_ref[...]
    log_a_per_r = -RG_C * _softplus(-lam_ref[...])

    def taps(xm2, xm1, x0, xp1):
        return cb_ref[...] + xm2 * cw[0:1, :] + xm1 * cw[1:2, :] + x0 * cw[2:3, :] + xp1 * cw[3:4, :]

    def conv(main_ref, prev_ref, next_ref, first, last):
        main = main_ref[...]
        prev = jnp.where(first, 0.0, prev_ref[...])
        nxt = jnp.where(last, 0.0, next_ref[...])
        body = taps(pltpu.roll(main, 2, 0), pltpu.roll(main, 1, 0), main, pltpu.roll(main, tb - 1, 0))
        e = SUBLANES
        head, tail = main[0:e, :], main[tb - e:tb, :]
        before_tail = main[tb - 2 * e:tb - e, :]
        fix_head = taps(
            jnp.where(row8 == 0, prev[6:7, :], jnp.where(row8 == 1, prev[7:8, :], pltpu.roll(head, 2, 0))),
            jnp.where(row8 == 0, prev[7:8, :], pltpu.roll(head, 1, 0)), head,
            jnp.where(row8 == e - 1, main[e:e + 1, :], pltpu.roll(head, e - 1, 0)))
        fix_tail = taps(
            jnp.where(row8 == 0, before_tail[6:7, :],
                      jnp.where(row8 == 1, before_tail[7:8, :], pltpu.roll(tail, 2, 0))),
            jnp.where(row8 == 0, before_tail[7:8, :], pltpu.roll(tail, 1, 0)), tail,
            jnp.where(row8 == e - 1, nxt[0:1, :], pltpu.roll(tail, e - 1, 0)))
        return jnp.concatenate([fix_head, body[e:tb - e, :], fix_tail], axis=0)

    def recurrence_terms(xc, d):
        z = jnp.dot(xc.astype(BF16), w_ref[:, d * 2 * D_RG:(d + 1) * 2 * D_RG],
                    preferred_element_type=F32) + bias_ref[:, d * 2 * D_RG:(d + 1) * 2 * D_RG]
        r = 0.5 * jnp.tanh(0.5 * z[:, 0:D_RG]) + 0.5
        ig = 0.5 * jnp.tanh(0.5 * z[:, D_RG:2 * D_RG]) + 0.5
        log_a = r * log_a_per_r[d:d + 1, :]
        a = jnp.exp(log_a)
        u = jnp.sqrt(_neg_expm1_2x(log_a)) * (ig * xc)
        return a, u

    a_f, u_f = recurrence_terms(conv(xf, xf_prev, xf_next, i == 0, i == nb - 1), 0)
    a_b, u_b = recurrence_terms(conv(xb, xb_prev, xb_next, i == nb - 1, i == 0), 1)
    seg = tb // SUBLANES
    pitch = seg + RG_SEG_PAD
    ncol = D_RG // LANES
    for lc in range(ncol):
        lanes = slice(lc * LANES, (lc + 1) * LANES)
        for s in range(SUBLANES):
            src = slice(s * seg, (s + 1) * seg)
            dst = slice(s * pitch, s * pitch + seg)
            af_sc[lc, dst, :], uf_sc[lc, dst, :] = a_f[src, lanes], u_f[src, lanes]
            ab_sc[lc, dst, :], ub_sc[lc, dst, :] = a_b[src, lanes], u_b[src, lanes]
    slab = lambda k: (slice(None), pl.ds(k, SUBLANES, stride=pitch), slice(None))
    hf = jnp.zeros((ncol, SUBLANES, LANES), F32)
    hb = jnp.zeros((ncol, SUBLANES, LANES), F32)
    pf = jnp.ones((ncol, SUBLANES, LANES), F32)
    pb = jnp.ones((ncol, SUBLANES, LANES), F32)
    for k in range(seg):
        kb = seg - 1 - k
        ak = af_sc[slab(k)]
        hf = ak * hf + uf_sc[slab(k)]
        pf = pf * ak
        hf_sc[slab(k)] = hf
        pf_sc[slab(k)] = pf
        ak = ab_sc[slab(kb)]
        hb = ak * hb + ub_sc[slab(kb)]
        pb = pb * ak
        hb_sc[slab(kb)] = hb
        pb_sc[slab(kb)] = pb
    for lc in range(ncol):
        lanes = slice(lc * LANES, (lc + 1) * LANES)
        c = carry[0:1, lanes]
        cin_f = []
        for s in range(SUBLANES):
            cin_f.append(c)
            c = pf[lc, s:s + 1, :] * c + hf[lc, s:s + 1, :]
        carry[0:1, lanes] = c
        c = carry[1:2, lanes]
        cin_b = [None] * SUBLANES
        for s in reversed(range(SUBLANES)):
            cin_b[s] = c
            c = pb[lc, s:s + 1, :] * c + hb[lc, s:s + 1, :]
        carry[1:2, lanes] = c
        for s in range(SUBLANES):
            rows = slice(s * seg, (s + 1) * seg)
            src = slice(s * pitch, s * pitch + seg)
            hf_ref[rows, lanes] = hf_sc[lc, src, :] + pf_sc[lc, src, :] * cin_f[s]
            hb_ref[rows, lanes] = hb_sc[lc, src, :] + pb_sc[lc, src, :] * cin_b[s]

    write_state()


def _rglru_plan(xr, h0, wbd, bias, lam, cw, cb, b, t, tb, emit_state):
    nb = t // tb
    n = b * t
    r8 = tb // SUBLANES
    last8 = n // SUBLANES - 1
    fwd = lambda bi, i: (bi * nb + i, 0)
    bwd = lambda bi, i: (bi * nb + nb - 1 - i, 0)
    fwd_prev = lambda bi, i: (jnp.maximum((bi * nb + i) * r8 - 1, 0), 0)
    fwd_next = lambda bi, i: (jnp.minimum((bi * nb + i + 1) * r8, last8), 0)
    bwd_prev = lambda bi, i: (jnp.maximum((bi * nb + nb - 1 - i) * r8 - 1, 0), 0)
    bwd_next = lambda bi, i: (jnp.minimum((bi * nb + nb - i) * r8, last8), 0)
    const = lambda bi, i: (0, 0)
    h_idx = (lambda bi, i: (bi, 0, 0)) if h0.shape[0] > 1 else (lambda bi, i: (0, 0, 0))
    halo = lambda m: pl.BlockSpec((SUBLANES, D_RG), m)
    in_specs = [pl.BlockSpec((tb, D_RG), fwd), halo(fwd_prev), halo(fwd_next),
                pl.BlockSpec((tb, D_RG), bwd), halo(bwd_prev), halo(bwd_next),
                pl.BlockSpec((1, 2, D_RG), h_idx),
                pl.BlockSpec((D_RG, 4 * D_RG), const), pl.BlockSpec((1, 4 * D_RG), const),
                pl.BlockSpec((2, D_RG), const), pl.BlockSpec((4, D_RG), const), pl.BlockSpec((1, D_RG), const)]
    out_specs = [pl.BlockSpec((tb, D_RG), fwd), pl.BlockSpec((tb, D_RG), bwd)]
    out_shape = [jax.ShapeDtypeStruct((n, D_RG), F32)] * 2
    if emit_state:
        out_specs.append(pl.BlockSpec((1, 2, D_RG), lambda bi, i: (bi, 0, 0)))
        out_shape.append(jax.ShapeDtypeStruct((b, 2, D_RG), F32))
    return dict(
        kernel=functools.partial(_rglru_kernel, tb, nb, emit_state), in_specs=in_specs, out_specs=out_specs,
        out_shape=out_shape,
        scratch=[pltpu.VMEM((2, D_RG), F32)]
        + [pltpu.VMEM((D_RG // LANES, tb + SUBLANES * RG_SEG_PAD, LANES), F32)] * 8,
        args=(xr, xr, xr, xr, xr, xr, h0, wbd, bias, lam, cw, cb))


def _scans_kernel(ml, rg, *refs):
    def split(plan, ins, outs, scr):
        n_in, n_out, n_sc = len(plan['in_specs']), len(plan['out_specs']), len(plan['scratch'])
        return ins[:n_in], ins[n_in:], outs[:n_out], outs[n_out:], scr[:n_sc], scr[n_sc:]

    n_in = len(ml['in_specs']) + len(rg['in_specs'])
    n_out = len(ml['out_specs']) + len(rg['out_specs'])
    ml_in, ins, ml_out, outs, ml_sc, scr = split(ml, refs[:n_in], refs[n_in:n_in + n_out], refs[n_in + n_out:])
    rg_in, _, rg_out, _, rg_sc, _ = split(rg, ins, outs, scr)
    for parts in (("init",), ("main",), ("final",)):
        ml['kernel'](*ml_in, *ml_out, *ml_sc, parts=parts)
        rg['kernel'](*rg_in, *rg_out, *rg_sc, parts=parts)


def _scans(ml, rg, b, nsteps):
    static = lambda plan: {k: v for k, v in plan.items() if k != 'args'}
    outs = pl.pallas_call(
        functools.partial(_scans_kernel, static(ml), static(rg)),
        grid=(b, nsteps),
        in_specs=ml['in_specs'] + rg['in_specs'],
        out_specs=ml['out_specs'] + rg['out_specs'],
        out_shape=ml['out_shape'] + rg['out_shape'],
        scratch_shapes=ml['scratch'] + rg['scratch'],
        name="scans",
        compiler_params=_params("arbitrary", "arbitrary"),
    )(*ml['args'], *rg['args'])
    n_ml = len(ml['out_specs'])
    return outs[:n_ml], outs[n_ml:]


def _route(s, sb):
    tm = s.shape[1]
    neg = -jnp.inf
    sub = lax.broadcasted_iota(I32, (GROUP, tm), 0)
    blocks = [sb[gi * GROUP:(gi + 1) * GROUP, :] for gi in range(N_GROUPS)]
    gscore = []
    for blk in blocks:
        m1 = jnp.max(blk, axis=0, keepdims=True)
        first = jnp.min(jnp.where(blk == m1, sub, GROUP), axis=0, keepdims=True)
        m2 = jnp.max(jnp.where(sub == first, neg, blk), axis=0, keepdims=True)
        gscore.append(m1 + m2)
    masked = []
    for gi in range(N_GROUPS):
        rank = jnp.zeros((1, tm), F32)
        for gj in range(N_GROUPS):
            if gj == gi:
                continue
            ahead = (gscore[gj] >= gscore[gi]) if gj < gi else (gscore[gj] > gscore[gi])
            rank = rank + jnp.where(ahead, 1.0, 0.0)
        masked.append(jnp.where(rank < TOPK_GROUPS, blocks[gi], neg))
    v = jnp.concatenate(masked, axis=0)
    eid = lax.broadcasted_iota(I32, (N_EXPERTS, tm), 0)
    sel = jnp.zeros((N_EXPERTS, tm), F32)
    picks = []
    for _ in range(TOP_K):
        mx = jnp.max(v, axis=0, keepdims=True)
        idx = jnp.min(jnp.where(v == mx, eid, N_EXPERTS), axis=0, keepdims=True)
        pick = eid == idx
        picks.append(pick)
        sel = jnp.where(pick, 1.0, sel)
        v = jnp.where(pick, neg, v)
    ws = s * sel
    return ws * (ROUTED_SCALE / jnp.sum(ws, axis=0, keepdims=True)), sel, picks


def _mix_out_kernel(hmf_ref, hmb_ref, ot_ref, hrf_ref, hrb_ref, gr_ref, x_ref, rt_ref, ct_ref, mod_ref,
                    mln_ref, rgn_ref, wo_ml_ref, wo_rg_ref, n2_ref, rwt_ref, rb_ref, cnt0_ref,
                    x1_ref, hn2p_ref, ek_ref, pk_ref, wtok_ref, cnt_ref, cnt_sc, earlier_sc):
    i = pl.program_id(0)
    tm = x_ref.shape[0]

    @pl.when(i == 0)
    def _():
        cnt_sc[...] = cnt0_ref[...].astype(F32)
        earlier = (lax.broadcasted_iota(I32, (tm, tm), 0) < lax.broadcasted_iota(I32, (tm, tm), 1))
        earlier_sc[...] = earlier.astype(BF16)

    hm = hmf_ref[...] + hmb_ref[...]
    heads = []
    for h in range(N_HEADS):
        seg = hm[h * HEAD:(h + 1) * HEAD, :]
        heads.append(seg * lax.rsqrt(jnp.mean(seg * seg, axis=0, keepdims=True) + EPS))
    y_ml_t = jnp.concatenate(heads, axis=0) * mln_ref[...] * jax.nn.sigmoid(ot_ref[...])
    y_rg = _rms(hrf_ref[...] + hrb_ref[...], rgn_ref[...]) * jax.nn.gelu(gr_ref[...])
    mix = (lax.dot_general(y_ml_t.astype(BF16), wo_ml_ref[...], (((0,), (0,)), ((), ())),
                           preferred_element_type=F32)
           + jnp.dot(y_rg.astype(BF16), wo_rg_ref[...], preferred_element_type=F32))
    x1 = x_ref[...] + _pos_tile(rt_ref, ct_ref) + mod_ref[0, 2:3, :] * mix
    x1_ref[...] = x1
    hn2 = _rms(x1, n2_ref[...]) * (1.0 + mod_ref[0, 4:5, :]) + mod_ref[0, 3:4, :]
    hn2p_ref[...] = _pack_bf16_pairs(hn2)
    hb = hn2.astype(BF16)
    h_lo = (hn2 - hb.astype(F32)).astype(BF16)
    nt = (((1,), (1,)), ((), ()))
    two = lax.dot_general(rwt_ref[...], hb, nt, preferred_element_type=F32)
    logits_t = (two[0:N_EXPERTS, :] + two[N_EXPERTS:2 * N_EXPERTS, :]
                + lax.dot_general(rwt_ref[0:N_EXPERTS, :], h_lo, nt, preferred_element_type=F32))
    s = jax.nn.sigmoid(logits_t)
    wt, sel, picks = _route(s, s + rb_ref[...])

    prefix = jnp.dot(sel.astype(BF16), earlier_sc[...], preferred_element_type=F32)
    pos_all = cnt_sc[:, 0:1] + prefix
    eid = lax.broadcasted_iota(I32, (N_EXPERTS, tm), 0)
    eid_f = eid.astype(F32)
    row8 = lax.broadcasted_iota(I32, (TOP_K, tm), 0)
    ek = jnp.zeros((TOP_K, tm), F32)
    pk = jnp.zeros((TOP_K, tm), F32)
    wk = jnp.zeros((N_EXPERTS, tm), F32)
    for k, pick in enumerate(picks):
        take = lambda a: jnp.sum(jnp.where(pick, a, 0.0), axis=0, keepdims=True)
        ek = jnp.where(row8 == k, take(eid_f), ek)
        pk = jnp.where(row8 == k, take(pos_all), pk)
        wk = jnp.where(eid == k, take(wt), wk)
    ek_ref[...] = ek.astype(I32)
    pk_ref[...] = pk.astype(I32)
    wtok_ref[...] = wk.T
    cnt_sc[...] += jnp.broadcast_to(jnp.sum(sel, axis=1, keepdims=True), cnt_sc.shape)

    @pl.when(i == pl.num_programs(0) - 1)
    def _():
        cnt_ref[...] = cnt_sc[...].astype(I32)


def _mix_out(hmf, hmb, ot, hrf, hrb, gr, x2d, rtab, ctab, mod, t, tm, mln, rgn, wo_ml, wo_rg, norm2, rwt, rbias,
             cnt0):
    n = x2d.shape[0]
    tok = lambda i: (i, 0)
    tok_t = lambda i: (0, i)
    const = lambda i: (0, 0)
    return pl.pallas_call(
        _mix_out_kernel,
        grid=(n // tm,),
        in_specs=[pl.BlockSpec((D_ML, tm), tok_t)] * 3 + [pl.BlockSpec((tm, D_RG), tok)] * 3 + [
            pl.BlockSpec((tm, D_MODEL), tok)] + _pos_specs(rtab, ctab, tm) + [
            pl.BlockSpec((1, N_MOD, D_MODEL), _mod_index(mod.shape[0], tm, t)),
            pl.BlockSpec((D_ML, 1), const), pl.BlockSpec((1, D_RG), const),
            pl.BlockSpec((D_ML, D_MODEL), const), pl.BlockSpec((D_RG, D_MODEL), const),
            pl.BlockSpec((1, D_MODEL), const),
            pl.BlockSpec((2 * N_EXPERTS, D_MODEL), const), pl.BlockSpec((N_EXPERTS, 1), const),
            pl.BlockSpec((N_EXPERTS, LANES), const),
        ],
        out_specs=[pl.BlockSpec((tm, D_MODEL), tok), pl.BlockSpec((tm, D_PACK), tok),
                   pl.BlockSpec((TOP_K, tm), tok_t), pl.BlockSpec((TOP_K, tm), tok_t),
                   pl.BlockSpec((tm, N_EXPERTS), tok), pl.BlockSpec((N_EXPERTS, LANES), const)],
        out_shape=[jax.ShapeDtypeStruct((n, D_MODEL), F32), jax.ShapeDtypeStruct((n, D_PACK), I32),
                   jax.ShapeDtypeStruct((TOP_K, n), I32), jax.ShapeDtypeStruct((TOP_K, n), I32),
                   jax.ShapeDtypeStruct((n, N_EXPERTS), F32), jax.ShapeDtypeStruct((N_EXPERTS, LANES), I32)],
        scratch_shapes=[pltpu.VMEM((N_EXPERTS, LANES), F32), pltpu.VMEM((tm, tm), BF16)],
        name="mix_out",
        compiler_params=_params("arbitrary"),
    )(hmf, hmb, ot, hrf, hrb, gr, x2d, rtab, ctab, mod, mln, rgn, wo_ml, wo_rg, norm2, rwt, rbias, cnt0)


def _sc_mesh():
    return plsc.VectorSubcoreMesh(core_axis_name="core", subcore_axis_name="subcore")


def _sc_worker():
    info = plsc.get_sparse_core_info()
    return lax.axis_index("subcore") * info.num_cores + lax.axis_index("core"), info.num_cores * info.num_subcores


def _sc_dispatch(xp, dest3, n_slots):
    n, w = xp.shape
    nwin = n // SC_WINDOW

    @pl.kernel(out_type=jax.ShapeDtypeStruct((n_slots, w), xp.dtype), mesh=_sc_mesh(),
               scratch_types=[pltpu.VMEM((SC_WINDOW, w), xp.dtype), pltpu.VMEM((TOP_K, SC_WINDOW), I32)],
               name="sc_dispatch")
    def k(x_hbm, i_hbm, o_hbm, x_v, i_v):
        wid, nworkers = _sc_worker()
        per = nwin // nworkers

        @pl.loop(0, per)
        def _(s):
            win = wid * per + s
            pltpu.sync_copy(x_hbm.at[pl.ds(win * SC_WINDOW, SC_WINDOW)], x_v)
            pltpu.sync_copy(i_hbm.at[win], i_v)
            for j in range(TOP_K):
                pltpu.sync_copy(x_v, o_hbm.at[i_v.at[j]])

    return k(xp, dest3)


def _sc_pack_rows(w):
    r, c = w.shape
    half = c // 2
    lanes = plsc.get_sparse_core_info().num_lanes
    chunk = SC_PACK_WORDS // c
    nchunks = r // chunk

    def rne_high(x):
        u = plsc.bitcast(x, I32)
        u = u + 0x7FFF + (lax.shift_right_logical(u, jnp.full(u.shape, 16, I32)) & 1)
        return u & jnp.int32(-65536)

    @pl.kernel(out_type=jax.ShapeDtypeStruct((r, half), I32), mesh=_sc_mesh(),
               scratch_types=[pltpu.VMEM((chunk, c), F32), pltpu.VMEM((chunk, half), I32)],
               compiler_params=pltpu.CompilerParams(needs_layout_passes=False), name="sc_pack")
    def k(w_hbm, o_hbm, in_v, out_v):
        wid, nworkers = _sc_worker()
        per = nchunks // nworkers

        @pl.loop(0, per)
        def _(s):
            first = (wid * per + s) * chunk
            pltpu.sync_copy(w_hbm.at[pl.ds(first, chunk)], in_v)

            @pl.loop(0, chunk)
            def _(row):
                for j in range(half // lanes):
                    at = lambda off: (row, pl.ds(off + j * lanes, lanes))
                    hi = rne_high(in_v.at[*at(0)][...])
                    lo = rne_high(in_v.at[*at(half)][...])
                    out_v.at[*at(0)][...] = hi | lax.shift_right_logical(lo, jnp.full(lo.shape, 16, I32))

            pltpu.sync_copy(out_v, o_hbm.at[pl.ds(first, chunk)])

    return k(w)


def _sc_combine_gather(ys, dest3):
    nwin = dest3.shape[0]
    w = ys.shape[1]

    @pl.kernel(out_type=jax.ShapeDtypeStruct((nwin, TOP_K, SC_WINDOW, w), ys.dtype), mesh=_sc_mesh(),
               scratch_types=[pltpu.VMEM((SC_WINDOW, w), ys.dtype), pltpu.VMEM((TOP_K, SC_WINDOW), I32)],
               name="sc_combine")
    def k(y_hbm, i_hbm, o_hbm, y_v, i_v):
        wid, nworkers = _sc_worker()
        per = nwin // nworkers

        @pl.loop(0, per)
        def _(s):
            win = wid * per + s
            pltpu.sync_copy(i_hbm.at[win], i_v)
            for j in range(TOP_K):
                pltpu.sync_copy(y_hbm.at[i_v.at[j]], y_v)
                pltpu.sync_copy(y_v, o_hbm.at[win, j])

    return k(ys, dest3)


def _swiglu(x, w13):
    h = jnp.dot(x, w13, preferred_element_type=F32)
    return _silu(h[:, 0:D_EXPERT]) * h[:, D_EXPERT:2 * D_EXPERT]


def _unpack_rows_bf16(p):
    hi, lo = _unpack_bf16_pairs(p)
    return jnp.concatenate([hi.astype(BF16), lo.astype(BF16)], axis=1)


def _expert_kernel(rows, rank_ref, expert_ref, nu_ref, nr_ref, x_ref, w1_hbm, w3_hbm, w2_hbm, y_ref,
                   w1buf, w3buf, w2buf, sem, progress):
    b = pl.program_id(0)
    g = EXPERT_BLOCKS_PER_STEP
    n_ranks = nr_ref[0]

    @pl.when(b == 0)
    def _():
        progress[0] = 0
        progress[1] = 0

    def copies(r):
        slot = lax.rem(r, EXPERT_RING)
        e = expert_ref[r]
        return (pltpu.make_async_copy(w1_hbm.at[e], w1buf.at[slot], sem.at[slot, 0]),
                pltpu.make_async_copy(w3_hbm.at[e], w3buf.at[slot], sem.at[slot, 1]),
                pltpu.make_async_copy(w2_hbm.at[e], w2buf.at[slot], sem.at[slot, 2]))

    def start(r, carry):
        for cp in copies(r):
            cp.start()
        return carry

    def wait(r, carry):
        for cp in copies(r):
            cp.wait()
        return carry

    first = rank_ref[b * g]
    last = rank_ref[b * g + g - 1]
    started = jnp.minimum(first + EXPERT_RING, n_ranks)
    lax.fori_loop(progress[0], started, start, 0)
    progress[0] = jnp.maximum(progress[0], started)
    needed = jnp.where(b == pl.num_programs(0) - 1, progress[0], jnp.minimum(last + 1, n_ranks))
    lax.fori_loop(progress[1], needed, wait, 0)
    progress[1] = jnp.maximum(progress[1], needed)

    @pl.when(b * g < nu_ref[0])
    def _():
        for j in range(g):
            slot = lax.rem(rank_ref[b * g + j], EXPERT_RING)
            sl = slice(j * rows, (j + 1) * rows)
            x = _unpack_rows_bf16(x_ref[sl, :])
            h = (_silu(jnp.dot(x, _unpack_rows_bf16(w1buf[slot]), preferred_element_type=F32))
                 * jnp.dot(x, _unpack_rows_bf16(w3buf[slot]), preferred_element_type=F32))
            y_ref[sl, :] = _pack_bf16_pairs(
                jnp.dot(h.astype(BF16), _unpack_rows_bf16(w2buf[slot]), preferred_element_type=F32))


def _experts(xs, block_rank, rank_expert, n_used, n_ranks, w1, w3, w2, rows):
    g = EXPERT_BLOCKS_PER_STEP
    nb = xs.shape[0] // rows
    tok = lambda b, *_: (b, 0)
    return pl.pallas_call(
        functools.partial(_expert_kernel, rows),
        grid_spec=pltpu.PrefetchScalarGridSpec(
            num_scalar_prefetch=4,
            grid=(nb // g,),
            in_specs=[pl.BlockSpec((g * rows, D_PACK), tok)] + [pl.BlockSpec(memory_space=pl.ANY)] * 3,
            out_specs=pl.BlockSpec((g * rows, D_PACK), tok),
            scratch_shapes=[pltpu.VMEM((EXPERT_RING, D_MODEL, D_EXPERT // 2), I32),
                            pltpu.VMEM((EXPERT_RING, D_MODEL, D_EXPERT // 2), I32),
                            pltpu.VMEM((EXPERT_RING, D_EXPERT, D_PACK), I32),
                            pltpu.SemaphoreType.DMA((EXPERT_RING, 3)), pltpu.SMEM((2,), I32)],
        ),
        out_shape=jax.ShapeDtypeStruct(xs.shape, I32),
        name="experts",
        compiler_params=_params("arbitrary", vmem=VMEM_LIMIT_EXPERTS),
    )(block_rank, rank_expert, n_used, n_ranks, xs, w1, w3, w2)


def _moe_out_kernel(yk_ref, hn2p_ref, wtok_ref, sw13_ref, sw2_ref, x1_ref, mod_ref, nf_ref, y_ref):
    shared = jnp.dot(_swiglu(_unpack_rows_bf16(hn2p_ref[...]), sw13_ref[...]).astype(BF16), sw2_ref[...],
                     preferred_element_type=F32)
    w = wtok_ref[...]
    parts = []
    for wi in range(yk_ref.shape[0]):
        rows = slice(wi * SC_WINDOW, (wi + 1) * SC_WINDOW)
        a_hi = shared[rows, 0:D_PACK]
        a_lo = shared[rows, D_PACK:D_MODEL]
        for k in range(TOP_K):
            y_hi, y_lo = _unpack_bf16_pairs(yk_ref[wi, k])
            wc = w[rows, k:k + 1]
            a_hi = a_hi + wc * y_hi
            a_lo = a_lo + wc * y_lo
        parts.append(jnp.concatenate([a_hi, a_lo], axis=1))
    x2 = x1_ref[...] + mod_ref[0, 5:6, :] * jnp.concatenate(parts, axis=0)
    y_ref[...] = _rms(x2, nf_ref[...])


def _moe_out(yk, tok0, hn2p, wtok, sw13, sw2, x1, mod, t, tm, norm_final):
    n = hn2p.shape[0]
    tok = lambda i: (i, 0)
    const = lambda i: (0, 0)
    blk0 = tok0 // tm
    return pl.pallas_call(
        _moe_out_kernel,
        grid=(n // tm,),
        in_specs=[
            pl.BlockSpec((tm // SC_WINDOW, TOP_K, SC_WINDOW, D_PACK), lambda i: (i + blk0, 0, 0, 0)),
            pl.BlockSpec((tm, D_PACK), tok),
            pl.BlockSpec((tm, N_EXPERTS), tok),
            pl.BlockSpec((D_MODEL, 2 * D_EXPERT), const),
            pl.BlockSpec((D_EXPERT, D_MODEL), const),
            pl.BlockSpec((tm, D_MODEL), tok),
            pl.BlockSpec((1, N_MOD, D_MODEL), _mod_index(mod.shape[0], tm, t)),
            pl.BlockSpec((1, D_MODEL), const),
        ],
        out_specs=pl.BlockSpec((tm, D_MODEL), tok),
        out_shape=jax.ShapeDtypeStruct((n, D_MODEL), F32),
        name="moe_out",
        compiler_params=_params("arbitrary"),
    )(yk, hn2p, wtok, sw13, sw2, x1, mod, norm_final)


def _dispatch_plan(cnt, ek, pk, rows):
    n = ek.shape[1]
    nb = n * TOP_K // rows + N_EXPERTS
    nblk = (cnt + rows - 1) // rows
    block_end = jnp.cumsum(nblk)
    experts = jnp.arange(N_EXPERTS, dtype=I32)
    first_row = jnp.sum(jnp.where(ek[:, :, None] == experts, (block_end - nblk) * rows, 0), axis=-1)
    dest3 = (first_row + pk).reshape(TOP_K, n // SC_WINDOW, SC_WINDOW).transpose(1, 0, 2)
    owns = nblk > 0
    n_ranks = jnp.sum(owns.astype(I32))
    blocks = jnp.arange(nb, dtype=I32)[:, None]
    block_rank = jnp.minimum(jnp.sum((owns & (block_end <= blocks)).astype(I32), axis=1), n_ranks - 1)
    rank_expert = jnp.minimum(jnp.sum((jnp.cumsum(owns.astype(I32)) <= experts[:, None]).astype(I32), axis=1),
                              N_EXPERTS - 1)
    return (dest3, block_rank.astype(I32), rank_expert.astype(I32), block_end[-1:].astype(I32),
            n_ranks.reshape(1).astype(I32), nb * rows)


def _grid_pos_tables(n_tokens, dim):
    quarter = dim // 4
    omega = 1.0 / (POS_BASE ** (jnp.arange(quarter, dtype=F32) / quarter))
    ra = jnp.arange(n_tokens // GRID_W).astype(F32)[:, None] * omega
    ca = jnp.arange(GRID_W).astype(F32)[:, None] * omega
    return (jnp.concatenate([jnp.sin(ra), jnp.cos(ra)], axis=-1),
            jnp.concatenate([jnp.sin(ca), jnp.cos(ca)], axis=-1))


def _hi_lo_rows(w):
    hi = w.astype(BF16)
    return jnp.concatenate([hi, (w - hi.astype(F32)).astype(BF16)], axis=0)


def _block_diag(w):
    eye = jnp.eye(N_RG_BLOCKS, dtype=w.dtype)
    return jnp.einsum('nij,nm->nimj', w, eye).reshape(D_RG, D_RG)


def _layer_weights(l, norm1, w_in, mlstm_gate_bias, mlstm_norm, rg_conv_w, rg_conv_b, rg_wa, rg_ba, rg_wx,
                   rg_bx, rg_lambda, rg_norm, w_out, norm2, router_w, router_bias, exp_w1, exp_w3, exp_w2,
                   shared_w1, shared_w3, shared_w2):
    wi = w_in[l]
    c0, c1 = 4 * D_ML, 4 * D_ML + N_GATE
    wi16 = wi.astype(BF16)
    wg = wi[:, c0:c1]
    wg_hi = wi16[:, c0:c1]
    wg_lo = (wg - wg_hi.astype(F32)).astype(BF16)
    zcols = lambda w: jnp.zeros((D_MODEL, w), BF16)
    return dict(
        norm1=norm1[l].reshape(1, D_MODEL),
        wq=wi16[:, :c0],
        wr=jnp.concatenate([wi16[:, c1:], wg_hi, wg_lo, zcols(LANES - 2 * N_GATE)], axis=1),
        wgh=jnp.concatenate([wg_hi, zcols(LANES - N_GATE)], axis=1),
        gbias=jnp.pad(mlstm_gate_bias[l].reshape(1, N_GATE), ((0, 0), (0, LANES - N_GATE))),
        mln=mlstm_norm[l].reshape(D_ML, 1),
        cw=rg_conv_w[l], cb=rg_conv_b[l].reshape(1, D_RG),
        wbd=jnp.concatenate([_block_diag(rg_wa[l, 0]), _block_diag(rg_wx[l, 0]),
                             _block_diag(rg_wa[l, 1]), _block_diag(rg_wx[l, 1])], axis=1).astype(BF16),
        rbias=jnp.concatenate([rg_ba[l, 0], rg_bx[l, 0], rg_ba[l, 1], rg_bx[l, 1]]).reshape(1, 4 * D_RG),
        lam=rg_lambda[l], rgn=rg_norm[l].reshape(1, D_RG),
        wo_ml=w_out[l, :D_ML].astype(BF16), wo_rg=w_out[l, D_ML:].astype(BF16),
        norm2=norm2[l].reshape(1, D_MODEL),
        rwt=_hi_lo_rows(router_w[l].T), rb=router_bias[l].reshape(N_EXPERTS, 1),
        w1=_sc_pack_rows(exp_w1[l].reshape(N_EXPERTS * D_MODEL, D_EXPERT)).reshape(N_EXPERTS, D_MODEL, -1),
        w3=_sc_pack_rows(exp_w3[l].reshape(N_EXPERTS * D_MODEL, D_EXPERT)).reshape(N_EXPERTS, D_MODEL, -1),
        w2=_sc_pack_rows(exp_w2[l].reshape(N_EXPERTS * D_EXPERT, D_MODEL)).reshape(N_EXPERTS, D_EXPERT, -1),
        sw13=jnp.concatenate([shared_w1[l], shared_w3[l]], axis=-1).astype(BF16),
        sw2=shared_w2[l].astype(BF16),
    )


def _mixers(x2d, pos_tables, mod, c0, n0, m0, h0, cnt0, lw, b, t, emit_state):
    tl = _tiles(t, mod.shape[0] > 1)
    tm = tl['tok']
    rtab, ctab = pos_tables
    qt, k, kt, v, vt, ot, xr, gr, g, gt = _in_proj(x2d, rtab, ctab, mod, t, tm, lw['norm1'], lw['wq'], lw['wr'],
                                                   lw['wgh'], lw['gbias'])
    chunk = tl['chunk']
    ml, rg = _scans(
        _mlstm_plan(qt, k, kt, v, vt, g, gt, c0, n0, m0, b, t, chunk, emit_state),
        _rglru_plan(xr, h0, lw['wbd'], lw['rbias'], lw['lam'], lw['cw'], lw['cb'], b, t, chunk, emit_state),
        b, t // chunk)
    routed = _mix_out(ml[0], ml[1], ot, rg[0], rg[1], gr, x2d, rtab, ctab, mod, t, tm, lw['mln'], lw['rgn'],
                      lw['wo_ml'], lw['wo_rg'], lw['norm2'], lw['rwt'], lw['rb'], cnt0)
    return routed, ml[2:], rg[2:]


def _routed_experts(paths, lw):
    cnt = paths[-1][5][:, 0]
    hn2p = jnp.concatenate([p[1] for p in paths], axis=0)
    ek = jnp.concatenate([p[2] for p in paths], axis=1)
    pk = jnp.concatenate([p[3] for p in paths], axis=1)
    dest3, block_rank, rank_expert, n_used, n_ranks, n_slots = _dispatch_plan(cnt, ek, pk, EXPERT_ROWS)
    xs = _sc_dispatch(hn2p, dest3, n_slots)
    ys = _experts(xs, block_rank, rank_expert, n_used, n_ranks, lw['w1'], lw['w3'], lw['w2'], EXPERT_ROWS)
    return _sc_combine_gather(ys, dest3)


def kernel(x_prompt, x_sample, c, state_mlstm_C, state_mlstm_n, state_mlstm_m, state_rglru_h, c_ctx, w_ada, b_ada, norm1, w_in, mlstm_gate_bias, mlstm_norm, rg_conv_w, rg_conv_b, rg_wa, rg_ba, rg_wx, rg_bx, rg_lambda, rg_norm, w_out, norm2, router_w, router_bias, exp_w1, exp_w3, exp_w2, shared_w1, shared_w3, shared_w2, norm_final):
    bp, tp, _ = x_prompt.shape
    bs, ts, _ = x_sample.shape
    depth = w_ada.shape[0]
    assert depth == 1, "the final norm is fused into the single layer's MoE output kernel"
    nd = 2 * N_HEADS
    l = 0
    lw = _layer_weights(l, norm1, w_in, mlstm_gate_bias, mlstm_norm, rg_conv_w, rg_conv_b, rg_wa, rg_ba, rg_wx,
                        rg_bx, rg_lambda, rg_norm, w_out, norm2, router_w, router_bias, exp_w1, exp_w3, exp_w2,
                        shared_w1, shared_w3, shared_w2)
    nf = norm_final.reshape(1, D_MODEL)
    cvecs = jnp.concatenate([c_ctx[None], c, jnp.zeros((SUBLANES - 1 - bs, D_MODEL), F32)], axis=0)
    mod = _ada(cvecs, w_ada[l], b_ada[l]).reshape(SUBLANES, N_MOD, D_MODEL)

    xp2d, lw['w1'], lw['w3'], lw['w2'] = lax.optimization_barrier(
        (x_prompt.reshape(bp * tp, D_MODEL), lw['w1'], lw['w3'], lw['w2']))

    mod_p, mod_s = mod[0:1], mod[1:1 + bs]
    tm_p, tm_s = _tiles(tp, False)['tok'], _tiles(ts, True)['tok']
    rp, (cc, nc_, mc), (hc,) = _mixers(
        xp2d,
        (jnp.zeros((tm_p // GRID_W, D_MODEL // 2), F32), jnp.zeros((GRID_W, D_MODEL // 2), F32)), mod_p,
        jnp.zeros((1, nd, HEAD, HEAD), F32), jnp.zeros((1, nd, HEAD), F32), jnp.zeros((1, nd, HEAD), F32),
        jnp.zeros((1, 2, D_RG), F32), jnp.zeros((N_EXPERTS, LANES), I32), lw, bp, tp, True)
    rs, _, _ = _mixers(
        x_sample.reshape(bs * ts, D_MODEL), _grid_pos_tables(ts, D_MODEL), mod_s,
        state_mlstm_C[:, l].reshape(bs, nd, HEAD, HEAD), state_mlstm_n[:, l].reshape(bs, nd, HEAD),
        jnp.broadcast_to(state_mlstm_m[:, l].reshape(bs, nd, 1), (bs, nd, HEAD)),
        state_rglru_h[:, l], jnp.zeros((N_EXPERTS, LANES), I32), lw, bs, ts, False)
    yp = _moe_out(_routed_experts([rp], lw), 0, rp[1], rp[4], lw['sw13'], lw['sw2'], rp[0], mod_p, tp, tm_p, nf)
    ys = _moe_out(_routed_experts([rs], lw), 0, rs[1], rs[4], lw['sw13'], lw['sw2'], rs[0], mod_s, ts, tm_s, nf)

    y_prompt = yp.reshape(bp, tp, D_MODEL)
    y_sample = ys.reshape(bs, ts, D_MODEL)
    new_c = cc.reshape(bp, 1, 2, N_HEADS, HEAD, HEAD)
    new_n = nc_.reshape(bp, 1, 2, N_HEADS, HEAD)
    new_m = mc[:, :, 0].reshape(bp, 1, 2, N_HEADS)
    new_h = hc.reshape(bp, 1, 2, D_RG)
    return (y_prompt, y_sample, new_c, new_n, new_m, new_h)
```

```python
import functools

import jax
import jax.numpy as jnp
from jax import lax
from jax.experimental import pallas as pl
from jax.experimental.pallas import tpu as pltpu
from jax.experimental.pallas import tpu_sc as plsc

F32 = jnp.float32
BF16 = jnp.bfloat16
I32 = jnp.int32
HIGHEST = lax.Precision.HIGHEST

D_MODEL = 1024
N_MOD = 6
D_ML = 512
N_HEADS = 4
HEAD = 128
D_RG = 512
N_RG_BLOCKS = 8
RG_BLOCK = 64
RG_C = 8.0
N_GATE = 16
N_EXPERTS = 64
N_GROUPS = 8
GROUP = 8
TOPK_GROUPS = 4
TOP_K = 8
D_EXPERT = 256
ROUTED_SCALE = 2.5
EPS = 1e-6
GRID_W = 64
POS_BASE = 10000.0

RG_SEG_PAD = 8
SC_WINDOW = 128
SC_PACK_WORDS = 65536
D_PACK = D_MODEL // 2
MOE_OUT_BUFFERS = 3
EXPERT_ROWS = 512
EXPERT_BLOCKS_PER_STEP = 8
EXPERT_RING = 10

SUBLANES = 8
LANES = 128
VMEM_LIMIT = 48 * 1024 * 1024
VMEM_LIMIT_EXPERTS = 58 * 1024 * 1024


def _params(*sem, vmem=VMEM_LIMIT):
    return pltpu.CompilerParams(dimension_semantics=sem, vmem_limit_bytes=vmem)


def _tiles(t, per_sequence_mod):
    cap = t if per_sequence_mod else 1 << 30
    return dict(
        tok=min(512, cap),
        chunk=min(256, t),
    )


def _silu(x):
    return x * jax.nn.sigmoid(x)


def _softplus(x):
    return jnp.maximum(x, 0.0) + jnp.log1p(jnp.exp(-jnp.abs(x)))


def _rms(x, g):
    return x * lax.rsqrt(jnp.mean(x * x, axis=-1, keepdims=True) + EPS) * g


def _bf16_pieces(x):
    hi = x.astype(BF16)
    r = x - hi.astype(F32)
    mid = r.astype(BF16)
    return hi, mid, (r - mid.astype(F32)).astype(BF16)


def _pack_bf16_pairs(x):
    w = x.shape[1] // 2
    hi = lax.bitcast_convert_type(x[:, :w].astype(BF16).astype(F32), I32)
    lo = lax.bitcast_convert_type(x[:, w:].astype(BF16).astype(F32), I32)
    return hi | lax.shift_right_logical(lo, jnp.full(lo.shape, 16, I32))


def _unpack_bf16_pairs(p):
    hi = lax.bitcast_convert_type(p & jnp.int32(-65536), F32)
    lo = lax.bitcast_convert_type(lax.shift_left(p, jnp.full(p.shape, 16, I32)), F32)
    return hi, lo


def _ada_kernel(c_ref, w_ref, b_ref, o_ref):
    s_hi, s_mid, s_lo = _bf16_pieces(_silu(c_ref[...]))
    w = w_ref[...]
    w_hi = w.astype(BF16)
    w_lo = (w - w_hi.astype(F32)).astype(BF16)
    dot = lambda a, b: jnp.dot(a, b, preferred_element_type=F32)
    o_ref[...] = dot(s_hi, w_hi) + dot(s_hi, w_lo) + dot(s_mid, w_hi) + dot(s_lo, w_hi) + b_ref[...]


def _ada(cvecs, w_ada, b_ada):
    n_out = w_ada.shape[1]
    tn = 1536
    return pl.pallas_call(
        _ada_kernel,
        grid=(n_out // tn,),
        in_specs=[
            pl.BlockSpec((SUBLANES, D_MODEL), lambda j: (0, 0)),
            pl.BlockSpec((D_MODEL, tn), lambda j: (0, j)),
            pl.BlockSpec((1, tn), lambda j: (0, j)),
        ],
        out_specs=pl.BlockSpec((SUBLANES, tn), lambda j: (0, j)),
        out_shape=jax.ShapeDtypeStruct((SUBLANES, n_out), F32),
        name="ada",
        compiler_params=_params("arbitrary"),
    )(cvecs, w_ada, b_ada.reshape(1, n_out))


def _pos_tile(rt_ref, ct_ref):
    left = jnp.concatenate([jnp.broadcast_to(rt_ref[r:r + 1, :], (GRID_W, rt_ref.shape[1]))
                            for r in range(rt_ref.shape[0])], axis=0)
    right = jnp.concatenate([ct_ref[...]] * rt_ref.shape[0], axis=0)
    return jnp.concatenate([left, right], axis=1)


def _in_proj_kernel(x_ref, rt_ref, ct_ref, mod_ref, n1_ref, wq_ref, wr_ref, wgh_ref, gb_ref,
                    qt_ref, k_ref, kt_ref, v_ref, vt_ref, ot_ref, xr_ref, gr_ref, g_ref, gt_ref):
    x = x_ref[...] + _pos_tile(rt_ref, ct_ref)
    hn = _rms(x, n1_ref[...]) * (1.0 + mod_ref[0, 1:2, :]) + mod_ref[0, 0:1, :]
    hb = hn.astype(BF16)
    z = jnp.dot(hb, wq_ref[...], preferred_element_type=F32)
    k = z[:, D_ML:2 * D_ML] * (HEAD ** -0.5)
    v = z[:, 2 * D_ML:3 * D_ML]
    qt_ref[...] = z[:, 0:D_ML].T.astype(BF16)
    k_ref[...] = k.astype(BF16)
    kt_ref[...] = k.T.astype(BF16)
    v_ref[...] = v.astype(BF16)
    vt_ref[...] = v.T.astype(BF16)
    ot_ref[...] = z[:, 3 * D_ML:4 * D_ML].T
    zr = jnp.dot(hb, wr_ref[...], preferred_element_type=F32)
    xr_ref[...] = zr[:, 0:D_RG]
    gr_ref[...] = zr[:, D_RG:2 * D_RG]
    zg = zr[:, 2 * D_RG:2 * D_RG + LANES]
    h_lo = (hn - hb.astype(F32)).astype(BF16)
    g = (zg + pltpu.roll(zg, LANES - N_GATE, 1)
         + jnp.dot(h_lo, wgh_ref[...], preferred_element_type=F32) + gb_ref[...])
    col = lax.broadcasted_iota(I32, g.shape, 1)
    g = jnp.where((col & 4) != 0, -_softplus(-g), g)
    g_ref[...] = g[:, 0:N_GATE]
    gt_ref[...] = g.T[0:N_GATE, :]


def _mod_index(bm, tm, t):
    if bm > 1:
        return lambda i: ((i * tm) // t, 0, 0)
    return lambda i: (0, 0, 0)


def _pos_specs(rtab, ctab, tm):
    rows = tm // GRID_W
    period = rtab.shape[0] // rows
    return [pl.BlockSpec((rows, D_MODEL // 2), lambda i: (i % period, 0)),
            pl.BlockSpec((GRID_W, D_MODEL // 2), lambda i: (0, 0))]


def _in_proj(x2d, rtab, ctab, mod, t, tm, norm1, wq, wr, wgh, gbias):
    n = x2d.shape[0]
    tok = lambda i: (i, 0)
    tok_t = lambda i: (0, i)
    const = lambda i: (0, 0)
    f = lambda w: jax.ShapeDtypeStruct((n, w), F32)
    row16 = jax.ShapeDtypeStruct((n, D_ML), BF16)
    col16 = jax.ShapeDtypeStruct((D_ML, n), BF16)
    return pl.pallas_call(
        _in_proj_kernel,
        grid=(n // tm,),
        in_specs=[pl.BlockSpec((tm, D_MODEL), tok)] + _pos_specs(rtab, ctab, tm) + [
            pl.BlockSpec((1, N_MOD, D_MODEL), _mod_index(mod.shape[0], tm, t)),
            pl.BlockSpec((1, D_MODEL), const),
            pl.BlockSpec((D_MODEL, 4 * D_ML), const),
            pl.BlockSpec((D_MODEL, 2 * D_RG + LANES), const),
            pl.BlockSpec((D_MODEL, LANES), const),
            pl.BlockSpec((1, LANES), const),
        ],
        out_specs=[pl.BlockSpec((D_ML, tm), tok_t), pl.BlockSpec((tm, D_ML), tok), pl.BlockSpec((D_ML, tm), tok_t),
                   pl.BlockSpec((tm, D_ML), tok), pl.BlockSpec((D_ML, tm), tok_t), pl.BlockSpec((D_ML, tm), tok_t),
                   pl.BlockSpec((tm, D_RG), tok), pl.BlockSpec((tm, D_RG), tok),
                   pl.BlockSpec((tm, N_GATE), tok), pl.BlockSpec((N_GATE, tm), tok_t)],
        out_shape=[col16, row16, col16, row16, col16, jax.ShapeDtypeStruct((D_ML, n), F32),
                   f(D_RG), f(D_RG), f(N_GATE), jax.ShapeDtypeStruct((N_GATE, n), F32)],
        name="in_proj",
        compiler_params=_params("arbitrary"),
    )(x2d, rtab, ctab, mod, norm1, wq, wr, wgh, gbias)


ALL_PARTS = ("init", "main", "final")


def _mlstm_kernel(chunk, nc, emit_state, *refs, parts=ALL_PARTS):
    (qtf, kf, ktf, vf, vtf, gf, gtf, qtb, kb, ktb, vb, vtb, gb, gtb, c0_ref, n0_ref, m0_ref) = refs[:17]
    if emit_state:
        hf_ref, hb_ref, c_out, n_out, m_out, c_sc, n_sc, m_sc = refs[17:]
    else:
        hf_ref, hb_ref, c_sc, n_sc, m_sc = refs[17:]
    i = pl.program_id(1)

    if "init" in parts:
        @pl.when(i == 0)
        def _():
            c_sc[...] = c0_ref[0]
            n_sc[...] = n0_ref[0]
            m_sc[...] = m0_ref[0]

    def write_state():
        if emit_state and "final" in parts:
            @pl.when(i == nc - 1)
            def _():
                c_out[0] = c_sc[...]
                n_out[0] = n_sc[...]
                m_out[0] = m_sc[...]

    if "main" not in parts:
        write_state()
        return

    key = lax.broadcasted_iota(I32, (chunk, chunk), 0)
    qry = lax.broadcasted_iota(I32, (chunk, chunk), 1)
    hd = []
    for d, (qt_ref, k_ref, kt_ref, v_ref, vt_ref, g_ref, gt_ref, h_ref) in enumerate(
            ((qtf, kf, ktf, vf, vtf, gf, gtf, hf_ref), (qtb, kb, ktb, vb, vtb, gb, gtb, hb_ref))):
        tri = (key <= qry) if d == 0 else (key >= qry)
        tri_t = (qry <= key) if d == 0 else (qry >= key)
        g = g_ref[...]
        gt = gt_ref[...]
        rows3 = jnp.dot(jnp.concatenate(_bf16_pieces(gt), axis=0), tri.astype(BF16), preferred_element_type=F32)
        brow = rows3[0:N_GATE] + rows3[N_GATE:2 * N_GATE] + rows3[2 * N_GATE:3 * N_GATE]
        tri_t16 = tri_t.astype(BF16)
        bcol = sum(jnp.dot(tri_t16, piece, preferred_element_type=F32) for piece in _bf16_pieces(g))
        blast = bcol[chunk - 1:chunk, :] if d == 0 else bcol[0:1, :]
        for h in range(N_HEADS):
            ci = d * 8 + h
            cf = d * 8 + 4 + h
            j = d * N_HEADS + h
            sl = slice(h * HEAD, (h + 1) * HEAD)
            hd.append(dict(
                j=j, sl=sl, tri=tri, h_ref=h_ref, qt=qt_ref[sl, :], k=k_ref[:, sl], kt=kt_ref[sl, :],
                v=v_ref[:, sl], vt=vt_ref[sl, :], b_row=brow[cf:cf + 1, :],
                gate_col=g[:, ci:ci + 1] - bcol[:, cf:cf + 1], gate_row=gt[ci:ci + 1, :] - brow[cf:cf + 1, :],
                b_last=blast[:, cf:cf + 1], m_prev=m_sc[j:j + 1, 0:1], c_prev=c_sc[j], n_prev=n_sc[j:j + 1, :]))
    for x in hd:
        top = jnp.max(jnp.where(x['tri'], x['gate_col'], -jnp.inf), axis=0, keepdims=True)
        mx = jnp.maximum(x['m_prev'], top)
        x['dm'] = jnp.exp(jnp.where(x['tri'], x['gate_col'] - mx, -jnp.inf))
        x['w_inter'] = jnp.exp(x['m_prev'] - mx)
        x['floor'] = jnp.exp(-(x['b_row'] + mx))
    for x in hd:
        x['st'] = jnp.dot(x['k'], x['qt'], preferred_element_type=F32) * x['dm']
    for x in hd:
        inter = lax.dot_general(x['c_prev'].astype(BF16), x['qt'], (((0,), (0,)), ((), ())),
                                preferred_element_type=F32)
        num = x['w_inter'] * inter + jnp.dot(x['vt'], x['st'].astype(BF16), preferred_element_type=F32)
        qn = jnp.dot(jnp.broadcast_to(x['n_prev'], (SUBLANES, HEAD)).astype(BF16), x['qt'],
                     preferred_element_type=F32)[0:1, :]
        den = x['w_inter'] * qn + jnp.sum(x['st'], axis=0, keepdims=True)
        x['h_ref'][x['sl'], :] = num / jnp.maximum(jnp.abs(den), x['floor'])
    for x in hd:
        j = x['j']
        log_w = x['b_last'] + x['gate_row']
        m_new = jnp.maximum(x['b_last'] + x['m_prev'], jnp.max(log_w, axis=1, keepdims=True))
        decay = jnp.exp(x['b_last'] + x['m_prev'] - m_new)
        w_row = jnp.exp(log_w - m_new)
        kwt = (x['kt'].astype(F32) * w_row).astype(BF16)
        c_sc[j] = decay * x['c_prev'] + jnp.dot(kwt, x['v'], preferred_element_type=F32)
        n_sc[j:j + 1, :] = decay * x['n_prev'] + jnp.dot(
            jnp.broadcast_to(w_row, (SUBLANES, chunk)).astype(BF16), x['k'], preferred_element_type=F32)[0:1, :]
        m_sc[j:j + 1, :] = jnp.broadcast_to(m_new, (1, HEAD))

    write_state()


def _state_index(bm):
    if bm > 1:
        return lambda b, i: (b,) + (0,) * 3, lambda b, i: (b, 0, 0)
    return lambda b, i: (0,) * 4, lambda b, i: (0, 0, 0)


def _mlstm_plan(qt, k, kt, v, vt, g, gt, c0, n0, m0, b, t, chunk, emit_state):
    nc = t // chunk
    n = b * t
    nd = 2 * N_HEADS
    fwd = lambda bi, i: (bi * nc + i, 0)
    bwd = lambda bi, i: (bi * nc + nc - 1 - i, 0)
    fwd_t = lambda bi, i: (0, bi * nc + i)
    bwd_t = lambda bi, i: (0, bi * nc + nc - 1 - i)
    c_idx, n_idx = _state_index(c0.shape[0])
    rows = lambda m: pl.BlockSpec((chunk, D_ML), m)
    cols = lambda m: pl.BlockSpec((D_ML, chunk), m)
    one_dir = lambda m, mt: [cols(mt), rows(m), cols(mt), rows(m), cols(mt),
                             pl.BlockSpec((chunk, N_GATE), m), pl.BlockSpec((N_GATE, chunk), mt)]
    in_specs = (one_dir(fwd, fwd_t) + one_dir(bwd, bwd_t)
                + [pl.BlockSpec((1, nd, HEAD, HEAD), c_idx),
                   pl.BlockSpec((1, nd, HEAD), n_idx), pl.BlockSpec((1, nd, HEAD), n_idx)])
    out_specs = [cols(fwd_t), cols(bwd_t)]
    out_shape = [jax.ShapeDtypeStruct((D_ML, n), F32)] * 2
    if emit_state:
        out_specs += [pl.BlockSpec((1, nd, HEAD, HEAD), lambda bi, i: (bi, 0, 0, 0)),
                      pl.BlockSpec((1, nd, HEAD), lambda bi, i: (bi, 0, 0)),
                      pl.BlockSpec((1, nd, HEAD), lambda bi, i: (bi, 0, 0))]
        out_shape += [jax.ShapeDtypeStruct((b, nd, HEAD, HEAD), F32),
                      jax.ShapeDtypeStruct((b, nd, HEAD), F32),
                      jax.ShapeDtypeStruct((b, nd, HEAD), F32)]
    return dict(
        kernel=functools.partial(_mlstm_kernel, chunk, nc, emit_state), in_specs=in_specs, out_specs=out_specs,
        out_shape=out_shape,
        scratch=[pltpu.VMEM((nd, HEAD, HEAD), F32), pltpu.VMEM((nd, HEAD), F32), pltpu.VMEM((nd, HEAD), F32)],
        args=(qt, k, kt, v, vt, g, gt, qt, k, kt, v, vt, g, gt, c0, n0, m0))


def _neg_expm1_2x(x):
    t = jnp.tanh(x)
    return -2.0 * t / (1.0 - t)


def _rglru_kernel(tb, nb, emit_state, *refs, parts=ALL_PARTS):
    (xf, xf_prev, xf_next, xb, xb_prev, xb_next, h0_ref, w_ref, bias_ref, lam_ref, cw_ref, cb_ref) = refs[:12]
    n_out = 3 if emit_state else 2
    hf_ref, hb_ref = refs[12:14]
    hfin_ref = refs[14] if emit_state else None
    carry, af_sc, uf_sc, ab_sc, ub_sc, hf_sc, pf_sc, hb_sc, pb_sc = refs[12 + n_out:]
    i = pl.program_id(1)

    if "init" in parts:
        @pl.when(i == 0)
        def _():
            carry[...] = h0_ref[0]

    def write_state():
        if emit_state and "final" in parts:
            @pl.when(i == nb - 1)
            def _():
                hfin_ref[0] = carry[...]

    if "main" not in parts:
        write_state()
        return

    row8 = lax.broadcasted_iota(I32, (SUBLANES, D_RG), 0)
    cw = cw_ref[...]
    log_a_per_r = -RG_C * _softplus(-lam_ref[...])

    def taps(xm2, xm1, x0, xp1):
        return cb_ref[...] + xm2 * cw[0:1, :] + xm1 * cw[1:2, :] + x0 * cw[2:3, :] + xp1 * cw[3:4, :]

    def conv(main_ref, prev_ref, next_ref, first, last):
        main = main_ref[...]
        prev = jnp.where(first, 0.0, prev_ref[...])
        nxt = jnp.where(last, 0.0, next_ref[...])
        body = taps(pltpu.roll(main, 2, 0), pltpu.roll(main, 1, 0), main, pltpu.roll(main, tb - 1, 0))
        e = SUBLANES
        head, tail = main[0:e, :], main[tb - e:tb, :]
        before_tail = main[tb - 2 * e:tb - e, :]
        fix_head = taps(
            jnp.where(row8 == 0, prev[6:7, :], jnp.where(row8 == 1, prev[7:8, :], pltpu.roll(head, 2, 0))),
            jnp.where(row8 == 0, prev[7:8, :], pltpu.roll(head, 1, 0)), head,
            jnp.where(row8 == e - 1, main[e:e + 1, :], pltpu.roll(head, e - 1, 0)))
        fix_tail = taps(
            jnp.where(row8 == 0, before_tail[6:7, :],
                      jnp.where(row8 == 1, before_tail[7:8, :], pltpu.roll(tail, 2, 0))),
            jnp.where(row8 == 0, before_tail[7:8, :], pltpu.roll(tail, 1, 0)), tail,
            jnp.where(row8 == e - 1, nxt[0:1, :], pltpu.roll(tail, e - 1, 0)))
        return jnp.concatenate([fix_head, body[e:tb - e, :], fix_tail], axis=0)

    def recurrence_terms(xc, d):
        z = jnp.dot(xc.astype(BF16), w_ref[:, d * 2 * D_RG:(d + 1) * 2 * D_RG],
                    preferred_element_type=F32) + bias_ref[:, d * 2 * D_RG:(d + 1) * 2 * D_RG]
        r = 0.5 * jnp.tanh(0.5 * z[:, 0:D_RG]) + 0.5
        ig = 0.5 * jnp.tanh(0.5 * z[:, D_RG:2 * D_RG]) + 0.5
        log_a = r * log_a_per_r[d:d + 1, :]
        a = jnp.exp(log_a)
        u = jnp.sqrt(_neg_expm1_2x(log_a)) * (ig * xc)
        return a, u

    a_f, u_f = recurrence_terms(conv(xf, xf_prev, xf_next, i == 0, i == nb - 1), 0)
    a_b, u_b = recurrence_terms(conv(xb, xb_prev, xb_next, i == nb - 1, i == 0), 1)
    seg = tb // SUBLANES
    pitch = seg + RG_SEG_PAD
    ncol = D_RG // LANES
    for lc in range(ncol):
        lanes = slice(lc * LANES, (lc + 1) * LANES)
        for s in range(SUBLANES):
            src = slice(s * seg, (s + 1) * seg)
            dst = slice(s * pitch, s * pitch + seg)
            af_sc[lc, dst, :], uf_sc[lc, dst, :] = a_f[src, lanes], u_f[src, lanes]
            ab_sc[lc, dst, :], ub_sc[lc, dst, :] = a_b[src, lanes], u_b[src, lanes]
    slab = lambda k: (slice(None), pl.ds(k, SUBLANES, stride=pitch), slice(None))
    hf = jnp.zeros((ncol, SUBLANES, LANES), F32)
    hb = jnp.zeros((ncol, SUBLANES, LANES), F32)
    pf = jnp.ones((ncol, SUBLANES, LANES), F32)
    pb = jnp.ones((ncol, SUBLANES, LANES), F32)
    for k in range(seg):
        kb = seg - 1 - k
        ak = af_sc[slab(k)]
        hf = ak * hf + uf_sc[slab(k)]
        pf = pf * ak
        hf_sc[slab(k)] = hf
        pf_sc[slab(k)] = pf
        ak = ab_sc[slab(kb)]
        hb = ak * hb + ub_sc[slab(kb)]
        pb = pb * ak
        hb_sc[slab(kb)] = hb
        pb_sc[slab(kb)] = pb
    for lc in range(ncol):
        lanes = slice(lc * LANES, (lc + 1) * LANES)
        c = carry[0:1, lanes]
        cin_f = []
        for s in range(SUBLANES):
            cin_f.append(c)
            c = pf[lc, s:s + 1, :] * c + hf[lc, s:s + 1, :]
        carry[0:1, lanes] = c
        c = carry[1:2, lanes]
        cin_b = [None] * SUBLANES
        for s in reversed(range(SUBLANES)):
            cin_b[s] = c
            c = pb[lc, s:s + 1, :] * c + hb[lc, s:s + 1, :]
        carry[1:2, lanes] = c
        for s in range(SUBLANES):
            rows = slice(s * seg, (s + 1) * seg)
            src = slice(s * pitch, s * pitch + seg)
            hf_ref[rows, lanes] = hf_sc[lc, src, :] + pf_sc[lc, src, :] * cin_f[s]
            hb_ref[rows, lanes] = hb_sc[lc, src, :] + pb_sc[lc, src, :] * cin_b[s]

    write_state()


def _rglru_plan(xr, h0, wbd, bias, lam, cw, cb, b, t, tb, emit_state):
    nb = t // tb
    n = b * t
    r8 = tb // SUBLANES
    last8 = n // SUBLANES - 1
    fwd = lambda bi, i: (bi * nb + i, 0)
    bwd = lambda bi, i: (bi * nb + nb - 1 - i, 0)
    fwd_prev = lambda bi, i: (jnp.maximum((bi * nb + i) * r8 - 1, 0), 0)
    fwd_next = lambda bi, i: (jnp.minimum((bi * nb + i + 1) * r8, last8), 0)
    bwd_prev = lambda bi, i: (jnp.maximum((bi * nb + nb - 1 - i) * r8 - 1, 0), 0)
    bwd_next = lambda bi, i: (jnp.minimum((bi * nb + nb - i) * r8, last8), 0)
    const = lambda bi, i: (0, 0)
    h_idx = (lambda bi, i: (bi, 0, 0)) if h0.shape[0] > 1 else (lambda bi, i: (0, 0, 0))
    halo = lambda m: pl.BlockSpec((SUBLANES, D_RG), m)
    in_specs = [pl.BlockSpec((tb, D_RG), fwd), halo(fwd_prev), halo(fwd_next),
                pl.BlockSpec((tb, D_RG), bwd), halo(bwd_prev), halo(bwd_next),
                pl.BlockSpec((1, 2, D_RG), h_idx),
                pl.BlockSpec((D_RG, 4 * D_RG), const), pl.BlockSpec((1, 4 * D_RG), const),
                pl.BlockSpec((2, D_RG), const), pl.BlockSpec((4, D_RG), const), pl.BlockSpec((1, D_RG), const)]
    out_specs = [pl.BlockSpec((tb, D_RG), fwd), pl.BlockSpec((tb, D_RG), bwd)]
    out_shape = [jax.ShapeDtypeStruct((n, D_RG), F32)] * 2
    if emit_state:
        out_specs.append(pl.BlockSpec((1, 2, D_RG), lambda bi, i: (bi, 0, 0)))
        out_shape.append(jax.ShapeDtypeStruct((b, 2, D_RG), F32))
    return dict(
        kernel=functools.partial(_rglru_kernel, tb, nb, emit_state), in_specs=in_specs, out_specs=out_specs,
        out_shape=out_shape,
        scratch=[pltpu.VMEM((2, D_RG), F32)]
        + [pltpu.VMEM((D_RG // LANES, tb + SUBLANES * RG_SEG_PAD, LANES), F32)] * 8,
        args=(xr, xr, xr, xr, xr, xr, h0, wbd, bias, lam, cw, cb))


def _scans_kernel(ml, rg, *refs):
    def split(plan, ins, outs, scr):
        n_in, n_out, n_sc = len(plan['in_specs']), len(plan['out_specs']), len(plan['scratch'])
        return ins[:n_in], ins[n_in:], outs[:n_out], outs[n_out:], scr[:n_sc], scr[n_sc:]

    n_in = len(ml['in_specs']) + len(rg['in_specs'])
    n_out = len(ml['out_specs']) + len(rg['out_specs'])
    ml_in, ins, ml_out, outs, ml_sc, scr = split(ml, refs[:n_in], refs[n_in:n_in + n_out], refs[n_in + n_out:])
    rg_in, _, rg_out, _, rg_sc, _ = split(rg, ins, outs, scr)
    for parts in (("init",), ("main",), ("final",)):
        ml['kernel'](*ml_in, *ml_out, *ml_sc, parts=parts)
        rg['kernel'](*rg_in, *rg_out, *rg_sc, parts=parts)


def _scans(ml, rg, b, nsteps):
    static = lambda plan: {k: v for k, v in plan.items() if k != 'args'}
    outs = pl.pallas_call(
        functools.partial(_scans_kernel, static(ml), static(rg)),
        grid=(b, nsteps),
        in_specs=ml['in_specs'] + rg['in_specs'],
        out_specs=ml['out_specs'] + rg['out_specs'],
        out_shape=ml['out_shape'] + rg['out_shape'],
        scratch_shapes=ml['scratch'] + rg['scratch'],
        name="scans",
        compiler_params=_params("arbitrary", "arbitrary"),
    )(*ml['args'], *rg['args'])
    n_ml = len(ml['out_specs'])
    return outs[:n_ml], outs[n_ml:]


def _route(s, sb):
    tm = s.shape[1]
    neg = -jnp.inf
    sub = lax.broadcasted_iota(I32, (GROUP, tm), 0)
    blocks = [sb[gi * GROUP:(gi + 1) * GROUP, :] for gi in range(N_GROUPS)]
    gscore = []
    for blk in blocks:
        m1 = jnp.max(blk, axis=0, keepdims=True)
        first = jnp.min(jnp.where(blk == m1, sub, GROUP), axis=0, keepdims=True)
        m2 = jnp.max(jnp.where(sub == first, neg, blk), axis=0, keepdims=True)
        gscore.append(m1 + m2)
    masked = []
    for gi in range(N_GROUPS):
        rank = jnp.zeros((1, tm), F32)
        for gj in range(N_GROUPS):
            if gj == gi:
                continue
            ahead = (gscore[gj] >= gscore[gi]) if gj < gi else (gscore[gj] > gscore[gi])
            rank = rank + jnp.where(ahead, 1.0, 0.0)
        masked.append(jnp.where(rank < TOPK_GROUPS, blocks[gi], neg))
    v = jnp.concatenate(masked, axis=0)
    eid = lax.broadcasted_iota(I32, (N_EXPERTS, tm), 0)
    sel = jnp.zeros((N_EXPERTS, tm), F32)
    picks = []
    for _ in range(TOP_K):
        mx = jnp.max(v, axis=0, keepdims=True)
        idx = jnp.min(jnp.where(v == mx, eid, N_EXPERTS), axis=0, keepdims=True)
        pick = eid == idx
        picks.append(pick)
        sel = jnp.where(pick, 1.0, sel)
        v = jnp.where(pick, neg, v)
    ws = s * sel
    return ws * (ROUTED_SCALE / jnp.sum(ws, axis=0, keepdims=True)), sel, picks


def _mix_out_kernel(hmf_ref, hmb_ref, ot_ref, hrf_ref, hrb_ref, gr_ref, x_ref, rt_ref, ct_ref, mod_ref,
                    mln_ref, rgn_ref, wo_ml_ref, wo_rg_ref, n2_ref, rwt_ref, rb_ref, cnt0_ref,
                    x1_ref, hn2p_ref, ek_ref, pk_ref, wtok_ref, cnt_ref, cnt_sc, earlier_sc):
    i = pl.program_id(0)
    tm = x_ref.shape[0]

    @pl.when(i == 0)
    def _():
        cnt_sc[...] = cnt0_ref[...].astype(F32)
        earlier = (lax.broadcasted_iota(I32, (tm, tm), 0) < lax.broadcasted_iota(I32, (tm, tm), 1))
        earlier_sc[...] = earlier.astype(BF16)

    hm = hmf_ref[...] + hmb_ref[...]
    heads = []
    for h in range(N_HEADS):
        seg = hm[h * HEAD:(h + 1) * HEAD, :]
        heads.append(seg * lax.rsqrt(jnp.mean(seg * seg, axis=0, keepdims=True) + EPS))
    y_ml_t = jnp.concatenate(heads, axis=0) * mln_ref[...] * jax.nn.sigmoid(ot_ref[...])
    y_rg = _rms(hrf_ref[...] + hrb_ref[...], rgn_ref[...]) * jax.nn.gelu(gr_ref[...])
    mix = (lax.dot_general(y_ml_t.astype(BF16), wo_ml_ref[...], (((0,), (0,)), ((), ())),
                           preferred_element_type=F32)
           + jnp.dot(y_rg.astype(BF16), wo_rg_ref[...], preferred_element_type=F32))
    x1 = x_ref[...] + _pos_tile(rt_ref, ct_ref) + mod_ref[0, 2:3, :] * mix
    x1_ref[...] = x1
    hn2 = _rms(x1, n2_ref[...]) * (1.0 + mod_ref[0, 4:5, :]) + mod_ref[0, 3:4, :]
    hn2p_ref[...] = _pack_bf16_pairs(hn2)
    hb = hn2.astype(BF16)
    h_lo = (hn2 - hb.astype(F32)).astype(BF16)
    nt = (((1,), (1,)), ((), ()))
    two = lax.dot_general(rwt_ref[...], hb, nt, preferred_element_type=F32)
    logits_t = (two[0:N_EXPERTS, :] + two[N_EXPERTS:2 * N_EXPERTS, :]
                + lax.dot_general(rwt_ref[0:N_EXPERTS, :], h_lo, nt, preferred_element_type=F32))
    s = jax.nn.sigmoid(logits_t)
    wt, sel, picks = _route(s, s + rb_ref[...])

    prefix = jnp.dot(sel.astype(BF16), earlier_sc[...], preferred_element_type=F32)
    pos_all = cnt_sc[:, 0:1] + prefix
    eid = lax.broadcasted_iota(I32, (N_EXPERTS, tm), 0)
    eid_f = eid.astype(F32)
    row8 = lax.broadcasted_iota(I32, (TOP_K, tm), 0)
    ek = jnp.zeros((TOP_K, tm), F32)
    pk = jnp.zeros((TOP_K, tm), F32)
    wk = jnp.zeros((N_EXPERTS, tm), F32)
    for k, pick in enumerate(picks):
        take = lambda a: jnp.sum(jnp.where(pick, a, 0.0), axis=0, keepdims=True)
        ek = jnp.where(row8 == k, take(eid_f), ek)
        pk = jnp.where(row8 == k, take(pos_all), pk)
        wk = jnp.where(eid == k, take(wt), wk)
    ek_ref[...] = ek.astype(I32)
    pk_ref[...] = pk.astype(I32)
    wtok_ref[...] = wk.T
    cnt_sc[...] += jnp.broadcast_to(jnp.sum(sel, axis=1, keepdims=True), cnt_sc.shape)

    @pl.when(i == pl.num_programs(0) - 1)
    def _():
        cnt_ref[...] = cnt_sc[...].astype(I32)


def _mix_out(hmf, hmb, ot, hrf, hrb, gr, x2d, rtab, ctab, mod, t, tm, mln, rgn, wo_ml, wo_rg, norm2, rwt, rbias,
             cnt0):
    n = x2d.shape[0]
    tok = lambda i: (i, 0)
    tok_t = lambda i: (0, i)
    const = lambda i: (0, 0)
    return pl.pallas_call(
        _mix_out_kernel,
        grid=(n // tm,),
        in_specs=[pl.BlockSpec((D_ML, tm), tok_t)] * 3 + [pl.BlockSpec((tm, D_RG), tok)] * 3 + [
            pl.BlockSpec((tm, D_MODEL), tok)] + _pos_specs(rtab, ctab, tm) + [
            pl.BlockSpec((1, N_MOD, D_MODEL), _mod_index(mod.shape[0], tm, t)),
            pl.BlockSpec((D_ML, 1), const), pl.BlockSpec((1, D_RG), const),
            pl.BlockSpec((D_ML, D_MODEL), const), pl.BlockSpec((D_RG, D_MODEL), const),
            pl.BlockSpec((1, D_MODEL), const),
            pl.BlockSpec((2 * N_EXPERTS, D_MODEL), const), pl.BlockSpec((N_EXPERTS, 1), const),
            pl.BlockSpec((N_EXPERTS, LANES), const),
        ],
        out_specs=[pl.BlockSpec((tm, D_MODEL), tok), pl.BlockSpec((tm, D_PACK), tok),
                   pl.BlockSpec((TOP_K, tm), tok_t), pl.BlockSpec((TOP_K, tm), tok_t),
                   pl.BlockSpec((tm, N_EXPERTS), tok), pl.BlockSpec((N_EXPERTS, LANES), const)],
        out_shape=[jax.ShapeDtypeStruct((n, D_MODEL), F32), jax.ShapeDtypeStruct((n, D_PACK), I32),
                   jax.ShapeDtypeStruct((TOP_K, n), I32), jax.ShapeDtypeStruct((TOP_K, n), I32),
                   jax.ShapeDtypeStruct((n, N_EXPERTS), F32), jax.ShapeDtypeStruct((N_EXPERTS, LANES), I32)],
        scratch_shapes=[pltpu.VMEM((N_EXPERTS, LANES), F32), pltpu.VMEM((tm, tm), BF16)],
        name="mix_out",
        compiler_params=_params("arbitrary"),
    )(hmf, hmb, ot, hrf, hrb, gr, x2d, rtab, ctab, mod, mln, rgn, wo_ml, wo_rg, norm2, rwt, rbias, cnt0)


def _sc_mesh():
    return plsc.VectorSubcoreMesh(core_axis_name="core", subcore_axis_name="subcore")


def _sc_worker():
    info = plsc.get_sparse_core_info()
    return lax.axis_index("subcore") * info.num_cores + lax.axis_index("core"), info.num_cores * info.num_subcores


def _sc_dispatch(xp, dest3, n_slots):
    n, w = xp.shape
    nwin = n // SC_WINDOW

    @pl.kernel(out_type=jax.ShapeDtypeStruct((n_slots, w), xp.dtype), mesh=_sc_mesh(),
               scratch_types=[pltpu.VMEM((SC_WINDOW, w), xp.dtype), pltpu.VMEM((TOP_K, SC_WINDOW), I32)],
               name="sc_dispatch")
    def k(x_hbm, i_hbm, o_hbm, x_v, i_v):
        wid, nworkers = _sc_worker()
        per = nwin // nworkers

        @pl.loop(0, per)
        def _(s):
            win = wid * per + s
            pltpu.sync_copy(x_hbm.at[pl.ds(win * SC_WINDOW, SC_WINDOW)], x_v)
            pltpu.sync_copy(i_hbm.at[win], i_v)
            for j in range(TOP_K):
                pltpu.sync_copy(x_v, o_hbm.at[i_v.at[j]])

    return k(xp, dest3)


def _sc_pack_rows(w):
    r, c = w.shape
    half = c // 2
    lanes = plsc.get_sparse_core_info().num_lanes
    chunk = SC_PACK_WORDS // c
    nchunks = r // chunk

    def rne_high(x):
        u = plsc.bitcast(x, I32)
        u = u + 0x7FFF + (lax.shift_right_logical(u, jnp.full(u.shape, 16, I32)) & 1)
        return u & jnp.int32(-65536)

    @pl.kernel(out_type=jax.ShapeDtypeStruct((r, half), I32), mesh=_sc_mesh(),
               scratch_types=[pltpu.VMEM((chunk, c), F32), pltpu.VMEM((chunk, half), I32)],
               compiler_params=pltpu.CompilerParams(needs_layout_passes=False), name="sc_pack")
    def k(w_hbm, o_hbm, in_v, out_v):
        wid, nworkers = _sc_worker()
        per = nchunks // nworkers

        @pl.loop(0, per)
        def _(s):
            first = (wid * per + s) * chunk
            pltpu.sync_copy(w_hbm.at[pl.ds(first, chunk)], in_v)

            @pl.loop(0, chunk)
            def _(row):
                for j in range(half // lanes):
                    at = lambda off: (row, pl.ds(off + j * lanes, lanes))
                    hi = rne_high(in_v.at[*at(0)][...])
                    lo = rne_high(in_v.at[*at(half)][...])
                    out_v.at[*at(0)][...] = hi | lax.shift_right_logical(lo, jnp.full(lo.shape, 16, I32))

            pltpu.sync_copy(out_v, o_hbm.at[pl.ds(first, chunk)])

    return k(w)


def _sc_combine_gather(ys, dest3):
    nwin = dest3.shape[0]
    w = ys.shape[1]

    @pl.kernel(out_type=jax.ShapeDtypeStruct((nwin, TOP_K, SC_WINDOW, w), ys.dtype), mesh=_sc_mesh(),
               scratch_types=[pltpu.VMEM((SC_WINDOW, w), ys.dtype), pltpu.VMEM((TOP_K, SC_WINDOW), I32)],
               name="sc_combine")
    def k(y_hbm, i_hbm, o_hbm, y_v, i_v):
        wid, nworkers = _sc_worker()
        per = nwin // nworkers

        @pl.loop(0, per)
        def _(s):
            win = wid * per + s
            pltpu.sync_copy(i_hbm.at[win], i_v)
            for j in range(TOP_K):
                pltpu.sync_copy(y_hbm.at[i_v.at[j]], y_v)
                pltpu.sync_copy(y_v, o_hbm.at[win, j])

    return k(ys, dest3)


def _swiglu(x, w13):
    h = jnp.dot(x, w13, preferred_element_type=F32)
    return _silu(h[:, 0:D_EXPERT]) * h[:, D_EXPERT:2 * D_EXPERT]


def _unpack_rows_bf16(p):
    hi, lo = _unpack_bf16_pairs(p)
    return jnp.concatenate([hi.astype(BF16), lo.astype(BF16)], axis=1)


def _expert_kernel(rows, rank_ref, expert_ref, nu_ref, nr_ref, x_ref, w1_hbm, w3_hbm, w2_hbm, y_ref,
                   w1buf, w3buf, w2buf, sem, progress):
    b = pl.program_id(0)
    g = EXPERT_BLOCKS_PER_STEP
    n_ranks = nr_ref[0]

    @pl.when(b == 0)
    def _():
        progress[0] = 0
        progress[1] = 0

    def copies(r):
        slot = lax.rem(r, EXPERT_RING)
        e = expert_ref[r]
        return (pltpu.make_async_copy(w1_hbm.at[e], w1buf.at[slot], sem.at[slot, 0]),
                pltpu.make_async_copy(w3_hbm.at[e], w3buf.at[slot], sem.at[slot, 1]),
                pltpu.make_async_copy(w2_hbm.at[e], w2buf.at[slot], sem.at[slot, 2]))

    def start(r, carry):
        for cp in copies(r):
            cp.start()
        return carry

    def wait(r, carry):
        for cp in copies(r):
            cp.wait()
        return carry

    first = rank_ref[b * g]
    last = rank_ref[b * g + g - 1]
    started = jnp.minimum(first + EXPERT_RING, n_ranks)
    lax.fori_loop(progress[0], started, start, 0)
    progress[0] = jnp.maximum(progress[0], started)
    needed = jnp.where(b == pl.num_programs(0) - 1, progress[0], jnp.minimum(last + 1, n_ranks))
    lax.fori_loop(progress[1], needed, wait, 0)
    progress[1] = jnp.maximum(progress[1], needed)

    @pl.when(b * g < nu_ref[0])
    def _():
        for j in range(g):
            slot = lax.rem(rank_ref[b * g + j], EXPERT_RING)
            sl = slice(j * rows, (j + 1) * rows)
            x = _unpack_rows_bf16(x_ref[sl, :])
            h = (_silu(jnp.dot(x, _unpack_rows_bf16(w1buf[slot]), preferred_element_type=F32))
                 * jnp.dot(x, _unpack_rows_bf16(w3buf[slot]), preferred_element_type=F32))
            y_ref[sl, :] = _pack_bf16_pairs(
                jnp.dot(h.astype(BF16), _unpack_rows_bf16(w2buf[slot]), preferred_element_type=F32))


def _experts(xs, block_rank, rank_expert, n_used, n_ranks, w1, w3, w2, rows):
    g = EXPERT_BLOCKS_PER_STEP
    nb = xs.shape[0] // rows
    tok = lambda b, *_: (b, 0)
    return pl.pallas_call(
        functools.partial(_expert_kernel, rows),
        grid_spec=pltpu.PrefetchScalarGridSpec(
            num_scalar_prefetch=4,
            grid=(nb // g,),
            in_specs=[pl.BlockSpec((g * rows, D_PACK), tok)] + [pl.BlockSpec(memory_space=pl.ANY)] * 3,
            out_specs=pl.BlockSpec((g * rows, D_PACK), tok),
            scratch_shapes=[pltpu.VMEM((EXPERT_RING, D_MODEL, D_EXPERT // 2), I32),
                            pltpu.VMEM((EXPERT_RING, D_MODEL, D_EXPERT // 2), I32),
                            pltpu.VMEM((EXPERT_RING, D_EXPERT, D_PACK), I32),
                            pltpu.SemaphoreType.DMA((EXPERT_RING, 3)), pltpu.SMEM((2,), I32)],
        ),
        out_shape=jax.ShapeDtypeStruct(xs.shape, I32),
        name="experts",
        compiler_params=_params("arbitrary", vmem=VMEM_LIMIT_EXPERTS),
    )(block_rank, rank_expert, n_used, n_ranks, xs, w1, w3, w2)


def _moe_out_kernel(win0, yk_hbm, hn2p_ref, wtok_ref, sw13_ref, sw2_ref, x1_ref, mod_ref, nf_ref, y_ref,
                    ykbuf, sem):
    i = pl.program_id(0)
    nsteps = pl.num_programs(0)
    nwin = ykbuf.shape[1]

    def fetch(step):
        slot = lax.rem(step, MOE_OUT_BUFFERS)
        return pltpu.make_async_copy(yk_hbm.at[pl.ds(win0 + step * nwin, nwin)], ykbuf.at[slot], sem.at[slot])

    @pl.when(i == 0)
    def _():
        for step in range(MOE_OUT_BUFFERS - 1):
            fetch(step).start()

    @pl.when(i + MOE_OUT_BUFFERS - 1 < nsteps)
    def _():
        fetch(i + MOE_OUT_BUFFERS - 1).start()

    fetch(i).wait()

    def body(slot):
        shared = jnp.dot(_swiglu(_unpack_rows_bf16(hn2p_ref[...]), sw13_ref[...]).astype(BF16), sw2_ref[...],
                         preferred_element_type=F32)
        w = wtok_ref[...]
        parts = []
        for wi in range(nwin):
            rows = slice(wi * SC_WINDOW, (wi + 1) * SC_WINDOW)
            a_hi = shared[rows, 0:D_PACK]
            a_lo = shared[rows, D_PACK:D_MODEL]
            for k in range(TOP_K):
                y_hi, y_lo = _unpack_bf16_pairs(ykbuf[slot, wi, k])
                wc = w[rows, k:k + 1]
                a_hi = a_hi + wc * y_hi
                a_lo = a_lo + wc * y_lo
            parts.append(jnp.concatenate([a_hi, a_lo], axis=1))
        x2 = x1_ref[...] + mod_ref[0, 5:6, :] * jnp.concatenate(parts, axis=0)
        y_ref[...] = _rms(x2, nf_ref[...])

    for s in range(MOE_OUT_BUFFERS):
        pl.when(lax.rem(i, MOE_OUT_BUFFERS) == s)(functools.partial(body, s))


def _moe_out(yk, tok0, hn2p, wtok, sw13, sw2, x1, mod, t, tm, norm_final):
    n = hn2p.shape[0]
    tok = lambda i: (i, 0)
    const = lambda i: (0, 0)
    nwin = tm // SC_WINDOW
    assert n // tm >= MOE_OUT_BUFFERS - 1
    return pl.pallas_call(
        functools.partial(_moe_out_kernel, tok0 // SC_WINDOW),
        grid=(n // tm,),
        in_specs=[
            pl.BlockSpec(memory_space=pl.ANY),
            pl.BlockSpec((tm, D_PACK), tok),
            pl.BlockSpec((tm, N_EXPERTS), tok),
            pl.BlockSpec((D_MODEL, 2 * D_EXPERT), const),
            pl.BlockSpec((D_EXPERT, D_MODEL), const),
            pl.BlockSpec((tm, D_MODEL), tok),
            pl.BlockSpec((1, N_MOD, D_MODEL), _mod_index(mod.shape[0], tm, t)),
            pl.BlockSpec((1, D_MODEL), const),
        ],
        out_specs=pl.BlockSpec((tm, D_MODEL), tok),
        out_shape=jax.ShapeDtypeStruct((n, D_MODEL), F32),
        scratch_shapes=[pltpu.VMEM((MOE_OUT_BUFFERS, nwin, TOP_K, SC_WINDOW, D_PACK), I32),
                        pltpu.SemaphoreType.DMA((MOE_OUT_BUFFERS,))],
        name="moe_out",
        compiler_params=_params("arbitrary"),
    )(yk, hn2p, wtok, sw13, sw2, x1, mod, norm_final)


def _dispatch_plan(cnt, ek, pk, rows):
    n = ek.shape[1]
    nb = n * TOP_K // rows + N_EXPERTS
    nblk = (cnt + rows - 1) // rows
    block_end = jnp.cumsum(nblk)
    experts = jnp.arange(N_EXPERTS, dtype=I32)
    first_row = jnp.sum(jnp.where(ek[:, :, None] == experts, (block_end - nblk) * rows, 0), axis=-1)
    dest3 = (first_row + pk).reshape(TOP_K, n // SC_WINDOW, SC_WINDOW).transpose(1, 0, 2)
    owns = nblk > 0
    n_ranks = jnp.sum(owns.astype(I32))
    blocks = jnp.arange(nb, dtype=I32)[:, None]
    block_rank = jnp.minimum(jnp.sum((owns & (block_end <= blocks)).astype(I32), axis=1), n_ranks - 1)
    rank_expert = jnp.minimum(jnp.sum((jnp.cumsum(owns.astype(I32)) <= experts[:, None]).astype(I32), axis=1),
                              N_EXPERTS - 1)
    return (dest3, block_rank.astype(I32), rank_expert.astype(I32), block_end[-1:].astype(I32),
            n_ranks.reshape(1).astype(I32), nb * rows)


def _grid_pos_tables(n_tokens, dim):
    quarter = dim // 4
    omega = 1.0 / (POS_BASE ** (jnp.arange(quarter, dtype=F32) / quarter))
    ra = jnp.arange(n_tokens // GRID_W).astype(F32)[:, None] * omega
    ca = jnp.arange(GRID_W).astype(F32)[:, None] * omega
    return (jnp.concatenate([jnp.sin(ra), jnp.cos(ra)], axis=-1),
            jnp.concatenate([jnp.sin(ca), jnp.cos(ca)], axis=-1))


def _hi_lo_rows(w):
    hi = w.astype(BF16)
    return jnp.concatenate([hi, (w - hi.astype(F32)).astype(BF16)], axis=0)


def _block_diag(w):
    eye = jnp.eye(N_RG_BLOCKS, dtype=w.dtype)
    return jnp.einsum('nij,nm->nimj', w, eye).reshape(D_RG, D_RG)


def _layer_weights(l, norm1, w_in, mlstm_gate_bias, mlstm_norm, rg_conv_w, rg_conv_b, rg_wa, rg_ba, rg_wx,
                   rg_bx, rg_lambda, rg_norm, w_out, norm2, router_w, router_bias, exp_w1, exp_w3, exp_w2,
                   shared_w1, shared_w3, shared_w2):
    wi = w_in[l]
    c0, c1 = 4 * D_ML, 4 * D_ML + N_GATE
    wi16 = wi.astype(BF16)
    wg = wi[:, c0:c1]
    wg_hi = wi16[:, c0:c1]
    wg_lo = (wg - wg_hi.astype(F32)).astype(BF16)
    zcols = lambda w: jnp.zeros((D_MODEL, w), BF16)
    return dict(
        norm1=norm1[l].reshape(1, D_MODEL),
        wq=wi16[:, :c0],
        wr=jnp.concatenate([wi16[:, c1:], wg_hi, wg_lo, zcols(LANES - 2 * N_GATE)], axis=1),
        wgh=jnp.concatenate([wg_hi, zcols(LANES - N_GATE)], axis=1),
        gbias=jnp.pad(mlstm_gate_bias[l].reshape(1, N_GATE), ((0, 0), (0, LANES - N_GATE))),
        mln=mlstm_norm[l].reshape(D_ML, 1),
        cw=rg_conv_w[l], cb=rg_conv_b[l].reshape(1, D_RG),
        wbd=jnp.concatenate([_block_diag(rg_wa[l, 0]), _block_diag(rg_wx[l, 0]),
                             _block_diag(rg_wa[l, 1]), _block_diag(rg_wx[l, 1])], axis=1).astype(BF16),
        rbias=jnp.concatenate([rg_ba[l, 0], rg_bx[l, 0], rg_ba[l, 1], rg_bx[l, 1]]).reshape(1, 4 * D_RG),
        lam=rg_lambda[l], rgn=rg_norm[l].reshape(1, D_RG),
        wo_ml=w_out[l, :D_ML].astype(BF16), wo_rg=w_out[l, D_ML:].astype(BF16),
        norm2=norm2[l].reshape(1, D_MODEL),
        rwt=_hi_lo_rows(router_w[l].T), rb=router_bias[l].reshape(N_EXPERTS, 1),
        w1=_sc_pack_rows(exp_w1[l].reshape(N_EXPERTS * D_MODEL, D_EXPERT)).reshape(N_EXPERTS, D_MODEL, -1),
        w3=_sc_pack_rows(exp_w3[l].reshape(N_EXPERTS * D_MODEL, D_EXPERT)).reshape(N_EXPERTS, D_MODEL, -1),
        w2=_sc_pack_rows(exp_w2[l].reshape(N_EXPERTS * D_EXPERT, D_MODEL)).reshape(N_EXPERTS, D_EXPERT, -1),
        sw13=jnp.concatenate([shared_w1[l], shared_w3[l]], axis=-1).astype(BF16),
        sw2=shared_w2[l].astype(BF16),
    )


def _mixers(x2d, pos_tables, mod, c0, n0, m0, h0, cnt0, lw, b, t, emit_state):
    tl = _tiles(t, mod.shape[0] > 1)
    tm = tl['tok']
    rtab, ctab = pos_tables
    qt, k, kt, v, vt, ot, xr, gr, g, gt = _in_proj(x2d, rtab, ctab, mod, t, tm, lw['norm1'], lw['wq'], lw['wr'],
                                                   lw['wgh'], lw['gbias'])
    chunk = tl['chunk']
    ml, rg = _scans(
        _mlstm_plan(qt, k, kt, v, vt, g, gt, c0, n0, m0, b, t, chunk, emit_state),
        _rglru_plan(xr, h0, lw['wbd'], lw['rbias'], lw['lam'], lw['cw'], lw['cb'], b, t, chunk, emit_state),
        b, t // chunk)
    routed = _mix_out(ml[0], ml[1], ot, rg[0], rg[1], gr, x2d, rtab, ctab, mod, t, tm, lw['mln'], lw['rgn'],
                      lw['wo_ml'], lw['wo_rg'], lw['norm2'], lw['rwt'], lw['rb'], cnt0)
    return routed, ml[2:], rg[2:]


def _routed_experts(paths, lw):
    cnt = paths[-1][5][:, 0]
    hn2p = jnp.concatenate([p[1] for p in paths], axis=0)
    ek = jnp.concatenate([p[2] for p in paths], axis=1)
    pk = jnp.concatenate([p[3] for p in paths], axis=1)
    dest3, block_rank, rank_expert, n_used, n_ranks, n_slots = _dispatch_plan(cnt, ek, pk, EXPERT_ROWS)
    xs = _sc_dispatch(hn2p, dest3, n_slots)
    ys = _experts(xs, block_rank, rank_expert, n_used, n_ranks, lw['w1'], lw['w3'], lw['w2'], EXPERT_ROWS)
    return _sc_combine_gather(ys, dest3)


def kernel(x_prompt, x_sample, c, state_mlstm_C, state_mlstm_n, state_mlstm_m, state_rglru_h, c_ctx, w_ada, b_ada, norm1, w_in, mlstm_gate_bias, mlstm_norm, rg_conv_w, rg_conv_b, rg_wa, rg_ba, rg_wx, rg_bx, rg_lambda, rg_norm, w_out, norm2, router_w, router_bias, exp_w1, exp_w3, exp_w2, shared_w1, shared_w3, shared_w2, norm_final):
    bp, tp, _ = x_prompt.shape
    bs, ts, _ = x_sample.shape
    depth = w_ada.shape[0]
    assert depth == 1, "the final norm is fused into the single layer's MoE output kernel"
    nd = 2 * N_HEADS
    l = 0
    lw = _layer_weights(l, norm1, w_in, mlstm_gate_bias, mlstm_norm, rg_conv_w, rg_conv_b, rg_wa, rg_ba, rg_wx,
                        rg_bx, rg_lambda, rg_norm, w_out, norm2, router_w, router_bias, exp_w1, exp_w3, exp_w2,
                        shared_w1, shared_w3, shared_w2)
    nf = norm_final.reshape(1, D_MODEL)
    cvecs = jnp.concatenate([c_ctx[None], c, jnp.zeros((SUBLANES - 1 - bs, D_MODEL), F32)], axis=0)
    mod = _ada(cvecs, w_ada[l], b_ada[l]).reshape(SUBLANES, N_MOD, D_MODEL)

    xp2d, lw['w1'], lw['w3'], lw['w2'] = lax.optimization_barrier(
        (x_prompt.reshape(bp * tp, D_MODEL), lw['w1'], lw['w3'], lw['w2']))

    mod_p, mod_s = mod[0:1], mod[1:1 + bs]
    tm_p, tm_s = _tiles(tp, False)['tok'], _tiles(ts, True)['tok']
    rp, (cc, nc_, mc), (hc,) = _mixers(
        xp2d,
        (jnp.zeros((tm_p // GRID_W, D_MODEL // 2), F32), jnp.zeros((GRID_W, D_MODEL // 2), F32)), mod_p,
        jnp.zeros((1, nd, HEAD, HEAD), F32), jnp.zeros((1, nd, HEAD), F32), jnp.zeros((1, nd, HEAD), F32),
        jnp.zeros((1, 2, D_RG), F32), jnp.zeros((N_EXPERTS, LANES), I32), lw, bp, tp, True)
    rs, _, _ = _mixers(
        x_sample.reshape(bs * ts, D_MODEL), _grid_pos_tables(ts, D_MODEL), mod_s,
        state_mlstm_C[:, l].reshape(bs, nd, HEAD, HEAD), state_mlstm_n[:, l].reshape(bs, nd, HEAD),
        jnp.broadcast_to(state_mlstm_m[:, l].reshape(bs, nd, 1), (bs, nd, HEAD)),
        state_rglru_h[:, l], jnp.zeros((N_EXPERTS, LANES), I32), lw, bs, ts, False)
    yp = _moe_out(_routed_experts([rp], lw), 0, rp[1], rp[4], lw['sw13'], lw['sw2'], rp[0], mod_p, tp, tm_p, nf)
    ys = _moe_out(_routed_experts([rs], lw), 0, rs[1], rs[4], lw['sw13'], lw['sw2'], rs[0], mod_s, ts, tm_s, nf)

    y_prompt = yp.reshape(bp, tp, D_MODEL)
    y_sample = ys.reshape(bs, ts, D_MODEL)
    new_c = cc.reshape(bp, 1, 2, N_HEADS, HEAD, HEAD)
    new_n = nc_.reshape(bp, 1, 2, N_HEADS, HEAD)
    new_m = mc[:, :, 0].reshape(bp, 1, 2, N_HEADS)
    new_h = hc.reshape(bp, 1, 2, D_RG)
    return (y_prompt, y_sample, new_c, new_n, new_m, new_h)
```

```python
import functools

import jax
import jax.numpy as jnp
from jax import lax
from jax.experimental import pallas as pl
from jax.experimental.pallas import tpu as pltpu
from jax.experimental.pallas import tpu_sc as plsc

F32 = jnp.float32
BF16 = jnp.bfloat16
I32 = jnp.int32
HIGHEST = lax.Precision.HIGHEST

D_MODEL = 1024
N_MOD = 6
D_ML = 512
N_HEADS = 4
HEAD = 128
D_RG = 512
N_RG_BLOCKS = 8
RG_BLOCK = 64
RG_C = 8.0
N_GATE = 16
N_EXPERTS = 64
N_GROUPS = 8
GROUP = 8
TOPK_GROUPS = 4
TOP_K = 8
D_EXPERT = 256
ROUTED_SCALE = 2.5
EPS = 1e-6
GRID_W = 64
POS_BASE = 10000.0

RG_SEG_PAD = 8
SC_WINDOW = 128
SC_PACK_WORDS = 65536
D_PACK = D_MODEL // 2
MOE_OUT_BUFFERS = 3
EXPERT_ROWS = 512
EXPERT_BLOCKS_PER_STEP = 8
EXPERT_RING = 10

SUBLANES = 8
LANES = 128
VMEM_LIMIT = 48 * 1024 * 1024
VMEM_LIMIT_EXPERTS = 58 * 1024 * 1024


def _params(*sem, vmem=VMEM_LIMIT):
    return pltpu.CompilerParams(dimension_semantics=sem, vmem_limit_bytes=vmem)


def _tiles(t, per_sequence_mod):
    cap = t if per_sequence_mod else 1 << 30
    return dict(
        tok=min(512, cap),
        chunk=min(256, t),
    )


def _silu(x):
    return x * jax.nn.sigmoid(x)


def _softplus(x):
    return jnp.maximum(x, 0.0) + jnp.log1p(jnp.exp(-jnp.abs(x)))


def _rms(x, g):
    return x * lax.rsqrt(jnp.mean(x * x, axis=-1, keepdims=True) + EPS) * g


def _bf16_pieces(x):
    hi = x.astype(BF16)
    r = x - hi.astype(F32)
    mid = r.astype(BF16)
    return hi, mid, (r - mid.astype(F32)).astype(BF16)


def _pack_bf16_pairs(x):
    w = x.shape[1] // 2
    hi = lax.bitcast_convert_type(x[:, :w].astype(BF16).astype(F32), I32)
    lo = lax.bitcast_convert_type(x[:, w:].astype(BF16).astype(F32), I32)
    return hi | lax.shift_right_logical(lo, jnp.full(lo.shape, 16, I32))


def _unpack_bf16_pairs(p):
    hi = lax.bitcast_convert_type(p & jnp.int32(-65536), F32)
    lo = lax.bitcast_convert_type(lax.shift_left(p, jnp.full(p.shape, 16, I32)), F32)
    return hi, lo


def _ada_kernel(c_ref, w_ref, b_ref, o_ref):
    s_hi, s_mid, s_lo = _bf16_pieces(_silu(c_ref[...]))
    w = w_ref[...]
    w_hi = w.astype(BF16)
    w_lo = (w - w_hi.astype(F32)).astype(BF16)
    dot = lambda a, b: jnp.dot(a, b, preferred_element_type=F32)
    o_ref[...] = dot(s_hi, w_hi) + dot(s_hi, w_lo) + dot(s_mid, w_hi) + dot(s_lo, w_hi) + b_ref[...]


def _ada(cvecs, w_ada, b_ada):
    n_out = w_ada.shape[1]
    tn = 1536
    return pl.pallas_call(
        _ada_kernel,
        grid=(n_out // tn,),
        in_specs=[
            pl.BlockSpec((SUBLANES, D_MODEL), lambda j: (0, 0)),
            pl.BlockSpec((D_MODEL, tn), lambda j: (0, j)),
            pl.BlockSpec((1, tn), lambda j: (0, j)),
        ],
        out_specs=pl.BlockSpec((SUBLANES, tn), lambda j: (0, j)),
        out_shape=jax.ShapeDtypeStruct((SUBLANES, n_out), F32),
        name="ada",
        compiler_params=_params("arbitrary"),
    )(cvecs, w_ada, b_ada.reshape(1, n_out))


def _pos_tile(rt_ref, ct_ref):
    left = jnp.concatenate([jnp.broadcast_to(rt_ref[r:r + 1, :], (GRID_W, rt_ref.shape[1]))
                            for r in range(rt_ref.shape[0])], axis=0)
    right = jnp.concatenate([ct_ref[...]] * rt_ref.shape[0], axis=0)
    return jnp.concatenate([left, right], axis=1)


def _in_proj_kernel(x_ref, rt_ref, ct_ref, mod_ref, n1_ref, wq_ref, wr_ref, wgh_ref, gb_ref,
                    qt_ref, k_ref, kt_ref, v_ref, vt_ref, ot_ref, xr_ref, gr_ref, g_ref, gt_ref):
    x = x_ref[...] + _pos_tile(rt_ref, ct_ref)
    hn = _rms(x, n1_ref[...]) * (1.0 + mod_ref[0, 1:2, :]) + mod_ref[0, 0:1, :]
    hb = hn.astype(BF16)
    z = jnp.dot(hb, wq_ref[...], preferred_element_type=F32)
    k = z[:, D_ML:2 * D_ML] * (HEAD ** -0.5)
    v = z[:, 2 * D_ML:3 * D_ML]
    qt_ref[...] = z[:, 0:D_ML].T.astype(BF16)
    k_ref[...] = k.astype(BF16)
    kt_ref[...] = k.T.astype(BF16)
    v_ref[...] = v.astype(BF16)
    vt_ref[...] = v.T.astype(BF16)
    ot_ref[...] = z[:, 3 * D_ML:4 * D_ML].T
    zr = jnp.dot(hb, wr_ref[...], preferred_element_type=F32)
    xr_ref[...] = zr[:, 0:D_RG]
    gr_ref[...] = zr[:, D_RG:2 * D_RG]
    zg = zr[:, 2 * D_RG:2 * D_RG + LANES]
    h_lo = (hn - hb.astype(F32)).astype(BF16)
    g = (zg + pltpu.roll(zg, LANES - N_GATE, 1)
         + jnp.dot(h_lo, wgh_ref[...], preferred_element_type=F32) + gb_ref[...])
    col = lax.broadcasted_iota(I32, g.shape, 1)
    g = jnp.where((col & 4) != 0, -_softplus(-g), g)
    g_ref[...] = g[:, 0:N_GATE]
    gt_ref[...] = g.T[0:N_GATE, :]


def _mod_index(bm, tm, t):
    if bm > 1:
        return lambda i: ((i * tm) // t, 0, 0)
    return lambda i: (0, 0, 0)


def _pos_specs(rtab, ctab, tm):
    rows = tm // GRID_W
    period = rtab.shape[0] // rows
    return [pl.BlockSpec((rows, D_MODEL // 2), lambda i: (i % period, 0)),
            pl.BlockSpec((GRID_W, D_MODEL // 2), lambda i: (0, 0))]


def _in_proj(x2d, rtab, ctab, mod, t, tm, norm1, wq, wr, wgh, gbias):
    n = x2d.shape[0]
    tok = lambda i: (i, 0)
    tok_t = lambda i: (0, i)
    const = lambda i: (0, 0)
    f = lambda w: jax.ShapeDtypeStruct((n, w), F32)
    row16 = jax.ShapeDtypeStruct((n, D_ML), BF16)
    col16 = jax.ShapeDtypeStruct((D_ML, n), BF16)
    return pl.pallas_call(
        _in_proj_kernel,
        grid=(n // tm,),
        in_specs=[pl.BlockSpec((tm, D_MODEL), tok)] + _pos_specs(rtab, ctab, tm) + [
            pl.BlockSpec((1, N_MOD, D_MODEL), _mod_index(mod.shape[0], tm, t)),
            pl.BlockSpec((1, D_MODEL), const),
            pl.BlockSpec((D_MODEL, 4 * D_ML), const),
            pl.BlockSpec((D_MODEL, 2 * D_RG + LANES), const),
            pl.BlockSpec((D_MODEL, LANES), const),
            pl.BlockSpec((1, LANES), const),
        ],
        out_specs=[pl.BlockSpec((D_ML, tm), tok_t), pl.BlockSpec((tm, D_ML), tok), pl.BlockSpec((D_ML, tm), tok_t),
                   pl.BlockSpec((tm, D_ML), tok), pl.BlockSpec((D_ML, tm), tok_t), pl.BlockSpec((D_ML, tm), tok_t),
                   pl.BlockSpec((tm, D_RG), tok), pl.BlockSpec((tm, D_RG), tok),
                   pl.BlockSpec((tm, N_GATE), tok), pl.BlockSpec((N_GATE, tm), tok_t)],
        out_shape=[col16, row16, col16, row16, col16, jax.ShapeDtypeStruct((D_ML, n), F32),
                   f(D_RG), f(D_RG), f(N_GATE), jax.ShapeDtypeStruct((N_GATE, n), F32)],
        name="in_proj",
        compiler_params=_params("arbitrary"),
    )(x2d, rtab, ctab, mod, norm1, wq, wr, wgh, gbias)


ALL_PARTS = ("init", "main", "final")


def _mlstm_kernel(chunk, nc, emit_state, *refs, parts=ALL_PARTS):
    (qtf, kf, ktf, vf, vtf, gf, gtf, qtb, kb, ktb, vb, vtb, gb, gtb, c0_ref, n0_ref, m0_ref) = refs[:17]
    if emit_state:
        hf_ref, hb_ref, c_out, n_out, m_out, c_sc, n_sc, m_sc = refs[17:]
    else:
        hf_ref, hb_ref, c_sc, n_sc, m_sc = refs[17:]
    i = pl.program_id(1)

    if "init" in parts:
        @pl.when(i == 0)
        def _():
            c_sc[...] = c0_ref[0]
            n_sc[...] = n0_ref[0]
            m_sc[...] = m0_ref[0]

    def write_state():
        if emit_state and "final" in parts:
            @pl.when(i == nc - 1)
            def _():
                c_out[0] = c_sc[...]
                n_out[0] = n_sc[...]
                m_out[0] = m_sc[...]

    if "main" not in parts:
        write_state()
        return

    key = lax.broadcasted_iota(I32, (chunk, chunk), 0)
    qry = lax.broadcasted_iota(I32, (chunk, chunk), 1)
    hd = []
    for d, (qt_ref, k_ref, kt_ref, v_ref, vt_ref, g_ref, gt_ref, h_ref) in enumerate(
            ((qtf, kf, ktf, vf, vtf, gf, gtf, hf_ref), (qtb, kb, ktb, vb, vtb, gb, gtb, hb_ref))):
        tri = (key <= qry) if d == 0 else (key >= qry)
        tri_t = (qry <= key) if d == 0 else (qry >= key)
        g = g_ref[...]
        gt = gt_ref[...]
        rows3 = jnp.dot(jnp.concatenate(_bf16_pieces(gt), axis=0), tri.astype(BF16), preferred_element_type=F32)
        brow = rows3[0:N_GATE] + rows3[N_GATE:2 * N_GATE] + rows3[2 * N_GATE:3 * N_GATE]
        tri_t16 = tri_t.astype(BF16)
        bcol = sum(jnp.dot(tri_t16, piece, preferred_element_type=F32) for piece in _bf16_pieces(g))
        blast = bcol[chunk - 1:chunk, :] if d == 0 else bcol[0:1, :]
        for h in range(N_HEADS):
            ci = d * 8 + h
            cf = d * 8 + 4 + h
            j = d * N_HEADS + h
            sl = slice(h * HEAD, (h + 1) * HEAD)
            hd.append(dict(
                j=j, sl=sl, tri=tri, h_ref=h_ref, qt=qt_ref[sl, :], k=k_ref[:, sl], kt=kt_ref[sl, :],
                v=v_ref[:, sl], vt=vt_ref[sl, :], b_row=brow[cf:cf + 1, :],
                gate_col=g[:, ci:ci + 1] - bcol[:, cf:cf + 1], gate_row=gt[ci:ci + 1, :] - brow[cf:cf + 1, :],
                b_last=blast[:, cf:cf + 1], m_prev=m_sc[j:j + 1, 0:1], c_prev=c_sc[j], n_prev=n_sc[j:j + 1, :]))
    for x in hd:
        top = jnp.max(jnp.where(x['tri'], x['gate_col'], -jnp.inf), axis=0, keepdims=True)
        mx = jnp.maximum(x['m_prev'], top)
        x['dm'] = jnp.exp(jnp.where(x['tri'], x['gate_col'] - mx, -jnp.inf))
        x['w_inter'] = jnp.exp(x['m_prev'] - mx)
        x['floor'] = jnp.exp(-(x['b_row'] + mx))
    for x in hd:
        x['st'] = jnp.dot(x['k'], x['qt'], preferred_element_type=F32) * x['dm']
    for x in hd:
        inter = lax.dot_general(x['c_prev'].astype(BF16), x['qt'], (((0,), (0,)), ((), ())),
                                preferred_element_type=F32)
        num = x['w_inter'] * inter + jnp.dot(x['vt'], x['st'].astype(BF16), preferred_element_type=F32)
        qn = jnp.dot(jnp.broadcast_to(x['n_prev'], (SUBLANES, HEAD)).astype(BF16), x['qt'],
                     preferred_element_type=F32)[0:1, :]
        den = x['w_inter'] * qn + jnp.sum(x['st'], axis=0, keepdims=True)
        x['h_ref'][x['sl'], :] = num / jnp.maximum(jnp.abs(den), x['floor'])
    for x in hd:
        j = x['j']
        log_w = x['b_last'] + x['gate_row']
        m_new = jnp.maximum(x['b_last'] + x['m_prev'], jnp.max(log_w, axis=1, keepdims=True))
        decay = jnp.exp(x['b_last'] + x['m_prev'] - m_new)
        w_row = jnp.exp(log_w - m_new)
        kwt = (x['kt'].astype(F32) * w_row).astype(BF16)
        c_sc[j] = decay * x['c_prev'] + jnp.dot(kwt, x['v'], preferred_element_type=F32)
        n_sc[j:j + 1, :] = decay * x['n_prev'] + jnp.dot(
            jnp.broadcast_to(w_row, (SUBLANES, chunk)).astype(BF16), x['k'], preferred_element_type=F32)[0:1, :]
        m_sc[j:j + 1, :] = jnp.broadcast_to(m_new, (1, HEAD))

    write_state()


def _state_index(bm):
    if bm > 1:
        return lambda b, i: (b,) + (0,) * 3, lambda b, i: (b, 0, 0)
    return lambda b, i: (0,) * 4, lambda b, i: (0, 0, 0)


def _mlstm_plan(qt, k, kt, v, vt, g, gt, c0, n0, m0, b, t, chunk, emit_state):
    nc = t // chunk
    n = b * t
    nd = 2 * N_HEADS
    fwd = lambda bi, i: (bi * nc + i, 0)
    bwd = lambda bi, i: (bi * nc + nc - 1 - i, 0)
    fwd_t = lambda bi, i: (0, bi * nc + i)
    bwd_t = lambda bi, i: (0, bi * nc + nc - 1 - i)
    c_idx, n_idx = _state_index(c0.shape[0])
    rows = lambda m: pl.BlockSpec((chunk, D_ML), m)
    cols = lambda m: pl.BlockSpec((D_ML, chunk), m)
    one_dir = lambda m, mt: [cols(mt), rows(m), cols(mt), rows(m), cols(mt),
                             pl.BlockSpec((chunk, N_GATE), m), pl.BlockSpec((N_GATE, chunk), mt)]
    in_specs = (one_dir(fwd, fwd_t) + one_dir(bwd, bwd_t)
                + [pl.BlockSpec((1, nd, HEAD, HEAD), c_idx),
                   pl.BlockSpec((1, nd, HEAD), n_idx), pl.BlockSpec((1, nd, HEAD), n_idx)])
    out_specs = [cols(fwd_t), cols(bwd_t)]
    out_shape = [jax.ShapeDtypeStruct((D_ML, n), F32)] * 2
    if emit_state:
        out_specs += [pl.BlockSpec((1, nd, HEAD, HEAD), lambda bi, i: (bi, 0, 0, 0)),
                      pl.BlockSpec((1, nd, HEAD), lambda bi, i: (bi, 0, 0)),
                      pl.BlockSpec((1, nd, HEAD), lambda bi, i: (bi, 0, 0))]
        out_shape += [jax.ShapeDtypeStruct((b, nd, HEAD, HEAD), F32),
                      jax.ShapeDtypeStruct((b, nd, HEAD), F32),
                      jax.ShapeDtypeStruct((b, nd, HEAD), F32)]
    return dict(
        kernel=functools.partial(_mlstm_kernel, chunk, nc, emit_state), in_specs=in_specs, out_specs=out_specs,
        out_shape=out_shape,
        scratch=[pltpu.VMEM((nd, HEAD, HEAD), F32), pltpu.VMEM((nd, HEAD), F32), pltpu.VMEM((nd, HEAD), F32)],
        args=(qt, k, kt, v, vt, g, gt, qt, k, kt, v, vt, g, gt, c0, n0, m0))


def _neg_expm1_2x(x):
    t = jnp.tanh(x)
    return -2.0 * t / (1.0 - t)


def _rglru_kernel(tb, nb, emit_state, *refs, parts=ALL_PARTS):
    (xf, xf_prev, xf_next, xb, xb_prev, xb_next, h0_ref, w_ref, bias_ref, lam_ref, cw_ref, cb_ref) = refs[:12]
    n_out = 3 if emit_state else 2
    hf_ref, hb_ref = refs[12:14]
    hfin_ref = refs[14] if emit_state else None
    carry, af_sc, uf_sc, ab_sc, ub_sc, hf_sc, pf_sc, hb_sc, pb_sc = refs[12 + n_out:]
    i = pl.program_id(1)

    if "init" in parts:
        @pl.when(i == 0)
        def _():
            carry[...] = h0_ref[0]

    def write_state():
        if emit_state and "final" in parts:
            @pl.when(i == nb - 1)
            def _():
                hfin_ref[0] = carry[...]

    if "main" not in parts:
        write_state()
        return

    row8 = lax.broadcasted_iota(I32, (SUBLANES, D_RG), 0)
    cw = cw_ref[...]
    log_a_per_r = -RG_C * _softplus(-lam_ref[...])

    def taps(xm2, xm1, x0, xp1):
        return cb_ref[...] + xm2 * cw[0:1, :] + xm1 * cw[1:2, :] + x0 * cw[2:3, :] + xp1 * cw[3:4, :]

    def conv(main_ref, prev_ref, next_ref, first, last):
        main = main_ref[...]
        prev = jnp.where(first, 0.0, prev_ref[...])
        nxt = jnp.where(last, 0.0, next_ref[...])
        body = taps(pltpu.roll(main, 2, 0), pltpu.roll(main, 1, 0), main, pltpu.roll(main, tb - 1, 0))
        e = SUBLANES
        head, tail = main[0:e, :], main[tb - e:tb, :]
        before_tail = main[tb - 2 * e:tb - e, :]
        fix_head = taps(
            jnp.where(row8 == 0, prev[6:7, :], jnp.where(row8 == 1, prev[7:8, :], pltpu.roll(head, 2, 0))),
            jnp.where(row8 == 0, prev[7:8, :], pltpu.roll(head, 1, 0)), head,
            jnp.where(row8 == e - 1, main[e:e + 1, :], pltpu.roll(head, e - 1, 0)))
        fix_tail = taps(
            jnp.where(row8 == 0, before_tail[6:7, :],
                      jnp.where(row8 == 1, before_tail[7:8, :], pltpu.roll(tail, 2, 0))),
            jnp.where(row8 == 0, before_tail[7:8, :], pltpu.roll(tail, 1, 0)), tail,
            jnp.where(row8 == e - 1, nxt[0:1, :], pltpu.roll(tail, e - 1, 0)))
        return jnp.concatenate([fix_head, body[e:tb - e, :], fix_tail], axis=0)

    def recurrence_terms(xc, d):
        z = jnp.dot(xc.astype(BF16), w_ref[:, d * 2 * D_RG:(d + 1) * 2 * D_RG],
                    preferred_element_type=F32) + bias_ref[:, d * 2 * D_RG:(d + 1) * 2 * D_RG]
        r = 0.5 * jnp.tanh(0.5 * z[:, 0:D_RG]) + 0.5
        ig = 0.5 * jnp.tanh(0.5 * z[:, D_RG:2 * D_RG]) + 0.5
        log_a = r * log_a_per_r[d:d + 1, :]
        a = jnp.exp(log_a)
        u = jnp.sqrt(_neg_expm1_2x(log_a)) * (ig * xc)
        return a, u

    a_f, u_f = recurrence_terms(conv(xf, xf_prev, xf_next, i == 0, i == nb - 1), 0)
    a_b, u_b = recurrence_terms(conv(xb, xb_prev, xb_next, i == nb - 1, i == 0), 1)
    seg = tb // SUBLANES
    pitch = seg + RG_SEG_PAD
    ncol = D_RG // LANES
    for lc in range(ncol):
        lanes = slice(lc * LANES, (lc + 1) * LANES)
        for s in range(SUBLANES):
            src = slice(s * seg, (s + 1) * seg)
            dst = slice(s * pitch, s * pitch + seg)
            af_sc[lc, dst, :], uf_sc[lc, dst, :] = a_f[src, lanes], u_f[src, lanes]
            ab_sc[lc, dst, :], ub_sc[lc, dst, :] = a_b[src, lanes], u_b[src, lanes]
    slab = lambda k: (slice(None), pl.ds(k, SUBLANES, stride=pitch), slice(None))
    hf = jnp.zeros((ncol, SUBLANES, LANES), F32)
    hb = jnp.zeros((ncol, SUBLANES, LANES), F32)
    pf = jnp.ones((ncol, SUBLANES, LANES), F32)
    pb = jnp.ones((ncol, SUBLANES, LANES), F32)
    for k in range(seg):
        kb = seg - 1 - k
        ak = af_sc[slab(k)]
        hf = ak * hf + uf_sc[slab(k)]
        pf = pf * ak
        hf_sc[slab(k)] = hf
        pf_sc[slab(k)] = pf
        ak = ab_sc[slab(kb)]
        hb = ak * hb + ub_sc[slab(kb)]
        pb = pb * ak
        hb_sc[slab(kb)] = hb
        pb_sc[slab(kb)] = pb
    for lc in range(ncol):
        lanes = slice(lc * LANES, (lc + 1) * LANES)
        c = carry[0:1, lanes]
        cin_f = []
        for s in range(SUBLANES):
            cin_f.append(c)
            c = pf[lc, s:s + 1, :] * c + hf[lc, s:s + 1, :]
        carry[0:1, lanes] = c
        c = carry[1:2, lanes]
        cin_b = [None] * SUBLANES
        for s in reversed(range(SUBLANES)):
            cin_b[s] = c
            c = pb[lc, s:s + 1, :] * c + hb[lc, s:s + 1, :]
        carry[1:2, lanes] = c
        for s in range(SUBLANES):
            rows = slice(s * seg, (s + 1) * seg)
            src = slice(s * pitch, s * pitch + seg)
            hf_ref[rows, lanes] = hf_sc[lc, src, :] + pf_sc[lc, src, :] * cin_f[s]
            hb_ref[rows, lanes] = hb_sc[lc, src, :] + pb_sc[lc, src, :] * cin_b[s]

    write_state()


def _rglru_plan(xr, h0, wbd, bias, lam, cw, cb, b, t, tb, emit_state):
    nb = t // tb
    n = b * t
    r8 = tb // SUBLANES
    last8 = n // SUBLANES - 1
    fwd = lambda bi, i: (bi * nb + i, 0)
    bwd = lambda bi, i: (bi * nb + nb - 1 - i, 0)
    fwd_prev = lambda bi, i: (jnp.maximum((bi * nb + i) * r8 - 1, 0), 0)
    fwd_next = lambda bi, i: (jnp.minimum((bi * nb + i + 1) * r8, last8), 0)
    bwd_prev = lambda bi, i: (jnp.maximum((bi * nb + nb - 1 - i) * r8 - 1, 0), 0)
    bwd_next = lambda bi, i: (jnp.minimum((bi * nb + nb - i) * r8, last8), 0)
    const = lambda bi, i: (0, 0)
    h_idx = (lambda bi, i: (bi, 0, 0)) if h0.shape[0] > 1 else (lambda bi, i: (0, 0, 0))
    halo = lambda m: pl.BlockSpec((SUBLANES, D_RG), m)
    in_specs = [pl.BlockSpec((tb, D_RG), fwd), halo(fwd_prev), halo(fwd_next),
                pl.BlockSpec((tb, D_RG), bwd), halo(bwd_prev), halo(bwd_next),
                pl.BlockSpec((1, 2, D_RG), h_idx),
                pl.BlockSpec((D_RG, 4 * D_RG), const), pl.BlockSpec((1, 4 * D_RG), const),
                pl.BlockSpec((2, D_RG), const), pl.BlockSpec((4, D_RG), const), pl.BlockSpec((1, D_RG), const)]
    out_specs = [pl.BlockSpec((tb, D_RG), fwd), pl.BlockSpec((tb, D_RG), bwd)]
    out_shape = [jax.ShapeDtypeStruct((n, D_RG), F32)] * 2
    if emit_state:
        out_specs.append(pl.BlockSpec((1, 2, D_RG), lambda bi, i: (bi, 0, 0)))
        out_shape.append(jax.ShapeDtypeStruct((b, 2, D_RG), F32))
    return dict(
        kernel=functools.partial(_rglru_kernel, tb, nb, emit_state), in_specs=in_specs, out_specs=out_specs,
        out_shape=out_shape,
        scratch=[pltpu.VMEM((2, D_RG), F32)]
        + [pltpu.VMEM((D_RG // LANES, tb + SUBLANES * RG_SEG_PAD, LANES), F32)] * 8,
        args=(xr, xr, xr, xr, xr, xr, h0, wbd, bias, lam, cw, cb))


def _scans_kernel(ml, rg, *refs):
    def split(plan, ins, outs, scr):
        n_in, n_out, n_sc = len(plan['in_specs']), len(plan['out_specs']), len(plan['scratch'])
        return ins[:n_in], ins[n_in:], outs[:n_out], outs[n_out:], scr[:n_sc], scr[n_sc:]

    n_in = len(ml['in_specs']) + len(rg['in_specs'])
    n_out = len(ml['out_specs']) + len(rg['out_specs'])
    ml_in, ins, ml_out, outs, ml_sc, scr = split(ml, refs[:n_in], refs[n_in:n_in + n_out], refs[n_in + n_out:])
    rg_in, _, rg_out, _, rg_sc, _ = split(rg, ins, outs, scr)
    for parts in (("init",), ("main",), ("final",)):
        ml['kernel'](*ml_in, *ml_out, *ml_sc, parts=parts)
        rg['kernel'](*rg_in, *rg_out, *rg_sc, parts=parts)


def _scans(ml, rg, b, nsteps):
    static = lambda plan: {k: v for k, v in plan.items() if k != 'args'}
    outs = pl.pallas_call(
        functools.partial(_scans_kernel, static(ml), static(rg)),
        grid=(b, nsteps),
        in_specs=ml['in_specs'] + rg['in_specs'],
        out_specs=ml['out_specs'] + rg['out_specs'],
        out_shape=ml['out_shape'] + rg['out_shape'],
        scratch_shapes=ml['scratch'] + rg['scratch'],
        name="scans",
        compiler_params=_params("arbitrary", "arbitrary"),
    )(*ml['args'], *rg['args'])
    n_ml = len(ml['out_specs'])
    return outs[:n_ml], outs[n_ml:]


def _route(s, sb):
    tm = s.shape[1]
    neg = -jnp.inf
    sub = lax.broadcasted_iota(I32, (GROUP, tm), 0)
    blocks = [sb[gi * GROUP:(gi + 1) * GROUP, :] for gi in range(N_GROUPS)]
    gscore = []
    for blk in blocks:
        m1 = jnp.max(blk, axis=0, keepdims=True)
        first = jnp.min(jnp.where(blk == m1, sub, GROUP), axis=0, keepdims=True)
        m2 = jnp.max(jnp.where(sub == first, neg, blk), axis=0, keepdims=True)
        gscore.append(m1 + m2)
    masked = []
    for gi in range(N_GROUPS):
        rank = jnp.zeros((1, tm), F32)
        for gj in range(N_GROUPS):
            if gj == gi:
                continue
            ahead = (gscore[gj] >= gscore[gi]) if gj < gi else (gscore[gj] > gscore[gi])
            rank = rank + jnp.where(ahead, 1.0, 0.0)
        masked.append(jnp.where(rank < TOPK_GROUPS, blocks[gi], neg))
    v = jnp.concatenate(masked, axis=0)
    eid = lax.broadcasted_iota(I32, (N_EXPERTS, tm), 0)
    sel = jnp.zeros((N_EXPERTS, tm), F32)
    picks = []
    for _ in range(TOP_K):
        mx = jnp.max(v, axis=0, keepdims=True)
        idx = jnp.min(jnp.where(v == mx, eid, N_EXPERTS), axis=0, keepdims=True)
        pick = eid == idx
        picks.append(pick)
        sel = jnp.where(pick, 1.0, sel)
        v = jnp.where(pick, neg, v)
    ws = s * sel
    return ws * (ROUTED_SCALE / jnp.sum(ws, axis=0, keepdims=True)), sel, picks


def _mix_out_kernel(hmf_ref, hmb_ref, ot_ref, hrf_ref, hrb_ref, gr_ref, x_ref, rt_ref, ct_ref, mod_ref,
                    mln_ref, rgn_ref, wo_ml_ref, wo_rg_ref, n2_ref, rwt_ref, rb_ref, cnt0_ref,
                    x1_ref, hn2p_ref, ek_ref, pk_ref, wtok_ref, cnt_ref, cnt_sc, earlier_sc):
    i = pl.program_id(0)
    tm = x_ref.shape[0]

    @pl.when(i == 0)
    def _():
        cnt_sc[...] = cnt0_ref[...].astype(F32)
        earlier = (lax.broadcasted_iota(I32, (tm, tm), 0) < lax.broadcasted_iota(I32, (tm, tm), 1))
        earlier_sc[...] = earlier.astype(BF16)

    hm = hmf_ref[...] + hmb_ref[...]
    heads = []
    for h in range(N_HEADS):
        seg = hm[h * HEAD:(h + 1) * HEAD, :]
        heads.append(seg * lax.rsqrt(jnp.mean(seg * seg, axis=0, keepdims=True) + EPS))
    y_ml_t = jnp.concatenate(heads, axis=0) * mln_ref[...] * jax.nn.sigmoid(ot_ref[...])
    y_rg = _rms(hrf_ref[...] + hrb_ref[...], rgn_ref[...]) * jax.nn.gelu(gr_ref[...])
    mix = (lax.dot_general(y_ml_t.astype(BF16), wo_ml_ref[...], (((0,), (0,)), ((), ())),
                           preferred_element_type=F32)
           + jnp.dot(y_rg.astype(BF16), wo_rg_ref[...], preferred_element_type=F32))
    x1 = x_ref[...] + _pos_tile(rt_ref, ct_ref) + mod_ref[0, 2:3, :] * mix
    x1_ref[...] = x1
    hn2 = _rms(x1, n2_ref[...]) * (1.0 + mod_ref[0, 4:5, :]) + mod_ref[0, 3:4, :]
    hn2p_ref[...] = _pack_bf16_pairs(hn2)
    hb = hn2.astype(BF16)
    h_lo = (hn2 - hb.astype(F32)).astype(BF16)
    nt = (((1,), (1,)), ((), ()))
    two = lax.dot_general(rwt_ref[...], hb, nt, preferred_element_type=F32)
    logits_t = (two[0:N_EXPERTS, :] + two[N_EXPERTS:2 * N_EXPERTS, :]
                + lax.dot_general(rwt_ref[0:N_EXPERTS, :], h_lo, nt, preferred_element_type=F32))
    s = jax.nn.sigmoid(logits_t)
    wt, sel, picks = _route(s, s + rb_ref[...])

    prefix = jnp.dot(sel.astype(BF16), earlier_sc[...], preferred_element_type=F32)
    pos_all = cnt_sc[:, 0:1] + prefix
    eid = lax.broadcasted_iota(I32, (N_EXPERTS, tm), 0)
    eid_f = eid.astype(F32)
    row8 = lax.broadcasted_iota(I32, (TOP_K, tm), 0)
    ek = jnp.zeros((TOP_K, tm), F32)
    pk = jnp.zeros((TOP_K, tm), F32)
    wk = jnp.zeros((N_EXPERTS, tm), F32)
    for k, pick in enumerate(picks):
        take = lambda a: jnp.sum(jnp.where(pick, a, 0.0), axis=0, keepdims=True)
        ek = jnp.where(row8 == k, take(eid_f), ek)
        pk = jnp.where(row8 == k, take(pos_all), pk)
        wk = jnp.where(eid == k, take(wt), wk)
    ek_ref[...] = ek.astype(I32)
    pk_ref[...] = pk.astype(I32)
    wtok_ref[...] = wk.T
    cnt_sc[...] += jnp.broadcast_to(jnp.sum(sel, axis=1, keepdims=True), cnt_sc.shape)

    @pl.when(i == pl.num_programs(0) - 1)
    def _():
        cnt_ref[...] = cnt_sc[...].astype(I32)


def _mix_out(hmf, hmb, ot, hrf, hrb, gr, x2d, rtab, ctab, mod, t, tm, mln, rgn, wo_ml, wo_rg, norm2, rwt, rbias,
             cnt0):
    n = x2d.shape[0]
    tok = lambda i: (i, 0)
    tok_t = lambda i: (0, i)
    const = lambda i: (0, 0)
    return pl.pallas_call(
        _mix_out_kernel,
        grid=(n // tm,),
        in_specs=[pl.BlockSpec((D_ML, tm), tok_t)] * 3 + [pl.BlockSpec((tm, D_RG), tok)] * 3 + [
            pl.BlockSpec((tm, D_MODEL), tok)] + _pos_specs(rtab, ctab, tm) + [
            pl.BlockSpec((1, N_MOD, D_MODEL), _mod_index(mod.shape[0], tm, t)),
            pl.BlockSpec((D_ML, 1), const), pl.BlockSpec((1, D_RG), const),
            pl.BlockSpec((D_ML, D_MODEL), const), pl.BlockSpec((D_RG, D_MODEL), const),
            pl.BlockSpec((1, D_MODEL), const),
            pl.BlockSpec((2 * N_EXPERTS, D_MODEL), const), pl.BlockSpec((N_EXPERTS, 1), const),
            pl.BlockSpec((N_EXPERTS, LANES), const),
        ],
        out_specs=[pl.BlockSpec((tm, D_MODEL), tok), pl.BlockSpec((tm, D_PACK), tok),
                   pl.BlockSpec((TOP_K, tm), tok_t), pl.BlockSpec((TOP_K, tm), tok_t),
                   pl.BlockSpec((tm, N_EXPERTS), tok), pl.BlockSpec((N_EXPERTS, LANES), const)],
        out_shape=[jax.ShapeDtypeStruct((n, D_MODEL), F32), jax.ShapeDtypeStruct((n, D_PACK), I32),
                   jax.ShapeDtypeStruct((TOP_K, n), I32), jax.ShapeDtypeStruct((TOP_K, n), I32),
                   jax.ShapeDtypeStruct((n, N_EXPERTS), F32), jax.ShapeDtypeStruct((N_EXPERTS, LANES), I32)],
        scratch_shapes=[pltpu.VMEM((N_EXPERTS, LANES), F32), pltpu.VMEM((tm, tm), BF16)],
        name="mix_out",
        compiler_params=_params("arbitrary"),
    )(hmf, hmb, ot, hrf, hrb, gr, x2d, rtab, ctab, mod, mln, rgn, wo_ml, wo_rg, norm2, rwt, rbias, cnt0)


def _sc_mesh():
    return plsc.VectorSubcoreMesh(core_axis_name="core", subcore_axis_name="subcore")


def _sc_worker():
    info = plsc.get_sparse_core_info()
    return lax.axis_index("subcore") * info.num_cores + lax.axis_index("core"), info.num_cores * info.num_subcores


def _sc_dispatch(xp, dest3, n_slots):
    n, w = xp.shape
    nwin = n // SC_WINDOW

    @pl.kernel(out_type=jax.ShapeDtypeStruct((n_slots, w), xp.dtype), mesh=_sc_mesh(),
               scratch_types=[pltpu.VMEM((SC_WINDOW, w), xp.dtype), pltpu.VMEM((TOP_K, SC_WINDOW), I32)],
               name="sc_dispatch")
    def k(x_hbm, i_hbm, o_hbm, x_v, i_v):
        wid, nworkers = _sc_worker()
        per = nwin // nworkers

        @pl.loop(0, per)
        def _(s):
            win = wid * per + s
            pltpu.sync_copy(x_hbm.at[pl.ds(win * SC_WINDOW, SC_WINDOW)], x_v)
            pltpu.sync_copy(i_hbm.at[win], i_v)
            for j in range(TOP_K):
                pltpu.sync_copy(x_v, o_hbm.at[i_v.at[j]])

    return k(xp, dest3)


def _sc_pack_rows(w):
    r, c = w.shape
    lanes = plsc.get_sparse_core_info().num_lanes
    chunk = SC_PACK_WORDS // c
    nchunks = r // chunk

    def rne_high(x):
        u = plsc.bitcast(x, I32)
        u = u + 0x7FFF + (lax.shift_right_logical(u, jnp.full(u.shape, 16, I32)) & 1)
        return u & jnp.int32(-65536)

    @pl.kernel(out_type=jax.ShapeDtypeStruct((r // 2, c), I32), mesh=_sc_mesh(),
               scratch_types=[pltpu.VMEM((chunk, c), F32), pltpu.VMEM((chunk // 2, c), I32)],
               compiler_params=pltpu.CompilerParams(needs_layout_passes=False), name="sc_pack")
    def k(w_hbm, o_hbm, in_v, out_v):
        wid, nworkers = _sc_worker()
        per = nchunks // nworkers

        @pl.loop(0, per)
        def _(s):
            first = (wid * per + s) * chunk
            pltpu.sync_copy(w_hbm.at[pl.ds(first, chunk)], in_v)

            @pl.loop(0, chunk // 2)
            def _(row):
                for j in range(c // lanes):
                    at = lambda r_: (r_, pl.ds(j * lanes, lanes))
                    lo = rne_high(in_v.at[*at(2 * row)][...])
                    hi = rne_high(in_v.at[*at(2 * row + 1)][...])
                    out_v.at[*at(row)][...] = hi | lax.shift_right_logical(lo, jnp.full(lo.shape, 16, I32))

            pltpu.sync_copy(out_v, o_hbm.at[pl.ds((wid * per + s) * (chunk // 2), chunk // 2)])

    return k(w)


def _sc_combine_gather(ys, dest3):
    nwin = dest3.shape[0]
    w = ys.shape[1]

    @pl.kernel(out_type=jax.ShapeDtypeStruct((nwin, TOP_K, SC_WINDOW, w), ys.dtype), mesh=_sc_mesh(),
               scratch_types=[pltpu.VMEM((SC_WINDOW, w), ys.dtype), pltpu.VMEM((TOP_K, SC_WINDOW), I32)],
               name="sc_combine")
    def k(y_hbm, i_hbm, o_hbm, y_v, i_v):
        wid, nworkers = _sc_worker()
        per = nwin // nworkers

        @pl.loop(0, per)
        def _(s):
            win = wid * per + s
            pltpu.sync_copy(i_hbm.at[win], i_v)
            for j in range(TOP_K):
                pltpu.sync_copy(y_hbm.at[i_v.at[j]], y_v)
                pltpu.sync_copy(y_v, o_hbm.at[win, j])

    return k(ys, dest3)


def _swiglu(x, w13):
    h = jnp.dot(x, w13, preferred_element_type=F32)
    return _silu(h[:, 0:D_EXPERT]) * h[:, D_EXPERT:2 * D_EXPERT]


def _unpack_rows_bf16(p):
    hi, lo = _unpack_bf16_pairs(p)
    return jnp.concatenate([hi.astype(BF16), lo.astype(BF16)], axis=1)


def _expert_kernel(rows, rank_ref, expert_ref, nu_ref, nr_ref, x_ref, w1_hbm, w3_hbm, w2_hbm, y_ref,
                   w1buf, w3buf, w2buf, sem, progress):
    b = pl.program_id(0)
    g = EXPERT_BLOCKS_PER_STEP
    n_ranks = nr_ref[0]

    @pl.when(b == 0)
    def _():
        progress[0] = 0
        progress[1] = 0

    def copies(r):
        slot = lax.rem(r, EXPERT_RING)
        e = expert_ref[r]
        return (pltpu.make_async_copy(w1_hbm.at[e], w1buf.at[slot], sem.at[slot, 0]),
                pltpu.make_async_copy(w3_hbm.at[e], w3buf.at[slot], sem.at[slot, 1]),
                pltpu.make_async_copy(w2_hbm.at[e], w2buf.at[slot], sem.at[slot, 2]))

    def start(r, carry):
        for cp in copies(r):
            cp.start()
        return carry

    def wait(r, carry):
        for cp in copies(r):
            cp.wait()
        return carry

    first = rank_ref[b * g]
    last = rank_ref[b * g + g - 1]
    started = jnp.minimum(first + EXPERT_RING, n_ranks)
    lax.fori_loop(progress[0], started, start, 0)
    progress[0] = jnp.maximum(progress[0], started)
    needed = jnp.where(b == pl.num_programs(0) - 1, progress[0], jnp.minimum(last + 1, n_ranks))
    lax.fori_loop(progress[1], needed, wait, 0)
    progress[1] = jnp.maximum(progress[1], needed)

    @pl.when(b * g < nu_ref[0])
    def _():
        for j in range(g):
            slot = lax.rem(rank_ref[b * g + j], EXPERT_RING)
            sl = slice(j * rows, (j + 1) * rows)
            x = _unpack_rows_bf16(x_ref[sl, :])
            w = lambda buf: pltpu.bitcast(buf[slot], BF16)
            h = (_silu(jnp.dot(x, w(w1buf), preferred_element_type=F32))
                 * jnp.dot(x, w(w3buf), preferred_element_type=F32))
            y_ref[sl, :] = _pack_bf16_pairs(jnp.dot(h.astype(BF16), w(w2buf), preferred_element_type=F32))


def _experts(xs, block_rank, rank_expert, n_used, n_ranks, w1, w3, w2, rows):
    g = EXPERT_BLOCKS_PER_STEP
    nb = xs.shape[0] // rows
    tok = lambda b, *_: (b, 0)
    return pl.pallas_call(
        functools.partial(_expert_kernel, rows),
        grid_spec=pltpu.PrefetchScalarGridSpec(
            num_scalar_prefetch=4,
            grid=(nb // g,),
            in_specs=[pl.BlockSpec((g * rows, D_PACK), tok)] + [pl.BlockSpec(memory_space=pl.ANY)] * 3,
            out_specs=pl.BlockSpec((g * rows, D_PACK), tok),
            scratch_shapes=[pltpu.VMEM((EXPERT_RING, D_MODEL // 2, D_EXPERT), I32),
                            pltpu.VMEM((EXPERT_RING, D_MODEL // 2, D_EXPERT), I32),
                            pltpu.VMEM((EXPERT_RING, D_EXPERT // 2, D_MODEL), I32),
                            pltpu.SemaphoreType.DMA((EXPERT_RING, 3)), pltpu.SMEM((2,), I32)],
        ),
        out_shape=jax.ShapeDtypeStruct(xs.shape, I32),
        name="experts",
        compiler_params=_params("arbitrary", vmem=VMEM_LIMIT_EXPERTS),
    )(block_rank, rank_expert, n_used, n_ranks, xs, w1, w3, w2)


def _moe_out_kernel(win0, yk_hbm, hn2p_ref, wtok_ref, sw13_ref, sw2_ref, x1_ref, mod_ref, nf_ref, y_ref,
                    ykbuf, sem):
    i = pl.program_id(0)
    nsteps = pl.num_programs(0)
    nwin = ykbuf.shape[1]

    def fetch(step):
        slot = lax.rem(step, MOE_OUT_BUFFERS)
        return pltpu.make_async_copy(yk_hbm.at[pl.ds(win0 + step * nwin, nwin)], ykbuf.at[slot], sem.at[slot])

    @pl.when(i == 0)
    def _():
        for step in range(MOE_OUT_BUFFERS - 1):
            fetch(step).start()

    @pl.when(i + MOE_OUT_BUFFERS - 1 < nsteps)
    def _():
        fetch(i + MOE_OUT_BUFFERS - 1).start()

    fetch(i).wait()

    def body(slot):
        shared = jnp.dot(_swiglu(_unpack_rows_bf16(hn2p_ref[...]), sw13_ref[...]).astype(BF16), sw2_ref[...],
                         preferred_element_type=F32)
        w = wtok_ref[...]
        parts = []
        for wi in range(nwin):
            rows = slice(wi * SC_WINDOW, (wi + 1) * SC_WINDOW)
            a_hi = shared[rows, 0:D_PACK]
            a_lo = shared[rows, D_PACK:D_MODEL]
            for k in range(TOP_K):
                y_hi, y_lo = _unpack_bf16_pairs(ykbuf[slot, wi, k])
                wc = w[rows, k:k + 1]
                a_hi = a_hi + wc * y_hi
                a_lo = a_lo + wc * y_lo
            parts.append(jnp.concatenate([a_hi, a_lo], axis=1))
        x2 = x1_ref[...] + mod_ref[0, 5:6, :] * jnp.concatenate(parts, axis=0)
        y_ref[...] = _rms(x2, nf_ref[...])

    for s in range(MOE_OUT_BUFFERS):
        pl.when(lax.rem(i, MOE_OUT_BUFFERS) == s)(functools.partial(body, s))


def _moe_out(yk, tok0, hn2p, wtok, sw13, sw2, x1, mod, t, tm, norm_final):
    n = hn2p.shape[0]
    tok = lambda i: (i, 0)
    const = lambda i: (0, 0)
    nwin = tm // SC_WINDOW
    assert n // tm >= MOE_OUT_BUFFERS - 1
    return pl.pallas_call(
        functools.partial(_moe_out_kernel, tok0 // SC_WINDOW),
        grid=(n // tm,),
        in_specs=[
            pl.BlockSpec(memory_space=pl.ANY),
            pl.BlockSpec((tm, D_PACK), tok),
            pl.BlockSpec((tm, N_EXPERTS), tok),
            pl.BlockSpec((D_MODEL, 2 * D_EXPERT), const),
            pl.BlockSpec((D_EXPERT, D_MODEL), const),
            pl.BlockSpec((tm, D_MODEL), tok),
            pl.BlockSpec((1, N_MOD, D_MODEL), _mod_index(mod.shape[0], tm, t)),
            pl.BlockSpec((1, D_MODEL), const),
        ],
        out_specs=pl.BlockSpec((tm, D_MODEL), tok),
        out_shape=jax.ShapeDtypeStruct((n, D_MODEL), F32),
        scratch_shapes=[pltpu.VMEM((MOE_OUT_BUFFERS, nwin, TOP_K, SC_WINDOW, D_PACK), I32),
                        pltpu.SemaphoreType.DMA((MOE_OUT_BUFFERS,))],
        name="moe_out",
        compiler_params=_params("arbitrary"),
    )(yk, hn2p, wtok, sw13, sw2, x1, mod, norm_final)


def _dispatch_plan(cnt, ek, pk, rows):
    n = ek.shape[1]
    nb = n * TOP_K // rows + N_EXPERTS
    nblk = (cnt + rows - 1) // rows
    block_end = jnp.cumsum(nblk)
    experts = jnp.arange(N_EXPERTS, dtype=I32)
    first_row = jnp.sum(jnp.where(ek[:, :, None] == experts, (block_end - nblk) * rows, 0), axis=-1)
    dest3 = (first_row + pk).reshape(TOP_K, n // SC_WINDOW, SC_WINDOW).transpose(1, 0, 2)
    owns = nblk > 0
    n_ranks = jnp.sum(owns.astype(I32))
    blocks = jnp.arange(nb, dtype=I32)[:, None]
    block_rank = jnp.minimum(jnp.sum((owns & (block_end <= blocks)).astype(I32), axis=1), n_ranks - 1)
    rank_expert = jnp.minimum(jnp.sum((jnp.cumsum(owns.astype(I32)) <= experts[:, None]).astype(I32), axis=1),
                              N_EXPERTS - 1)
    return (dest3, block_rank.astype(I32), rank_expert.astype(I32), block_end[-1:].astype(I32),
            n_ranks.reshape(1).astype(I32), nb * rows)


def _grid_pos_tables(n_tokens, dim):
    quarter = dim // 4
    omega = 1.0 / (POS_BASE ** (jnp.arange(quarter, dtype=F32) / quarter))
    ra = jnp.arange(n_tokens // GRID_W).astype(F32)[:, None] * omega
    ca = jnp.arange(GRID_W).astype(F32)[:, None] * omega
    return (jnp.concatenate([jnp.sin(ra), jnp.cos(ra)], axis=-1),
            jnp.concatenate([jnp.sin(ca), jnp.cos(ca)], axis=-1))


def _hi_lo_rows(w):
    hi = w.astype(BF16)
    return jnp.concatenate([hi, (w - hi.astype(F32)).astype(BF16)], axis=0)


def _block_diag(w):
    eye = jnp.eye(N_RG_BLOCKS, dtype=w.dtype)
    return jnp.einsum('nij,nm->nimj', w, eye).reshape(D_RG, D_RG)


def _layer_weights(l, norm1, w_in, mlstm_gate_bias, mlstm_norm, rg_conv_w, rg_conv_b, rg_wa, rg_ba, rg_wx,
                   rg_bx, rg_lambda, rg_norm, w_out, norm2, router_w, router_bias, exp_w1, exp_w3, exp_w2,
                   shared_w1, shared_w3, shared_w2):
    wi = w_in[l]
    c0, c1 = 4 * D_ML, 4 * D_ML + N_GATE
    wi16 = wi.astype(BF16)
    wg = wi[:, c0:c1]
    wg_hi = wi16[:, c0:c1]
    wg_lo = (wg - wg_hi.astype(F32)).astype(BF16)
    zcols = lambda w: jnp.zeros((D_MODEL, w), BF16)
    return dict(
        norm1=norm1[l].reshape(1, D_MODEL),
        wq=wi16[:, :c0],
        wr=jnp.concatenate([wi16[:, c1:], wg_hi, wg_lo, zcols(LANES - 2 * N_GATE)], axis=1),
        wgh=jnp.concatenate([wg_hi, zcols(LANES - N_GATE)], axis=1),
        gbias=jnp.pad(mlstm_gate_bias[l].reshape(1, N_GATE), ((0, 0), (0, LANES - N_GATE))),
        mln=mlstm_norm[l].reshape(D_ML, 1),
        cw=rg_conv_w[l], cb=rg_conv_b[l].reshape(1, D_RG),
        wbd=jnp.concatenate([_block_diag(rg_wa[l, 0]), _block_diag(rg_wx[l, 0]),
                             _block_diag(rg_wa[l, 1]), _block_diag(rg_wx[l, 1])], axis=1).astype(BF16),
        rbias=jnp.concatenate([rg_ba[l, 0], rg_bx[l, 0], rg_ba[l, 1], rg_bx[l, 1]]).reshape(1, 4 * D_RG),
        lam=rg_lambda[l], rgn=rg_norm[l].reshape(1, D_RG),
        wo_ml=w_out[l, :D_ML].astype(BF16), wo_rg=w_out[l, D_ML:].astype(BF16),
        norm2=norm2[l].reshape(1, D_MODEL),
        rwt=_hi_lo_rows(router_w[l].T), rb=router_bias[l].reshape(N_EXPERTS, 1),
        w1=_sc_pack_rows(exp_w1[l].reshape(N_EXPERTS * D_MODEL, D_EXPERT)).reshape(N_EXPERTS, D_MODEL // 2, -1),
        w3=_sc_pack_rows(exp_w3[l].reshape(N_EXPERTS * D_MODEL, D_EXPERT)).reshape(N_EXPERTS, D_MODEL // 2, -1),
        w2=_sc_pack_rows(exp_w2[l].reshape(N_EXPERTS * D_EXPERT, D_MODEL)).reshape(N_EXPERTS, D_EXPERT // 2, -1),
        sw13=jnp.concatenate([shared_w1[l], shared_w3[l]], axis=-1).astype(BF16),
        sw2=shared_w2[l].astype(BF16),
    )


def _mixers(x2d, pos_tables, mod, c0, n0, m0, h0, cnt0, lw, b, t, emit_state):
    tl = _tiles(t, mod.shape[0] > 1)
    tm = tl['tok']
    rtab, ctab = pos_tables
    qt, k, kt, v, vt, ot, xr, gr, g, gt = _in_proj(x2d, rtab, ctab, mod, t, tm, lw['norm1'], lw['wq'], lw['wr'],
                                                   lw['wgh'], lw['gbias'])
    chunk = tl['chunk']
    ml, rg = _scans(
        _mlstm_plan(qt, k, kt, v, vt, g, gt, c0, n0, m0, b, t, chunk, emit_state),
        _rglru_plan(xr, h0, lw['wbd'], lw['rbias'], lw['lam'], lw['cw'], lw['cb'], b, t, chunk, emit_state),
        b, t // chunk)
    routed = _mix_out(ml[0], ml[1], ot, rg[0], rg[1], gr, x2d, rtab, ctab, mod, t, tm, lw['mln'], lw['rgn'],
                      lw['wo_ml'], lw['wo_rg'], lw['norm2'], lw['rwt'], lw['rb'], cnt0)
    return routed, ml[2:], rg[2:]


def _routed_experts(paths, lw):
    cnt = paths[-1][5][:, 0]
    hn2p = jnp.concatenate([p[1] for p in paths], axis=0)
    ek = jnp.concatenate([p[2] for p in paths], axis=1)
    pk = jnp.concatenate([p[3] for p in paths], axis=1)
    rows = EXPERT_ROWS if ek.shape[1] * TOP_K >= 4 * EXPERT_ROWS * N_EXPERTS else EXPERT_ROWS // 2
    dest3, block_rank, rank_expert, n_used, n_ranks, n_slots = _dispatch_plan(cnt, ek, pk, rows)
    xs = _sc_dispatch(hn2p, dest3, n_slots)
    ys = _experts(xs, block_rank, rank_expert, n_used, n_ranks, lw['w1'], lw['w3'], lw['w2'], rows)
    return _sc_combine_gather(ys, dest3)


def kernel(x_prompt, x_sample, c, state_mlstm_C, state_mlstm_n, state_mlstm_m, state_rglru_h, c_ctx, w_ada, b_ada, norm1, w_in, mlstm_gate_bias, mlstm_norm, rg_conv_w, rg_conv_b, rg_wa, rg_ba, rg_wx, rg_bx, rg_lambda, rg_norm, w_out, norm2, router_w, router_bias, exp_w1, exp_w3, exp_w2, shared_w1, shared_w3, shared_w2, norm_final):
    bp, tp, _ = x_prompt.shape
    bs, ts, _ = x_sample.shape
    depth = w_ada.shape[0]
    assert depth == 1, "the final norm is fused into the single layer's MoE output kernel"
    nd = 2 * N_HEADS
    l = 0
    lw = _layer_weights(l, norm1, w_in, mlstm_gate_bias, mlstm_norm, rg_conv_w, rg_conv_b, rg_wa, rg_ba, rg_wx,
                        rg_bx, rg_lambda, rg_norm, w_out, norm2, router_w, router_bias, exp_w1, exp_w3, exp_w2,
                        shared_w1, shared_w3, shared_w2)
    nf = norm_final.reshape(1, D_MODEL)
    cvecs = jnp.concatenate([c_ctx[None], c, jnp.zeros((SUBLANES - 1 - bs, D_MODEL), F32)], axis=0)
    mod = _ada(cvecs, w_ada[l], b_ada[l]).reshape(SUBLANES, N_MOD, D_MODEL)

    xp2d, lw['w1'], lw['w3'], lw['w2'] = lax.optimization_barrier(
        (x_prompt.reshape(bp * tp, D_MODEL), lw['w1'], lw['w3'], lw['w2']))

    mod_p, mod_s = mod[0:1], mod[1:1 + bs]
    tm_p, tm_s = _tiles(tp, False)['tok'], _tiles(ts, True)['tok']
    rp, (cc, nc_, mc), (hc,) = _mixers(
        xp2d,
        (jnp.zeros((tm_p // GRID_W, D_MODEL // 2), F32), jnp.zeros((GRID_W, D_MODEL // 2), F32)), mod_p,
        jnp.zeros((1, nd, HEAD, HEAD), F32), jnp.zeros((1, nd, HEAD), F32), jnp.zeros((1, nd, HEAD), F32),
        jnp.zeros((1, 2, D_RG), F32), jnp.zeros((N_EXPERTS, LANES), I32), lw, bp, tp, True)
    rs, _, _ = _mixers(
        x_sample.reshape(bs * ts, D_MODEL), _grid_pos_tables(ts, D_MODEL), mod_s,
        state_mlstm_C[:, l].reshape(bs, nd, HEAD, HEAD), state_mlstm_n[:, l].reshape(bs, nd, HEAD),
        jnp.broadcast_to(state_mlstm_m[:, l].reshape(bs, nd, 1), (bs, nd, HEAD)),
        state_rglru_h[:, l], jnp.zeros((N_EXPERTS, LANES), I32), lw, bs, ts, False)
    yp = _moe_out(_routed_experts([rp], lw), 0, rp[1], rp[4], lw['sw13'], lw['sw2'], rp[0], mod_p, tp, tm_p, nf)
    ys = _moe_out(_routed_experts([rs], lw), 0, rs[1], rs[4], lw['sw13'], lw['sw2'], rs[0], mod_s, ts, tm_s, nf)

    y_prompt = yp.reshape(bp, tp, D_MODEL)
    y_sample = ys.reshape(bs, ts, D_MODEL)
    new_c = cc.reshape(bp, 1, 2, N_HEADS, HEAD, HEAD)
    new_n = nc_.reshape(bp, 1, 2, N_HEADS, HEAD)
    new_m = mc[:, :, 0].reshape(bp, 1, 2, N_HEADS)
    new_h = hc.reshape(bp, 1, 2, D_RG)
    return (y_prompt, y_sample, new_c, new_n, new_m, new_h)
```

```python
import functools

import jax
import jax.numpy as jnp
from jax import lax
from jax.experimental import pallas as pl
from jax.experimental.pallas import tpu as pltpu
from jax.experimental.pallas import tpu_sc as plsc

F32 = jnp.float32
BF16 = jnp.bfloat16
I32 = jnp.int32
HIGHEST = lax.Precision.HIGHEST

D_MODEL = 1024
N_MOD = 6
D_ML = 512
N_HEADS = 4
HEAD = 128
D_RG = 512
N_RG_BLOCKS = 8
RG_BLOCK = 64
RG_C = 8.0
N_GATE = 16
N_EXPERTS = 64
N_GROUPS = 8
GROUP = 8
TOPK_GROUPS = 4
TOP_K = 8
D_EXPERT = 256
ROUTED_SCALE = 2.5
EPS = 1e-6
GRID_W = 64
POS_BASE = 10000.0

RG_SEG_PAD = 8
SC_WINDOW = 128
SC_PACK_WORDS = 65536
D_PACK = D_MODEL // 2
MOE_OUT_BUFFERS = 3
EXPERT_ROWS = 512
EXPERT_STEP_ROWS = 4096
EXPERT_STEP_BLOCKS = 16

SUBLANES = 8
LANES = 128
VMEM_LIMIT = 48 * 1024 * 1024
VMEM_LIMIT_EXPERTS = 58 * 1024 * 1024


def _params(*sem, vmem=VMEM_LIMIT):
    return pltpu.CompilerParams(dimension_semantics=sem, vmem_limit_bytes=vmem)


def _tiles(t, per_sequence_mod):
    cap = t if per_sequence_mod else 1 << 30
    return dict(
        tok=min(512, cap),
        proj=1024 if t % 1024 == 0 else min(512, cap),
        chunk=min(256, t),
    )


def _silu(x):
    return x * jax.nn.sigmoid(x)


def _softplus(x):
    return jnp.maximum(x, 0.0) + jnp.log1p(jnp.exp(-jnp.abs(x)))


def _rms(x, g):
    return x * lax.rsqrt(jnp.mean(x * x, axis=-1, keepdims=True) + EPS) * g


def _bf16_pieces(x):
    hi = x.astype(BF16)
    r = x - hi.astype(F32)
    mid = r.astype(BF16)
    return hi, mid, (r - mid.astype(F32)).astype(BF16)


def _pack_bf16_pairs(x):
    w = x.shape[1] // 2
    hi = lax.bitcast_convert_type(x[:, :w].astype(BF16).astype(F32), I32)
    lo = lax.bitcast_convert_type(x[:, w:].astype(BF16).astype(F32), I32)
    return hi | lax.shift_right_logical(lo, jnp.full(lo.shape, 16, I32))


def _unpack_bf16_pairs(p):
    hi = lax.bitcast_convert_type(p & jnp.int32(-65536), F32)
    lo = lax.bitcast_convert_type(lax.shift_left(p, jnp.full(p.shape, 16, I32)), F32)
    return hi, lo


def _ada_kernel(c_ref, w_ref, b_ref, o_ref):
    s_hi, s_mid, s_lo = _bf16_pieces(_silu(c_ref[...]))
    w = w_ref[...]
    w_hi = w.astype(BF16)
    w_lo = (w - w_hi.astype(F32)).astype(BF16)
    dot = lambda a, b: jnp.dot(a, b, preferred_element_type=F32)
    o_ref[...] = dot(s_hi, w_hi) + dot(s_hi, w_lo) + dot(s_mid, w_hi) + dot(s_lo, w_hi) + b_ref[...]


def _ada(cvecs, w_ada, b_ada):
    n_out = w_ada.shape[1]
    tn = 1536
    return pl.pallas_call(
        _ada_kernel,
        grid=(n_out // tn,),
        in_specs=[
            pl.BlockSpec((SUBLANES, D_MODEL), lambda j: (0, 0)),
            pl.BlockSpec((D_MODEL, tn), lambda j: (0, j)),
            pl.BlockSpec((1, tn), lambda j: (0, j)),
        ],
        out_specs=pl.BlockSpec((SUBLANES, tn), lambda j: (0, j)),
        out_shape=jax.ShapeDtypeStruct((SUBLANES, n_out), F32),
        name="ada",
        compiler_params=_params("arbitrary"),
    )(cvecs, w_ada, b_ada.reshape(1, n_out))


def _pos_tile(rt_ref, ct_ref):
    left = jnp.concatenate([jnp.broadcast_to(rt_ref[r:r + 1, :], (GRID_W, rt_ref.shape[1]))
                            for r in range(rt_ref.shape[0])], axis=0)
    right = jnp.concatenate([ct_ref[...]] * rt_ref.shape[0], axis=0)
    return jnp.concatenate([left, right], axis=1)


def _in_proj_kernel(x_ref, rt_ref, ct_ref, mod_ref, n1_ref, wq_ref, wr_ref, wgh_ref, gb_ref,
                    qt_ref, k_ref, kt_ref, v_ref, vt_ref, ot_ref, xr_ref, gr_ref, g_ref, gt_ref):
    x = x_ref[...] + _pos_tile(rt_ref, ct_ref)
    hn = _rms(x, n1_ref[...]) * (1.0 + mod_ref[0, 1:2, :]) + mod_ref[0, 0:1, :]
    hb = hn.astype(BF16)
    z = jnp.dot(hb, wq_ref[...], preferred_element_type=F32)
    k = z[:, D_ML:2 * D_ML] * (HEAD ** -0.5)
    v = z[:, 2 * D_ML:3 * D_ML]
    qt_ref[...] = z[:, 0:D_ML].T.astype(BF16)
    k_ref[...] = k.astype(BF16)
    kt_ref[...] = k.T.astype(BF16)
    v_ref[...] = v.astype(BF16)
    vt_ref[...] = v.T.astype(BF16)
    ot_ref[...] = z[:, 3 * D_ML:4 * D_ML].T
    zr = jnp.dot(hb, wr_ref[...], preferred_element_type=F32)
    xr_ref[...] = zr[:, 0:D_RG]
    gr_ref[...] = zr[:, D_RG:2 * D_RG]
    zg = zr[:, 2 * D_RG:2 * D_RG + LANES]
    h_lo = (hn - hb.astype(F32)).astype(BF16)
    g = (zg + pltpu.roll(zg, LANES - N_GATE, 1)
         + jnp.dot(h_lo, wgh_ref[...], preferred_element_type=F32) + gb_ref[...])
    col = lax.broadcasted_iota(I32, g.shape, 1)
    g = jnp.where((col & 4) != 0, -_softplus(-g), g)
    g_ref[...] = g[:, 0:N_GATE]
    gt_ref[...] = g.T[0:N_GATE, :]


def _mod_index(bm, tm, t):
    if bm > 1:
        return lambda i: ((i * tm) // t, 0, 0)
    return lambda i: (0, 0, 0)


def _pos_specs(rtab, ctab, tm):
    rows = tm // GRID_W
    period = rtab.shape[0] // rows
    return [pl.BlockSpec((rows, D_MODEL // 2), lambda i: (i % period, 0)),
            pl.BlockSpec((GRID_W, D_MODEL // 2), lambda i: (0, 0))]


def _in_proj(x2d, rtab, ctab, mod, t, tm, norm1, wq, wr, wgh, gbias):
    n = x2d.shape[0]
    tok = lambda i: (i, 0)
    tok_t = lambda i: (0, i)
    const = lambda i: (0, 0)
    f = lambda w: jax.ShapeDtypeStruct((n, w), F32)
    row16 = jax.ShapeDtypeStruct((n, D_ML), BF16)
    col16 = jax.ShapeDtypeStruct((D_ML, n), BF16)
    return pl.pallas_call(
        _in_proj_kernel,
        grid=(n // tm,),
        in_specs=[pl.BlockSpec((tm, D_MODEL), tok)] + _pos_specs(rtab, ctab, tm) + [
            pl.BlockSpec((1, N_MOD, D_MODEL), _mod_index(mod.shape[0], tm, t)),
            pl.BlockSpec((1, D_MODEL), const),
            pl.BlockSpec((D_MODEL, 4 * D_ML), const),
            pl.BlockSpec((D_MODEL, 2 * D_RG + LANES), const),
            pl.BlockSpec((D_MODEL, LANES), const),
            pl.BlockSpec((1, LANES), const),
        ],
        out_specs=[pl.BlockSpec((D_ML, tm), tok_t), pl.BlockSpec((tm, D_ML), tok), pl.BlockSpec((D_ML, tm), tok_t),
                   pl.BlockSpec((tm, D_ML), tok), pl.BlockSpec((D_ML, tm), tok_t), pl.BlockSpec((D_ML, tm), tok_t),
                   pl.BlockSpec((tm, D_RG), tok), pl.BlockSpec((tm, D_RG), tok),
                   pl.BlockSpec((tm, N_GATE), tok), pl.BlockSpec((N_GATE, tm), tok_t)],
        out_shape=[col16, row16, col16, row16, col16, jax.ShapeDtypeStruct((D_ML, n), F32),
                   f(D_RG), f(D_RG), f(N_GATE), jax.ShapeDtypeStruct((N_GATE, n), F32)],
        name="in_proj",
        compiler_params=_params("arbitrary"),
    )(x2d, rtab, ctab, mod, norm1, wq, wr, wgh, gbias)


ALL_PARTS = ("init", "main", "final")


def _mlstm_kernel(chunk, nc, emit_state, *refs, parts=ALL_PARTS):
    (qtf, kf, ktf, vf, vtf, gf, gtf, qtb, kb, ktb, vb, vtb, gb, gtb, c0_ref, n0_ref, m0_ref) = refs[:17]
    if emit_state:
        hf_ref, hb_ref, c_out, n_out, m_out, c_sc, n_sc, m_sc = refs[17:]
    else:
        hf_ref, hb_ref, c_sc, n_sc, m_sc = refs[17:]
    i = pl.program_id(1)

    if "init" in parts:
        @pl.when(i == 0)
        def _():
            c_sc[...] = c0_ref[0]
            n_sc[...] = n0_ref[0]
            m_sc[...] = m0_ref[0]

    def write_state():
        if emit_state and "final" in parts:
            @pl.when(i == nc - 1)
            def _():
                c_out[0] = c_sc[...]
                n_out[0] = n_sc[...]
                m_out[0] = m_sc[...]

    if "main" not in parts:
        write_state()
        return

    key = lax.broadcasted_iota(I32, (chunk, chunk), 0)
    qry = lax.broadcasted_iota(I32, (chunk, chunk), 1)
    hd = []
    for d, (qt_ref, k_ref, kt_ref, v_ref, vt_ref, g_ref, gt_ref, h_ref) in enumerate(
            ((qtf, kf, ktf, vf, vtf, gf, gtf, hf_ref), (qtb, kb, ktb, vb, vtb, gb, gtb, hb_ref))):
        tri = (key <= qry) if d == 0 else (key >= qry)
        tri_t = (qry <= key) if d == 0 else (qry >= key)
        g = g_ref[...]
        gt = gt_ref[...]
        rows3 = jnp.dot(jnp.concatenate(_bf16_pieces(gt), axis=0), tri.astype(BF16), preferred_element_type=F32)
        brow = rows3[0:N_GATE] + rows3[N_GATE:2 * N_GATE] + rows3[2 * N_GATE:3 * N_GATE]
        tri_t16 = tri_t.astype(BF16)
        bcol = sum(jnp.dot(tri_t16, piece, preferred_element_type=F32) for piece in _bf16_pieces(g))
        blast = bcol[chunk - 1:chunk, :] if d == 0 else bcol[0:1, :]
        for h in range(N_HEADS):
            ci = d * 8 + h
            cf = d * 8 + 4 + h
            j = d * N_HEADS + h
            sl = slice(h * HEAD, (h + 1) * HEAD)
            hd.append(dict(
                j=j, sl=sl, tri=tri, h_ref=h_ref, qt=qt_ref[sl, :], k=k_ref[:, sl], kt=kt_ref[sl, :],
                v=v_ref[:, sl], vt=vt_ref[sl, :], b_row=brow[cf:cf + 1, :],
                gate_col=g[:, ci:ci + 1] - bcol[:, cf:cf + 1], gate_row=gt[ci:ci + 1, :] - brow[cf:cf + 1, :],
                b_last=blast[:, cf:cf + 1], m_prev=m_sc[j:j + 1, 0:1], c_prev=c_sc[j], n_prev=n_sc[j:j + 1, :]))
    for x in hd:
        top = jnp.max(jnp.where(x['tri'], x['gate_col'], -jnp.inf), axis=0, keepdims=True)
        mx = jnp.maximum(x['m_prev'], top)
        x['dm'] = jnp.exp(jnp.where(x['tri'], x['gate_col'] - mx, -jnp.inf))
        x['w_inter'] = jnp.exp(x['m_prev'] - mx)
        x['floor'] = jnp.exp(-(x['b_row'] + mx))
    for x in hd:
        x['st'] = jnp.dot(x['k'], x['qt'], preferred_element_type=F32) * x['dm']
    for x in hd:
        inter = lax.dot_general(x['c_prev'].astype(BF16), x['qt'], (((0,), (0,)), ((), ())),
                                preferred_element_type=F32)
        num = x['w_inter'] * inter + jnp.dot(x['vt'], x['st'].astype(BF16), preferred_element_type=F32)
        qn = jnp.dot(jnp.broadcast_to(x['n_prev'], (SUBLANES, HEAD)).astype(BF16), x['qt'],
                     preferred_element_type=F32)[0:1, :]
        den = x['w_inter'] * qn + jnp.sum(x['st'], axis=0, keepdims=True)
        x['h_ref'][x['sl'], :] = num / jnp.maximum(jnp.abs(den), x['floor'])
    for x in hd:
        j = x['j']
        log_w = x['b_last'] + x['gate_row']
        m_new = jnp.maximum(x['b_last'] + x['m_prev'], jnp.max(log_w, axis=1, keepdims=True))
        decay = jnp.exp(x['b_last'] + x['m_prev'] - m_new)
        w_row = jnp.exp(log_w - m_new)
        kwt = (x['kt'].astype(F32) * w_row).astype(BF16)
        c_sc[j] = decay * x['c_prev'] + jnp.dot(kwt, x['v'], preferred_element_type=F32)
        n_sc[j:j + 1, :] = decay * x['n_prev'] + jnp.dot(
            jnp.broadcast_to(w_row, (SUBLANES, chunk)).astype(BF16), x['k'], preferred_element_type=F32)[0:1, :]
        m_sc[j:j + 1, :] = jnp.broadcast_to(m_new, (1, HEAD))

    write_state()


def _state_index(bm):
    if bm > 1:
        return lambda b, i: (b,) + (0,) * 3, lambda b, i: (b, 0, 0)
    return lambda b, i: (0,) * 4, lambda b, i: (0, 0, 0)


def _mlstm_plan(qt, k, kt, v, vt, g, gt, c0, n0, m0, b, t, chunk, emit_state):
    nc = t // chunk
    n = b * t
    nd = 2 * N_HEADS
    fwd = lambda bi, i: (bi * nc + i, 0)
    bwd = lambda bi, i: (bi * nc + nc - 1 - i, 0)
    fwd_t = lambda bi, i: (0, bi * nc + i)
    bwd_t = lambda bi, i: (0, bi * nc + nc - 1 - i)
    c_idx, n_idx = _state_index(c0.shape[0])
    rows = lambda m: pl.BlockSpec((chunk, D_ML), m)
    cols = lambda m: pl.BlockSpec((D_ML, chunk), m)
    one_dir = lambda m, mt: [cols(mt), rows(m), cols(mt), rows(m), cols(mt),
                             pl.BlockSpec((chunk, N_GATE), m), pl.BlockSpec((N_GATE, chunk), mt)]
    in_specs = (one_dir(fwd, fwd_t) + one_dir(bwd, bwd_t)
                + [pl.BlockSpec((1, nd, HEAD, HEAD), c_idx),
                   pl.BlockSpec((1, nd, HEAD), n_idx), pl.BlockSpec((1, nd, HEAD), n_idx)])
    out_specs = [cols(fwd_t), cols(bwd_t)]
    out_shape = [jax.ShapeDtypeStruct((D_ML, n), F32)] * 2
    if emit_state:
        out_specs += [pl.BlockSpec((1, nd, HEAD, HEAD), lambda bi, i: (bi, 0, 0, 0)),
                      pl.BlockSpec((1, nd, HEAD), lambda bi, i: (bi, 0, 0)),
                      pl.BlockSpec((1, nd, HEAD), lambda bi, i: (bi, 0, 0))]
        out_shape += [jax.ShapeDtypeStruct((b, nd, HEAD, HEAD), F32),
                      jax.ShapeDtypeStruct((b, nd, HEAD), F32),
                      jax.ShapeDtypeStruct((b, nd, HEAD), F32)]
    return dict(
        kernel=functools.partial(_mlstm_kernel, chunk, nc, emit_state), in_specs=in_specs, out_specs=out_specs,
        out_shape=out_shape,
        scratch=[pltpu.VMEM((nd, HEAD, HEAD), F32), pltpu.VMEM((nd, HEAD), F32), pltpu.VMEM((nd, HEAD), F32)],
        args=(qt, k, kt, v, vt, g, gt, qt, k, kt, v, vt, g, gt, c0, n0, m0))


def _neg_expm1_2x(x):
    t = jnp.tanh(x)
    return -2.0 * t / (1.0 - t)


def _rglru_kernel(tb, nb, emit_state, *refs, parts=ALL_PARTS):
    (xf, xf_prev, xf_next, xb, xb_prev, xb_next, h0_ref, w_ref, bias_ref, lam_ref, cw_ref, cb_ref) = refs[:12]
    n_out = 3 if emit_state else 2
    hf_ref, hb_ref = refs[12:14]
    hfin_ref = refs[14] if emit_state else None
    carry, af_sc, uf_sc, ab_sc, ub_sc, hf_sc, pf_sc, hb_sc, pb_sc = refs[12 + n_out:]
    i = pl.program_id(1)

    if "init" in parts:
        @pl.when(i == 0)
        def _():
            carry[...] = h0_ref[0]

    def write_state():
        if emit_state and "final" in parts:
            @pl.when(i == nb - 1)
            def _():
                hfin_ref[0] = carry[...]

    if "main" not in parts:
        write_state()
        return

    row8 = lax.broadcasted_iota(I32, (SUBLANES, D_RG), 0)
    cw = cw_ref[...]
    log_a_per_r = -RG_C * _softplus(-lam_ref[...])

    def taps(xm2, xm1, x0, xp1):
        return cb_ref[...] + xm2 * cw[0:1, :] + xm1 * cw[1:2, :] + x0 * cw[2:3, :] + xp1 * cw[3:4, :]

    def conv(main_ref, prev_ref, next_ref, first, last):
        main = main_ref[...]
        prev = jnp.where(first, 0.0, prev_ref[...])
        nxt = jnp.where(last, 0.0, next_ref[...])
        body = taps(pltpu.roll(main, 2, 0), pltpu.roll(main, 1, 0), main, pltpu.roll(main, tb - 1, 0))
        e = SUBLANES
        head, tail = main[0:e, :], main[tb - e:tb, :]
        before_tail = main[tb - 2 * e:tb - e, :]
        fix_head = taps(
            jnp.where(row8 == 0, prev[6:7, :], jnp.where(row8 == 1, prev[7:8, :], pltpu.roll(head, 2, 0))),
            jnp.where(row8 == 0, prev[7:8, :], pltpu.roll(head, 1, 0)), head,
            jnp.where(row8 == e - 1, main[e:e + 1, :], pltpu.roll(head, e - 1, 0)))
        fix_tail = taps(
            jnp.where(row8 == 0, before_tail[6:7, :],
                      jnp.where(row8 == 1, before_tail[7:8, :], pltpu.roll(tail, 2, 0))),
            jnp.where(row8 == 0, before_tail[7:8, :], pltpu.roll(tail, 1, 0)), tail,
            jnp.where(row8 == e - 1, nxt[0:1, :], pltpu.roll(tail, e - 1, 0)))
        return jnp.concatenate([fix_head, body[e:tb - e, :], fix_tail], axis=0)

    def recurrence_terms(xc, d):
        z = jnp.dot(xc.astype(BF16), w_ref[:, d * 2 * D_RG:(d + 1) * 2 * D_RG],
                    preferred_element_type=F32) + bias_ref[:, d * 2 * D_RG:(d + 1) * 2 * D_RG]
        r = 0.5 * jnp.tanh(0.5 * z[:, 0:D_RG]) + 0.5
        ig = 0.5 * jnp.tanh(0.5 * z[:, D_RG:2 * D_RG]) + 0.5
        log_a = r * log_a_per_r[d:d + 1, :]
        a = jnp.exp(log_a)
        u = jnp.sqrt(_neg_expm1_2x(log_a)) * (ig * xc)
        return a, u

    a_f, u_f = recurrence_terms(conv(xf, xf_prev, xf_next, i == 0, i == nb - 1), 0)
    a_b, u_b = recurrence_terms(conv(xb, xb_prev, xb_next, i == nb - 1, i == 0), 1)
    seg = tb // SUBLANES
    pitch = seg + RG_SEG_PAD
    ncol = D_RG // LANES
    for lc in range(ncol):
        lanes = slice(lc * LANES, (lc + 1) * LANES)
        for s in range(SUBLANES):
            src = slice(s * seg, (s + 1) * seg)
            dst = slice(s * pitch, s * pitch + seg)
            af_sc[lc, dst, :], uf_sc[lc, dst, :] = a_f[src, lanes], u_f[src, lanes]
            ab_sc[lc, dst, :], ub_sc[lc, dst, :] = a_b[src, lanes], u_b[src, lanes]
    slab = lambda k: (slice(None), pl.ds(k, SUBLANES, stride=pitch), slice(None))
    hf = jnp.zeros((ncol, SUBLANES, LANES), F32)
    hb = jnp.zeros((ncol, SUBLANES, LANES), F32)
    pf = jnp.ones((ncol, SUBLANES, LANES), F32)
    pb = jnp.ones((ncol, SUBLANES, LANES), F32)
    for k in range(seg):
        kb = seg - 1 - k
        ak = af_sc[slab(k)]
        hf = ak * hf + uf_sc[slab(k)]
        pf = pf * ak
        hf_sc[slab(k)] = hf
        pf_sc[slab(k)] = pf
        ak = ab_sc[slab(kb)]
        hb = ak * hb + ub_sc[slab(kb)]
        pb = pb * ak
        hb_sc[slab(kb)] = hb
        pb_sc[slab(kb)] = pb
    for lc in range(ncol):
        lanes = slice(lc * LANES, (lc + 1) * LANES)
        c = carry[0:1, lanes]
        cin_f = []
        for s in range(SUBLANES):
            cin_f.append(c)
            c = pf[lc, s:s + 1, :] * c + hf[lc, s:s + 1, :]
        carry[0:1, lanes] = c
        c = carry[1:2, lanes]
        cin_b = [None] * SUBLANES
        for s in reversed(range(SUBLANES)):
            cin_b[s] = c
            c = pb[lc, s:s + 1, :] * c + hb[lc, s:s + 1, :]
        carry[1:2, lanes] = c
        for s in range(SUBLANES):
            rows = slice(s * seg, (s + 1) * seg)
            src = slice(s * pitch, s * pitch + seg)
            hf_ref[rows, lanes] = hf_sc[lc, src, :] + pf_sc[lc, src, :] * cin_f[s]
            hb_ref[rows, lanes] = hb_sc[lc, src, :] + pb_sc[lc, src, :] * cin_b[s]

    write_state()


def _rglru_plan(xr, h0, wbd, bias, lam, cw, cb, b, t, tb, emit_state):
    nb = t // tb
    n = b * t
    r8 = tb // SUBLANES
    last8 = n // SUBLANES - 1
    fwd = lambda bi, i: (bi * nb + i, 0)
    bwd = lambda bi, i: (bi * nb + nb - 1 - i, 0)
    fwd_prev = lambda bi, i: (jnp.maximum((bi * nb + i) * r8 - 1, 0), 0)
    fwd_next = lambda bi, i: (jnp.minimum((bi * nb + i + 1) * r8, last8), 0)
    bwd_prev = lambda bi, i: (jnp.maximum((bi * nb + nb - 1 - i) * r8 - 1, 0), 0)
    bwd_next = lambda bi, i: (jnp.minimum((bi * nb + nb - i) * r8, last8), 0)
    const = lambda bi, i: (0, 0)
    h_idx = (lambda bi, i: (bi, 0, 0)) if h0.shape[0] > 1 else (lambda bi, i: (0, 0, 0))
    halo = lambda m: pl.BlockSpec((SUBLANES, D_RG), m)
    in_specs = [pl.BlockSpec((tb, D_RG), fwd), halo(fwd_prev), halo(fwd_next),
                pl.BlockSpec((tb, D_RG), bwd), halo(bwd_prev), halo(bwd_next),
                pl.BlockSpec((1, 2, D_RG), h_idx),
                pl.BlockSpec((D_RG, 4 * D_RG), const), pl.BlockSpec((1, 4 * D_RG), const),
                pl.BlockSpec((2, D_RG), const), pl.BlockSpec((4, D_RG), const), pl.BlockSpec((1, D_RG), const)]
    out_specs = [pl.BlockSpec((tb, D_RG), fwd), pl.BlockSpec((tb, D_RG), bwd)]
    out_shape = [jax.ShapeDtypeStruct((n, D_RG), F32)] * 2
    if emit_state:
        out_specs.append(pl.BlockSpec((1, 2, D_RG), lambda bi, i: (bi, 0, 0)))
        out_shape.append(jax.ShapeDtypeStruct((b, 2, D_RG), F32))
    return dict(
        kernel=functools.partial(_rglru_kernel, tb, nb, emit_state), in_specs=in_specs, out_specs=out_specs,
        out_shape=out_shape,
        scratch=[pltpu.VMEM((2, D_RG), F32)]
        + [pltpu.VMEM((D_RG // LANES, tb + SUBLANES * RG_SEG_PAD, LANES), F32)] * 8,
        args=(xr, xr, xr, xr, xr, xr, h0, wbd, bias, lam, cw, cb))


def _scans_kernel(ml, rg, *refs):
    def split(plan, ins, outs, scr):
        n_in, n_out, n_sc = len(plan['in_specs']), len(plan['out_specs']), len(plan['scratch'])
        return ins[:n_in], ins[n_in:], outs[:n_out], outs[n_out:], scr[:n_sc], scr[n_sc:]

    n_in = len(ml['in_specs']) + len(rg['in_specs'])
    n_out = len(ml['out_specs']) + len(rg['out_specs'])
    ml_in, ins, ml_out, outs, ml_sc, scr = split(ml, refs[:n_in], refs[n_in:n_in + n_out], refs[n_in + n_out:])
    rg_in, _, rg_out, _, rg_sc, _ = split(rg, ins, outs, scr)
    for parts in (("init",), ("main",), ("final",)):
        ml['kernel'](*ml_in, *ml_out, *ml_sc, parts=parts)
        rg['kernel'](*rg_in, *rg_out, *rg_sc, parts=parts)


def _scans(ml, rg, b, nsteps):
    static = lambda plan: {k: v for k, v in plan.items() if k != 'args'}
    outs = pl.pallas_call(
        functools.partial(_scans_kernel, static(ml), static(rg)),
        grid=(b, nsteps),
        in_specs=ml['in_specs'] + rg['in_specs'],
        out_specs=ml['out_specs'] + rg['out_specs'],
        out_shape=ml['out_shape'] + rg['out_shape'],
        scratch_shapes=ml['scratch'] + rg['scratch'],
        name="scans",
        compiler_params=_params("arbitrary", "arbitrary"),
    )(*ml['args'], *rg['args'])
    n_ml = len(ml['out_specs'])
    return outs[:n_ml], outs[n_ml:]


def _route(s, sb):
    tm = s.shape[1]
    neg = -jnp.inf
    sub = lax.broadcasted_iota(I32, (GROUP, tm), 0)
    blocks = [sb[gi * GROUP:(gi + 1) * GROUP, :] for gi in range(N_GROUPS)]
    gscore = []
    for blk in blocks:
        m1 = jnp.max(blk, axis=0, keepdims=True)
        first = jnp.min(jnp.where(blk == m1, sub, GROUP), axis=0, keepdims=True)
        m2 = jnp.max(jnp.where(sub == first, neg, blk), axis=0, keepdims=True)
        gscore.append(m1 + m2)
    masked = []
    for gi in range(N_GROUPS):
        rank = jnp.zeros((1, tm), F32)
        for gj in range(N_GROUPS):
            if gj == gi:
                continue
            ahead = (gscore[gj] >= gscore[gi]) if gj < gi else (gscore[gj] > gscore[gi])
            rank = rank + jnp.where(ahead, 1.0, 0.0)
        masked.append(jnp.where(rank < TOPK_GROUPS, blocks[gi], neg))
    v = jnp.concatenate(masked, axis=0)
    eid = lax.broadcasted_iota(I32, (N_EXPERTS, tm), 0)
    sel = jnp.zeros((N_EXPERTS, tm), F32)
    picks = []
    for _ in range(TOP_K):
        mx = jnp.max(v, axis=0, keepdims=True)
        idx = jnp.min(jnp.where(v == mx, eid, N_EXPERTS), axis=0, keepdims=True)
        pick = eid == idx
        picks.append(pick)
        sel = jnp.where(pick, 1.0, sel)
        v = jnp.where(pick, neg, v)
    ws = s * sel
    return ws * (ROUTED_SCALE / jnp.sum(ws, axis=0, keepdims=True)), sel, picks


def _mix_out_kernel(hmf_ref, hmb_ref, ot_ref, hrf_ref, hrb_ref, gr_ref, x_ref, rt_ref, ct_ref, mod_ref,
                    mln_ref, rgn_ref, wo_ml_ref, wo_rg_ref, n2_ref, rwt_ref, rb_ref, cnt0_ref,
                    x1_ref, hn2p_ref, ek_ref, pk_ref, wtok_ref, cnt_ref, cnt_sc, earlier_sc):
    i = pl.program_id(0)
    tm = x_ref.shape[0]

    @pl.when(i == 0)
    def _():
        cnt_sc[...] = cnt0_ref[...].astype(F32)
        earlier = (lax.broadcasted_iota(I32, (tm, tm), 0) < lax.broadcasted_iota(I32, (tm, tm), 1))
        earlier_sc[...] = earlier.astype(BF16)

    hm = hmf_ref[...] + hmb_ref[...]
    heads = []
    for h in range(N_HEADS):
        seg = hm[h * HEAD:(h + 1) * HEAD, :]
        heads.append(seg * lax.rsqrt(jnp.mean(seg * seg, axis=0, keepdims=True) + EPS))
    y_ml_t = jnp.concatenate(heads, axis=0) * mln_ref[...] * jax.nn.sigmoid(ot_ref[...])
    y_rg = _rms(hrf_ref[...] + hrb_ref[...], rgn_ref[...]) * jax.nn.gelu(gr_ref[...])
    mix = (lax.dot_general(y_ml_t.astype(BF16), wo_ml_ref[...], (((0,), (0,)), ((), ())),
                           preferred_element_type=F32)
           + jnp.dot(y_rg.astype(BF16), wo_rg_ref[...], preferred_element_type=F32))
    x1 = x_ref[...] + _pos_tile(rt_ref, ct_ref) + mod_ref[0, 2:3, :] * mix
    x1_ref[...] = x1
    hn2 = _rms(x1, n2_ref[...]) * (1.0 + mod_ref[0, 4:5, :]) + mod_ref[0, 3:4, :]
    hn2p_ref[...] = _pack_bf16_pairs(hn2)
    hb = hn2.astype(BF16)
    h_lo = (hn2 - hb.astype(F32)).astype(BF16)
    nt = (((1,), (1,)), ((), ()))
    two = lax.dot_general(rwt_ref[...], hb, nt, preferred_element_type=F32)
    logits_t = (two[0:N_EXPERTS, :] + two[N_EXPERTS:2 * N_EXPERTS, :]
                + lax.dot_general(rwt_ref[0:N_EXPERTS, :], h_lo, nt, preferred_element_type=F32))
    s = jax.nn.sigmoid(logits_t)
    wt, sel, picks = _route(s, s + rb_ref[...])

    prefix = jnp.dot(sel.astype(BF16), earlier_sc[...], preferred_element_type=F32)
    pos_all = cnt_sc[:, 0:1] + prefix
    eid = lax.broadcasted_iota(I32, (N_EXPERTS, tm), 0)
    eid_f = eid.astype(F32)
    row8 = lax.broadcasted_iota(I32, (TOP_K, tm), 0)
    ek = jnp.zeros((TOP_K, tm), F32)
    pk = jnp.zeros((TOP_K, tm), F32)
    wk = jnp.zeros((N_EXPERTS, tm), F32)
    for k, pick in enumerate(picks):
        take = lambda a: jnp.sum(jnp.where(pick, a, 0.0), axis=0, keepdims=True)
        ek = jnp.where(row8 == k, take(eid_f), ek)
        pk = jnp.where(row8 == k, take(pos_all), pk)
        wk = jnp.where(eid == k, take(wt), wk)
    ek_ref[...] = ek.astype(I32)
    pk_ref[...] = pk.astype(I32)
    wtok_ref[...] = wk.T
    cnt_sc[...] += jnp.broadcast_to(jnp.sum(sel, axis=1, keepdims=True), cnt_sc.shape)

    @pl.when(i == pl.num_programs(0) - 1)
    def _():
        cnt_ref[...] = cnt_sc[...].astype(I32)


def _mix_out(hmf, hmb, ot, hrf, hrb, gr, x2d, rtab, ctab, mod, t, tm, mln, rgn, wo_ml, wo_rg, norm2, rwt, rbias,
             cnt0):
    n = x2d.shape[0]
    tok = lambda i: (i, 0)
    tok_t = lambda i: (0, i)
    const = lambda i: (0, 0)
    return pl.pallas_call(
        _mix_out_kernel,
        grid=(n // tm,),
        in_specs=[pl.BlockSpec((D_ML, tm), tok_t)] * 3 + [pl.BlockSpec((tm, D_RG), tok)] * 3 + [
            pl.BlockSpec((tm, D_MODEL), tok)] + _pos_specs(rtab, ctab, tm) + [
            pl.BlockSpec((1, N_MOD, D_MODEL), _mod_index(mod.shape[0], tm, t)),
            pl.BlockSpec((D_ML, 1), const), pl.BlockSpec((1, D_RG), const),
            pl.BlockSpec((D_ML, D_MODEL), const), pl.BlockSpec((D_RG, D_MODEL), const),
            pl.BlockSpec((1, D_MODEL), const),
            pl.BlockSpec((2 * N_EXPERTS, D_MODEL), const), pl.BlockSpec((N_EXPERTS, 1), const),
            pl.BlockSpec((N_EXPERTS, LANES), const),
        ],
        out_specs=[pl.BlockSpec((tm, D_MODEL), tok), pl.BlockSpec((tm, D_PACK), tok),
                   pl.BlockSpec((TOP_K, tm), tok_t), pl.BlockSpec((TOP_K, tm), tok_t),
                   pl.BlockSpec((tm, N_EXPERTS), tok), pl.BlockSpec((N_EXPERTS, LANES), const)],
        out_shape=[jax.ShapeDtypeStruct((n, D_MODEL), F32), jax.ShapeDtypeStruct((n, D_PACK), I32),
                   jax.ShapeDtypeStruct((TOP_K, n), I32), jax.ShapeDtypeStruct((TOP_K, n), I32),
                   jax.ShapeDtypeStruct((n, N_EXPERTS), F32), jax.ShapeDtypeStruct((N_EXPERTS, LANES), I32)],
        scratch_shapes=[pltpu.VMEM((N_EXPERTS, LANES), F32), pltpu.VMEM((tm, tm), BF16)],
        name="mix_out",
        compiler_params=_params("arbitrary"),
    )(hmf, hmb, ot, hrf, hrb, gr, x2d, rtab, ctab, mod, mln, rgn, wo_ml, wo_rg, norm2, rwt, rbias, cnt0)


def _sc_mesh():
    return plsc.VectorSubcoreMesh(core_axis_name="core", subcore_axis_name="subcore")


def _sc_worker():
    info = plsc.get_sparse_core_info()
    return lax.axis_index("subcore") * info.num_cores + lax.axis_index("core"), info.num_cores * info.num_subcores


def _sc_dispatch(xp, dest3, n_slots):
    n, w = xp.shape
    nwin = n // SC_WINDOW

    @pl.kernel(out_type=jax.ShapeDtypeStruct((n_slots, w), xp.dtype), mesh=_sc_mesh(),
               scratch_types=[pltpu.VMEM((SC_WINDOW, w), xp.dtype), pltpu.VMEM((TOP_K, SC_WINDOW), I32)],
               name="sc_dispatch")
    def k(x_hbm, i_hbm, o_hbm, x_v, i_v):
        wid, nworkers = _sc_worker()
        per = nwin // nworkers

        @pl.loop(0, per)
        def _(s):
            win = wid * per + s
            pltpu.sync_copy(x_hbm.at[pl.ds(win * SC_WINDOW, SC_WINDOW)], x_v)
            pltpu.sync_copy(i_hbm.at[win], i_v)
            for j in range(TOP_K):
                pltpu.sync_copy(x_v, o_hbm.at[i_v.at[j]])

    return k(xp, dest3)


def _sc_pack_rows(w):
    r, c = w.shape
    lanes = plsc.get_sparse_core_info().num_lanes
    chunk = SC_PACK_WORDS // c
    nchunks = r // chunk

    def rne_high(x):
        u = plsc.bitcast(x, I32)
        u = u + 0x7FFF + (lax.shift_right_logical(u, jnp.full(u.shape, 16, I32)) & 1)
        return u & jnp.int32(-65536)

    @pl.kernel(out_type=jax.ShapeDtypeStruct((r // 2, c), I32), mesh=_sc_mesh(),
               scratch_types=[pltpu.VMEM((chunk, c), F32), pltpu.VMEM((chunk // 2, c), I32)],
               compiler_params=pltpu.CompilerParams(needs_layout_passes=False), name="sc_pack")
    def k(w_hbm, o_hbm, in_v, out_v):
        wid, nworkers = _sc_worker()
        per = nchunks // nworkers

        @pl.loop(0, per)
        def _(s):
            first = (wid * per + s) * chunk
            pltpu.sync_copy(w_hbm.at[pl.ds(first, chunk)], in_v)

            @pl.loop(0, chunk // 2)
            def _(row):
                for j in range(c // lanes):
                    at = lambda r_: (r_, pl.ds(j * lanes, lanes))
                    lo = rne_high(in_v.at[*at(2 * row)][...])
                    hi = rne_high(in_v.at[*at(2 * row + 1)][...])
                    out_v.at[*at(row)][...] = hi | lax.shift_right_logical(lo, jnp.full(lo.shape, 16, I32))

            pltpu.sync_copy(out_v, o_hbm.at[pl.ds((wid * per + s) * (chunk // 2), chunk // 2)])

    return k(w)


def _sc_combine_gather(ys, dest3):
    nwin = dest3.shape[0]
    w = ys.shape[1]

    @pl.kernel(out_type=jax.ShapeDtypeStruct((nwin, TOP_K, SC_WINDOW, w), ys.dtype), mesh=_sc_mesh(),
               scratch_types=[pltpu.VMEM((SC_WINDOW, w), ys.dtype), pltpu.VMEM((TOP_K, SC_WINDOW), I32)],
               name="sc_combine")
    def k(y_hbm, i_hbm, o_hbm, y_v, i_v):
        wid, nworkers = _sc_worker()
        per = nwin // nworkers

        @pl.loop(0, per)
        def _(s):
            win = wid * per + s
            pltpu.sync_copy(i_hbm.at[win], i_v)
            for j in range(TOP_K):
                pltpu.sync_copy(y_hbm.at[i_v.at[j]], y_v)
                pltpu.sync_copy(y_v, o_hbm.at[win, j])

    return k(ys, dest3)


def _swiglu(x, w13):
    h = jnp.dot(x, w13, preferred_element_type=F32)
    return _silu(h[:, 0:D_EXPERT]) * h[:, D_EXPERT:2 * D_EXPERT]


def _unpack_rows_bf16(p):
    hi, lo = _unpack_bf16_pairs(p)
    return jnp.concatenate([hi.astype(BF16), lo.astype(BF16)], axis=1)


def _expert_kernel(rows, g, rank_ref, expert_ref, nu_ref, nr_ref, x_ref, w1_hbm, w3_hbm, w2_hbm, y_ref,
                   w1buf, w3buf, w2buf, sem, progress):
    b = pl.program_id(0)
    ring = w1buf.shape[0]
    n_ranks = nr_ref[0]

    @pl.when(b == 0)
    def _():
        progress[0] = 0
        progress[1] = 0

    def copies(r):
        slot = lax.rem(r, ring)
        e = expert_ref[r]
        return (pltpu.make_async_copy(w1_hbm.at[e], w1buf.at[slot], sem.at[slot, 0]),
                pltpu.make_async_copy(w3_hbm.at[e], w3buf.at[slot], sem.at[slot, 1]),
                pltpu.make_async_copy(w2_hbm.at[e], w2buf.at[slot], sem.at[slot, 2]))

    def start(r, carry):
        for cp in copies(r):
            cp.start()
        return carry

    def wait(r, carry):
        for cp in copies(r):
            cp.wait()
        return carry

    first = rank_ref[b * g]
    last = rank_ref[b * g + g - 1]
    started = jnp.minimum(first + ring, n_ranks)
    lax.fori_loop(progress[0], started, start, 0)
    progress[0] = jnp.maximum(progress[0], started)
    needed = jnp.where(b == pl.num_programs(0) - 1, progress[0], jnp.minimum(last + 1, n_ranks))
    lax.fori_loop(progress[1], needed, wait, 0)
    progress[1] = jnp.maximum(progress[1], needed)

    @pl.when(b * g < nu_ref[0])
    def _():
        for j in range(g):
            slot = lax.rem(rank_ref[b * g + j], ring)
            sl = slice(j * rows, (j + 1) * rows)
            x = _unpack_rows_bf16(x_ref[sl, :])
            w = lambda buf: pltpu.bitcast(buf[slot], BF16)
            h = (_silu(jnp.dot(x, w(w1buf), preferred_element_type=F32))
                 * jnp.dot(x, w(w3buf), preferred_element_type=F32))
            y_ref[sl, :] = _pack_bf16_pairs(jnp.dot(h.astype(BF16), w(w2buf), preferred_element_type=F32))


def _experts(xs, block_rank, rank_expert, n_used, n_ranks, w1, w3, w2, rows):
    g = min(EXPERT_STEP_ROWS // rows, EXPERT_STEP_BLOCKS)
    ring = g + 2
    nb = xs.shape[0] // rows
    assert nb % g == 0
    tok = lambda b, *_: (b, 0)
    return pl.pallas_call(
        functools.partial(_expert_kernel, rows, g),
        grid_spec=pltpu.PrefetchScalarGridSpec(
            num_scalar_prefetch=4,
            grid=(nb // g,),
            in_specs=[pl.BlockSpec((g * rows, D_PACK), tok)] + [pl.BlockSpec(memory_space=pl.ANY)] * 3,
            out_specs=pl.BlockSpec((g * rows, D_PACK), tok),
            scratch_shapes=[pltpu.VMEM((ring, D_MODEL // 2, D_EXPERT), I32),
                            pltpu.VMEM((ring, D_MODEL // 2, D_EXPERT), I32),
                            pltpu.VMEM((ring, D_EXPERT // 2, D_MODEL), I32),
                            pltpu.SemaphoreType.DMA((ring, 3)), pltpu.SMEM((2,), I32)],
        ),
        out_shape=jax.ShapeDtypeStruct(xs.shape, I32),
        name="experts",
        compiler_params=_params("arbitrary", vmem=VMEM_LIMIT_EXPERTS),
    )(block_rank, rank_expert, n_used, n_ranks, xs, w1, w3, w2)


def _moe_out_kernel(win0, yk_hbm, hn2p_ref, wtok_ref, sw13_ref, sw2_ref, x1_ref, mod_ref, nf_ref, y_ref,
                    ykbuf, sem):
    i = pl.program_id(0)
    nsteps = pl.num_programs(0)
    nwin = ykbuf.shape[1]

    def fetch(step):
        slot = lax.rem(step, MOE_OUT_BUFFERS)
        return pltpu.make_async_copy(yk_hbm.at[pl.ds(win0 + step * nwin, nwin)], ykbuf.at[slot], sem.at[slot])

    @pl.when(i == 0)
    def _():
        for step in range(MOE_OUT_BUFFERS - 1):
            fetch(step).start()

    @pl.when(i + MOE_OUT_BUFFERS - 1 < nsteps)
    def _():
        fetch(i + MOE_OUT_BUFFERS - 1).start()

    fetch(i).wait()

    def body(slot):
        shared = jnp.dot(_swiglu(_unpack_rows_bf16(hn2p_ref[...]), sw13_ref[...]).astype(BF16), sw2_ref[...],
                         preferred_element_type=F32)
        w = wtok_ref[...]
        parts = []
        for wi in range(nwin):
            rows = slice(wi * SC_WINDOW, (wi + 1) * SC_WINDOW)
            a_hi = shared[rows, 0:D_PACK]
            a_lo = shared[rows, D_PACK:D_MODEL]
            for k in range(TOP_K):
                y_hi, y_lo = _unpack_bf16_pairs(ykbuf[slot, wi, k])
                wc = w[rows, k:k + 1]
                a_hi = a_hi + wc * y_hi
                a_lo = a_lo + wc * y_lo
            parts.append(jnp.concatenate([a_hi, a_lo], axis=1))
        x2 = x1_ref[...] + mod_ref[0, 5:6, :] * jnp.concatenate(parts, axis=0)
        y_ref[...] = _rms(x2, nf_ref[...])

    for s in range(MOE_OUT_BUFFERS):
        pl.when(lax.rem(i, MOE_OUT_BUFFERS) == s)(functools.partial(body, s))


def _moe_out(yk, tok0, hn2p, wtok, sw13, sw2, x1, mod, t, tm, norm_final):
    n = hn2p.shape[0]
    tok = lambda i: (i, 0)
    const = lambda i: (0, 0)
    nwin = tm // SC_WINDOW
    assert n // tm >= MOE_OUT_BUFFERS - 1
    return pl.pallas_call(
        functools.partial(_moe_out_kernel, tok0 // SC_WINDOW),
        grid=(n // tm,),
        in_specs=[
            pl.BlockSpec(memory_space=pl.ANY),
            pl.BlockSpec((tm, D_PACK), tok),
            pl.BlockSpec((tm, N_EXPERTS), tok),
            pl.BlockSpec((D_MODEL, 2 * D_EXPERT), const),
            pl.BlockSpec((D_EXPERT, D_MODEL), const),
            pl.BlockSpec((tm, D_MODEL), tok),
            pl.BlockSpec((1, N_MOD, D_MODEL), _mod_index(mod.shape[0], tm, t)),
            pl.BlockSpec((1, D_MODEL), const),
        ],
        out_specs=pl.BlockSpec((tm, D_MODEL), tok),
        out_shape=jax.ShapeDtypeStruct((n, D_MODEL), F32),
        scratch_shapes=[pltpu.VMEM((MOE_OUT_BUFFERS, nwin, TOP_K, SC_WINDOW, D_PACK), I32),
                        pltpu.SemaphoreType.DMA((MOE_OUT_BUFFERS,))],
        name="moe_out",
        compiler_params=_params("arbitrary"),
    )(yk, hn2p, wtok, sw13, sw2, x1, mod, norm_final)


def _dispatch_plan(cnt, ek, pk, rows):
    n = ek.shape[1]
    nb = n * TOP_K // rows + N_EXPERTS
    nblk = (cnt + rows - 1) // rows
    block_end = jnp.cumsum(nblk)
    experts = jnp.arange(N_EXPERTS, dtype=I32)
    first_row = jnp.sum(jnp.where(ek[:, :, None] == experts, (block_end - nblk) * rows, 0), axis=-1)
    dest3 = (first_row + pk).reshape(TOP_K, n // SC_WINDOW, SC_WINDOW).transpose(1, 0, 2)
    owns = nblk > 0
    n_ranks = jnp.sum(owns.astype(I32))
    blocks = jnp.arange(nb, dtype=I32)[:, None]
    block_rank = jnp.minimum(jnp.sum((owns & (block_end <= blocks)).astype(I32), axis=1), n_ranks - 1)
    rank_expert = jnp.minimum(jnp.sum((jnp.cumsum(owns.astype(I32)) <= experts[:, None]).astype(I32), axis=1),
                              N_EXPERTS - 1)
    return (dest3, block_rank.astype(I32), rank_expert.astype(I32), block_end[-1:].astype(I32),
            n_ranks.reshape(1).astype(I32), nb * rows)


def _grid_pos_tables(n_tokens, dim):
    quarter = dim // 4
    omega = 1.0 / (POS_BASE ** (jnp.arange(quarter, dtype=F32) / quarter))
    ra = jnp.arange(n_tokens // GRID_W).astype(F32)[:, None] * omega
    ca = jnp.arange(GRID_W).astype(F32)[:, None] * omega
    return (jnp.concatenate([jnp.sin(ra), jnp.cos(ra)], axis=-1),
            jnp.concatenate([jnp.sin(ca), jnp.cos(ca)], axis=-1))


def _hi_lo_rows(w):
    hi = w.astype(BF16)
    return jnp.concatenate([hi, (w - hi.astype(F32)).astype(BF16)], axis=0)


def _block_diag(w):
    eye = jnp.eye(N_RG_BLOCKS, dtype=w.dtype)
    return jnp.einsum('nij,nm->nimj', w, eye).reshape(D_RG, D_RG)


def _layer_weights(l, norm1, w_in, mlstm_gate_bias, mlstm_norm, rg_conv_w, rg_conv_b, rg_wa, rg_ba, rg_wx,
                   rg_bx, rg_lambda, rg_norm, w_out, norm2, router_w, router_bias, exp_w1, exp_w3, exp_w2,
                   shared_w1, shared_w3, shared_w2):
    wi = w_in[l]
    c0, c1 = 4 * D_ML, 4 * D_ML + N_GATE
    wi16 = wi.astype(BF16)
    wg = wi[:, c0:c1]
    wg_hi = wi16[:, c0:c1]
    wg_lo = (wg - wg_hi.astype(F32)).astype(BF16)
    zcols = lambda w: jnp.zeros((D_MODEL, w), BF16)
    return dict(
        norm1=norm1[l].reshape(1, D_MODEL),
        wq=wi16[:, :c0],
        wr=jnp.concatenate([wi16[:, c1:], wg_hi, wg_lo, zcols(LANES - 2 * N_GATE)], axis=1),
        wgh=jnp.concatenate([wg_hi, zcols(LANES - N_GATE)], axis=1),
        gbias=jnp.pad(mlstm_gate_bias[l].reshape(1, N_GATE), ((0, 0), (0, LANES - N_GATE))),
        mln=mlstm_norm[l].reshape(D_ML, 1),
        cw=rg_conv_w[l], cb=rg_conv_b[l].reshape(1, D_RG),
        wbd=jnp.concatenate([_block_diag(rg_wa[l, 0]), _block_diag(rg_wx[l, 0]),
                             _block_diag(rg_wa[l, 1]), _block_diag(rg_wx[l, 1])], axis=1).astype(BF16),
        rbias=jnp.concatenate([rg_ba[l, 0], rg_bx[l, 0], rg_ba[l, 1], rg_bx[l, 1]]).reshape(1, 4 * D_RG),
        lam=rg_lambda[l], rgn=rg_norm[l].reshape(1, D_RG),
        wo_ml=w_out[l, :D_ML].astype(BF16), wo_rg=w_out[l, D_ML:].astype(BF16),
        norm2=norm2[l].reshape(1, D_MODEL),
        rwt=_hi_lo_rows(router_w[l].T), rb=router_bias[l].reshape(N_EXPERTS, 1),
        w1=_sc_pack_rows(exp_w1[l].reshape(N_EXPERTS * D_MODEL, D_EXPERT)).reshape(N_EXPERTS, D_MODEL // 2, -1),
        w3=_sc_pack_rows(exp_w3[l].reshape(N_EXPERTS * D_MODEL, D_EXPERT)).reshape(N_EXPERTS, D_MODEL // 2, -1),
        w2=_sc_pack_rows(exp_w2[l].reshape(N_EXPERTS * D_EXPERT, D_MODEL)).reshape(N_EXPERTS, D_EXPERT // 2, -1),
        sw13=jnp.concatenate([shared_w1[l], shared_w3[l]], axis=-1).astype(BF16),
        sw2=shared_w2[l].astype(BF16),
    )


def _mixers(x2d, pos_tables, mod, c0, n0, m0, h0, cnt0, lw, b, t, emit_state):
    tl = _tiles(t, mod.shape[0] > 1)
    tm = tl['tok']
    rtab, ctab = pos_tables
    qt, k, kt, v, vt, ot, xr, gr, g, gt = _in_proj(x2d, rtab, ctab, mod, t, tl['proj'], lw['norm1'], lw['wq'], lw['wr'],
                                                   lw['wgh'], lw['gbias'])
    chunk = tl['chunk']
    ml, rg = _scans(
        _mlstm_plan(qt, k, kt, v, vt, g, gt, c0, n0, m0, b, t, chunk, emit_state),
        _rglru_plan(xr, h0, lw['wbd'], lw['rbias'], lw['lam'], lw['cw'], lw['cb'], b, t, chunk, emit_state),
        b, t // chunk)
    routed = _mix_out(ml[0], ml[1], ot, rg[0], rg[1], gr, x2d, rtab, ctab, mod, t, tm, lw['mln'], lw['rgn'],
                      lw['wo_ml'], lw['wo_rg'], lw['norm2'], lw['rwt'], lw['rb'], cnt0)
    return routed, ml[2:], rg[2:]


def _routed_experts(paths, lw):
    cnt = paths[-1][5][:, 0]
    hn2p = jnp.concatenate([p[1] for p in paths], axis=0)
    ek = jnp.concatenate([p[2] for p in paths], axis=1)
    pk = jnp.concatenate([p[3] for p in paths], axis=1)
    rows = EXPERT_ROWS if ek.shape[1] * TOP_K >= 4 * EXPERT_ROWS * N_EXPERTS else EXPERT_ROWS // 4
    dest3, block_rank, rank_expert, n_used, n_ranks, n_slots = _dispatch_plan(cnt, ek, pk, rows)
    xs = _sc_dispatch(hn2p, dest3, n_slots)
    ys = _experts(xs, block_rank, rank_expert, n_used, n_ranks, lw['w1'], lw['w3'], lw['w2'], rows)
    return _sc_combine_gather(ys, dest3)


def kernel(x_prompt, x_sample, c, state_mlstm_C, state_mlstm_n, state_mlstm_m, state_rglru_h, c_ctx, w_ada, b_ada, norm1, w_in, mlstm_gate_bias, mlstm_norm, rg_conv_w, rg_conv_b, rg_wa, rg_ba, rg_wx, rg_bx, rg_lambda, rg_norm, w_out, norm2, router_w, router_bias, exp_w1, exp_w3, exp_w2, shared_w1, shared_w3, shared_w2, norm_final):
    bp, tp, _ = x_prompt.shape
    bs, ts, _ = x_sample.shape
    depth = w_ada.shape[0]
    assert depth == 1, "the final norm is fused into the single layer's MoE output kernel"
    nd = 2 * N_HEADS
    l = 0
    lw = _layer_weights(l, norm1, w_in, mlstm_gate_bias, mlstm_norm, rg_conv_w, rg_conv_b, rg_wa, rg_ba, rg_wx,
                        rg_bx, rg_lambda, rg_norm, w_out, norm2, router_w, router_bias, exp_w1, exp_w3, exp_w2,
                        shared_w1, shared_w3, shared_w2)
    nf = norm_final.reshape(1, D_MODEL)
    cvecs = jnp.concatenate([c_ctx[None], c, jnp.zeros((SUBLANES - 1 - bs, D_MODEL), F32)], axis=0)
    mod = _ada(cvecs, w_ada[l], b_ada[l]).reshape(SUBLANES, N_MOD, D_MODEL)

    xp2d, lw['w1'], lw['w3'], lw['w2'] = lax.optimization_barrier(
        (x_prompt.reshape(bp * tp, D_MODEL), lw['w1'], lw['w3'], lw['w2']))

    mod_p, mod_s = mod[0:1], mod[1:1 + bs]
    tm_p, tm_s = _tiles(tp, False)['tok'], _tiles(ts, True)['tok']
    rp, (cc, nc_, mc), (hc,) = _mixers(
        xp2d,
        (jnp.zeros((tm_p // GRID_W, D_MODEL // 2), F32), jnp.zeros((GRID_W, D_MODEL // 2), F32)), mod_p,
        jnp.zeros((1, nd, HEAD, HEAD), F32), jnp.zeros((1, nd, HEAD), F32), jnp.zeros((1, nd, HEAD), F32),
        jnp.zeros((1, 2, D_RG), F32), jnp.zeros((N_EXPERTS, LANES), I32), lw, bp, tp, True)
    rs, _, _ = _mixers(
        x_sample.reshape(bs * ts, D_MODEL), _grid_pos_tables(ts, D_MODEL), mod_s,
        state_mlstm_C[:, l].reshape(bs, nd, HEAD, HEAD), state_mlstm_n[:, l].reshape(bs, nd, HEAD),
        jnp.broadcast_to(state_mlstm_m[:, l].reshape(bs, nd, 1), (bs, nd, HEAD)),
        state_rglru_h[:, l], jnp.zeros((N_EXPERTS, LANES), I32), lw, bs, ts, False)
    yp = _moe_out(_routed_experts([rp], lw), 0, rp[1], rp[4], lw['sw13'], lw['sw2'], rp[0], mod_p, tp, tm_p, nf)
    ys = _moe_out(_routed_experts([rs], lw), 0, rs[1], rs[4], lw['sw13'], lw['sw2'], rs[0], mod_s, ts, tm_s, nf)

    y_prompt = yp.reshape(bp, tp, D_MODEL)
    y_sample = ys.reshape(bs, ts, D_MODEL)
    new_c = cc.reshape(bp, 1, 2, N_HEADS, HEAD, HEAD)
    new_n = nc_.reshape(bp, 1, 2, N_HEADS, HEAD)
    new_m = mc[:, :, 0].reshape(bp, 1, 2, N_HEADS)
    new_h = hc.reshape(bp, 1, 2, D_RG)
    return (y_prompt, y_sample, new_c, new_n, new_m, new_h)
```
